```python
import jax, jax.numpy as jnp
from jax import lax
import numpy as np

D_MODEL = 2048
BATCH = 8
SEQ = 8192
DEPTH = 4

N_Q_HEADS = 16
N_KV_HEADS = 4
HEAD_DIM = 64
Q_PER_KV = N_Q_HEADS // N_KV_HEADS
ATTN_WIDTH = N_Q_HEADS * HEAD_DIM
KV_WIDTH = N_KV_HEADS * HEAD_DIM
WINDOW = 128
ROPE_THETA = 500000.0
ROPE_DIM = HEAD_DIM // 4
SGU_WIDTH = D_MODEL // 2
SGU_GROUPS = 8
SGU_GROUP_DIM = SGU_WIDTH // SGU_GROUPS
CHUNK = 128
D_FF = -(-8 * D_MODEL // (3 * 256)) * 256
OFF_Q = 0
OFF_K = OFF_Q + ATTN_WIDTH
OFF_V = OFF_K + KV_WIDTH
OFF_Z = OFF_V + KV_WIDTH
OFF_G = OFF_Z + 2 * SGU_WIDTH
IN_WIDTH = OFF_G + 2 * D_MODEL
EPS = 1e-5
NEG = -1e30

kernel_name = "hybrid_swa_sink_gmlp_gated_block"


def rmsnorm(x, g):
    xf = x.astype(jnp.float32)
    y = xf * lax.rsqrt(jnp.mean(xf * xf, axis=-1, keepdims=True) + EPS) * g.astype(jnp.float32)
    return y.astype(x.dtype)


def partial_rope(t, cos, sin):
    half = ROPE_DIM // 2
    t1 = t[..., :half].astype(jnp.float32)
    t2 = t[..., half:ROPE_DIM].astype(jnp.float32)
    rot = jnp.concatenate([t1 * cos - t2 * sin, t2 * cos + t1 * sin], axis=-1).astype(t.dtype)
    return jnp.concatenate([rot, t[..., ROPE_DIM:]], axis=-1)


def sliding_window_attention(q, k, v, sinks):
    B, S = q.shape[0], q.shape[1]
    nb = S // WINDOW
    qb = q.reshape(B, nb, WINDOW, N_KV_HEADS, Q_PER_KV, HEAD_DIM)

    def band(t):
        tb = t.reshape(B, nb, WINDOW, N_KV_HEADS, HEAD_DIM)
        prev = jnp.pad(tb, ((0, 0), (1, 0), (0, 0), (0, 0), (0, 0)))[:, :-1]
        return jnp.concatenate([prev, tb], axis=2)

    kb, vb = band(k), band(v)
    scores = jnp.einsum('bnqhgd,bnkhd->bnhgqk', qb, kb).astype(jnp.float32) * (HEAD_DIM ** -0.5)
    qi = jnp.arange(WINDOW)[:, None]
    kj = jnp.arange(2 * WINDOW)[None, :]
    rel = qi + WINDOW - kj
    band_ok = (rel >= 0) & (rel < WINDOW)
    blk_ok = (jnp.arange(nb)[:, None, None] > 0) | (kj >= WINDOW)[None]
    mask = band_ok[None] & blk_ok
    scores = jnp.where(mask[None, :, None, None], scores, NEG)
    sink = sinks.astype(jnp.float32).reshape(N_KV_HEADS, Q_PER_KV)[None, None, :, :, None, None]
    sink = jnp.broadcast_to(sink, scores.shape[:-1] + (1,))
    probs = jax.nn.softmax(jnp.concatenate([scores, sink], axis=-1), axis=-1)[..., :-1]
    out = jnp.einsum('bnhgqk,bnkhd->bnqhgd', probs.astype(v.dtype), vb)
    return out.reshape(B, S, ATTN_WIDTH)


def spatial_gating(z, ln_g, ln_b, w_s, b_s):
    B, S = z.shape[0], z.shape[1]
    u, v = jnp.split(z, 2, axis=-1)
    vf = v.astype(jnp.float32)
    mu = jnp.mean(vf, axis=-1, keepdims=True)
    var = jnp.mean(jnp.square(vf - mu), axis=-1, keepdims=True)
    vn = ((vf - mu) * lax.rsqrt(var + EPS) * ln_g.astype(jnp.float32) + ln_b.astype(jnp.float32)).astype(z.dtype)
    vc = vn.reshape(B, S // CHUNK, CHUNK, SGU_GROUPS, SGU_GROUP_DIM)
    causal = jnp.tril(jnp.ones((CHUNK, CHUNK), dtype=bool))
    w = jnp.where(causal[None], w_s, jnp.zeros_like(w_s))
    sv = jnp.einsum('gts,bnsgc->bntgc', w, vc) + b_s.T[None, None, :, :, None]
    return u * sv.reshape(B, S, SGU_WIDTH)


def _fwd_setup_inputs(seed: int = 0) -> dict:
    key = jax.random.key(seed)
    ks = jax.random.split(key, 20)
    f32 = jnp.float32
    nrm = lambda k, shape, scale: jax.random.normal(k, shape, f32) * scale
    x = jax.random.normal(ks[0], (BATCH, SEQ, D_MODEL), f32)
    offset = jax.random.randint(ks[1], (BATCH, 1), 0, 4096, dtype=jnp.int32)
    positions = (jnp.arange(SEQ, dtype=jnp.int32)[None, :] + offset).astype(jnp.int32)
    return {
        "x": x,
        "positions": positions,
        "norm1_g": 1.0 + nrm(ks[2], (DEPTH, D_MODEL), 0.02),
        "w_in": nrm(ks[3], (DEPTH, D_MODEL, IN_WIDTH), D_MODEL ** -0.5),
        "b_in": nrm(ks[4], (DEPTH, IN_WIDTH), 0.02),
        "sinks": nrm(ks[5], (DEPTH, N_Q_HEADS), 1.0),
        "sgu_ln_g": 1.0 + nrm(ks[6], (DEPTH, SGU_WIDTH), 0.02),
        "sgu_ln_b": nrm(ks[7], (DEPTH, SGU_WIDTH), 0.02),
        "sgu_w": nrm(ks[8], (DEPTH, SGU_GROUPS, CHUNK, CHUNK), CHUNK ** -0.5),
        "sgu_b": 1.0 + nrm(ks[9], (DEPTH, SGU_GROUPS, CHUNK), 0.02),
        "w_attn_branch": nrm(ks[10], (DEPTH, ATTN_WIDTH, D_MODEL), ATTN_WIDTH ** -0.5),
        "w_sgu_branch": nrm(ks[11], (DEPTH, SGU_WIDTH, D_MODEL), SGU_WIDTH ** -0.5),
        "w_out": nrm(ks[12], (DEPTH, D_MODEL, D_MODEL), D_MODEL ** -0.5),
        "norm2_g": 1.0 + nrm(ks[13], (DEPTH, D_MODEL), 0.02),
        "w_gate_up": nrm(ks[14], (DEPTH, D_MODEL, 2 * D_FF), D_MODEL ** -0.5),
        "w_down": nrm(ks[15], (DEPTH, D_FF, D_MODEL), D_FF ** -0.5),
        "final_g": 1.0 + nrm(ks[16], (D_MODEL,), 0.02),
    }


def _fwd_reference(x, positions, norm1_g, w_in, b_in, sinks, sgu_ln_g, sgu_ln_b, sgu_w, sgu_b,
              w_attn_branch, w_sgu_branch, w_out, norm2_g, w_gate_up, w_down, final_g):
    B, S = x.shape[0], x.shape[1]
    inv_freq = ROPE_THETA ** (-jnp.arange(0, ROPE_DIM, 2, dtype=jnp.float32) / ROPE_DIM)
    ang = positions.astype(jnp.float32)[..., None] * inv_freq
    cos = jnp.cos(ang)[:, :, None, :]
    sin = jnp.sin(ang)[:, :, None, :]
    h = x
    for l in range(DEPTH):
        xn = rmsnorm(h, norm1_g[l])
        proj = jnp.einsum('bsd,de->bse', xn, w_in[l]) + b_in[l]
        q = proj[..., OFF_Q:OFF_K].reshape(B, S, N_Q_HEADS, HEAD_DIM)
        k = proj[..., OFF_K:OFF_V].reshape(B, S, N_KV_HEADS, HEAD_DIM)
        v = proj[..., OFF_V:OFF_Z].reshape(B, S, N_KV_HEADS, HEAD_DIM)
        z = jax.nn.gelu(proj[..., OFF_Z:OFF_G], approximate=False)
        gates = jax.nn.sigmoid(proj[..., OFF_G:].astype(jnp.float32)).astype(h.dtype)
        g_attn, g_sgu = jnp.split(gates, 2, axis=-1)
        q = partial_rope(q, cos, sin)
        k = partial_rope(k, cos, sin)
        y_attn = sliding_window_attention(q, k, v, sinks[l])
        y_sgu = spatial_gating(z, sgu_ln_g[l], sgu_ln_b[l], sgu_w[l], sgu_b[l])
        merged = (g_attn * jnp.einsum('bse,ed->bsd', y_attn, w_attn_branch[l])
                  + g_sgu * jnp.einsum('bse,ed->bsd', y_sgu, w_sgu_branch[l]))
        h = h + jnp.einsum('bsd,de->bse', merged, w_out[l])
        hn = rmsnorm(h, norm2_g[l])
        gu = jnp.einsum('bsd,df->bsf', hn, w_gate_up[l])
        gate, up = jnp.split(gu, 2, axis=-1)
        h = h + jnp.einsum('bsf,fd->bsd', jax.nn.silu(gate) * up, w_down[l])
    return rmsnorm(h, final_g)


import jax as _jax
import jax.numpy as _jnp

TWIN_FORMAT = 'train_step'
FWD_PARAMS = ['x', 'positions', 'norm1_g', 'w_in', 'b_in', 'sinks', 'sgu_ln_g', 'sgu_ln_b', 'sgu_w', 'sgu_b', 'w_attn_branch', 'w_sgu_branch', 'w_out', 'norm2_g', 'w_gate_up', 'w_down', 'final_g']
TWIN_WEIGHTS = ['norm1_g', 'w_in', 'b_in', 'sinks', 'sgu_ln_g', 'sgu_ln_b', 'sgu_w', 'sgu_b', 'w_attn_branch', 'w_sgu_branch', 'w_out', 'norm2_g', 'w_gate_up', 'w_down', 'final_g']
TWIN_DIFF_INPUT = 'x'
TWIN_INPUTS = ['x', 'positions', 'norm1_g', 'w_in', 'b_in', 'sinks', 'sgu_ln_g', 'sgu_ln_b', 'sgu_w', 'sgu_b', 'w_attn_branch', 'w_sgu_branch', 'w_out', 'norm2_g', 'w_gate_up', 'w_down', 'final_g', 'loss_target', 'm_norm1_g', 'm_w_in', 'm_b_in', 'm_sinks', 'm_sgu_ln_g', 'm_sgu_ln_b', 'm_sgu_w', 'm_sgu_b', 'm_w_attn_branch', 'm_w_sgu_branch', 'm_w_out', 'm_norm2_g', 'm_w_gate_up', 'm_w_down', 'm_final_g', 'v_norm1_g', 'v_w_in', 'v_b_in', 'v_sinks', 'v_sgu_ln_g', 'v_sgu_ln_b', 'v_sgu_w', 'v_sgu_b', 'v_w_attn_branch', 'v_w_sgu_branch', 'v_w_out', 'v_norm2_g', 'v_w_gate_up', 'v_w_down', 'v_final_g']
TWIN_OUTPUTS = ['loss', 'grad_x', 'grad_norm1_g', 'grad_w_in', 'grad_b_in', 'grad_sinks', 'grad_sgu_ln_g', 'grad_sgu_ln_b', 'grad_sgu_w', 'grad_sgu_b', 'grad_w_attn_branch', 'grad_w_sgu_branch', 'grad_w_out', 'grad_norm2_g', 'grad_w_gate_up', 'grad_w_down', 'grad_final_g', 'delta_norm1_g', 'delta_w_in', 'delta_b_in', 'delta_sinks', 'delta_sgu_ln_g', 'delta_sgu_ln_b', 'delta_sgu_w', 'delta_sgu_b', 'delta_w_attn_branch', 'delta_w_sgu_branch', 'delta_w_out', 'delta_norm2_g', 'delta_w_gate_up', 'delta_w_down', 'delta_final_g', 'new_m_norm1_g', 'new_m_w_in', 'new_m_b_in', 'new_m_sinks', 'new_m_sgu_ln_g', 'new_m_sgu_ln_b', 'new_m_sgu_w', 'new_m_sgu_b', 'new_m_w_attn_branch', 'new_m_w_sgu_branch', 'new_m_w_out', 'new_m_norm2_g', 'new_m_w_gate_up', 'new_m_w_down', 'new_m_final_g', 'new_v_norm1_g', 'new_v_w_in', 'new_v_b_in', 'new_v_sinks', 'new_v_sgu_ln_g', 'new_v_sgu_ln_b', 'new_v_sgu_w', 'new_v_sgu_b', 'new_v_w_attn_branch', 'new_v_w_sgu_branch', 'new_v_w_out', 'new_v_norm2_g', 'new_v_w_gate_up', 'new_v_w_down', 'new_v_final_g']
TWIN_LEAF_KINDS = {'loss': 'loss', 'grad_x': 'grad_x', 'grad_norm1_g': 'grad_w', 'grad_w_in': 'grad_w', 'grad_b_in': 'grad_w', 'grad_sinks': 'grad_w', 'grad_sgu_ln_g': 'grad_w', 'grad_sgu_ln_b': 'grad_w', 'grad_sgu_w': 'grad_w', 'grad_sgu_b': 'grad_w', 'grad_w_attn_branch': 'grad_w', 'grad_w_sgu_branch': 'grad_w', 'grad_w_out': 'grad_w', 'grad_norm2_g': 'grad_w', 'grad_w_gate_up': 'grad_w', 'grad_w_down': 'grad_w', 'grad_final_g': 'grad_w', 'delta_norm1_g': 'delta_w', 'delta_w_in': 'delta_w', 'delta_b_in': 'delta_w', 'delta_sinks': 'delta_w', 'delta_sgu_ln_g': 'delta_w', 'delta_sgu_ln_b': 'delta_w', 'delta_sgu_w': 'delta_w', 'delta_sgu_b': 'delta_w', 'delta_w_attn_branch': 'delta_w', 'delta_w_sgu_branch': 'delta_w', 'delta_w_out': 'delta_w', 'delta_norm2_g': 'delta_w', 'delta_w_gate_up': 'delta_w', 'delta_w_down': 'delta_w', 'delta_final_g': 'delta_w', 'new_m_norm1_g': 'new_m', 'new_m_w_in': 'new_m', 'new_m_b_in': 'new_m', 'new_m_sinks': 'new_m', 'new_m_sgu_ln_g': 'new_m', 'new_m_sgu_ln_b': 'new_m', 'new_m_sgu_w': 'new_m', 'new_m_sgu_b': 'new_m', 'new_m_w_attn_branch': 'new_m', 'new_m_w_sgu_branch': 'new_m', 'new_m_w_out': 'new_m', 'new_m_norm2_g': 'new_m', 'new_m_w_gate_up': 'new_m', 'new_m_w_down': 'new_m', 'new_m_final_g': 'new_m', 'new_v_norm1_g': 'new_v', 'new_v_w_in': 'new_v', 'new_v_b_in': 'new_v', 'new_v_sinks': 'new_v', 'new_v_sgu_ln_g': 'new_v', 'new_v_sgu_ln_b': 'new_v', 'new_v_sgu_w': 'new_v', 'new_v_sgu_b': 'new_v', 'new_v_w_attn_branch': 'new_v', 'new_v_w_sgu_branch': 'new_v', 'new_v_w_out': 'new_v', 'new_v_norm2_g': 'new_v', 'new_v_w_gate_up': 'new_v', 'new_v_w_down': 'new_v', 'new_v_final_g': 'new_v'}


def _forward(args):
    return _fwd_reference(*[args[k] for k in FWD_PARAMS])


def _output_shape():
    def fwd():
        inp = _fwd_setup_inputs(0)
        return _fwd_reference(*[inp[k] for k in FWD_PARAMS])
    out = _jax.eval_shape(fwd)
    return out.shape, out.dtype

N_MICROBATCH = 1
ADAM_LR = 0.001
ADAM_B1 = 0.9
ADAM_B2 = 0.999
ADAM_EPS = 1e-08
ADAM_WD = 0.01
ADAM_STEP = 10
PER_EXAMPLE_BATCH_AXIS = {'x': 0, 'positions': 0, 'loss_target': 0}
SHARED_INPUTS = []
_WEIGHT_DTYPES = {'norm1_g': _jnp.float32, 'w_in': _jnp.float32, 'b_in': _jnp.float32, 'sinks': _jnp.float32, 'sgu_ln_g': _jnp.float32, 'sgu_ln_b': _jnp.float32, 'sgu_w': _jnp.float32, 'sgu_b': _jnp.float32, 'w_attn_branch': _jnp.float32, 'w_sgu_branch': _jnp.float32, 'w_out': _jnp.float32, 'norm2_g': _jnp.float32, 'w_gate_up': _jnp.float32, 'w_down': _jnp.float32, 'final_g': _jnp.float32}
MOMENT_SCALE = {'norm1_g': 7.214313e-02, 'w_in': 3.674288e-02, 'b_in': 5.215569e-02, 'sinks': 1.783950e-02, 'sgu_ln_g': 4.246677e-02, 'sgu_ln_b': 4.207401e-02, 'sgu_w': 4.230956e-02, 'sgu_b': 5.988736e-02, 'w_attn_branch': 1.563233e-02, 'w_sgu_branch': 5.408107e-02, 'w_out': 5.509517e-02, 'norm2_g': 8.568138e-02, 'w_gate_up': 3.686665e-02, 'w_down': 6.019322e-02, 'final_g': 3.198748e+01}


def _to_microbatches(a, axis):
    t = _jnp.moveaxis(a, axis, 0)
    t = t.reshape((N_MICROBATCH, t.shape[0] // N_MICROBATCH) + t.shape[1:])
    return _jnp.moveaxis(t, 1, axis + 1)


def setup_inputs(seed: int = 0) -> dict:
    inp = _fwd_setup_inputs(seed)
    key = _jax.random.fold_in(_jax.random.key(seed), 7919)
    shape, _ = _output_shape()
    out = dict(inp)
    out["loss_target"] = _jax.random.normal(_jax.random.fold_in(key, 0), shape, _jnp.float32)
    for i, name in enumerate(TWIN_WEIGHTS):
        w = inp[name].astype(_jnp.float32)
        if MOMENT_SCALE is None:
            s = _jnp.sqrt(_jnp.mean(_jnp.square(w)) + 1e-30)
        else:
            s = MOMENT_SCALE[name]
        km, kv = _jax.random.split(_jax.random.fold_in(key, i + 1))
        out[name] = w
        out["m_" + name] = s * _jax.random.normal(km, w.shape, _jnp.float32)
        out["v_" + name] = (s * s) * _jax.random.uniform(kv, w.shape, _jnp.float32, 0.5, 1.5)
    if N_MICROBATCH > 1:
        for name, axis in PER_EXAMPLE_BATCH_AXIS.items():
            out[name] = _to_microbatches(out[name], axis)
    return {'x': out['x'], 'positions': out['positions'], 'norm1_g': out['norm1_g'], 'w_in': out['w_in'], 'b_in': out['b_in'], 'sinks': out['sinks'], 'sgu_ln_g': out['sgu_ln_g'], 'sgu_ln_b': out['sgu_ln_b'], 'sgu_w': out['sgu_w'], 'sgu_b': out['sgu_b'], 'w_attn_branch': out['w_attn_branch'], 'w_sgu_branch': out['w_sgu_branch'], 'w_out': out['w_out'], 'norm2_g': out['norm2_g'], 'w_gate_up': out['w_gate_up'], 'w_down': out['w_down'], 'final_g': out['final_g'], 'loss_target': out['loss_target'], 'm_norm1_g': out['m_norm1_g'], 'm_w_in': out['m_w_in'], 'm_b_in': out['m_b_in'], 'm_sinks': out['m_sinks'], 'm_sgu_ln_g': out['m_sgu_ln_g'], 'm_sgu_ln_b': out['m_sgu_ln_b'], 'm_sgu_w': out['m_sgu_w'], 'm_sgu_b': out['m_sgu_b'], 'm_w_attn_branch': out['m_w_attn_branch'], 'm_w_sgu_branch': out['m_w_sgu_branch'], 'm_w_out': out['m_w_out'], 'm_norm2_g': out['m_norm2_g'], 'm_w_gate_up': out['m_w_gate_up'], 'm_w_down': out['m_w_down'], 'm_final_g': out['m_final_g'], 'v_norm1_g': out['v_norm1_g'], 'v_w_in': out['v_w_in'], 'v_b_in': out['v_b_in'], 'v_sinks': out['v_sinks'], 'v_sgu_ln_g': out['v_sgu_ln_g'], 'v_sgu_ln_b': out['v_sgu_ln_b'], 'v_sgu_w': out['v_sgu_w'], 'v_sgu_b': out['v_sgu_b'], 'v_w_attn_branch': out['v_w_attn_branch'], 'v_w_sgu_branch': out['v_w_sgu_branch'], 'v_w_out': out['v_w_out'], 'v_norm2_g': out['v_norm2_g'], 'v_w_gate_up': out['v_w_gate_up'], 'v_w_down': out['v_w_down'], 'v_final_g': out['v_final_g']}


def _loss(weights, diff, rest, loss_target):
    with _jax.named_scope("forward"):
        args = {**rest, TWIN_DIFF_INPUT: diff, **{k: w.astype(_WEIGHT_DTYPES[k]) for k, w in weights.items()}}
        y = _forward(args)
    with _jax.named_scope("loss_head"):
        err = _jnp.square(y.astype(_jnp.float32) - loss_target)
        return 0.5 * _jnp.sum(_jnp.mean(err, axis=-1)) if err.ndim else 0.5 * err


def _adamw(w, g, m, v):
    m = ADAM_B1 * m + (1.0 - ADAM_B1) * g
    v = ADAM_B2 * v + (1.0 - ADAM_B2) * _jnp.square(g)
    m_hat = m / (1.0 - ADAM_B1 ** ADAM_STEP)
    v_hat = v / (1.0 - ADAM_B2 ** ADAM_STEP)
    delta = -ADAM_LR * (m_hat / (_jnp.sqrt(v_hat) + ADAM_EPS) + ADAM_WD * w)
    return delta, m, v


def reference(x, positions, norm1_g, w_in, b_in, sinks, sgu_ln_g, sgu_ln_b, sgu_w, sgu_b, w_attn_branch, w_sgu_branch, w_out, norm2_g, w_gate_up, w_down, final_g, loss_target, m_norm1_g, m_w_in, m_b_in, m_sinks, m_sgu_ln_g, m_sgu_ln_b, m_sgu_w, m_sgu_b, m_w_attn_branch, m_w_sgu_branch, m_w_out, m_norm2_g, m_w_gate_up, m_w_down, m_final_g, v_norm1_g, v_w_in, v_b_in, v_sinks, v_sgu_ln_g, v_sgu_ln_b, v_sgu_w, v_sgu_b, v_w_attn_branch, v_w_sgu_branch, v_w_out, v_norm2_g, v_w_gate_up, v_w_down, v_final_g):
    given = dict(x=x, positions=positions, norm1_g=norm1_g, w_in=w_in, b_in=b_in, sinks=sinks, sgu_ln_g=sgu_ln_g, sgu_ln_b=sgu_ln_b, sgu_w=sgu_w, sgu_b=sgu_b, w_attn_branch=w_attn_branch, w_sgu_branch=w_sgu_branch, w_out=w_out, norm2_g=norm2_g, w_gate_up=w_gate_up, w_down=w_down, final_g=final_g, loss_target=loss_target, m_norm1_g=m_norm1_g, m_w_in=m_w_in, m_b_in=m_b_in, m_sinks=m_sinks, m_sgu_ln_g=m_sgu_ln_g, m_sgu_ln_b=m_sgu_ln_b, m_sgu_w=m_sgu_w, m_sgu_b=m_sgu_b, m_w_attn_branch=m_w_attn_branch, m_w_sgu_branch=m_w_sgu_branch, m_w_out=m_w_out, m_norm2_g=m_norm2_g, m_w_gate_up=m_w_gate_up, m_w_down=m_w_down, m_final_g=m_final_g, v_norm1_g=v_norm1_g, v_w_in=v_w_in, v_b_in=v_b_in, v_sinks=v_sinks, v_sgu_ln_g=v_sgu_ln_g, v_sgu_ln_b=v_sgu_ln_b, v_sgu_w=v_sgu_w, v_sgu_b=v_sgu_b, v_w_attn_branch=v_w_attn_branch, v_w_sgu_branch=v_w_sgu_branch, v_w_out=v_w_out, v_norm2_g=v_norm2_g, v_w_gate_up=v_w_gate_up, v_w_down=v_w_down, v_final_g=v_final_g)
    weights = {n: given[n] for n in TWIN_WEIGHTS}
    shared = {n: given[n] for n in SHARED_INPUTS}
    per_example = {n: given[n] for n in ['x', 'positions']}
    grad_fn = _jax.value_and_grad(_loss, argnums=(0, 1))

    def one_microbatch(ex, loss_target):
        ex = dict(ex)
        diff = ex.pop(TWIN_DIFF_INPUT)
        return grad_fn(weights, diff, {**shared, **ex}, loss_target)

    if N_MICROBATCH == 1:
        loss, (grad_w, grad_x) = one_microbatch(per_example, given["loss_target"])
    else:
        def body(carry, xs):
            loss_sum, grad_sum = carry
            l_k, (gw_k, gx_k) = one_microbatch(xs[0], xs[1])
            with _jax.named_scope("update"):
                return (loss_sum + l_k, _jax.tree.map(_jnp.add, grad_sum, gw_k)), gx_k

        init = (_jnp.zeros((), _jnp.float32), _jax.tree.map(_jnp.zeros_like, weights))
        (loss, grad_w), grad_x = _jax.lax.scan(body, init, (per_example, given["loss_target"]))
    with _jax.named_scope("update"):
        delta_w, new_m, new_v = {}, {}, {}
        for n in TWIN_WEIGHTS:
            delta_w[n], new_m[n], new_v[n] = _adamw(weights[n], grad_w[n], given["m_" + n], given["v_" + n])
    return (loss, grad_x, *[grad_w[n] for n in TWIN_WEIGHTS], *[delta_w[n] for n in TWIN_WEIGHTS],
            *[new_m[n] for n in TWIN_WEIGHTS], *[new_v[n] for n in TWIN_WEIGHTS])
```

```python
import math

import jax
import jax.numpy as jnp
from jax import lax
from jax.experimental import pallas as pl
from jax.experimental.pallas import tpu as pltpu

F32 = jnp.float32
BF16 = jnp.bfloat16
MESH = pl.DeviceIdType.MESH
ANY = pl.BlockSpec(memory_space=pl.ANY)

HEAD_DIM = 64
N_KV_HEADS = 4
WINDOW = 128
ROPE_DIM = HEAD_DIM // 4
ROPE_THETA = 500000.0
EPS = 1e-5
NEG = -1e30
N_CHIPS = 4
LANES = 128
V7X_VMEM_LIMIT = 56 * 1024 * 1024

ADAM_LR = 0.001
ADAM_B1 = 0.9
ADAM_B2 = 0.999
ADAM_EPS = 1e-08
ADAM_WD = 0.01
ADAM_STEP = 10

NN = (((1,), (0,)), ((), ()))
NT = (((1,), (1,)), ((), ()))
TN = (((0,), (0,)), ((), ()))


ROW_TILES = (1024, 512, 256, 128, 64, 32, 16, 8)
BLOCK_BYTES = 2 * 1024 * 1024


def _pick(n, prefs):
    for p in prefs:
        if n % p == 0:
            return p
    raise ValueError(f"no tile for {n} among {prefs}")


def _row_tile(rows, cols, itemsize=4):
    return _pick(rows, [t for t in ROW_TILES if t * cols * itemsize <= BLOCK_BYTES or t == ROW_TILES[-1]])


def _dot(a, b, dims):
    return lax.dot_general(a, b, dims, preferred_element_type=F32)


def _sigmoid(x):
    return 1.0 / (1.0 + jnp.exp(-x))


def _gelu(x):
    return 0.5 * x * (1.0 + lax.erf(x * (1.0 / math.sqrt(2.0))))


def _gelu_grad(x):
    return 0.5 * (1.0 + lax.erf(x * (1.0 / math.sqrt(2.0)))) + x * jnp.exp(-0.5 * x * x) * (1.0 / math.sqrt(2.0 * math.pi))


def _params(sem):
    return pltpu.CompilerParams(dimension_semantics=sem, vmem_limit_bytes=V7X_VMEM_LIMIT)


def _matmul(name, lhs, rhs_list, *, dims, grid, lhs_spec, rhs_specs, acc_shape, out_shape, out_specs,
            epilogue, extra=(), extra_specs=()):
    gk = grid[2]
    nr, ne, no = len(rhs_list), len(extra), len(out_shape)

    def body(*refs):
        a_ref = refs[0]
        b_refs = refs[1:1 + nr]
        e_refs = refs[1 + nr:1 + nr + ne]
        o_refs = refs[1 + nr + ne:1 + nr + ne + no]
        acc_refs = refs[1 + nr + ne + no:]
        a = a_ref[...]
        parts = [_dot(a, b[...], dims) for b in b_refs]
        if gk == 1:
            epilogue(parts, e_refs, o_refs)
            return
        k = pl.program_id(2)

        @pl.when(k == 0)
        def _():
            for acc, p in zip(acc_refs, parts):
                acc[...] = p

        @pl.when(k > 0)
        def _():
            for acc, p in zip(acc_refs, parts):
                acc[...] += p

        @pl.when(k == gk - 1)
        def _():
            epilogue([acc[...] for acc in acc_refs], e_refs, o_refs)

    scratch = [pltpu.VMEM(acc_shape, F32) for _ in range(nr)] if gk > 1 else []
    return pl.pallas_call(
        body, name=name, grid=grid,
        in_specs=[lhs_spec, *rhs_specs, *extra_specs],
        out_specs=out_specs, out_shape=out_shape, scratch_shapes=scratch,
        compiler_params=_params(("parallel", "parallel", "arbitrary")),
    )(lhs, *rhs_list, *extra)


def _store_epilogue(dtype):
    def ep(parts, e_refs, o_refs):
        o_refs[0][...] = parts[0].astype(dtype)
    return ep


def _rms_fwd(h, g_row):
    S, D = h.shape
    tm = _row_tile(S, D)

    def body(h_ref, g_ref, o_ref):
        x = h_ref[...]
        r = lax.rsqrt(jnp.mean(x * x, axis=-1, keepdims=True) + EPS)
        o_ref[...] = (x * r * g_ref[...]).astype(BF16)

    return pl.pallas_call(
        body, name="rms_fwd", grid=(S // tm,),
        in_specs=[pl.BlockSpec((tm, D), lambda i: (i, 0)), pl.BlockSpec((1, D), lambda i: (0, 0))],
        out_specs=pl.BlockSpec((tm, D), lambda i: (i, 0)),
        out_shape=jax.ShapeDtypeStruct((S, D), BF16),
        compiler_params=_params(("parallel",)),
    )(h, g_row)


def _rms_bwd(dy, h, g_row, dres):
    S, D = h.shape
    tm = _row_tile(S, D)

    def body(dy_ref, h_ref, g_ref, dres_ref, dh_ref, dhb_ref, dg_ref):
        i = pl.program_id(0)
        x = h_ref[...]
        d = dy_ref[...]
        r = lax.rsqrt(jnp.mean(x * x, axis=-1, keepdims=True) + EPS)
        dg = d * g_ref[...]
        dot = jnp.mean(dg * x, axis=-1, keepdims=True)
        dh = dres_ref[...] + r * dg - x * (r * r * r) * dot
        dh_ref[...] = dh
        dhb_ref[...] = dh.astype(BF16)
        part = jnp.sum(d * x * r, axis=0, keepdims=True)

        @pl.when(i == 0)
        def _():
            dg_ref[...] = jnp.zeros_like(dg_ref)

        dg_ref[0:1, :] += part

    return pl.pallas_call(
        body, name="rms_bwd", grid=(S // tm,),
        in_specs=[pl.BlockSpec((tm, D), lambda i: (i, 0)), pl.BlockSpec((tm, D), lambda i: (i, 0)),
                  pl.BlockSpec((1, D), lambda i: (0, 0)), pl.BlockSpec((tm, D), lambda i: (i, 0))],
        out_specs=[pl.BlockSpec((tm, D), lambda i: (i, 0)), pl.BlockSpec((tm, D), lambda i: (i, 0)),
                   pl.BlockSpec((8, D), lambda i: (0, 0))],
        out_shape=[jax.ShapeDtypeStruct((S, D), F32), jax.ShapeDtypeStruct((S, D), BF16),
                   jax.ShapeDtypeStruct((8, D), F32)],
        compiler_params=_params(("arbitrary",)),
    )(dy, h, g_row, dres)


def _loss_head(h, g_row, target):
    S, D = h.shape
    tm = _row_tile(S, D)

    def body(h_ref, g_ref, t_ref, dh_ref, dhb_ref, dg_ref, loss_ref):
        i = pl.program_id(0)
        x = h_ref[...]
        g = g_ref[...]
        r = lax.rsqrt(jnp.mean(x * x, axis=-1, keepdims=True) + EPS)
        y = x * r * g
        e = y - t_ref[...]
        d = e * (1.0 / D)
        dg = d * g
        dot = jnp.mean(dg * x, axis=-1, keepdims=True)
        dh = r * dg - x * (r * r * r) * dot
        dh_ref[...] = dh
        dhb_ref[...] = dh.astype(BF16)

        @pl.when(i == 0)
        def _():
            dg_ref[...] = jnp.zeros_like(dg_ref)
            loss_ref[...] = jnp.zeros_like(loss_ref)

        dg_ref[0:1, :] += jnp.sum(d * x * r, axis=0, keepdims=True)
        loss_ref[0:1, :] += jnp.sum((0.5 / D) * e * e, axis=0, keepdims=True)

    return pl.pallas_call(
        body, name="loss_head", grid=(S // tm,),
        in_specs=[pl.BlockSpec((tm, D), lambda i: (i, 0)), pl.BlockSpec((1, D), lambda i: (0, 0)),
                  pl.BlockSpec((tm, D), lambda i: (i, 0))],
        out_specs=[pl.BlockSpec((tm, D), lambda i: (i, 0)), pl.BlockSpec((tm, D), lambda i: (i, 0)),
                   pl.BlockSpec((8, D), lambda i: (0, 0)), pl.BlockSpec((8, D), lambda i: (0, 0))],
        out_shape=[jax.ShapeDtypeStruct((S, D), F32), jax.ShapeDtypeStruct((S, D), BF16),
                   jax.ShapeDtypeStruct((8, D), F32), jax.ShapeDtypeStruct((8, D), F32)],
        compiler_params=_params(("arbitrary",)),
    )(h, g_row, target)


def _colsum(x):
    S, C = x.shape
    tm = _row_tile(S, C)

    def body(x_ref, o_ref):
        i = pl.program_id(0)

        @pl.when(i == 0)
        def _():
            o_ref[...] = jnp.zeros_like(o_ref)

        o_ref[0:1, :] += jnp.sum(x_ref[...].astype(F32), axis=0, keepdims=True)

    return pl.pallas_call(
        body, name="colsum", grid=(S // tm,),
        in_specs=[pl.BlockSpec((tm, C), lambda i: (i, 0))],
        out_specs=pl.BlockSpec((8, C), lambda i: (0, 0)),
        out_shape=jax.ShapeDtypeStruct((8, C), F32),
        compiler_params=_params(("arbitrary",)),
    )(x)


def _rope(t, cos, sa, sb):
    w = t.shape[-1]
    return t * cos + pltpu.roll(t, w - 8, 1) * sa + pltpu.roll(t, 8, 1) * sb


def _rope_t(g, cos, sa, sb):
    w = g.shape[-1]
    return g * cos + pltpu.roll(g * sa, 8, 1) + pltpu.roll(g * sb, w - 8, 1)


def _band_mask(n):
    qi = lax.broadcasted_iota(jnp.int32, (WINDOW, 2 * WINDOW), 0)
    kj = lax.broadcasted_iota(jnp.int32, (WINDOW, 2 * WINDOW), 1)
    rel = qi + WINDOW - kj
    ok = (rel >= 0) & (rel < WINDOW)
    return ok & ((kj >= WINDOW) | (n > 0))


def _attn_specs(dm, nb):
    A, KV = dm["A"], dm["KV"]
    kb, vb = dm["OFF_K"] // KV, dm["OFF_V"] // KV
    cur = lambda n: jnp.minimum(n, nb - 1)
    prev = lambda n: jnp.maximum(jnp.minimum(n, nb - 1) - 1, 0)
    proj_specs = [
        pl.BlockSpec((WINDOW, A), lambda n: (cur(n), 0)),
        pl.BlockSpec((WINDOW, KV), lambda n: (prev(n), kb)),
        pl.BlockSpec((WINDOW, KV), lambda n: (cur(n), kb)),
        pl.BlockSpec((WINDOW, KV), lambda n: (prev(n), vb)),
        pl.BlockSpec((WINDOW, KV), lambda n: (cur(n), vb)),
    ]
    trig_cur = [pl.BlockSpec((WINDOW, LANES), lambda n: (cur(n), 0)) for _ in range(3)]
    trig_prev = [pl.BlockSpec((WINDOW, LANES), lambda n: (prev(n), 0)) for _ in range(3)]
    return proj_specs, trig_cur, trig_prev, cur, prev


def _attn_fwd(proj, trig, sink_row, dm):
    S = proj.shape[0]
    A, KV, NQ = dm["A"], dm["KV"], dm["NQ"]
    qpk = NQ // N_KV_HEADS
    nb = S // WINDOW
    scale = HEAD_DIM ** -0.5
    proj_specs, trig_cur, trig_prev, cur, _ = _attn_specs(dm, nb)

    def body(q_ref, kp_ref, kc_ref, vp_ref, vc_ref, cc_ref, sac_ref, sbc_ref, cp_ref, sap_ref, sbp_ref,
             sink_ref, y_ref, lse_ref):
        n = pl.program_id(0)
        tq = lambda r: jnp.tile(r[...], (1, A // LANES))
        tk = lambda rp, rc: jnp.tile(jnp.concatenate([rp[...], rc[...]], axis=0), (1, KV // LANES))
        qr = _rope(q_ref[...].astype(F32), tq(cc_ref), tq(sac_ref), tq(sbc_ref)).astype(BF16)
        kband = jnp.concatenate([kp_ref[...], kc_ref[...]], axis=0).astype(F32)
        kr = _rope(kband, tk(cp_ref, cc_ref), tk(sap_ref, sac_ref), tk(sbp_ref, sbc_ref)).astype(BF16)
        vband = jnp.concatenate([vp_ref[...], vc_ref[...]], axis=0)
        mask = _band_mask(n)
        lane = lax.broadcasted_iota(jnp.int32, (WINDOW, LANES), 1)
        lse_all = jnp.zeros((WINDOW, LANES), F32)
        for g in range(N_KV_HEADS):
            k_g = kr[:, g * HEAD_DIM:(g + 1) * HEAD_DIM]
            v_g = vband[:, g * HEAD_DIM:(g + 1) * HEAD_DIM]
            for hh in range(qpk):
                h = g * qpk + hh
                q_h = qr[:, h * HEAD_DIM:(h + 1) * HEAD_DIM]
                s = jnp.where(mask, _dot(q_h, k_g, NT) * scale, NEG)
                sink = sink_ref[0:1, h:h + 1]
                m = jnp.maximum(jnp.max(s, axis=-1, keepdims=True), sink)
                p = jnp.exp(s - m)
                den = jnp.sum(p, axis=-1, keepdims=True) + jnp.exp(sink - m)
                p = p / den
                y_ref[:, h * HEAD_DIM:(h + 1) * HEAD_DIM] = _dot(p.astype(BF16), v_g, NN).astype(BF16)
                lse_all = jnp.where(lane == h, m + jnp.log(den), lse_all)
        lse_ref[...] = lse_all

    return pl.pallas_call(
        body, name="attn_fwd", grid=(nb,),
        in_specs=[*proj_specs, *trig_cur, *trig_prev, pl.BlockSpec((8, LANES), lambda n: (0, 0))],
        out_specs=[pl.BlockSpec((WINDOW, A), lambda n: (n, 0)), pl.BlockSpec((WINDOW, LANES), lambda n: (n, 0))],
        out_shape=[jax.ShapeDtypeStruct((S, A), BF16), jax.ShapeDtypeStruct((S, LANES), F32)],
        compiler_params=_params(("parallel",)),
    )(proj, proj, proj, proj, proj, *trig, *trig, sink_row)


def _attn_bwd(proj, trig, sink_row, y, lse, dy, dm):
    S = proj.shape[0]
    A, KV, NQ = dm["A"], dm["KV"], dm["NQ"]
    qpk = NQ // N_KV_HEADS
    nb = S // WINDOW
    scale = HEAD_DIM ** -0.5
    proj_specs, trig_cur, trig_prev, cur, prev = _attn_specs(dm, nb)

    def body(q_ref, kp_ref, kc_ref, vp_ref, vc_ref, cc_ref, sac_ref, sbc_ref, cp_ref, sap_ref, sbp_ref,
             sink_ref, y_ref, lse_ref, dy_ref, dq_ref, dk_ref, dv_ref, dsink_ref,
             ck_ref, cv_ref, bk_ref, bv_ref, dqr_ref):
        n = pl.program_id(0)

        @pl.when(n == 0)
        def _():
            dsink_ref[...] = jnp.zeros_like(dsink_ref)
            ck_ref[...] = jnp.zeros_like(ck_ref)
            cv_ref[...] = jnp.zeros_like(cv_ref)

        @pl.when(n < nb)
        def _():
            tq = lambda r: jnp.tile(r[...], (1, A // LANES))
            tk = lambda rp, rc: jnp.tile(jnp.concatenate([rp[...], rc[...]], axis=0), (1, KV // LANES))
            cq, saq, sbq = tq(cc_ref), tq(sac_ref), tq(sbc_ref)
            ck, sak, sbk = tk(cp_ref, cc_ref), tk(sap_ref, sac_ref), tk(sbp_ref, sbc_ref)
            qr = _rope(q_ref[...].astype(F32), cq, saq, sbq).astype(BF16)
            kband = jnp.concatenate([kp_ref[...], kc_ref[...]], axis=0).astype(F32)
            kr = _rope(kband, ck, sak, sbk).astype(BF16)
            vband = jnp.concatenate([vp_ref[...], vc_ref[...]], axis=0)
            mask = _band_mask(n)
            lane = lax.broadcasted_iota(jnp.int32, (1, LANES), 1)
            lse_all = lse_ref[...]
            dsink = jnp.zeros((1, LANES), F32)
            for g in range(N_KV_HEADS):
                k_g = kr[:, g * HEAD_DIM:(g + 1) * HEAD_DIM]
                v_g = vband[:, g * HEAD_DIM:(g + 1) * HEAD_DIM]
                dk_acc = jnp.zeros((2 * WINDOW, HEAD_DIM), F32)
                dv_acc = jnp.zeros((2 * WINDOW, HEAD_DIM), F32)
                for hh in range(qpk):
                    h = g * qpk + hh
                    q_h = qr[:, h * HEAD_DIM:(h + 1) * HEAD_DIM]
                    dy_h = dy_ref[:, h * HEAD_DIM:(h + 1) * HEAD_DIM]
                    y_h = y_ref[:, h * HEAD_DIM:(h + 1) * HEAD_DIM]
                    lse_h = lse_all[:, h:h + 1]
                    s = jnp.where(mask, _dot(q_h, k_g, NT) * scale, NEG)
                    p = jnp.exp(s - lse_h)
                    dp = _dot(dy_h, v_g, NT)
                    delta = jnp.sum(dy_h.astype(F32) * y_h.astype(F32), axis=-1, keepdims=True)
                    ds = (p * (dp - delta) * scale).astype(BF16)
                    dqr_ref[:, h * HEAD_DIM:(h + 1) * HEAD_DIM] = _dot(ds, k_g, NN)
                    dk_acc = dk_acc + _dot(ds, q_h, TN)
                    dv_acc = dv_acc + _dot(p.astype(BF16), dy_h, TN)
                    psink = jnp.exp(sink_ref[0:1, h:h + 1] - lse_h)
                    dsink = dsink + jnp.where(lane == h, -jnp.sum(psink * delta, axis=0, keepdims=True), 0.0)
                bk_ref[:, g * HEAD_DIM:(g + 1) * HEAD_DIM] = dk_acc
                bv_ref[:, g * HEAD_DIM:(g + 1) * HEAD_DIM] = dv_acc
            dsink_ref[0:1, :] += dsink
            dq_ref[...] = _rope_t(dqr_ref[...], cq, saq, sbq).astype(BF16)
            dkb = _rope_t(bk_ref[...], ck, sak, sbk)
            dvb = bv_ref[...]
            dk_ref[...] = (ck_ref[...] + dkb[:WINDOW]).astype(BF16)
            dv_ref[...] = (cv_ref[...] + dvb[:WINDOW]).astype(BF16)
            ck_ref[...] = dkb[WINDOW:]
            cv_ref[...] = dvb[WINDOW:]

        @pl.when(n == nb)
        def _():
            dk_ref[...] = ck_ref[...].astype(BF16)
            dv_ref[...] = cv_ref[...].astype(BF16)

    row = lambda w: pl.BlockSpec((WINDOW, w), lambda n: (cur(n), 0))
    done = lambda w: pl.BlockSpec((WINDOW, w), lambda n: (jnp.maximum(n - 1, 0), 0))
    return pl.pallas_call(
        body, name="attn_bwd", grid=(nb + 1,),
        in_specs=[*proj_specs, *trig_cur, *trig_prev, pl.BlockSpec((8, LANES), lambda n: (0, 0)),
                  row(A), row(LANES), row(A)],
        out_specs=[row(A), done(KV), done(KV), pl.BlockSpec((8, LANES), lambda n: (0, 0))],
        out_shape=[jax.ShapeDtypeStruct((S, A), BF16), jax.ShapeDtypeStruct((S, KV), BF16),
                   jax.ShapeDtypeStruct((S, KV), BF16), jax.ShapeDtypeStruct((8, LANES), F32)],
        scratch_shapes=[pltpu.VMEM((WINDOW, KV), F32), pltpu.VMEM((WINDOW, KV), F32),
                        pltpu.VMEM((2 * WINDOW, KV), F32), pltpu.VMEM((2 * WINDOW, KV), F32),
                        pltpu.VMEM((WINDOW, A), F32)],
        compiler_params=_params(("arbitrary",)),
    )(proj, proj, proj, proj, proj, *trig, *trig, sink_row, y, lse, dy)


def _sgu_layout(dm, S):
    G = dm["G"]
    pw = math.gcd(dm["OFF_Z"], G)
    npc = G // pw
    tm = _pick(S, (256, 128))
    u_specs = [pl.BlockSpec((tm, pw), lambda i, p=p: (i, dm["OFF_Z"] // pw + p)) for p in range(npc)]
    v_specs = [pl.BlockSpec((tm, pw), lambda i, p=p: (i, (dm["OFF_Z"] + G) // pw + p)) for p in range(npc)]
    return pw, npc, tm, u_specs, v_specs


def _sgu_norm(v_refs, lg_ref, lb_ref):
    v = jnp.concatenate([_gelu(r[...].astype(F32)) for r in v_refs], axis=1)
    mu = jnp.mean(v, axis=-1, keepdims=True)
    vc = v - mu
    rstd = lax.rsqrt(jnp.mean(vc * vc, axis=-1, keepdims=True) + EPS)
    xhat = vc * rstd
    return xhat, rstd, (xhat * lg_ref[...] + lb_ref[...]).astype(BF16)


def _sgu_fwd(proj, w_tril, b_t, ln_g_row, ln_b_row, dm):
    S = proj.shape[0]
    G, NG = dm["G"], dm["NG"]
    pw, npc, tm, u_specs, v_specs = _sgu_layout(dm, S)
    nch = tm // WINDOW

    def body(*refs):
        u_refs, v_refs = refs[:npc], refs[npc:2 * npc]
        w_ref, bt_ref, lg_ref, lb_ref, y_ref = refs[2 * npc:]
        _, _, vn = _sgu_norm(v_refs, lg_ref, lb_ref)
        u = jnp.concatenate([_gelu(r[...].astype(F32)) for r in u_refs], axis=1)
        for c in range(nch):
            rows = slice(c * WINDOW, (c + 1) * WINDOW)
            for g in range(NG):
                cols = slice(g * LANES, (g + 1) * LANES)
                sv = _dot(w_ref[g], vn[rows, cols], NN) + bt_ref[:, g:g + 1]
                y_ref[rows, cols] = (u[rows, cols] * sv).astype(BF16)

    return pl.pallas_call(
        body, name="sgu_fwd", grid=(S // tm,),
        in_specs=[*u_specs, *v_specs,
                  pl.BlockSpec((NG, WINDOW, WINDOW), lambda i: (0, 0, 0)),
                  pl.BlockSpec((WINDOW, LANES), lambda i: (0, 0)),
                  pl.BlockSpec((1, G), lambda i: (0, 0)), pl.BlockSpec((1, G), lambda i: (0, 0))],
        out_specs=pl.BlockSpec((tm, G), lambda i: (i, 0)),
        out_shape=jax.ShapeDtypeStruct((S, G), BF16),
        compiler_params=_params(("parallel",)),
    )(*([proj] * (2 * npc)), w_tril, b_t, ln_g_row, ln_b_row)


def _sgu_bwd(proj, w_tril, b_t, ln_g_row, ln_b_row, dy, dm):
    S = proj.shape[0]
    G, NG = dm["G"], dm["NG"]
    pw, npc, tm, u_specs, v_specs = _sgu_layout(dm, S)
    nch = tm // WINDOW

    def body(*refs):
        u_refs, v_refs = refs[:npc], refs[npc:2 * npc]
        w_ref, bt_ref, lg_ref, lb_ref, dy_ref, dz_ref, dw_ref, dbt_ref, dlg_ref, dlb_ref, dvn_ref = refs[2 * npc:]
        i = pl.program_id(0)

        @pl.when(i == 0)
        def _():
            dw_ref[...] = jnp.zeros_like(dw_ref)
            dbt_ref[...] = jnp.zeros_like(dbt_ref)
            dlg_ref[...] = jnp.zeros_like(dlg_ref)
            dlb_ref[...] = jnp.zeros_like(dlb_ref)

        xhat, rstd, vn = _sgu_norm(v_refs, lg_ref, lb_ref)
        u_pre = jnp.concatenate([r[...].astype(F32) for r in u_refs], axis=1)
        u = _gelu(u_pre)
        dy = dy_ref[...].astype(F32)
        lane = lax.broadcasted_iota(jnp.int32, (WINDOW, LANES), 1)
        tri = lax.broadcasted_iota(jnp.int32, (WINDOW, WINDOW), 0) >= lax.broadcasted_iota(jnp.int32, (WINDOW, WINDOW), 1)
        dbt = jnp.zeros((WINDOW, LANES), F32)
        for c in range(nch):
            rows = slice(c * WINDOW, (c + 1) * WINDOW)
            for g in range(NG):
                cols = slice(g * LANES, (g + 1) * LANES)
                vn_cg = vn[rows, cols]
                sv = _dot(w_ref[g], vn_cg, NN) + bt_ref[:, g:g + 1]
                dy_cg = dy[rows, cols]
                dsv = dy_cg * u[rows, cols]
                dsv_b = dsv.astype(BF16)
                dz_ref[rows, cols] = (dy_cg * sv * _gelu_grad(u_pre[rows, cols])).astype(BF16)
                dvn_ref[rows, cols] = _dot(w_ref[g], dsv_b, TN)
                dw_ref[g] += jnp.where(tri, _dot(dsv_b, vn_cg, NT), 0.0)
                dbt = dbt + jnp.where(lane == g, jnp.sum(dsv, axis=-1, keepdims=True), 0.0)
        dbt_ref[...] += dbt
        dvn = dvn_ref[...]
        dlg_ref[0:1, :] += jnp.sum(dvn * xhat, axis=0, keepdims=True)
        dlb_ref[0:1, :] += jnp.sum(dvn, axis=0, keepdims=True)
        dxh = dvn * lg_ref[...]
        dv = rstd * (dxh - jnp.mean(dxh, axis=-1, keepdims=True) - xhat * jnp.mean(dxh * xhat, axis=-1, keepdims=True))
        v_pre = jnp.concatenate([r[...].astype(F32) for r in v_refs], axis=1)
        dz_ref[:, G:] = (dv * _gelu_grad(v_pre)).astype(BF16)

    return pl.pallas_call(
        body, name="sgu_bwd", grid=(S // tm,),
        in_specs=[*u_specs, *v_specs,
                  pl.BlockSpec((NG, WINDOW, WINDOW), lambda i: (0, 0, 0)),
                  pl.BlockSpec((WINDOW, LANES), lambda i: (0, 0)),
                  pl.BlockSpec((1, G), lambda i: (0, 0)), pl.BlockSpec((1, G), lambda i: (0, 0)),
                  pl.BlockSpec((tm, G), lambda i: (i, 0))],
        out_specs=[pl.BlockSpec((tm, 2 * G), lambda i: (i, 0)),
                   pl.BlockSpec((NG, WINDOW, WINDOW), lambda i: (0, 0, 0)),
                   pl.BlockSpec((WINDOW, LANES), lambda i: (0, 0)),
                   pl.BlockSpec((8, G), lambda i: (0, 0)), pl.BlockSpec((8, G), lambda i: (0, 0))],
        out_shape=[jax.ShapeDtypeStruct((S, 2 * G), BF16), jax.ShapeDtypeStruct((NG, WINDOW, WINDOW), F32),
                   jax.ShapeDtypeStruct((WINDOW, LANES), F32), jax.ShapeDtypeStruct((8, G), F32),
                   jax.ShapeDtypeStruct((8, G), F32)],
        scratch_shapes=[pltpu.VMEM((tm, G), F32)],
        compiler_params=_params(("arbitrary",)),
    )(*([proj] * (2 * npc)), w_tril, b_t, ln_g_row, ln_b_row, dy)


def _in_proj(xn, w_in_g, b_row, dm):
    S, D = xn.shape
    IN = dm["IN"]
    cw = IN // N_CHIPS
    tm = _pick(S, (512, 256, 128))
    tn = _pick(cw, (640, 512, 256, 128))
    nbc = cw // tn

    def ep(parts, e_refs, o_refs):
        o_refs[0][...] = (parts[0] + e_refs[0][...]).astype(BF16)

    return _matmul(
        "in_proj", xn, [w_in_g], dims=NN, grid=(S // tm, IN // tn, 1),
        lhs_spec=pl.BlockSpec((tm, D), lambda i, j, k: (i, 0)),
        rhs_specs=[pl.BlockSpec((None, D, tn), lambda i, j, k: (j // nbc, 0, j % nbc))],
        acc_shape=(tm, tn), extra=[b_row], extra_specs=[pl.BlockSpec((1, tn), lambda i, j, k: (0, j))],
        out_shape=[jax.ShapeDtypeStruct((S, IN), BF16)],
        out_specs=[pl.BlockSpec((tm, tn), lambda i, j, k: (i, j))], epilogue=ep)[0]


def _branch_attn(y_attn, w_ab_g, dm):
    S, A = y_attn.shape
    D = dm["D"]
    cw = D // N_CHIPS
    tm = _pick(S, (1024, 512, 256, 128))
    return _matmul(
        "branch_attn", y_attn, [w_ab_g], dims=NN, grid=(S // tm, N_CHIPS, 1),
        lhs_spec=pl.BlockSpec((tm, A), lambda i, j, k: (i, 0)),
        rhs_specs=[pl.BlockSpec((None, A, cw), lambda i, j, k: (j, 0, 0))],
        acc_shape=(tm, cw), out_shape=[jax.ShapeDtypeStruct((S, D), BF16)],
        out_specs=[pl.BlockSpec((tm, cw), lambda i, j, k: (i, j))], epilogue=_store_epilogue(BF16))[0]


def _branch_sgu_merge(y_sgu, w_sb_g, a_attn, proj, dm):
    S, G = y_sgu.shape
    D, OFF_G = dm["D"], dm["OFF_G"]
    cw = D // N_CHIPS
    tm = _pick(S, (1024, 512, 256, 128))

    def ep(parts, e_refs, o_refs):
        a_sgu = parts[0].astype(BF16)
        ga = _sigmoid(e_refs[1][...].astype(F32))
        gs = _sigmoid(e_refs[2][...].astype(F32))
        o_refs[0][...] = a_sgu
        o_refs[1][...] = (ga * e_refs[0][...].astype(F32) + gs * a_sgu.astype(F32)).astype(BF16)

    blk = pl.BlockSpec((tm, cw), lambda i, j, k: (i, j))
    return _matmul(
        "branch_sgu_merge", y_sgu, [w_sb_g], dims=NN, grid=(S // tm, N_CHIPS, 1),
        lhs_spec=pl.BlockSpec((tm, G), lambda i, j, k: (i, 0)),
        rhs_specs=[pl.BlockSpec((None, G, cw), lambda i, j, k: (j, 0, 0))],
        acc_shape=(tm, cw), extra=[a_attn, proj, proj],
        extra_specs=[blk, pl.BlockSpec((tm, cw), lambda i, j, k: (i, OFF_G // cw + j)),
                     pl.BlockSpec((tm, cw), lambda i, j, k: (i, (OFF_G + D) // cw + j))],
        out_shape=[jax.ShapeDtypeStruct((S, D), BF16), jax.ShapeDtypeStruct((S, D), BF16)],
        out_specs=[blk, blk], epilogue=ep)


def _residual_matmul(name, a, w_g, h):
    S, K = a.shape
    D = w_g.shape[1]
    tm = _pick(S, (512, 256, 128))
    tn = _pick(D, (512, 256, 128))

    def ep(parts, e_refs, o_refs):
        o_refs[0][...] = e_refs[0][...] + parts[0]

    blk = pl.BlockSpec((tm, tn), lambda i, j, k: (i, j))
    return _matmul(
        name, a, [w_g], dims=NN, grid=(S // tm, D // tn, 1),
        lhs_spec=pl.BlockSpec((tm, K), lambda i, j, k: (i, 0)),
        rhs_specs=[pl.BlockSpec((K, tn), lambda i, j, k: (0, j))],
        acc_shape=(tm, tn), extra=[h], extra_specs=[blk],
        out_shape=[jax.ShapeDtypeStruct((S, D), F32)], out_specs=[blk], epilogue=ep)[0]


def _gate_up(hn, w_gu_g, dm):
    S, D = hn.shape
    Fd = dm["F"]
    cw = 2 * Fd // N_CHIPS
    tm = _pick(S, (512, 256, 128))
    tn = _pick(cw, (1408, 512, 384, 256, 128))
    nbc = cw // tn
    half = N_CHIPS // 2

    def ep(parts, e_refs, o_refs):
        gate, up = parts[0].astype(BF16), parts[1].astype(BF16)
        o_refs[0][0] = gate
        o_refs[0][1] = up
        g32 = gate.astype(F32)
        o_refs[1][...] = (g32 * _sigmoid(g32) * up.astype(F32)).astype(BF16)

    return _matmul(
        "gate_up", hn, [w_gu_g, w_gu_g], dims=NN, grid=(S // tm, Fd // tn, 1),
        lhs_spec=pl.BlockSpec((tm, D), lambda i, j, k: (i, 0)),
        rhs_specs=[pl.BlockSpec((None, D, tn), lambda i, j, k: (j // nbc, 0, j % nbc)),
                   pl.BlockSpec((None, D, tn), lambda i, j, k: (half + j // nbc, 0, j % nbc))],
        acc_shape=(tm, tn),
        out_shape=[jax.ShapeDtypeStruct((2, S, Fd), BF16), jax.ShapeDtypeStruct((S, Fd), BF16)],
        out_specs=[pl.BlockSpec((2, tm, tn), lambda i, j, k: (0, i, j)), pl.BlockSpec((tm, tn), lambda i, j, k: (i, j))],
        epilogue=ep)


def _down_bwd(dh_b, w_down_g, gu, dm):
    S, D = dh_b.shape
    Fd = dm["F"]
    tm = _pick(S, (1024, 512, 256, 128))
    tn = _pick(Fd, (512, 256, 128))

    def ep(parts, e_refs, o_refs):
        gate = e_refs[0][0].astype(F32)
        up = e_refs[0][1].astype(F32)
        s = _sigmoid(gate)
        dact = parts[0]
        o_refs[0][0] = (dact * up * s * (1.0 + gate * (1.0 - s))).astype(BF16)
        o_refs[0][1] = (dact * gate * s).astype(BF16)

    blk = pl.BlockSpec((2, tm, tn), lambda i, j, k: (0, i, j))
    return _matmul(
        "down_bwd", dh_b, [w_down_g], dims=NT, grid=(S // tm, Fd // tn, 1),
        lhs_spec=pl.BlockSpec((tm, D), lambda i, j, k: (i, 0)),
        rhs_specs=[pl.BlockSpec((tn, D), lambda i, j, k: (j, 0))],
        acc_shape=(tm, tn), extra=[gu], extra_specs=[blk],
        out_shape=[jax.ShapeDtypeStruct((2, S, Fd), BF16)], out_specs=[blk], epilogue=ep)[0]


def _gate_up_bwd(dgu, w_gu_g, dm):
    S = dgu.shape[1]
    D, Fd = dm["D"], dm["F"]
    cw = 2 * Fd // N_CHIPS
    half = N_CHIPS // 2
    tm = _pick(S, (512, 256, 128))
    tn = _pick(D, (1024, 512, 256, 128))
    return _matmul(
        "gate_up_bwd", dgu, [w_gu_g], dims=NT, grid=(S // tm, D // tn, N_CHIPS),
        lhs_spec=pl.BlockSpec((None, tm, cw), lambda i, j, k: (k // half, i, k % half)),
        rhs_specs=[pl.BlockSpec((None, tn, cw), lambda i, j, k: (k, j, 0))],
        acc_shape=(tm, tn), out_shape=[jax.ShapeDtypeStruct((S, D), F32)],
        out_specs=[pl.BlockSpec((tm, tn), lambda i, j, k: (i, j))], epilogue=_store_epilogue(F32))[0]


def _out_bwd(dh_b, w_out_g, proj, a_attn, a_sgu, dm):
    S, D = dh_b.shape
    OFF_G = dm["OFF_G"]
    tm = _pick(S, (1024, 512, 256, 128))
    tn = D // N_CHIPS

    def ep(parts, e_refs, o_refs):
        dm_ = parts[0]
        ga = _sigmoid(e_refs[0][...].astype(F32))
        gs = _sigmoid(e_refs[1][...].astype(F32))
        o_refs[0][...] = (dm_ * ga).astype(BF16)
        o_refs[1][...] = (dm_ * gs).astype(BF16)
        o_refs[2][0] = (dm_ * e_refs[2][...].astype(F32) * ga * (1.0 - ga)).astype(BF16)
        o_refs[2][1] = (dm_ * e_refs[3][...].astype(F32) * gs * (1.0 - gs)).astype(BF16)

    blk = pl.BlockSpec((tm, tn), lambda i, j, k: (i, j))
    return _matmul(
        "out_bwd", dh_b, [w_out_g], dims=NT, grid=(S // tm, D // tn, 1),
        lhs_spec=pl.BlockSpec((tm, D), lambda i, j, k: (i, 0)),
        rhs_specs=[pl.BlockSpec((tn, D), lambda i, j, k: (j, 0))],
        acc_shape=(tm, tn), extra=[proj, proj, a_attn, a_sgu],
        extra_specs=[pl.BlockSpec((tm, tn), lambda i, j, k: (i, OFF_G // tn + j)),
                     pl.BlockSpec((tm, tn), lambda i, j, k: (i, (OFF_G + D) // tn + j)), blk, blk],
        out_shape=[jax.ShapeDtypeStruct((S, D), BF16), jax.ShapeDtypeStruct((S, D), BF16),
                   jax.ShapeDtypeStruct((2, S, D), BF16)],
        out_specs=[blk, blk, pl.BlockSpec((2, tm, tn), lambda i, j, k: (0, i, j))], epilogue=ep)


def _colsharded_bwd(name, dy, w_g, out_dtype):
    S = dy.shape[0]
    _, K, cw = w_g.shape
    tm = _pick(S, (512, 256, 128))
    tn = _pick(K, (1024, 512, 256, 128))
    return _matmul(
        name, dy, [w_g], dims=NT, grid=(S // tm, K // tn, N_CHIPS),
        lhs_spec=pl.BlockSpec((tm, cw), lambda i, j, k: (i, k)),
        rhs_specs=[pl.BlockSpec((None, tn, cw), lambda i, j, k: (k, j, 0))],
        acc_shape=(tm, tn), out_shape=[jax.ShapeDtypeStruct((S, K), out_dtype)],
        out_specs=[pl.BlockSpec((tm, tn), lambda i, j, k: (i, j))], epilogue=_store_epilogue(out_dtype))[0]


def _wgrad_cols(name, x, dy):
    S, R = x.shape
    C = dy.shape[1]
    cw = C // N_CHIPS
    tm = _pick(R, (1024, 512, 256, 128))
    tk = _pick(S, (512, 256, 128))
    return _matmul(
        name, x, [dy], dims=TN, grid=(R // tm, N_CHIPS, S // tk),
        lhs_spec=pl.BlockSpec((tk, tm), lambda i, j, k: (k, i)),
        rhs_specs=[pl.BlockSpec((tk, cw), lambda i, j, k: (k, j))],
        acc_shape=(tm, cw), out_shape=[jax.ShapeDtypeStruct((N_CHIPS, R, cw), F32)],
        out_specs=[pl.BlockSpec((None, tm, cw), lambda i, j, k: (j, i, 0))], epilogue=_store_epilogue(F32))[0]


def _wgrad_gate_up(hn, dgu, dm):
    S, D = hn.shape
    Fd = dm["F"]
    cw = 2 * Fd // N_CHIPS
    half = N_CHIPS // 2
    tm = _pick(D, (512, 256, 128))
    tk = _pick(S, (512, 256, 128))
    return _matmul(
        "wgrad_gate_up", hn, [dgu], dims=TN, grid=(D // tm, N_CHIPS, S // tk),
        lhs_spec=pl.BlockSpec((tk, tm), lambda i, j, k: (k, i)),
        rhs_specs=[pl.BlockSpec((None, tk, cw), lambda i, j, k: (j // half, k, j % half))],
        acc_shape=(tm, cw), out_shape=[jax.ShapeDtypeStruct((N_CHIPS, D, cw), F32)],
        out_specs=[pl.BlockSpec((None, tm, cw), lambda i, j, k: (j, i, 0))], epilogue=_store_epilogue(F32))[0]


def _wgrad_rows(name, x, dy):
    S, R = x.shape
    C = dy.shape[1]
    rw = R // N_CHIPS
    tn = _pick(C, (1024, 512, 256, 128))
    tk = _pick(S, (512, 256, 128))
    return _matmul(
        name, x, [dy], dims=TN, grid=(N_CHIPS, C // tn, S // tk),
        lhs_spec=pl.BlockSpec((tk, rw), lambda i, j, k: (k, i)),
        rhs_specs=[pl.BlockSpec((tk, tn), lambda i, j, k: (k, j))],
        acc_shape=(rw, tn), out_shape=[jax.ShapeDtypeStruct((N_CHIPS, rw, C), F32)],
        out_specs=[pl.BlockSpec((None, rw, tn), lambda i, j, k: (i, 0, j))], epilogue=_store_epilogue(F32))[0]


def _place():
    x, y, c = lax.axis_index("x"), lax.axis_index("y"), lax.axis_index("c")
    others = [(1 - x, y), (x, 1 - y), (1 - x, 1 - y)]
    return x, y, c, others


def _chip_index(chip):
    return 2 * chip[0] + chip[1]


def _gather_weights(shards):
    n = len(shards)
    shapes = [s.shape for s in shards]

    def body(*refs):
        src, out = refs[:n], refs[n:2 * n]
        send_sems, recv_sems, local_sems = refs[2 * n:]
        x, y, c, others = _place()
        me, sibling = (x, y, c), (x, y, 1 - c)
        mine = _chip_index((x, y))

        def half(t, chip_idx, hc):
            r2 = shapes[t][0] // 2
            return out[t].at[chip_idx, pl.ds(hc * r2, r2), :]

        def copy(t, k, chip_idx, hc, to, src_ref=None):
            dst = half(t, chip_idx, hc)
            return pltpu.make_async_remote_copy(
                src_ref=dst if src_ref is None else src_ref, dst_ref=dst,
                send_sem=send_sems.at[6 * t + k], recv_sem=recv_sems.at[6 * t + k],
                device_id=to, device_id_type=MESH)

        local = [pltpu.make_async_copy(src[t], out[t].at[mine], local_sems.at[t]) for t in range(n)]
        for cp in local:
            cp.start()
        started = []
        for t in range(n):
            r2 = shapes[t][0] // 2
            for j, chip in enumerate(others):
                cp = copy(t, j, mine, c, (*chip, c), src_ref=src[t].at[pl.ds(c * r2, r2), :])
                cp.start()
                started.append(cp)
        for t in range(n):
            for j, chip in enumerate(others):
                copy(t, j, _chip_index(chip), c, me).wait_recv()
                cp = copy(t, 3 + j, _chip_index(chip), c, sibling)
                cp.start()
                started.append(cp)
        for t in range(n):
            for j, chip in enumerate(others):
                copy(t, 3 + j, _chip_index(chip), 1 - c, me).wait_recv()
        for cp in started:
            cp.wait_send()
        for cp in local:
            cp.wait()

    return pl.pallas_call(
        body, name="gather_weights",
        in_specs=[ANY] * n, out_specs=[ANY] * n,
        out_shape=[jax.ShapeDtypeStruct((N_CHIPS, *s), BF16) for s in shapes],
        scratch_shapes=[pltpu.SemaphoreType.DMA((6 * n,)), pltpu.SemaphoreType.DMA((6 * n,)),
                        pltpu.SemaphoreType.DMA((n,))],
    )(*shards)


def _sibling_exchange(grads):
    n = len(grads)
    shapes = [g.shape for g in grads]

    def body(*refs):
        src, keep, land = refs[:n], refs[n:2 * n], refs[2 * n:3 * n]
        send_sems, recv_sems, local_sems = refs[3 * n:]
        x, y, c, _ = _place()
        sibling = (x, y, 1 - c)
        local, remote = [], []
        for t in range(n):
            r2 = shapes[t][1] // 2
            local.append(pltpu.make_async_copy(src[t].at[:, pl.ds(c * r2, r2), :], keep[t], local_sems.at[t]))
            remote.append(pltpu.make_async_remote_copy(
                src_ref=src[t].at[:, pl.ds((1 - c) * r2, r2), :], dst_ref=land[t],
                send_sem=send_sems.at[t], recv_sem=recv_sems.at[t], device_id=sibling, device_id_type=MESH))
        for cp in remote + local:
            cp.start()
        for cp in remote:
            cp.wait_recv()
        for cp in remote:
            cp.wait_send()
        for cp in local:
            cp.wait()

    halves = [jax.ShapeDtypeStruct((s[0], s[1] // 2, s[2]), F32) for s in shapes]
    outs = pl.pallas_call(
        body, name="sibling_exchange",
        in_specs=[ANY] * n, out_specs=[ANY] * (2 * n), out_shape=halves + halves,
        scratch_shapes=[pltpu.SemaphoreType.DMA((n,)), pltpu.SemaphoreType.DMA((n,)), pltpu.SemaphoreType.DMA((n,))],
    )(*grads)
    return outs[:n], outs[n:]


def _chip_exchange(sends, sums):
    n = len(sends)
    shapes = [s.shape for s in sends]

    def body(*refs):
        snd, sm, got, own = refs[:n], refs[n:2 * n], refs[2 * n:3 * n], refs[3 * n:4 * n]
        send_sems, recv_sems, local_sems = refs[4 * n:]
        x, y, c, others = _place()
        mine = _chip_index((x, y))
        local, remote = [], []
        for t in range(n):
            local.append(pltpu.make_async_copy(sm[t].at[mine], own[t], local_sems.at[t]))
            for j, chip in enumerate(others):
                remote.append(pltpu.make_async_remote_copy(
                    src_ref=snd[t].at[_chip_index(chip)], dst_ref=got[t].at[j],
                    send_sem=send_sems.at[3 * t + j], recv_sem=recv_sems.at[3 * t + j],
                    device_id=(*chip, c), device_id_type=MESH))
        for cp in remote + local:
            cp.start()
        for cp in remote:
            cp.wait_recv()
        for cp in remote:
            cp.wait_send()
        for cp in local:
            cp.wait()

    outs = pl.pallas_call(
        body, name="chip_exchange",
        in_specs=[ANY] * (2 * n), out_specs=[ANY] * (2 * n),
        out_shape=[jax.ShapeDtypeStruct((3, s[1], s[2]), BF16) for s in shapes]
        + [jax.ShapeDtypeStruct((s[1], s[2]), F32) for s in shapes],
        scratch_shapes=[pltpu.SemaphoreType.DMA((3 * n,)), pltpu.SemaphoreType.DMA((3 * n,)),
                        pltpu.SemaphoreType.DMA((n,))],
    )(*sends, *sums)
    return outs[:n], outs[n:]


def _sibling_share(halves):
    n = len(halves)
    shapes = [h.shape for h in halves]

    def body(*refs):
        src, out = refs[:n], refs[n:2 * n]
        send_sems, recv_sems, local_sems = refs[2 * n:]
        x, y, c, _ = _place()
        sibling = (x, y, 1 - c)
        local, remote = [], []
        for t in range(n):
            r2 = shapes[t][0]
            dst = out[t].at[pl.ds(c * r2, r2), :]
            local.append(pltpu.make_async_copy(src[t], dst, local_sems.at[t]))
            remote.append(pltpu.make_async_remote_copy(
                src_ref=src[t], dst_ref=dst, send_sem=send_sems.at[t], recv_sem=recv_sems.at[t],
                device_id=sibling, device_id_type=MESH))
        for cp in remote + local:
            cp.start()
        for t in range(n):
            r2 = shapes[t][0]
            theirs = out[t].at[pl.ds((1 - c) * r2, r2), :]
            pltpu.make_async_remote_copy(
                src_ref=theirs, dst_ref=theirs, send_sem=send_sems.at[t], recv_sem=recv_sems.at[t],
                device_id=sibling, device_id_type=MESH).wait_recv()
        for cp in remote:
            cp.wait_send()
        for cp in local:
            cp.wait()

    return pl.pallas_call(
        body, name="sibling_share",
        in_specs=[ANY] * n, out_specs=[ANY] * n,
        out_shape=[jax.ShapeDtypeStruct((2 * s[0], s[1]), F32) for s in shapes],
        scratch_shapes=[pltpu.SemaphoreType.DMA((n,)), pltpu.SemaphoreType.DMA((n,)), pltpu.SemaphoreType.DMA((n,))],
    )(*halves)


def _gather_all(v):
    R, C = v.shape

    def body(v_ref, out_ref, send_sems, recv_sems, local_sem):
        x, y, c, others = _place()
        me, sibling = (x, y, c), (x, y, 1 - c)

        def rows(px, py, pc):
            return out_ref.at[4 * px + 2 * py + pc]

        def copy(k, block, to, src=None):
            return pltpu.make_async_remote_copy(
                src_ref=rows(*block) if src is None else src, dst_ref=rows(*block),
                send_sem=send_sems.at[k], recv_sem=recv_sems.at[k], device_id=to, device_id_type=MESH)

        mine = pltpu.make_async_copy(v_ref, rows(*me), local_sem)
        mine.start()
        first = [copy(0, me, sibling, src=v_ref)]
        first += [copy(1 + j, me, (*chip, c), src=v_ref) for j, chip in enumerate(others)]
        for cp in first:
            cp.start()
        passed = [copy(4 + j, (*chip, c), sibling) for j, chip in enumerate(others)]
        for j, chip in enumerate(others):
            copy(1 + j, (*chip, c), me).wait_recv()
            passed[j].start()
        copy(0, sibling, me).wait_recv()
        for j, chip in enumerate(others):
            copy(4 + j, (*chip, 1 - c), me).wait_recv()
        for cp in first + passed:
            cp.wait_send()
        mine.wait()

    return pl.pallas_call(
        body, name="gather_all", in_specs=[ANY], out_specs=ANY,
        out_shape=jax.ShapeDtypeStruct((8, R, C), F32),
        scratch_shapes=[pltpu.SemaphoreType.DMA((7,)), pltpu.SemaphoreType.DMA((7,)), pltpu.SemaphoreType.DMA],
    )(v)


def _flat_tiles(shape):
    rows = math.prod(shape[:-1])
    return rows, shape[-1], _row_tile(rows, shape[-1])


def _pair_sum(keep, land):
    rows, C, tm = _flat_tiles(keep.shape)

    def body(a_ref, b_ref, s_ref, sb_ref):
        s = a_ref[...] + b_ref[...]
        s_ref[...] = s
        sb_ref[...] = s.astype(BF16)

    blk = pl.BlockSpec((tm, C), lambda i: (i, 0))
    s, sb = pl.pallas_call(
        body, name="pair_sum", grid=(rows // tm,), in_specs=[blk, blk], out_specs=[blk, blk],
        out_shape=[jax.ShapeDtypeStruct((rows, C), F32), jax.ShapeDtypeStruct((rows, C), BF16)],
        compiler_params=_params(("parallel",)),
    )(keep.reshape(rows, C), land.reshape(rows, C))
    return s.reshape(keep.shape), sb.reshape(keep.shape)


def _chip_sum(own, got):
    R2, C = own.shape
    tm = _row_tile(R2, C)

    def body(o_ref, g_ref, s_ref):
        s_ref[...] = ((o_ref[...] + g_ref[0].astype(F32)) + g_ref[1].astype(F32)) + g_ref[2].astype(F32)

    return pl.pallas_call(
        body, name="chip_sum", grid=(R2 // tm,),
        in_specs=[pl.BlockSpec((tm, C), lambda i: (i, 0)), pl.BlockSpec((3, tm, C), lambda i: (0, i, 0))],
        out_specs=pl.BlockSpec((tm, C), lambda i: (i, 0)),
        out_shape=jax.ShapeDtypeStruct((R2, C), F32),
        compiler_params=_params(("parallel",)),
    )(own, got)


def _adamw_math(w, g, m, v):
    m = ADAM_B1 * m + (1.0 - ADAM_B1) * g
    v = ADAM_B2 * v + (1.0 - ADAM_B2) * (g * g)
    m_hat = m / (1.0 - ADAM_B1 ** ADAM_STEP)
    v_hat = v / (1.0 - ADAM_B2 ** ADAM_STEP)
    delta = -ADAM_LR * (m_hat / (jnp.sqrt(v_hat) + ADAM_EPS) + ADAM_WD * w)
    return delta, m, v


def _adamw_stacked(grads, w, m, v):
    L, R, C = w.shape
    tm = _row_tile(R, C)
    nrb = R // tm

    def body(*refs):
        g_refs = refs[:L]
        w_ref, m_ref, v_ref, go_ref, d_ref, mo_ref, vo_ref = refs[L:]
        l = pl.program_id(0)
        for ll in range(L):
            @pl.when(l == ll)
            def _(ll=ll):
                g = g_refs[ll][...]
                delta, mn, vn = _adamw_math(w_ref[...], g, m_ref[...], v_ref[...])
                go_ref[...] = g
                d_ref[...] = delta
                mo_ref[...] = mn
                vo_ref[...] = vn

    stacked = pl.BlockSpec((None, tm, C), lambda l, r: (l, r, 0))
    g_specs = [pl.BlockSpec((tm, C), lambda l, r, ll=ll: (jnp.where(l == ll, r, 0), 0)) for ll in range(L)]
    shp = jax.ShapeDtypeStruct((L, R, C), F32)
    return pl.pallas_call(
        body, name="adamw", grid=(L, nrb),
        in_specs=[*g_specs, stacked, stacked, stacked], out_specs=[stacked] * 4, out_shape=[shp] * 4,
        compiler_params=_params(("arbitrary", "arbitrary")),
    )(*grads, w, m, v)


def _adamw_small(parts, w, m, v):
    _, R, C = parts.shape
    tm = _row_tile(R, 8 * C)

    def body(p_ref, w_ref, m_ref, v_ref, go_ref, d_ref, mo_ref, vo_ref):
        g = p_ref[0]
        for k in range(1, 8):
            g = g + p_ref[k]
        delta, mn, vn = _adamw_math(w_ref[...], g, m_ref[...], v_ref[...])
        go_ref[...] = g
        d_ref[...] = delta
        mo_ref[...] = mn
        vo_ref[...] = vn

    blk = pl.BlockSpec((tm, C), lambda i: (i, 0))
    shp = jax.ShapeDtypeStruct((R, C), F32)
    return pl.pallas_call(
        body, name="adamw_small", grid=(R // tm,),
        in_specs=[pl.BlockSpec((8, tm, C), lambda i: (0, i, 0)), blk, blk, blk],
        out_specs=[blk] * 4, out_shape=[shp] * 4,
        compiler_params=_params(("parallel",)),
    )(parts, w, m, v)


def _cast_bf16(w):
    shape = w.shape
    rows, C, tm = _flat_tiles(shape)

    def body(w_ref, o_ref):
        o_ref[...] = w_ref[...].astype(BF16)

    blk = pl.BlockSpec((tm, C), lambda i: (i, 0))
    return pl.pallas_call(
        body, name="cast_bf16", grid=(rows // tm,), in_specs=[blk], out_specs=blk,
        out_shape=jax.ShapeDtypeStruct((rows, C), BF16),
        compiler_params=_params(("parallel",)),
    )(w.reshape(rows, C)).reshape(shape)


def _trig_tables(positions):
    half = ROPE_DIM // 2
    inv_freq = ROPE_THETA ** (-jnp.arange(0, ROPE_DIM, 2, dtype=F32) / ROPE_DIM)
    ang = positions.astype(F32)[:, None] * inv_freq
    cos, sin = jnp.cos(ang), jnp.sin(ang)
    S = positions.shape[0]
    zeros = lambda w: jnp.zeros((S, w), F32)
    cos_h = jnp.concatenate([cos, cos, jnp.ones((S, HEAD_DIM - ROPE_DIM), F32)], axis=1)
    sa_h = jnp.concatenate([-sin, zeros(HEAD_DIM - half)], axis=1)
    sb_h = jnp.concatenate([zeros(half), sin, zeros(HEAD_DIM - ROPE_DIM)], axis=1)
    rep = LANES // HEAD_DIM
    return [jnp.tile(t, (1, rep)) for t in (cos_h, sa_h, sb_h)]


def _row(vec):
    return vec.reshape(1, -1)


def _lane_row(vec):
    return jnp.zeros((8, LANES), F32).at[0, :vec.shape[0]].set(vec)


def _pack(pieces, rows):
    flat = jnp.concatenate([p.reshape(-1).astype(F32) for p in pieces])
    return jnp.pad(flat, (0, rows * LANES - flat.shape[0])).reshape(rows, LANES)


def kernel(x, positions, norm1_g, w_in, b_in, sinks, sgu_ln_g, sgu_ln_b, sgu_w, sgu_b, w_attn_branch, w_sgu_branch, w_out, norm2_g, w_gate_up, w_down, final_g, loss_target, m_norm1_g, m_w_in, m_b_in, m_sinks, m_sgu_ln_g, m_sgu_ln_b, m_sgu_w, m_sgu_b, m_w_attn_branch, m_w_sgu_branch, m_w_out, m_norm2_g, m_w_gate_up, m_w_down, m_final_g, v_norm1_g, v_w_in, v_b_in, v_sinks, v_sgu_ln_g, v_sgu_ln_b, v_sgu_w, v_sgu_b, v_w_attn_branch, v_w_sgu_branch, v_w_out, v_norm2_g, v_w_gate_up, v_w_down, v_final_g):
    L = norm1_g.shape[0]
    S, D = x.shape[1], x.shape[2]
    NQ = sinks.shape[1]
    A = NQ * HEAD_DIM
    KV = N_KV_HEADS * HEAD_DIM
    G = sgu_ln_g.shape[1]
    NG = sgu_w.shape[1]
    IN = b_in.shape[1]
    Fd = w_down.shape[1] * N_CHIPS
    dm = dict(D=D, A=A, KV=KV, NQ=NQ, G=G, NG=NG, IN=IN, F=Fd,
              OFF_K=A, OFF_V=A + KV, OFF_Z=A + 2 * KV, OFF_G=A + 2 * KV + 2 * G)
    assert sgu_w.shape[2] == WINDOW and G == NG * LANES and IN == dm["OFF_G"] + 2 * D

    h = x[0]
    target = loss_target[0]
    trig = _trig_tables(positions[0])
    tril = jnp.tril(jnp.ones((WINDOW, WINDOW), bool))

    big = [w_in, w_attn_branch, w_sgu_branch, w_out, w_gate_up, w_down]
    big_m = [m_w_in, m_w_attn_branch, m_w_sgu_branch, m_w_out, m_w_gate_up, m_w_down]
    big_v = [v_w_in, v_w_attn_branch, v_w_sgu_branch, v_w_out, v_w_gate_up, v_w_down]

    big_b = [_cast_bf16(w) for w in big]
    gathered = [_gather_weights([wb[l] for wb in big_b]) for l in range(L)]

    def weights(l):
        w_in_g, w_ab_g, w_sb_g, w_out_g, w_gu_g, w_down_g = gathered[l]
        return (w_in_g, w_ab_g, w_sb_g, w_out_g.reshape(D, D), w_gu_g, w_down_g.reshape(Fd, D))

    def small(l):
        return dict(
            g1=_row(norm1_g[l]), b_in=_row(b_in[l]), sink=_lane_row(sinks[l]),
            ln_g=_row(sgu_ln_g[l]), ln_b=_row(sgu_ln_b[l]),
            w_tril=jnp.where(tril[None], sgu_w[l], 0.0).astype(BF16),
            b_t=jnp.zeros((WINDOW, LANES), F32).at[:, :NG].set(sgu_b[l].T),
            g2=_row(norm2_g[l]))

    saved = []
    for l in range(L):
        w_in_g, w_ab_g, w_sb_g, w_out_g, w_gu_g, w_down_g = weights(l)
        sp = small(l)
        xn = _rms_fwd(h, sp["g1"])
        proj = _in_proj(xn, w_in_g, sp["b_in"], dm)
        y_attn, lse = _attn_fwd(proj, trig, sp["sink"], dm)
        y_sgu = _sgu_fwd(proj, sp["w_tril"], sp["b_t"], sp["ln_g"], sp["ln_b"], dm)
        a_attn = _branch_attn(y_attn, w_ab_g, dm)
        a_sgu, merged = _branch_sgu_merge(y_sgu, w_sb_g, a_attn, proj, dm)
        h_mid = _residual_matmul("out_proj", merged, w_out_g, h)
        hn = _rms_fwd(h_mid, sp["g2"])
        gu, act = _gate_up(hn, w_gu_g, dm)
        h_out = _residual_matmul("down_proj", act, w_down_g, h_mid)
        saved.append(dict(h=h, xn=xn, proj=proj, y_attn=y_attn, lse=lse, y_sgu=y_sgu, a_attn=a_attn, a_sgu=a_sgu,
                          merged=merged, h_mid=h_mid, hn=hn, gu=gu, act=act))
        h = h_out

    dh, dh_b, d_final, loss_part = _loss_head(h, _row(final_g), target)

    big_grads = [None] * L
    small_grads = [None] * L
    for l in reversed(range(L)):
        w_in_g, w_ab_g, w_sb_g, w_out_g, w_gu_g, w_down_g = weights(l)
        sp, sv = small(l), saved[l]
        dgu = _down_bwd(dh_b, w_down_g, sv["gu"], dm)
        g_down = _wgrad_rows("wgrad_down", sv["act"], dh_b)
        dhn = _gate_up_bwd(dgu, w_gu_g, dm)
        g_gu = _wgrad_gate_up(sv["hn"], dgu, dm)
        dh_mid, dh_mid_b, d_g2 = _rms_bwd(dhn, sv["h_mid"], sp["g2"], dh)
        da_attn, da_sgu, dgate = _out_bwd(dh_mid_b, w_out_g, sv["proj"], sv["a_attn"], sv["a_sgu"], dm)
        g_out = _wgrad_rows("wgrad_out", sv["merged"], dh_mid_b)
        dy_attn = _colsharded_bwd("branch_attn_bwd", da_attn, w_ab_g, BF16)
        dy_sgu = _colsharded_bwd("branch_sgu_bwd", da_sgu, w_sb_g, BF16)
        g_ab = _wgrad_cols("wgrad_attn_branch", sv["y_attn"], da_attn)
        g_sb = _wgrad_cols("wgrad_sgu_branch", sv["y_sgu"], da_sgu)
        dq, dk, dv, d_sink = _attn_bwd(sv["proj"], trig, sp["sink"], sv["y_attn"], sv["lse"], dy_attn, dm)
        dz, d_sgu_w, d_bt, d_lng, d_lnb = _sgu_bwd(sv["proj"], sp["w_tril"], sp["b_t"], sp["ln_g"], sp["ln_b"], dy_sgu, dm)
        dproj = jnp.concatenate([dq, dk, dv, dz, dgate[0], dgate[1]], axis=1)
        d_bin = _colsum(dproj)
        dxn = _colsharded_bwd("in_proj_bwd", dproj, w_in_g, F32)
        g_in = _wgrad_cols("wgrad_in", sv["xn"], dproj)
        dh, dh_b, d_g1 = _rms_bwd(dxn, sv["h"], sp["g1"], dh_mid)
        big_grads[l] = [g_in, g_ab, g_sb, g_out, g_gu, g_down]
        small_grads[l] = dict(norm1_g=d_g1[0], b_in=d_bin[0], sinks=d_sink[0, :NQ], sgu_ln_g=d_lng[0], sgu_ln_b=d_lnb[0],
                              sgu_w=d_sgu_w, sgu_b=d_bt[:, :NG].T, norm2_g=d_g2[0])
    grad_x = dh[None]

    reduced = [None] * L
    for l in range(L):
        keep, land = _sibling_exchange(big_grads[l])
        sums, sends = zip(*[_pair_sum(k, d) for k, d in zip(keep, land)])
        got, own = _chip_exchange(list(sends), list(sums))
        halves = [_chip_sum(o, g) for o, g in zip(own, got)]
        reduced[l] = _sibling_share(halves)
    big_out = [_adamw_stacked([reduced[l][t] for l in range(L)], big[t], big_m[t], big_v[t]) for t in range(len(big))]

    names = ["norm1_g", "b_in", "sinks", "sgu_ln_g", "sgu_ln_b", "sgu_w", "sgu_b", "norm2_g"]
    small_w = [norm1_g, b_in, sinks, sgu_ln_g, sgu_ln_b, sgu_w, sgu_b, norm2_g, final_g]
    small_m = [m_norm1_g, m_b_in, m_sinks, m_sgu_ln_g, m_sgu_ln_b, m_sgu_w, m_sgu_b, m_norm2_g, m_final_g]
    small_v = [v_norm1_g, v_b_in, v_sinks, v_sgu_ln_g, v_sgu_ln_b, v_sgu_w, v_sgu_b, v_norm2_g, v_final_g]
    small_g = [jnp.stack([small_grads[l][nm] for l in range(L)]) for nm in names] + [d_final[0]]
    sizes = [w.size for w in small_w]
    total = sum(sizes) + 1
    rows = -(-total // (8 * LANES)) * 8
    loss_piece = jnp.sum(loss_part[0]).reshape(1)
    packed_g = _pack(small_g + [loss_piece], rows)
    one = jnp.ones((1,), F32)
    parts = _gather_all(packed_g)
    outs = _adamw_small(parts, _pack(small_w + [one], rows), _pack(small_m + [one], rows), _pack(small_v + [one], rows))

    def unpack(p):
        flat = p.reshape(-1)
        res, off = [], 0
        for w, n in zip(small_w, sizes):
            res.append(flat[off:off + n].reshape(w.shape))
            off += n
        return res, flat[off]

    (sg, loss), (sd, _), (smm, _), (svv, _) = [unpack(o) for o in outs]

    order = ["norm1_g", "w_in", "b_in", "sinks", "sgu_ln_g", "sgu_ln_b", "sgu_w", "sgu_b", "w_attn_branch",
             "w_sgu_branch", "w_out", "norm2_g", "w_gate_up", "w_down", "final_g"]
    big_names = ["w_in", "w_attn_branch", "w_sgu_branch", "w_out", "w_gate_up", "w_down"]
    small_names = names + ["final_g"]

    def collect(kind):
        res = []
        for nm in order:
            if nm in big_names:
                res.append(big_out[big_names.index(nm)][kind])
            else:
                res.append((sg, sd, smm, svv)[kind][small_names.index(nm)])
        return res

    return (loss, grad_x, *collect(0), *collect(1), *collect(2), *collect(3))
```

```python
import math

import jax
import jax.numpy as jnp
from jax import lax
from jax.experimental import pallas as pl
from jax.experimental.pallas import tpu as pltpu

F32 = jnp.float32
BF16 = jnp.bfloat16
MESH = pl.DeviceIdType.MESH
ANY = pl.BlockSpec(memory_space=pl.ANY)

HEAD_DIM = 64
N_KV_HEADS = 4
WINDOW = 128
ROPE_DIM = HEAD_DIM // 4
ROPE_THETA = 500000.0
EPS = 1e-5
NEG = -1e30
N_CHIPS = 4
LANES = 128
V7X_VMEM_LIMIT = 56 * 1024 * 1024

ADAM_LR = 0.001
ADAM_B1 = 0.9
ADAM_B2 = 0.999
ADAM_EPS = 1e-08
ADAM_WD = 0.01
ADAM_STEP = 10

NN = (((1,), (0,)), ((), ()))
NT = (((1,), (1,)), ((), ()))
TN = (((0,), (0,)), ((), ()))


ROW_TILES = (1024, 512, 256, 128, 64, 32, 16, 8)
BLOCK_BYTES = 2 * 1024 * 1024


def _pick(n, prefs):
    for p in prefs:
        if n % p == 0:
            return p
    raise ValueError(f"no tile for {n} among {prefs}")


def _row_tile(rows, cols, itemsize=4):
    return _pick(rows, [t for t in ROW_TILES if t * cols * itemsize <= BLOCK_BYTES or t == ROW_TILES[-1]])


def _dot(a, b, dims):
    return lax.dot_general(a, b, dims, preferred_element_type=F32)


def _sigmoid(x):
    return 1.0 / (1.0 + jnp.exp(-x))


def _gelu(x):
    return 0.5 * x * (1.0 + lax.erf(x * (1.0 / math.sqrt(2.0))))


def _gelu_grad(x):
    return 0.5 * (1.0 + lax.erf(x * (1.0 / math.sqrt(2.0)))) + x * jnp.exp(-0.5 * x * x) * (1.0 / math.sqrt(2.0 * math.pi))


def _params(sem):
    return pltpu.CompilerParams(dimension_semantics=sem, vmem_limit_bytes=V7X_VMEM_LIMIT)


def _matmul(name, lhs, rhs_list, *, dims, grid, lhs_spec, rhs_specs, acc_shape, out_shape, out_specs,
            epilogue, extra=(), extra_specs=()):
    gk = grid[2]
    nr, ne, no = len(rhs_list), len(extra), len(out_shape)

    def body(*refs):
        a_ref = refs[0]
        b_refs = refs[1:1 + nr]
        e_refs = refs[1 + nr:1 + nr + ne]
        o_refs = refs[1 + nr + ne:1 + nr + ne + no]
        acc_refs = refs[1 + nr + ne + no:]
        a = a_ref[...]
        parts = [_dot(a, b[...], dims) for b in b_refs]
        if gk == 1:
            epilogue(parts, e_refs, o_refs)
            return
        k = pl.program_id(2)

        @pl.when(k == 0)
        def _():
            for acc, p in zip(acc_refs, parts):
                acc[...] = p

        @pl.when(k > 0)
        def _():
            for acc, p in zip(acc_refs, parts):
                acc[...] += p

        @pl.when(k == gk - 1)
        def _():
            epilogue([acc[...] for acc in acc_refs], e_refs, o_refs)

    scratch = [pltpu.VMEM(acc_shape, F32) for _ in range(nr)] if gk > 1 else []
    return pl.pallas_call(
        body, name=name, grid=grid,
        in_specs=[lhs_spec, *rhs_specs, *extra_specs],
        out_specs=out_specs, out_shape=out_shape, scratch_shapes=scratch,
        compiler_params=_params(("parallel", "parallel", "arbitrary")),
    )(lhs, *rhs_list, *extra)


def _store_epilogue(dtype):
    def ep(parts, e_refs, o_refs):
        o_refs[0][...] = parts[0].astype(dtype)
    return ep


def _rms_fwd(h, g_row):
    S, D = h.shape
    tm = _row_tile(S, D)

    def body(h_ref, g_ref, o_ref):
        x = h_ref[...]
        r = lax.rsqrt(jnp.mean(x * x, axis=-1, keepdims=True) + EPS)
        o_ref[...] = (x * r * g_ref[...]).astype(BF16)

    return pl.pallas_call(
        body, name="rms_fwd", grid=(S // tm,),
        in_specs=[pl.BlockSpec((tm, D), lambda i: (i, 0)), pl.BlockSpec((1, D), lambda i: (0, 0))],
        out_specs=pl.BlockSpec((tm, D), lambda i: (i, 0)),
        out_shape=jax.ShapeDtypeStruct((S, D), BF16),
        compiler_params=_params(("parallel",)),
    )(h, g_row)


def _rms_bwd(dy, h, g_row, dres):
    S, D = h.shape
    tm = _row_tile(S, D)

    def body(dy_ref, h_ref, g_ref, dres_ref, dh_ref, dhb_ref, dg_ref):
        i = pl.program_id(0)
        x = h_ref[...]
        d = dy_ref[...]
        r = lax.rsqrt(jnp.mean(x * x, axis=-1, keepdims=True) + EPS)
        dg = d * g_ref[...]
        dot = jnp.mean(dg * x, axis=-1, keepdims=True)
        dh = dres_ref[...] + r * dg - x * (r * r * r) * dot
        dh_ref[...] = dh
        dhb_ref[...] = dh.astype(BF16)
        part = jnp.sum(d * x * r, axis=0, keepdims=True)

        @pl.when(i == 0)
        def _():
            dg_ref[...] = jnp.zeros_like(dg_ref)

        dg_ref[0:1, :] += part

    return pl.pallas_call(
        body, name="rms_bwd", grid=(S // tm,),
        in_specs=[pl.BlockSpec((tm, D), lambda i: (i, 0)), pl.BlockSpec((tm, D), lambda i: (i, 0)),
                  pl.BlockSpec((1, D), lambda i: (0, 0)), pl.BlockSpec((tm, D), lambda i: (i, 0))],
        out_specs=[pl.BlockSpec((tm, D), lambda i: (i, 0)), pl.BlockSpec((tm, D), lambda i: (i, 0)),
                   pl.BlockSpec((8, D), lambda i: (0, 0))],
        out_shape=[jax.ShapeDtypeStruct((S, D), F32), jax.ShapeDtypeStruct((S, D), BF16),
                   jax.ShapeDtypeStruct((8, D), F32)],
        compiler_params=_params(("arbitrary",)),
    )(dy, h, g_row, dres)


def _loss_head(h, g_row, target):
    S, D = h.shape
    tm = _row_tile(S, D)

    def body(h_ref, g_ref, t_ref, dh_ref, dhb_ref, dg_ref, loss_ref):
        i = pl.program_id(0)
        x = h_ref[...]
        g = g_ref[...]
        r = lax.rsqrt(jnp.mean(x * x, axis=-1, keepdims=True) + EPS)
        y = x * r * g
        e = y - t_ref[...]
        d = e * (1.0 / D)
        dg = d * g
        dot = jnp.mean(dg * x, axis=-1, keepdims=True)
        dh = r * dg - x * (r * r * r) * dot
        dh_ref[...] = dh
        dhb_ref[...] = dh.astype(BF16)

        @pl.when(i == 0)
        def _():
            dg_ref[...] = jnp.zeros_like(dg_ref)
            loss_ref[...] = jnp.zeros_like(loss_ref)

        dg_ref[0:1, :] += jnp.sum(d * x * r, axis=0, keepdims=True)
        loss_ref[0:1, :] += jnp.sum((0.5 / D) * e * e, axis=0, keepdims=True)

    return pl.pallas_call(
        body, name="loss_head", grid=(S // tm,),
        in_specs=[pl.BlockSpec((tm, D), lambda i: (i, 0)), pl.BlockSpec((1, D), lambda i: (0, 0)),
                  pl.BlockSpec((tm, D), lambda i: (i, 0))],
        out_specs=[pl.BlockSpec((tm, D), lambda i: (i, 0)), pl.BlockSpec((tm, D), lambda i: (i, 0)),
                   pl.BlockSpec((8, D), lambda i: (0, 0)), pl.BlockSpec((8, D), lambda i: (0, 0))],
        out_shape=[jax.ShapeDtypeStruct((S, D), F32), jax.ShapeDtypeStruct((S, D), BF16),
                   jax.ShapeDtypeStruct((8, D), F32), jax.ShapeDtypeStruct((8, D), F32)],
        compiler_params=_params(("arbitrary",)),
    )(h, g_row, target)


def _colsum(x):
    S, C = x.shape
    tm = _row_tile(S, C)

    def body(x_ref, o_ref):
        i = pl.program_id(0)

        @pl.when(i == 0)
        def _():
            o_ref[...] = jnp.zeros_like(o_ref)

        o_ref[0:1, :] += jnp.sum(x_ref[...].astype(F32), axis=0, keepdims=True)

    return pl.pallas_call(
        body, name="colsum", grid=(S // tm,),
        in_specs=[pl.BlockSpec((tm, C), lambda i: (i, 0))],
        out_specs=pl.BlockSpec((8, C), lambda i: (0, 0)),
        out_shape=jax.ShapeDtypeStruct((8, C), F32),
        compiler_params=_params(("arbitrary",)),
    )(x)


def _rope(t, cos, sa, sb):
    w = t.shape[-1]
    return t * cos + pltpu.roll(t, w - 8, 1) * sa + pltpu.roll(t, 8, 1) * sb


def _rope_t(g, cos, sa, sb):
    w = g.shape[-1]
    return g * cos + pltpu.roll(g * sa, 8, 1) + pltpu.roll(g * sb, w - 8, 1)


def _band_mask(n, qpk):
    qi = lax.broadcasted_iota(jnp.int32, (qpk * WINDOW, 2 * WINDOW), 0) & (WINDOW - 1)
    kj = lax.broadcasted_iota(jnp.int32, (qpk * WINDOW, 2 * WINDOW), 1)
    rel = qi + WINDOW - kj
    ok = (rel >= 0) & (rel < WINDOW)
    return ok & ((kj >= WINDOW) | (n > 0))


def _stack_heads(x, g, qpk):
    return jnp.concatenate([x[:, (g * qpk + hh) * HEAD_DIM:(g * qpk + hh + 1) * HEAD_DIM] for hh in range(qpk)], axis=0)


def _stack_cols(row, g, qpk):
    return jnp.concatenate([row[:, g * qpk + hh:g * qpk + hh + 1] for hh in range(qpk)], axis=0)


def _attn_specs(dm, nb):
    A, KV = dm["A"], dm["KV"]
    kb, vb = dm["OFF_K"] // KV, dm["OFF_V"] // KV
    cur = lambda n: jnp.minimum(n, nb - 1)
    prev = lambda n: jnp.maximum(jnp.minimum(n, nb - 1) - 1, 0)
    proj_specs = [
        pl.BlockSpec((WINDOW, A), lambda n: (cur(n), 0)),
        pl.BlockSpec((WINDOW, KV), lambda n: (prev(n), kb)),
        pl.BlockSpec((WINDOW, KV), lambda n: (cur(n), kb)),
        pl.BlockSpec((WINDOW, KV), lambda n: (prev(n), vb)),
        pl.BlockSpec((WINDOW, KV), lambda n: (cur(n), vb)),
    ]
    trig_cur = [pl.BlockSpec((WINDOW, LANES), lambda n: (cur(n), 0)) for _ in range(3)]
    trig_prev = [pl.BlockSpec((WINDOW, LANES), lambda n: (prev(n), 0)) for _ in range(3)]
    return proj_specs, trig_cur, trig_prev, cur, prev


def _attn_fwd(proj, trig, sink_row, dm):
    S = proj.shape[0]
    A, KV, NQ = dm["A"], dm["KV"], dm["NQ"]
    qpk = NQ // N_KV_HEADS
    nb = S // WINDOW
    scale = HEAD_DIM ** -0.5
    proj_specs, trig_cur, trig_prev, cur, _ = _attn_specs(dm, nb)

    def body(q_ref, kp_ref, kc_ref, vp_ref, vc_ref, cc_ref, sac_ref, sbc_ref, cp_ref, sap_ref, sbp_ref,
             sink_ref, y_ref, lse_ref):
        n = pl.program_id(0)
        tq = lambda r: jnp.tile(r[...], (1, A // LANES))
        tk = lambda rp, rc: jnp.tile(jnp.concatenate([rp[...], rc[...]], axis=0), (1, KV // LANES))
        qr = _rope(q_ref[...].astype(F32), tq(cc_ref), tq(sac_ref), tq(sbc_ref)).astype(BF16)
        kband = jnp.concatenate([kp_ref[...], kc_ref[...]], axis=0).astype(F32)
        kr = _rope(kband, tk(cp_ref, cc_ref), tk(sap_ref, sac_ref), tk(sbp_ref, sbc_ref)).astype(BF16)
        vband = jnp.concatenate([vp_ref[...], vc_ref[...]], axis=0)
        mask = _band_mask(n, qpk)
        lane = lax.broadcasted_iota(jnp.int32, (WINDOW, LANES), 1)
        lse_all = jnp.zeros((WINDOW, LANES), F32)
        sink_rows = jnp.broadcast_to(sink_ref[0:1, :], (WINDOW, LANES))
        for g in range(N_KV_HEADS):
            k_g = kr[:, g * HEAD_DIM:(g + 1) * HEAD_DIM]
            v_g = vband[:, g * HEAD_DIM:(g + 1) * HEAD_DIM]
            q_g = _stack_heads(qr, g, qpk)
            sink = _stack_cols(sink_rows, g, qpk)
            s = jnp.where(mask, _dot(q_g, k_g, NT) * scale, NEG)
            m = jnp.maximum(jnp.max(s, axis=-1, keepdims=True), sink)
            p = jnp.exp(s - m)
            den = jnp.sum(p, axis=-1, keepdims=True) + jnp.exp(sink - m)
            o = _dot(p.astype(BF16), v_g, NN) * (1.0 / den)
            lse_g = m + jnp.log(den)
            for hh in range(qpk):
                h = g * qpk + hh
                rows = slice(hh * WINDOW, (hh + 1) * WINDOW)
                y_ref[:, h * HEAD_DIM:(h + 1) * HEAD_DIM] = o[rows].astype(BF16)
                lse_all = jnp.where(lane == h, lse_g[rows], lse_all)
        lse_ref[...] = lse_all

    return pl.pallas_call(
        body, name="attn_fwd", grid=(nb,),
        in_specs=[*proj_specs, *trig_cur, *trig_prev, pl.BlockSpec((8, LANES), lambda n: (0, 0))],
        out_specs=[pl.BlockSpec((WINDOW, A), lambda n: (n, 0)), pl.BlockSpec((WINDOW, LANES), lambda n: (n, 0))],
        out_shape=[jax.ShapeDtypeStruct((S, A), BF16), jax.ShapeDtypeStruct((S, LANES), F32)],
        compiler_params=_params(("parallel",)),
    )(proj, proj, proj, proj, proj, *trig, *trig, sink_row)


def _attn_bwd(proj, trig, sink_row, y, lse, dy, dm):
    S = proj.shape[0]
    A, KV, NQ = dm["A"], dm["KV"], dm["NQ"]
    qpk = NQ // N_KV_HEADS
    nb = S // WINDOW
    scale = HEAD_DIM ** -0.5
    proj_specs, trig_cur, trig_prev, cur, prev = _attn_specs(dm, nb)

    def body(q_ref, kp_ref, kc_ref, vp_ref, vc_ref, cc_ref, sac_ref, sbc_ref, cp_ref, sap_ref, sbp_ref,
             sink_ref, y_ref, lse_ref, dy_ref, dq_ref, dk_ref, dv_ref, dsink_ref,
             ck_ref, cv_ref, bk_ref, bv_ref, dqr_ref):
        n = pl.program_id(0)

        @pl.when(n == 0)
        def _():
            dsink_ref[...] = jnp.zeros_like(dsink_ref)
            ck_ref[...] = jnp.zeros_like(ck_ref)
            cv_ref[...] = jnp.zeros_like(cv_ref)

        @pl.when(n < nb)
        def _():
            tq = lambda r: jnp.tile(r[...], (1, A // LANES))
            tk = lambda rp, rc: jnp.tile(jnp.concatenate([rp[...], rc[...]], axis=0), (1, KV // LANES))
            cq, saq, sbq = tq(cc_ref), tq(sac_ref), tq(sbc_ref)
            ck, sak, sbk = tk(cp_ref, cc_ref), tk(sap_ref, sac_ref), tk(sbp_ref, sbc_ref)
            qr = _rope(q_ref[...].astype(F32), cq, saq, sbq).astype(BF16)
            kband = jnp.concatenate([kp_ref[...], kc_ref[...]], axis=0).astype(F32)
            kr = _rope(kband, ck, sak, sbk).astype(BF16)
            vband = jnp.concatenate([vp_ref[...], vc_ref[...]], axis=0)
            mask = _band_mask(n, qpk)
            lane = lax.broadcasted_iota(jnp.int32, (1, LANES), 1)
            lse_all = lse_ref[...]
            sink_rows = jnp.broadcast_to(sink_ref[0:1, :], (WINDOW, LANES))
            dy_all = dy_ref[...]
            y_all = y_ref[...]
            dsink = jnp.zeros((1, LANES), F32)
            for g in range(N_KV_HEADS):
                k_g = kr[:, g * HEAD_DIM:(g + 1) * HEAD_DIM]
                v_g = vband[:, g * HEAD_DIM:(g + 1) * HEAD_DIM]
                q_g = _stack_heads(qr, g, qpk)
                dy_g = _stack_heads(dy_all, g, qpk)
                y_g = _stack_heads(y_all, g, qpk)
                lse_g = _stack_cols(lse_all, g, qpk)
                s = jnp.where(mask, _dot(q_g, k_g, NT) * scale, NEG)
                p = jnp.exp(s - lse_g)
                dp = _dot(dy_g, v_g, NT)
                delta = jnp.sum(dy_g.astype(F32) * y_g.astype(F32), axis=-1, keepdims=True)
                ds = (p * (dp - delta) * scale).astype(BF16)
                dq_g = _dot(ds, k_g, NN)
                bk_ref[:, g * HEAD_DIM:(g + 1) * HEAD_DIM] = _dot(ds, q_g, TN)
                bv_ref[:, g * HEAD_DIM:(g + 1) * HEAD_DIM] = _dot(p.astype(BF16), dy_g, TN)
                sink_d = jnp.exp(_stack_cols(sink_rows, g, qpk) - lse_g) * delta
                for hh in range(qpk):
                    h = g * qpk + hh
                    rows = slice(hh * WINDOW, (hh + 1) * WINDOW)
                    dqr_ref[:, h * HEAD_DIM:(h + 1) * HEAD_DIM] = dq_g[rows]
                    dsink = dsink + jnp.where(lane == h, -jnp.sum(sink_d[rows], axis=0, keepdims=True), 0.0)
            dsink_ref[0:1, :] += dsink
            dq_ref[...] = _rope_t(dqr_ref[...], cq, saq, sbq).astype(BF16)
            dkb = _rope_t(bk_ref[...], ck, sak, sbk)
            dvb = bv_ref[...]
            dk_ref[...] = (ck_ref[...] + dkb[:WINDOW]).astype(BF16)
            dv_ref[...] = (cv_ref[...] + dvb[:WINDOW]).astype(BF16)
            ck_ref[...] = dkb[WINDOW:]
            cv_ref[...] = dvb[WINDOW:]

        @pl.when(n == nb)
        def _():
            dk_ref[...] = ck_ref[...].astype(BF16)
            dv_ref[...] = cv_ref[...].astype(BF16)

    row = lambda w: pl.BlockSpec((WINDOW, w), lambda n: (cur(n), 0))
    done = lambda w: pl.BlockSpec((WINDOW, w), lambda n: (jnp.maximum(n - 1, 0), 0))
    return pl.pallas_call(
        body, name="attn_bwd", grid=(nb + 1,),
        in_specs=[*proj_specs, *trig_cur, *trig_prev, pl.BlockSpec((8, LANES), lambda n: (0, 0)),
                  row(A), row(LANES), row(A)],
        out_specs=[row(A), done(KV), done(KV), pl.BlockSpec((8, LANES), lambda n: (0, 0))],
        out_shape=[jax.ShapeDtypeStruct((S, A), BF16), jax.ShapeDtypeStruct((S, KV), BF16),
                   jax.ShapeDtypeStruct((S, KV), BF16), jax.ShapeDtypeStruct((8, LANES), F32)],
        scratch_shapes=[pltpu.VMEM((WINDOW, KV), F32), pltpu.VMEM((WINDOW, KV), F32),
                        pltpu.VMEM((2 * WINDOW, KV), F32), pltpu.VMEM((2 * WINDOW, KV), F32),
                        pltpu.VMEM((WINDOW, A), F32)],
        compiler_params=_params(("arbitrary",)),
    )(proj, proj, proj, proj, proj, *trig, *trig, sink_row, y, lse, dy)


def _sgu_layout(dm, S):
    G = dm["G"]
    pw = math.gcd(dm["OFF_Z"], G)
    npc = G // pw
    tm = _pick(S, (256, 128))
    u_specs = [pl.BlockSpec((tm, pw), lambda i, p=p: (i, dm["OFF_Z"] // pw + p)) for p in range(npc)]
    v_specs = [pl.BlockSpec((tm, pw), lambda i, p=p: (i, (dm["OFF_Z"] + G) // pw + p)) for p in range(npc)]
    return pw, npc, tm, u_specs, v_specs


def _sgu_norm(v_refs, lg_ref, lb_ref):
    v = jnp.concatenate([_gelu(r[...].astype(F32)) for r in v_refs], axis=1)
    mu = jnp.mean(v, axis=-1, keepdims=True)
    vc = v - mu
    rstd = lax.rsqrt(jnp.mean(vc * vc, axis=-1, keepdims=True) + EPS)
    xhat = vc * rstd
    return xhat, rstd, (xhat * lg_ref[...] + lb_ref[...]).astype(BF16)


def _sgu_fwd(proj, w_tril, b_t, ln_g_row, ln_b_row, dm):
    S = proj.shape[0]
    G, NG = dm["G"], dm["NG"]
    pw, npc, tm, u_specs, v_specs = _sgu_layout(dm, S)
    nch = tm // WINDOW

    def body(*refs):
        u_refs, v_refs = refs[:npc], refs[npc:2 * npc]
        w_ref, bt_ref, lg_ref, lb_ref, y_ref = refs[2 * npc:]
        _, _, vn = _sgu_norm(v_refs, lg_ref, lb_ref)
        u = jnp.concatenate([_gelu(r[...].astype(F32)) for r in u_refs], axis=1)
        for c in range(nch):
            rows = slice(c * WINDOW, (c + 1) * WINDOW)
            for g in range(NG):
                cols = slice(g * LANES, (g + 1) * LANES)
                sv = _dot(w_ref[g], vn[rows, cols], NN) + bt_ref[:, g:g + 1]
                y_ref[rows, cols] = (u[rows, cols] * sv).astype(BF16)

    return pl.pallas_call(
        body, name="sgu_fwd", grid=(S // tm,),
        in_specs=[*u_specs, *v_specs,
                  pl.BlockSpec((NG, WINDOW, WINDOW), lambda i: (0, 0, 0)),
                  pl.BlockSpec((WINDOW, LANES), lambda i: (0, 0)),
                  pl.BlockSpec((1, G), lambda i: (0, 0)), pl.BlockSpec((1, G), lambda i: (0, 0))],
        out_specs=pl.BlockSpec((tm, G), lambda i: (i, 0)),
        out_shape=jax.ShapeDtypeStruct((S, G), BF16),
        compiler_params=_params(("parallel",)),
    )(*([proj] * (2 * npc)), w_tril, b_t, ln_g_row, ln_b_row)


def _sgu_bwd(proj, w_tril, b_t, ln_g_row, ln_b_row, dy, dm):
    S = proj.shape[0]
    G, NG = dm["G"], dm["NG"]
    pw, npc, tm, u_specs, v_specs = _sgu_layout(dm, S)
    nch = tm // WINDOW

    def body(*refs):
        u_refs, v_refs = refs[:npc], refs[npc:2 * npc]
        w_ref, bt_ref, lg_ref, lb_ref, dy_ref, dz_ref, dw_ref, dbt_ref, dlg_ref, dlb_ref, dvn_ref = refs[2 * npc:]
        i = pl.program_id(0)

        @pl.when(i == 0)
        def _():
            dw_ref[...] = jnp.zeros_like(dw_ref)
            dbt_ref[...] = jnp.zeros_like(dbt_ref)
            dlg_ref[...] = jnp.zeros_like(dlg_ref)
            dlb_ref[...] = jnp.zeros_like(dlb_ref)

        xhat, rstd, vn = _sgu_norm(v_refs, lg_ref, lb_ref)
        u_pre = jnp.concatenate([r[...].astype(F32) for r in u_refs], axis=1)
        u = _gelu(u_pre)
        dy = dy_ref[...].astype(F32)
        lane = lax.broadcasted_iota(jnp.int32, (WINDOW, LANES), 1)
        tri = lax.broadcasted_iota(jnp.int32, (WINDOW, WINDOW), 0) >= lax.broadcasted_iota(jnp.int32, (WINDOW, WINDOW), 1)
        dbt = jnp.zeros((WINDOW, LANES), F32)
        for c in range(nch):
            rows = slice(c * WINDOW, (c + 1) * WINDOW)
            for g in range(NG):
                cols = slice(g * LANES, (g + 1) * LANES)
                vn_cg = vn[rows, cols]
                sv = _dot(w_ref[g], vn_cg, NN) + bt_ref[:, g:g + 1]
                dy_cg = dy[rows, cols]
                dsv = dy_cg * u[rows, cols]
                dsv_b = dsv.astype(BF16)
                dz_ref[rows, cols] = (dy_cg * sv * _gelu_grad(u_pre[rows, cols])).astype(BF16)
                dvn_ref[rows, cols] = _dot(w_ref[g], dsv_b, TN)
                dw_ref[g] += jnp.where(tri, _dot(dsv_b, vn_cg, NT), 0.0)
                dbt = dbt + jnp.where(lane == g, jnp.sum(dsv, axis=-1, keepdims=True), 0.0)
        dbt_ref[...] += dbt
        dvn = dvn_ref[...]
        dlg_ref[0:1, :] += jnp.sum(dvn * xhat, axis=0, keepdims=True)
        dlb_ref[0:1, :] += jnp.sum(dvn, axis=0, keepdims=True)
        dxh = dvn * lg_ref[...]
        dv = rstd * (dxh - jnp.mean(dxh, axis=-1, keepdims=True) - xhat * jnp.mean(dxh * xhat, axis=-1, keepdims=True))
        v_pre = jnp.concatenate([r[...].astype(F32) for r in v_refs], axis=1)
        dz_ref[:, G:] = (dv * _gelu_grad(v_pre)).astype(BF16)

    return pl.pallas_call(
        body, name="sgu_bwd", grid=(S // tm,),
        in_specs=[*u_specs, *v_specs,
                  pl.BlockSpec((NG, WINDOW, WINDOW), lambda i: (0, 0, 0)),
                  pl.BlockSpec((WINDOW, LANES), lambda i: (0, 0)),
                  pl.BlockSpec((1, G), lambda i: (0, 0)), pl.BlockSpec((1, G), lambda i: (0, 0)),
                  pl.BlockSpec((tm, G), lambda i: (i, 0))],
        out_specs=[pl.BlockSpec((tm, 2 * G), lambda i: (i, 0)),
                   pl.BlockSpec((NG, WINDOW, WINDOW), lambda i: (0, 0, 0)),
                   pl.BlockSpec((WINDOW, LANES), lambda i: (0, 0)),
                   pl.BlockSpec((8, G), lambda i: (0, 0)), pl.BlockSpec((8, G), lambda i: (0, 0))],
        out_shape=[jax.ShapeDtypeStruct((S, 2 * G), BF16), jax.ShapeDtypeStruct((NG, WINDOW, WINDOW), F32),
                   jax.ShapeDtypeStruct((WINDOW, LANES), F32), jax.ShapeDtypeStruct((8, G), F32),
                   jax.ShapeDtypeStruct((8, G), F32)],
        scratch_shapes=[pltpu.VMEM((tm, G), F32)],
        compiler_params=_params(("arbitrary",)),
    )(*([proj] * (2 * npc)), w_tril, b_t, ln_g_row, ln_b_row, dy)


def _in_proj(xn, w_in_g, b_row, dm):
    S, D = xn.shape
    IN = dm["IN"]
    cw = IN // N_CHIPS
    tm = _pick(S, (512, 256, 128))
    tn = _pick(cw, (1920, 640, 512, 256, 128))
    nbc = cw // tn

    def ep(parts, e_refs, o_refs):
        o_refs[0][...] = (parts[0] + e_refs[0][...]).astype(BF16)

    return _matmul(
        "in_proj", xn, [w_in_g], dims=NN, grid=(S // tm, IN // tn, 1),
        lhs_spec=pl.BlockSpec((tm, D), lambda i, j, k: (i, 0)),
        rhs_specs=[pl.BlockSpec((None, D, tn), lambda i, j, k: (j // nbc, 0, j % nbc))],
        acc_shape=(tm, tn), extra=[b_row], extra_specs=[pl.BlockSpec((1, tn), lambda i, j, k: (0, j))],
        out_shape=[jax.ShapeDtypeStruct((S, IN), BF16)],
        out_specs=[pl.BlockSpec((tm, tn), lambda i, j, k: (i, j))], epilogue=ep)[0]


def _branch_attn(y_attn, w_ab_g, dm):
    S, A = y_attn.shape
    D = dm["D"]
    cw = D // N_CHIPS
    tm = _pick(S, (1024, 512, 256, 128))
    return _matmul(
        "branch_attn", y_attn, [w_ab_g], dims=NN, grid=(S // tm, N_CHIPS, 1),
        lhs_spec=pl.BlockSpec((tm, A), lambda i, j, k: (i, 0)),
        rhs_specs=[pl.BlockSpec((None, A, cw), lambda i, j, k: (j, 0, 0))],
        acc_shape=(tm, cw), out_shape=[jax.ShapeDtypeStruct((S, D), BF16)],
        out_specs=[pl.BlockSpec((tm, cw), lambda i, j, k: (i, j))], epilogue=_store_epilogue(BF16))[0]


def _branch_sgu_merge(y_sgu, w_sb_g, a_attn, proj, dm):
    S, G = y_sgu.shape
    D, OFF_G = dm["D"], dm["OFF_G"]
    cw = D // N_CHIPS
    tm = _pick(S, (1024, 512, 256, 128))

    def ep(parts, e_refs, o_refs):
        a_sgu = parts[0].astype(BF16)
        ga = _sigmoid(e_refs[1][...].astype(F32))
        gs = _sigmoid(e_refs[2][...].astype(F32))
        o_refs[0][...] = a_sgu
        o_refs[1][...] = (ga * e_refs[0][...].astype(F32) + gs * a_sgu.astype(F32)).astype(BF16)

    blk = pl.BlockSpec((tm, cw), lambda i, j, k: (i, j))
    return _matmul(
        "branch_sgu_merge", y_sgu, [w_sb_g], dims=NN, grid=(S // tm, N_CHIPS, 1),
        lhs_spec=pl.BlockSpec((tm, G), lambda i, j, k: (i, 0)),
        rhs_specs=[pl.BlockSpec((None, G, cw), lambda i, j, k: (j, 0, 0))],
        acc_shape=(tm, cw), extra=[a_attn, proj, proj],
        extra_specs=[blk, pl.BlockSpec((tm, cw), lambda i, j, k: (i, OFF_G // cw + j)),
                     pl.BlockSpec((tm, cw), lambda i, j, k: (i, (OFF_G + D) // cw + j))],
        out_shape=[jax.ShapeDtypeStruct((S, D), BF16), jax.ShapeDtypeStruct((S, D), BF16)],
        out_specs=[blk, blk], epilogue=ep)


def _residual_matmul(name, a, w_g, h):
    S, K = a.shape
    D = w_g.shape[1]
    tm = _pick(S, (512, 256, 128))
    tn = _pick(D, (512, 256, 128))

    def ep(parts, e_refs, o_refs):
        o_refs[0][...] = e_refs[0][...] + parts[0]

    blk = pl.BlockSpec((tm, tn), lambda i, j, k: (i, j))
    return _matmul(
        name, a, [w_g], dims=NN, grid=(S // tm, D // tn, 1),
        lhs_spec=pl.BlockSpec((tm, K), lambda i, j, k: (i, 0)),
        rhs_specs=[pl.BlockSpec((K, tn), lambda i, j, k: (0, j))],
        acc_shape=(tm, tn), extra=[h], extra_specs=[blk],
        out_shape=[jax.ShapeDtypeStruct((S, D), F32)], out_specs=[blk], epilogue=ep)[0]


def _gate_up(hn, w_gu_g, dm):
    S, D = hn.shape
    Fd = dm["F"]
    cw = 2 * Fd // N_CHIPS
    tm = _pick(S, (512, 256, 128))
    tn = _pick(cw, (1408, 512, 384, 256, 128))
    nbc = cw // tn
    half = N_CHIPS // 2

    def ep(parts, e_refs, o_refs):
        gate, up = parts[0].astype(BF16), parts[1].astype(BF16)
        o_refs[0][0] = gate
        o_refs[0][1] = up
        g32 = gate.astype(F32)
        o_refs[1][...] = (g32 * _sigmoid(g32) * up.astype(F32)).astype(BF16)

    return _matmul(
        "gate_up", hn, [w_gu_g, w_gu_g], dims=NN, grid=(S // tm, Fd // tn, 1),
        lhs_spec=pl.BlockSpec((tm, D), lambda i, j, k: (i, 0)),
        rhs_specs=[pl.BlockSpec((None, D, tn), lambda i, j, k: (j // nbc, 0, j % nbc)),
                   pl.BlockSpec((None, D, tn), lambda i, j, k: (half + j // nbc, 0, j % nbc))],
        acc_shape=(tm, tn),
        out_shape=[jax.ShapeDtypeStruct((2, S, Fd), BF16), jax.ShapeDtypeStruct((S, Fd), BF16)],
        out_specs=[pl.BlockSpec((2, tm, tn), lambda i, j, k: (0, i, j)), pl.BlockSpec((tm, tn), lambda i, j, k: (i, j))],
        epilogue=ep)


def _down_bwd(dh_b, w_down_g, gu, dm):
    S, D = dh_b.shape
    Fd = dm["F"]
    tm = _pick(S, (1024, 512, 256, 128))
    tn = _pick(Fd, (512, 256, 128))

    def ep(parts, e_refs, o_refs):
        gate = e_refs[0][0].astype(F32)
        up = e_refs[0][1].astype(F32)
        s = _sigmoid(gate)
        dact = parts[0]
        o_refs[0][0] = (dact * up * s * (1.0 + gate * (1.0 - s))).astype(BF16)
        o_refs[0][1] = (dact * gate * s).astype(BF16)

    blk = pl.BlockSpec((2, tm, tn), lambda i, j, k: (0, i, j))
    return _matmul(
        "down_bwd", dh_b, [w_down_g], dims=NT, grid=(S // tm, Fd // tn, 1),
        lhs_spec=pl.BlockSpec((tm, D), lambda i, j, k: (i, 0)),
        rhs_specs=[pl.BlockSpec((tn, D), lambda i, j, k: (j, 0))],
        acc_shape=(tm, tn), extra=[gu], extra_specs=[blk],
        out_shape=[jax.ShapeDtypeStruct((2, S, Fd), BF16)], out_specs=[blk], epilogue=ep)[0]


def _gate_up_bwd(dgu, w_gu_g, dm):
    S = dgu.shape[1]
    D, Fd = dm["D"], dm["F"]
    cw = 2 * Fd // N_CHIPS
    half = N_CHIPS // 2
    tm = _pick(S, (512, 256, 128))
    tn = _pick(D, (1024, 512, 256, 128))
    return _matmul(
        "gate_up_bwd", dgu, [w_gu_g], dims=NT, grid=(S // tm, D // tn, N_CHIPS),
        lhs_spec=pl.BlockSpec((None, tm, cw), lambda i, j, k: (k // half, i, k % half)),
        rhs_specs=[pl.BlockSpec((None, tn, cw), lambda i, j, k: (k, j, 0))],
        acc_shape=(tm, tn), out_shape=[jax.ShapeDtypeStruct((S, D), F32)],
        out_specs=[pl.BlockSpec((tm, tn), lambda i, j, k: (i, j))], epilogue=_store_epilogue(F32))[0]


def _out_bwd(dh_b, w_out_g, proj, a_attn, a_sgu, dm):
    S, D = dh_b.shape
    OFF_G = dm["OFF_G"]
    tm = _pick(S, (1024, 512, 256, 128))
    tn = D // N_CHIPS

    def ep(parts, e_refs, o_refs):
        dm_ = parts[0]
        ga = _sigmoid(e_refs[0][...].astype(F32))
        gs = _sigmoid(e_refs[1][...].astype(F32))
        o_refs[0][...] = (dm_ * ga).astype(BF16)
        o_refs[1][...] = (dm_ * gs).astype(BF16)
        o_refs[2][0] = (dm_ * e_refs[2][...].astype(F32) * ga * (1.0 - ga)).astype(BF16)
        o_refs[2][1] = (dm_ * e_refs[3][...].astype(F32) * gs * (1.0 - gs)).astype(BF16)

    blk = pl.BlockSpec((tm, tn), lambda i, j, k: (i, j))
    return _matmul(
        "out_bwd", dh_b, [w_out_g], dims=NT, grid=(S // tm, D // tn, 1),
        lhs_spec=pl.BlockSpec((tm, D), lambda i, j, k: (i, 0)),
        rhs_specs=[pl.BlockSpec((tn, D), lambda i, j, k: (j, 0))],
        acc_shape=(tm, tn), extra=[proj, proj, a_attn, a_sgu],
        extra_specs=[pl.BlockSpec((tm, tn), lambda i, j, k: (i, OFF_G // tn + j)),
                     pl.BlockSpec((tm, tn), lambda i, j, k: (i, (OFF_G + D) // tn + j)), blk, blk],
        out_shape=[jax.ShapeDtypeStruct((S, D), BF16), jax.ShapeDtypeStruct((S, D), BF16),
                   jax.ShapeDtypeStruct((2, S, D), BF16)],
        out_specs=[blk, blk, pl.BlockSpec((2, tm, tn), lambda i, j, k: (0, i, j))], epilogue=ep)


def _colsharded_bwd(name, dy, w_g, out_dtype):
    S = dy.shape[0]
    _, K, cw = w_g.shape
    tm = _pick(S, (512, 256, 128))
    tn = _pick(K, (1024, 512, 256, 128))
    return _matmul(
        name, dy, [w_g], dims=NT, grid=(S // tm, K // tn, N_CHIPS),
        lhs_spec=pl.BlockSpec((tm, cw), lambda i, j, k: (i, k)),
        rhs_specs=[pl.BlockSpec((None, tn, cw), lambda i, j, k: (k, j, 0))],
        acc_shape=(tm, tn), out_shape=[jax.ShapeDtypeStruct((S, K), out_dtype)],
        out_specs=[pl.BlockSpec((tm, tn), lambda i, j, k: (i, j))], epilogue=_store_epilogue(out_dtype))[0]


def _wgrad_cols(name, x, dy):
    S, R = x.shape
    C = dy.shape[1]
    cw = C // N_CHIPS
    tm = _pick(R, (512, 256, 128) if cw >= 1024 else (1024, 512, 256, 128))
    tk = _pick(S, (2048, 1024, 512, 256, 128))
    return _matmul(
        name, x, [dy], dims=TN, grid=(R // tm, N_CHIPS, S // tk),
        lhs_spec=pl.BlockSpec((tk, tm), lambda i, j, k: (k, i)),
        rhs_specs=[pl.BlockSpec((tk, cw), lambda i, j, k: (k, j))],
        acc_shape=(tm, cw), out_shape=[jax.ShapeDtypeStruct((N_CHIPS, R, cw), F32)],
        out_specs=[pl.BlockSpec((None, tm, cw), lambda i, j, k: (j, i, 0))], epilogue=_store_epilogue(F32))[0]


def _wgrad_gate_up(hn, dgu, dm):
    S, D = hn.shape
    Fd = dm["F"]
    cw = 2 * Fd // N_CHIPS
    half = N_CHIPS // 2
    tm = _pick(D, (512, 256, 128))
    tk = _pick(S, (1024, 512, 256, 128))
    return _matmul(
        "wgrad_gate_up", hn, [dgu], dims=TN, grid=(D // tm, N_CHIPS, S // tk),
        lhs_spec=pl.BlockSpec((tk, tm), lambda i, j, k: (k, i)),
        rhs_specs=[pl.BlockSpec((None, tk, cw), lambda i, j, k: (j // half, k, j % half))],
        acc_shape=(tm, cw), out_shape=[jax.ShapeDtypeStruct((N_CHIPS, D, cw), F32)],
        out_specs=[pl.BlockSpec((None, tm, cw), lambda i, j, k: (j, i, 0))], epilogue=_store_epilogue(F32))[0]


def _wgrad_rows(name, x, dy):
    S, R = x.shape
    C = dy.shape[1]
    rw = R // N_CHIPS
    tn = _pick(C, (1024, 512, 256, 128))
    tk = _pick(S, (1024, 512, 256, 128))
    return _matmul(
        name, x, [dy], dims=TN, grid=(N_CHIPS, C // tn, S // tk),
        lhs_spec=pl.BlockSpec((tk, rw), lambda i, j, k: (k, i)),
        rhs_specs=[pl.BlockSpec((tk, tn), lambda i, j, k: (k, j))],
        acc_shape=(rw, tn), out_shape=[jax.ShapeDtypeStruct((N_CHIPS, rw, C), F32)],
        out_specs=[pl.BlockSpec((None, rw, tn), lambda i, j, k: (i, 0, j))], epilogue=_store_epilogue(F32))[0]


def _place():
    x, y, c = lax.axis_index("x"), lax.axis_index("y"), lax.axis_index("c")
    others = [(1 - x, y), (x, 1 - y), (1 - x, 1 - y)]
    return x, y, c, others


def _chip_index(chip):
    return 2 * chip[0] + chip[1]


def _gather_weights(bufs):
    n = len(bufs)
    shapes = [b.shape for b in bufs]

    def body(*refs):
        src, out = refs[:n], refs[n:2 * n]
        send_sems, recv_sems = refs[2 * n:]
        x, y, c, others = _place()
        me, sibling = (x, y, c), (x, y, 1 - c)
        mine = _chip_index((x, y))

        def half(ref, chip_idx, hc):
            r2 = ref.shape[1] // 2
            return ref.at[chip_idx, pl.ds(hc * r2, r2), :]

        def copy(t, k, chip_idx, hc, to):
            return pltpu.make_async_remote_copy(
                src_ref=half(src[t], chip_idx, hc), dst_ref=half(out[t], chip_idx, hc),
                send_sem=send_sems.at[6 * t + k], recv_sem=recv_sems.at[6 * t + k],
                device_id=to, device_id_type=MESH)

        started = []
        for t in range(n):
            for j, chip in enumerate(others):
                cp = copy(t, j, mine, c, (*chip, c))
                cp.start()
                started.append(cp)
        for t in range(n):
            for j, chip in enumerate(others):
                copy(t, j, _chip_index(chip), c, me).wait_recv()
                cp = copy(t, 3 + j, _chip_index(chip), c, sibling)
                cp.start()
                started.append(cp)
        for t in range(n):
            for j, chip in enumerate(others):
                copy(t, 3 + j, _chip_index(chip), 1 - c, me).wait_recv()
        for cp in started:
            cp.wait_send()

    return pl.pallas_call(
        body, name="gather_weights",
        in_specs=[ANY] * n, out_specs=[ANY] * n,
        out_shape=[jax.ShapeDtypeStruct(s, BF16) for s in shapes],
        input_output_aliases={t: t for t in range(n)},
        scratch_shapes=[pltpu.SemaphoreType.DMA((6 * n,)), pltpu.SemaphoreType.DMA((6 * n,))],
    )(*bufs)


def _sibling_exchange(grads):
    n = len(grads)
    shapes = [g.shape for g in grads]

    def body(*refs):
        src, land = refs[:n], refs[n:2 * n]
        send_sems, recv_sems = refs[2 * n:]
        x, y, c, _ = _place()
        remote = []
        for t in range(n):
            r2 = shapes[t][1] // 2
            remote.append(pltpu.make_async_remote_copy(
                src_ref=src[t].at[:, pl.ds((1 - c) * r2, r2), :], dst_ref=land[t],
                send_sem=send_sems.at[t], recv_sem=recv_sems.at[t], device_id=(x, y, 1 - c), device_id_type=MESH))
        for cp in remote:
            cp.start()
        for cp in remote:
            cp.wait_recv()
        for cp in remote:
            cp.wait_send()

    return pl.pallas_call(
        body, name="sibling_exchange",
        in_specs=[ANY] * n, out_specs=[ANY] * n,
        out_shape=[jax.ShapeDtypeStruct((s[0], s[1] // 2, s[2]), F32) for s in shapes],
        scratch_shapes=[pltpu.SemaphoreType.DMA((n,)), pltpu.SemaphoreType.DMA((n,))],
    )(*grads)


def _chip_exchange(sends):
    n = len(sends)
    shapes = [s.shape for s in sends]

    def body(*refs):
        snd, got = refs[:n], refs[n:2 * n]
        send_sems, recv_sems = refs[2 * n:]
        x, y, c, others = _place()
        remote = []
        for t in range(n):
            for j, chip in enumerate(others):
                remote.append(pltpu.make_async_remote_copy(
                    src_ref=snd[t].at[_chip_index(chip)], dst_ref=got[t].at[j],
                    send_sem=send_sems.at[3 * t + j], recv_sem=recv_sems.at[3 * t + j],
                    device_id=(*chip, c), device_id_type=MESH))
        for cp in remote:
            cp.start()
        for cp in remote:
            cp.wait_recv()
        for cp in remote:
            cp.wait_send()

    return pl.pallas_call(
        body, name="chip_exchange",
        in_specs=[ANY] * n, out_specs=[ANY] * n,
        out_shape=[jax.ShapeDtypeStruct((3, s[1], s[2]), BF16) for s in shapes],
        scratch_shapes=[pltpu.SemaphoreType.DMA((3 * n,)), pltpu.SemaphoreType.DMA((3 * n,))],
    )(*sends)


def _sibling_share(fulls):
    n = len(fulls)
    shapes = [f.shape for f in fulls]

    def body(*refs):
        src, out = refs[:n], refs[n:2 * n]
        send_sems, recv_sems = refs[2 * n:]
        x, y, c, _ = _place()
        sibling = (x, y, 1 - c)

        def copy(t, hc):
            r2 = shapes[t][0] // 2
            return pltpu.make_async_remote_copy(
                src_ref=src[t].at[pl.ds(hc * r2, r2), :], dst_ref=out[t].at[pl.ds(hc * r2, r2), :],
                send_sem=send_sems.at[t], recv_sem=recv_sems.at[t], device_id=sibling, device_id_type=MESH)

        remote = [copy(t, c) for t in range(n)]
        for cp in remote:
            cp.start()
        for t in range(n):
            copy(t, 1 - c).wait_recv()
        for cp in remote:
            cp.wait_send()

    return pl.pallas_call(
        body, name="sibling_share",
        in_specs=[ANY] * n, out_specs=[ANY] * n,
        out_shape=[jax.ShapeDtypeStruct(s, F32) for s in shapes],
        input_output_aliases={t: t for t in range(n)},
        scratch_shapes=[pltpu.SemaphoreType.DMA((n,)), pltpu.SemaphoreType.DMA((n,))],
    )(*fulls)


def _gather_all(v):
    R, C = v.shape

    def body(v_ref, out_ref, send_sems, recv_sems, local_sem):
        x, y, c, others = _place()
        me, sibling = (x, y, c), (x, y, 1 - c)

        def rows(px, py, pc):
            return out_ref.at[4 * px + 2 * py + pc]

        def copy(k, block, to, src=None):
            return pltpu.make_async_remote_copy(
                src_ref=rows(*block) if src is None else src, dst_ref=rows(*block),
                send_sem=send_sems.at[k], recv_sem=recv_sems.at[k], device_id=to, device_id_type=MESH)

        mine = pltpu.make_async_copy(v_ref, rows(*me), local_sem)
        mine.start()
        first = [copy(0, me, sibling, src=v_ref)]
        first += [copy(1 + j, me, (*chip, c), src=v_ref) for j, chip in enumerate(others)]
        for cp in first:
            cp.start()
        passed = [copy(4 + j, (*chip, c), sibling) for j, chip in enumerate(others)]
        for j, chip in enumerate(others):
            copy(1 + j, (*chip, c), me).wait_recv()
            passed[j].start()
        copy(0, sibling, me).wait_recv()
        for j, chip in enumerate(others):
            copy(4 + j, (*chip, 1 - c), me).wait_recv()
        for cp in first + passed:
            cp.wait_send()
        mine.wait()

    return pl.pallas_call(
        body, name="gather_all", in_specs=[ANY], out_specs=ANY,
        out_shape=jax.ShapeDtypeStruct((8, R, C), F32),
        scratch_shapes=[pltpu.SemaphoreType.DMA((7,)), pltpu.SemaphoreType.DMA((7,)), pltpu.SemaphoreType.DMA],
    )(v)


def _my_chip():
    return 2 * lax.axis_index("x") + lax.axis_index("y")


def _my_core():
    return lax.axis_index("c")


def _pair_sum(grad, land):
    K, R2, C = land.shape
    tm = _row_tile(R2, C)
    nrb = R2 // tm

    def body(a_ref, b_ref, s_ref, sb_ref):
        s = a_ref[...] + b_ref[...]
        s_ref[...] = s
        sb_ref[...] = s.astype(BF16)

    blk = pl.BlockSpec((None, tm, C), lambda k, r: (k, r, 0))
    return pl.pallas_call(
        body, name="pair_sum", grid=(K, nrb),
        in_specs=[pl.BlockSpec((None, tm, C), lambda k, r: (k, _my_core() * nrb + r, 0)), blk],
        out_specs=[blk, blk],
        out_shape=[jax.ShapeDtypeStruct((K, R2, C), F32), jax.ShapeDtypeStruct((K, R2, C), BF16)],
        compiler_params=_params(("parallel", "parallel")),
    )(grad, land)


def _chip_sum(sums, got):
    _, R2, C = sums.shape
    tm = _row_tile(R2, C)
    nrb = R2 // tm

    def body(o_ref, g_ref, s_ref):
        s_ref[...] = ((o_ref[...] + g_ref[0].astype(F32)) + g_ref[1].astype(F32)) + g_ref[2].astype(F32)

    return pl.pallas_call(
        body, name="chip_sum", grid=(nrb,),
        in_specs=[pl.BlockSpec((None, tm, C), lambda r: (_my_chip(), r, 0)),
                  pl.BlockSpec((3, tm, C), lambda r: (0, r, 0))],
        out_specs=pl.BlockSpec((tm, C), lambda r: (_my_core() * nrb + r, 0)),
        out_shape=jax.ShapeDtypeStruct((2 * R2, C), F32),
        compiler_params=_params(("parallel",)),
    )(sums, got)


def _adamw_math(w, g, m, v):
    m = ADAM_B1 * m + (1.0 - ADAM_B1) * g
    v = ADAM_B2 * v + (1.0 - ADAM_B2) * (g * g)
    m_hat = m / (1.0 - ADAM_B1 ** ADAM_STEP)
    v_hat = v / (1.0 - ADAM_B2 ** ADAM_STEP)
    delta = -ADAM_LR * (m_hat / (jnp.sqrt(v_hat) + ADAM_EPS) + ADAM_WD * w)
    return delta, m, v


def _adamw_stacked(grads, w, m, v):
    L, R, C = w.shape
    tm = _row_tile(R, C)
    nrb = R // tm

    def body(*refs):
        g_refs = refs[:L]
        w_ref, m_ref, v_ref, go_ref, d_ref, mo_ref, vo_ref = refs[L:]
        l = pl.program_id(0)
        for ll in range(L):
            @pl.when(l == ll)
            def _(ll=ll):
                g = g_refs[ll][...]
                delta, mn, vn = _adamw_math(w_ref[...], g, m_ref[...], v_ref[...])
                go_ref[...] = g
                d_ref[...] = delta
                mo_ref[...] = mn
                vo_ref[...] = vn

    stacked = pl.BlockSpec((None, tm, C), lambda l, r: (l, r, 0))
    g_specs = [pl.BlockSpec((tm, C), lambda l, r, ll=ll: (jnp.where(l == ll, r, 0), 0)) for ll in range(L)]
    shp = jax.ShapeDtypeStruct((L, R, C), F32)
    return pl.pallas_call(
        body, name="adamw", grid=(L, nrb),
        in_specs=[*g_specs, stacked, stacked, stacked], out_specs=[stacked] * 4, out_shape=[shp] * 4,
        compiler_params=_params(("arbitrary", "arbitrary")),
    )(*grads, w, m, v)


def _adamw_small(parts, w, m, v):
    _, R, C = parts.shape
    tm = _row_tile(R, 8 * C)

    def body(p_ref, w_ref, m_ref, v_ref, go_ref, d_ref, mo_ref, vo_ref):
        g = p_ref[0]
        for k in range(1, 8):
            g = g + p_ref[k]
        delta, mn, vn = _adamw_math(w_ref[...], g, m_ref[...], v_ref[...])
        go_ref[...] = g
        d_ref[...] = delta
        mo_ref[...] = mn
        vo_ref[...] = vn

    blk = pl.BlockSpec((tm, C), lambda i: (i, 0))
    shp = jax.ShapeDtypeStruct((R, C), F32)
    return pl.pallas_call(
        body, name="adamw_small", grid=(R // tm,),
        in_specs=[pl.BlockSpec((8, tm, C), lambda i: (0, i, 0)), blk, blk, blk],
        out_specs=[blk] * 4, out_shape=[shp] * 4,
        compiler_params=_params(("parallel",)),
    )(parts, w, m, v)


def _cast_place(w, layer):
    _, R, C = w.shape
    tm = _row_tile(R, C)

    def body(w_ref, o_ref):
        o_ref[...] = w_ref[...].astype(BF16)

    return pl.pallas_call(
        body, name="cast_place", grid=(R // tm,),
        in_specs=[pl.BlockSpec((None, tm, C), lambda r: (layer, r, 0))],
        out_specs=pl.BlockSpec((None, tm, C), lambda r: (_my_chip(), r, 0)),
        out_shape=jax.ShapeDtypeStruct((N_CHIPS, R, C), BF16),
        compiler_params=_params(("parallel",)),
    )(w)


def _trig_tables(positions):
    half = ROPE_DIM // 2
    inv_freq = ROPE_THETA ** (-jnp.arange(0, ROPE_DIM, 2, dtype=F32) / ROPE_DIM)
    ang = positions.astype(F32)[:, None] * inv_freq
    cos, sin = jnp.cos(ang), jnp.sin(ang)
    S = positions.shape[0]
    zeros = lambda w: jnp.zeros((S, w), F32)
    cos_h = jnp.concatenate([cos, cos, jnp.ones((S, HEAD_DIM - ROPE_DIM), F32)], axis=1)
    sa_h = jnp.concatenate([-sin, zeros(HEAD_DIM - half)], axis=1)
    sb_h = jnp.concatenate([zeros(half), sin, zeros(HEAD_DIM - ROPE_DIM)], axis=1)
    rep = LANES // HEAD_DIM
    return [jnp.tile(t, (1, rep)) for t in (cos_h, sa_h, sb_h)]


def _row(vec):
    return vec.reshape(1, -1)


def _lane_row(vec):
    return jnp.zeros((8, LANES), F32).at[0, :vec.shape[0]].set(vec)


def _pack(pieces, rows):
    flat = jnp.concatenate([p.reshape(-1).astype(F32) for p in pieces])
    return jnp.pad(flat, (0, rows * LANES - flat.shape[0])).reshape(rows, LANES)


def kernel(x, positions, norm1_g, w_in, b_in, sinks, sgu_ln_g, sgu_ln_b, sgu_w, sgu_b, w_attn_branch, w_sgu_branch, w_out, norm2_g, w_gate_up, w_down, final_g, loss_target, m_norm1_g, m_w_in, m_b_in, m_sinks, m_sgu_ln_g, m_sgu_ln_b, m_sgu_w, m_sgu_b, m_w_attn_branch, m_w_sgu_branch, m_w_out, m_norm2_g, m_w_gate_up, m_w_down, m_final_g, v_norm1_g, v_w_in, v_b_in, v_sinks, v_sgu_ln_g, v_sgu_ln_b, v_sgu_w, v_sgu_b, v_w_attn_branch, v_w_sgu_branch, v_w_out, v_norm2_g, v_w_gate_up, v_w_down, v_final_g):
    L = norm1_g.shape[0]
    S, D = x.shape[1], x.shape[2]
    NQ = sinks.shape[1]
    A = NQ * HEAD_DIM
    KV = N_KV_HEADS * HEAD_DIM
    G = sgu_ln_g.shape[1]
    NG = sgu_w.shape[1]
    IN = b_in.shape[1]
    Fd = w_down.shape[1] * N_CHIPS
    dm = dict(D=D, A=A, KV=KV, NQ=NQ, G=G, NG=NG, IN=IN, F=Fd,
              OFF_K=A, OFF_V=A + KV, OFF_Z=A + 2 * KV, OFF_G=A + 2 * KV + 2 * G)
    assert sgu_w.shape[2] == WINDOW and G == NG * LANES and IN == dm["OFF_G"] + 2 * D

    h = x[0]
    target = loss_target[0]
    trig = _trig_tables(positions[0])
    tril = jnp.tril(jnp.ones((WINDOW, WINDOW), bool))

    big = [w_in, w_attn_branch, w_sgu_branch, w_out, w_gate_up, w_down]
    big_m = [m_w_in, m_w_attn_branch, m_w_sgu_branch, m_w_out, m_w_gate_up, m_w_down]
    big_v = [v_w_in, v_w_attn_branch, v_w_sgu_branch, v_w_out, v_w_gate_up, v_w_down]

    gathered = [_gather_weights([_cast_place(w, l) for w in big]) for l in range(L)]

    def weights(l):
        w_in_g, w_ab_g, w_sb_g, w_out_g, w_gu_g, w_down_g = gathered[l]
        return (w_in_g, w_ab_g, w_sb_g, w_out_g.reshape(D, D), w_gu_g, w_down_g.reshape(Fd, D))

    def small(l):
        return dict(
            g1=_row(norm1_g[l]), b_in=_row(b_in[l]), sink=_lane_row(sinks[l]),
            ln_g=_row(sgu_ln_g[l]), ln_b=_row(sgu_ln_b[l]),
            w_tril=jnp.where(tril[None], sgu_w[l], 0.0).astype(BF16),
            b_t=jnp.zeros((WINDOW, LANES), F32).at[:, :NG].set(sgu_b[l].T),
            g2=_row(norm2_g[l]))

    saved = []
    for l in range(L):
        w_in_g, w_ab_g, w_sb_g, w_out_g, w_gu_g, w_down_g = weights(l)
        sp = small(l)
        xn = _rms_fwd(h, sp["g1"])
        proj = _in_proj(xn, w_in_g, sp["b_in"], dm)
        y_attn, lse = _attn_fwd(proj, trig, sp["sink"], dm)
        y_sgu = _sgu_fwd(proj, sp["w_tril"], sp["b_t"], sp["ln_g"], sp["ln_b"], dm)
        a_attn = _branch_attn(y_attn, w_ab_g, dm)
        a_sgu, merged = _branch_sgu_merge(y_sgu, w_sb_g, a_attn, proj, dm)
        h_mid = _residual_matmul("out_proj", merged, w_out_g, h)
        hn = _rms_fwd(h_mid, sp["g2"])
        gu, act = _gate_up(hn, w_gu_g, dm)
        h_out = _residual_matmul("down_proj", act, w_down_g, h_mid)
        saved.append(dict(h=h, xn=xn, proj=proj, y_attn=y_attn, lse=lse, y_sgu=y_sgu, a_attn=a_attn, a_sgu=a_sgu,
                          merged=merged, h_mid=h_mid, hn=hn, gu=gu, act=act))
        h = h_out

    dh, dh_b, d_final, loss_part = _loss_head(h, _row(final_g), target)

    big_grads = [None] * L
    small_grads = [None] * L
    for l in reversed(range(L)):
        w_in_g, w_ab_g, w_sb_g, w_out_g, w_gu_g, w_down_g = weights(l)
        sp, sv = small(l), saved[l]
        dgu = _down_bwd(dh_b, w_down_g, sv["gu"], dm)
        g_down = _wgrad_rows("wgrad_down", sv["act"], dh_b)
        dhn = _gate_up_bwd(dgu, w_gu_g, dm)
        g_gu = _wgrad_gate_up(sv["hn"], dgu, dm)
        dh_mid, dh_mid_b, d_g2 = _rms_bwd(dhn, sv["h_mid"], sp["g2"], dh)
        da_attn, da_sgu, dgate = _out_bwd(dh_mid_b, w_out_g, sv["proj"], sv["a_attn"], sv["a_sgu"], dm)
        g_out = _wgrad_rows("wgrad_out", sv["merged"], dh_mid_b)
        dy_attn = _colsharded_bwd("branch_attn_bwd", da_attn, w_ab_g, BF16)
        dy_sgu = _colsharded_bwd("branch_sgu_bwd", da_sgu, w_sb_g, BF16)
        g_ab = _wgrad_cols("wgrad_attn_branch", sv["y_attn"], da_attn)
        g_sb = _wgrad_cols("wgrad_sgu_branch", sv["y_sgu"], da_sgu)
        dq, dk, dv, d_sink = _attn_bwd(sv["proj"], trig, sp["sink"], sv["y_attn"], sv["lse"], dy_attn, dm)
        dz, d_sgu_w, d_bt, d_lng, d_lnb = _sgu_bwd(sv["proj"], sp["w_tril"], sp["b_t"], sp["ln_g"], sp["ln_b"], dy_sgu, dm)
        dproj = jnp.concatenate([dq, dk, dv, dz, dgate[0], dgate[1]], axis=1)
        d_bin = _colsum(dproj)
        dxn = _colsharded_bwd("in_proj_bwd", dproj, w_in_g, F32)
        g_in = _wgrad_cols("wgrad_in", sv["xn"], dproj)
        dh, dh_b, d_g1 = _rms_bwd(dxn, sv["h"], sp["g1"], dh_mid)
        big_grads[l] = [g_in, g_ab, g_sb, g_out, g_gu, g_down]
        small_grads[l] = dict(norm1_g=d_g1[0], b_in=d_bin[0], sinks=d_sink[0, :NQ], sgu_ln_g=d_lng[0], sgu_ln_b=d_lnb[0],
                              sgu_w=d_sgu_w, sgu_b=d_bt[:, :NG].T, norm2_g=d_g2[0])
    grad_x = dh[None]

    reduced = [None] * L
    for l in range(L):
        land = _sibling_exchange(big_grads[l])
        sums, sends = zip(*[_pair_sum(g, d) for g, d in zip(big_grads[l], land)])
        got = _chip_exchange(list(sends))
        reduced[l] = _sibling_share([_chip_sum(s, g) for s, g in zip(sums, got)])
    big_out = [_adamw_stacked([reduced[l][t] for l in range(L)], big[t], big_m[t], big_v[t]) for t in range(len(big))]

    names = ["norm1_g", "b_in", "sinks", "sgu_ln_g", "sgu_ln_b", "sgu_w", "sgu_b", "norm2_g"]
    small_w = [norm1_g, b_in, sinks, sgu_ln_g, sgu_ln_b, sgu_w, sgu_b, norm2_g, final_g]
    small_m = [m_norm1_g, m_b_in, m_sinks, m_sgu_ln_g, m_sgu_ln_b, m_sgu_w, m_sgu_b, m_norm2_g, m_final_g]
    small_v = [v_norm1_g, v_b_in, v_sinks, v_sgu_ln_g, v_sgu_ln_b, v_sgu_w, v_sgu_b, v_norm2_g, v_final_g]
    small_g = [jnp.stack([small_grads[l][nm] for l in range(L)]) for nm in names] + [d_final[0]]
    sizes = [w.size for w in small_w]
    total = sum(sizes) + 1
    rows = -(-total // (8 * LANES)) * 8
    loss_piece = jnp.sum(loss_part[0]).reshape(1)
    packed_g = _pack(small_g + [loss_piece], rows)
    one = jnp.ones((1,), F32)
    parts = _gather_all(packed_g)
    outs = _adamw_small(parts, _pack(small_w + [one], rows), _pack(small_m + [one], rows), _pack(small_v + [one], rows))

    def unpack(p):
        flat = p.reshape(-1)
        res, off = [], 0
        for w, n in zip(small_w, sizes):
            res.append(flat[off:off + n].reshape(w.shape))
            off += n
        return res, flat[off]

    (sg, loss), (sd, _), (smm, _), (svv, _) = [unpack(o) for o in outs]

    order = ["norm1_g", "w_in", "b_in", "sinks", "sgu_ln_g", "sgu_ln_b", "sgu_w", "sgu_b", "w_attn_branch",
             "w_sgu_branch", "w_out", "norm2_g", "w_gate_up", "w_down", "final_g"]
    big_names = ["w_in", "w_attn_branch", "w_sgu_branch", "w_out", "w_gate_up", "w_down"]
    small_names = names + ["final_g"]

    def collect(kind):
        res = []
        for nm in order:
            if nm in big_names:
                res.append(big_out[big_names.index(nm)][kind])
            else:
                res.append((sg, sd, smm, svv)[kind][small_names.index(nm)])
        return res

    return (loss, grad_x, *collect(0), *collect(1), *collect(2), *collect(3))
```

```python
import math

import jax
import jax.numpy as jnp
from jax import lax
from jax.experimental import pallas as pl
from jax.experimental.pallas import tpu as pltpu

F32 = jnp.float32
BF16 = jnp.bfloat16
MESH = pl.DeviceIdType.MESH
ANY = pl.BlockSpec(memory_space=pl.ANY)

HEAD_DIM = 64
N_KV_HEADS = 4
WINDOW = 128
ROPE_DIM = HEAD_DIM // 4
ROPE_THETA = 500000.0
EPS = 1e-5
NEG = -1e30
N_CHIPS = 4
LANES = 128
V7X_VMEM_LIMIT = 56 * 1024 * 1024

ADAM_LR = 0.001
ADAM_B1 = 0.9
ADAM_B2 = 0.999
ADAM_EPS = 1e-08
ADAM_WD = 0.01
ADAM_STEP = 10

NN = (((1,), (0,)), ((), ()))
NT = (((1,), (1,)), ((), ()))
TN = (((0,), (0,)), ((), ()))


ROW_TILES = (1024, 512, 256, 128, 64, 32, 16, 8)
BLOCK_BYTES = 2 * 1024 * 1024


def _pick(n, prefs):
    for p in prefs:
        if n % p == 0:
            return p
    raise ValueError(f"no tile for {n} among {prefs}")


def _row_tile(rows, cols, itemsize=4):
    return _pick(rows, [t for t in ROW_TILES if t * cols * itemsize <= BLOCK_BYTES or t == ROW_TILES[-1]])


def _dot(a, b, dims):
    return lax.dot_general(a, b, dims, preferred_element_type=F32)


def _sigmoid(x):
    return 1.0 / (1.0 + jnp.exp(-x))


def _gelu(x):
    return 0.5 * x * (1.0 + lax.erf(x * (1.0 / math.sqrt(2.0))))


def _gelu_grad(x):
    return 0.5 * (1.0 + lax.erf(x * (1.0 / math.sqrt(2.0)))) + x * jnp.exp(-0.5 * x * x) * (1.0 / math.sqrt(2.0 * math.pi))


def _params(sem):
    return pltpu.CompilerParams(dimension_semantics=sem, vmem_limit_bytes=V7X_VMEM_LIMIT)


def _matmul(name, lhs, rhs_list, *, dims, grid, lhs_spec, rhs_specs, acc_shape, out_shape, out_specs,
            epilogue, extra=(), extra_specs=(), carry=None):
    gk = grid[2]
    nr, ne, no = len(rhs_list), len(extra), len(out_shape)
    nci = len(carry.ins) if carry else 0
    nco = len(carry.outs) if carry else 0
    nacc = nr if gk > 1 else 0

    def body(*refs):
        a_ref = refs[0]
        b_refs = refs[1:1 + nr]
        e_refs = refs[1 + nr:1 + nr + ne]
        base = 1 + nr + ne
        ci_refs = refs[base:base + nci]
        o_refs = refs[base + nci:base + nci + no]
        co_refs = refs[base + nci + no:base + nci + no + nco]
        acc_refs = refs[base + nci + no + nco:base + nci + no + nco + nacc]
        sems = refs[base + nci + no + nco + nacc:]
        ids = [pl.program_id(d) for d in range(3)]
        if carry:
            @pl.when((ids[0] == 0) & (ids[1] == 0) & (ids[2] == 0))
            def _():
                carry.start(ci_refs, co_refs, *sems)

        a = a_ref[...]
        parts = [_dot(a, b[...], dims) for b in b_refs]
        if gk == 1:
            epilogue(parts, e_refs, o_refs)
        else:
            k = ids[2]

            @pl.when(k == 0)
            def _():
                for acc, p in zip(acc_refs, parts):
                    acc[...] = p

            @pl.when(k > 0)
            def _():
                for acc, p in zip(acc_refs, parts):
                    acc[...] += p

            @pl.when(k == gk - 1)
            def _():
                epilogue([acc[...] for acc in acc_refs], e_refs, o_refs)

        if carry:
            @pl.when((ids[0] == grid[0] - 1) & (ids[1] == grid[1] - 1) & (ids[2] == grid[2] - 1))
            def _():
                carry.finish(ci_refs, co_refs, *sems)

    scratch = [pltpu.VMEM(acc_shape, F32) for _ in range(nacc)]
    kwargs = {}
    if carry:
        scratch += carry.sem_scratch()
        kwargs["input_output_aliases"] = {1 + nr + ne + i: no + o for i, o in carry.aliases.items()}
    outs = pl.pallas_call(
        body, name=name, grid=grid,
        in_specs=[lhs_spec, *rhs_specs, *extra_specs, *([ANY] * nci)],
        out_specs=[*out_specs, *([ANY] * nco)],
        out_shape=[*out_shape, *(carry.outs if carry else [])], scratch_shapes=scratch,
        compiler_params=_params(("arbitrary",) * 3 if carry else ("parallel", "parallel", "arbitrary")),
        **kwargs,
    )(lhs, *rhs_list, *extra, *(carry.ins if carry else []))
    return outs


class _Comm:
    def __init__(self, name, ins, outs, aliases, n_sems, start, finish):
        self.name, self.ins, self.outs, self.aliases, self.n_sems = name, list(ins), list(outs), dict(aliases), n_sems
        self.start, self.finish = start, finish

    def sem_scratch(self):
        return [pltpu.SemaphoreType.DMA((self.n_sems,)), pltpu.SemaphoreType.DMA((self.n_sems,))]

    def run(self):
        ni = len(self.ins)

        def body(*refs):
            in_refs, out_refs, sems = refs[:ni], refs[ni:ni + len(self.outs)], refs[ni + len(self.outs):]
            self.start(in_refs, out_refs, *sems)
            self.finish(in_refs, out_refs, *sems)

        return pl.pallas_call(
            body, name=self.name, in_specs=[ANY] * ni, out_specs=[ANY] * len(self.outs), out_shape=self.outs,
            input_output_aliases=self.aliases, scratch_shapes=self.sem_scratch(),
        )(*self.ins)


def _store_epilogue(dtype):
    def ep(parts, e_refs, o_refs):
        o_refs[0][...] = parts[0].astype(dtype)
    return ep


def _rms_fwd(h, g_row):
    S, D = h.shape
    tm = _row_tile(S, D)

    def body(h_ref, g_ref, o_ref):
        x = h_ref[...]
        r = lax.rsqrt(jnp.mean(x * x, axis=-1, keepdims=True) + EPS)
        o_ref[...] = (x * r * g_ref[...]).astype(BF16)

    return pl.pallas_call(
        body, name="rms_fwd", grid=(S // tm,),
        in_specs=[pl.BlockSpec((tm, D), lambda i: (i, 0)), pl.BlockSpec((1, D), lambda i: (0, 0))],
        out_specs=pl.BlockSpec((tm, D), lambda i: (i, 0)),
        out_shape=jax.ShapeDtypeStruct((S, D), BF16),
        compiler_params=_params(("parallel",)),
    )(h, g_row)


def _rms_bwd(dy, h, g_row, dres):
    S, D = h.shape
    tm = _row_tile(S, D)

    def body(dy_ref, h_ref, g_ref, dres_ref, dh_ref, dhb_ref, dg_ref):
        i = pl.program_id(0)
        x = h_ref[...]
        d = dy_ref[...]
        r = lax.rsqrt(jnp.mean(x * x, axis=-1, keepdims=True) + EPS)
        dg = d * g_ref[...]
        dot = jnp.mean(dg * x, axis=-1, keepdims=True)
        dh = dres_ref[...] + r * dg - x * (r * r * r) * dot
        dh_ref[...] = dh
        dhb_ref[...] = dh.astype(BF16)
        part = jnp.sum(d * x * r, axis=0, keepdims=True)

        @pl.when(i == 0)
        def _():
            dg_ref[...] = jnp.zeros_like(dg_ref)

        dg_ref[0:1, :] += part

    return pl.pallas_call(
        body, name="rms_bwd", grid=(S // tm,),
        in_specs=[pl.BlockSpec((tm, D), lambda i: (i, 0)), pl.BlockSpec((tm, D), lambda i: (i, 0)),
                  pl.BlockSpec((1, D), lambda i: (0, 0)), pl.BlockSpec((tm, D), lambda i: (i, 0))],
        out_specs=[pl.BlockSpec((tm, D), lambda i: (i, 0)), pl.BlockSpec((tm, D), lambda i: (i, 0)),
                   pl.BlockSpec((8, D), lambda i: (0, 0))],
        out_shape=[jax.ShapeDtypeStruct((S, D), F32), jax.ShapeDtypeStruct((S, D), BF16),
                   jax.ShapeDtypeStruct((8, D), F32)],
        compiler_params=_params(("arbitrary",)),
    )(dy, h, g_row, dres)


def _loss_head(h, g_row, target):
    S, D = h.shape
    tm = _row_tile(S, D)

    def body(h_ref, g_ref, t_ref, dh_ref, dhb_ref, dg_ref, loss_ref):
        i = pl.program_id(0)
        x = h_ref[...]
        g = g_ref[...]
        r = lax.rsqrt(jnp.mean(x * x, axis=-1, keepdims=True) + EPS)
        y = x * r * g
        e = y - t_ref[...]
        d = e * (1.0 / D)
        dg = d * g
        dot = jnp.mean(dg * x, axis=-1, keepdims=True)
        dh = r * dg - x * (r * r * r) * dot
        dh_ref[...] = dh
        dhb_ref[...] = dh.astype(BF16)

        @pl.when(i == 0)
        def _():
            dg_ref[...] = jnp.zeros_like(dg_ref)
            loss_ref[...] = jnp.zeros_like(loss_ref)

        dg_ref[0:1, :] += jnp.sum(d * x * r, axis=0, keepdims=True)
        loss_ref[0:1, :] += jnp.sum((0.5 / D) * e * e, axis=0, keepdims=True)

    return pl.pallas_call(
        body, name="loss_head", grid=(S // tm,),
        in_specs=[pl.BlockSpec((tm, D), lambda i: (i, 0)), pl.BlockSpec((1, D), lambda i: (0, 0)),
                  pl.BlockSpec((tm, D), lambda i: (i, 0))],
        out_specs=[pl.BlockSpec((tm, D), lambda i: (i, 0)), pl.BlockSpec((tm, D), lambda i: (i, 0)),
                   pl.BlockSpec((8, D), lambda i: (0, 0)), pl.BlockSpec((8, D), lambda i: (0, 0))],
        out_shape=[jax.ShapeDtypeStruct((S, D), F32), jax.ShapeDtypeStruct((S, D), BF16),
                   jax.ShapeDtypeStruct((8, D), F32), jax.ShapeDtypeStruct((8, D), F32)],
        compiler_params=_params(("arbitrary",)),
    )(h, g_row, target)


def _colsum(x):
    S, C = x.shape
    tm = _row_tile(S, C)

    def body(x_ref, o_ref):
        i = pl.program_id(0)

        @pl.when(i == 0)
        def _():
            o_ref[...] = jnp.zeros_like(o_ref)

        o_ref[0:1, :] += jnp.sum(x_ref[...].astype(F32), axis=0, keepdims=True)

    return pl.pallas_call(
        body, name="colsum", grid=(S // tm,),
        in_specs=[pl.BlockSpec((tm, C), lambda i: (i, 0))],
        out_specs=pl.BlockSpec((8, C), lambda i: (0, 0)),
        out_shape=jax.ShapeDtypeStruct((8, C), F32),
        compiler_params=_params(("arbitrary",)),
    )(x)


def _rope(t, cos, sa, sb):
    w = t.shape[-1]
    return t * cos + pltpu.roll(t, w - 8, 1) * sa + pltpu.roll(t, 8, 1) * sb


def _rope_t(g, cos, sa, sb):
    w = g.shape[-1]
    return g * cos + pltpu.roll(g * sa, 8, 1) + pltpu.roll(g * sb, w - 8, 1)


def _band_mask(n, qpk):
    qi = lax.broadcasted_iota(jnp.int32, (qpk * WINDOW, 2 * WINDOW), 0) & (WINDOW - 1)
    kj = lax.broadcasted_iota(jnp.int32, (qpk * WINDOW, 2 * WINDOW), 1)
    rel = qi + WINDOW - kj
    ok = (rel >= 0) & (rel < WINDOW)
    return ok & ((kj >= WINDOW) | (n > 0))


def _stack_heads(x, g, qpk):
    return jnp.concatenate([x[:, (g * qpk + hh) * HEAD_DIM:(g * qpk + hh + 1) * HEAD_DIM] for hh in range(qpk)], axis=0)


def _stack_cols(row, g, qpk):
    return jnp.concatenate([row[:, g * qpk + hh:g * qpk + hh + 1] for hh in range(qpk)], axis=0)


def _attn_specs(dm, nb):
    A, KV = dm["A"], dm["KV"]
    kb, vb = dm["OFF_K"] // KV, dm["OFF_V"] // KV
    cur = lambda n: jnp.minimum(n, nb - 1)
    prev = lambda n: jnp.maximum(jnp.minimum(n, nb - 1) - 1, 0)
    proj_specs = [
        pl.BlockSpec((WINDOW, A), lambda n: (cur(n), 0)),
        pl.BlockSpec((WINDOW, KV), lambda n: (prev(n), kb)),
        pl.BlockSpec((WINDOW, KV), lambda n: (cur(n), kb)),
        pl.BlockSpec((WINDOW, KV), lambda n: (prev(n), vb)),
        pl.BlockSpec((WINDOW, KV), lambda n: (cur(n), vb)),
    ]
    trig_cur = [pl.BlockSpec((WINDOW, LANES), lambda n: (cur(n), 0)) for _ in range(3)]
    trig_prev = [pl.BlockSpec((WINDOW, LANES), lambda n: (prev(n), 0)) for _ in range(3)]
    return proj_specs, trig_cur, trig_prev, cur, prev


def _attn_fwd(proj, trig, sink_row, dm):
    S = proj.shape[0]
    A, KV, NQ = dm["A"], dm["KV"], dm["NQ"]
    qpk = NQ // N_KV_HEADS
    nb = S // WINDOW
    scale = HEAD_DIM ** -0.5
    proj_specs, trig_cur, trig_prev, cur, _ = _attn_specs(dm, nb)

    def body(q_ref, kp_ref, kc_ref, vp_ref, vc_ref, cc_ref, sac_ref, sbc_ref, cp_ref, sap_ref, sbp_ref,
             sink_ref, y_ref, lse_ref):
        n = pl.program_id(0)
        tq = lambda r: jnp.tile(r[...], (1, A // LANES))
        tk = lambda rp, rc: jnp.tile(jnp.concatenate([rp[...], rc[...]], axis=0), (1, KV // LANES))
        qr = _rope(q_ref[...].astype(F32), tq(cc_ref), tq(sac_ref), tq(sbc_ref)).astype(BF16)
        kband = jnp.concatenate([kp_ref[...], kc_ref[...]], axis=0).astype(F32)
        kr = _rope(kband, tk(cp_ref, cc_ref), tk(sap_ref, sac_ref), tk(sbp_ref, sbc_ref)).astype(BF16)
        vband = jnp.concatenate([vp_ref[...], vc_ref[...]], axis=0)
        mask = _band_mask(n, qpk)
        lane = lax.broadcasted_iota(jnp.int32, (WINDOW, LANES), 1)
        lse_all = jnp.zeros((WINDOW, LANES), F32)
        sink_rows = jnp.broadcast_to(sink_ref[0:1, :], (WINDOW, LANES))
        for g in range(N_KV_HEADS):
            k_g = kr[:, g * HEAD_DIM:(g + 1) * HEAD_DIM]
            v_g = vband[:, g * HEAD_DIM:(g + 1) * HEAD_DIM]
            q_g = _stack_heads(qr, g, qpk)
            sink = _stack_cols(sink_rows, g, qpk)
            s = jnp.where(mask, _dot(q_g, k_g, NT) * scale, NEG)
            m = jnp.maximum(jnp.max(s, axis=-1, keepdims=True), sink)
            p = jnp.exp(s - m)
            den = jnp.sum(p, axis=-1, keepdims=True) + jnp.exp(sink - m)
            o = _dot(p.astype(BF16), v_g, NN) * (1.0 / den)
            lse_g = m + jnp.log(den)
            for hh in range(qpk):
                h = g * qpk + hh
                rows = slice(hh * WINDOW, (hh + 1) * WINDOW)
                y_ref[:, h * HEAD_DIM:(h + 1) * HEAD_DIM] = o[rows].astype(BF16)
                lse_all = jnp.where(lane == h, lse_g[rows], lse_all)
        lse_ref[...] = lse_all

    return pl.pallas_call(
        body, name="attn_fwd", grid=(nb,),
        in_specs=[*proj_specs, *trig_cur, *trig_prev, pl.BlockSpec((8, LANES), lambda n: (0, 0))],
        out_specs=[pl.BlockSpec((WINDOW, A), lambda n: (n, 0)), pl.BlockSpec((WINDOW, LANES), lambda n: (n, 0))],
        out_shape=[jax.ShapeDtypeStruct((S, A), BF16), jax.ShapeDtypeStruct((S, LANES), F32)],
        compiler_params=_params(("parallel",)),
    )(proj, proj, proj, proj, proj, *trig, *trig, sink_row)


def _attn_bwd(proj, trig, sink_row, y, lse, dy, dm):
    S = proj.shape[0]
    A, KV, NQ = dm["A"], dm["KV"], dm["NQ"]
    qpk = NQ // N_KV_HEADS
    nb = S // WINDOW
    scale = HEAD_DIM ** -0.5
    proj_specs, trig_cur, trig_prev, cur, prev = _attn_specs(dm, nb)

    def body(q_ref, kp_ref, kc_ref, vp_ref, vc_ref, cc_ref, sac_ref, sbc_ref, cp_ref, sap_ref, sbp_ref,
             sink_ref, y_ref, lse_ref, dy_ref, dq_ref, dk_ref, dv_ref, dsink_ref,
             ck_ref, cv_ref, bk_ref, bv_ref, dqr_ref):
        n = pl.program_id(0)

        @pl.when(n == 0)
        def _():
            dsink_ref[...] = jnp.zeros_like(dsink_ref)
            ck_ref[...] = jnp.zeros_like(ck_ref)
            cv_ref[...] = jnp.zeros_like(cv_ref)

        @pl.when(n < nb)
        def _():
            tq = lambda r: jnp.tile(r[...], (1, A // LANES))
            tk = lambda rp, rc: jnp.tile(jnp.concatenate([rp[...], rc[...]], axis=0), (1, KV // LANES))
            cq, saq, sbq = tq(cc_ref), tq(sac_ref), tq(sbc_ref)
            ck, sak, sbk = tk(cp_ref, cc_ref), tk(sap_ref, sac_ref), tk(sbp_ref, sbc_ref)
            qr = _rope(q_ref[...].astype(F32), cq, saq, sbq).astype(BF16)
            kband = jnp.concatenate([kp_ref[...], kc_ref[...]], axis=0).astype(F32)
            kr = _rope(kband, ck, sak, sbk).astype(BF16)
            vband = jnp.concatenate([vp_ref[...], vc_ref[...]], axis=0)
            mask = _band_mask(n, qpk)
            lane = lax.broadcasted_iota(jnp.int32, (1, LANES), 1)
            lse_all = lse_ref[...]
            sink_rows = jnp.broadcast_to(sink_ref[0:1, :], (WINDOW, LANES))
            dy_all = dy_ref[...]
            y_all = y_ref[...]
            dsink = jnp.zeros((1, LANES), F32)
            for g in range(N_KV_HEADS):
                k_g = kr[:, g * HEAD_DIM:(g + 1) * HEAD_DIM]
                v_g = vband[:, g * HEAD_DIM:(g + 1) * HEAD_DIM]
                q_g = _stack_heads(qr, g, qpk)
                dy_g = _stack_heads(dy_all, g, qpk)
                y_g = _stack_heads(y_all, g, qpk)
                lse_g = _stack_cols(lse_all, g, qpk)
                s = jnp.where(mask, _dot(q_g, k_g, NT) * scale, NEG)
                p = jnp.exp(s - lse_g)
                dp = _dot(dy_g, v_g, NT)
                delta = jnp.sum(dy_g.astype(F32) * y_g.astype(F32), axis=-1, keepdims=True)
                ds = (p * (dp - delta) * scale).astype(BF16)
                dq_g = _dot(ds, k_g, NN)
                bk_ref[:, g * HEAD_DIM:(g + 1) * HEAD_DIM] = _dot(ds, q_g, TN)
                bv_ref[:, g * HEAD_DIM:(g + 1) * HEAD_DIM] = _dot(p.astype(BF16), dy_g, TN)
                sink_d = jnp.exp(_stack_cols(sink_rows, g, qpk) - lse_g) * delta
                for hh in range(qpk):
                    h = g * qpk + hh
                    rows = slice(hh * WINDOW, (hh + 1) * WINDOW)
                    dqr_ref[:, h * HEAD_DIM:(h + 1) * HEAD_DIM] = dq_g[rows]
                    dsink = dsink + jnp.where(lane == h, -jnp.sum(sink_d[rows], axis=0, keepdims=True), 0.0)
            dsink_ref[0:1, :] += dsink
            dq_ref[...] = _rope_t(dqr_ref[...], cq, saq, sbq).astype(BF16)
            dkb = _rope_t(bk_ref[...], ck, sak, sbk)
            dvb = bv_ref[...]
            dk_ref[...] = (ck_ref[...] + dkb[:WINDOW]).astype(BF16)
            dv_ref[...] = (cv_ref[...] + dvb[:WINDOW]).astype(BF16)
            ck_ref[...] = dkb[WINDOW:]
            cv_ref[...] = dvb[WINDOW:]

        @pl.when(n == nb)
        def _():
            dk_ref[...] = ck_ref[...].astype(BF16)
            dv_ref[...] = cv_ref[...].astype(BF16)

    row = lambda w: pl.BlockSpec((WINDOW, w), lambda n: (cur(n), 0))
    done = lambda w: pl.BlockSpec((WINDOW, w), lambda n: (jnp.maximum(n - 1, 0), 0))
    return pl.pallas_call(
        body, name="attn_bwd", grid=(nb + 1,),
        in_specs=[*proj_specs, *trig_cur, *trig_prev, pl.BlockSpec((8, LANES), lambda n: (0, 0)),
                  row(A), row(LANES), row(A)],
        out_specs=[row(A), done(KV), done(KV), pl.BlockSpec((8, LANES), lambda n: (0, 0))],
        out_shape=[jax.ShapeDtypeStruct((S, A), BF16), jax.ShapeDtypeStruct((S, KV), BF16),
                   jax.ShapeDtypeStruct((S, KV), BF16), jax.ShapeDtypeStruct((8, LANES), F32)],
        scratch_shapes=[pltpu.VMEM((WINDOW, KV), F32), pltpu.VMEM((WINDOW, KV), F32),
                        pltpu.VMEM((2 * WINDOW, KV), F32), pltpu.VMEM((2 * WINDOW, KV), F32),
                        pltpu.VMEM((WINDOW, A), F32)],
        compiler_params=_params(("arbitrary",)),
    )(proj, proj, proj, proj, proj, *trig, *trig, sink_row, y, lse, dy)


def _sgu_layout(dm, S):
    G = dm["G"]
    pw = math.gcd(dm["OFF_Z"], G)
    npc = G // pw
    tm = _pick(S, (256, 128))
    u_specs = [pl.BlockSpec((tm, pw), lambda i, p=p: (i, dm["OFF_Z"] // pw + p)) for p in range(npc)]
    v_specs = [pl.BlockSpec((tm, pw), lambda i, p=p: (i, (dm["OFF_Z"] + G) // pw + p)) for p in range(npc)]
    return pw, npc, tm, u_specs, v_specs


def _sgu_norm(v_refs, lg_ref, lb_ref):
    v = jnp.concatenate([_gelu(r[...].astype(F32)) for r in v_refs], axis=1)
    mu = jnp.mean(v, axis=-1, keepdims=True)
    vc = v - mu
    rstd = lax.rsqrt(jnp.mean(vc * vc, axis=-1, keepdims=True) + EPS)
    xhat = vc * rstd
    return xhat, rstd, (xhat * lg_ref[...] + lb_ref[...]).astype(BF16)


def _sgu_fwd(proj, w_tril, b_t, ln_g_row, ln_b_row, dm):
    S = proj.shape[0]
    G, NG = dm["G"], dm["NG"]
    pw, npc, tm, u_specs, v_specs = _sgu_layout(dm, S)
    nch = tm // WINDOW

    def body(*refs):
        u_refs, v_refs = refs[:npc], refs[npc:2 * npc]
        w_ref, bt_ref, lg_ref, lb_ref, y_ref = refs[2 * npc:]
        _, _, vn = _sgu_norm(v_refs, lg_ref, lb_ref)
        u = jnp.concatenate([_gelu(r[...].astype(F32)) for r in u_refs], axis=1)
        for c in range(nch):
            rows = slice(c * WINDOW, (c + 1) * WINDOW)
            for g in range(NG):
                cols = slice(g * LANES, (g + 1) * LANES)
                sv = _dot(w_ref[g], vn[rows, cols], NN) + bt_ref[:, g:g + 1]
                y_ref[rows, cols] = (u[rows, cols] * sv).astype(BF16)

    return pl.pallas_call(
        body, name="sgu_fwd", grid=(S // tm,),
        in_specs=[*u_specs, *v_specs,
                  pl.BlockSpec((NG, WINDOW, WINDOW), lambda i: (0, 0, 0)),
                  pl.BlockSpec((WINDOW, LANES), lambda i: (0, 0)),
                  pl.BlockSpec((1, G), lambda i: (0, 0)), pl.BlockSpec((1, G), lambda i: (0, 0))],
        out_specs=pl.BlockSpec((tm, G), lambda i: (i, 0)),
        out_shape=jax.ShapeDtypeStruct((S, G), BF16),
        compiler_params=_params(("parallel",)),
    )(*([proj] * (2 * npc)), w_tril, b_t, ln_g_row, ln_b_row)


def _sgu_bwd(proj, w_tril, b_t, ln_g_row, ln_b_row, dy, dm):
    S = proj.shape[0]
    G, NG = dm["G"], dm["NG"]
    pw, npc, tm, u_specs, v_specs = _sgu_layout(dm, S)
    nch = tm // WINDOW

    def body(*refs):
        u_refs, v_refs = refs[:npc], refs[npc:2 * npc]
        w_ref, bt_ref, lg_ref, lb_ref, dy_ref, dz_ref, dw_ref, dbt_ref, dlg_ref, dlb_ref, dvn_ref = refs[2 * npc:]
        i = pl.program_id(0)

        @pl.when(i == 0)
        def _():
            dw_ref[...] = jnp.zeros_like(dw_ref)
            dbt_ref[...] = jnp.zeros_like(dbt_ref)
            dlg_ref[...] = jnp.zeros_like(dlg_ref)
            dlb_ref[...] = jnp.zeros_like(dlb_ref)

        xhat, rstd, vn = _sgu_norm(v_refs, lg_ref, lb_ref)
        u_pre = jnp.concatenate([r[...].astype(F32) for r in u_refs], axis=1)
        u = _gelu(u_pre)
        dy = dy_ref[...].astype(F32)
        lane = lax.broadcasted_iota(jnp.int32, (WINDOW, LANES), 1)
        tri = lax.broadcasted_iota(jnp.int32, (WINDOW, WINDOW), 0) >= lax.broadcasted_iota(jnp.int32, (WINDOW, WINDOW), 1)
        dbt = jnp.zeros((WINDOW, LANES), F32)
        for c in range(nch):
            rows = slice(c * WINDOW, (c + 1) * WINDOW)
            for g in range(NG):
                cols = slice(g * LANES, (g + 1) * LANES)
                vn_cg = vn[rows, cols]
                sv = _dot(w_ref[g], vn_cg, NN) + bt_ref[:, g:g + 1]
                dy_cg = dy[rows, cols]
                dsv = dy_cg * u[rows, cols]
                dsv_b = dsv.astype(BF16)
                dz_ref[rows, cols] = (dy_cg * sv * _gelu_grad(u_pre[rows, cols])).astype(BF16)
                dvn_ref[rows, cols] = _dot(w_ref[g], dsv_b, TN)
                dw_ref[g] += jnp.where(tri, _dot(dsv_b, vn_cg, NT), 0.0)
                dbt = dbt + jnp.where(lane == g, jnp.sum(dsv, axis=-1, keepdims=True), 0.0)
        dbt_ref[...] += dbt
        dvn = dvn_ref[...]
        dlg_ref[0:1, :] += jnp.sum(dvn * xhat, axis=0, keepdims=True)
        dlb_ref[0:1, :] += jnp.sum(dvn, axis=0, keepdims=True)
        dxh = dvn * lg_ref[...]
        dv = rstd * (dxh - jnp.mean(dxh, axis=-1, keepdims=True) - xhat * jnp.mean(dxh * xhat, axis=-1, keepdims=True))
        v_pre = jnp.concatenate([r[...].astype(F32) for r in v_refs], axis=1)
        dz_ref[:, G:] = (dv * _gelu_grad(v_pre)).astype(BF16)

    return pl.pallas_call(
        body, name="sgu_bwd", grid=(S // tm,),
        in_specs=[*u_specs, *v_specs,
                  pl.BlockSpec((NG, WINDOW, WINDOW), lambda i: (0, 0, 0)),
                  pl.BlockSpec((WINDOW, LANES), lambda i: (0, 0)),
                  pl.BlockSpec((1, G), lambda i: (0, 0)), pl.BlockSpec((1, G), lambda i: (0, 0)),
                  pl.BlockSpec((tm, G), lambda i: (i, 0))],
        out_specs=[pl.BlockSpec((tm, 2 * G), lambda i: (i, 0)),
                   pl.BlockSpec((NG, WINDOW, WINDOW), lambda i: (0, 0, 0)),
                   pl.BlockSpec((WINDOW, LANES), lambda i: (0, 0)),
                   pl.BlockSpec((8, G), lambda i: (0, 0)), pl.BlockSpec((8, G), lambda i: (0, 0))],
        out_shape=[jax.ShapeDtypeStruct((S, 2 * G), BF16), jax.ShapeDtypeStruct((NG, WINDOW, WINDOW), F32),
                   jax.ShapeDtypeStruct((WINDOW, LANES), F32), jax.ShapeDtypeStruct((8, G), F32),
                   jax.ShapeDtypeStruct((8, G), F32)],
        scratch_shapes=[pltpu.VMEM((tm, G), F32)],
        compiler_params=_params(("arbitrary",)),
    )(*([proj] * (2 * npc)), w_tril, b_t, ln_g_row, ln_b_row, dy)


def _result(outs, n_main, carry):
    main = outs[0] if n_main == 1 else tuple(outs[:n_main])
    return (main, list(outs[n_main:])) if carry else main


def _in_proj(xn, w_in_g, b_row, dm, carry=None):
    S, D = xn.shape
    IN = dm["IN"]
    cw = IN // N_CHIPS
    tm = _pick(S, (512, 256, 128))
    tn = _pick(cw, (1920, 640, 512, 256, 128))
    nbc = cw // tn

    def ep(parts, e_refs, o_refs):
        o_refs[0][...] = (parts[0] + e_refs[0][...]).astype(BF16)

    return _result(_matmul(
        "in_proj", xn, [w_in_g], dims=NN, grid=(S // tm, IN // tn, 1),
        lhs_spec=pl.BlockSpec((tm, D), lambda i, j, k: (i, 0)),
        rhs_specs=[pl.BlockSpec((None, D, tn), lambda i, j, k: (j // nbc, 0, j % nbc))],
        acc_shape=(tm, tn), extra=[b_row], extra_specs=[pl.BlockSpec((1, tn), lambda i, j, k: (0, j))],
        out_shape=[jax.ShapeDtypeStruct((S, IN), BF16)],
        out_specs=[pl.BlockSpec((tm, tn), lambda i, j, k: (i, j))], epilogue=ep, carry=carry), 1, carry)


def _branch_attn(y_attn, w_ab_g, dm):
    S, A = y_attn.shape
    D = dm["D"]
    cw = D // N_CHIPS
    tm = _pick(S, (1024, 512, 256, 128))
    return _matmul(
        "branch_attn", y_attn, [w_ab_g], dims=NN, grid=(S // tm, N_CHIPS, 1),
        lhs_spec=pl.BlockSpec((tm, A), lambda i, j, k: (i, 0)),
        rhs_specs=[pl.BlockSpec((None, A, cw), lambda i, j, k: (j, 0, 0))],
        acc_shape=(tm, cw), out_shape=[jax.ShapeDtypeStruct((S, D), BF16)],
        out_specs=[pl.BlockSpec((tm, cw), lambda i, j, k: (i, j))], epilogue=_store_epilogue(BF16))[0]


def _branch_sgu_merge(y_sgu, w_sb_g, a_attn, proj, dm):
    S, G = y_sgu.shape
    D, OFF_G = dm["D"], dm["OFF_G"]
    cw = D // N_CHIPS
    tm = _pick(S, (1024, 512, 256, 128))

    def ep(parts, e_refs, o_refs):
        a_sgu = parts[0].astype(BF16)
        ga = _sigmoid(e_refs[1][...].astype(F32))
        gs = _sigmoid(e_refs[2][...].astype(F32))
        o_refs[0][...] = a_sgu
        o_refs[1][...] = (ga * e_refs[0][...].astype(F32) + gs * a_sgu.astype(F32)).astype(BF16)

    blk = pl.BlockSpec((tm, cw), lambda i, j, k: (i, j))
    return _matmul(
        "branch_sgu_merge", y_sgu, [w_sb_g], dims=NN, grid=(S // tm, N_CHIPS, 1),
        lhs_spec=pl.BlockSpec((tm, G), lambda i, j, k: (i, 0)),
        rhs_specs=[pl.BlockSpec((None, G, cw), lambda i, j, k: (j, 0, 0))],
        acc_shape=(tm, cw), extra=[a_attn, proj, proj],
        extra_specs=[blk, pl.BlockSpec((tm, cw), lambda i, j, k: (i, OFF_G // cw + j)),
                     pl.BlockSpec((tm, cw), lambda i, j, k: (i, (OFF_G + D) // cw + j))],
        out_shape=[jax.ShapeDtypeStruct((S, D), BF16), jax.ShapeDtypeStruct((S, D), BF16)],
        out_specs=[blk, blk], epilogue=ep)


def _residual_matmul(name, a, w_g, h, carry=None):
    S, K = a.shape
    D = w_g.shape[1]
    tm = _pick(S, (512, 256, 128))
    tn = _pick(D, (512, 256, 128))

    def ep(parts, e_refs, o_refs):
        o_refs[0][...] = e_refs[0][...] + parts[0]

    blk = pl.BlockSpec((tm, tn), lambda i, j, k: (i, j))
    return _result(_matmul(
        name, a, [w_g], dims=NN, grid=(S // tm, D // tn, 1),
        lhs_spec=pl.BlockSpec((tm, K), lambda i, j, k: (i, 0)),
        rhs_specs=[pl.BlockSpec((K, tn), lambda i, j, k: (0, j))],
        acc_shape=(tm, tn), extra=[h], extra_specs=[blk],
        out_shape=[jax.ShapeDtypeStruct((S, D), F32)], out_specs=[blk], epilogue=ep, carry=carry), 1, carry)


def _gate_up(hn, w_gu_g, dm, carry=None):
    S, D = hn.shape
    Fd = dm["F"]
    cw = 2 * Fd // N_CHIPS
    tm = _pick(S, (512, 256, 128))
    tn = _pick(cw, (1408, 512, 384, 256, 128))
    nbc = cw // tn
    half = N_CHIPS // 2

    def ep(parts, e_refs, o_refs):
        gate, up = parts[0].astype(BF16), parts[1].astype(BF16)
        o_refs[0][0] = gate
        o_refs[0][1] = up
        g32 = gate.astype(F32)
        o_refs[1][...] = (g32 * _sigmoid(g32) * up.astype(F32)).astype(BF16)

    return _result(_matmul(
        "gate_up", hn, [w_gu_g, w_gu_g], dims=NN, grid=(S // tm, Fd // tn, 1),
        lhs_spec=pl.BlockSpec((tm, D), lambda i, j, k: (i, 0)),
        rhs_specs=[pl.BlockSpec((None, D, tn), lambda i, j, k: (j // nbc, 0, j % nbc)),
                   pl.BlockSpec((None, D, tn), lambda i, j, k: (half + j // nbc, 0, j % nbc))],
        acc_shape=(tm, tn),
        out_shape=[jax.ShapeDtypeStruct((2, S, Fd), BF16), jax.ShapeDtypeStruct((S, Fd), BF16)],
        out_specs=[pl.BlockSpec((2, tm, tn), lambda i, j, k: (0, i, j)), pl.BlockSpec((tm, tn), lambda i, j, k: (i, j))],
        epilogue=ep, carry=carry), 2, carry)


def _down_bwd(dh_b, w_down_g, gu, dm, carry=None):
    S, D = dh_b.shape
    Fd = dm["F"]
    tm = _pick(S, (1024, 512, 256, 128))
    tn = _pick(Fd, (512, 256, 128))

    def ep(parts, e_refs, o_refs):
        gate = e_refs[0][0].astype(F32)
        up = e_refs[0][1].astype(F32)
        s = _sigmoid(gate)
        dact = parts[0]
        o_refs[0][0] = (dact * up * s * (1.0 + gate * (1.0 - s))).astype(BF16)
        o_refs[0][1] = (dact * gate * s).astype(BF16)

    blk = pl.BlockSpec((2, tm, tn), lambda i, j, k: (0, i, j))
    return _result(_matmul(
        "down_bwd", dh_b, [w_down_g], dims=NT, grid=(S // tm, Fd // tn, 1),
        lhs_spec=pl.BlockSpec((tm, D), lambda i, j, k: (i, 0)),
        rhs_specs=[pl.BlockSpec((tn, D), lambda i, j, k: (j, 0))],
        acc_shape=(tm, tn), extra=[gu], extra_specs=[blk],
        out_shape=[jax.ShapeDtypeStruct((2, S, Fd), BF16)], out_specs=[blk], epilogue=ep, carry=carry), 1, carry)


def _gate_up_bwd(dgu, w_gu_g, dm, carry=None):
    S = dgu.shape[1]
    D, Fd = dm["D"], dm["F"]
    cw = 2 * Fd // N_CHIPS
    half = N_CHIPS // 2
    tm = _pick(S, (512, 256, 128))
    tn = _pick(D, (1024, 512, 256, 128))
    return _result(_matmul(
        "gate_up_bwd", dgu, [w_gu_g], dims=NT, grid=(S // tm, D // tn, N_CHIPS),
        lhs_spec=pl.BlockSpec((None, tm, cw), lambda i, j, k: (k // half, i, k % half)),
        rhs_specs=[pl.BlockSpec((None, tn, cw), lambda i, j, k: (k, j, 0))],
        acc_shape=(tm, tn), out_shape=[jax.ShapeDtypeStruct((S, D), F32)],
        out_specs=[pl.BlockSpec((tm, tn), lambda i, j, k: (i, j))], epilogue=_store_epilogue(F32), carry=carry), 1, carry)


def _out_bwd(dh_b, w_out_g, proj, a_attn, a_sgu, dm):
    S, D = dh_b.shape
    OFF_G = dm["OFF_G"]
    tm = _pick(S, (1024, 512, 256, 128))
    tn = D // N_CHIPS

    def ep(parts, e_refs, o_refs):
        dm_ = parts[0]
        ga = _sigmoid(e_refs[0][...].astype(F32))
        gs = _sigmoid(e_refs[1][...].astype(F32))
        o_refs[0][...] = (dm_ * ga).astype(BF16)
        o_refs[1][...] = (dm_ * gs).astype(BF16)
        o_refs[2][0] = (dm_ * e_refs[2][...].astype(F32) * ga * (1.0 - ga)).astype(BF16)
        o_refs[2][1] = (dm_ * e_refs[3][...].astype(F32) * gs * (1.0 - gs)).astype(BF16)

    blk = pl.BlockSpec((tm, tn), lambda i, j, k: (i, j))
    return _matmul(
        "out_bwd", dh_b, [w_out_g], dims=NT, grid=(S // tm, D // tn, 1),
        lhs_spec=pl.BlockSpec((tm, D), lambda i, j, k: (i, 0)),
        rhs_specs=[pl.BlockSpec((tn, D), lambda i, j, k: (j, 0))],
        acc_shape=(tm, tn), extra=[proj, proj, a_attn, a_sgu],
        extra_specs=[pl.BlockSpec((tm, tn), lambda i, j, k: (i, OFF_G // tn + j)),
                     pl.BlockSpec((tm, tn), lambda i, j, k: (i, (OFF_G + D) // tn + j)), blk, blk],
        out_shape=[jax.ShapeDtypeStruct((S, D), BF16), jax.ShapeDtypeStruct((S, D), BF16),
                   jax.ShapeDtypeStruct((2, S, D), BF16)],
        out_specs=[blk, blk, pl.BlockSpec((2, tm, tn), lambda i, j, k: (0, i, j))], epilogue=ep)


def _colsharded_bwd(name, dy, w_g, out_dtype):
    S = dy.shape[0]
    _, K, cw = w_g.shape
    tm = _pick(S, (512, 256, 128))
    tn = _pick(K, (1024, 512, 256, 128))
    return _matmul(
        name, dy, [w_g], dims=NT, grid=(S // tm, K // tn, N_CHIPS),
        lhs_spec=pl.BlockSpec((tm, cw), lambda i, j, k: (i, k)),
        rhs_specs=[pl.BlockSpec((None, tn, cw), lambda i, j, k: (k, j, 0))],
        acc_shape=(tm, tn), out_shape=[jax.ShapeDtypeStruct((S, K), out_dtype)],
        out_specs=[pl.BlockSpec((tm, tn), lambda i, j, k: (i, j))], epilogue=_store_epilogue(out_dtype))[0]


def _wgrad_cols(name, x, dy):
    S, R = x.shape
    C = dy.shape[1]
    cw = C // N_CHIPS
    tm = _pick(R, (512, 256, 128) if cw >= 1024 else (1024, 512, 256, 128))
    tk = _pick(S, (2048, 1024, 512, 256, 128))
    return _matmul(
        name, x, [dy], dims=TN, grid=(R // tm, N_CHIPS, S // tk),
        lhs_spec=pl.BlockSpec((tk, tm), lambda i, j, k: (k, i)),
        rhs_specs=[pl.BlockSpec((tk, cw), lambda i, j, k: (k, j))],
        acc_shape=(tm, cw), out_shape=[jax.ShapeDtypeStruct((N_CHIPS, R, cw), F32)],
        out_specs=[pl.BlockSpec((None, tm, cw), lambda i, j, k: (j, i, 0))], epilogue=_store_epilogue(F32))[0]


def _wgrad_gate_up(hn, dgu, dm, carry=None):
    S, D = hn.shape
    Fd = dm["F"]
    cw = 2 * Fd // N_CHIPS
    half = N_CHIPS // 2
    tm = _pick(D, (512, 256, 128))
    tk = _pick(S, (1024, 512, 256, 128))
    return _result(_matmul(
        "wgrad_gate_up", hn, [dgu], dims=TN, grid=(D // tm, N_CHIPS, S // tk),
        lhs_spec=pl.BlockSpec((tk, tm), lambda i, j, k: (k, i)),
        rhs_specs=[pl.BlockSpec((None, tk, cw), lambda i, j, k: (j // half, k, j % half))],
        acc_shape=(tm, cw), out_shape=[jax.ShapeDtypeStruct((N_CHIPS, D, cw), F32)],
        out_specs=[pl.BlockSpec((None, tm, cw), lambda i, j, k: (j, i, 0))], epilogue=_store_epilogue(F32),
        carry=carry), 1, carry)


def _wgrad_rows(name, x, dy):
    S, R = x.shape
    C = dy.shape[1]
    rw = R // N_CHIPS
    tn = _pick(C, (1024, 512, 256, 128))
    tk = _pick(S, (1024, 512, 256, 128))
    return _matmul(
        name, x, [dy], dims=TN, grid=(N_CHIPS, C // tn, S // tk),
        lhs_spec=pl.BlockSpec((tk, rw), lambda i, j, k: (k, i)),
        rhs_specs=[pl.BlockSpec((tk, tn), lambda i, j, k: (k, j))],
        acc_shape=(rw, tn), out_shape=[jax.ShapeDtypeStruct((N_CHIPS, rw, C), F32)],
        out_specs=[pl.BlockSpec((None, rw, tn), lambda i, j, k: (i, 0, j))], epilogue=_store_epilogue(F32))[0]


def _place():
    x, y, c = lax.axis_index("x"), lax.axis_index("y"), lax.axis_index("c")
    others = [(1 - x, y), (x, 1 - y), (1 - x, 1 - y)]
    return x, y, c, others


def _chip_index(chip):
    return 2 * chip[0] + chip[1]


def _gather_weights(bufs):
    n = len(bufs)

    def copies(src, out, send_sems, recv_sems):
        x, y, c, others = _place()

        def half(ref, chip_idx, hc):
            r2 = ref.shape[1] // 2
            return ref.at[chip_idx, pl.ds(hc * r2, r2), :]

        def copy(t, k, chip, hc, to):
            return pltpu.make_async_remote_copy(
                src_ref=half(src[t], _chip_index(chip), hc), dst_ref=half(out[t], _chip_index(chip), hc),
                send_sem=send_sems.at[6 * t + k], recv_sem=recv_sems.at[6 * t + k],
                device_id=to, device_id_type=MESH)

        me, sibling = (x, y, c), (x, y, 1 - c)
        pairs = [(t, j, chip) for t in range(n) for j, chip in enumerate(others)]
        sent = [copy(t, j, (x, y), c, (*chip, c)) for t, j, chip in pairs]
        landed = [copy(t, j, chip, c, me) for t, j, chip in pairs]
        passed = [copy(t, 3 + j, chip, c, sibling) for t, j, chip in pairs]
        handed = [copy(t, 3 + j, chip, 1 - c, me) for t, j, chip in pairs]
        return sent, landed, passed, handed

    def start(src, out, send_sems, recv_sems):
        for cp in copies(src, out, send_sems, recv_sems)[0]:
            cp.start()

    def finish(src, out, send_sems, recv_sems):
        sent, landed, passed, handed = copies(src, out, send_sems, recv_sems)
        for arrival, forward in zip(landed, passed):
            arrival.wait_recv()
            forward.start()
        for cp in handed:
            cp.wait_recv()
        for cp in sent + passed:
            cp.wait_send()

    return _Comm("gather_weights", bufs, [jax.ShapeDtypeStruct(b.shape, BF16) for b in bufs],
                 {t: t for t in range(n)}, 6 * n, start, finish)


def _sibling_exchange(grads):
    n = len(grads)
    shapes = [g.shape for g in grads]

    def copies(src, land, send_sems, recv_sems):
        x, y, c, _ = _place()
        res = []
        for t in range(n):
            r2 = shapes[t][1] // 2
            res.append(pltpu.make_async_remote_copy(
                src_ref=src[t].at[:, pl.ds((1 - c) * r2, r2), :], dst_ref=land[t],
                send_sem=send_sems.at[t], recv_sem=recv_sems.at[t], device_id=(x, y, 1 - c), device_id_type=MESH))
        return res

    def start(*refs):
        for cp in copies(*refs):
            cp.start()

    def finish(*refs):
        remote = copies(*refs)
        for cp in remote:
            cp.wait_recv()
        for cp in remote:
            cp.wait_send()

    return _Comm("sibling_exchange", grads, [jax.ShapeDtypeStruct((s[0], s[1] // 2, s[2]), F32) for s in shapes],
                 {}, n, start, finish)


def _chip_exchange(sends):
    n = len(sends)
    shapes = [s.shape for s in sends]

    def copies(snd, got, send_sems, recv_sems):
        x, y, c, others = _place()
        return [pltpu.make_async_remote_copy(
            src_ref=snd[t].at[_chip_index(chip)], dst_ref=got[t].at[j],
            send_sem=send_sems.at[3 * t + j], recv_sem=recv_sems.at[3 * t + j],
            device_id=(*chip, c), device_id_type=MESH) for t in range(n) for j, chip in enumerate(others)]

    def start(*refs):
        for cp in copies(*refs):
            cp.start()

    def finish(*refs):
        remote = copies(*refs)
        for cp in remote:
            cp.wait_recv()
        for cp in remote:
            cp.wait_send()

    return _Comm("chip_exchange", sends, [jax.ShapeDtypeStruct((3, s[1], s[2]), BF16) for s in shapes],
                 {}, 3 * n, start, finish)


def _sibling_share(fulls):
    n = len(fulls)
    shapes = [f.shape for f in fulls]

    def copies(src, out, send_sems, recv_sems, mine):
        x, y, c, _ = _place()
        hc = c if mine else 1 - c
        res = []
        for t in range(n):
            r2 = shapes[t][0] // 2
            res.append(pltpu.make_async_remote_copy(
                src_ref=src[t].at[pl.ds(hc * r2, r2), :], dst_ref=out[t].at[pl.ds(hc * r2, r2), :],
                send_sem=send_sems.at[t], recv_sem=recv_sems.at[t], device_id=(x, y, 1 - c), device_id_type=MESH))
        return res

    def start(*refs):
        for cp in copies(*refs, mine=True):
            cp.start()

    def finish(*refs):
        for cp in copies(*refs, mine=False):
            cp.wait_recv()
        for cp in copies(*refs, mine=True):
            cp.wait_send()

    return _Comm("sibling_share", fulls, [jax.ShapeDtypeStruct(s, F32) for s in shapes],
                 {t: t for t in range(n)}, n, start, finish)


def _gather_all(v):
    R, C = v.shape

    def body(v_ref, out_ref, send_sems, recv_sems, local_sem):
        x, y, c, others = _place()
        me, sibling = (x, y, c), (x, y, 1 - c)

        def rows(px, py, pc):
            return out_ref.at[4 * px + 2 * py + pc]

        def copy(k, block, to, src=None):
            return pltpu.make_async_remote_copy(
                src_ref=rows(*block) if src is None else src, dst_ref=rows(*block),
                send_sem=send_sems.at[k], recv_sem=recv_sems.at[k], device_id=to, device_id_type=MESH)

        mine = pltpu.make_async_copy(v_ref, rows(*me), local_sem)
        mine.start()
        first = [copy(0, me, sibling, src=v_ref)]
        first += [copy(1 + j, me, (*chip, c), src=v_ref) for j, chip in enumerate(others)]
        for cp in first:
            cp.start()
        passed = [copy(4 + j, (*chip, c), sibling) for j, chip in enumerate(others)]
        for j, chip in enumerate(others):
            copy(1 + j, (*chip, c), me).wait_recv()
            passed[j].start()
        copy(0, sibling, me).wait_recv()
        for j, chip in enumerate(others):
            copy(4 + j, (*chip, 1 - c), me).wait_recv()
        for cp in first + passed:
            cp.wait_send()
        mine.wait()

    return pl.pallas_call(
        body, name="gather_all", in_specs=[ANY], out_specs=ANY,
        out_shape=jax.ShapeDtypeStruct((8, R, C), F32),
        scratch_shapes=[pltpu.SemaphoreType.DMA((7,)), pltpu.SemaphoreType.DMA((7,)), pltpu.SemaphoreType.DMA],
    )(v)


def _my_chip():
    return 2 * lax.axis_index("x") + lax.axis_index("y")


def _my_core():
    return lax.axis_index("c")


def _pair_sum(grad, land):
    K, R2, C = land.shape
    tm = _row_tile(R2, C)
    nrb = R2 // tm

    def body(a_ref, b_ref, s_ref, sb_ref):
        s = a_ref[...] + b_ref[...]
        s_ref[...] = s
        sb_ref[...] = s.astype(BF16)

    blk = pl.BlockSpec((None, tm, C), lambda k, r: (k, r, 0))
    return pl.pallas_call(
        body, name="pair_sum", grid=(K, nrb),
        in_specs=[pl.BlockSpec((None, tm, C), lambda k, r: (k, _my_core() * nrb + r, 0)), blk],
        out_specs=[blk, blk],
        out_shape=[jax.ShapeDtypeStruct((K, R2, C), F32), jax.ShapeDtypeStruct((K, R2, C), BF16)],
        compiler_params=_params(("parallel", "parallel")),
    )(grad, land)


def _chip_sum(sums, got):
    _, R2, C = sums.shape
    tm = _row_tile(R2, C)
    nrb = R2 // tm

    def body(o_ref, g_ref, s_ref):
        s_ref[...] = ((o_ref[...] + g_ref[0].astype(F32)) + g_ref[1].astype(F32)) + g_ref[2].astype(F32)

    return pl.pallas_call(
        body, name="chip_sum", grid=(nrb,),
        in_specs=[pl.BlockSpec((None, tm, C), lambda r: (_my_chip(), r, 0)),
                  pl.BlockSpec((3, tm, C), lambda r: (0, r, 0))],
        out_specs=pl.BlockSpec((tm, C), lambda r: (_my_core() * nrb + r, 0)),
        out_shape=jax.ShapeDtypeStruct((2 * R2, C), F32),
        compiler_params=_params(("parallel",)),
    )(sums, got)


def _adamw_math(w, g, m, v):
    m = ADAM_B1 * m + (1.0 - ADAM_B1) * g
    v = ADAM_B2 * v + (1.0 - ADAM_B2) * (g * g)
    m_hat = m / (1.0 - ADAM_B1 ** ADAM_STEP)
    v_hat = v / (1.0 - ADAM_B2 ** ADAM_STEP)
    delta = -ADAM_LR * (m_hat / (jnp.sqrt(v_hat) + ADAM_EPS) + ADAM_WD * w)
    return delta, m, v


def _adamw_stacked(grads, w, m, v):
    L, R, C = w.shape
    tm = _row_tile(R, C)
    nrb = R // tm

    def body(*refs):
        g_refs = refs[:L]
        w_ref, m_ref, v_ref, go_ref, d_ref, mo_ref, vo_ref = refs[L:]
        l = pl.program_id(0)
        for ll in range(L):
            @pl.when(l == ll)
            def _(ll=ll):
                g = g_refs[ll][...]
                delta, mn, vn = _adamw_math(w_ref[...], g, m_ref[...], v_ref[...])
                go_ref[...] = g
                d_ref[...] = delta
                mo_ref[...] = mn
                vo_ref[...] = vn

    stacked = pl.BlockSpec((None, tm, C), lambda l, r: (l, r, 0))
    g_specs = [pl.BlockSpec((tm, C), lambda l, r, ll=ll: (jnp.where(l == ll, r, 0), 0)) for ll in range(L)]
    shp = jax.ShapeDtypeStruct((L, R, C), F32)
    return pl.pallas_call(
        body, name="adamw", grid=(L, nrb),
        in_specs=[*g_specs, stacked, stacked, stacked], out_specs=[stacked] * 4, out_shape=[shp] * 4,
        compiler_params=_params(("arbitrary", "arbitrary")),
    )(*grads, w, m, v)


def _adamw_small(parts, w, m, v):
    _, R, C = parts.shape
    tm = _row_tile(R, 8 * C)

    def body(p_ref, w_ref, m_ref, v_ref, go_ref, d_ref, mo_ref, vo_ref):
        g = p_ref[0]
        for k in range(1, 8):
            g = g + p_ref[k]
        delta, mn, vn = _adamw_math(w_ref[...], g, m_ref[...], v_ref[...])
        go_ref[...] = g
        d_ref[...] = delta
        mo_ref[...] = mn
        vo_ref[...] = vn

    blk = pl.BlockSpec((tm, C), lambda i: (i, 0))
    shp = jax.ShapeDtypeStruct((R, C), F32)
    return pl.pallas_call(
        body, name="adamw_small", grid=(R // tm,),
        in_specs=[pl.BlockSpec((8, tm, C), lambda i: (0, i, 0)), blk, blk, blk],
        out_specs=[blk] * 4, out_shape=[shp] * 4,
        compiler_params=_params(("parallel",)),
    )(parts, w, m, v)


def _cast_place(w, layer):
    _, R, C = w.shape
    tm = _row_tile(R, C)

    def body(w_ref, o_ref):
        o_ref[...] = w_ref[...].astype(BF16)

    return pl.pallas_call(
        body, name="cast_place", grid=(R // tm,),
        in_specs=[pl.BlockSpec((None, tm, C), lambda r: (layer, r, 0))],
        out_specs=pl.BlockSpec((None, tm, C), lambda r: (_my_chip(), r, 0)),
        out_shape=jax.ShapeDtypeStruct((N_CHIPS, R, C), BF16),
        compiler_params=_params(("parallel",)),
    )(w)


def _trig_tables(positions):
    half = ROPE_DIM // 2
    inv_freq = ROPE_THETA ** (-jnp.arange(0, ROPE_DIM, 2, dtype=F32) / ROPE_DIM)
    ang = positions.astype(F32)[:, None] * inv_freq
    cos, sin = jnp.cos(ang), jnp.sin(ang)
    S = positions.shape[0]
    zeros = lambda w: jnp.zeros((S, w), F32)
    cos_h = jnp.concatenate([cos, cos, jnp.ones((S, HEAD_DIM - ROPE_DIM), F32)], axis=1)
    sa_h = jnp.concatenate([-sin, zeros(HEAD_DIM - half)], axis=1)
    sb_h = jnp.concatenate([zeros(half), sin, zeros(HEAD_DIM - ROPE_DIM)], axis=1)
    rep = LANES // HEAD_DIM
    return [jnp.tile(t, (1, rep)) for t in (cos_h, sa_h, sb_h)]


def _row(vec):
    return vec.reshape(1, -1)


def _lane_row(vec):
    return jnp.zeros((8, LANES), F32).at[0, :vec.shape[0]].set(vec)


def _pack(pieces, rows):
    flat = jnp.concatenate([p.reshape(-1).astype(F32) for p in pieces])
    return jnp.pad(flat, (0, rows * LANES - flat.shape[0])).reshape(rows, LANES)


def kernel(x, positions, norm1_g, w_in, b_in, sinks, sgu_ln_g, sgu_ln_b, sgu_w, sgu_b, w_attn_branch, w_sgu_branch, w_out, norm2_g, w_gate_up, w_down, final_g, loss_target, m_norm1_g, m_w_in, m_b_in, m_sinks, m_sgu_ln_g, m_sgu_ln_b, m_sgu_w, m_sgu_b, m_w_attn_branch, m_w_sgu_branch, m_w_out, m_norm2_g, m_w_gate_up, m_w_down, m_final_g, v_norm1_g, v_w_in, v_b_in, v_sinks, v_sgu_ln_g, v_sgu_ln_b, v_sgu_w, v_sgu_b, v_w_attn_branch, v_w_sgu_branch, v_w_out, v_norm2_g, v_w_gate_up, v_w_down, v_final_g):
    L = norm1_g.shape[0]
    S, D = x.shape[1], x.shape[2]
    NQ = sinks.shape[1]
    A = NQ * HEAD_DIM
    KV = N_KV_HEADS * HEAD_DIM
    G = sgu_ln_g.shape[1]
    NG = sgu_w.shape[1]
    IN = b_in.shape[1]
    Fd = w_down.shape[1] * N_CHIPS
    dm = dict(D=D, A=A, KV=KV, NQ=NQ, G=G, NG=NG, IN=IN, F=Fd,
              OFF_K=A, OFF_V=A + KV, OFF_Z=A + 2 * KV, OFF_G=A + 2 * KV + 2 * G)
    assert sgu_w.shape[2] == WINDOW and G == NG * LANES and IN == dm["OFF_G"] + 2 * D

    h = x[0]
    target = loss_target[0]
    trig = _trig_tables(positions[0])
    tril = jnp.tril(jnp.ones((WINDOW, WINDOW), bool))

    big = [w_in, w_attn_branch, w_sgu_branch, w_out, w_gate_up, w_down]
    big_m = [m_w_in, m_w_attn_branch, m_w_sgu_branch, m_w_out, m_w_gate_up, m_w_down]
    big_v = [v_w_in, v_w_attn_branch, v_w_sgu_branch, v_w_out, v_w_gate_up, v_w_down]

    placed = [[_cast_place(w, l) for w in big] for l in range(L)]
    gathered = [None] * L
    gathered[0] = _gather_weights(placed[0]).run()

    def weights(l):
        w_in_g, w_ab_g, w_sb_g, w_out_g, w_gu_g, w_down_g = gathered[l]
        return (w_in_g, w_ab_g, w_sb_g, w_out_g.reshape(D, D), w_gu_g, w_down_g.reshape(Fd, D))

    def small(l):
        return dict(
            g1=_row(norm1_g[l]), b_in=_row(b_in[l]), sink=_lane_row(sinks[l]),
            ln_g=_row(sgu_ln_g[l]), ln_b=_row(sgu_ln_b[l]),
            w_tril=jnp.where(tril[None], sgu_w[l], 0.0).astype(BF16),
            b_t=jnp.zeros((WINDOW, LANES), F32).at[:, :NG].set(sgu_b[l].T),
            g2=_row(norm2_g[l]))

    saved = []
    for l in range(L):
        w_in_g, w_ab_g, w_sb_g, w_out_g, w_gu_g, w_down_g = weights(l)
        sp = small(l)
        nxt = placed[l + 1] if l + 1 < L else None
        fetch = (lambda idx: _gather_weights([nxt[t] for t in idx])) if nxt else (lambda idx: None)
        xn = _rms_fwd(h, sp["g1"])
        proj = _in_proj(xn, w_in_g, sp["b_in"], dm, carry=fetch([0]))
        if nxt:
            proj, got_in = proj
        y_attn, lse = _attn_fwd(proj, trig, sp["sink"], dm)
        y_sgu = _sgu_fwd(proj, sp["w_tril"], sp["b_t"], sp["ln_g"], sp["ln_b"], dm)
        a_attn = _branch_attn(y_attn, w_ab_g, dm)
        a_sgu, merged = _branch_sgu_merge(y_sgu, w_sb_g, a_attn, proj, dm)
        h_mid = _residual_matmul("out_proj", merged, w_out_g, h, carry=fetch([1, 2, 3]))
        if nxt:
            h_mid, got_mid = h_mid
        hn = _rms_fwd(h_mid, sp["g2"])
        gu_act = _gate_up(hn, w_gu_g, dm, carry=fetch([4]))
        if nxt:
            gu_act, got_gu = gu_act
        gu, act = gu_act
        h_out = _residual_matmul("down_proj", act, w_down_g, h_mid, carry=fetch([5]))
        if nxt:
            h_out, got_down = h_out
            gathered[l + 1] = [*got_in, *got_mid, *got_gu, *got_down]
        saved.append(dict(h=h, xn=xn, proj=proj, y_attn=y_attn, lse=lse, y_sgu=y_sgu, a_attn=a_attn, a_sgu=a_sgu,
                          merged=merged, h_mid=h_mid, hn=hn, gu=gu, act=act))
        h = h_out

    dh, dh_b, d_final, loss_part = _loss_head(h, _row(final_g), target)

    small_grads = [None] * L
    reduced = [None] * L
    pending = None
    for l in reversed(range(L)):
        w_in_g, w_ab_g, w_sb_g, w_out_g, w_gu_g, w_down_g = weights(l)
        sp, sv = small(l), saved[l]
        if pending is None:
            dgu = _down_bwd(dh_b, w_down_g, sv["gu"], dm)
            g_down = _wgrad_rows("wgrad_down", sv["act"], dh_b)
            dhn = _gate_up_bwd(dgu, w_gu_g, dm)
            g_gu = _wgrad_gate_up(sv["hn"], dgu, dm)
        else:
            dgu, land = _down_bwd(dh_b, w_down_g, sv["gu"], dm, carry=_sibling_exchange(pending))
            sums, sends = zip(*[_pair_sum(g, d) for g, d in zip(pending, land)])
            g_down = _wgrad_rows("wgrad_down", sv["act"], dh_b)
            rest = [0, 1, 2, 3, 5]
            dhn, got_gu = _gate_up_bwd(dgu, w_gu_g, dm, carry=_chip_exchange([sends[4]]))
            g_gu, got_rest = _wgrad_gate_up(sv["hn"], dgu, dm, carry=_chip_exchange([sends[t] for t in rest]))
            got = [None] * len(big)
            got[4] = got_gu[0]
            for t, g in zip(rest, got_rest):
                got[t] = g
            reduced[l + 1] = _sibling_share([_chip_sum(s, g) for s, g in zip(sums, got)]).run()
        dh_mid, dh_mid_b, d_g2 = _rms_bwd(dhn, sv["h_mid"], sp["g2"], dh)
        da_attn, da_sgu, dgate = _out_bwd(dh_mid_b, w_out_g, sv["proj"], sv["a_attn"], sv["a_sgu"], dm)
        g_out = _wgrad_rows("wgrad_out", sv["merged"], dh_mid_b)
        dy_attn = _colsharded_bwd("branch_attn_bwd", da_attn, w_ab_g, BF16)
        dy_sgu = _colsharded_bwd("branch_sgu_bwd", da_sgu, w_sb_g, BF16)
        g_ab = _wgrad_cols("wgrad_attn_branch", sv["y_attn"], da_attn)
        g_sb = _wgrad_cols("wgrad_sgu_branch", sv["y_sgu"], da_sgu)
        dq, dk, dv, d_sink = _attn_bwd(sv["proj"], trig, sp["sink"], sv["y_attn"], sv["lse"], dy_attn, dm)
        dz, d_sgu_w, d_bt, d_lng, d_lnb = _sgu_bwd(sv["proj"], sp["w_tril"], sp["b_t"], sp["ln_g"], sp["ln_b"], dy_sgu, dm)
        dproj = jnp.concatenate([dq, dk, dv, dz, dgate[0], dgate[1]], axis=1)
        d_bin = _colsum(dproj)
        dxn = _colsharded_bwd("in_proj_bwd", dproj, w_in_g, F32)
        g_in = _wgrad_cols("wgrad_in", sv["xn"], dproj)
        dh, dh_b, d_g1 = _rms_bwd(dxn, sv["h"], sp["g1"], dh_mid)
        pending = [g_in, g_ab, g_sb, g_out, g_gu, g_down]
        small_grads[l] = dict(norm1_g=d_g1[0], b_in=d_bin[0], sinks=d_sink[0, :NQ], sgu_ln_g=d_lng[0], sgu_ln_b=d_lnb[0],
                              sgu_w=d_sgu_w, sgu_b=d_bt[:, :NG].T, norm2_g=d_g2[0])
    grad_x = dh[None]

    land = _sibling_exchange(pending).run()
    sums, sends = zip(*[_pair_sum(g, d) for g, d in zip(pending, land)])
    got = _chip_exchange(list(sends)).run()
    reduced[0] = _sibling_share([_chip_sum(s, g) for s, g in zip(sums, got)]).run()
    big_out =[_adamw_stacked([reduced[l][t] for l in range(L)], big[t], big_m[t], big_v[t]) for t in range(len(big))]

    names = ["norm1_g", "b_in", "sinks", "sgu_ln_g", "sgu_ln_b", "sgu_w", "sgu_b", "norm2_g"]
    small_w = [norm1_g, b_in, sinks, sgu_ln_g, sgu_ln_b, sgu_w, sgu_b, norm2_g, final_g]
    small_m = [m_norm1_g, m_b_in, m_sinks, m_sgu_ln_g, m_sgu_ln_b, m_sgu_w, m_sgu_b, m_norm2_g, m_final_g]
    small_v = [v_norm1_g, v_b_in, v_sinks, v_sgu_ln_g, v_sgu_ln_b, v_sgu_w, v_sgu_b, v_norm2_g, v_final_g]
    small_g = [jnp.stack([small_grads[l][nm] for l in range(L)]) for nm in names] + [d_final[0]]
    sizes = [w.size for w in small_w]
    total = sum(sizes) + 1
    rows = -(-total // (8 * LANES)) * 8
    loss_piece = jnp.sum(loss_part[0]).reshape(1)
    packed_g = _pack(small_g + [loss_piece], rows)
    one = jnp.ones((1,), F32)
    parts = _gather_all(packed_g)
    outs = _adamw_small(parts, _pack(small_w + [one], rows), _pack(small_m + [one], rows), _pack(small_v + [one], rows))

    def unpack(p):
        flat = p.reshape(-1)
        res, off = [], 0
        for w, n in zip(small_w, sizes):
            res.append(flat[off:off + n].reshape(w.shape))
            off += n
        return res, flat[off]

    (sg, loss), (sd, _), (smm, _), (svv, _) = [unpack(o) for o in outs]

    order = ["norm1_g", "w_in", "b_in", "sinks", "sgu_ln_g", "sgu_ln_b", "sgu_w", "sgu_b", "w_attn_branch",
             "w_sgu_branch", "w_out", "norm2_g", "w_gate_up", "w_down", "final_g"]
    big_names = ["w_in", "w_attn_branch", "w_sgu_branch", "w_out", "w_gate_up", "w_down"]
    small_names = names + ["final_g"]

    def collect(kind):
        res = []
        for nm in order:
            if nm in big_names:
                res.append(big_out[big_names.index(nm)][kind])
            else:
                res.append((sg, sd, smm, svv)[kind][small_names.index(nm)])
        return res

    return (loss, grad_x, *collect(0), *collect(1), *collect(2), *collect(3))
```

```python
import math

import jax
import jax.numpy as jnp
from jax import lax
from jax.experimental import pallas as pl
from jax.experimental.pallas import tpu as pltpu

F32 = jnp.float32
BF16 = jnp.bfloat16
MESH = pl.DeviceIdType.MESH
ANY = pl.BlockSpec(memory_space=pl.ANY)

HEAD_DIM = 64
N_KV_HEADS = 4
WINDOW = 128
ROPE_DIM = HEAD_DIM // 4
ROPE_THETA = 500000.0
EPS = 1e-5
NEG = -1e30
N_CHIPS = 4
LANES = 128
V7X_VMEM_LIMIT = 56 * 1024 * 1024

ADAM_LR = 0.001
ADAM_B1 = 0.9
ADAM_B2 = 0.999
ADAM_EPS = 1e-08
ADAM_WD = 0.01
ADAM_STEP = 10

NN = (((1,), (0,)), ((), ()))
NT = (((1,), (1,)), ((), ()))
TN = (((0,), (0,)), ((), ()))


ROW_TILES = (1024, 512, 256, 128, 64, 32, 16, 8)
BLOCK_BYTES = 2 * 1024 * 1024


def _pick(n, prefs):
    for p in prefs:
        if n % p == 0:
            return p
    raise ValueError(f"no tile for {n} among {prefs}")


def _row_tile(rows, cols, itemsize=4):
    return _pick(rows, [t for t in ROW_TILES if t * cols * itemsize <= BLOCK_BYTES or t == ROW_TILES[-1]])


def _dot(a, b, dims):
    return lax.dot_general(a, b, dims, preferred_element_type=F32)


def _sigmoid(x):
    return 1.0 / (1.0 + jnp.exp(-x))


def _gelu(x):
    return 0.5 * x * (1.0 + lax.erf(x * (1.0 / math.sqrt(2.0))))


def _gelu_grad(x):
    return 0.5 * (1.0 + lax.erf(x * (1.0 / math.sqrt(2.0)))) + x * jnp.exp(-0.5 * x * x) * (1.0 / math.sqrt(2.0 * math.pi))


def _params(sem):
    return pltpu.CompilerParams(dimension_semantics=sem, vmem_limit_bytes=V7X_VMEM_LIMIT)


def _matmul(name, lhs, rhs_list, *, dims, grid, lhs_spec, rhs_specs, acc_shape, out_shape, out_specs,
            epilogue, extra=(), extra_specs=(), carry=None, rhs_colsum=False):
    gk = grid[2]
    nr, ne, no = len(rhs_list), len(extra), len(out_shape)
    nci = len(carry.ins) if carry else 0
    nco = len(carry.outs) if carry else 0
    acc_shapes = [acc_shape] * nr + ([(8, acc_shape[1])] if rhs_colsum else [])
    nacc = len(acc_shapes) if gk > 1 else 0

    def body(*refs):
        a_ref = refs[0]
        b_refs = refs[1:1 + nr]
        e_refs = refs[1 + nr:1 + nr + ne]
        base = 1 + nr + ne
        ci_refs = refs[base:base + nci]
        o_refs = refs[base + nci:base + nci + no]
        co_refs = refs[base + nci + no:base + nci + no + nco]
        acc_refs = refs[base + nci + no + nco:base + nci + no + nco + nacc]
        sems = refs[base + nci + no + nco + nacc:]
        ids = [pl.program_id(d) for d in range(3)]
        if carry:
            @pl.when((ids[0] == 0) & (ids[1] == 0) & (ids[2] == 0))
            def _():
                carry.start(ci_refs, co_refs, *sems)

        a = a_ref[...]
        parts = [_dot(a, b[...], dims) for b in b_refs]
        if rhs_colsum:
            b0 = b_refs[0][...]
            parts.append(_dot(jnp.ones((8, b0.shape[0]), b0.dtype), b0, NN))
        if gk == 1:
            epilogue(parts, e_refs, o_refs)
        else:
            k = ids[2]

            @pl.when(k == 0)
            def _():
                for acc, p in zip(acc_refs, parts):
                    acc[...] = p

            @pl.when(k > 0)
            def _():
                for acc, p in zip(acc_refs, parts):
                    acc[...] += p

            @pl.when(k == gk - 1)
            def _():
                epilogue([acc[...] for acc in acc_refs], e_refs, o_refs)

        if carry:
            @pl.when((ids[0] == grid[0] - 1) & (ids[1] == grid[1] - 1) & (ids[2] == grid[2] - 1))
            def _():
                carry.finish(ci_refs, co_refs, *sems)

    scratch = [pltpu.VMEM(s, F32) for s in acc_shapes[:nacc]]
    kwargs = {}
    if carry:
        scratch += carry.sem_scratch()
        kwargs["input_output_aliases"] = {1 + nr + ne + i: no + o for i, o in carry.aliases.items()}
    outs = pl.pallas_call(
        body, name=name, grid=grid,
        in_specs=[lhs_spec, *rhs_specs, *extra_specs, *([ANY] * nci)],
        out_specs=[*out_specs, *([ANY] * nco)],
        out_shape=[*out_shape, *(carry.outs if carry else [])], scratch_shapes=scratch,
        compiler_params=_params(("arbitrary",) * 3 if carry else ("parallel", "parallel", "arbitrary")),
        **kwargs,
    )(lhs, *rhs_list, *extra, *(carry.ins if carry else []))
    return outs


class _Comm:
    def __init__(self, name, ins, outs, aliases, n_sems, start, finish):
        self.name, self.ins, self.outs, self.aliases, self.n_sems = name, list(ins), list(outs), dict(aliases), n_sems
        self.start, self.finish = start, finish

    def sem_scratch(self):
        return [pltpu.SemaphoreType.DMA((self.n_sems,)), pltpu.SemaphoreType.DMA((self.n_sems,))]

    def run(self):
        ni = len(self.ins)

        def body(*refs):
            in_refs, out_refs, sems = refs[:ni], refs[ni:ni + len(self.outs)], refs[ni + len(self.outs):]
            self.start(in_refs, out_refs, *sems)
            self.finish(in_refs, out_refs, *sems)

        return pl.pallas_call(
            body, name=self.name, in_specs=[ANY] * ni, out_specs=[ANY] * len(self.outs), out_shape=self.outs,
            input_output_aliases=self.aliases, scratch_shapes=self.sem_scratch(),
        )(*self.ins)


def _store_epilogue(dtype):
    def ep(parts, e_refs, o_refs):
        o_refs[0][...] = parts[0].astype(dtype)
    return ep


def _rms_fwd(h, g_row):
    S, D = h.shape
    tm = _row_tile(S, D)

    def body(h_ref, g_ref, o_ref):
        x = h_ref[...]
        r = lax.rsqrt(jnp.mean(x * x, axis=-1, keepdims=True) + EPS)
        o_ref[...] = (x * r * g_ref[...]).astype(BF16)

    return pl.pallas_call(
        body, name="rms_fwd", grid=(S // tm,),
        in_specs=[pl.BlockSpec((tm, D), lambda i: (i, 0)), pl.BlockSpec((1, D), lambda i: (0, 0))],
        out_specs=pl.BlockSpec((tm, D), lambda i: (i, 0)),
        out_shape=jax.ShapeDtypeStruct((S, D), BF16),
        compiler_params=_params(("parallel",)),
    )(h, g_row)


def _rms_bwd(dy, h, g_row, dres):
    S, D = h.shape
    tm = _row_tile(S, D)

    def body(dy_ref, h_ref, g_ref, dres_ref, dh_ref, dhb_ref, dg_ref):
        i = pl.program_id(0)
        x = h_ref[...]
        d = dy_ref[...]
        r = lax.rsqrt(jnp.mean(x * x, axis=-1, keepdims=True) + EPS)
        dg = d * g_ref[...]
        dot = jnp.mean(dg * x, axis=-1, keepdims=True)
        dh = dres_ref[...] + r * dg - x * (r * r * r) * dot
        dh_ref[...] = dh
        dhb_ref[...] = dh.astype(BF16)
        part = jnp.sum(d * x * r, axis=0, keepdims=True)

        @pl.when(i == 0)
        def _():
            dg_ref[...] = jnp.zeros_like(dg_ref)

        dg_ref[0:1, :] += part

    return pl.pallas_call(
        body, name="rms_bwd", grid=(S // tm,),
        in_specs=[pl.BlockSpec((tm, D), lambda i: (i, 0)), pl.BlockSpec((tm, D), lambda i: (i, 0)),
                  pl.BlockSpec((1, D), lambda i: (0, 0)), pl.BlockSpec((tm, D), lambda i: (i, 0))],
        out_specs=[pl.BlockSpec((tm, D), lambda i: (i, 0)), pl.BlockSpec((tm, D), lambda i: (i, 0)),
                   pl.BlockSpec((8, D), lambda i: (0, 0))],
        out_shape=[jax.ShapeDtypeStruct((S, D), F32), jax.ShapeDtypeStruct((S, D), BF16),
                   jax.ShapeDtypeStruct((8, D), F32)],
        compiler_params=_params(("arbitrary",)),
    )(dy, h, g_row, dres)


def _loss_head(h, g_row, target):
    S, D = h.shape
    tm = _row_tile(S, D)

    def body(h_ref, g_ref, t_ref, dh_ref, dhb_ref, dg_ref, loss_ref):
        i = pl.program_id(0)
        x = h_ref[...]
        g = g_ref[...]
        r = lax.rsqrt(jnp.mean(x * x, axis=-1, keepdims=True) + EPS)
        y = x * r * g
        e = y - t_ref[...]
        d = e * (1.0 / D)
        dg = d * g
        dot = jnp.mean(dg * x, axis=-1, keepdims=True)
        dh = r * dg - x * (r * r * r) * dot
        dh_ref[...] = dh
        dhb_ref[...] = dh.astype(BF16)

        @pl.when(i == 0)
        def _():
            dg_ref[...] = jnp.zeros_like(dg_ref)
            loss_ref[...] = jnp.zeros_like(loss_ref)

        dg_ref[0:1, :] += jnp.sum(d * x * r, axis=0, keepdims=True)
        loss_ref[0:1, :] += jnp.sum((0.5 / D) * e * e, axis=0, keepdims=True)

    return pl.pallas_call(
        body, name="loss_head", grid=(S // tm,),
        in_specs=[pl.BlockSpec((tm, D), lambda i: (i, 0)), pl.BlockSpec((1, D), lambda i: (0, 0)),
                  pl.BlockSpec((tm, D), lambda i: (i, 0))],
        out_specs=[pl.BlockSpec((tm, D), lambda i: (i, 0)), pl.BlockSpec((tm, D), lambda i: (i, 0)),
                   pl.BlockSpec((8, D), lambda i: (0, 0)), pl.BlockSpec((8, D), lambda i: (0, 0))],
        out_shape=[jax.ShapeDtypeStruct((S, D), F32), jax.ShapeDtypeStruct((S, D), BF16),
                   jax.ShapeDtypeStruct((8, D), F32), jax.ShapeDtypeStruct((8, D), F32)],
        compiler_params=_params(("arbitrary",)),
    )(h, g_row, target)


def _rope(t, cos, sa, sb):
    w = t.shape[-1]
    return t * cos + pltpu.roll(t, w - 8, 1) * sa + pltpu.roll(t, 8, 1) * sb


def _rope_t(g, cos, sa, sb):
    w = g.shape[-1]
    return g * cos + pltpu.roll(g * sa, 8, 1) + pltpu.roll(g * sb, w - 8, 1)


def _band_mask(n, qpk):
    qi = lax.broadcasted_iota(jnp.int32, (qpk * WINDOW, 2 * WINDOW), 0) & (WINDOW - 1)
    kj = lax.broadcasted_iota(jnp.int32, (qpk * WINDOW, 2 * WINDOW), 1)
    rel = qi + WINDOW - kj
    ok = (rel >= 0) & (rel < WINDOW)
    return ok & ((kj >= WINDOW) | (n > 0))


def _stack_heads(x, g, qpk):
    return jnp.concatenate([x[:, (g * qpk + hh) * HEAD_DIM:(g * qpk + hh + 1) * HEAD_DIM] for hh in range(qpk)], axis=0)


def _stack_cols(row, g, qpk):
    return jnp.concatenate([row[:, g * qpk + hh:g * qpk + hh + 1] for hh in range(qpk)], axis=0)


def _attn_specs(dm, nb):
    A, KV = dm["A"], dm["KV"]
    kb, vb = dm["OFF_K"] // KV, dm["OFF_V"] // KV
    cur = lambda n: jnp.minimum(n, nb - 1)
    prev = lambda n: jnp.maximum(jnp.minimum(n, nb - 1) - 1, 0)
    proj_specs = [
        pl.BlockSpec((WINDOW, A), lambda n: (cur(n), 0)),
        pl.BlockSpec((WINDOW, KV), lambda n: (prev(n), kb)),
        pl.BlockSpec((WINDOW, KV), lambda n: (cur(n), kb)),
        pl.BlockSpec((WINDOW, KV), lambda n: (prev(n), vb)),
        pl.BlockSpec((WINDOW, KV), lambda n: (cur(n), vb)),
    ]
    trig_cur = [pl.BlockSpec((WINDOW, LANES), lambda n: (cur(n), 0)) for _ in range(3)]
    trig_prev = [pl.BlockSpec((WINDOW, LANES), lambda n: (prev(n), 0)) for _ in range(3)]
    return proj_specs, trig_cur, trig_prev, cur, prev


def _attn_fwd(proj, trig, sink_row, dm):
    S = proj.shape[0]
    A, KV, NQ = dm["A"], dm["KV"], dm["NQ"]
    qpk = NQ // N_KV_HEADS
    nb = S // WINDOW
    scale = HEAD_DIM ** -0.5
    proj_specs, trig_cur, trig_prev, cur, _ = _attn_specs(dm, nb)

    def body(q_ref, kp_ref, kc_ref, vp_ref, vc_ref, cc_ref, sac_ref, sbc_ref, cp_ref, sap_ref, sbp_ref,
             sink_ref, y_ref, lse_ref):
        n = pl.program_id(0)
        tq = lambda r: jnp.tile(r[...], (1, A // LANES))
        tk = lambda rp, rc: jnp.tile(jnp.concatenate([rp[...], rc[...]], axis=0), (1, KV // LANES))
        qr = _rope(q_ref[...].astype(F32), tq(cc_ref), tq(sac_ref), tq(sbc_ref)).astype(BF16)
        kband = jnp.concatenate([kp_ref[...], kc_ref[...]], axis=0).astype(F32)
        kr = _rope(kband, tk(cp_ref, cc_ref), tk(sap_ref, sac_ref), tk(sbp_ref, sbc_ref)).astype(BF16)
        vband = jnp.concatenate([vp_ref[...], vc_ref[...]], axis=0)
        mask = _band_mask(n, qpk)
        lane = lax.broadcasted_iota(jnp.int32, (WINDOW, LANES), 1)
        lse_all = jnp.zeros((WINDOW, LANES), F32)
        sink_rows = jnp.broadcast_to(sink_ref[0:1, :], (WINDOW, LANES))
        for g in range(N_KV_HEADS):
            k_g = kr[:, g * HEAD_DIM:(g + 1) * HEAD_DIM]
            v_g = vband[:, g * HEAD_DIM:(g + 1) * HEAD_DIM]
            q_g = _stack_heads(qr, g, qpk)
            sink = _stack_cols(sink_rows, g, qpk)
            s = jnp.where(mask, _dot(q_g, k_g, NT) * scale, NEG)
            m = jnp.maximum(jnp.max(s, axis=-1, keepdims=True), sink)
            p = jnp.exp(s - m)
            den = jnp.sum(p, axis=-1, keepdims=True) + jnp.exp(sink - m)
            o = _dot(p.astype(BF16), v_g, NN) * (1.0 / den)
            lse_g = m + jnp.log(den)
            for hh in range(qpk):
                h = g * qpk + hh
                rows = slice(hh * WINDOW, (hh + 1) * WINDOW)
                y_ref[:, h * HEAD_DIM:(h + 1) * HEAD_DIM] = o[rows].astype(BF16)
                lse_all = jnp.where(lane == h, lse_g[rows], lse_all)
        lse_ref[...] = lse_all

    return pl.pallas_call(
        body, name="attn_fwd", grid=(nb,),
        in_specs=[*proj_specs, *trig_cur, *trig_prev, pl.BlockSpec((8, LANES), lambda n: (0, 0))],
        out_specs=[pl.BlockSpec((WINDOW, A), lambda n: (n, 0)), pl.BlockSpec((WINDOW, LANES), lambda n: (n, 0))],
        out_shape=[jax.ShapeDtypeStruct((S, A), BF16), jax.ShapeDtypeStruct((S, LANES), F32)],
        compiler_params=_params(("parallel",)),
    )(proj, proj, proj, proj, proj, *trig, *trig, sink_row)


def _attn_bwd(proj, trig, sink_row, y, lse, dy, dm):
    S = proj.shape[0]
    A, KV, NQ = dm["A"], dm["KV"], dm["NQ"]
    qpk = NQ // N_KV_HEADS
    nb = S // WINDOW
    scale = HEAD_DIM ** -0.5
    proj_specs, trig_cur, trig_prev, cur, prev = _attn_specs(dm, nb)

    def body(q_ref, kp_ref, kc_ref, vp_ref, vc_ref, cc_ref, sac_ref, sbc_ref, cp_ref, sap_ref, sbp_ref,
             sink_ref, y_ref, lse_ref, dy_ref, dq_ref, dk_ref, dv_ref, dsink_ref,
             ck_ref, cv_ref, bk_ref, bv_ref, dqr_ref):
        n = pl.program_id(0)

        @pl.when(n == 0)
        def _():
            dsink_ref[...] = jnp.zeros_like(dsink_ref)
            ck_ref[...] = jnp.zeros_like(ck_ref)
            cv_ref[...] = jnp.zeros_like(cv_ref)

        @pl.when(n < nb)
        def _():
            tq = lambda r: jnp.tile(r[...], (1, A // LANES))
            tk = lambda rp, rc: jnp.tile(jnp.concatenate([rp[...], rc[...]], axis=0), (1, KV // LANES))
            cq, saq, sbq = tq(cc_ref), tq(sac_ref), tq(sbc_ref)
            ck, sak, sbk = tk(cp_ref, cc_ref), tk(sap_ref, sac_ref), tk(sbp_ref, sbc_ref)
            qr = _rope(q_ref[...].astype(F32), cq, saq, sbq).astype(BF16)
            kband = jnp.concatenate([kp_ref[...], kc_ref[...]], axis=0).astype(F32)
            kr = _rope(kband, ck, sak, sbk).astype(BF16)
            vband = jnp.concatenate([vp_ref[...], vc_ref[...]], axis=0)
            mask = _band_mask(n, qpk)
            lane = lax.broadcasted_iota(jnp.int32, (1, LANES), 1)
            lse_all = lse_ref[...]
            sink_rows = jnp.broadcast_to(sink_ref[0:1, :], (WINDOW, LANES))
            dy_all = dy_ref[...]
            y_all = y_ref[...]
            dsink = jnp.zeros((1, LANES), F32)
            for g in range(N_KV_HEADS):
                k_g = kr[:, g * HEAD_DIM:(g + 1) * HEAD_DIM]
                v_g = vband[:, g * HEAD_DIM:(g + 1) * HEAD_DIM]
                q_g = _stack_heads(qr, g, qpk)
                dy_g = _stack_heads(dy_all, g, qpk)
                y_g = _stack_heads(y_all, g, qpk)
                lse_g = _stack_cols(lse_all, g, qpk)
                s = jnp.where(mask, _dot(q_g, k_g, NT) * scale, NEG)
                p = jnp.exp(s - lse_g)
                dp = _dot(dy_g, v_g, NT)
                delta = jnp.sum(dy_g.astype(F32) * y_g.astype(F32), axis=-1, keepdims=True)
                ds = (p * (dp - delta) * scale).astype(BF16)
                dq_g = _dot(ds, k_g, NN)
                bk_ref[:, g * HEAD_DIM:(g + 1) * HEAD_DIM] = _dot(ds, q_g, TN)
                bv_ref[:, g * HEAD_DIM:(g + 1) * HEAD_DIM] = _dot(p.astype(BF16), dy_g, TN)
                sink_d = jnp.exp(_stack_cols(sink_rows, g, qpk) - lse_g) * delta
                for hh in range(qpk):
                    h = g * qpk + hh
                    rows = slice(hh * WINDOW, (hh + 1) * WINDOW)
                    dqr_ref[:, h * HEAD_DIM:(h + 1) * HEAD_DIM] = dq_g[rows]
                    dsink = dsink + jnp.where(lane == h, -jnp.sum(sink_d[rows], axis=0, keepdims=True), 0.0)
            dsink_ref[0:1, :] += dsink
            dq_ref[...] = _rope_t(dqr_ref[...], cq, saq, sbq).astype(BF16)
            dkb = _rope_t(bk_ref[...], ck, sak, sbk)
            dvb = bv_ref[...]
            dk_ref[...] = (ck_ref[...] + dkb[:WINDOW]).astype(BF16)
            dv_ref[...] = (cv_ref[...] + dvb[:WINDOW]).astype(BF16)
            ck_ref[...] = dkb[WINDOW:]
            cv_ref[...] = dvb[WINDOW:]

        @pl.when(n == nb)
        def _():
            dk_ref[...] = ck_ref[...].astype(BF16)
            dv_ref[...] = cv_ref[...].astype(BF16)

    row = lambda w: pl.BlockSpec((WINDOW, w), lambda n: (cur(n), 0))
    done = lambda w: pl.BlockSpec((WINDOW, w), lambda n: (jnp.maximum(n - 1, 0), 0))
    return pl.pallas_call(
        body, name="attn_bwd", grid=(nb + 1,),
        in_specs=[*proj_specs, *trig_cur, *trig_prev, pl.BlockSpec((8, LANES), lambda n: (0, 0)),
                  row(A), row(LANES), row(A)],
        out_specs=[row(A), done(KV), done(KV), pl.BlockSpec((8, LANES), lambda n: (0, 0))],
        out_shape=[jax.ShapeDtypeStruct((S, A), BF16), jax.ShapeDtypeStruct((S, KV), BF16),
                   jax.ShapeDtypeStruct((S, KV), BF16), jax.ShapeDtypeStruct((8, LANES), F32)],
        scratch_shapes=[pltpu.VMEM((WINDOW, KV), F32), pltpu.VMEM((WINDOW, KV), F32),
                        pltpu.VMEM((2 * WINDOW, KV), F32), pltpu.VMEM((2 * WINDOW, KV), F32),
                        pltpu.VMEM((WINDOW, A), F32)],
        compiler_params=_params(("arbitrary",)),
    )(proj, proj, proj, proj, proj, *trig, *trig, sink_row, y, lse, dy)


def _sgu_layout(dm, S):
    G = dm["G"]
    pw = math.gcd(dm["OFF_Z"], G)
    npc = G // pw
    tm = _pick(S, (256, 128))
    u_specs = [pl.BlockSpec((tm, pw), lambda i, p=p: (i, dm["OFF_Z"] // pw + p)) for p in range(npc)]
    v_specs = [pl.BlockSpec((tm, pw), lambda i, p=p: (i, (dm["OFF_Z"] + G) // pw + p)) for p in range(npc)]
    return pw, npc, tm, u_specs, v_specs


def _sgu_norm(v_refs, lg_ref, lb_ref):
    v = jnp.concatenate([_gelu(r[...].astype(F32)) for r in v_refs], axis=1)
    mu = jnp.mean(v, axis=-1, keepdims=True)
    vc = v - mu
    rstd = lax.rsqrt(jnp.mean(vc * vc, axis=-1, keepdims=True) + EPS)
    xhat = vc * rstd
    return xhat, rstd, (xhat * lg_ref[...] + lb_ref[...]).astype(BF16)


def _sgu_fwd(proj, w_tril, b_t, ln_g_row, ln_b_row, dm):
    S = proj.shape[0]
    G, NG = dm["G"], dm["NG"]
    pw, npc, tm, u_specs, v_specs = _sgu_layout(dm, S)
    nch = tm // WINDOW

    def body(*refs):
        u_refs, v_refs = refs[:npc], refs[npc:2 * npc]
        w_ref, bt_ref, lg_ref, lb_ref, y_ref = refs[2 * npc:]
        _, _, vn = _sgu_norm(v_refs, lg_ref, lb_ref)
        u = jnp.concatenate([_gelu(r[...].astype(F32)) for r in u_refs], axis=1)
        for c in range(nch):
            rows = slice(c * WINDOW, (c + 1) * WINDOW)
            for g in range(NG):
                cols = slice(g * LANES, (g + 1) * LANES)
                sv = _dot(w_ref[g], vn[rows, cols], NN) + bt_ref[:, g:g + 1]
                y_ref[rows, cols] = (u[rows, cols] * sv).astype(BF16)

    return pl.pallas_call(
        body, name="sgu_fwd", grid=(S // tm,),
        in_specs=[*u_specs, *v_specs,
                  pl.BlockSpec((NG, WINDOW, WINDOW), lambda i: (0, 0, 0)),
                  pl.BlockSpec((WINDOW, LANES), lambda i: (0, 0)),
                  pl.BlockSpec((1, G), lambda i: (0, 0)), pl.BlockSpec((1, G), lambda i: (0, 0))],
        out_specs=pl.BlockSpec((tm, G), lambda i: (i, 0)),
        out_shape=jax.ShapeDtypeStruct((S, G), BF16),
        compiler_params=_params(("parallel",)),
    )(*([proj] * (2 * npc)), w_tril, b_t, ln_g_row, ln_b_row)


def _sgu_bwd(proj, w_tril, b_t, ln_g_row, ln_b_row, dy, dm):
    S = proj.shape[0]
    G, NG = dm["G"], dm["NG"]
    pw, npc, tm, u_specs, v_specs = _sgu_layout(dm, S)
    nch = tm // WINDOW

    def body(*refs):
        u_refs, v_refs = refs[:npc], refs[npc:2 * npc]
        w_ref, bt_ref, lg_ref, lb_ref, dy_ref, dz_ref, dw_ref, dbt_ref, dlg_ref, dlb_ref, dvn_ref = refs[2 * npc:]
        i = pl.program_id(0)

        @pl.when(i == 0)
        def _():
            dw_ref[...] = jnp.zeros_like(dw_ref)
            dbt_ref[...] = jnp.zeros_like(dbt_ref)
            dlg_ref[...] = jnp.zeros_like(dlg_ref)
            dlb_ref[...] = jnp.zeros_like(dlb_ref)

        xhat, rstd, vn = _sgu_norm(v_refs, lg_ref, lb_ref)
        u_pre = jnp.concatenate([r[...].astype(F32) for r in u_refs], axis=1)
        u = _gelu(u_pre)
        dy = dy_ref[...].astype(F32)
        lane = lax.broadcasted_iota(jnp.int32, (WINDOW, LANES), 1)
        tri = lax.broadcasted_iota(jnp.int32, (WINDOW, WINDOW), 0) >= lax.broadcasted_iota(jnp.int32, (WINDOW, WINDOW), 1)
        dbt = jnp.zeros((WINDOW, LANES), F32)
        for c in range(nch):
            rows = slice(c * WINDOW, (c + 1) * WINDOW)
            for g in range(NG):
                cols = slice(g * LANES, (g + 1) * LANES)
                vn_cg = vn[rows, cols]
                sv = _dot(w_ref[g], vn_cg, NN) + bt_ref[:, g:g + 1]
                dy_cg = dy[rows, cols]
                dsv = dy_cg * u[rows, cols]
                dsv_b = dsv.astype(BF16)
                dz_ref[rows, cols] = (dy_cg * sv * _gelu_grad(u_pre[rows, cols])).astype(BF16)
                dvn_ref[rows, cols] = _dot(w_ref[g], dsv_b, TN)
                dw_ref[g] += jnp.where(tri, _dot(dsv_b, vn_cg, NT), 0.0)
                dbt = dbt + jnp.where(lane == g, jnp.sum(dsv, axis=-1, keepdims=True), 0.0)
        dbt_ref[...] += dbt
        dvn = dvn_ref[...]
        dlg_ref[0:1, :] += jnp.sum(dvn * xhat, axis=0, keepdims=True)
        dlb_ref[0:1, :] += jnp.sum(dvn, axis=0, keepdims=True)
        dxh = dvn * lg_ref[...]
        dv = rstd * (dxh - jnp.mean(dxh, axis=-1, keepdims=True) - xhat * jnp.mean(dxh * xhat, axis=-1, keepdims=True))
        v_pre = jnp.concatenate([r[...].astype(F32) for r in v_refs], axis=1)
        dz_ref[:, G:] = (dv * _gelu_grad(v_pre)).astype(BF16)

    return pl.pallas_call(
        body, name="sgu_bwd", grid=(S // tm,),
        in_specs=[*u_specs, *v_specs,
                  pl.BlockSpec((NG, WINDOW, WINDOW), lambda i: (0, 0, 0)),
                  pl.BlockSpec((WINDOW, LANES), lambda i: (0, 0)),
                  pl.BlockSpec((1, G), lambda i: (0, 0)), pl.BlockSpec((1, G), lambda i: (0, 0)),
                  pl.BlockSpec((tm, G), lambda i: (i, 0))],
        out_specs=[pl.BlockSpec((tm, 2 * G), lambda i: (i, 0)),
                   pl.BlockSpec((NG, WINDOW, WINDOW), lambda i: (0, 0, 0)),
                   pl.BlockSpec((WINDOW, LANES), lambda i: (0, 0)),
                   pl.BlockSpec((8, G), lambda i: (0, 0)), pl.BlockSpec((8, G), lambda i: (0, 0))],
        out_shape=[jax.ShapeDtypeStruct((S, 2 * G), BF16), jax.ShapeDtypeStruct((NG, WINDOW, WINDOW), F32),
                   jax.ShapeDtypeStruct((WINDOW, LANES), F32), jax.ShapeDtypeStruct((8, G), F32),
                   jax.ShapeDtypeStruct((8, G), F32)],
        scratch_shapes=[pltpu.VMEM((tm, G), F32)],
        compiler_params=_params(("arbitrary",)),
    )(*([proj] * (2 * npc)), w_tril, b_t, ln_g_row, ln_b_row, dy)


def _result(outs, n_main, carry):
    main = outs[0] if n_main == 1 else tuple(outs[:n_main])
    return (main, list(outs[n_main:])) if carry else main


def _in_proj(xn, w_in_g, b_row, dm, carry=None):
    S, D = xn.shape
    IN = dm["IN"]
    cw = IN // N_CHIPS
    tm = _pick(S, (512, 256, 128))
    tn = _pick(cw, (1920, 640, 512, 256, 128))
    nbc = cw // tn

    def ep(parts, e_refs, o_refs):
        o_refs[0][...] = (parts[0] + e_refs[0][...]).astype(BF16)

    return _result(_matmul(
        "in_proj", xn, [w_in_g], dims=NN, grid=(S // tm, IN // tn, 1),
        lhs_spec=pl.BlockSpec((tm, D), lambda i, j, k: (i, 0)),
        rhs_specs=[pl.BlockSpec((None, D, tn), lambda i, j, k: (j // nbc, 0, j % nbc))],
        acc_shape=(tm, tn), extra=[b_row], extra_specs=[pl.BlockSpec((1, tn), lambda i, j, k: (0, j))],
        out_shape=[jax.ShapeDtypeStruct((S, IN), BF16)],
        out_specs=[pl.BlockSpec((tm, tn), lambda i, j, k: (i, j))], epilogue=ep, carry=carry), 1, carry)


def _branch_attn(y_attn, w_ab_g, dm):
    S, A = y_attn.shape
    D = dm["D"]
    cw = D // N_CHIPS
    tm = _pick(S, (1024, 512, 256, 128))
    return _matmul(
        "branch_attn", y_attn, [w_ab_g], dims=NN, grid=(S // tm, N_CHIPS, 1),
        lhs_spec=pl.BlockSpec((tm, A), lambda i, j, k: (i, 0)),
        rhs_specs=[pl.BlockSpec((None, A, cw), lambda i, j, k: (j, 0, 0))],
        acc_shape=(tm, cw), out_shape=[jax.ShapeDtypeStruct((S, D), BF16)],
        out_specs=[pl.BlockSpec((tm, cw), lambda i, j, k: (i, j))], epilogue=_store_epilogue(BF16))[0]


def _branch_sgu_merge(y_sgu, w_sb_g, a_attn, proj, dm):
    S, G = y_sgu.shape
    D, OFF_G = dm["D"], dm["OFF_G"]
    cw = D // N_CHIPS
    tm = _pick(S, (1024, 512, 256, 128))

    def ep(parts, e_refs, o_refs):
        a_sgu = parts[0].astype(BF16)
        ga = _sigmoid(e_refs[1][...].astype(F32))
        gs = _sigmoid(e_refs[2][...].astype(F32))
        o_refs[0][...] = a_sgu
        o_refs[1][...] = (ga * e_refs[0][...].astype(F32) + gs * a_sgu.astype(F32)).astype(BF16)

    blk = pl.BlockSpec((tm, cw), lambda i, j, k: (i, j))
    return _matmul(
        "branch_sgu_merge", y_sgu, [w_sb_g], dims=NN, grid=(S // tm, N_CHIPS, 1),
        lhs_spec=pl.BlockSpec((tm, G), lambda i, j, k: (i, 0)),
        rhs_specs=[pl.BlockSpec((None, G, cw), lambda i, j, k: (j, 0, 0))],
        acc_shape=(tm, cw), extra=[a_attn, proj, proj],
        extra_specs=[blk, pl.BlockSpec((tm, cw), lambda i, j, k: (i, OFF_G // cw + j)),
                     pl.BlockSpec((tm, cw), lambda i, j, k: (i, (OFF_G + D) // cw + j))],
        out_shape=[jax.ShapeDtypeStruct((S, D), BF16), jax.ShapeDtypeStruct((S, D), BF16)],
        out_specs=[blk, blk], epilogue=ep)


def _residual_matmul(name, a, w_g, h, carry=None):
    S, K = a.shape
    D = w_g.shape[1]
    tm = _pick(S, (512, 256, 128))
    tn = _pick(D, (512, 256, 128))

    def ep(parts, e_refs, o_refs):
        o_refs[0][...] = e_refs[0][...] + parts[0]

    blk = pl.BlockSpec((tm, tn), lambda i, j, k: (i, j))
    return _result(_matmul(
        name, a, [w_g], dims=NN, grid=(S // tm, D // tn, 1),
        lhs_spec=pl.BlockSpec((tm, K), lambda i, j, k: (i, 0)),
        rhs_specs=[pl.BlockSpec((K, tn), lambda i, j, k: (0, j))],
        acc_shape=(tm, tn), extra=[h], extra_specs=[blk],
        out_shape=[jax.ShapeDtypeStruct((S, D), F32)], out_specs=[blk], epilogue=ep, carry=carry), 1, carry)


def _gate_up(hn, w_gu_g, dm, carry=None):
    S, D = hn.shape
    Fd = dm["F"]
    cw = 2 * Fd // N_CHIPS
    tm = _pick(S, (512, 256, 128))
    tn = _pick(cw, (1408, 512, 384, 256, 128))
    nbc = cw // tn
    half = N_CHIPS // 2

    def ep(parts, e_refs, o_refs):
        gate, up = parts[0].astype(BF16), parts[1].astype(BF16)
        o_refs[0][0] = gate
        o_refs[0][1] = up
        g32 = gate.astype(F32)
        o_refs[1][...] = (g32 * _sigmoid(g32) * up.astype(F32)).astype(BF16)

    return _result(_matmul(
        "gate_up", hn, [w_gu_g, w_gu_g], dims=NN, grid=(S // tm, Fd // tn, 1),
        lhs_spec=pl.BlockSpec((tm, D), lambda i, j, k: (i, 0)),
        rhs_specs=[pl.BlockSpec((None, D, tn), lambda i, j, k: (j // nbc, 0, j % nbc)),
                   pl.BlockSpec((None, D, tn), lambda i, j, k: (half + j // nbc, 0, j % nbc))],
        acc_shape=(tm, tn),
        out_shape=[jax.ShapeDtypeStruct((2, S, Fd), BF16), jax.ShapeDtypeStruct((S, Fd), BF16)],
        out_specs=[pl.BlockSpec((2, tm, tn), lambda i, j, k: (0, i, j)), pl.BlockSpec((tm, tn), lambda i, j, k: (i, j))],
        epilogue=ep, carry=carry), 2, carry)


def _down_bwd(dh_b, w_down_g, gu, dm, carry=None):
    S, D = dh_b.shape
    Fd = dm["F"]
    tm = _pick(S, (1024, 512, 256, 128))
    tn = _pick(Fd, (512, 256, 128))

    def ep(parts, e_refs, o_refs):
        gate = e_refs[0][0].astype(F32)
        up = e_refs[0][1].astype(F32)
        s = _sigmoid(gate)
        dact = parts[0]
        o_refs[0][0] = (dact * up * s * (1.0 + gate * (1.0 - s))).astype(BF16)
        o_refs[0][1] = (dact * gate * s).astype(BF16)

    blk = pl.BlockSpec((2, tm, tn), lambda i, j, k: (0, i, j))
    return _result(_matmul(
        "down_bwd", dh_b, [w_down_g], dims=NT, grid=(S // tm, Fd // tn, 1),
        lhs_spec=pl.BlockSpec((tm, D), lambda i, j, k: (i, 0)),
        rhs_specs=[pl.BlockSpec((tn, D), lambda i, j, k: (j, 0))],
        acc_shape=(tm, tn), extra=[gu], extra_specs=[blk],
        out_shape=[jax.ShapeDtypeStruct((2, S, Fd), BF16)], out_specs=[blk], epilogue=ep, carry=carry), 1, carry)


def _gate_up_bwd(dgu, w_gu_g, dm, carry=None):
    S = dgu.shape[1]
    D, Fd = dm["D"], dm["F"]
    cw = 2 * Fd // N_CHIPS
    half = N_CHIPS // 2
    tm = _pick(S, (512, 256, 128))
    tn = _pick(D, (1024, 512, 256, 128))
    return _result(_matmul(
        "gate_up_bwd", dgu, [w_gu_g], dims=NT, grid=(S // tm, D // tn, N_CHIPS),
        lhs_spec=pl.BlockSpec((None, tm, cw), lambda i, j, k: (k // half, i, k % half)),
        rhs_specs=[pl.BlockSpec((None, tn, cw), lambda i, j, k: (k, j, 0))],
        acc_shape=(tm, tn), out_shape=[jax.ShapeDtypeStruct((S, D), F32)],
        out_specs=[pl.BlockSpec((tm, tn), lambda i, j, k: (i, j))], epilogue=_store_epilogue(F32), carry=carry), 1, carry)


def _out_bwd(dh_b, w_out_g, proj, a_attn, a_sgu, dm, carry=None):
    S, D = dh_b.shape
    OFF_G = dm["OFF_G"]
    tm = _pick(S, (1024, 512, 256, 128))
    tn = D // N_CHIPS

    def ep(parts, e_refs, o_refs):
        dm_ = parts[0]
        ga = _sigmoid(e_refs[0][...].astype(F32))
        gs = _sigmoid(e_refs[1][...].astype(F32))
        o_refs[0][...] = (dm_ * ga).astype(BF16)
        o_refs[1][...] = (dm_ * gs).astype(BF16)
        o_refs[2][0] = (dm_ * e_refs[2][...].astype(F32) * ga * (1.0 - ga)).astype(BF16)
        o_refs[2][1] = (dm_ * e_refs[3][...].astype(F32) * gs * (1.0 - gs)).astype(BF16)

    blk = pl.BlockSpec((tm, tn), lambda i, j, k: (i, j))
    return _result(_matmul(
        "out_bwd", dh_b, [w_out_g], dims=NT, grid=(S // tm, D // tn, 1),
        lhs_spec=pl.BlockSpec((tm, D), lambda i, j, k: (i, 0)),
        rhs_specs=[pl.BlockSpec((tn, D), lambda i, j, k: (j, 0))],
        acc_shape=(tm, tn), extra=[proj, proj, a_attn, a_sgu],
        extra_specs=[pl.BlockSpec((tm, tn), lambda i, j, k: (i, OFF_G // tn + j)),
                     pl.BlockSpec((tm, tn), lambda i, j, k: (i, (OFF_G + D) // tn + j)), blk, blk],
        out_shape=[jax.ShapeDtypeStruct((S, D), BF16), jax.ShapeDtypeStruct((S, D), BF16),
                   jax.ShapeDtypeStruct((2, S, D), BF16)],
        out_specs=[blk, blk, pl.BlockSpec((2, tm, tn), lambda i, j, k: (0, i, j))], epilogue=ep, carry=carry), 3, carry)


def _colsharded_bwd(name, dy, w_g, out_dtype, carry=None):
    S = dy.shape[0]
    _, K, cw = w_g.shape
    tm = _pick(S, (512, 256, 128))
    tn = _pick(K, (1024, 512, 256, 128))
    return _result(_matmul(
        name, dy, [w_g], dims=NT, grid=(S // tm, K // tn, N_CHIPS),
        lhs_spec=pl.BlockSpec((tm, cw), lambda i, j, k: (i, k)),
        rhs_specs=[pl.BlockSpec((None, tn, cw), lambda i, j, k: (k, j, 0))],
        acc_shape=(tm, tn), out_shape=[jax.ShapeDtypeStruct((S, K), out_dtype)],
        out_specs=[pl.BlockSpec((tm, tn), lambda i, j, k: (i, j))], epilogue=_store_epilogue(out_dtype),
        carry=carry), 1, carry)


def _wgrad_cols(name, x, dy, carry=None, colsum=False):
    S, R = x.shape
    C = dy.shape[1]
    cw = C // N_CHIPS
    tm = _pick(R, (512, 256, 128) if cw >= 1024 else (1024, 512, 256, 128))
    tk = _pick(S, (2048, 1024, 512, 256, 128))

    def ep(parts, e_refs, o_refs):
        for o, p in zip(o_refs, parts):
            o[...] = p

    out_shape = [jax.ShapeDtypeStruct((N_CHIPS, R, cw), F32)]
    out_specs = [pl.BlockSpec((None, tm, cw), lambda i, j, k: (j, i, 0))]
    if colsum:
        out_shape.append(jax.ShapeDtypeStruct((R // tm, N_CHIPS, 8, cw), F32))
        out_specs.append(pl.BlockSpec((None, None, 8, cw), lambda i, j, k: (i, j, 0, 0)))
    return _result(_matmul(
        name, x, [dy], dims=TN, grid=(R // tm, N_CHIPS, S // tk),
        lhs_spec=pl.BlockSpec((tk, tm), lambda i, j, k: (k, i)),
        rhs_specs=[pl.BlockSpec((tk, cw), lambda i, j, k: (k, j))],
        acc_shape=(tm, cw), out_shape=out_shape, out_specs=out_specs, epilogue=ep,
        carry=carry, rhs_colsum=colsum), len(out_shape), carry)


def _wgrad_gate_up(hn, dgu, dm, carry=None):
    S, D = hn.shape
    Fd = dm["F"]
    cw = 2 * Fd // N_CHIPS
    half = N_CHIPS // 2
    tm = _pick(D, (512, 256, 128))
    tk = _pick(S, (1024, 512, 256, 128))
    return _result(_matmul(
        "wgrad_gate_up", hn, [dgu], dims=TN, grid=(D // tm, N_CHIPS, S // tk),
        lhs_spec=pl.BlockSpec((tk, tm), lambda i, j, k: (k, i)),
        rhs_specs=[pl.BlockSpec((None, tk, cw), lambda i, j, k: (j // half, k, j % half))],
        acc_shape=(tm, cw), out_shape=[jax.ShapeDtypeStruct((N_CHIPS, D, cw), F32)],
        out_specs=[pl.BlockSpec((None, tm, cw), lambda i, j, k: (j, i, 0))], epilogue=_store_epilogue(F32),
        carry=carry), 1, carry)


def _wgrad_rows(name, x, dy):
    S, R = x.shape
    C = dy.shape[1]
    rw = R // N_CHIPS
    tn = _pick(C, (1024, 512, 256, 128))
    tk = _pick(S, (1024, 512, 256, 128))
    return _matmul(
        name, x, [dy], dims=TN, grid=(N_CHIPS, C // tn, S // tk),
        lhs_spec=pl.BlockSpec((tk, rw), lambda i, j, k: (k, i)),
        rhs_specs=[pl.BlockSpec((tk, tn), lambda i, j, k: (k, j))],
        acc_shape=(rw, tn), out_shape=[jax.ShapeDtypeStruct((N_CHIPS, rw, C), F32)],
        out_specs=[pl.BlockSpec((None, rw, tn), lambda i, j, k: (i, 0, j))], epilogue=_store_epilogue(F32))[0]


def _place():
    x, y, c = lax.axis_index("x"), lax.axis_index("y"), lax.axis_index("c")
    others = [(1 - x, y), (x, 1 - y), (1 - x, 1 - y)]
    return x, y, c, others


def _chip_index(chip):
    return 2 * chip[0] + chip[1]


def _gather_weights(bufs):
    n = len(bufs)

    def copies(src, out, send_sems, recv_sems):
        x, y, c, others = _place()

        def half(ref, chip_idx, hc):
            r2 = ref.shape[1] // 2
            return ref.at[chip_idx, pl.ds(hc * r2, r2), :]

        def copy(t, k, chip, hc, to):
            return pltpu.make_async_remote_copy(
                src_ref=half(src[t], _chip_index(chip), hc), dst_ref=half(out[t], _chip_index(chip), hc),
                send_sem=send_sems.at[6 * t + k], recv_sem=recv_sems.at[6 * t + k],
                device_id=to, device_id_type=MESH)

        me, sibling = (x, y, c), (x, y, 1 - c)
        pairs = [(t, j, chip) for t in range(n) for j, chip in enumerate(others)]
        sent = [copy(t, j, (x, y), c, (*chip, c)) for t, j, chip in pairs]
        landed = [copy(t, j, chip, c, me) for t, j, chip in pairs]
        passed = [copy(t, 3 + j, chip, c, sibling) for t, j, chip in pairs]
        handed = [copy(t, 3 + j, chip, 1 - c, me) for t, j, chip in pairs]
        return sent, landed, passed, handed

    def start(src, out, send_sems, recv_sems):
        for cp in copies(src, out, send_sems, recv_sems)[0]:
            cp.start()

    def finish(src, out, send_sems, recv_sems):
        sent, landed, passed, handed = copies(src, out, send_sems, recv_sems)
        for arrival, forward in zip(landed, passed):
            arrival.wait_recv()
            forward.start()
        for cp in handed:
            cp.wait_recv()
        for cp in sent + passed:
            cp.wait_send()

    return _Comm("gather_weights", bufs, [jax.ShapeDtypeStruct(b.shape, BF16) for b in bufs],
                 {t: t for t in range(n)}, 6 * n, start, finish)


def _sibling_exchange(grads):
    n = len(grads)
    shapes = [g.shape for g in grads]

    def copies(src, land, send_sems, recv_sems):
        x, y, c, _ = _place()
        res = []
        for t in range(n):
            r2 = shapes[t][1] // 2
            res.append(pltpu.make_async_remote_copy(
                src_ref=src[t].at[:, pl.ds((1 - c) * r2, r2), :], dst_ref=land[t],
                send_sem=send_sems.at[t], recv_sem=recv_sems.at[t], device_id=(x, y, 1 - c), device_id_type=MESH))
        return res

    def start(*refs):
        for cp in copies(*refs):
            cp.start()

    def finish(*refs):
        remote = copies(*refs)
        for cp in remote:
            cp.wait_recv()
        for cp in remote:
            cp.wait_send()

    return _Comm("sibling_exchange", grads, [jax.ShapeDtypeStruct((s[0], s[1] // 2, s[2]), F32) for s in shapes],
                 {}, n, start, finish)


def _chip_exchange(sends):
    n = len(sends)
    shapes = [s.shape for s in sends]

    def copies(snd, got, send_sems, recv_sems):
        x, y, c, others = _place()
        return [pltpu.make_async_remote_copy(
            src_ref=snd[t].at[_chip_index(chip)], dst_ref=got[t].at[j],
            send_sem=send_sems.at[3 * t + j], recv_sem=recv_sems.at[3 * t + j],
            device_id=(*chip, c), device_id_type=MESH) for t in range(n) for j, chip in enumerate(others)]

    def start(*refs):
        for cp in copies(*refs):
            cp.start()

    def finish(*refs):
        remote = copies(*refs)
        for cp in remote:
            cp.wait_recv()
        for cp in remote:
            cp.wait_send()

    return _Comm("chip_exchange", sends, [jax.ShapeDtypeStruct((3, s[1], s[2]), BF16) for s in shapes],
                 {}, 3 * n, start, finish)


def _sibling_share(fulls):
    n = len(fulls)
    shapes = [f.shape for f in fulls]

    def copies(src, out, send_sems, recv_sems, mine):
        x, y, c, _ = _place()
        hc = c if mine else 1 - c
        res = []
        for t in range(n):
            r2 = shapes[t][0] // 2
            res.append(pltpu.make_async_remote_copy(
                src_ref=src[t].at[pl.ds(hc * r2, r2), :], dst_ref=out[t].at[pl.ds(hc * r2, r2), :],
                send_sem=send_sems.at[t], recv_sem=recv_sems.at[t], device_id=(x, y, 1 - c), device_id_type=MESH))
        return res

    def start(*refs):
        for cp in copies(*refs, mine=True):
            cp.start()

    def finish(*refs):
        for cp in copies(*refs, mine=False):
            cp.wait_recv()
        for cp in copies(*refs, mine=True):
            cp.wait_send()

    return _Comm("sibling_share", fulls, [jax.ShapeDtypeStruct(s, F32) for s in shapes],
                 {t: t for t in range(n)}, n, start, finish)


def _gather_all(v):
    R, C = v.shape

    def body(v_ref, out_ref, send_sems, recv_sems, local_sem):
        x, y, c, others = _place()
        me, sibling = (x, y, c), (x, y, 1 - c)

        def rows(px, py, pc):
            return out_ref.at[4 * px + 2 * py + pc]

        def copy(k, block, to, src=None):
            return pltpu.make_async_remote_copy(
                src_ref=rows(*block) if src is None else src, dst_ref=rows(*block),
                send_sem=send_sems.at[k], recv_sem=recv_sems.at[k], device_id=to, device_id_type=MESH)

        mine = pltpu.make_async_copy(v_ref, rows(*me), local_sem)
        mine.start()
        first = [copy(0, me, sibling, src=v_ref)]
        first += [copy(1 + j, me, (*chip, c), src=v_ref) for j, chip in enumerate(others)]
        for cp in first:
            cp.start()
        passed = [copy(4 + j, (*chip, c), sibling) for j, chip in enumerate(others)]
        for j, chip in enumerate(others):
            copy(1 + j, (*chip, c), me).wait_recv()
            passed[j].start()
        copy(0, sibling, me).wait_recv()
        for j, chip in enumerate(others):
            copy(4 + j, (*chip, 1 - c), me).wait_recv()
        for cp in first + passed:
            cp.wait_send()
        mine.wait()

    return pl.pallas_call(
        body, name="gather_all", in_specs=[ANY], out_specs=ANY,
        out_shape=jax.ShapeDtypeStruct((8, R, C), F32),
        scratch_shapes=[pltpu.SemaphoreType.DMA((7,)), pltpu.SemaphoreType.DMA((7,)), pltpu.SemaphoreType.DMA],
    )(v)


def _my_chip():
    return 2 * lax.axis_index("x") + lax.axis_index("y")


def _my_core():
    return lax.axis_index("c")


def _pair_sum(grad, land):
    K, R2, C = land.shape
    tm = _row_tile(R2, C)
    nrb = R2 // tm

    def body(a_ref, b_ref, s_ref, sb_ref):
        s = a_ref[...] + b_ref[...]
        s_ref[...] = s
        sb_ref[...] = s.astype(BF16)

    blk = pl.BlockSpec((None, tm, C), lambda k, r: (k, r, 0))
    return pl.pallas_call(
        body, name="pair_sum", grid=(K, nrb),
        in_specs=[pl.BlockSpec((None, tm, C), lambda k, r: (k, _my_core() * nrb + r, 0)), blk],
        out_specs=[blk, blk],
        out_shape=[jax.ShapeDtypeStruct((K, R2, C), F32), jax.ShapeDtypeStruct((K, R2, C), BF16)],
        compiler_params=_params(("parallel", "parallel")),
    )(grad, land)


def _chip_sum(sums, got):
    _, R2, C = sums.shape
    tm = _row_tile(R2, C)
    nrb = R2 // tm

    def body(o_ref, g_ref, s_ref):
        s_ref[...] = ((o_ref[...] + g_ref[0].astype(F32)) + g_ref[1].astype(F32)) + g_ref[2].astype(F32)

    return pl.pallas_call(
        body, name="chip_sum", grid=(nrb,),
        in_specs=[pl.BlockSpec((None, tm, C), lambda r: (_my_chip(), r, 0)),
                  pl.BlockSpec((3, tm, C), lambda r: (0, r, 0))],
        out_specs=pl.BlockSpec((tm, C), lambda r: (_my_core() * nrb + r, 0)),
        out_shape=jax.ShapeDtypeStruct((2 * R2, C), F32),
        compiler_params=_params(("parallel",)),
    )(sums, got)


def _adamw_math(w, g, m, v):
    m = ADAM_B1 * m + (1.0 - ADAM_B1) * g
    v = ADAM_B2 * v + (1.0 - ADAM_B2) * (g * g)
    m_hat = m / (1.0 - ADAM_B1 ** ADAM_STEP)
    v_hat = v / (1.0 - ADAM_B2 ** ADAM_STEP)
    delta = -ADAM_LR * (m_hat / (jnp.sqrt(v_hat) + ADAM_EPS) + ADAM_WD * w)
    return delta, m, v


def _adamw_stacked(grads, w, m, v):
    L, R, C = w.shape
    tm = _row_tile(R, C)
    nrb = R // tm

    def body(*refs):
        g_refs = refs[:L]
        w_ref, m_ref, v_ref, go_ref, d_ref, mo_ref, vo_ref = refs[L:]
        l = pl.program_id(0)
        for ll in range(L):
            @pl.when(l == ll)
            def _(ll=ll):
                g = g_refs[ll][...]
                delta, mn, vn = _adamw_math(w_ref[...], g, m_ref[...], v_ref[...])
                go_ref[...] = g
                d_ref[...] = delta
                mo_ref[...] = mn
                vo_ref[...] = vn

    stacked = pl.BlockSpec((None, tm, C), lambda l, r: (l, r, 0))
    g_specs = [pl.BlockSpec((tm, C), lambda l, r, ll=ll: (jnp.where(l == ll, r, 0), 0)) for ll in range(L)]
    shp = jax.ShapeDtypeStruct((L, R, C), F32)
    return pl.pallas_call(
        body, name="adamw", grid=(L, nrb),
        in_specs=[*g_specs, stacked, stacked, stacked], out_specs=[stacked] * 4, out_shape=[shp] * 4,
        compiler_params=_params(("arbitrary", "arbitrary")),
    )(*grads, w, m, v)


def _adamw_small(parts, w, m, v):
    _, R, C = parts.shape
    tm = _row_tile(R, 8 * C)

    def body(p_ref, w_ref, m_ref, v_ref, go_ref, d_ref, mo_ref, vo_ref):
        g = p_ref[0]
        for k in range(1, 8):
            g = g + p_ref[k]
        delta, mn, vn = _adamw_math(w_ref[...], g, m_ref[...], v_ref[...])
        go_ref[...] = g
        d_ref[...] = delta
        mo_ref[...] = mn
        vo_ref[...] = vn

    blk = pl.BlockSpec((tm, C), lambda i: (i, 0))
    shp = jax.ShapeDtypeStruct((R, C), F32)
    return pl.pallas_call(
        body, name="adamw_small", grid=(R // tm,),
        in_specs=[pl.BlockSpec((8, tm, C), lambda i: (0, i, 0)), blk, blk, blk],
        out_specs=[blk] * 4, out_shape=[shp] * 4,
        compiler_params=_params(("parallel",)),
    )(parts, w, m, v)


def _cast_place(w, layer):
    _, R, C = w.shape
    tm = _row_tile(R, C)

    def body(w_ref, o_ref):
        o_ref[...] = w_ref[...].astype(BF16)

    return pl.pallas_call(
        body, name="cast_place", grid=(R // tm,),
        in_specs=[pl.BlockSpec((None, tm, C), lambda r: (layer, r, 0))],
        out_specs=pl.BlockSpec((None, tm, C), lambda r: (_my_chip(), r, 0)),
        out_shape=jax.ShapeDtypeStruct((N_CHIPS, R, C), BF16),
        compiler_params=_params(("parallel",)),
    )(w)


def _trig_tables(positions):
    half = ROPE_DIM // 2
    inv_freq = ROPE_THETA ** (-jnp.arange(0, ROPE_DIM, 2, dtype=F32) / ROPE_DIM)
    ang = positions.astype(F32)[:, None] * inv_freq
    cos, sin = jnp.cos(ang), jnp.sin(ang)
    S = positions.shape[0]
    zeros = lambda w: jnp.zeros((S, w), F32)
    cos_h = jnp.concatenate([cos, cos, jnp.ones((S, HEAD_DIM - ROPE_DIM), F32)], axis=1)
    sa_h = jnp.concatenate([-sin, zeros(HEAD_DIM - half)], axis=1)
    sb_h = jnp.concatenate([zeros(half), sin, zeros(HEAD_DIM - ROPE_DIM)], axis=1)
    rep = LANES // HEAD_DIM
    return [jnp.tile(t, (1, rep)) for t in (cos_h, sa_h, sb_h)]


def _row(vec):
    return vec.reshape(1, -1)


def _lane_row(vec):
    return jnp.zeros((8, LANES), F32).at[0, :vec.shape[0]].set(vec)


def _pack(pieces, rows):
    flat = jnp.concatenate([p.reshape(-1).astype(F32) for p in pieces])
    return jnp.pad(flat, (0, rows * LANES - flat.shape[0])).reshape(rows, LANES)


def kernel(x, positions, norm1_g, w_in, b_in, sinks, sgu_ln_g, sgu_ln_b, sgu_w, sgu_b, w_attn_branch, w_sgu_branch, w_out, norm2_g, w_gate_up, w_down, final_g, loss_target, m_norm1_g, m_w_in, m_b_in, m_sinks, m_sgu_ln_g, m_sgu_ln_b, m_sgu_w, m_sgu_b, m_w_attn_branch, m_w_sgu_branch, m_w_out, m_norm2_g, m_w_gate_up, m_w_down, m_final_g, v_norm1_g, v_w_in, v_b_in, v_sinks, v_sgu_ln_g, v_sgu_ln_b, v_sgu_w, v_sgu_b, v_w_attn_branch, v_w_sgu_branch, v_w_out, v_norm2_g, v_w_gate_up, v_w_down, v_final_g):
    L = norm1_g.shape[0]
    S, D = x.shape[1], x.shape[2]
    NQ = sinks.shape[1]
    A = NQ * HEAD_DIM
    KV = N_KV_HEADS * HEAD_DIM
    G = sgu_ln_g.shape[1]
    NG = sgu_w.shape[1]
    IN = b_in.shape[1]
    Fd = w_down.shape[1] * N_CHIPS
    dm = dict(D=D, A=A, KV=KV, NQ=NQ, G=G, NG=NG, IN=IN, F=Fd,
              OFF_K=A, OFF_V=A + KV, OFF_Z=A + 2 * KV, OFF_G=A + 2 * KV + 2 * G)
    assert sgu_w.shape[2] == WINDOW and G == NG * LANES and IN == dm["OFF_G"] + 2 * D

    h = x[0]
    target = loss_target[0]
    trig = _trig_tables(positions[0])
    tril = jnp.tril(jnp.ones((WINDOW, WINDOW), bool))

    big = [w_in, w_attn_branch, w_sgu_branch, w_out, w_gate_up, w_down]
    big_m = [m_w_in, m_w_attn_branch, m_w_sgu_branch, m_w_out, m_w_gate_up, m_w_down]
    big_v = [v_w_in, v_w_attn_branch, v_w_sgu_branch, v_w_out, v_w_gate_up, v_w_down]

    placed = [[_cast_place(w, l) for w in big] for l in range(L)]
    IN_, AB, SB, OUT, GU, DOWN = range(len(big))
    gathered = [[None] * len(big) for _ in range(L)]
    gathered[0][IN_] = _gather_weights([placed[0][IN_]]).run()[0]

    def fetch(layer, idx):
        return _gather_weights([placed[layer][t] for t in idx]) if layer < L else None

    def fetched(layer, idx, res):
        if layer >= L:
            return res
        main, got = res
        for t, g in zip(idx, got):
            gathered[layer][t] = g
        return main

    def weights(l):
        flat = lambda w, rows: None if w is None else w.reshape(rows, D)
        w_in_g, w_ab_g, w_sb_g, w_out_g, w_gu_g, w_down_g = gathered[l]
        return (w_in_g, w_ab_g, w_sb_g, flat(w_out_g, D), w_gu_g, flat(w_down_g, Fd))

    def small(l):
        return dict(
            g1=_row(norm1_g[l]), b_in=_row(b_in[l]), sink=_lane_row(sinks[l]),
            ln_g=_row(sgu_ln_g[l]), ln_b=_row(sgu_ln_b[l]),
            w_tril=jnp.where(tril[None], sgu_w[l], 0.0).astype(BF16),
            b_t=jnp.zeros((WINDOW, LANES), F32).at[:, :NG].set(sgu_b[l].T),
            g2=_row(norm2_g[l]))

    saved = []
    for l in range(L):
        sp = small(l)
        xn = _rms_fwd(h, sp["g1"])
        now = [AB, SB, OUT, GU] if l == 0 else [DOWN]
        proj = fetched(l, now, _in_proj(xn, gathered[l][IN_], sp["b_in"], dm, carry=fetch(l, now)))
        w_in_g, w_ab_g, w_sb_g, w_out_g = weights(l)[:4]
        y_attn, lse = _attn_fwd(proj, trig, sp["sink"], dm)
        y_sgu = _sgu_fwd(proj, sp["w_tril"], sp["b_t"], sp["ln_g"], sp["ln_b"], dm)
        a_attn = _branch_attn(y_attn, w_ab_g, dm)
        a_sgu, merged = _branch_sgu_merge(y_sgu, w_sb_g, a_attn, proj, dm)
        if l == 0:
            h_mid = fetched(l, [DOWN], _residual_matmul("out_proj", merged, w_out_g, h, carry=fetch(l, [DOWN])))
        else:
            h_mid = _residual_matmul("out_proj", merged, w_out_g, h)
        w_gu_g, w_down_g = weights(l)[4:]
        hn = _rms_fwd(h_mid, sp["g2"])
        ahead = [IN_, AB, SB, OUT]
        gu, act = fetched(l + 1, ahead, _gate_up(hn, w_gu_g, dm, carry=fetch(l + 1, ahead)))
        h_out = fetched(l + 1, [GU], _residual_matmul("down_proj", act, w_down_g, h_mid, carry=fetch(l + 1, [GU])))
        saved.append(dict(h=h, xn=xn, proj=proj, y_attn=y_attn, lse=lse, y_sgu=y_sgu, a_attn=a_attn, a_sgu=a_sgu,
                          merged=merged, h_mid=h_mid, hn=hn, gu=gu, act=act))
        h = h_out

    dh, dh_b, d_final, loss_part = _loss_head(h, _row(final_g), target)

    small_grads = [None] * L
    reduced = [[None] * len(big) for _ in range(L)]
    early, late = [GU, DOWN], [IN_, AB, SB, OUT]

    def riding(has_carry, res):
        return res if has_carry else (res, None)

    def sums_of(grads, land):
        return zip(*[_pair_sum(g, d) for g, d in zip(grads, land)])

    def file_reduced(layer, idx, fulls):
        for t, f in zip(idx, fulls):
            reduced[layer][t] = f

    late_grads = None
    for l in reversed(range(L)):
        w_in_g, w_ab_g, w_sb_g, w_out_g, w_gu_g, w_down_g = weights(l)
        sp, sv = small(l), saved[l]
        have = late_grads is not None
        dgu, land = riding(have,_down_bwd(dh_b, w_down_g, sv["gu"], dm,
                                                    carry=_sibling_exchange(late_grads) if have else None))
        if have:
            sums, sends = sums_of(late_grads, land)
        g_down = _wgrad_rows("wgrad_down", sv["act"], dh_b)
        dhn, got = riding(have,_gate_up_bwd(dgu, w_gu_g, dm,
                                                     carry=_chip_exchange(list(sends)) if have else None))
        if have:
            fulls = [_chip_sum(s, g) for s, g in zip(sums, got)]
        g_gu, shared = riding(have,_wgrad_gate_up(sv["hn"], dgu, dm,
                                                           carry=_sibling_share(fulls) if have else None))
        if have:
            file_reduced(l + 1, late, shared)
        dh_mid, dh_mid_b, d_g2 = _rms_bwd(dhn, sv["h_mid"], sp["g2"], dh)
        (da_attn, da_sgu, dgate), land = _out_bwd(dh_mid_b, w_out_g, sv["proj"], sv["a_attn"], sv["a_sgu"], dm,
                                                   carry=_sibling_exchange([g_gu, g_down]))
        sums, sends = sums_of([g_gu, g_down], land)
        g_out = _wgrad_rows("wgrad_out", sv["merged"], dh_mid_b)
        dy_attn = _colsharded_bwd("branch_attn_bwd", da_attn, w_ab_g, BF16)
        dy_sgu = _colsharded_bwd("branch_sgu_bwd", da_sgu, w_sb_g, BF16)
        g_ab = _wgrad_cols("wgrad_attn_branch", sv["y_attn"], da_attn)
        g_sb = _wgrad_cols("wgrad_sgu_branch", sv["y_sgu"], da_sgu)
        dq, dk, dv, d_sink = _attn_bwd(sv["proj"], trig, sp["sink"], sv["y_attn"], sv["lse"], dy_attn, dm)
        dz, d_sgu_w, d_bt, d_lng, d_lnb = _sgu_bwd(sv["proj"], sp["w_tril"], sp["b_t"], sp["ln_g"], sp["ln_b"], dy_sgu, dm)
        dproj = jnp.concatenate([dq, dk, dv, dz, dgate[0], dgate[1]], axis=1)
        dxn, got = _colsharded_bwd("in_proj_bwd", dproj, w_in_g, F32, carry=_chip_exchange(list(sends)))
        fulls = [_chip_sum(s, g) for s, g in zip(sums, got)]
        (g_in, d_bin), shared = _wgrad_cols("wgrad_in", sv["xn"], dproj, carry=_sibling_share(fulls), colsum=True)
        file_reduced(l, early, shared)
        dh, dh_b, d_g1 = _rms_bwd(dxn, sv["h"], sp["g1"], dh_mid)
        late_grads = [g_in, g_ab, g_sb, g_out]
        small_grads[l] = dict(norm1_g=d_g1[0], b_in=d_bin[0, :, 0, :].reshape(-1), sinks=d_sink[0, :NQ],
                              sgu_ln_g=d_lng[0], sgu_ln_b=d_lnb[0], sgu_w=d_sgu_w, sgu_b=d_bt[:, :NG].T, norm2_g=d_g2[0])
    grad_x = dh[None]

    land = _sibling_exchange(late_grads).run()
    sums, sends = sums_of(late_grads, land)
    got = _chip_exchange(list(sends)).run()
    file_reduced(0, late, _sibling_share([_chip_sum(s, g) for s, g in zip(sums, got)]).run())
    big_out =[_adamw_stacked([reduced[l][t] for l in range(L)], big[t], big_m[t], big_v[t]) for t in range(len(big))]

    names = ["norm1_g", "b_in", "sinks", "sgu_ln_g", "sgu_ln_b", "sgu_w", "sgu_b", "norm2_g"]
    small_w = [norm1_g, b_in, sinks, sgu_ln_g, sgu_ln_b, sgu_w, sgu_b, norm2_g, final_g]
    small_m = [m_norm1_g, m_b_in, m_sinks, m_sgu_ln_g, m_sgu_ln_b, m_sgu_w, m_sgu_b, m_norm2_g, m_final_g]
    small_v = [v_norm1_g, v_b_in, v_sinks, v_sgu_ln_g, v_sgu_ln_b, v_sgu_w, v_sgu_b, v_norm2_g, v_final_g]
    small_g = [jnp.stack([small_grads[l][nm] for l in range(L)]) for nm in names] + [d_final[0]]
    sizes = [w.size for w in small_w]
    total = sum(sizes) + 1
    rows = -(-total // (512 * LANES)) * 512
    loss_piece = jnp.sum(loss_part[0]).reshape(1)
    packed_g = _pack(small_g + [loss_piece], rows)
    one = jnp.ones((1,), F32)
    parts = _gather_all(packed_g)
    outs = _adamw_small(parts, _pack(small_w + [one], rows), _pack(small_m + [one], rows), _pack(small_v + [one], rows))

    def unpack(p):
        flat = p.reshape(-1)
        res, off = [], 0
        for w, n in zip(small_w, sizes):
            res.append(flat[off:off + n].reshape(w.shape))
            off += n
        return res, flat[off]

    (sg, loss), (sd, _), (smm, _), (svv, _) = [unpack(o) for o in outs]

    order = ["norm1_g", "w_in", "b_in", "sinks", "sgu_ln_g", "sgu_ln_b", "sgu_w", "sgu_b", "w_attn_branch",
             "w_sgu_branch", "w_out", "norm2_g", "w_gate_up", "w_down", "final_g"]
    big_names = ["w_in", "w_attn_branch", "w_sgu_branch", "w_out", "w_gate_up", "w_down"]
    small_names = names + ["final_g"]

    def collect(kind):
        res = []
        for nm in order:
            if nm in big_names:
                res.append(big_out[big_names.index(nm)][kind])
            else:
                res.append((sg, sd, smm, svv)[kind][small_names.index(nm)])
        return res

    return (loss, grad_x, *collect(0), *collect(1), *collect(2), *collect(3))
```

```python
import math

import jax
import jax.numpy as jnp
from jax import lax
from jax.experimental import pallas as pl
from jax.experimental.pallas import tpu as pltpu

F32 = jnp.float32
BF16 = jnp.bfloat16
MESH = pl.DeviceIdType.MESH
ANY = pl.BlockSpec(memory_space=pl.ANY)

HEAD_DIM = 64
N_KV_HEADS = 4
WINDOW = 128
ROPE_DIM = HEAD_DIM // 4
ROPE_THETA = 500000.0
EPS = 1e-5
NEG = -1e30
N_CHIPS = 4
LANES = 128
V7X_VMEM_LIMIT = 56 * 1024 * 1024

ADAM_LR = 0.001
ADAM_B1 = 0.9
ADAM_B2 = 0.999
ADAM_EPS = 1e-08
ADAM_WD = 0.01
ADAM_STEP = 10

NN = (((1,), (0,)), ((), ()))
NT = (((1,), (1,)), ((), ()))
TN = (((0,), (0,)), ((), ()))


ROW_TILES = (1024, 512, 256, 128, 64, 32, 16, 8)
BLOCK_BYTES = 2 * 1024 * 1024


def _pick(n, prefs):
    for p in prefs:
        if n % p == 0:
            return p
    raise ValueError(f"no tile for {n} among {prefs}")


def _row_tile(rows, cols, itemsize=4):
    return _pick(rows, [t for t in ROW_TILES if t * cols * itemsize <= BLOCK_BYTES or t == ROW_TILES[-1]])


def _dot(a, b, dims):
    return lax.dot_general(a, b, dims, preferred_element_type=F32)


def _sigmoid(x):
    return 1.0 / (1.0 + jnp.exp(-x))


def _gelu(x):
    return 0.5 * x * (1.0 + lax.erf(x * (1.0 / math.sqrt(2.0))))


def _gelu_grad(x):
    return 0.5 * (1.0 + lax.erf(x * (1.0 / math.sqrt(2.0)))) + x * jnp.exp(-0.5 * x * x) * (1.0 / math.sqrt(2.0 * math.pi))


def _params(sem):
    return pltpu.CompilerParams(dimension_semantics=sem, vmem_limit_bytes=V7X_VMEM_LIMIT)


def _matmul(name, lhs, rhs_list, *, dims, grid, lhs_spec, rhs_specs, acc_shape, out_shape, out_specs,
            epilogue, extra=(), extra_specs=(), carry=None, rhs_colsum=False, cols_outer=False):
    if cols_outer:
        swap = lambda s: pl.BlockSpec(s.block_shape, lambda j, i, k, f=s.index_map: f(i, j, k))
        grid = (grid[1], grid[0], grid[2])
        lhs_spec, rhs_specs = swap(lhs_spec), [swap(s) for s in rhs_specs]
        extra_specs, out_specs = [swap(s) for s in extra_specs], [swap(s) for s in out_specs]
    gk = grid[2]
    nr, ne, no = len(rhs_list), len(extra), len(out_shape)
    nci = len(carry.ins) if carry else 0
    nco = len(carry.outs) if carry else 0
    acc_shapes = [acc_shape] * nr + ([(8, acc_shape[1])] if rhs_colsum else [])
    nacc = len(acc_shapes) if gk > 1 else 0

    def body(*refs):
        a_ref = refs[0]
        b_refs = refs[1:1 + nr]
        e_refs = refs[1 + nr:1 + nr + ne]
        base = 1 + nr + ne
        ci_refs = refs[base:base + nci]
        o_refs = refs[base + nci:base + nci + no]
        co_refs = refs[base + nci + no:base + nci + no + nco]
        acc_refs = refs[base + nci + no + nco:base + nci + no + nco + nacc]
        sems = refs[base + nci + no + nco + nacc:]
        ids = [pl.program_id(d) for d in range(3)]
        if carry:
            @pl.when((ids[0] == 0) & (ids[1] == 0) & (ids[2] == 0))
            def _():
                carry.start(ci_refs, co_refs, *sems)

        a = a_ref[...]
        parts = [_dot(a, b[...], dims) for b in b_refs]
        if rhs_colsum:
            b0 = b_refs[0][...]
            parts.append(_dot(jnp.ones((8, b0.shape[0]), b0.dtype), b0, NN))
        if gk == 1:
            epilogue(parts, e_refs, o_refs)
        else:
            k = ids[2]

            @pl.when(k == 0)
            def _():
                for acc, p in zip(acc_refs, parts):
                    acc[...] = p

            @pl.when(k > 0)
            def _():
                for acc, p in zip(acc_refs, parts):
                    acc[...] += p

            @pl.when(k == gk - 1)
            def _():
                epilogue([acc[...] for acc in acc_refs], e_refs, o_refs)

        if carry:
            @pl.when((ids[0] == grid[0] - 1) & (ids[1] == grid[1] - 1) & (ids[2] == grid[2] - 1))
            def _():
                carry.finish(ci_refs, co_refs, *sems)

    scratch = [pltpu.VMEM(s, F32) for s in acc_shapes[:nacc]]
    kwargs = {}
    if carry:
        scratch += carry.sem_scratch()
        kwargs["input_output_aliases"] = {1 + nr + ne + i: no + o for i, o in carry.aliases.items()}
    outs = pl.pallas_call(
        body, name=name, grid=grid,
        in_specs=[lhs_spec, *rhs_specs, *extra_specs, *([ANY] * nci)],
        out_specs=[*out_specs, *([ANY] * nco)],
        out_shape=[*out_shape, *(carry.outs if carry else [])], scratch_shapes=scratch,
        compiler_params=_params(("arbitrary",) * 3 if carry else ("parallel", "parallel", "arbitrary")),
        **kwargs,
    )(lhs, *rhs_list, *extra, *(carry.ins if carry else []))
    return outs


class _Comm:
    def __init__(self, name, ins, outs, aliases, n_sems, start, finish):
        self.name, self.ins, self.outs, self.aliases, self.n_sems = name, list(ins), list(outs), dict(aliases), n_sems
        self.start, self.finish = start, finish

    def sem_scratch(self):
        return [pltpu.SemaphoreType.DMA((self.n_sems,)), pltpu.SemaphoreType.DMA((self.n_sems,))]

    def run(self):
        ni = len(self.ins)

        def body(*refs):
            in_refs, out_refs, sems = refs[:ni], refs[ni:ni + len(self.outs)], refs[ni + len(self.outs):]
            self.start(in_refs, out_refs, *sems)
            self.finish(in_refs, out_refs, *sems)

        return pl.pallas_call(
            body, name=self.name, in_specs=[ANY] * ni, out_specs=[ANY] * len(self.outs), out_shape=self.outs,
            input_output_aliases=self.aliases, scratch_shapes=self.sem_scratch(),
        )(*self.ins)


def _store_epilogue(dtype):
    def ep(parts, e_refs, o_refs):
        o_refs[0][...] = parts[0].astype(dtype)
    return ep


def _rms_fwd(h, g_row):
    S, D = h.shape
    tm = _row_tile(S, D)

    def body(h_ref, g_ref, o_ref):
        x = h_ref[...]
        r = lax.rsqrt(jnp.mean(x * x, axis=-1, keepdims=True) + EPS)
        o_ref[...] = (x * r * g_ref[...]).astype(BF16)

    return pl.pallas_call(
        body, name="rms_fwd", grid=(S // tm,),
        in_specs=[pl.BlockSpec((tm, D), lambda i: (i, 0)), pl.BlockSpec((1, D), lambda i: (0, 0))],
        out_specs=pl.BlockSpec((tm, D), lambda i: (i, 0)),
        out_shape=jax.ShapeDtypeStruct((S, D), BF16),
        compiler_params=_params(("parallel",)),
    )(h, g_row)


def _rms_bwd(dy, h, g_row, dres):
    S, D = h.shape
    tm = _row_tile(S, D)

    def body(dy_ref, h_ref, g_ref, dres_ref, dh_ref, dhb_ref, dg_ref):
        i = pl.program_id(0)
        x = h_ref[...]
        d = dy_ref[...]
        r = lax.rsqrt(jnp.mean(x * x, axis=-1, keepdims=True) + EPS)
        dg = d * g_ref[...]
        dot = jnp.mean(dg * x, axis=-1, keepdims=True)
        dh = dres_ref[...] + r * dg - x * (r * r * r) * dot
        dh_ref[...] = dh
        dhb_ref[...] = dh.astype(BF16)
        part = jnp.sum(d * x * r, axis=0, keepdims=True)

        @pl.when(i == 0)
        def _():
            dg_ref[...] = jnp.zeros_like(dg_ref)

        dg_ref[0:1, :] += part

    return pl.pallas_call(
        body, name="rms_bwd", grid=(S // tm,),
        in_specs=[pl.BlockSpec((tm, D), lambda i: (i, 0)), pl.BlockSpec((tm, D), lambda i: (i, 0)),
                  pl.BlockSpec((1, D), lambda i: (0, 0)), pl.BlockSpec((tm, D), lambda i: (i, 0))],
        out_specs=[pl.BlockSpec((tm, D), lambda i: (i, 0)), pl.BlockSpec((tm, D), lambda i: (i, 0)),
                   pl.BlockSpec((8, D), lambda i: (0, 0))],
        out_shape=[jax.ShapeDtypeStruct((S, D), F32), jax.ShapeDtypeStruct((S, D), BF16),
                   jax.ShapeDtypeStruct((8, D), F32)],
        compiler_params=_params(("arbitrary",)),
    )(dy, h, g_row, dres)


def _loss_head(h, g_row, target):
    S, D = h.shape
    tm = _row_tile(S, D)

    def body(h_ref, g_ref, t_ref, dh_ref, dhb_ref, dg_ref, loss_ref):
        i = pl.program_id(0)
        x = h_ref[...]
        g = g_ref[...]
        r = lax.rsqrt(jnp.mean(x * x, axis=-1, keepdims=True) + EPS)
        y = x * r * g
        e = y - t_ref[...]
        d = e * (1.0 / D)
        dg = d * g
        dot = jnp.mean(dg * x, axis=-1, keepdims=True)
        dh = r * dg - x * (r * r * r) * dot
        dh_ref[...] = dh
        dhb_ref[...] = dh.astype(BF16)

        @pl.when(i == 0)
        def _():
            dg_ref[...] = jnp.zeros_like(dg_ref)
            loss_ref[...] = jnp.zeros_like(loss_ref)

        dg_ref[0:1, :] += jnp.sum(d * x * r, axis=0, keepdims=True)
        loss_ref[0:1, :] += jnp.sum((0.5 / D) * e * e, axis=0, keepdims=True)

    return pl.pallas_call(
        body, name="loss_head", grid=(S // tm,),
        in_specs=[pl.BlockSpec((tm, D), lambda i: (i, 0)), pl.BlockSpec((1, D), lambda i: (0, 0)),
                  pl.BlockSpec((tm, D), lambda i: (i, 0))],
        out_specs=[pl.BlockSpec((tm, D), lambda i: (i, 0)), pl.BlockSpec((tm, D), lambda i: (i, 0)),
                   pl.BlockSpec((8, D), lambda i: (0, 0)), pl.BlockSpec((8, D), lambda i: (0, 0))],
        out_shape=[jax.ShapeDtypeStruct((S, D), F32), jax.ShapeDtypeStruct((S, D), BF16),
                   jax.ShapeDtypeStruct((8, D), F32), jax.ShapeDtypeStruct((8, D), F32)],
        compiler_params=_params(("arbitrary",)),
    )(h, g_row, target)


def _rope(t, cos, sa, sb):
    w = t.shape[-1]
    return t * cos + pltpu.roll(t, w - 8, 1) * sa + pltpu.roll(t, 8, 1) * sb


def _rope_t(g, cos, sa, sb):
    w = g.shape[-1]
    return g * cos + pltpu.roll(g * sa, 8, 1) + pltpu.roll(g * sb, w - 8, 1)


def _band_mask(n, qpk):
    qi = lax.broadcasted_iota(jnp.int32, (qpk * WINDOW, 2 * WINDOW), 0) & (WINDOW - 1)
    kj = lax.broadcasted_iota(jnp.int32, (qpk * WINDOW, 2 * WINDOW), 1)
    rel = qi + WINDOW - kj
    ok = (rel >= 0) & (rel < WINDOW)
    return ok & ((kj >= WINDOW) | (n > 0))


def _stack_heads(x, g, qpk):
    return jnp.concatenate([x[:, (g * qpk + hh) * HEAD_DIM:(g * qpk + hh + 1) * HEAD_DIM] for hh in range(qpk)], axis=0)


def _stack_cols(row, g, qpk):
    return jnp.concatenate([row[:, g * qpk + hh:g * qpk + hh + 1] for hh in range(qpk)], axis=0)


def _attn_specs(dm, nb):
    A, KV = dm["A"], dm["KV"]
    kb, vb = dm["OFF_K"] // KV, dm["OFF_V"] // KV
    cur = lambda n: jnp.minimum(n, nb - 1)
    prev = lambda n: jnp.maximum(jnp.minimum(n, nb - 1) - 1, 0)
    proj_specs = [
        pl.BlockSpec((WINDOW, A), lambda n: (cur(n), 0)),
        pl.BlockSpec((WINDOW, KV), lambda n: (prev(n), kb)),
        pl.BlockSpec((WINDOW, KV), lambda n: (cur(n), kb)),
        pl.BlockSpec((WINDOW, KV), lambda n: (prev(n), vb)),
        pl.BlockSpec((WINDOW, KV), lambda n: (cur(n), vb)),
    ]
    trig_cur = [pl.BlockSpec((WINDOW, LANES), lambda n: (cur(n), 0)) for _ in range(3)]
    trig_prev = [pl.BlockSpec((WINDOW, LANES), lambda n: (prev(n), 0)) for _ in range(3)]
    return proj_specs, trig_cur, trig_prev, cur, prev


def _attn_fwd(proj, trig, sink_row, dm):
    S = proj.shape[0]
    A, KV, NQ = dm["A"], dm["KV"], dm["NQ"]
    qpk = NQ // N_KV_HEADS
    nb = S // WINDOW
    scale = HEAD_DIM ** -0.5
    proj_specs, trig_cur, trig_prev, cur, _ = _attn_specs(dm, nb)

    def body(q_ref, kp_ref, kc_ref, vp_ref, vc_ref, cc_ref, sac_ref, sbc_ref, cp_ref, sap_ref, sbp_ref,
             sink_ref, y_ref, lse_ref):
        n = pl.program_id(0)
        tq = lambda r: jnp.tile(r[...], (1, A // LANES))
        tk = lambda rp, rc: jnp.tile(jnp.concatenate([rp[...], rc[...]], axis=0), (1, KV // LANES))
        qr = _rope(q_ref[...].astype(F32), tq(cc_ref), tq(sac_ref), tq(sbc_ref)).astype(BF16)
        kband = jnp.concatenate([kp_ref[...], kc_ref[...]], axis=0).astype(F32)
        kr = _rope(kband, tk(cp_ref, cc_ref), tk(sap_ref, sac_ref), tk(sbp_ref, sbc_ref)).astype(BF16)
        vband = jnp.concatenate([vp_ref[...], vc_ref[...]], axis=0)
        mask = _band_mask(n, qpk)
        lane = lax.broadcasted_iota(jnp.int32, (WINDOW, LANES), 1)
        lse_all = jnp.zeros((WINDOW, LANES), F32)
        sink_rows = jnp.broadcast_to(sink_ref[0:1, :], (WINDOW, LANES))
        for g in range(N_KV_HEADS):
            k_g = kr[:, g * HEAD_DIM:(g + 1) * HEAD_DIM]
            v_g = vband[:, g * HEAD_DIM:(g + 1) * HEAD_DIM]
            q_g = _stack_heads(qr, g, qpk)
            sink = _stack_cols(sink_rows, g, qpk)
            s = jnp.where(mask, _dot(q_g, k_g, NT) * scale, NEG)
            m = jnp.maximum(jnp.max(s, axis=-1, keepdims=True), sink)
            p = jnp.exp(s - m)
            den = jnp.sum(p, axis=-1, keepdims=True) + jnp.exp(sink - m)
            o = _dot(p.astype(BF16), v_g, NN) * (1.0 / den)
            lse_g = m + jnp.log(den)
            for hh in range(qpk):
                h = g * qpk + hh
                rows = slice(hh * WINDOW, (hh + 1) * WINDOW)
                y_ref[:, h * HEAD_DIM:(h + 1) * HEAD_DIM] = o[rows].astype(BF16)
                lse_all = jnp.where(lane == h, lse_g[rows], lse_all)
        lse_ref[...] = lse_all

    return pl.pallas_call(
        body, name="attn_fwd", grid=(nb,),
        in_specs=[*proj_specs, *trig_cur, *trig_prev, pl.BlockSpec((8, LANES), lambda n: (0, 0))],
        out_specs=[pl.BlockSpec((WINDOW, A), lambda n: (n, 0)), pl.BlockSpec((WINDOW, LANES), lambda n: (n, 0))],
        out_shape=[jax.ShapeDtypeStruct((S, A), BF16), jax.ShapeDtypeStruct((S, LANES), F32)],
        compiler_params=_params(("parallel",)),
    )(proj, proj, proj, proj, proj, *trig, *trig, sink_row)


def _attn_bwd(proj, trig, sink_row, y, lse, dy, dm):
    S = proj.shape[0]
    A, KV, NQ = dm["A"], dm["KV"], dm["NQ"]
    qpk = NQ // N_KV_HEADS
    nb = S // WINDOW
    scale = HEAD_DIM ** -0.5
    proj_specs, trig_cur, trig_prev, cur, prev = _attn_specs(dm, nb)

    def body(q_ref, kp_ref, kc_ref, vp_ref, vc_ref, cc_ref, sac_ref, sbc_ref, cp_ref, sap_ref, sbp_ref,
             sink_ref, y_ref, lse_ref, dy_ref, dq_ref, dk_ref, dv_ref, dsink_ref,
             ck_ref, cv_ref, bk_ref, bv_ref, dqr_ref):
        n = pl.program_id(0)

        @pl.when(n == 0)
        def _():
            dsink_ref[...] = jnp.zeros_like(dsink_ref)
            ck_ref[...] = jnp.zeros_like(ck_ref)
            cv_ref[...] = jnp.zeros_like(cv_ref)

        @pl.when(n < nb)
        def _():
            tq = lambda r: jnp.tile(r[...], (1, A // LANES))
            tk = lambda rp, rc: jnp.tile(jnp.concatenate([rp[...], rc[...]], axis=0), (1, KV // LANES))
            cq, saq, sbq = tq(cc_ref), tq(sac_ref), tq(sbc_ref)
            ck, sak, sbk = tk(cp_ref, cc_ref), tk(sap_ref, sac_ref), tk(sbp_ref, sbc_ref)
            qr = _rope(q_ref[...].astype(F32), cq, saq, sbq).astype(BF16)
            kband = jnp.concatenate([kp_ref[...], kc_ref[...]], axis=0).astype(F32)
            kr = _rope(kband, ck, sak, sbk).astype(BF16)
            vband = jnp.concatenate([vp_ref[...], vc_ref[...]], axis=0)
            mask = _band_mask(n, qpk)
            lane = lax.broadcasted_iota(jnp.int32, (1, LANES), 1)
            lse_all = lse_ref[...]
            sink_rows = jnp.broadcast_to(sink_ref[0:1, :], (WINDOW, LANES))
            dy_all = dy_ref[...]
            y_all = y_ref[...]
            dsink = jnp.zeros((1, LANES), F32)
            for g in range(N_KV_HEADS):
                k_g = kr[:, g * HEAD_DIM:(g + 1) * HEAD_DIM]
                v_g = vband[:, g * HEAD_DIM:(g + 1) * HEAD_DIM]
                q_g = _stack_heads(qr, g, qpk)
                dy_g = _stack_heads(dy_all, g, qpk)
                y_g = _stack_heads(y_all, g, qpk)
                lse_g = _stack_cols(lse_all, g, qpk)
                s = jnp.where(mask, _dot(q_g, k_g, NT) * scale, NEG)
                p = jnp.exp(s - lse_g)
                dp = _dot(dy_g, v_g, NT)
                delta = jnp.sum(dy_g.astype(F32) * y_g.astype(F32), axis=-1, keepdims=True)
                ds = (p * (dp - delta) * scale).astype(BF16)
                dq_g = _dot(ds, k_g, NN)
                bk_ref[:, g * HEAD_DIM:(g + 1) * HEAD_DIM] = _dot(ds, q_g, TN)
                bv_ref[:, g * HEAD_DIM:(g + 1) * HEAD_DIM] = _dot(p.astype(BF16), dy_g, TN)
                sink_d = jnp.exp(_stack_cols(sink_rows, g, qpk) - lse_g) * delta
                for hh in range(qpk):
                    h = g * qpk + hh
                    rows = slice(hh * WINDOW, (hh + 1) * WINDOW)
                    dqr_ref[:, h * HEAD_DIM:(h + 1) * HEAD_DIM] = dq_g[rows]
                    dsink = dsink + jnp.where(lane == h, -jnp.sum(sink_d[rows], axis=0, keepdims=True), 0.0)
            dsink_ref[0:1, :] += dsink
            dq_ref[...] = _rope_t(dqr_ref[...], cq, saq, sbq).astype(BF16)
            dkb = _rope_t(bk_ref[...], ck, sak, sbk)
            dvb = bv_ref[...]
            dk_ref[...] = (ck_ref[...] + dkb[:WINDOW]).astype(BF16)
            dv_ref[...] = (cv_ref[...] + dvb[:WINDOW]).astype(BF16)
            ck_ref[...] = dkb[WINDOW:]
            cv_ref[...] = dvb[WINDOW:]

        @pl.when(n == nb)
        def _():
            dk_ref[...] = ck_ref[...].astype(BF16)
            dv_ref[...] = cv_ref[...].astype(BF16)

    row = lambda w: pl.BlockSpec((WINDOW, w), lambda n: (cur(n), 0))
    done = lambda w: pl.BlockSpec((WINDOW, w), lambda n: (jnp.maximum(n - 1, 0), 0))
    return pl.pallas_call(
        body, name="attn_bwd", grid=(nb + 1,),
        in_specs=[*proj_specs, *trig_cur, *trig_prev, pl.BlockSpec((8, LANES), lambda n: (0, 0)),
                  row(A), row(LANES), row(A)],
        out_specs=[row(A), done(KV), done(KV), pl.BlockSpec((8, LANES), lambda n: (0, 0))],
        out_shape=[jax.ShapeDtypeStruct((S, A), BF16), jax.ShapeDtypeStruct((S, KV), BF16),
                   jax.ShapeDtypeStruct((S, KV), BF16), jax.ShapeDtypeStruct((8, LANES), F32)],
        scratch_shapes=[pltpu.VMEM((WINDOW, KV), F32), pltpu.VMEM((WINDOW, KV), F32),
                        pltpu.VMEM((2 * WINDOW, KV), F32), pltpu.VMEM((2 * WINDOW, KV), F32),
                        pltpu.VMEM((WINDOW, A), F32)],
        compiler_params=_params(("arbitrary",)),
    )(proj, proj, proj, proj, proj, *trig, *trig, sink_row, y, lse, dy)


def _sgu_layout(dm, S):
    G = dm["G"]
    pw = math.gcd(dm["OFF_Z"], G)
    npc = G // pw
    tm = _pick(S, (256, 128))
    u_specs = [pl.BlockSpec((tm, pw), lambda i, p=p: (i, dm["OFF_Z"] // pw + p)) for p in range(npc)]
    v_specs = [pl.BlockSpec((tm, pw), lambda i, p=p: (i, (dm["OFF_Z"] + G) // pw + p)) for p in range(npc)]
    return pw, npc, tm, u_specs, v_specs


def _sgu_norm(v_refs, lg_ref, lb_ref):
    v = jnp.concatenate([_gelu(r[...].astype(F32)) for r in v_refs], axis=1)
    mu = jnp.mean(v, axis=-1, keepdims=True)
    vc = v - mu
    rstd = lax.rsqrt(jnp.mean(vc * vc, axis=-1, keepdims=True) + EPS)
    xhat = vc * rstd
    return xhat, rstd, (xhat * lg_ref[...] + lb_ref[...]).astype(BF16)


def _sgu_fwd(proj, w_tril, b_t, ln_g_row, ln_b_row, dm):
    S = proj.shape[0]
    G, NG = dm["G"], dm["NG"]
    pw, npc, tm, u_specs, v_specs = _sgu_layout(dm, S)
    nch = tm // WINDOW

    def body(*refs):
        u_refs, v_refs = refs[:npc], refs[npc:2 * npc]
        w_ref, bt_ref, lg_ref, lb_ref, y_ref = refs[2 * npc:]
        _, _, vn = _sgu_norm(v_refs, lg_ref, lb_ref)
        u = jnp.concatenate([_gelu(r[...].astype(F32)) for r in u_refs], axis=1)
        for c in range(nch):
            rows = slice(c * WINDOW, (c + 1) * WINDOW)
            for g in range(NG):
                cols = slice(g * LANES, (g + 1) * LANES)
                sv = _dot(w_ref[g], vn[rows, cols], NN) + bt_ref[:, g:g + 1]
                y_ref[rows, cols] = (u[rows, cols] * sv).astype(BF16)

    return pl.pallas_call(
        body, name="sgu_fwd", grid=(S // tm,),
        in_specs=[*u_specs, *v_specs,
                  pl.BlockSpec((NG, WINDOW, WINDOW), lambda i: (0, 0, 0)),
                  pl.BlockSpec((WINDOW, LANES), lambda i: (0, 0)),
                  pl.BlockSpec((1, G), lambda i: (0, 0)), pl.BlockSpec((1, G), lambda i: (0, 0))],
        out_specs=pl.BlockSpec((tm, G), lambda i: (i, 0)),
        out_shape=jax.ShapeDtypeStruct((S, G), BF16),
        compiler_params=_params(("parallel",)),
    )(*([proj] * (2 * npc)), w_tril, b_t, ln_g_row, ln_b_row)


def _sgu_bwd(proj, w_tril, b_t, ln_g_row, ln_b_row, dy, dm):
    S = proj.shape[0]
    G, NG = dm["G"], dm["NG"]
    pw, npc, tm, u_specs, v_specs = _sgu_layout(dm, S)
    nch = tm // WINDOW

    def body(*refs):
        u_refs, v_refs = refs[:npc], refs[npc:2 * npc]
        w_ref, bt_ref, lg_ref, lb_ref, dy_ref, dz_ref, dw_ref, dbt_ref, dlg_ref, dlb_ref, dvn_ref = refs[2 * npc:]
        i = pl.program_id(0)

        @pl.when(i == 0)
        def _():
            dw_ref[...] = jnp.zeros_like(dw_ref)
            dbt_ref[...] = jnp.zeros_like(dbt_ref)
            dlg_ref[...] = jnp.zeros_like(dlg_ref)
            dlb_ref[...] = jnp.zeros_like(dlb_ref)

        xhat, rstd, vn = _sgu_norm(v_refs, lg_ref, lb_ref)
        u_pre = jnp.concatenate([r[...].astype(F32) for r in u_refs], axis=1)
        u = _gelu(u_pre)
        dy = dy_ref[...].astype(F32)
        lane = lax.broadcasted_iota(jnp.int32, (WINDOW, LANES), 1)
        tri = lax.broadcasted_iota(jnp.int32, (WINDOW, WINDOW), 0) >= lax.broadcasted_iota(jnp.int32, (WINDOW, WINDOW), 1)
        dbt = jnp.zeros((WINDOW, LANES), F32)
        for c in range(nch):
            rows = slice(c * WINDOW, (c + 1) * WINDOW)
            for g in range(NG):
                cols = slice(g * LANES, (g + 1) * LANES)
                vn_cg = vn[rows, cols]
                sv = _dot(w_ref[g], vn_cg, NN) + bt_ref[:, g:g + 1]
                dy_cg = dy[rows, cols]
                dsv = dy_cg * u[rows, cols]
                dsv_b = dsv.astype(BF16)
                dz_ref[rows, cols] = (dy_cg * sv * _gelu_grad(u_pre[rows, cols])).astype(BF16)
                dvn_ref[rows, cols] = _dot(w_ref[g], dsv_b, TN)
                dw_ref[g] += jnp.where(tri, _dot(dsv_b, vn_cg, NT), 0.0)
                dbt = dbt + jnp.where(lane == g, jnp.sum(dsv, axis=-1, keepdims=True), 0.0)
        dbt_ref[...] += dbt
        dvn = dvn_ref[...]
        dlg_ref[0:1, :] += jnp.sum(dvn * xhat, axis=0, keepdims=True)
        dlb_ref[0:1, :] += jnp.sum(dvn, axis=0, keepdims=True)
        dxh = dvn * lg_ref[...]
        dv = rstd * (dxh - jnp.mean(dxh, axis=-1, keepdims=True) - xhat * jnp.mean(dxh * xhat, axis=-1, keepdims=True))
        v_pre = jnp.concatenate([r[...].astype(F32) for r in v_refs], axis=1)
        dz_ref[:, G:] = (dv * _gelu_grad(v_pre)).astype(BF16)

    return pl.pallas_call(
        body, name="sgu_bwd", grid=(S // tm,),
        in_specs=[*u_specs, *v_specs,
                  pl.BlockSpec((NG, WINDOW, WINDOW), lambda i: (0, 0, 0)),
                  pl.BlockSpec((WINDOW, LANES), lambda i: (0, 0)),
                  pl.BlockSpec((1, G), lambda i: (0, 0)), pl.BlockSpec((1, G), lambda i: (0, 0)),
                  pl.BlockSpec((tm, G), lambda i: (i, 0))],
        out_specs=[pl.BlockSpec((tm, 2 * G), lambda i: (i, 0)),
                   pl.BlockSpec((NG, WINDOW, WINDOW), lambda i: (0, 0, 0)),
                   pl.BlockSpec((WINDOW, LANES), lambda i: (0, 0)),
                   pl.BlockSpec((8, G), lambda i: (0, 0)), pl.BlockSpec((8, G), lambda i: (0, 0))],
        out_shape=[jax.ShapeDtypeStruct((S, 2 * G), BF16), jax.ShapeDtypeStruct((NG, WINDOW, WINDOW), F32),
                   jax.ShapeDtypeStruct((WINDOW, LANES), F32), jax.ShapeDtypeStruct((8, G), F32),
                   jax.ShapeDtypeStruct((8, G), F32)],
        scratch_shapes=[pltpu.VMEM((tm, G), F32)],
        compiler_params=_params(("arbitrary",)),
    )(*([proj] * (2 * npc)), w_tril, b_t, ln_g_row, ln_b_row, dy)


def _result(outs, n_main, carry):
    main = outs[0] if n_main == 1 else tuple(outs[:n_main])
    return (main, list(outs[n_main:])) if carry else main


def _in_proj(xn, w_in_g, b_row, dm, carry=None):
    S, D = xn.shape
    IN = dm["IN"]
    cw = IN // N_CHIPS
    tm = _pick(S, (512, 256, 128))
    tn = _pick(cw, (1920, 640, 512, 256, 128))
    nbc = cw // tn

    def ep(parts, e_refs, o_refs):
        o_refs[0][...] = (parts[0] + e_refs[0][...]).astype(BF16)

    return _result(_matmul(
        "in_proj", xn, [w_in_g], dims=NN, grid=(S // tm, IN // tn, 1),
        lhs_spec=pl.BlockSpec((tm, D), lambda i, j, k: (i, 0)),
        rhs_specs=[pl.BlockSpec((None, D, tn), lambda i, j, k: (j // nbc, 0, j % nbc))],
        acc_shape=(tm, tn), extra=[b_row], extra_specs=[pl.BlockSpec((1, tn), lambda i, j, k: (0, j))],
        out_shape=[jax.ShapeDtypeStruct((S, IN), BF16)],
        out_specs=[pl.BlockSpec((tm, tn), lambda i, j, k: (i, j))], epilogue=ep, carry=carry, cols_outer=True), 1, carry)


def _branch_attn(y_attn, w_ab_g, dm):
    S, A = y_attn.shape
    D = dm["D"]
    cw = D // N_CHIPS
    tm = _pick(S, (1024, 512, 256, 128))
    return _matmul(
        "branch_attn", y_attn, [w_ab_g], dims=NN, grid=(S // tm, N_CHIPS, 1),
        lhs_spec=pl.BlockSpec((tm, A), lambda i, j, k: (i, 0)),
        rhs_specs=[pl.BlockSpec((None, A, cw), lambda i, j, k: (j, 0, 0))],
        acc_shape=(tm, cw), out_shape=[jax.ShapeDtypeStruct((S, D), BF16)],
        out_specs=[pl.BlockSpec((tm, cw), lambda i, j, k: (i, j))], epilogue=_store_epilogue(BF16))[0]


def _branch_sgu_merge(y_sgu, w_sb_g, a_attn, proj, dm):
    S, G = y_sgu.shape
    D, OFF_G = dm["D"], dm["OFF_G"]
    cw = D // N_CHIPS
    tm = _pick(S, (1024, 512, 256, 128))

    def ep(parts, e_refs, o_refs):
        a_sgu = parts[0].astype(BF16)
        ga = _sigmoid(e_refs[1][...].astype(F32))
        gs = _sigmoid(e_refs[2][...].astype(F32))
        o_refs[0][...] = a_sgu
        o_refs[1][...] = (ga * e_refs[0][...].astype(F32) + gs * a_sgu.astype(F32)).astype(BF16)

    blk = pl.BlockSpec((tm, cw), lambda i, j, k: (i, j))
    return _matmul(
        "branch_sgu_merge", y_sgu, [w_sb_g], dims=NN, grid=(S // tm, N_CHIPS, 1),
        lhs_spec=pl.BlockSpec((tm, G), lambda i, j, k: (i, 0)),
        rhs_specs=[pl.BlockSpec((None, G, cw), lambda i, j, k: (j, 0, 0))],
        acc_shape=(tm, cw), extra=[a_attn, proj, proj],
        extra_specs=[blk, pl.BlockSpec((tm, cw), lambda i, j, k: (i, OFF_G // cw + j)),
                     pl.BlockSpec((tm, cw), lambda i, j, k: (i, (OFF_G + D) // cw + j))],
        out_shape=[jax.ShapeDtypeStruct((S, D), BF16), jax.ShapeDtypeStruct((S, D), BF16)],
        out_specs=[blk, blk], epilogue=ep)


def _residual_matmul(name, a, w_g, h, carry=None):
    S, K = a.shape
    D = w_g.shape[1]
    tm = _pick(S, (1024, 512, 256, 128))
    tn = _pick(D, (512, 256, 128))

    def ep(parts, e_refs, o_refs):
        o_refs[0][...] = e_refs[0][...] + parts[0]

    blk = pl.BlockSpec((tm, tn), lambda i, j, k: (i, j))
    return _result(_matmul(
        name, a, [w_g], dims=NN, grid=(S // tm, D // tn, 1),
        lhs_spec=pl.BlockSpec((tm, K), lambda i, j, k: (i, 0)),
        rhs_specs=[pl.BlockSpec((K, tn), lambda i, j, k: (0, j))],
        acc_shape=(tm, tn), extra=[h], extra_specs=[blk],
        out_shape=[jax.ShapeDtypeStruct((S, D), F32)], out_specs=[blk], epilogue=ep, carry=carry), 1, carry)


def _gate_up(hn, w_gu_g, dm, carry=None):
    S, D = hn.shape
    Fd = dm["F"]
    cw = 2 * Fd // N_CHIPS
    tm = _pick(S, (512, 256, 128))
    tn = _pick(cw, (1408, 512, 384, 256, 128))
    nbc = cw // tn
    half = N_CHIPS // 2

    def ep(parts, e_refs, o_refs):
        gate, up = parts[0].astype(BF16), parts[1].astype(BF16)
        o_refs[0][0] = gate
        o_refs[0][1] = up
        g32 = gate.astype(F32)
        o_refs[1][...] = (g32 * _sigmoid(g32) * up.astype(F32)).astype(BF16)

    return _result(_matmul(
        "gate_up", hn, [w_gu_g, w_gu_g], dims=NN, grid=(S // tm, Fd // tn, 1),
        lhs_spec=pl.BlockSpec((tm, D), lambda i, j, k: (i, 0)),
        rhs_specs=[pl.BlockSpec((None, D, tn), lambda i, j, k: (j // nbc, 0, j % nbc)),
                   pl.BlockSpec((None, D, tn), lambda i, j, k: (half + j // nbc, 0, j % nbc))],
        acc_shape=(tm, tn),
        out_shape=[jax.ShapeDtypeStruct((2, S, Fd), BF16), jax.ShapeDtypeStruct((S, Fd), BF16)],
        out_specs=[pl.BlockSpec((2, tm, tn), lambda i, j, k: (0, i, j)), pl.BlockSpec((tm, tn), lambda i, j, k: (i, j))],
        epilogue=ep, carry=carry, cols_outer=True), 2, carry)


def _down_bwd(dh_b, w_down_g, gu, dm, carry=None):
    S, D = dh_b.shape
    Fd = dm["F"]
    tm = _pick(S, (1024, 512, 256, 128))
    tn = _pick(Fd, (512, 256, 128))

    def ep(parts, e_refs, o_refs):
        gate = e_refs[0][0].astype(F32)
        up = e_refs[0][1].astype(F32)
        s = _sigmoid(gate)
        dact = parts[0]
        o_refs[0][0] = (dact * up * s * (1.0 + gate * (1.0 - s))).astype(BF16)
        o_refs[0][1] = (dact * gate * s).astype(BF16)

    blk = pl.BlockSpec((2, tm, tn), lambda i, j, k: (0, i, j))
    return _result(_matmul(
        "down_bwd", dh_b, [w_down_g], dims=NT, grid=(S // tm, Fd // tn, 1),
        lhs_spec=pl.BlockSpec((tm, D), lambda i, j, k: (i, 0)),
        rhs_specs=[pl.BlockSpec((tn, D), lambda i, j, k: (j, 0))],
        acc_shape=(tm, tn), extra=[gu], extra_specs=[blk],
        out_shape=[jax.ShapeDtypeStruct((2, S, Fd), BF16)], out_specs=[blk], epilogue=ep, carry=carry), 1, carry)


def _gate_up_bwd(dgu, w_gu_g, dm, carry=None):
    S = dgu.shape[1]
    D, Fd = dm["D"], dm["F"]
    cw = 2 * Fd // N_CHIPS
    half = N_CHIPS // 2
    tm = _pick(S, (1024, 512, 256, 128))
    tn = _pick(D, (1024, 512, 256, 128))
    return _result(_matmul(
        "gate_up_bwd", dgu, [w_gu_g], dims=NT, grid=(S // tm, D // tn, N_CHIPS),
        lhs_spec=pl.BlockSpec((None, tm, cw), lambda i, j, k: (k // half, i, k % half)),
        rhs_specs=[pl.BlockSpec((None, tn, cw), lambda i, j, k: (k, j, 0))],
        acc_shape=(tm, tn), out_shape=[jax.ShapeDtypeStruct((S, D), F32)],
        out_specs=[pl.BlockSpec((tm, tn), lambda i, j, k: (i, j))], epilogue=_store_epilogue(F32), carry=carry), 1, carry)


def _out_bwd(dh_b, w_out_g, proj, a_attn, a_sgu, dm, carry=None):
    S, D = dh_b.shape
    OFF_G = dm["OFF_G"]
    tm = _pick(S, (1024, 512, 256, 128))
    tn = D // N_CHIPS

    def ep(parts, e_refs, o_refs):
        dm_ = parts[0]
        ga = _sigmoid(e_refs[0][...].astype(F32))
        gs = _sigmoid(e_refs[1][...].astype(F32))
        o_refs[0][...] = (dm_ * ga).astype(BF16)
        o_refs[1][...] = (dm_ * gs).astype(BF16)
        o_refs[2][0] = (dm_ * e_refs[2][...].astype(F32) * ga * (1.0 - ga)).astype(BF16)
        o_refs[2][1] = (dm_ * e_refs[3][...].astype(F32) * gs * (1.0 - gs)).astype(BF16)

    blk = pl.BlockSpec((tm, tn), lambda i, j, k: (i, j))
    return _result(_matmul(
        "out_bwd", dh_b, [w_out_g], dims=NT, grid=(S // tm, D // tn, 1),
        lhs_spec=pl.BlockSpec((tm, D), lambda i, j, k: (i, 0)),
        rhs_specs=[pl.BlockSpec((tn, D), lambda i, j, k: (j, 0))],
        acc_shape=(tm, tn), extra=[proj, proj, a_attn, a_sgu],
        extra_specs=[pl.BlockSpec((tm, tn), lambda i, j, k: (i, OFF_G // tn + j)),
                     pl.BlockSpec((tm, tn), lambda i, j, k: (i, (OFF_G + D) // tn + j)), blk, blk],
        out_shape=[jax.ShapeDtypeStruct((S, D), BF16), jax.ShapeDtypeStruct((S, D), BF16),
                   jax.ShapeDtypeStruct((2, S, D), BF16)],
        out_specs=[blk, blk, pl.BlockSpec((2, tm, tn), lambda i, j, k: (0, i, j))], epilogue=ep, carry=carry), 3, carry)


def _colsharded_bwd(name, dy, w_g, out_dtype, carry=None):
    S = dy.shape[0]
    _, K, cw = w_g.shape
    tm = _pick(S, (1024, 512, 256, 128))
    tn = _pick(K, (1024, 512, 256, 128))
    return _result(_matmul(
        name, dy, [w_g], dims=NT, grid=(S // tm, K // tn, N_CHIPS),
        lhs_spec=pl.BlockSpec((tm, cw), lambda i, j, k: (i, k)),
        rhs_specs=[pl.BlockSpec((None, tn, cw), lambda i, j, k: (k, j, 0))],
        acc_shape=(tm, tn), out_shape=[jax.ShapeDtypeStruct((S, K), out_dtype)],
        out_specs=[pl.BlockSpec((tm, tn), lambda i, j, k: (i, j))], epilogue=_store_epilogue(out_dtype),
        carry=carry), 1, carry)


def _wgrad_cols(name, x, dy, carry=None, colsum=False):
    S, R = x.shape
    C = dy.shape[1]
    cw = C // N_CHIPS
    tm = _pick(R, (1024, 512, 256, 128))
    tk = _pick(S, (1024, 512, 256, 128) if cw >= 1024 else (2048, 1024, 512, 256, 128))

    def ep(parts, e_refs, o_refs):
        for o, p in zip(o_refs, parts):
            o[...] = p

    out_shape = [jax.ShapeDtypeStruct((N_CHIPS, R, cw), F32)]
    out_specs = [pl.BlockSpec((None, tm, cw), lambda i, j, k: (j, i, 0))]
    if colsum:
        out_shape.append(jax.ShapeDtypeStruct((R // tm, N_CHIPS, 8, cw), F32))
        out_specs.append(pl.BlockSpec((None, None, 8, cw), lambda i, j, k: (i, j, 0, 0)))
    return _result(_matmul(
        name, x, [dy], dims=TN, grid=(R // tm, N_CHIPS, S // tk),
        lhs_spec=pl.BlockSpec((tk, tm), lambda i, j, k: (k, i)),
        rhs_specs=[pl.BlockSpec((tk, cw), lambda i, j, k: (k, j))],
        acc_shape=(tm, cw), out_shape=out_shape, out_specs=out_specs, epilogue=ep,
        carry=carry, rhs_colsum=colsum), len(out_shape), carry)


def _wgrad_gate_up(hn, dgu, dm, carry=None):
    S, D = hn.shape
    Fd = dm["F"]
    cw = 2 * Fd // N_CHIPS
    half = N_CHIPS // 2
    tm = _pick(D, (1024, 512, 256, 128))
    tk = _pick(S, (1024, 512, 256, 128))
    tn = _pick(cw, (1408, 512, 384, 256, 128))
    nbc = cw // tn
    return _result(_matmul(
        "wgrad_gate_up", hn, [dgu], dims=TN, grid=(D // tm, 2 * Fd // tn, S // tk),
        lhs_spec=pl.BlockSpec((tk, tm), lambda i, j, k: (k, i)),
        rhs_specs=[pl.BlockSpec((None, tk, tn), lambda i, j, k: (j // (half * nbc), k, j % (half * nbc)))],
        acc_shape=(tm, tn), out_shape=[jax.ShapeDtypeStruct((N_CHIPS, D, cw), F32)],
        out_specs=[pl.BlockSpec((None, tm, tn), lambda i, j, k: (j // nbc, i, j % nbc))], epilogue=_store_epilogue(F32),
        carry=carry), 1, carry)


def _wgrad_rows(name, x, dy):
    S, R = x.shape
    C = dy.shape[1]
    rw = R // N_CHIPS
    tn = _pick(C, (1024, 512, 256, 128))
    tk = _pick(S, (1024, 512, 256, 128))
    return _matmul(
        name, x, [dy], dims=TN, grid=(N_CHIPS, C // tn, S // tk),
        lhs_spec=pl.BlockSpec((tk, rw), lambda i, j, k: (k, i)),
        rhs_specs=[pl.BlockSpec((tk, tn), lambda i, j, k: (k, j))],
        acc_shape=(rw, tn), out_shape=[jax.ShapeDtypeStruct((N_CHIPS, rw, C), F32)],
        out_specs=[pl.BlockSpec((None, rw, tn), lambda i, j, k: (i, 0, j))], epilogue=_store_epilogue(F32))[0]


def _place():
    x, y, c = lax.axis_index("x"), lax.axis_index("y"), lax.axis_index("c")
    others = [(1 - x, y), (x, 1 - y), (1 - x, 1 - y)]
    return x, y, c, others


def _chip_index(chip):
    return 2 * chip[0] + chip[1]


def _gather_weights(bufs):
    n = len(bufs)

    def copies(src, out, send_sems, recv_sems):
        x, y, c, others = _place()

        def half(ref, chip_idx, hc):
            r2 = ref.shape[1] // 2
            return ref.at[chip_idx, pl.ds(hc * r2, r2), :]

        def copy(t, k, chip, hc, to):
            return pltpu.make_async_remote_copy(
                src_ref=half(src[t], _chip_index(chip), hc), dst_ref=half(out[t], _chip_index(chip), hc),
                send_sem=send_sems.at[6 * t + k], recv_sem=recv_sems.at[6 * t + k],
                device_id=to, device_id_type=MESH)

        me, sibling = (x, y, c), (x, y, 1 - c)
        pairs = [(t, j, chip) for t in range(n) for j, chip in enumerate(others)]
        sent = [copy(t, j, (x, y), c, (*chip, c)) for t, j, chip in pairs]
        landed = [copy(t, j, chip, c, me) for t, j, chip in pairs]
        passed = [copy(t, 3 + j, chip, c, sibling) for t, j, chip in pairs]
        handed = [copy(t, 3 + j, chip, 1 - c, me) for t, j, chip in pairs]
        return sent, landed, passed, handed

    def start(src, out, send_sems, recv_sems):
        for cp in copies(src, out, send_sems, recv_sems)[0]:
            cp.start()

    def finish(src, out, send_sems, recv_sems):
        sent, landed, passed, handed = copies(src, out, send_sems, recv_sems)
        for arrival, forward in zip(landed, passed):
            arrival.wait_recv()
            forward.start()
        for cp in handed:
            cp.wait_recv()
        for cp in sent + passed:
            cp.wait_send()

    return _Comm("gather_weights", bufs, [jax.ShapeDtypeStruct(b.shape, BF16) for b in bufs],
                 {t: t for t in range(n)}, 6 * n, start, finish)


def _sibling_exchange(grads):
    n = len(grads)
    shapes = [g.shape for g in grads]

    def copies(src, land, send_sems, recv_sems):
        x, y, c, _ = _place()
        res = []
        for t in range(n):
            r2 = shapes[t][1] // 2
            res.append(pltpu.make_async_remote_copy(
                src_ref=src[t].at[:, pl.ds((1 - c) * r2, r2), :], dst_ref=land[t],
                send_sem=send_sems.at[t], recv_sem=recv_sems.at[t], device_id=(x, y, 1 - c), device_id_type=MESH))
        return res

    def start(*refs):
        for cp in copies(*refs):
            cp.start()

    def finish(*refs):
        remote = copies(*refs)
        for cp in remote:
            cp.wait_recv()
        for cp in remote:
            cp.wait_send()

    return _Comm("sibling_exchange", grads, [jax.ShapeDtypeStruct((s[0], s[1] // 2, s[2]), F32) for s in shapes],
                 {}, n, start, finish)


def _chip_exchange(sends):
    n = len(sends)
    shapes = [s.shape for s in sends]

    def copies(snd, got, send_sems, recv_sems):
        x, y, c, others = _place()
        return [pltpu.make_async_remote_copy(
            src_ref=snd[t].at[_chip_index(chip)], dst_ref=got[t].at[j],
            send_sem=send_sems.at[3 * t + j], recv_sem=recv_sems.at[3 * t + j],
            device_id=(*chip, c), device_id_type=MESH) for t in range(n) for j, chip in enumerate(others)]

    def start(*refs):
        for cp in copies(*refs):
            cp.start()

    def finish(*refs):
        remote = copies(*refs)
        for cp in remote:
            cp.wait_recv()
        for cp in remote:
            cp.wait_send()

    return _Comm("chip_exchange", sends, [jax.ShapeDtypeStruct((3, s[1], s[2]), BF16) for s in shapes],
                 {}, 3 * n, start, finish)


def _sibling_share(fulls):
    n = len(fulls)
    shapes = [f.shape for f in fulls]

    def copies(src, out, send_sems, recv_sems, mine):
        x, y, c, _ = _place()
        hc = c if mine else 1 - c
        res = []
        for t in range(n):
            r2 = shapes[t][0] // 2
            res.append(pltpu.make_async_remote_copy(
                src_ref=src[t].at[pl.ds(hc * r2, r2), :], dst_ref=out[t].at[pl.ds(hc * r2, r2), :],
                send_sem=send_sems.at[t], recv_sem=recv_sems.at[t], device_id=(x, y, 1 - c), device_id_type=MESH))
        return res

    def start(*refs):
        for cp in copies(*refs, mine=True):
            cp.start()

    def finish(*refs):
        for cp in copies(*refs, mine=False):
            cp.wait_recv()
        for cp in copies(*refs, mine=True):
            cp.wait_send()

    return _Comm("sibling_share", fulls, [jax.ShapeDtypeStruct(s, F32) for s in shapes],
                 {t: t for t in range(n)}, n, start, finish)


def _gather_all(v):
    R, C = v.shape

    def body(v_ref, out_ref, send_sems, recv_sems, local_sem):
        x, y, c, others = _place()
        me, sibling = (x, y, c), (x, y, 1 - c)

        def rows(px, py, pc):
            return out_ref.at[4 * px + 2 * py + pc]

        def copy(k, block, to, src=None):
            return pltpu.make_async_remote_copy(
                src_ref=rows(*block) if src is None else src, dst_ref=rows(*block),
                send_sem=send_sems.at[k], recv_sem=recv_sems.at[k], device_id=to, device_id_type=MESH)

        mine = pltpu.make_async_copy(v_ref, rows(*me), local_sem)
        mine.start()
        first = [copy(0, me, sibling, src=v_ref)]
        first += [copy(1 + j, me, (*chip, c), src=v_ref) for j, chip in enumerate(others)]
        for cp in first:
            cp.start()
        passed = [copy(4 + j, (*chip, c), sibling) for j, chip in enumerate(others)]
        for j, chip in enumerate(others):
            copy(1 + j, (*chip, c), me).wait_recv()
            passed[j].start()
        copy(0, sibling, me).wait_recv()
        for j, chip in enumerate(others):
            copy(4 + j, (*chip, 1 - c), me).wait_recv()
        for cp in first + passed:
            cp.wait_send()
        mine.wait()

    return pl.pallas_call(
        body, name="gather_all", in_specs=[ANY], out_specs=ANY,
        out_shape=jax.ShapeDtypeStruct((8, R, C), F32),
        scratch_shapes=[pltpu.SemaphoreType.DMA((7,)), pltpu.SemaphoreType.DMA((7,)), pltpu.SemaphoreType.DMA],
    )(v)


def _my_chip():
    return 2 * lax.axis_index("x") + lax.axis_index("y")


def _my_core():
    return lax.axis_index("c")


def _pair_sum(grad, land):
    K, R2, C = land.shape
    tm = _row_tile(R2, C)
    nrb = R2 // tm

    def body(a_ref, b_ref, s_ref, sb_ref):
        s = a_ref[...] + b_ref[...]
        s_ref[...] = s
        sb_ref[...] = s.astype(BF16)

    blk = pl.BlockSpec((None, tm, C), lambda k, r: (k, r, 0))
    return pl.pallas_call(
        body, name="pair_sum", grid=(K, nrb),
        in_specs=[pl.BlockSpec((None, tm, C), lambda k, r: (k, _my_core() * nrb + r, 0)), blk],
        out_specs=[blk, blk],
        out_shape=[jax.ShapeDtypeStruct((K, R2, C), F32), jax.ShapeDtypeStruct((K, R2, C), BF16)],
        compiler_params=_params(("parallel", "parallel")),
    )(grad, land)


def _chip_sum(sums, got):
    _, R2, C = sums.shape
    tm = _row_tile(R2, C)
    nrb = R2 // tm

    def body(o_ref, g_ref, s_ref):
        s_ref[...] = ((o_ref[...] + g_ref[0].astype(F32)) + g_ref[1].astype(F32)) + g_ref[2].astype(F32)

    return pl.pallas_call(
        body, name="chip_sum", grid=(nrb,),
        in_specs=[pl.BlockSpec((None, tm, C), lambda r: (_my_chip(), r, 0)),
                  pl.BlockSpec((3, tm, C), lambda r: (0, r, 0))],
        out_specs=pl.BlockSpec((tm, C), lambda r: (_my_core() * nrb + r, 0)),
        out_shape=jax.ShapeDtypeStruct((2 * R2, C), F32),
        compiler_params=_params(("parallel",)),
    )(sums, got)


def _adamw_math(w, g, m, v):
    m = ADAM_B1 * m + (1.0 - ADAM_B1) * g
    v = ADAM_B2 * v + (1.0 - ADAM_B2) * (g * g)
    m_hat = m / (1.0 - ADAM_B1 ** ADAM_STEP)
    v_hat = v / (1.0 - ADAM_B2 ** ADAM_STEP)
    delta = -ADAM_LR * (m_hat / (jnp.sqrt(v_hat) + ADAM_EPS) + ADAM_WD * w)
    return delta, m, v


def _adamw_stacked(grads, w, m, v):
    L, R, C = w.shape
    tm = _row_tile(R, C)
    nrb = R // tm

    def body(*refs):
        g_refs = refs[:L]
        w_ref, m_ref, v_ref, go_ref, d_ref, mo_ref, vo_ref = refs[L:]
        l = pl.program_id(0)
        for ll in range(L):
            @pl.when(l == ll)
            def _(ll=ll):
                g = g_refs[ll][...]
                delta, mn, vn = _adamw_math(w_ref[...], g, m_ref[...], v_ref[...])
                go_ref[...] = g
                d_ref[...] = delta
                mo_ref[...] = mn
                vo_ref[...] = vn

    stacked = pl.BlockSpec((None, tm, C), lambda l, r: (l, r, 0))
    g_specs = [pl.BlockSpec((tm, C), lambda l, r, ll=ll: (jnp.where(l == ll, r, 0), 0)) for ll in range(L)]
    shp = jax.ShapeDtypeStruct((L, R, C), F32)
    return pl.pallas_call(
        body, name="adamw", grid=(L, nrb),
        in_specs=[*g_specs, stacked, stacked, stacked], out_specs=[stacked] * 4, out_shape=[shp] * 4,
        compiler_params=_params(("arbitrary", "arbitrary")),
    )(*grads, w, m, v)


def _adamw_small(parts, w, m, v):
    _, R, C = parts.shape
    tm = _row_tile(R, 8 * C)

    def body(p_ref, w_ref, m_ref, v_ref, go_ref, d_ref, mo_ref, vo_ref):
        g = p_ref[0]
        for k in range(1, 8):
            g = g + p_ref[k]
        delta, mn, vn = _adamw_math(w_ref[...], g, m_ref[...], v_ref[...])
        go_ref[...] = g
        d_ref[...] = delta
        mo_ref[...] = mn
        vo_ref[...] = vn

    blk = pl.BlockSpec((tm, C), lambda i: (i, 0))
    shp = jax.ShapeDtypeStruct((R, C), F32)
    return pl.pallas_call(
        body, name="adamw_small", grid=(R // tm,),
        in_specs=[pl.BlockSpec((8, tm, C), lambda i: (0, i, 0)), blk, blk, blk],
        out_specs=[blk] * 4, out_shape=[shp] * 4,
        compiler_params=_params(("parallel",)),
    )(parts, w, m, v)


def _cast_place(w, layer):
    _, R, C = w.shape
    tm = _row_tile(R, C)

    def body(w_ref, o_ref):
        o_ref[...] = w_ref[...].astype(BF16)

    return pl.pallas_call(
        body, name="cast_place", grid=(R // tm,),
        in_specs=[pl.BlockSpec((None, tm, C), lambda r: (layer, r, 0))],
        out_specs=pl.BlockSpec((None, tm, C), lambda r: (_my_chip(), r, 0)),
        out_shape=jax.ShapeDtypeStruct((N_CHIPS, R, C), BF16),
        compiler_params=_params(("parallel",)),
    )(w)


def _trig_tables(positions):
    half = ROPE_DIM // 2
    inv_freq = ROPE_THETA ** (-jnp.arange(0, ROPE_DIM, 2, dtype=F32) / ROPE_DIM)
    ang = positions.astype(F32)[:, None] * inv_freq
    cos, sin = jnp.cos(ang), jnp.sin(ang)
    S = positions.shape[0]
    zeros = lambda w: jnp.zeros((S, w), F32)
    cos_h = jnp.concatenate([cos, cos, jnp.ones((S, HEAD_DIM - ROPE_DIM), F32)], axis=1)
    sa_h = jnp.concatenate([-sin, zeros(HEAD_DIM - half)], axis=1)
    sb_h = jnp.concatenate([zeros(half), sin, zeros(HEAD_DIM - ROPE_DIM)], axis=1)
    rep = LANES // HEAD_DIM
    return [jnp.tile(t, (1, rep)) for t in (cos_h, sa_h, sb_h)]


def _row(vec):
    return vec.reshape(1, -1)


def _lane_row(vec):
    return jnp.zeros((8, LANES), F32).at[0, :vec.shape[0]].set(vec)


def _pack(pieces, rows):
    flat = jnp.concatenate([p.reshape(-1).astype(F32) for p in pieces])
    return jnp.pad(flat, (0, rows * LANES - flat.shape[0])).reshape(rows, LANES)


def kernel(x, positions, norm1_g, w_in, b_in, sinks, sgu_ln_g, sgu_ln_b, sgu_w, sgu_b, w_attn_branch, w_sgu_branch, w_out, norm2_g, w_gate_up, w_down, final_g, loss_target, m_norm1_g, m_w_in, m_b_in, m_sinks, m_sgu_ln_g, m_sgu_ln_b, m_sgu_w, m_sgu_b, m_w_attn_branch, m_w_sgu_branch, m_w_out, m_norm2_g, m_w_gate_up, m_w_down, m_final_g, v_norm1_g, v_w_in, v_b_in, v_sinks, v_sgu_ln_g, v_sgu_ln_b, v_sgu_w, v_sgu_b, v_w_attn_branch, v_w_sgu_branch, v_w_out, v_norm2_g, v_w_gate_up, v_w_down, v_final_g):
    L = norm1_g.shape[0]
    S, D = x.shape[1], x.shape[2]
    NQ = sinks.shape[1]
    A = NQ * HEAD_DIM
    KV = N_KV_HEADS * HEAD_DIM
    G = sgu_ln_g.shape[1]
    NG = sgu_w.shape[1]
    IN = b_in.shape[1]
    Fd = w_down.shape[1] * N_CHIPS
    dm = dict(D=D, A=A, KV=KV, NQ=NQ, G=G, NG=NG, IN=IN, F=Fd,
              OFF_K=A, OFF_V=A + KV, OFF_Z=A + 2 * KV, OFF_G=A + 2 * KV + 2 * G)
    assert sgu_w.shape[2] == WINDOW and G == NG * LANES and IN == dm["OFF_G"] + 2 * D

    h = x[0]
    target = loss_target[0]
    trig = _trig_tables(positions[0])
    tril = jnp.tril(jnp.ones((WINDOW, WINDOW), bool))

    big = [w_in, w_attn_branch, w_sgu_branch, w_out, w_gate_up, w_down]
    big_m = [m_w_in, m_w_attn_branch, m_w_sgu_branch, m_w_out, m_w_gate_up, m_w_down]
    big_v = [v_w_in, v_w_attn_branch, v_w_sgu_branch, v_w_out, v_w_gate_up, v_w_down]

    placed = [[_cast_place(w, l) for w in big] for l in range(L)]
    IN_, AB, SB, OUT, GU, DOWN = range(len(big))
    gathered = [[None] * len(big) for _ in range(L)]
    gathered[0][IN_] = _gather_weights([placed[0][IN_]]).run()[0]

    def fetch(layer, idx):
        return _gather_weights([placed[layer][t] for t in idx]) if layer < L else None

    def fetched(layer, idx, res):
        if layer >= L:
            return res
        main, got = res
        for t, g in zip(idx, got):
            gathered[layer][t] = g
        return main

    def weights(l):
        flat = lambda w, rows: None if w is None else w.reshape(rows, D)
        w_in_g, w_ab_g, w_sb_g, w_out_g, w_gu_g, w_down_g = gathered[l]
        return (w_in_g, w_ab_g, w_sb_g, flat(w_out_g, D), w_gu_g, flat(w_down_g, Fd))

    def small(l):
        return dict(
            g1=_row(norm1_g[l]), b_in=_row(b_in[l]), sink=_lane_row(sinks[l]),
            ln_g=_row(sgu_ln_g[l]), ln_b=_row(sgu_ln_b[l]),
            w_tril=jnp.where(tril[None], sgu_w[l], 0.0).astype(BF16),
            b_t=jnp.zeros((WINDOW, LANES), F32).at[:, :NG].set(sgu_b[l].T),
            g2=_row(norm2_g[l]))

    saved = []
    for l in range(L):
        sp = small(l)
        xn = _rms_fwd(h, sp["g1"])
        now = [AB, SB, OUT, GU] if l == 0 else [DOWN]
        proj = fetched(l, now, _in_proj(xn, gathered[l][IN_], sp["b_in"], dm, carry=fetch(l, now)))
        w_in_g, w_ab_g, w_sb_g, w_out_g = weights(l)[:4]
        y_attn, lse = _attn_fwd(proj, trig, sp["sink"], dm)
        y_sgu = _sgu_fwd(proj, sp["w_tril"], sp["b_t"], sp["ln_g"], sp["ln_b"], dm)
        a_attn = _branch_attn(y_attn, w_ab_g, dm)
        a_sgu, merged = _branch_sgu_merge(y_sgu, w_sb_g, a_attn, proj, dm)
        if l == 0:
            h_mid = fetched(l, [DOWN], _residual_matmul("out_proj", merged, w_out_g, h, carry=fetch(l, [DOWN])))
        else:
            h_mid = _residual_matmul("out_proj", merged, w_out_g, h)
        w_gu_g, w_down_g = weights(l)[4:]
        hn = _rms_fwd(h_mid, sp["g2"])
        ahead = [IN_, AB, SB, OUT]
        gu, act = fetched(l + 1, ahead, _gate_up(hn, w_gu_g, dm, carry=fetch(l + 1, ahead)))
        h_out = fetched(l + 1, [GU], _residual_matmul("down_proj", act, w_down_g, h_mid, carry=fetch(l + 1, [GU])))
        saved.append(dict(h=h, xn=xn, proj=proj, y_attn=y_attn, lse=lse, y_sgu=y_sgu, a_attn=a_attn, a_sgu=a_sgu,
                          merged=merged, h_mid=h_mid, hn=hn, gu=gu, act=act))
        h = h_out

    dh, dh_b, d_final, loss_part = _loss_head(h, _row(final_g), target)

    small_grads = [None] * L
    reduced = [[None] * len(big) for _ in range(L)]
    early, late = [GU, DOWN], [IN_, AB, SB, OUT]

    def riding(has_carry, res):
        return res if has_carry else (res, None)

    def sums_of(grads, land):
        return zip(*[_pair_sum(g, d) for g, d in zip(grads, land)])

    def file_reduced(layer, idx, fulls):
        for t, f in zip(idx, fulls):
            reduced[layer][t] = f

    late_grads = None
    for l in reversed(range(L)):
        w_in_g, w_ab_g, w_sb_g, w_out_g, w_gu_g, w_down_g = weights(l)
        sp, sv = small(l), saved[l]
        have = late_grads is not None
        dgu, land = riding(have,_down_bwd(dh_b, w_down_g, sv["gu"], dm,
                                                    carry=_sibling_exchange(late_grads) if have else None))
        if have:
            sums, sends = sums_of(late_grads, land)
        g_down = _wgrad_rows("wgrad_down", sv["act"], dh_b)
        dhn, got = riding(have,_gate_up_bwd(dgu, w_gu_g, dm,
                                                     carry=_chip_exchange(list(sends)) if have else None))
        if have:
            fulls = [_chip_sum(s, g) for s, g in zip(sums, got)]
        g_gu, shared = riding(have,_wgrad_gate_up(sv["hn"], dgu, dm,
                                                           carry=_sibling_share(fulls) if have else None))
        if have:
            file_reduced(l + 1, late, shared)
        dh_mid, dh_mid_b, d_g2 = _rms_bwd(dhn, sv["h_mid"], sp["g2"], dh)
        (da_attn, da_sgu, dgate), land = _out_bwd(dh_mid_b, w_out_g, sv["proj"], sv["a_attn"], sv["a_sgu"], dm,
                                                   carry=_sibling_exchange([g_gu, g_down]))
        sums, sends = sums_of([g_gu, g_down], land)
        g_out = _wgrad_rows("wgrad_out", sv["merged"], dh_mid_b)
        dy_attn = _colsharded_bwd("branch_attn_bwd", da_attn, w_ab_g, BF16)
        dy_sgu = _colsharded_bwd("branch_sgu_bwd", da_sgu, w_sb_g, BF16)
        g_ab = _wgrad_cols("wgrad_attn_branch", sv["y_attn"], da_attn)
        g_sb = _wgrad_cols("wgrad_sgu_branch", sv["y_sgu"], da_sgu)
        dq, dk, dv, d_sink = _attn_bwd(sv["proj"], trig, sp["sink"], sv["y_attn"], sv["lse"], dy_attn, dm)
        dz, d_sgu_w, d_bt, d_lng, d_lnb = _sgu_bwd(sv["proj"], sp["w_tril"], sp["b_t"], sp["ln_g"], sp["ln_b"], dy_sgu, dm)
        dproj = jnp.concatenate([dq, dk, dv, dz, dgate[0], dgate[1]], axis=1)
        dxn, got = _colsharded_bwd("in_proj_bwd", dproj, w_in_g, F32, carry=_chip_exchange(list(sends)))
        fulls = [_chip_sum(s, g) for s, g in zip(sums, got)]
        (g_in, d_bin), shared = _wgrad_cols("wgrad_in", sv["xn"], dproj, carry=_sibling_share(fulls), colsum=True)
        file_reduced(l, early, shared)
        dh, dh_b, d_g1 = _rms_bwd(dxn, sv["h"], sp["g1"], dh_mid)
        late_grads = [g_in, g_ab, g_sb, g_out]
        small_grads[l] = dict(norm1_g=d_g1[0], b_in=d_bin[0, :, 0, :].reshape(-1), sinks=d_sink[0, :NQ],
                              sgu_ln_g=d_lng[0], sgu_ln_b=d_lnb[0], sgu_w=d_sgu_w, sgu_b=d_bt[:, :NG].T, norm2_g=d_g2[0])
    grad_x = dh[None]

    land = _sibling_exchange(late_grads).run()
    sums, sends = sums_of(late_grads, land)
    got = _chip_exchange(list(sends)).run()
    file_reduced(0, late, _sibling_share([_chip_sum(s, g) for s, g in zip(sums, got)]).run())
    big_out =[_adamw_stacked([reduced[l][t] for l in range(L)], big[t], big_m[t], big_v[t]) for t in range(len(big))]

    names = ["norm1_g", "b_in", "sinks", "sgu_ln_g", "sgu_ln_b", "sgu_w", "sgu_b", "norm2_g"]
    small_w = [norm1_g, b_in, sinks, sgu_ln_g, sgu_ln_b, sgu_w, sgu_b, norm2_g, final_g]
    small_m = [m_norm1_g, m_b_in, m_sinks, m_sgu_ln_g, m_sgu_ln_b, m_sgu_w, m_sgu_b, m_norm2_g, m_final_g]
    small_v = [v_norm1_g, v_b_in, v_sinks, v_sgu_ln_g, v_sgu_ln_b, v_sgu_w, v_sgu_b, v_norm2_g, v_final_g]
    small_g = [jnp.stack([small_grads[l][nm] for l in range(L)]) for nm in names] + [d_final[0]]
    sizes = [w.size for w in small_w]
    total = sum(sizes) + 1
    rows = -(-total // (512 * LANES)) * 512
    loss_piece = jnp.sum(loss_part[0]).reshape(1)
    packed_g = _pack(small_g + [loss_piece], rows)
    one = jnp.ones((1,), F32)
    parts = _gather_all(packed_g)
    outs = _adamw_small(parts, _pack(small_w + [one], rows), _pack(small_m + [one], rows), _pack(small_v + [one], rows))

    def unpack(p):
        flat = p.reshape(-1)
        res, off = [], 0
        for w, n in zip(small_w, sizes):
            res.append(flat[off:off + n].reshape(w.shape))
            off += n
        return res, flat[off]

    (sg, loss), (sd, _), (smm, _), (svv, _) = [unpack(o) for o in outs]

    order = ["norm1_g", "w_in", "b_in", "sinks", "sgu_ln_g", "sgu_ln_b", "sgu_w", "sgu_b", "w_attn_branch",
             "w_sgu_branch", "w_out", "norm2_g", "w_gate_up", "w_down", "final_g"]
    big_names = ["w_in", "w_attn_branch", "w_sgu_branch", "w_out", "w_gate_up", "w_down"]
    small_names = names + ["final_g"]

    def collect(kind):
        res = []
        for nm in order:
            if nm in big_names:
                res.append(big_out[big_names.index(nm)][kind])
            else:
                res.append((sg, sd, smm, svv)[kind][small_names.index(nm)])
        return res

    return (loss, grad_x, *collect(0), *collect(1), *collect(2), *collect(3))
```

```python
import math

import jax
import jax.numpy as jnp
from jax import lax
from jax.experimental import pallas as pl
from jax.experimental.pallas import tpu as pltpu

F32 = jnp.float32
BF16 = jnp.bfloat16
MESH = pl.DeviceIdType.MESH
ANY = pl.BlockSpec(memory_space=pl.ANY)

HEAD_DIM = 64
N_KV_HEADS = 4
WINDOW = 128
ROPE_DIM = HEAD_DIM // 4
ROPE_THETA = 500000.0
EPS = 1e-5
NEG = -1e30
N_CHIPS = 4
LANES = 128
V7X_VMEM_LIMIT = 56 * 1024 * 1024

ADAM_LR = 0.001
ADAM_B1 = 0.9
ADAM_B2 = 0.999
ADAM_EPS = 1e-08
ADAM_WD = 0.01
ADAM_STEP = 10

NN = (((1,), (0,)), ((), ()))
NT = (((1,), (1,)), ((), ()))
TN = (((0,), (0,)), ((), ()))


ROW_TILES = (1024, 512, 256, 128, 64, 32, 16, 8)
BLOCK_BYTES = 2 * 1024 * 1024


def _pick(n, prefs):
    for p in prefs:
        if n % p == 0:
            return p
    raise ValueError(f"no tile for {n} among {prefs}")


def _row_tile(rows, cols, itemsize=4):
    return _pick(rows, [t for t in ROW_TILES if t * cols * itemsize <= BLOCK_BYTES or t == ROW_TILES[-1]])


def _dot(a, b, dims):
    return lax.dot_general(a, b, dims, preferred_element_type=F32)


def _sigmoid(x):
    return 1.0 / (1.0 + jnp.exp(-x))


def _gelu(x):
    return 0.5 * x * (1.0 + lax.erf(x * (1.0 / math.sqrt(2.0))))


def _gelu_grad(x):
    return 0.5 * (1.0 + lax.erf(x * (1.0 / math.sqrt(2.0)))) + x * jnp.exp(-0.5 * x * x) * (1.0 / math.sqrt(2.0 * math.pi))


def _params(sem):
    return pltpu.CompilerParams(dimension_semantics=sem, vmem_limit_bytes=V7X_VMEM_LIMIT)


def _matmul(name, lhs, rhs_list, *, dims, grid, lhs_spec, rhs_specs, acc_shape, out_shape, out_specs,
            epilogue, extra=(), extra_specs=(), carry=None, rhs_colsum=False, cols_outer=False):
    if cols_outer:
        swap = lambda s: pl.BlockSpec(s.block_shape, lambda j, i, k, f=s.index_map: f(i, j, k))
        grid = (grid[1], grid[0], grid[2])
        lhs_spec, rhs_specs = swap(lhs_spec), [swap(s) for s in rhs_specs]
        extra_specs, out_specs = [swap(s) for s in extra_specs], [swap(s) for s in out_specs]
    gk = grid[2]
    nr, ne, no = len(rhs_list), len(extra), len(out_shape)
    nci = len(carry.ins) if carry else 0
    nco = len(carry.outs) if carry else 0
    acc_shapes = [acc_shape] * nr + ([(8, acc_shape[1])] if rhs_colsum else [])
    nacc = len(acc_shapes) if gk > 1 else 0

    def body(*refs):
        a_ref = refs[0]
        b_refs = refs[1:1 + nr]
        e_refs = refs[1 + nr:1 + nr + ne]
        base = 1 + nr + ne
        ci_refs = refs[base:base + nci]
        o_refs = refs[base + nci:base + nci + no]
        co_refs = refs[base + nci + no:base + nci + no + nco]
        acc_refs = refs[base + nci + no + nco:base + nci + no + nco + nacc]
        sems = refs[base + nci + no + nco + nacc:]
        ids = [pl.program_id(d) for d in range(3)]
        if carry:
            @pl.when((ids[0] == 0) & (ids[1] == 0) & (ids[2] == 0))
            def _():
                carry.start(ci_refs, co_refs, *sems)

        a = a_ref[...]
        if gk == 1:
            n_axis = 1 - dims[0][1][0]
            for cols in _col_chunks(acc_shape[1]):
                pick = (slice(None), cols) if n_axis == 1 else (cols, slice(None))
                parts = [_dot(a, b[pick], dims) for b in b_refs]
                if rhs_colsum:
                    b0 = b_refs[0][pick]
                    parts.append(_dot(jnp.ones((8, b0.shape[0]), b0.dtype), b0, NN))
                epilogue(parts, e_refs, o_refs, cols)
        else:
            k = ids[2]

            @pl.when(k == 0)
            def _():
                for acc in acc_refs:
                    acc[...] = jnp.zeros_like(acc)

            for acc, b in zip(acc_refs, b_refs):
                acc[...] += _dot(a, b[...], dims)
            if rhs_colsum:
                b0 = b_refs[0][...]
                acc_refs[-1][...] += _dot(jnp.ones((8, b0.shape[0]), b0.dtype), b0, NN)

            @pl.when(k == gk - 1)
            def _():
                epilogue([acc[...] for acc in acc_refs], e_refs, o_refs, slice(None))

        if carry:
            @pl.when((ids[0] == grid[0] - 1) & (ids[1] == grid[1] - 1) & (ids[2] == grid[2] - 1))
            def _():
                carry.finish(ci_refs, co_refs, *sems)

    scratch = [pltpu.VMEM(s, F32) for s in acc_shapes[:nacc]]
    kwargs = {}
    if carry:
        scratch += carry.sem_scratch()
        kwargs["input_output_aliases"] = {1 + nr + ne + i: no + o for i, o in carry.aliases.items()}
    outs = pl.pallas_call(
        body, name=name, grid=grid,
        in_specs=[lhs_spec, *rhs_specs, *extra_specs, *([ANY] * nci)],
        out_specs=[*out_specs, *([ANY] * nco)],
        out_shape=[*out_shape, *(carry.outs if carry else [])], scratch_shapes=scratch,
        compiler_params=_params(("arbitrary",) * 3 if carry else ("parallel", "parallel", "arbitrary")),
        **kwargs,
    )(lhs, *rhs_list, *extra, *(carry.ins if carry else []))
    return outs


class _Comm:
    def __init__(self, name, ins, outs, aliases, n_sems, start, finish):
        self.name, self.ins, self.outs, self.aliases, self.n_sems = name, list(ins), list(outs), dict(aliases), n_sems
        self.start, self.finish = start, finish

    def sem_scratch(self):
        return [pltpu.SemaphoreType.DMA((self.n_sems,)), pltpu.SemaphoreType.DMA((self.n_sems,))]

    def run(self):
        ni = len(self.ins)

        def body(*refs):
            in_refs, out_refs, sems = refs[:ni], refs[ni:ni + len(self.outs)], refs[ni + len(self.outs):]
            self.start(in_refs, out_refs, *sems)
            self.finish(in_refs, out_refs, *sems)

        return pl.pallas_call(
            body, name=self.name, in_specs=[ANY] * ni, out_specs=[ANY] * len(self.outs), out_shape=self.outs,
            input_output_aliases=self.aliases, scratch_shapes=self.sem_scratch(),
        )(*self.ins)


MXU_CHUNK = 512


def _col_chunks(n):
    if n % LANES:
        return [slice(0, n)]
    return [slice(s, min(s + MXU_CHUNK, n)) for s in range(0, n, MXU_CHUNK)]


def _store_epilogue(dtype):
    def ep(parts, e_refs, o_refs, cols):
        o_refs[0][:, cols] = parts[0].astype(dtype)
    return ep


def _rms_fwd(h, g_row):
    S, D = h.shape
    tm = _row_tile(S, D)

    def body(h_ref, g_ref, o_ref):
        x = h_ref[...]
        r = lax.rsqrt(jnp.mean(x * x, axis=-1, keepdims=True) + EPS)
        o_ref[...] = (x * r * g_ref[...]).astype(BF16)

    return pl.pallas_call(
        body, name="rms_fwd", grid=(S // tm,),
        in_specs=[pl.BlockSpec((tm, D), lambda i: (i, 0)), pl.BlockSpec((1, D), lambda i: (0, 0))],
        out_specs=pl.BlockSpec((tm, D), lambda i: (i, 0)),
        out_shape=jax.ShapeDtypeStruct((S, D), BF16),
        compiler_params=_params(("parallel",)),
    )(h, g_row)


def _rms_bwd(dy, h, g_row, dres):
    S, D = h.shape
    tm = _row_tile(S, D)

    def body(dy_ref, h_ref, g_ref, dres_ref, dh_ref, dhb_ref, dg_ref):
        i = pl.program_id(0)
        x = h_ref[...]
        d = dy_ref[...]
        r = lax.rsqrt(jnp.mean(x * x, axis=-1, keepdims=True) + EPS)
        dg = d * g_ref[...]
        dot = jnp.mean(dg * x, axis=-1, keepdims=True)
        dh = dres_ref[...] + r * dg - x * (r * r * r) * dot
        dh_ref[...] = dh
        dhb_ref[...] = dh.astype(BF16)
        part = jnp.sum(d * x * r, axis=0, keepdims=True)

        @pl.when(i == 0)
        def _():
            dg_ref[...] = jnp.zeros_like(dg_ref)

        dg_ref[0:1, :] += part

    return pl.pallas_call(
        body, name="rms_bwd", grid=(S // tm,),
        in_specs=[pl.BlockSpec((tm, D), lambda i: (i, 0)), pl.BlockSpec((tm, D), lambda i: (i, 0)),
                  pl.BlockSpec((1, D), lambda i: (0, 0)), pl.BlockSpec((tm, D), lambda i: (i, 0))],
        out_specs=[pl.BlockSpec((tm, D), lambda i: (i, 0)), pl.BlockSpec((tm, D), lambda i: (i, 0)),
                   pl.BlockSpec((8, D), lambda i: (0, 0))],
        out_shape=[jax.ShapeDtypeStruct((S, D), F32), jax.ShapeDtypeStruct((S, D), BF16),
                   jax.ShapeDtypeStruct((8, D), F32)],
        compiler_params=_params(("arbitrary",)),
    )(dy, h, g_row, dres)


def _loss_head(h, g_row, target):
    S, D = h.shape
    tm = _row_tile(S, D)

    def body(h_ref, g_ref, t_ref, dh_ref, dhb_ref, dg_ref, loss_ref):
        i = pl.program_id(0)
        x = h_ref[...]
        g = g_ref[...]
        r = lax.rsqrt(jnp.mean(x * x, axis=-1, keepdims=True) + EPS)
        y = x * r * g
        e = y - t_ref[...]
        d = e * (1.0 / D)
        dg = d * g
        dot = jnp.mean(dg * x, axis=-1, keepdims=True)
        dh = r * dg - x * (r * r * r) * dot
        dh_ref[...] = dh
        dhb_ref[...] = dh.astype(BF16)

        @pl.when(i == 0)
        def _():
            dg_ref[...] = jnp.zeros_like(dg_ref)
            loss_ref[...] = jnp.zeros_like(loss_ref)

        dg_ref[0:1, :] += jnp.sum(d * x * r, axis=0, keepdims=True)
        loss_ref[0:1, :] += jnp.sum((0.5 / D) * e * e, axis=0, keepdims=True)

    return pl.pallas_call(
        body, name="loss_head", grid=(S // tm,),
        in_specs=[pl.BlockSpec((tm, D), lambda i: (i, 0)), pl.BlockSpec((1, D), lambda i: (0, 0)),
                  pl.BlockSpec((tm, D), lambda i: (i, 0))],
        out_specs=[pl.BlockSpec((tm, D), lambda i: (i, 0)), pl.BlockSpec((tm, D), lambda i: (i, 0)),
                   pl.BlockSpec((8, D), lambda i: (0, 0)), pl.BlockSpec((8, D), lambda i: (0, 0))],
        out_shape=[jax.ShapeDtypeStruct((S, D), F32), jax.ShapeDtypeStruct((S, D), BF16),
                   jax.ShapeDtypeStruct((8, D), F32), jax.ShapeDtypeStruct((8, D), F32)],
        compiler_params=_params(("arbitrary",)),
    )(h, g_row, target)


def _rope(t, cos, sa, sb):
    w = t.shape[-1]
    return t * cos + pltpu.roll(t, w - 8, 1) * sa + pltpu.roll(t, 8, 1) * sb


def _rope_t(g, cos, sa, sb):
    w = g.shape[-1]
    return g * cos + pltpu.roll(g * sa, 8, 1) + pltpu.roll(g * sb, w - 8, 1)


def _band_mask(n, qpk):
    qi = lax.broadcasted_iota(jnp.int32, (qpk * WINDOW, 2 * WINDOW), 0) & (WINDOW - 1)
    kj = lax.broadcasted_iota(jnp.int32, (qpk * WINDOW, 2 * WINDOW), 1)
    rel = qi + WINDOW - kj
    ok = (rel >= 0) & (rel < WINDOW)
    return ok & ((kj >= WINDOW) | (n > 0))


def _stack_heads(x, g, qpk):
    return jnp.concatenate([x[:, (g * qpk + hh) * HEAD_DIM:(g * qpk + hh + 1) * HEAD_DIM] for hh in range(qpk)], axis=0)


def _stack_cols(row, g, qpk):
    return jnp.concatenate([row[:, g * qpk + hh:g * qpk + hh + 1] for hh in range(qpk)], axis=0)


def _attn_specs(dm, nb):
    A, KV = dm["A"], dm["KV"]
    kb, vb = dm["OFF_K"] // KV, dm["OFF_V"] // KV
    cur = lambda n: jnp.minimum(n, nb - 1)
    prev = lambda n: jnp.maximum(jnp.minimum(n, nb - 1) - 1, 0)
    proj_specs = [
        pl.BlockSpec((WINDOW, A), lambda n: (cur(n), 0)),
        pl.BlockSpec((WINDOW, KV), lambda n: (prev(n), kb)),
        pl.BlockSpec((WINDOW, KV), lambda n: (cur(n), kb)),
        pl.BlockSpec((WINDOW, KV), lambda n: (prev(n), vb)),
        pl.BlockSpec((WINDOW, KV), lambda n: (cur(n), vb)),
    ]
    trig_cur = [pl.BlockSpec((WINDOW, LANES), lambda n: (cur(n), 0)) for _ in range(3)]
    trig_prev = [pl.BlockSpec((WINDOW, LANES), lambda n: (prev(n), 0)) for _ in range(3)]
    return proj_specs, trig_cur, trig_prev, cur, prev


def _attn_fwd(proj, trig, sink_row, dm):
    S = proj.shape[0]
    A, KV, NQ = dm["A"], dm["KV"], dm["NQ"]
    qpk = NQ // N_KV_HEADS
    nb = S // WINDOW
    scale = HEAD_DIM ** -0.5
    proj_specs, trig_cur, trig_prev, cur, _ = _attn_specs(dm, nb)

    def body(q_ref, kp_ref, kc_ref, vp_ref, vc_ref, cc_ref, sac_ref, sbc_ref, cp_ref, sap_ref, sbp_ref,
             sink_ref, y_ref, lse_ref):
        n = pl.program_id(0)
        tq = lambda r: jnp.tile(r[...], (1, A // LANES))
        tk = lambda rp, rc: jnp.tile(jnp.concatenate([rp[...], rc[...]], axis=0), (1, KV // LANES))
        qr = _rope(q_ref[...].astype(F32), tq(cc_ref), tq(sac_ref), tq(sbc_ref)).astype(BF16)
        kband = jnp.concatenate([kp_ref[...], kc_ref[...]], axis=0).astype(F32)
        kr = _rope(kband, tk(cp_ref, cc_ref), tk(sap_ref, sac_ref), tk(sbp_ref, sbc_ref)).astype(BF16)
        vband = jnp.concatenate([vp_ref[...], vc_ref[...]], axis=0)
        mask = _band_mask(n, qpk)
        lane = lax.broadcasted_iota(jnp.int32, (WINDOW, LANES), 1)
        lse_all = jnp.zeros((WINDOW, LANES), F32)
        sink_rows = jnp.broadcast_to(sink_ref[0:1, :], (WINDOW, LANES))
        for g in range(N_KV_HEADS):
            k_g = kr[:, g * HEAD_DIM:(g + 1) * HEAD_DIM]
            v_g = vband[:, g * HEAD_DIM:(g + 1) * HEAD_DIM]
            q_g = _stack_heads(qr, g, qpk)
            sink = _stack_cols(sink_rows, g, qpk)
            s = jnp.where(mask, _dot(q_g, k_g, NT) * scale, NEG)
            m = jnp.maximum(jnp.max(s, axis=-1, keepdims=True), sink)
            p = jnp.exp(s - m)
            den = jnp.sum(p, axis=-1, keepdims=True) + jnp.exp(sink - m)
            o = _dot(p.astype(BF16), v_g, NN) * (1.0 / den)
            lse_g = m + jnp.log(den)
            for hh in range(qpk):
                h = g * qpk + hh
                rows = slice(hh * WINDOW, (hh + 1) * WINDOW)
                y_ref[:, h * HEAD_DIM:(h + 1) * HEAD_DIM] = o[rows].astype(BF16)
                lse_all = jnp.where(lane == h, lse_g[rows], lse_all)
        lse_ref[...] = lse_all

    return pl.pallas_call(
        body, name="attn_fwd", grid=(nb,),
        in_specs=[*proj_specs, *trig_cur, *trig_prev, pl.BlockSpec((8, LANES), lambda n: (0, 0))],
        out_specs=[pl.BlockSpec((WINDOW, A), lambda n: (n, 0)), pl.BlockSpec((WINDOW, LANES), lambda n: (n, 0))],
        out_shape=[jax.ShapeDtypeStruct((S, A), BF16), jax.ShapeDtypeStruct((S, LANES), F32)],
        compiler_params=_params(("parallel",)),
    )(proj, proj, proj, proj, proj, *trig, *trig, sink_row)


def _attn_bwd(proj, trig, sink_row, y, lse, dy, dm):
    S = proj.shape[0]
    A, KV, NQ = dm["A"], dm["KV"], dm["NQ"]
    qpk = NQ // N_KV_HEADS
    nb = S // WINDOW
    scale = HEAD_DIM ** -0.5
    proj_specs, trig_cur, trig_prev, cur, prev = _attn_specs(dm, nb)

    def body(q_ref, kp_ref, kc_ref, vp_ref, vc_ref, cc_ref, sac_ref, sbc_ref, cp_ref, sap_ref, sbp_ref,
             sink_ref, y_ref, lse_ref, dy_ref, dq_ref, dk_ref, dv_ref, dsink_ref,
             ck_ref, cv_ref, bk_ref, bv_ref, dqr_ref):
        n = pl.program_id(0)

        @pl.when(n == 0)
        def _():
            dsink_ref[...] = jnp.zeros_like(dsink_ref)
            ck_ref[...] = jnp.zeros_like(ck_ref)
            cv_ref[...] = jnp.zeros_like(cv_ref)

        @pl.when(n < nb)
        def _():
            tq = lambda r: jnp.tile(r[...], (1, A // LANES))
            tk = lambda rp, rc: jnp.tile(jnp.concatenate([rp[...], rc[...]], axis=0), (1, KV // LANES))
            cq, saq, sbq = tq(cc_ref), tq(sac_ref), tq(sbc_ref)
            ck, sak, sbk = tk(cp_ref, cc_ref), tk(sap_ref, sac_ref), tk(sbp_ref, sbc_ref)
            qr = _rope(q_ref[...].astype(F32), cq, saq, sbq).astype(BF16)
            kband = jnp.concatenate([kp_ref[...], kc_ref[...]], axis=0).astype(F32)
            kr = _rope(kband, ck, sak, sbk).astype(BF16)
            vband = jnp.concatenate([vp_ref[...], vc_ref[...]], axis=0)
            mask = _band_mask(n, qpk)
            lane = lax.broadcasted_iota(jnp.int32, (1, LANES), 1)
            lse_all = lse_ref[...]
            sink_rows = jnp.broadcast_to(sink_ref[0:1, :], (WINDOW, LANES))
            dy_all = dy_ref[...]
            y_all = y_ref[...]
            dsink = jnp.zeros((1, LANES), F32)
            for g in range(N_KV_HEADS):
                k_g = kr[:, g * HEAD_DIM:(g + 1) * HEAD_DIM]
                v_g = vband[:, g * HEAD_DIM:(g + 1) * HEAD_DIM]
                q_g = _stack_heads(qr, g, qpk)
                dy_g = _stack_heads(dy_all, g, qpk)
                y_g = _stack_heads(y_all, g, qpk)
                lse_g = _stack_cols(lse_all, g, qpk)
                s = jnp.where(mask, _dot(q_g, k_g, NT) * scale, NEG)
                p = jnp.exp(s - lse_g)
                dp = _dot(dy_g, v_g, NT)
                delta = jnp.sum(dy_g.astype(F32) * y_g.astype(F32), axis=-1, keepdims=True)
                ds = (p * (dp - delta) * scale).astype(BF16)
                dq_g = _dot(ds, k_g, NN)
                bk_ref[:, g * HEAD_DIM:(g + 1) * HEAD_DIM] = _dot(ds, q_g, TN)
                bv_ref[:, g * HEAD_DIM:(g + 1) * HEAD_DIM] = _dot(p.astype(BF16), dy_g, TN)
                sink_d = jnp.exp(_stack_cols(sink_rows, g, qpk) - lse_g) * delta
                for hh in range(qpk):
                    h = g * qpk + hh
                    rows = slice(hh * WINDOW, (hh + 1) * WINDOW)
                    dqr_ref[:, h * HEAD_DIM:(h + 1) * HEAD_DIM] = dq_g[rows]
                    dsink = dsink + jnp.where(lane == h, -jnp.sum(sink_d[rows], axis=0, keepdims=True), 0.0)
            dsink_ref[0:1, :] += dsink
            dq_ref[...] = _rope_t(dqr_ref[...], cq, saq, sbq).astype(BF16)
            dkb = _rope_t(bk_ref[...], ck, sak, sbk)
            dvb = bv_ref[...]
            dk_ref[...] = (ck_ref[...] + dkb[:WINDOW]).astype(BF16)
            dv_ref[...] = (cv_ref[...] + dvb[:WINDOW]).astype(BF16)
            ck_ref[...] = dkb[WINDOW:]
            cv_ref[...] = dvb[WINDOW:]

        @pl.when(n == nb)
        def _():
            dk_ref[...] = ck_ref[...].astype(BF16)
            dv_ref[...] = cv_ref[...].astype(BF16)

    row = lambda w: pl.BlockSpec((WINDOW, w), lambda n: (cur(n), 0))
    done = lambda w: pl.BlockSpec((WINDOW, w), lambda n: (jnp.maximum(n - 1, 0), 0))
    return pl.pallas_call(
        body, name="attn_bwd", grid=(nb + 1,),
        in_specs=[*proj_specs, *trig_cur, *trig_prev, pl.BlockSpec((8, LANES), lambda n: (0, 0)),
                  row(A), row(LANES), row(A)],
        out_specs=[row(A), done(KV), done(KV), pl.BlockSpec((8, LANES), lambda n: (0, 0))],
        out_shape=[jax.ShapeDtypeStruct((S, A), BF16), jax.ShapeDtypeStruct((S, KV), BF16),
                   jax.ShapeDtypeStruct((S, KV), BF16), jax.ShapeDtypeStruct((8, LANES), F32)],
        scratch_shapes=[pltpu.VMEM((WINDOW, KV), F32), pltpu.VMEM((WINDOW, KV), F32),
                        pltpu.VMEM((2 * WINDOW, KV), F32), pltpu.VMEM((2 * WINDOW, KV), F32),
                        pltpu.VMEM((WINDOW, A), F32)],
        compiler_params=_params(("arbitrary",)),
    )(proj, proj, proj, proj, proj, *trig, *trig, sink_row, y, lse, dy)


def _sgu_layout(dm, S):
    G = dm["G"]
    pw = math.gcd(dm["OFF_Z"], G)
    npc = G // pw
    tm = _pick(S, (256, 128))
    u_specs = [pl.BlockSpec((tm, pw), lambda i, p=p: (i, dm["OFF_Z"] // pw + p)) for p in range(npc)]
    v_specs = [pl.BlockSpec((tm, pw), lambda i, p=p: (i, (dm["OFF_Z"] + G) // pw + p)) for p in range(npc)]
    return pw, npc, tm, u_specs, v_specs


def _sgu_norm(v_refs, lg_ref, lb_ref):
    v = jnp.concatenate([_gelu(r[...].astype(F32)) for r in v_refs], axis=1)
    mu = jnp.mean(v, axis=-1, keepdims=True)
    vc = v - mu
    rstd = lax.rsqrt(jnp.mean(vc * vc, axis=-1, keepdims=True) + EPS)
    xhat = vc * rstd
    return xhat, rstd, (xhat * lg_ref[...] + lb_ref[...]).astype(BF16)


def _sgu_fwd(proj, w_tril, b_t, ln_g_row, ln_b_row, dm):
    S = proj.shape[0]
    G, NG = dm["G"], dm["NG"]
    pw, npc, tm, u_specs, v_specs = _sgu_layout(dm, S)
    nch = tm // WINDOW

    def body(*refs):
        u_refs, v_refs = refs[:npc], refs[npc:2 * npc]
        w_ref, bt_ref, lg_ref, lb_ref, y_ref = refs[2 * npc:]
        _, _, vn = _sgu_norm(v_refs, lg_ref, lb_ref)
        u = jnp.concatenate([_gelu(r[...].astype(F32)) for r in u_refs], axis=1)
        for c in range(nch):
            rows = slice(c * WINDOW, (c + 1) * WINDOW)
            for g in range(NG):
                cols = slice(g * LANES, (g + 1) * LANES)
                sv = _dot(w_ref[g], vn[rows, cols], NN) + bt_ref[:, g:g + 1]
                y_ref[rows, cols] = (u[rows, cols] * sv).astype(BF16)

    return pl.pallas_call(
        body, name="sgu_fwd", grid=(S // tm,),
        in_specs=[*u_specs, *v_specs,
                  pl.BlockSpec((NG, WINDOW, WINDOW), lambda i: (0, 0, 0)),
                  pl.BlockSpec((WINDOW, LANES), lambda i: (0, 0)),
                  pl.BlockSpec((1, G), lambda i: (0, 0)), pl.BlockSpec((1, G), lambda i: (0, 0))],
        out_specs=pl.BlockSpec((tm, G), lambda i: (i, 0)),
        out_shape=jax.ShapeDtypeStruct((S, G), BF16),
        compiler_params=_params(("parallel",)),
    )(*([proj] * (2 * npc)), w_tril, b_t, ln_g_row, ln_b_row)


def _sgu_bwd(proj, w_tril, b_t, ln_g_row, ln_b_row, dy, dm):
    S = proj.shape[0]
    G, NG = dm["G"], dm["NG"]
    pw, npc, tm, u_specs, v_specs = _sgu_layout(dm, S)
    nch = tm // WINDOW

    def body(*refs):
        u_refs, v_refs = refs[:npc], refs[npc:2 * npc]
        w_ref, bt_ref, lg_ref, lb_ref, dy_ref, dz_ref, dw_ref, dbt_ref, dlg_ref, dlb_ref, dvn_ref = refs[2 * npc:]
        i = pl.program_id(0)

        @pl.when(i == 0)
        def _():
            dw_ref[...] = jnp.zeros_like(dw_ref)
            dbt_ref[...] = jnp.zeros_like(dbt_ref)
            dlg_ref[...] = jnp.zeros_like(dlg_ref)
            dlb_ref[...] = jnp.zeros_like(dlb_ref)

        xhat, rstd, vn = _sgu_norm(v_refs, lg_ref, lb_ref)
        u_pre = jnp.concatenate([r[...].astype(F32) for r in u_refs], axis=1)
        u = _gelu(u_pre)
        dy = dy_ref[...].astype(F32)
        lane = lax.broadcasted_iota(jnp.int32, (WINDOW, LANES), 1)
        tri = lax.broadcasted_iota(jnp.int32, (WINDOW, WINDOW), 0) >= lax.broadcasted_iota(jnp.int32, (WINDOW, WINDOW), 1)
        dbt = jnp.zeros((WINDOW, LANES), F32)
        for c in range(nch):
            rows = slice(c * WINDOW, (c + 1) * WINDOW)
            for g in range(NG):
                cols = slice(g * LANES, (g + 1) * LANES)
                vn_cg = vn[rows, cols]
                sv = _dot(w_ref[g], vn_cg, NN) + bt_ref[:, g:g + 1]
                dy_cg = dy[rows, cols]
                dsv = dy_cg * u[rows, cols]
                dsv_b = dsv.astype(BF16)
                dz_ref[rows, cols] = (dy_cg * sv * _gelu_grad(u_pre[rows, cols])).astype(BF16)
                dvn_ref[rows, cols] = _dot(w_ref[g], dsv_b, TN)
                dw_ref[g] += jnp.where(tri, _dot(dsv_b, vn_cg, NT), 0.0)
                dbt = dbt + jnp.where(lane == g, jnp.sum(dsv, axis=-1, keepdims=True), 0.0)
        dbt_ref[...] += dbt
        dvn = dvn_ref[...]
        dlg_ref[0:1, :] += jnp.sum(dvn * xhat, axis=0, keepdims=True)
        dlb_ref[0:1, :] += jnp.sum(dvn, axis=0, keepdims=True)
        dxh = dvn * lg_ref[...]
        dv = rstd * (dxh - jnp.mean(dxh, axis=-1, keepdims=True) - xhat * jnp.mean(dxh * xhat, axis=-1, keepdims=True))
        v_pre = jnp.concatenate([r[...].astype(F32) for r in v_refs], axis=1)
        dz_ref[:, G:] = (dv * _gelu_grad(v_pre)).astype(BF16)

    return pl.pallas_call(
        body, name="sgu_bwd", grid=(S // tm,),
        in_specs=[*u_specs, *v_specs,
                  pl.BlockSpec((NG, WINDOW, WINDOW), lambda i: (0, 0, 0)),
                  pl.BlockSpec((WINDOW, LANES), lambda i: (0, 0)),
                  pl.BlockSpec((1, G), lambda i: (0, 0)), pl.BlockSpec((1, G), lambda i: (0, 0)),
                  pl.BlockSpec((tm, G), lambda i: (i, 0))],
        out_specs=[pl.BlockSpec((tm, 2 * G), lambda i: (i, 0)),
                   pl.BlockSpec((NG, WINDOW, WINDOW), lambda i: (0, 0, 0)),
                   pl.BlockSpec((WINDOW, LANES), lambda i: (0, 0)),
                   pl.BlockSpec((8, G), lambda i: (0, 0)), pl.BlockSpec((8, G), lambda i: (0, 0))],
        out_shape=[jax.ShapeDtypeStruct((S, 2 * G), BF16), jax.ShapeDtypeStruct((NG, WINDOW, WINDOW), F32),
                   jax.ShapeDtypeStruct((WINDOW, LANES), F32), jax.ShapeDtypeStruct((8, G), F32),
                   jax.ShapeDtypeStruct((8, G), F32)],
        scratch_shapes=[pltpu.VMEM((tm, G), F32)],
        compiler_params=_params(("arbitrary",)),
    )(*([proj] * (2 * npc)), w_tril, b_t, ln_g_row, ln_b_row, dy)


def _result(outs, n_main, carry):
    main = outs[0] if n_main == 1 else tuple(outs[:n_main])
    return (main, list(outs[n_main:])) if carry else main


def _in_proj(xn, w_in_g, b_row, dm, carry=None):
    S, D = xn.shape
    IN = dm["IN"]
    cw = IN // N_CHIPS
    tm = _pick(S, (512, 256, 128))
    tn = _pick(cw, (1920, 640, 512, 256, 128))
    nbc = cw // tn

    def ep(parts, e_refs, o_refs, cols):
        o_refs[0][:, cols] = (parts[0] + e_refs[0][:, cols]).astype(BF16)

    return _result(_matmul(
        "in_proj", xn, [w_in_g], dims=NN, grid=(S // tm, IN // tn, 1),
        lhs_spec=pl.BlockSpec((tm, D), lambda i, j, k: (i, 0)),
        rhs_specs=[pl.BlockSpec((None, D, tn), lambda i, j, k: (j // nbc, 0, j % nbc))],
        acc_shape=(tm, tn), extra=[b_row], extra_specs=[pl.BlockSpec((1, tn), lambda i, j, k: (0, j))],
        out_shape=[jax.ShapeDtypeStruct((S, IN), BF16)],
        out_specs=[pl.BlockSpec((tm, tn), lambda i, j, k: (i, j))], epilogue=ep, carry=carry, cols_outer=True), 1, carry)


def _branch_attn(y_attn, w_ab_g, dm):
    S, A = y_attn.shape
    D = dm["D"]
    cw = D // N_CHIPS
    tm = _pick(S, (1024, 512, 256, 128))
    return _matmul(
        "branch_attn", y_attn, [w_ab_g], dims=NN, grid=(S // tm, N_CHIPS, 1),
        lhs_spec=pl.BlockSpec((tm, A), lambda i, j, k: (i, 0)),
        rhs_specs=[pl.BlockSpec((None, A, cw), lambda i, j, k: (j, 0, 0))],
        acc_shape=(tm, cw), out_shape=[jax.ShapeDtypeStruct((S, D), BF16)],
        out_specs=[pl.BlockSpec((tm, cw), lambda i, j, k: (i, j))], epilogue=_store_epilogue(BF16))[0]


def _branch_sgu_merge(y_sgu, w_sb_g, a_attn, proj, dm):
    S, G = y_sgu.shape
    D, OFF_G = dm["D"], dm["OFF_G"]
    cw = D // N_CHIPS
    tm = _pick(S, (1024, 512, 256, 128))

    def ep(parts, e_refs, o_refs, cols):
        a_sgu = parts[0].astype(BF16)
        ga = _sigmoid(e_refs[1][:, cols].astype(F32))
        gs = _sigmoid(e_refs[2][:, cols].astype(F32))
        o_refs[0][:, cols] = a_sgu
        o_refs[1][:, cols] = (ga * e_refs[0][:, cols].astype(F32) + gs * a_sgu.astype(F32)).astype(BF16)

    blk = pl.BlockSpec((tm, cw), lambda i, j, k: (i, j))
    return _matmul(
        "branch_sgu_merge", y_sgu, [w_sb_g], dims=NN, grid=(S // tm, N_CHIPS, 1),
        lhs_spec=pl.BlockSpec((tm, G), lambda i, j, k: (i, 0)),
        rhs_specs=[pl.BlockSpec((None, G, cw), lambda i, j, k: (j, 0, 0))],
        acc_shape=(tm, cw), extra=[a_attn, proj, proj],
        extra_specs=[blk, pl.BlockSpec((tm, cw), lambda i, j, k: (i, OFF_G // cw + j)),
                     pl.BlockSpec((tm, cw), lambda i, j, k: (i, (OFF_G + D) // cw + j))],
        out_shape=[jax.ShapeDtypeStruct((S, D), BF16), jax.ShapeDtypeStruct((S, D), BF16)],
        out_specs=[blk, blk], epilogue=ep)


def _residual_matmul(name, a, w_g, h, carry=None):
    S, K = a.shape
    D = w_g.shape[1]
    tm = _pick(S, (1024, 512, 256, 128))
    tn = _pick(D, (512, 256, 128))

    def ep(parts, e_refs, o_refs, cols):
        o_refs[0][:, cols] = e_refs[0][:, cols] + parts[0]

    blk = pl.BlockSpec((tm, tn), lambda i, j, k: (i, j))
    return _result(_matmul(
        name, a, [w_g], dims=NN, grid=(S // tm, D // tn, 1),
        lhs_spec=pl.BlockSpec((tm, K), lambda i, j, k: (i, 0)),
        rhs_specs=[pl.BlockSpec((K, tn), lambda i, j, k: (0, j))],
        acc_shape=(tm, tn), extra=[h], extra_specs=[blk],
        out_shape=[jax.ShapeDtypeStruct((S, D), F32)], out_specs=[blk], epilogue=ep, carry=carry), 1, carry)


def _gate_up(hn, w_gu_g, dm, carry=None):
    S, D = hn.shape
    Fd = dm["F"]
    cw = 2 * Fd // N_CHIPS
    tm = _pick(S, (512, 256, 128))
    tn = _pick(cw, (1408, 512, 384, 256, 128))
    nbc = cw // tn
    half = N_CHIPS // 2

    def ep(parts, e_refs, o_refs, cols):
        gate, up = parts[0].astype(BF16), parts[1].astype(BF16)
        o_refs[0][0, :, cols] = gate
        o_refs[0][1, :, cols] = up
        g32 = gate.astype(F32)
        o_refs[1][:, cols] = (g32 * _sigmoid(g32) * up.astype(F32)).astype(BF16)

    return _result(_matmul(
        "gate_up", hn, [w_gu_g, w_gu_g], dims=NN, grid=(S // tm, Fd // tn, 1),
        lhs_spec=pl.BlockSpec((tm, D), lambda i, j, k: (i, 0)),
        rhs_specs=[pl.BlockSpec((None, D, tn), lambda i, j, k: (j // nbc, 0, j % nbc)),
                   pl.BlockSpec((None, D, tn), lambda i, j, k: (half + j // nbc, 0, j % nbc))],
        acc_shape=(tm, tn),
        out_shape=[jax.ShapeDtypeStruct((2, S, Fd), BF16), jax.ShapeDtypeStruct((S, Fd), BF16)],
        out_specs=[pl.BlockSpec((2, tm, tn), lambda i, j, k: (0, i, j)), pl.BlockSpec((tm, tn), lambda i, j, k: (i, j))],
        epilogue=ep, carry=carry, cols_outer=True), 2, carry)


def _down_bwd(dh_b, w_down_g, gu, dm, carry=None):
    S, D = dh_b.shape
    Fd = dm["F"]
    tm = _pick(S, (1024, 512, 256, 128))
    tn = _pick(Fd, (512, 256, 128))

    def ep(parts, e_refs, o_refs, cols):
        gate = e_refs[0][0, :, cols].astype(F32)
        up = e_refs[0][1, :, cols].astype(F32)
        s = _sigmoid(gate)
        dact = parts[0]
        o_refs[0][0, :, cols] = (dact * up * s * (1.0 + gate * (1.0 - s))).astype(BF16)
        o_refs[0][1, :, cols] = (dact * gate * s).astype(BF16)

    blk = pl.BlockSpec((2, tm, tn), lambda i, j, k: (0, i, j))
    return _result(_matmul(
        "down_bwd", dh_b, [w_down_g], dims=NT, grid=(S // tm, Fd // tn, 1),
        lhs_spec=pl.BlockSpec((tm, D), lambda i, j, k: (i, 0)),
        rhs_specs=[pl.BlockSpec((tn, D), lambda i, j, k: (j, 0))],
        acc_shape=(tm, tn), extra=[gu], extra_specs=[blk],
        out_shape=[jax.ShapeDtypeStruct((2, S, Fd), BF16)], out_specs=[blk], epilogue=ep, carry=carry), 1, carry)


def _gate_up_bwd(dgu, w_gu_g, dm, carry=None):
    S = dgu.shape[1]
    D, Fd = dm["D"], dm["F"]
    cw = 2 * Fd // N_CHIPS
    half = N_CHIPS // 2
    tm = _pick(S, (1024, 512, 256, 128))
    tn = _pick(D, (1024, 512, 256, 128))
    return _result(_matmul(
        "gate_up_bwd", dgu, [w_gu_g], dims=NT, grid=(S // tm, D // tn, N_CHIPS),
        lhs_spec=pl.BlockSpec((None, tm, cw), lambda i, j, k: (k // half, i, k % half)),
        rhs_specs=[pl.BlockSpec((None, tn, cw), lambda i, j, k: (k, j, 0))],
        acc_shape=(tm, tn), out_shape=[jax.ShapeDtypeStruct((S, D), F32)],
        out_specs=[pl.BlockSpec((tm, tn), lambda i, j, k: (i, j))], epilogue=_store_epilogue(F32), carry=carry), 1, carry)


def _out_bwd(dh_b, w_out_g, proj, a_attn, a_sgu, dm, carry=None):
    S, D = dh_b.shape
    OFF_G = dm["OFF_G"]
    tm = _pick(S, (1024, 512, 256, 128))
    tn = D // N_CHIPS

    def ep(parts, e_refs, o_refs, cols):
        dm_ = parts[0]
        ga = _sigmoid(e_refs[0][:, cols].astype(F32))
        gs = _sigmoid(e_refs[1][:, cols].astype(F32))
        o_refs[0][:, cols] = (dm_ * ga).astype(BF16)
        o_refs[1][:, cols] = (dm_ * gs).astype(BF16)
        o_refs[2][0, :, cols] = (dm_ * e_refs[2][:, cols].astype(F32) * ga * (1.0 - ga)).astype(BF16)
        o_refs[2][1, :, cols] = (dm_ * e_refs[3][:, cols].astype(F32) * gs * (1.0 - gs)).astype(BF16)

    blk = pl.BlockSpec((tm, tn), lambda i, j, k: (i, j))
    return _result(_matmul(
        "out_bwd", dh_b, [w_out_g], dims=NT, grid=(S // tm, D // tn, 1),
        lhs_spec=pl.BlockSpec((tm, D), lambda i, j, k: (i, 0)),
        rhs_specs=[pl.BlockSpec((tn, D), lambda i, j, k: (j, 0))],
        acc_shape=(tm, tn), extra=[proj, proj, a_attn, a_sgu],
        extra_specs=[pl.BlockSpec((tm, tn), lambda i, j, k: (i, OFF_G // tn + j)),
                     pl.BlockSpec((tm, tn), lambda i, j, k: (i, (OFF_G + D) // tn + j)), blk, blk],
        out_shape=[jax.ShapeDtypeStruct((S, D), BF16), jax.ShapeDtypeStruct((S, D), BF16),
                   jax.ShapeDtypeStruct((2, S, D), BF16)],
        out_specs=[blk, blk, pl.BlockSpec((2, tm, tn), lambda i, j, k: (0, i, j))], epilogue=ep, carry=carry), 3, carry)


def _colsharded_bwd(name, dy, w_g, out_dtype, carry=None):
    S = dy.shape[0]
    _, K, cw = w_g.shape
    tm = _pick(S, (1024, 512, 256, 128))
    tn = _pick(K, (1024, 512, 256, 128))
    return _result(_matmul(
        name, dy, [w_g], dims=NT, grid=(S // tm, K // tn, N_CHIPS),
        lhs_spec=pl.BlockSpec((tm, cw), lambda i, j, k: (i, k)),
        rhs_specs=[pl.BlockSpec((None, tn, cw), lambda i, j, k: (k, j, 0))],
        acc_shape=(tm, tn), out_shape=[jax.ShapeDtypeStruct((S, K), out_dtype)],
        out_specs=[pl.BlockSpec((tm, tn), lambda i, j, k: (i, j))], epilogue=_store_epilogue(out_dtype),
        carry=carry), 1, carry)


def _wgrad_cols(name, x, dy, carry=None, colsum=False):
    S, R = x.shape
    C = dy.shape[1]
    cw = C // N_CHIPS
    tm = _pick(R, (1024, 512, 256, 128))
    tk = _pick(S, (1024, 512, 256, 128) if cw >= 1024 else (2048, 1024, 512, 256, 128))

    def ep(parts, e_refs, o_refs, cols):
        for o, p in zip(o_refs, parts):
            o[:, cols] = p

    out_shape = [jax.ShapeDtypeStruct((N_CHIPS, R, cw), F32)]
    out_specs = [pl.BlockSpec((None, tm, cw), lambda i, j, k: (j, i, 0))]
    if colsum:
        out_shape.append(jax.ShapeDtypeStruct((R // tm, N_CHIPS, 8, cw), F32))
        out_specs.append(pl.BlockSpec((None, None, 8, cw), lambda i, j, k: (i, j, 0, 0)))
    return _result(_matmul(
        name, x, [dy], dims=TN, grid=(R // tm, N_CHIPS, S // tk),
        lhs_spec=pl.BlockSpec((tk, tm), lambda i, j, k: (k, i)),
        rhs_specs=[pl.BlockSpec((tk, cw), lambda i, j, k: (k, j))],
        acc_shape=(tm, cw), out_shape=out_shape, out_specs=out_specs, epilogue=ep,
        carry=carry, rhs_colsum=colsum), len(out_shape), carry)


def _wgrad_gate_up(hn, dgu, dm, carry=None):
    S, D = hn.shape
    Fd = dm["F"]
    cw = 2 * Fd // N_CHIPS
    half = N_CHIPS // 2
    tm = _pick(D, (1024, 512, 256, 128))
    tk = _pick(S, (1024, 512, 256, 128))
    tn = _pick(cw, (1408, 512, 384, 256, 128))
    nbc = cw // tn
    return _result(_matmul(
        "wgrad_gate_up", hn, [dgu], dims=TN, grid=(D // tm, 2 * Fd // tn, S // tk),
        lhs_spec=pl.BlockSpec((tk, tm), lambda i, j, k: (k, i)),
        rhs_specs=[pl.BlockSpec((None, tk, tn), lambda i, j, k: (j // (half * nbc), k, j % (half * nbc)))],
        acc_shape=(tm, tn), out_shape=[jax.ShapeDtypeStruct((N_CHIPS, D, cw), F32)],
        out_specs=[pl.BlockSpec((None, tm, tn), lambda i, j, k: (j // nbc, i, j % nbc))], epilogue=_store_epilogue(F32),
        carry=carry), 1, carry)


def _wgrad_rows(name, x, dy):
    S, R = x.shape
    C = dy.shape[1]
    rw = R // N_CHIPS
    tn = _pick(C, (1024, 512, 256, 128))
    tk = _pick(S, (1024, 512, 256, 128))
    return _matmul(
        name, x, [dy], dims=TN, grid=(N_CHIPS, C // tn, S // tk),
        lhs_spec=pl.BlockSpec((tk, rw), lambda i, j, k: (k, i)),
        rhs_specs=[pl.BlockSpec((tk, tn), lambda i, j, k: (k, j))],
        acc_shape=(rw, tn), out_shape=[jax.ShapeDtypeStruct((N_CHIPS, rw, C), F32)],
        out_specs=[pl.BlockSpec((None, rw, tn), lambda i, j, k: (i, 0, j))], epilogue=_store_epilogue(F32))[0]


def _place():
    x, y, c = lax.axis_index("x"), lax.axis_index("y"), lax.axis_index("c")
    others = [(1 - x, y), (x, 1 - y), (1 - x, 1 - y)]
    return x, y, c, others


def _chip_index(chip):
    return 2 * chip[0] + chip[1]


def _gather_weights(bufs):
    n = len(bufs)

    def copies(src, out, send_sems, recv_sems):
        x, y, c, others = _place()

        def half(ref, chip_idx, hc):
            r2 = ref.shape[1] // 2
            return ref.at[chip_idx, pl.ds(hc * r2, r2), :]

        def copy(t, k, chip, hc, to):
            return pltpu.make_async_remote_copy(
                src_ref=half(src[t], _chip_index(chip), hc), dst_ref=half(out[t], _chip_index(chip), hc),
                send_sem=send_sems.at[6 * t + k], recv_sem=recv_sems.at[6 * t + k],
                device_id=to, device_id_type=MESH)

        me, sibling = (x, y, c), (x, y, 1 - c)
        pairs = [(t, j, chip) for t in range(n) for j, chip in enumerate(others)]
        sent = [copy(t, j, (x, y), c, (*chip, c)) for t, j, chip in pairs]
        landed = [copy(t, j, chip, c, me) for t, j, chip in pairs]
        passed = [copy(t, 3 + j, chip, c, sibling) for t, j, chip in pairs]
        handed = [copy(t, 3 + j, chip, 1 - c, me) for t, j, chip in pairs]
        return sent, landed, passed, handed

    def start(src, out, send_sems, recv_sems):
        for cp in copies(src, out, send_sems, recv_sems)[0]:
            cp.start()

    def finish(src, out, send_sems, recv_sems):
        sent, landed, passed, handed = copies(src, out, send_sems, recv_sems)
        for arrival, forward in zip(landed, passed):
            arrival.wait_recv()
            forward.start()
        for cp in handed:
            cp.wait_recv()
        for cp in sent + passed:
            cp.wait_send()

    return _Comm("gather_weights", bufs, [jax.ShapeDtypeStruct(b.shape, BF16) for b in bufs],
                 {t: t for t in range(n)}, 6 * n, start, finish)


def _sibling_exchange(grads):
    n = len(grads)
    shapes = [g.shape for g in grads]

    def copies(src, land, send_sems, recv_sems):
        x, y, c, _ = _place()
        res = []
        for t in range(n):
            r2 = shapes[t][1] // 2
            res.append(pltpu.make_async_remote_copy(
                src_ref=src[t].at[:, pl.ds((1 - c) * r2, r2), :], dst_ref=land[t],
                send_sem=send_sems.at[t], recv_sem=recv_sems.at[t], device_id=(x, y, 1 - c), device_id_type=MESH))
        return res

    def start(*refs):
        for cp in copies(*refs):
            cp.start()

    def finish(*refs):
        remote = copies(*refs)
        for cp in remote:
            cp.wait_recv()
        for cp in remote:
            cp.wait_send()

    return _Comm("sibling_exchange", grads, [jax.ShapeDtypeStruct((s[0], s[1] // 2, s[2]), F32) for s in shapes],
                 {}, n, start, finish)


def _chip_exchange(sends):
    n = len(sends)
    shapes = [s.shape for s in sends]

    def copies(snd, got, send_sems, recv_sems):
        x, y, c, others = _place()
        return [pltpu.make_async_remote_copy(
            src_ref=snd[t].at[_chip_index(chip)], dst_ref=got[t].at[j],
            send_sem=send_sems.at[3 * t + j], recv_sem=recv_sems.at[3 * t + j],
            device_id=(*chip, c), device_id_type=MESH) for t in range(n) for j, chip in enumerate(others)]

    def start(*refs):
        for cp in copies(*refs):
            cp.start()

    def finish(*refs):
        remote = copies(*refs)
        for cp in remote:
            cp.wait_recv()
        for cp in remote:
            cp.wait_send()

    return _Comm("chip_exchange", sends, [jax.ShapeDtypeStruct((3, s[1], s[2]), BF16) for s in shapes],
                 {}, 3 * n, start, finish)


def _sibling_share(fulls):
    n = len(fulls)
    shapes = [f.shape for f in fulls]

    def copies(src, out, send_sems, recv_sems, mine):
        x, y, c, _ = _place()
        hc = c if mine else 1 - c
        res = []
        for t in range(n):
            r2 = shapes[t][0] // 2
            res.append(pltpu.make_async_remote_copy(
                src_ref=src[t].at[pl.ds(hc * r2, r2), :], dst_ref=out[t].at[pl.ds(hc * r2, r2), :],
                send_sem=send_sems.at[t], recv_sem=recv_sems.at[t], device_id=(x, y, 1 - c), device_id_type=MESH))
        return res

    def start(*refs):
        for cp in copies(*refs, mine=True):
            cp.start()

    def finish(*refs):
        for cp in copies(*refs, mine=False):
            cp.wait_recv()
        for cp in copies(*refs, mine=True):
            cp.wait_send()

    return _Comm("sibling_share", fulls, [jax.ShapeDtypeStruct(s, F32) for s in shapes],
                 {t: t for t in range(n)}, n, start, finish)


def _gather_all(v):
    R, C = v.shape

    def body(v_ref, out_ref, send_sems, recv_sems, local_sem):
        x, y, c, others = _place()
        me, sibling = (x, y, c), (x, y, 1 - c)

        def rows(px, py, pc):
            return out_ref.at[4 * px + 2 * py + pc]

        def copy(k, block, to, src=None):
            return pltpu.make_async_remote_copy(
                src_ref=rows(*block) if src is None else src, dst_ref=rows(*block),
                send_sem=send_sems.at[k], recv_sem=recv_sems.at[k], device_id=to, device_id_type=MESH)

        mine = pltpu.make_async_copy(v_ref, rows(*me), local_sem)
        mine.start()
        first = [copy(0, me, sibling, src=v_ref)]
        first += [copy(1 + j, me, (*chip, c), src=v_ref) for j, chip in enumerate(others)]
        for cp in first:
            cp.start()
        passed = [copy(4 + j, (*chip, c), sibling) for j, chip in enumerate(others)]
        for j, chip in enumerate(others):
            copy(1 + j, (*chip, c), me).wait_recv()
            passed[j].start()
        copy(0, sibling, me).wait_recv()
        for j, chip in enumerate(others):
            copy(4 + j, (*chip, 1 - c), me).wait_recv()
        for cp in first + passed:
            cp.wait_send()
        mine.wait()

    return pl.pallas_call(
        body, name="gather_all", in_specs=[ANY], out_specs=ANY,
        out_shape=jax.ShapeDtypeStruct((8, R, C), F32),
        scratch_shapes=[pltpu.SemaphoreType.DMA((7,)), pltpu.SemaphoreType.DMA((7,)), pltpu.SemaphoreType.DMA],
    )(v)


def _my_chip():
    return 2 * lax.axis_index("x") + lax.axis_index("y")


def _my_core():
    return lax.axis_index("c")


def _pair_sum(grad, land):
    K, R2, C = land.shape
    tm = _row_tile(R2, C)
    nrb = R2 // tm

    def body(a_ref, b_ref, sb_ref):
        sb_ref[...] = (a_ref[...] + b_ref[...]).astype(BF16)

    blk = pl.BlockSpec((None, tm, C), lambda k, r: (k, r, 0))
    return pl.pallas_call(
        body, name="pair_sum", grid=(K, nrb),
        in_specs=[pl.BlockSpec((None, tm, C), lambda k, r: (k, _my_core() * nrb + r, 0)), blk],
        out_specs=blk, out_shape=jax.ShapeDtypeStruct((K, R2, C), BF16),
        compiler_params=_params(("parallel", "parallel")),
    )(grad, land)


def _chip_sum(grad, land, got):
    _, R2, C = land.shape
    tm = _row_tile(R2, C)
    nrb = R2 // tm

    def body(a_ref, b_ref, g_ref, s_ref):
        own = a_ref[...] + b_ref[...]
        s_ref[...] = ((own + g_ref[0].astype(F32)) + g_ref[1].astype(F32)) + g_ref[2].astype(F32)

    return pl.pallas_call(
        body, name="chip_sum", grid=(nrb,),
        in_specs=[pl.BlockSpec((None, tm, C), lambda r: (_my_chip(), _my_core() * nrb + r, 0)),
                  pl.BlockSpec((None, tm, C), lambda r: (_my_chip(), r, 0)),
                  pl.BlockSpec((3, tm, C), lambda r: (0, r, 0))],
        out_specs=pl.BlockSpec((tm, C), lambda r: (_my_core() * nrb + r, 0)),
        out_shape=jax.ShapeDtypeStruct((2 * R2, C), F32),
        compiler_params=_params(("parallel",)),
    )(grad, land, got)


def _adamw_math(w, g, m, v):
    m = ADAM_B1 * m + (1.0 - ADAM_B1) * g
    v = ADAM_B2 * v + (1.0 - ADAM_B2) * (g * g)
    m_hat = m / (1.0 - ADAM_B1 ** ADAM_STEP)
    v_hat = v / (1.0 - ADAM_B2 ** ADAM_STEP)
    delta = -ADAM_LR * (m_hat / (jnp.sqrt(v_hat) + ADAM_EPS) + ADAM_WD * w)
    return delta, m, v


def _adamw_stacked(grads, w, m, v, carry=None):
    L, R, C = w.shape
    tm = _row_tile(R, C)
    nrb = R // tm
    nci = len(carry.ins) if carry else 0
    nco = len(carry.outs) if carry else 0

    def body(*refs):
        g_refs = refs[:L]
        w_ref, m_ref, v_ref = refs[L:L + 3]
        ci_refs = refs[L + 3:L + 3 + nci]
        go_ref, d_ref, mo_ref, vo_ref = refs[L + 3 + nci:L + 7 + nci]
        co_refs = refs[L + 7 + nci:L + 7 + nci + nco]
        sems = refs[L + 7 + nci + nco:]
        l, r = pl.program_id(0), pl.program_id(1)
        if carry:
            @pl.when((l == 0) & (r == 0))
            def _():
                carry.start(ci_refs, co_refs, *sems)

        for ll in range(L):
            @pl.when(l == ll)
            def _(ll=ll):
                g = g_refs[ll][...]
                delta, mn, vn = _adamw_math(w_ref[...], g, m_ref[...], v_ref[...])
                go_ref[...] = g
                d_ref[...] = delta
                mo_ref[...] = mn
                vo_ref[...] = vn

        if carry:
            @pl.when((l == L - 1) & (r == nrb - 1))
            def _():
                carry.finish(ci_refs, co_refs, *sems)

    stacked = pl.BlockSpec((None, tm, C), lambda l, r: (l, r, 0))
    g_specs = [pl.BlockSpec((tm, C), lambda l, r, ll=ll: (jnp.where(l == ll, r, 0), 0)) for ll in range(L)]
    shp = jax.ShapeDtypeStruct((L, R, C), F32)
    outs = pl.pallas_call(
        body, name="adamw", grid=(L, nrb),
        in_specs=[*g_specs, stacked, stacked, stacked, *([ANY] * nci)],
        out_specs=[*([stacked] * 4), *([ANY] * nco)], out_shape=[*([shp] * 4), *(carry.outs if carry else [])],
        scratch_shapes=carry.sem_scratch() if carry else [],
        input_output_aliases={L + 3 + i: 4 + o for i, o in carry.aliases.items()} if carry else {},
        compiler_params=_params(("arbitrary", "arbitrary")),
    )(*grads, w, m, v, *(carry.ins if carry else []))
    return (outs[:4], list(outs[4:])) if carry else outs


def _adamw_small(parts, w, m, v):
    _, R, C = parts.shape
    tm = _row_tile(R, 8 * C)

    def body(p_ref, w_ref, m_ref, v_ref, go_ref, d_ref, mo_ref, vo_ref):
        g = p_ref[0]
        for k in range(1, 8):
            g = g + p_ref[k]
        delta, mn, vn = _adamw_math(w_ref[...], g, m_ref[...], v_ref[...])
        go_ref[...] = g
        d_ref[...] = delta
        mo_ref[...] = mn
        vo_ref[...] = vn

    blk = pl.BlockSpec((tm, C), lambda i: (i, 0))
    shp = jax.ShapeDtypeStruct((R, C), F32)
    return pl.pallas_call(
        body, name="adamw_small", grid=(R // tm,),
        in_specs=[pl.BlockSpec((8, tm, C), lambda i: (0, i, 0)), blk, blk, blk],
        out_specs=[blk] * 4, out_shape=[shp] * 4,
        compiler_params=_params(("parallel",)),
    )(parts, w, m, v)


def _cast_place(w, layer):
    _, R, C = w.shape
    tm = _row_tile(R, C)

    def body(w_ref, o_ref):
        o_ref[...] = w_ref[...].astype(BF16)

    return pl.pallas_call(
        body, name="cast_place", grid=(R // tm,),
        in_specs=[pl.BlockSpec((None, tm, C), lambda r: (layer, r, 0))],
        out_specs=pl.BlockSpec((None, tm, C), lambda r: (_my_chip(), r, 0)),
        out_shape=jax.ShapeDtypeStruct((N_CHIPS, R, C), BF16),
        compiler_params=_params(("parallel",)),
    )(w)


def _trig_tables(positions):
    half = ROPE_DIM // 2
    inv_freq = ROPE_THETA ** (-jnp.arange(0, ROPE_DIM, 2, dtype=F32) / ROPE_DIM)
    ang = positions.astype(F32)[:, None] * inv_freq
    cos, sin = jnp.cos(ang), jnp.sin(ang)
    S = positions.shape[0]
    zeros = lambda w: jnp.zeros((S, w), F32)
    cos_h = jnp.concatenate([cos, cos, jnp.ones((S, HEAD_DIM - ROPE_DIM), F32)], axis=1)
    sa_h = jnp.concatenate([-sin, zeros(HEAD_DIM - half)], axis=1)
    sb_h = jnp.concatenate([zeros(half), sin, zeros(HEAD_DIM - ROPE_DIM)], axis=1)
    rep = LANES // HEAD_DIM
    return [jnp.tile(t, (1, rep)) for t in (cos_h, sa_h, sb_h)]


def _row(vec):
    return vec.reshape(1, -1)


def _lane_row(vec):
    return jnp.zeros((8, LANES), F32).at[0, :vec.shape[0]].set(vec)


def _pack(pieces, rows):
    flat = jnp.concatenate([p.reshape(-1).astype(F32) for p in pieces])
    return jnp.pad(flat, (0, rows * LANES - flat.shape[0])).reshape(rows, LANES)


def kernel(x, positions, norm1_g, w_in, b_in, sinks, sgu_ln_g, sgu_ln_b, sgu_w, sgu_b, w_attn_branch, w_sgu_branch, w_out, norm2_g, w_gate_up, w_down, final_g, loss_target, m_norm1_g, m_w_in, m_b_in, m_sinks, m_sgu_ln_g, m_sgu_ln_b, m_sgu_w, m_sgu_b, m_w_attn_branch, m_w_sgu_branch, m_w_out, m_norm2_g, m_w_gate_up, m_w_down, m_final_g, v_norm1_g, v_w_in, v_b_in, v_sinks, v_sgu_ln_g, v_sgu_ln_b, v_sgu_w, v_sgu_b, v_w_attn_branch, v_w_sgu_branch, v_w_out, v_norm2_g, v_w_gate_up, v_w_down, v_final_g):
    L = norm1_g.shape[0]
    S, D = x.shape[1], x.shape[2]
    NQ = sinks.shape[1]
    A = NQ * HEAD_DIM
    KV = N_KV_HEADS * HEAD_DIM
    G = sgu_ln_g.shape[1]
    NG = sgu_w.shape[1]
    IN = b_in.shape[1]
    Fd = w_down.shape[1] * N_CHIPS
    dm = dict(D=D, A=A, KV=KV, NQ=NQ, G=G, NG=NG, IN=IN, F=Fd,
              OFF_K=A, OFF_V=A + KV, OFF_Z=A + 2 * KV, OFF_G=A + 2 * KV + 2 * G)
    assert sgu_w.shape[2] == WINDOW and G == NG * LANES and IN == dm["OFF_G"] + 2 * D

    h = x[0]
    target = loss_target[0]
    trig = _trig_tables(positions[0])
    tril = jnp.tril(jnp.ones((WINDOW, WINDOW), bool))

    big = [w_in, w_attn_branch, w_sgu_branch, w_out, w_gate_up, w_down]
    big_m = [m_w_in, m_w_attn_branch, m_w_sgu_branch, m_w_out, m_w_gate_up, m_w_down]
    big_v = [v_w_in, v_w_attn_branch, v_w_sgu_branch, v_w_out, v_w_gate_up, v_w_down]

    placed = [[_cast_place(w, l) for w in big] for l in range(L)]
    IN_, AB, SB, OUT, GU, DOWN = range(len(big))
    gathered = [[None] * len(big) for _ in range(L)]
    gathered[0][IN_] = _gather_weights([placed[0][IN_]]).run()[0]

    def fetch(layer, idx):
        return _gather_weights([placed[layer][t] for t in idx]) if layer < L else None

    def fetched(layer, idx, res):
        if layer >= L:
            return res
        main, got = res
        for t, g in zip(idx, got):
            gathered[layer][t] = g
        return main

    def weights(l):
        flat = lambda w, rows: None if w is None else w.reshape(rows, D)
        w_in_g, w_ab_g, w_sb_g, w_out_g, w_gu_g, w_down_g = gathered[l]
        return (w_in_g, w_ab_g, w_sb_g, flat(w_out_g, D), w_gu_g, flat(w_down_g, Fd))

    def small(l):
        return dict(
            g1=_row(norm1_g[l]), b_in=_row(b_in[l]), sink=_lane_row(sinks[l]),
            ln_g=_row(sgu_ln_g[l]), ln_b=_row(sgu_ln_b[l]),
            w_tril=jnp.where(tril[None], sgu_w[l], 0.0).astype(BF16),
            b_t=jnp.zeros((WINDOW, LANES), F32).at[:, :NG].set(sgu_b[l].T),
            g2=_row(norm2_g[l]))

    saved = []
    for l in range(L):
        sp = small(l)
        xn = _rms_fwd(h, sp["g1"])
        now = [AB, SB, OUT, GU] if l == 0 else [DOWN]
        proj = fetched(l, now, _in_proj(xn, gathered[l][IN_], sp["b_in"], dm, carry=fetch(l, now)))
        w_in_g, w_ab_g, w_sb_g, w_out_g = weights(l)[:4]
        y_attn, lse = _attn_fwd(proj, trig, sp["sink"], dm)
        y_sgu = _sgu_fwd(proj, sp["w_tril"], sp["b_t"], sp["ln_g"], sp["ln_b"], dm)
        a_attn = _branch_attn(y_attn, w_ab_g, dm)
        a_sgu, merged = _branch_sgu_merge(y_sgu, w_sb_g, a_attn, proj, dm)
        if l == 0:
            h_mid = fetched(l, [DOWN], _residual_matmul("out_proj", merged, w_out_g, h, carry=fetch(l, [DOWN])))
        else:
            h_mid = _residual_matmul("out_proj", merged, w_out_g, h)
        w_gu_g, w_down_g = weights(l)[4:]
        hn = _rms_fwd(h_mid, sp["g2"])
        ahead = [IN_, AB, SB, OUT]
        gu, act = fetched(l + 1, ahead, _gate_up(hn, w_gu_g, dm, carry=fetch(l + 1, ahead)))
        h_out = fetched(l + 1, [GU], _residual_matmul("down_proj", act, w_down_g, h_mid, carry=fetch(l + 1, [GU])))
        saved.append(dict(h=h, xn=xn, proj=proj, y_attn=y_attn, lse=lse, y_sgu=y_sgu, a_attn=a_attn, a_sgu=a_sgu,
                          merged=merged, h_mid=h_mid, hn=hn, gu=gu, act=act))
        h = h_out

    dh, dh_b, d_final, loss_part = _loss_head(h, _row(final_g), target)

    small_grads = [None] * L
    reduced = [[None] * len(big) for _ in range(L)]
    early, late = [GU, DOWN], [IN_, AB, SB, OUT]

    def riding(has_carry, res):
        return res if has_carry else (res, None)

    def sends_of(grads, land):
        return [_pair_sum(g, d) for g, d in zip(grads, land)]

    def finished(grads, land, got):
        return [_chip_sum(g, d, p) for g, d, p in zip(grads, land, got)]

    def file_reduced(layer, idx, fulls):
        for t, f in zip(idx, fulls):
            reduced[layer][t] = f

    late_grads = None
    for l in reversed(range(L)):
        w_in_g, w_ab_g, w_sb_g, w_out_g, w_gu_g, w_down_g = weights(l)
        sp, sv = small(l), saved[l]
        have = late_grads is not None
        dgu, land = riding(have, _down_bwd(dh_b, w_down_g, sv["gu"], dm,
                                           carry=_sibling_exchange(late_grads) if have else None))
        g_down = _wgrad_rows("wgrad_down", sv["act"], dh_b)
        dhn, got = riding(have, _gate_up_bwd(dgu, w_gu_g, dm,
                                             carry=_chip_exchange(sends_of(late_grads, land)) if have else None))
        g_gu, shared = riding(have, _wgrad_gate_up(sv["hn"], dgu, dm,
                                                   carry=_sibling_share(finished(late_grads, land, got)) if have else None))
        if have:
            file_reduced(l + 1, late, shared)
        dh_mid, dh_mid_b, d_g2 = _rms_bwd(dhn, sv["h_mid"], sp["g2"], dh)
        early_grads = [g_gu, g_down]
        (da_attn, da_sgu, dgate), land = _out_bwd(dh_mid_b, w_out_g, sv["proj"], sv["a_attn"], sv["a_sgu"], dm,
                                                   carry=_sibling_exchange(early_grads))
        sends = sends_of(early_grads, land)
        g_out = _wgrad_rows("wgrad_out", sv["merged"], dh_mid_b)
        dy_attn = _colsharded_bwd("branch_attn_bwd", da_attn, w_ab_g, BF16)
        dy_sgu = _colsharded_bwd("branch_sgu_bwd", da_sgu, w_sb_g, BF16)
        g_ab = _wgrad_cols("wgrad_attn_branch", sv["y_attn"], da_attn)
        g_sb = _wgrad_cols("wgrad_sgu_branch", sv["y_sgu"], da_sgu)
        dq, dk, dv, d_sink = _attn_bwd(sv["proj"], trig, sp["sink"], sv["y_attn"], sv["lse"], dy_attn, dm)
        dz, d_sgu_w, d_bt, d_lng, d_lnb = _sgu_bwd(sv["proj"], sp["w_tril"], sp["b_t"], sp["ln_g"], sp["ln_b"], dy_sgu, dm)
        dproj = jnp.concatenate([dq, dk, dv, dz, dgate[0], dgate[1]], axis=1)
        dxn, got = _colsharded_bwd("in_proj_bwd", dproj, w_in_g, F32, carry=_chip_exchange(sends))
        (g_in, d_bin), shared = _wgrad_cols("wgrad_in", sv["xn"], dproj, colsum=True,
                                            carry=_sibling_share(finished(early_grads, land, got)))
        file_reduced(l, early, shared)
        dh, dh_b, d_g1 = _rms_bwd(dxn, sv["h"], sp["g1"], dh_mid)
        late_grads = [g_in, g_ab, g_sb, g_out]
        small_grads[l] = dict(norm1_g=d_g1[0], b_in=d_bin[0, :, 0, :].reshape(-1), sinks=d_sink[0, :NQ],
                              sgu_ln_g=d_lng[0], sgu_ln_b=d_lnb[0], sgu_w=d_sgu_w, sgu_b=d_bt[:, :NG].T, norm2_g=d_g2[0])
    grad_x = dh[None]

    def adamw(t, carry=None):
        return _adamw_stacked([reduced[l][t] for l in range(L)], big[t], big_m[t], big_v[t], carry=carry)

    big_out = [None] * len(big)
    land = _sibling_exchange(late_grads).run()
    sends = sends_of(late_grads, land)
    big_out[GU], got_in = adamw(GU, carry=_chip_exchange(sends[:1]))
    big_out[DOWN], got_rest = adamw(DOWN, carry=_chip_exchange(sends[1:]))
    file_reduced(0, late, _sibling_share(finished(late_grads, land, got_in + got_rest)).run())
    for t in late:
        big_out[t] = adamw(t)

    names = ["norm1_g", "b_in", "sinks", "sgu_ln_g", "sgu_ln_b", "sgu_w", "sgu_b", "norm2_g"]
    small_w = [norm1_g, b_in, sinks, sgu_ln_g, sgu_ln_b, sgu_w, sgu_b, norm2_g, final_g]
    small_m = [m_norm1_g, m_b_in, m_sinks, m_sgu_ln_g, m_sgu_ln_b, m_sgu_w, m_sgu_b, m_norm2_g, m_final_g]
    small_v = [v_norm1_g, v_b_in, v_sinks, v_sgu_ln_g, v_sgu_ln_b, v_sgu_w, v_sgu_b, v_norm2_g, v_final_g]
    small_g = [jnp.stack([small_grads[l][nm] for l in range(L)]) for nm in names] + [d_final[0]]
    sizes = [w.size for w in small_w]
    total = sum(sizes) + 1
    rows = -(-total // (512 * LANES)) * 512
    loss_piece = jnp.sum(loss_part[0]).reshape(1)
    packed_g = _pack(small_g + [loss_piece], rows)
    one = jnp.ones((1,), F32)
    parts = _gather_all(packed_g)
    outs = _adamw_small(parts, _pack(small_w + [one], rows), _pack(small_m + [one], rows), _pack(small_v + [one], rows))

    def unpack(p):
        flat = p.reshape(-1)
        res, off = [], 0
        for w, n in zip(small_w, sizes):
            res.append(flat[off:off + n].reshape(w.shape))
            off += n
        return res, flat[off]

    (sg, loss), (sd, _), (smm, _), (svv, _) = [unpack(o) for o in outs]

    order = ["norm1_g", "w_in", "b_in", "sinks", "sgu_ln_g", "sgu_ln_b", "sgu_w", "sgu_b", "w_attn_branch",
             "w_sgu_branch", "w_out", "norm2_g", "w_gate_up", "w_down", "final_g"]
    big_names = ["w_in", "w_attn_branch", "w_sgu_branch", "w_out", "w_gate_up", "w_down"]
    small_names = names + ["final_g"]

    def collect(kind):
        res = []
        for nm in order:
            if nm in big_names:
                res.append(big_out[big_names.index(nm)][kind])
            else:
                res.append((sg, sd, smm, svv)[kind][small_names.index(nm)])
        return res

    return (loss, grad_x, *collect(0), *collect(1), *collect(2), *collect(3))
```

```python
import math

import jax
import jax.numpy as jnp
from jax import lax
from jax.experimental import pallas as pl
from jax.experimental.pallas import tpu as pltpu

F32 = jnp.float32
BF16 = jnp.bfloat16
MESH = pl.DeviceIdType.MESH
ANY = pl.BlockSpec(memory_space=pl.ANY)

HEAD_DIM = 64
N_KV_HEADS = 4
WINDOW = 128
ROPE_DIM = HEAD_DIM // 4
ROPE_THETA = 500000.0
EPS = 1e-5
NEG = -1e30
N_CHIPS = 4
LANES = 128
V7X_VMEM_LIMIT = 56 * 1024 * 1024

ADAM_LR = 0.001
ADAM_B1 = 0.9
ADAM_B2 = 0.999
ADAM_EPS = 1e-08
ADAM_WD = 0.01
ADAM_STEP = 10

NN = (((1,), (0,)), ((), ()))
NT = (((1,), (1,)), ((), ()))
TN = (((0,), (0,)), ((), ()))


ROW_TILES = (1024, 512, 256, 128, 64, 32, 16, 8)
BLOCK_BYTES = 2 * 1024 * 1024


def _pick(n, prefs):
    for p in prefs:
        if n % p == 0:
            return p
    raise ValueError(f"no tile for {n} among {prefs}")


def _row_tile(rows, cols, itemsize=4):
    return _pick(rows, [t for t in ROW_TILES if t * cols * itemsize <= BLOCK_BYTES or t == ROW_TILES[-1]])


def _dot(a, b, dims):
    return lax.dot_general(a, b, dims, preferred_element_type=F32)


def _sigmoid(x):
    return 1.0 / (1.0 + jnp.exp(-x))


def _gelu(x):
    return 0.5 * x * (1.0 + lax.erf(x * (1.0 / math.sqrt(2.0))))


def _gelu_grad(x):
    return 0.5 * (1.0 + lax.erf(x * (1.0 / math.sqrt(2.0)))) + x * jnp.exp(-0.5 * x * x) * (1.0 / math.sqrt(2.0 * math.pi))


def _params(sem):
    return pltpu.CompilerParams(dimension_semantics=sem, vmem_limit_bytes=V7X_VMEM_LIMIT)


def _matmul(name, lhs, rhs_list, *, dims, grid, lhs_spec, rhs_specs, acc_shape, out_shape, out_specs,
            epilogue, extra=(), extra_specs=(), carry=None, rhs_colsum=False, cols_outer=False):
    if cols_outer:
        swap = lambda s: pl.BlockSpec(s.block_shape, lambda j, i, k, f=s.index_map: f(i, j, k))
        grid = (grid[1], grid[0], grid[2])
        lhs_spec, rhs_specs = swap(lhs_spec), [swap(s) for s in rhs_specs]
        extra_specs, out_specs = [swap(s) for s in extra_specs], [swap(s) for s in out_specs]
    gk = grid[2]
    nr, ne, no = len(rhs_list), len(extra), len(out_shape)
    nci = len(carry.ins) if carry else 0
    nco = len(carry.outs) if carry else 0
    acc_shapes = [acc_shape] * nr + ([(8, acc_shape[1])] if rhs_colsum else [])
    nacc = len(acc_shapes) if gk > 1 else 0

    def body(*refs):
        a_ref = refs[0]
        b_refs = refs[1:1 + nr]
        e_refs = refs[1 + nr:1 + nr + ne]
        base = 1 + nr + ne
        ci_refs = refs[base:base + nci]
        o_refs = refs[base + nci:base + nci + no]
        co_refs = refs[base + nci + no:base + nci + no + nco]
        acc_refs = refs[base + nci + no + nco:base + nci + no + nco + nacc]
        sems = refs[base + nci + no + nco + nacc:]
        ids = [pl.program_id(d) for d in range(3)]
        if carry:
            @pl.when((ids[0] == 0) & (ids[1] == 0) & (ids[2] == 0))
            def _():
                carry.start(ci_refs, co_refs, *sems)

        a = a_ref[...]
        if gk == 1:
            n_axis = 1 - dims[0][1][0]
            for cols in _col_chunks(acc_shape[1]):
                pick = (slice(None), cols) if n_axis == 1 else (cols, slice(None))
                parts = [_dot(a, b[pick], dims) for b in b_refs]
                if rhs_colsum:
                    b0 = b_refs[0][pick]
                    parts.append(_dot(jnp.ones((8, b0.shape[0]), b0.dtype), b0, NN))
                epilogue(parts, e_refs, o_refs, cols)
        else:
            k = ids[2]

            @pl.when(k == 0)
            def _():
                for acc in acc_refs:
                    acc[...] = jnp.zeros_like(acc)

            for acc, b in zip(acc_refs, b_refs):
                acc[...] += _dot(a, b[...], dims)
            if rhs_colsum:
                b0 = b_refs[0][...]
                acc_refs[-1][...] += _dot(jnp.ones((8, b0.shape[0]), b0.dtype), b0, NN)

            @pl.when(k == gk - 1)
            def _():
                epilogue([acc[...] for acc in acc_refs], e_refs, o_refs, slice(None))

        if carry:
            @pl.when((ids[0] == grid[0] - 1) & (ids[1] == grid[1] - 1) & (ids[2] == grid[2] - 1))
            def _():
                carry.finish(ci_refs, co_refs, *sems)

    scratch = [pltpu.VMEM(s, F32) for s in acc_shapes[:nacc]]
    kwargs = {}
    if carry:
        scratch += carry.sem_scratch()
        kwargs["input_output_aliases"] = {1 + nr + ne + i: no + o for i, o in carry.aliases.items()}
    outs = pl.pallas_call(
        body, name=name, grid=grid,
        in_specs=[lhs_spec, *rhs_specs, *extra_specs, *([ANY] * nci)],
        out_specs=[*out_specs, *([ANY] * nco)],
        out_shape=[*out_shape, *(carry.outs if carry else [])], scratch_shapes=scratch,
        compiler_params=_params(("arbitrary",) * 3 if carry else ("parallel", "parallel", "arbitrary")),
        **kwargs,
    )(lhs, *rhs_list, *extra, *(carry.ins if carry else []))
    return outs


class _Comm:
    def __init__(self, name, ins, outs, aliases, n_sems, start, finish):
        self.name, self.ins, self.outs, self.aliases, self.n_sems = name, list(ins), list(outs), dict(aliases), n_sems
        self.start, self.finish = start, finish

    def sem_scratch(self):
        return [pltpu.SemaphoreType.DMA((self.n_sems,)), pltpu.SemaphoreType.DMA((self.n_sems,))]

    def beside(self, other):
        ni, no, ns = len(self.ins), len(self.outs), self.n_sems

        def split(ins, outs, send_sems, recv_sems):
            mine = (ins[:ni], outs[:no], send_sems.at[pl.ds(0, ns)], recv_sems.at[pl.ds(0, ns)])
            theirs = (ins[ni:], outs[no:], send_sems.at[pl.ds(ns, other.n_sems)], recv_sems.at[pl.ds(ns, other.n_sems)])
            return mine, theirs

        def start(*refs):
            mine, theirs = split(*refs)
            self.start(*mine)
            other.start(*theirs)

        def finish(*refs):
            mine, theirs = split(*refs)
            self.finish(*mine)
            other.finish(*theirs)

        aliases = {**self.aliases, **{ni + i: no + o for i, o in other.aliases.items()}}
        return _Comm(self.name + "+" + other.name, self.ins + other.ins, self.outs + other.outs, aliases,
                     ns + other.n_sems, start, finish)

    def run(self):
        ni = len(self.ins)

        def body(*refs):
            in_refs, out_refs, sems = refs[:ni], refs[ni:ni + len(self.outs)], refs[ni + len(self.outs):]
            self.start(in_refs, out_refs, *sems)
            self.finish(in_refs, out_refs, *sems)

        return pl.pallas_call(
            body, name=self.name, in_specs=[ANY] * ni, out_specs=[ANY] * len(self.outs), out_shape=self.outs,
            input_output_aliases=self.aliases, scratch_shapes=self.sem_scratch(),
        )(*self.ins)


MXU_CHUNK = 512


def _col_chunks(n):
    if n % LANES:
        return [slice(0, n)]
    return [slice(s, min(s + MXU_CHUNK, n)) for s in range(0, n, MXU_CHUNK)]


def _store_epilogue(dtype):
    def ep(parts, e_refs, o_refs, cols):
        o_refs[0][:, cols] = parts[0].astype(dtype)
    return ep


def _rms_fwd(h, g_row):
    S, D = h.shape
    tm = _row_tile(S, D)

    def body(h_ref, g_ref, o_ref):
        x = h_ref[...]
        r = lax.rsqrt(jnp.mean(x * x, axis=-1, keepdims=True) + EPS)
        o_ref[...] = (x * r * g_ref[...]).astype(BF16)

    return pl.pallas_call(
        body, name="rms_fwd", grid=(S // tm,),
        in_specs=[pl.BlockSpec((tm, D), lambda i: (i, 0)), pl.BlockSpec((1, D), lambda i: (0, 0))],
        out_specs=pl.BlockSpec((tm, D), lambda i: (i, 0)),
        out_shape=jax.ShapeDtypeStruct((S, D), BF16),
        compiler_params=_params(("parallel",)),
    )(h, g_row)


def _rms_bwd(dy, h, g_row, dres):
    S, D = h.shape
    tm = _row_tile(S, D)

    def body(dy_ref, h_ref, g_ref, dres_ref, dh_ref, dhb_ref, dg_ref):
        i = pl.program_id(0)
        x = h_ref[...]
        d = dy_ref[...]
        r = lax.rsqrt(jnp.mean(x * x, axis=-1, keepdims=True) + EPS)
        dg = d * g_ref[...]
        dot = jnp.mean(dg * x, axis=-1, keepdims=True)
        dh = dres_ref[...] + r * dg - x * (r * r * r) * dot
        dh_ref[...] = dh
        dhb_ref[...] = dh.astype(BF16)
        part = jnp.sum(d * x * r, axis=0, keepdims=True)

        @pl.when(i == 0)
        def _():
            dg_ref[...] = jnp.zeros_like(dg_ref)

        dg_ref[0:1, :] += part

    return pl.pallas_call(
        body, name="rms_bwd", grid=(S // tm,),
        in_specs=[pl.BlockSpec((tm, D), lambda i: (i, 0)), pl.BlockSpec((tm, D), lambda i: (i, 0)),
                  pl.BlockSpec((1, D), lambda i: (0, 0)), pl.BlockSpec((tm, D), lambda i: (i, 0))],
        out_specs=[pl.BlockSpec((tm, D), lambda i: (i, 0)), pl.BlockSpec((tm, D), lambda i: (i, 0)),
                   pl.BlockSpec((8, D), lambda i: (0, 0))],
        out_shape=[jax.ShapeDtypeStruct((S, D), F32), jax.ShapeDtypeStruct((S, D), BF16),
                   jax.ShapeDtypeStruct((8, D), F32)],
        compiler_params=_params(("arbitrary",)),
    )(dy, h, g_row, dres)


def _loss_head(h, g_row, target):
    S, D = h.shape
    tm = _row_tile(S, D)

    def body(h_ref, g_ref, t_ref, dh_ref, dhb_ref, dg_ref, loss_ref):
        i = pl.program_id(0)
        x = h_ref[...]
        g = g_ref[...]
        r = lax.rsqrt(jnp.mean(x * x, axis=-1, keepdims=True) + EPS)
        y = x * r * g
        e = y - t_ref[...]
        d = e * (1.0 / D)
        dg = d * g
        dot = jnp.mean(dg * x, axis=-1, keepdims=True)
        dh = r * dg - x * (r * r * r) * dot
        dh_ref[...] = dh
        dhb_ref[...] = dh.astype(BF16)

        @pl.when(i == 0)
        def _():
            dg_ref[...] = jnp.zeros_like(dg_ref)
            loss_ref[...] = jnp.zeros_like(loss_ref)

        dg_ref[0:1, :] += jnp.sum(d * x * r, axis=0, keepdims=True)
        loss_ref[0:1, :] += jnp.sum((0.5 / D) * e * e, axis=0, keepdims=True)

    return pl.pallas_call(
        body, name="loss_head", grid=(S // tm,),
        in_specs=[pl.BlockSpec((tm, D), lambda i: (i, 0)), pl.BlockSpec((1, D), lambda i: (0, 0)),
                  pl.BlockSpec((tm, D), lambda i: (i, 0))],
        out_specs=[pl.BlockSpec((tm, D), lambda i: (i, 0)), pl.BlockSpec((tm, D), lambda i: (i, 0)),
                   pl.BlockSpec((8, D), lambda i: (0, 0)), pl.BlockSpec((8, D), lambda i: (0, 0))],
        out_shape=[jax.ShapeDtypeStruct((S, D), F32), jax.ShapeDtypeStruct((S, D), BF16),
                   jax.ShapeDtypeStruct((8, D), F32), jax.ShapeDtypeStruct((8, D), F32)],
        compiler_params=_params(("arbitrary",)),
    )(h, g_row, target)


def _rope(t, cos, sa, sb):
    w = t.shape[-1]
    return t * cos + pltpu.roll(t, w - 8, 1) * sa + pltpu.roll(t, 8, 1) * sb


def _rope_t(g, cos, sa, sb):
    w = g.shape[-1]
    return g * cos + pltpu.roll(g * sa, 8, 1) + pltpu.roll(g * sb, w - 8, 1)


def _band_mask(n, qpk):
    qi = lax.broadcasted_iota(jnp.int32, (qpk * WINDOW, 2 * WINDOW), 0) & (WINDOW - 1)
    kj = lax.broadcasted_iota(jnp.int32, (qpk * WINDOW, 2 * WINDOW), 1)
    rel = qi + WINDOW - kj
    ok = (rel >= 0) & (rel < WINDOW)
    return ok & ((kj >= WINDOW) | (n > 0))


def _stack_heads(x, g, qpk):
    return jnp.concatenate([x[:, (g * qpk + hh) * HEAD_DIM:(g * qpk + hh + 1) * HEAD_DIM] for hh in range(qpk)], axis=0)


def _stack_cols(row, g, qpk):
    return jnp.concatenate([row[:, g * qpk + hh:g * qpk + hh + 1] for hh in range(qpk)], axis=0)


def _attn_specs(dm, nb):
    A, KV = dm["A"], dm["KV"]
    kb, vb = dm["OFF_K"] // KV, dm["OFF_V"] // KV
    cur = lambda n: jnp.minimum(n, nb - 1)
    prev = lambda n: jnp.maximum(jnp.minimum(n, nb - 1) - 1, 0)
    proj_specs = [
        pl.BlockSpec((WINDOW, A), lambda n: (cur(n), 0)),
        pl.BlockSpec((WINDOW, KV), lambda n: (prev(n), kb)),
        pl.BlockSpec((WINDOW, KV), lambda n: (cur(n), kb)),
        pl.BlockSpec((WINDOW, KV), lambda n: (prev(n), vb)),
        pl.BlockSpec((WINDOW, KV), lambda n: (cur(n), vb)),
    ]
    trig_cur = [pl.BlockSpec((WINDOW, LANES), lambda n: (cur(n), 0)) for _ in range(3)]
    trig_prev = [pl.BlockSpec((WINDOW, LANES), lambda n: (prev(n), 0)) for _ in range(3)]
    return proj_specs, trig_cur, trig_prev, cur, prev


def _attn_fwd(proj, sink_row, dm):
    S = proj.shape[0]
    A, KV, NQ = dm["A"], dm["KV"], dm["NQ"]
    qpk = NQ // N_KV_HEADS
    nb = S // WINDOW
    scale = HEAD_DIM ** -0.5
    proj_specs = _attn_specs(dm, nb)[0]

    def body(q_ref, kp_ref, kc_ref, vp_ref, vc_ref, sink_ref, y_ref, lse_ref):
        n = pl.program_id(0)
        qr = q_ref[...]
        kr = jnp.concatenate([kp_ref[...], kc_ref[...]], axis=0)
        vband = jnp.concatenate([vp_ref[...], vc_ref[...]], axis=0)
        mask = _band_mask(n, qpk)
        lane = lax.broadcasted_iota(jnp.int32, (WINDOW, LANES), 1)
        lse_all = jnp.zeros((WINDOW, LANES), F32)
        sink_rows = jnp.broadcast_to(sink_ref[0:1, :], (WINDOW, LANES))
        for g in range(N_KV_HEADS):
            k_g = kr[:, g * HEAD_DIM:(g + 1) * HEAD_DIM]
            v_g = vband[:, g * HEAD_DIM:(g + 1) * HEAD_DIM]
            q_g = _stack_heads(qr, g, qpk)
            sink = _stack_cols(sink_rows, g, qpk)
            s = jnp.where(mask, _dot(q_g, k_g, NT) * scale, NEG)
            m = jnp.maximum(jnp.max(s, axis=-1, keepdims=True), sink)
            p = jnp.exp(s - m)
            den = jnp.sum(p, axis=-1, keepdims=True) + jnp.exp(sink - m)
            o = _dot(p.astype(BF16), v_g, NN) * (1.0 / den)
            lse_g = m + jnp.log(den)
            for hh in range(qpk):
                h = g * qpk + hh
                rows = slice(hh * WINDOW, (hh + 1) * WINDOW)
                y_ref[:, h * HEAD_DIM:(h + 1) * HEAD_DIM] = o[rows].astype(BF16)
                lse_all = jnp.where(lane == h, lse_g[rows], lse_all)
        lse_ref[...] = lse_all

    return pl.pallas_call(
        body, name="attn_fwd", grid=(nb,),
        in_specs=[*proj_specs, pl.BlockSpec((8, LANES), lambda n: (0, 0))],
        out_specs=[pl.BlockSpec((WINDOW, A), lambda n: (n, 0)), pl.BlockSpec((WINDOW, LANES), lambda n: (n, 0))],
        out_shape=[jax.ShapeDtypeStruct((S, A), BF16), jax.ShapeDtypeStruct((S, LANES), F32)],
        compiler_params=_params(("parallel",)),
    )(proj, proj, proj, proj, proj, sink_row)


def _attn_bwd(proj, trig, sink_row, y, lse, dy, dm):
    S = proj.shape[0]
    A, KV, NQ = dm["A"], dm["KV"], dm["NQ"]
    qpk = NQ // N_KV_HEADS
    nb = S // WINDOW
    scale = HEAD_DIM ** -0.5
    proj_specs, trig_cur, trig_prev, cur, prev = _attn_specs(dm, nb)

    def body(q_ref, kp_ref, kc_ref, vp_ref, vc_ref, cc_ref, sac_ref, sbc_ref, cp_ref, sap_ref, sbp_ref,
             sink_ref, y_ref, lse_ref, dy_ref, dq_ref, dk_ref, dv_ref, dsink_ref,
             ck_ref, cv_ref, bk_ref, bv_ref, dqr_ref):
        n = pl.program_id(0)

        @pl.when(n == 0)
        def _():
            dsink_ref[...] = jnp.zeros_like(dsink_ref)
            ck_ref[...] = jnp.zeros_like(ck_ref)
            cv_ref[...] = jnp.zeros_like(cv_ref)

        @pl.when(n < nb)
        def _():
            tq = lambda r: jnp.tile(r[...], (1, A // LANES))
            tk = lambda rp, rc: jnp.tile(jnp.concatenate([rp[...], rc[...]], axis=0), (1, KV // LANES))
            cq, saq, sbq = tq(cc_ref), tq(sac_ref), tq(sbc_ref)
            ck, sak, sbk = tk(cp_ref, cc_ref), tk(sap_ref, sac_ref), tk(sbp_ref, sbc_ref)
            qr = q_ref[...]
            kr = jnp.concatenate([kp_ref[...], kc_ref[...]], axis=0)
            vband = jnp.concatenate([vp_ref[...], vc_ref[...]], axis=0)
            mask = _band_mask(n, qpk)
            lane = lax.broadcasted_iota(jnp.int32, (1, LANES), 1)
            lse_all = lse_ref[...]
            sink_rows = jnp.broadcast_to(sink_ref[0:1, :], (WINDOW, LANES))
            dy_all = dy_ref[...]
            y_all = y_ref[...]
            dsink = jnp.zeros((1, LANES), F32)
            for g in range(N_KV_HEADS):
                k_g = kr[:, g * HEAD_DIM:(g + 1) * HEAD_DIM]
                v_g = vband[:, g * HEAD_DIM:(g + 1) * HEAD_DIM]
                q_g = _stack_heads(qr, g, qpk)
                dy_g = _stack_heads(dy_all, g, qpk)
                y_g = _stack_heads(y_all, g, qpk)
                lse_g = _stack_cols(lse_all, g, qpk)
                s = jnp.where(mask, _dot(q_g, k_g, NT) * scale, NEG)
                p = jnp.exp(s - lse_g)
                dp = _dot(dy_g, v_g, NT)
                delta = jnp.sum(dy_g.astype(F32) * y_g.astype(F32), axis=-1, keepdims=True)
                ds = (p * (dp - delta) * scale).astype(BF16)
                dq_g = _dot(ds, k_g, NN)
                bk_ref[:, g * HEAD_DIM:(g + 1) * HEAD_DIM] = _dot(ds, q_g, TN)
                bv_ref[:, g * HEAD_DIM:(g + 1) * HEAD_DIM] = _dot(p.astype(BF16), dy_g, TN)
                sink_d = jnp.exp(_stack_cols(sink_rows, g, qpk) - lse_g) * delta
                for hh in range(qpk):
                    h = g * qpk + hh
                    rows = slice(hh * WINDOW, (hh + 1) * WINDOW)
                    dqr_ref[:, h * HEAD_DIM:(h + 1) * HEAD_DIM] = dq_g[rows]
                    dsink = dsink + jnp.where(lane == h, -jnp.sum(sink_d[rows], axis=0, keepdims=True), 0.0)
            dsink_ref[0:1, :] += dsink
            dq_ref[...] = _rope_t(dqr_ref[...], cq, saq, sbq).astype(BF16)
            dkb = _rope_t(bk_ref[...], ck, sak, sbk)
            dvb = bv_ref[...]
            dk_ref[...] = (ck_ref[...] + dkb[:WINDOW]).astype(BF16)
            dv_ref[...] = (cv_ref[...] + dvb[:WINDOW]).astype(BF16)
            ck_ref[...] = dkb[WINDOW:]
            cv_ref[...] = dvb[WINDOW:]

        @pl.when(n == nb)
        def _():
            dk_ref[...] = ck_ref[...].astype(BF16)
            dv_ref[...] = cv_ref[...].astype(BF16)

    row = lambda w: pl.BlockSpec((WINDOW, w), lambda n: (cur(n), 0))
    done = lambda w: pl.BlockSpec((WINDOW, w), lambda n: (jnp.maximum(n - 1, 0), 0))
    return pl.pallas_call(
        body, name="attn_bwd", grid=(nb + 1,),
        in_specs=[*proj_specs, *trig_cur, *trig_prev, pl.BlockSpec((8, LANES), lambda n: (0, 0)),
                  row(A), row(LANES), row(A)],
        out_specs=[row(A), done(KV), done(KV), pl.BlockSpec((8, LANES), lambda n: (0, 0))],
        out_shape=[jax.ShapeDtypeStruct((S, A), BF16), jax.ShapeDtypeStruct((S, KV), BF16),
                   jax.ShapeDtypeStruct((S, KV), BF16), jax.ShapeDtypeStruct((8, LANES), F32)],
        scratch_shapes=[pltpu.VMEM((WINDOW, KV), F32), pltpu.VMEM((WINDOW, KV), F32),
                        pltpu.VMEM((2 * WINDOW, KV), F32), pltpu.VMEM((2 * WINDOW, KV), F32),
                        pltpu.VMEM((WINDOW, A), F32)],
        compiler_params=_params(("arbitrary",)),
    )(proj, proj, proj, proj, proj, *trig, *trig, sink_row, y, lse, dy)


def _sgu_layout(dm, S):
    G = dm["G"]
    pw = math.gcd(dm["OFF_Z"], G)
    npc = G // pw
    tm = _pick(S, (256, 128))
    u_specs = [pl.BlockSpec((tm, pw), lambda i, p=p: (i, dm["OFF_Z"] // pw + p)) for p in range(npc)]
    v_specs = [pl.BlockSpec((tm, pw), lambda i, p=p: (i, (dm["OFF_Z"] + G) // pw + p)) for p in range(npc)]
    return pw, npc, tm, u_specs, v_specs


def _sgu_norm(v_refs, lg_ref, lb_ref):
    v = jnp.concatenate([_gelu(r[...].astype(F32)) for r in v_refs], axis=1)
    mu = jnp.mean(v, axis=-1, keepdims=True)
    vc = v - mu
    rstd = lax.rsqrt(jnp.mean(vc * vc, axis=-1, keepdims=True) + EPS)
    xhat = vc * rstd
    return xhat, rstd, (xhat * lg_ref[...] + lb_ref[...]).astype(BF16)


def _sgu_fwd(proj, w_tril, b_t, ln_g_row, ln_b_row, dm):
    S = proj.shape[0]
    G, NG = dm["G"], dm["NG"]
    pw, npc, tm, u_specs, v_specs = _sgu_layout(dm, S)
    nch = tm // WINDOW

    def body(*refs):
        u_refs, v_refs = refs[:npc], refs[npc:2 * npc]
        w_ref, bt_ref, lg_ref, lb_ref, y_ref = refs[2 * npc:]
        _, _, vn = _sgu_norm(v_refs, lg_ref, lb_ref)
        u = jnp.concatenate([_gelu(r[...].astype(F32)) for r in u_refs], axis=1)
        for c in range(nch):
            rows = slice(c * WINDOW, (c + 1) * WINDOW)
            for g in range(NG):
                cols = slice(g * LANES, (g + 1) * LANES)
                sv = _dot(w_ref[g], vn[rows, cols], NN) + bt_ref[:, g:g + 1]
                y_ref[rows, cols] = (u[rows, cols] * sv).astype(BF16)

    return pl.pallas_call(
        body, name="sgu_fwd", grid=(S // tm,),
        in_specs=[*u_specs, *v_specs,
                  pl.BlockSpec((NG, WINDOW, WINDOW), lambda i: (0, 0, 0)),
                  pl.BlockSpec((WINDOW, LANES), lambda i: (0, 0)),
                  pl.BlockSpec((1, G), lambda i: (0, 0)), pl.BlockSpec((1, G), lambda i: (0, 0))],
        out_specs=pl.BlockSpec((tm, G), lambda i: (i, 0)),
        out_shape=jax.ShapeDtypeStruct((S, G), BF16),
        compiler_params=_params(("parallel",)),
    )(*([proj] * (2 * npc)), w_tril, b_t, ln_g_row, ln_b_row)


def _sgu_bwd(proj, w_tril, b_t, ln_g_row, ln_b_row, dy, dm):
    S = proj.shape[0]
    G, NG = dm["G"], dm["NG"]
    pw, npc, tm, u_specs, v_specs = _sgu_layout(dm, S)
    nch = tm // WINDOW

    def body(*refs):
        u_refs, v_refs = refs[:npc], refs[npc:2 * npc]
        w_ref, bt_ref, lg_ref, lb_ref, dy_ref, dz_ref, dw_ref, dbt_ref, dlg_ref, dlb_ref, dvn_ref = refs[2 * npc:]
        i = pl.program_id(0)

        @pl.when(i == 0)
        def _():
            dw_ref[...] = jnp.zeros_like(dw_ref)
            dbt_ref[...] = jnp.zeros_like(dbt_ref)
            dlg_ref[...] = jnp.zeros_like(dlg_ref)
            dlb_ref[...] = jnp.zeros_like(dlb_ref)

        xhat, rstd, vn = _sgu_norm(v_refs, lg_ref, lb_ref)
        u_pre = jnp.concatenate([r[...].astype(F32) for r in u_refs], axis=1)
        u = _gelu(u_pre)
        dy = dy_ref[...].astype(F32)
        lane = lax.broadcasted_iota(jnp.int32, (WINDOW, LANES), 1)
        tri = lax.broadcasted_iota(jnp.int32, (WINDOW, WINDOW), 0) >= lax.broadcasted_iota(jnp.int32, (WINDOW, WINDOW), 1)
        dbt = jnp.zeros((WINDOW, LANES), F32)
        for c in range(nch):
            rows = slice(c * WINDOW, (c + 1) * WINDOW)
            for g in range(NG):
                cols = slice(g * LANES, (g + 1) * LANES)
                vn_cg = vn[rows, cols]
                sv = _dot(w_ref[g], vn_cg, NN) + bt_ref[:, g:g + 1]
                dy_cg = dy[rows, cols]
                dsv = dy_cg * u[rows, cols]
                dsv_b = dsv.astype(BF16)
                dz_ref[rows, cols] = (dy_cg * sv * _gelu_grad(u_pre[rows, cols])).astype(BF16)
                dvn_ref[rows, cols] = _dot(w_ref[g], dsv_b, TN)
                dw_ref[g] += jnp.where(tri, _dot(dsv_b, vn_cg, NT), 0.0)
                dbt = dbt + jnp.where(lane == g, jnp.sum(dsv, axis=-1, keepdims=True), 0.0)
        dbt_ref[...] += dbt
        dvn = dvn_ref[...]
        dlg_ref[0:1, :] += jnp.sum(dvn * xhat, axis=0, keepdims=True)
        dlb_ref[0:1, :] += jnp.sum(dvn, axis=0, keepdims=True)
        dxh = dvn * lg_ref[...]
        dv = rstd * (dxh - jnp.mean(dxh, axis=-1, keepdims=True) - xhat * jnp.mean(dxh * xhat, axis=-1, keepdims=True))
        v_pre = jnp.concatenate([r[...].astype(F32) for r in v_refs], axis=1)
        dz_ref[:, G:] = (dv * _gelu_grad(v_pre)).astype(BF16)

    return pl.pallas_call(
        body, name="sgu_bwd", grid=(S // tm,),
        in_specs=[*u_specs, *v_specs,
                  pl.BlockSpec((NG, WINDOW, WINDOW), lambda i: (0, 0, 0)),
                  pl.BlockSpec((WINDOW, LANES), lambda i: (0, 0)),
                  pl.BlockSpec((1, G), lambda i: (0, 0)), pl.BlockSpec((1, G), lambda i: (0, 0)),
                  pl.BlockSpec((tm, G), lambda i: (i, 0))],
        out_specs=[pl.BlockSpec((tm, 2 * G), lambda i: (i, 0)),
                   pl.BlockSpec((NG, WINDOW, WINDOW), lambda i: (0, 0, 0)),
                   pl.BlockSpec((WINDOW, LANES), lambda i: (0, 0)),
                   pl.BlockSpec((8, G), lambda i: (0, 0)), pl.BlockSpec((8, G), lambda i: (0, 0))],
        out_shape=[jax.ShapeDtypeStruct((S, 2 * G), BF16), jax.ShapeDtypeStruct((NG, WINDOW, WINDOW), F32),
                   jax.ShapeDtypeStruct((WINDOW, LANES), F32), jax.ShapeDtypeStruct((8, G), F32),
                   jax.ShapeDtypeStruct((8, G), F32)],
        scratch_shapes=[pltpu.VMEM((tm, G), F32)],
        compiler_params=_params(("arbitrary",)),
    )(*([proj] * (2 * npc)), w_tril, b_t, ln_g_row, ln_b_row, dy)


def _result(outs, n_main, carry):
    main = outs[0] if n_main == 1 else tuple(outs[:n_main])
    return (main, list(outs[n_main:])) if carry else main


def _in_proj(xn, w_in_g, b_row, trig, dm, carry=None):
    S, D = xn.shape
    IN = dm["IN"]
    cw = IN // N_CHIPS
    tm = _pick(S, (512, 256, 128))
    tn = _pick(cw, (1920, 640, 512, 256, 128))
    nbc = cw // tn
    rope_cols = dm["OFF_V"]
    rope_blocks = -(-rope_cols // tn)

    def ep(parts, e_refs, o_refs, cols):
        j = pl.program_id(0)
        val = parts[0] + e_refs[0][:, cols]
        width = cols.stop - cols.start

        @pl.when(j >= rope_blocks)
        def _():
            o_refs[0][:, cols] = val.astype(BF16)

        for jj in range(rope_blocks):
            r = min(max(rope_cols - (jj * tn + cols.start), 0), width)

            @pl.when(j == jj)
            def _(r=r):
                out = val
                if r:
                    cos, sa, sb = [jnp.tile(e[...], (1, r // LANES)) for e in e_refs[1:]]
                    roped = _rope(val[:, :r], cos, sa, sb)
                    out = roped if r == width else jnp.concatenate([roped, val[:, r:]], axis=1)
                o_refs[0][:, cols] = out.astype(BF16)

    rows = pl.BlockSpec((tm, LANES), lambda i, j, k: (i, 0))
    return _result(_matmul(
        "in_proj", xn, [w_in_g], dims=NN, grid=(S // tm, IN // tn, 1),
        lhs_spec=pl.BlockSpec((tm, D), lambda i, j, k: (i, 0)),
        rhs_specs=[pl.BlockSpec((None, D, tn), lambda i, j, k: (j // nbc, 0, j % nbc))],
        acc_shape=(tm, tn), extra=[b_row, *trig],
        extra_specs=[pl.BlockSpec((1, tn), lambda i, j, k: (0, j)), rows, rows, rows],
        out_shape=[jax.ShapeDtypeStruct((S, IN), BF16)],
        out_specs=[pl.BlockSpec((tm, tn), lambda i, j, k: (i, j))], epilogue=ep, carry=carry, cols_outer=True), 1, carry)


def _branch_attn(y_attn, w_ab_g, dm):
    S, A = y_attn.shape
    D = dm["D"]
    cw = D // N_CHIPS
    tm = _pick(S, (1024, 512, 256, 128))
    return _matmul(
        "branch_attn", y_attn, [w_ab_g], dims=NN, grid=(S // tm, N_CHIPS, 1),
        lhs_spec=pl.BlockSpec((tm, A), lambda i, j, k: (i, 0)),
        rhs_specs=[pl.BlockSpec((None, A, cw), lambda i, j, k: (j, 0, 0))],
        acc_shape=(tm, cw), out_shape=[jax.ShapeDtypeStruct((S, D), BF16)],
        out_specs=[pl.BlockSpec((tm, cw), lambda i, j, k: (i, j))], epilogue=_store_epilogue(BF16))[0]


def _branch_sgu_merge(y_sgu, w_sb_g, a_attn, proj, dm):
    S, G = y_sgu.shape
    D, OFF_G = dm["D"], dm["OFF_G"]
    cw = D // N_CHIPS
    tm = _pick(S, (1024, 512, 256, 128))

    def ep(parts, e_refs, o_refs, cols):
        a_sgu = parts[0].astype(BF16)
        ga = _sigmoid(e_refs[1][:, cols].astype(F32))
        gs = _sigmoid(e_refs[2][:, cols].astype(F32))
        o_refs[0][:, cols] = a_sgu
        o_refs[1][:, cols] = (ga * e_refs[0][:, cols].astype(F32) + gs * a_sgu.astype(F32)).astype(BF16)

    blk = pl.BlockSpec((tm, cw), lambda i, j, k: (i, j))
    return _matmul(
        "branch_sgu_merge", y_sgu, [w_sb_g], dims=NN, grid=(S // tm, N_CHIPS, 1),
        lhs_spec=pl.BlockSpec((tm, G), lambda i, j, k: (i, 0)),
        rhs_specs=[pl.BlockSpec((None, G, cw), lambda i, j, k: (j, 0, 0))],
        acc_shape=(tm, cw), extra=[a_attn, proj, proj],
        extra_specs=[blk, pl.BlockSpec((tm, cw), lambda i, j, k: (i, OFF_G // cw + j)),
                     pl.BlockSpec((tm, cw), lambda i, j, k: (i, (OFF_G + D) // cw + j))],
        out_shape=[jax.ShapeDtypeStruct((S, D), BF16), jax.ShapeDtypeStruct((S, D), BF16)],
        out_specs=[blk, blk], epilogue=ep)


def _residual_matmul(name, a, w_g, h, carry=None):
    S, K = a.shape
    D = w_g.shape[1]
    tm = _pick(S, (1024, 512, 256, 128))
    tn = _pick(D, (512, 256, 128))

    def ep(parts, e_refs, o_refs, cols):
        o_refs[0][:, cols] = e_refs[0][:, cols] + parts[0]

    blk = pl.BlockSpec((tm, tn), lambda i, j, k: (i, j))
    return _result(_matmul(
        name, a, [w_g], dims=NN, grid=(S // tm, D // tn, 1),
        lhs_spec=pl.BlockSpec((tm, K), lambda i, j, k: (i, 0)),
        rhs_specs=[pl.BlockSpec((K, tn), lambda i, j, k: (0, j))],
        acc_shape=(tm, tn), extra=[h], extra_specs=[blk],
        out_shape=[jax.ShapeDtypeStruct((S, D), F32)], out_specs=[blk], epilogue=ep, carry=carry), 1, carry)


def _gate_up(hn, w_gu_g, dm, carry=None):
    S, D = hn.shape
    Fd = dm["F"]
    cw = 2 * Fd // N_CHIPS
    tm = _pick(S, (512, 256, 128))
    tn = _pick(cw, (1408, 512, 384, 256, 128))
    nbc = cw // tn
    half = N_CHIPS // 2

    def ep(parts, e_refs, o_refs, cols):
        gate, up = parts[0].astype(BF16), parts[1].astype(BF16)
        o_refs[0][0, :, cols] = gate
        o_refs[0][1, :, cols] = up
        g32 = gate.astype(F32)
        o_refs[1][:, cols] = (g32 * _sigmoid(g32) * up.astype(F32)).astype(BF16)

    return _result(_matmul(
        "gate_up", hn, [w_gu_g, w_gu_g], dims=NN, grid=(S // tm, Fd // tn, 1),
        lhs_spec=pl.BlockSpec((tm, D), lambda i, j, k: (i, 0)),
        rhs_specs=[pl.BlockSpec((None, D, tn), lambda i, j, k: (j // nbc, 0, j % nbc)),
                   pl.BlockSpec((None, D, tn), lambda i, j, k: (half + j // nbc, 0, j % nbc))],
        acc_shape=(tm, tn),
        out_shape=[jax.ShapeDtypeStruct((2, S, Fd), BF16), jax.ShapeDtypeStruct((S, Fd), BF16)],
        out_specs=[pl.BlockSpec((2, tm, tn), lambda i, j, k: (0, i, j)), pl.BlockSpec((tm, tn), lambda i, j, k: (i, j))],
        epilogue=ep, carry=carry, cols_outer=True), 2, carry)


def _down_bwd(dh_b, w_down_g, gu, dm, carry=None):
    S, D = dh_b.shape
    Fd = dm["F"]
    tm = _pick(S, (1024, 512, 256, 128))
    tn = _pick(Fd, (512, 256, 128))

    def ep(parts, e_refs, o_refs, cols):
        gate = e_refs[0][0, :, cols].astype(F32)
        up = e_refs[0][1, :, cols].astype(F32)
        s = _sigmoid(gate)
        dact = parts[0]
        o_refs[0][0, :, cols] = (dact * up * s * (1.0 + gate * (1.0 - s))).astype(BF16)
        o_refs[0][1, :, cols] = (dact * gate * s).astype(BF16)

    blk = pl.BlockSpec((2, tm, tn), lambda i, j, k: (0, i, j))
    return _result(_matmul(
        "down_bwd", dh_b, [w_down_g], dims=NT, grid=(S // tm, Fd // tn, 1),
        lhs_spec=pl.BlockSpec((tm, D), lambda i, j, k: (i, 0)),
        rhs_specs=[pl.BlockSpec((tn, D), lambda i, j, k: (j, 0))],
        acc_shape=(tm, tn), extra=[gu], extra_specs=[blk],
        out_shape=[jax.ShapeDtypeStruct((2, S, Fd), BF16)], out_specs=[blk], epilogue=ep, carry=carry), 1, carry)


def _gate_up_bwd(dgu, w_gu_g, dm, carry=None):
    S = dgu.shape[1]
    D, Fd = dm["D"], dm["F"]
    cw = 2 * Fd // N_CHIPS
    half = N_CHIPS // 2
    tm = _pick(S, (1024, 512, 256, 128))
    tn = _pick(D, (1024, 512, 256, 128))
    return _result(_matmul(
        "gate_up_bwd", dgu, [w_gu_g], dims=NT, grid=(S // tm, D // tn, N_CHIPS),
        lhs_spec=pl.BlockSpec((None, tm, cw), lambda i, j, k: (k // half, i, k % half)),
        rhs_specs=[pl.BlockSpec((None, tn, cw), lambda i, j, k: (k, j, 0))],
        acc_shape=(tm, tn), out_shape=[jax.ShapeDtypeStruct((S, D), F32)],
        out_specs=[pl.BlockSpec((tm, tn), lambda i, j, k: (i, j))], epilogue=_store_epilogue(F32), carry=carry), 1, carry)


def _out_bwd(dh_b, w_out_g, proj, a_attn, a_sgu, dm, carry=None):
    S, D = dh_b.shape
    OFF_G = dm["OFF_G"]
    tm = _pick(S, (1024, 512, 256, 128))
    tn = D // N_CHIPS

    def ep(parts, e_refs, o_refs, cols):
        dm_ = parts[0]
        ga = _sigmoid(e_refs[0][:, cols].astype(F32))
        gs = _sigmoid(e_refs[1][:, cols].astype(F32))
        o_refs[0][:, cols] = (dm_ * ga).astype(BF16)
        o_refs[1][:, cols] = (dm_ * gs).astype(BF16)
        o_refs[2][0, :, cols] = (dm_ * e_refs[2][:, cols].astype(F32) * ga * (1.0 - ga)).astype(BF16)
        o_refs[2][1, :, cols] = (dm_ * e_refs[3][:, cols].astype(F32) * gs * (1.0 - gs)).astype(BF16)

    blk = pl.BlockSpec((tm, tn), lambda i, j, k: (i, j))
    return _result(_matmul(
        "out_bwd", dh_b, [w_out_g], dims=NT, grid=(S // tm, D // tn, 1),
        lhs_spec=pl.BlockSpec((tm, D), lambda i, j, k: (i, 0)),
        rhs_specs=[pl.BlockSpec((tn, D), lambda i, j, k: (j, 0))],
        acc_shape=(tm, tn), extra=[proj, proj, a_attn, a_sgu],
        extra_specs=[pl.BlockSpec((tm, tn), lambda i, j, k: (i, OFF_G // tn + j)),
                     pl.BlockSpec((tm, tn), lambda i, j, k: (i, (OFF_G + D) // tn + j)), blk, blk],
        out_shape=[jax.ShapeDtypeStruct((S, D), BF16), jax.ShapeDtypeStruct((S, D), BF16),
                   jax.ShapeDtypeStruct((2, S, D), BF16)],
        out_specs=[blk, blk, pl.BlockSpec((2, tm, tn), lambda i, j, k: (0, i, j))], epilogue=ep, carry=carry), 3, carry)


def _colsharded_bwd(name, dy, w_g, out_dtype, carry=None):
    S = dy.shape[0]
    _, K, cw = w_g.shape
    tm = _pick(S, (1024, 512, 256, 128))
    tn = _pick(K, (1024, 512, 256, 128))
    return _result(_matmul(
        name, dy, [w_g], dims=NT, grid=(S // tm, K // tn, N_CHIPS),
        lhs_spec=pl.BlockSpec((tm, cw), lambda i, j, k: (i, k)),
        rhs_specs=[pl.BlockSpec((None, tn, cw), lambda i, j, k: (k, j, 0))],
        acc_shape=(tm, tn), out_shape=[jax.ShapeDtypeStruct((S, K), out_dtype)],
        out_specs=[pl.BlockSpec((tm, tn), lambda i, j, k: (i, j))], epilogue=_store_epilogue(out_dtype),
        carry=carry), 1, carry)


def _wgrad_cols(name, x, dy, carry=None, colsum=False):
    S, R = x.shape
    C = dy.shape[1]
    cw = C // N_CHIPS
    tm = _pick(R, (1024, 512, 256, 128))
    tk = _pick(S, (1024, 512, 256, 128) if cw >= 1024 else (2048, 1024, 512, 256, 128))

    def ep(parts, e_refs, o_refs, cols):
        for o, p in zip(o_refs, parts):
            o[:, cols] = p

    out_shape = [jax.ShapeDtypeStruct((N_CHIPS, R, cw), F32)]
    out_specs = [pl.BlockSpec((None, tm, cw), lambda i, j, k: (j, i, 0))]
    if colsum:
        out_shape.append(jax.ShapeDtypeStruct((R // tm, N_CHIPS, 8, cw), F32))
        out_specs.append(pl.BlockSpec((None, None, 8, cw), lambda i, j, k: (i, j, 0, 0)))
    return _result(_matmul(
        name, x, [dy], dims=TN, grid=(R // tm, N_CHIPS, S // tk),
        lhs_spec=pl.BlockSpec((tk, tm), lambda i, j, k: (k, i)),
        rhs_specs=[pl.BlockSpec((tk, cw), lambda i, j, k: (k, j))],
        acc_shape=(tm, cw), out_shape=out_shape, out_specs=out_specs, epilogue=ep,
        carry=carry, rhs_colsum=colsum), len(out_shape), carry)


def _wgrad_gate_up(hn, dgu, dm, carry=None):
    S, D = hn.shape
    Fd = dm["F"]
    cw = 2 * Fd // N_CHIPS
    half = N_CHIPS // 2
    tm = _pick(D, (1024, 512, 256, 128))
    tk = _pick(S, (1024, 512, 256, 128))
    tn = _pick(cw, (1408, 512, 384, 256, 128))
    nbc = cw // tn
    return _result(_matmul(
        "wgrad_gate_up", hn, [dgu], dims=TN, grid=(D // tm, 2 * Fd // tn, S // tk),
        lhs_spec=pl.BlockSpec((tk, tm), lambda i, j, k: (k, i)),
        rhs_specs=[pl.BlockSpec((None, tk, tn), lambda i, j, k: (j // (half * nbc), k, j % (half * nbc)))],
        acc_shape=(tm, tn), out_shape=[jax.ShapeDtypeStruct((N_CHIPS, D, cw), F32)],
        out_specs=[pl.BlockSpec((None, tm, tn), lambda i, j, k: (j // nbc, i, j % nbc))], epilogue=_store_epilogue(F32),
        carry=carry), 1, carry)


def _wgrad_rows(name, x, dy):
    S, R = x.shape
    C = dy.shape[1]
    rw = R // N_CHIPS
    tn = _pick(C, (1024, 512, 256, 128))
    tk = _pick(S, (1024, 512, 256, 128))
    return _matmul(
        name, x, [dy], dims=TN, grid=(N_CHIPS, C // tn, S // tk),
        lhs_spec=pl.BlockSpec((tk, rw), lambda i, j, k: (k, i)),
        rhs_specs=[pl.BlockSpec((tk, tn), lambda i, j, k: (k, j))],
        acc_shape=(rw, tn), out_shape=[jax.ShapeDtypeStruct((N_CHIPS, rw, C), F32)],
        out_specs=[pl.BlockSpec((None, rw, tn), lambda i, j, k: (i, 0, j))], epilogue=_store_epilogue(F32))[0]


def _place():
    x, y, c = lax.axis_index("x"), lax.axis_index("y"), lax.axis_index("c")
    others = [(1 - x, y), (x, 1 - y), (1 - x, 1 - y)]
    return x, y, c, others


def _chip_index(chip):
    return 2 * chip[0] + chip[1]


def _gather_weights(bufs):
    n = len(bufs)

    def copies(src, out, send_sems, recv_sems):
        x, y, c, others = _place()

        def half(ref, chip_idx, hc):
            r2 = ref.shape[1] // 2
            return ref.at[chip_idx, pl.ds(hc * r2, r2), :]

        def copy(t, k, chip, hc, to):
            return pltpu.make_async_remote_copy(
                src_ref=half(src[t], _chip_index(chip), hc), dst_ref=half(out[t], _chip_index(chip), hc),
                send_sem=send_sems.at[6 * t + k], recv_sem=recv_sems.at[6 * t + k],
                device_id=to, device_id_type=MESH)

        me, sibling = (x, y, c), (x, y, 1 - c)
        pairs = [(t, j, chip) for t in range(n) for j, chip in enumerate(others)]
        sent = [copy(t, j, (x, y), c, (*chip, c)) for t, j, chip in pairs]
        landed = [copy(t, j, chip, c, me) for t, j, chip in pairs]
        passed = [copy(t, 3 + j, chip, c, sibling) for t, j, chip in pairs]
        handed = [copy(t, 3 + j, chip, 1 - c, me) for t, j, chip in pairs]
        return sent, landed, passed, handed

    def start(src, out, send_sems, recv_sems):
        for cp in copies(src, out, send_sems, recv_sems)[0]:
            cp.start()

    def finish(src, out, send_sems, recv_sems):
        sent, landed, passed, handed = copies(src, out, send_sems, recv_sems)
        for arrival, forward in zip(landed, passed):
            arrival.wait_recv()
            forward.start()
        for cp in handed:
            cp.wait_recv()
        for cp in sent + passed:
            cp.wait_send()

    return _Comm("gather_weights", bufs, [jax.ShapeDtypeStruct(b.shape, BF16) for b in bufs],
                 {t: t for t in range(n)}, 6 * n, start, finish)


def _sibling_exchange(grads):
    n = len(grads)
    shapes = [g.shape for g in grads]

    def copies(src, land, send_sems, recv_sems):
        x, y, c, _ = _place()
        res = []
        for t in range(n):
            r2 = shapes[t][1] // 2
            res.append(pltpu.make_async_remote_copy(
                src_ref=src[t].at[:, pl.ds((1 - c) * r2, r2), :], dst_ref=land[t],
                send_sem=send_sems.at[t], recv_sem=recv_sems.at[t], device_id=(x, y, 1 - c), device_id_type=MESH))
        return res

    def start(*refs):
        for cp in copies(*refs):
            cp.start()

    def finish(*refs):
        remote = copies(*refs)
        for cp in remote:
            cp.wait_recv()
        for cp in remote:
            cp.wait_send()

    return _Comm("sibling_exchange", grads, [jax.ShapeDtypeStruct((s[0], s[1] // 2, s[2]), F32) for s in shapes],
                 {}, n, start, finish)


def _chip_exchange(sends):
    n = len(sends)
    shapes = [s.shape for s in sends]

    def copies(snd, got, send_sems, recv_sems):
        x, y, c, others = _place()
        return [pltpu.make_async_remote_copy(
            src_ref=snd[t].at[_chip_index(chip)], dst_ref=got[t].at[j],
            send_sem=send_sems.at[3 * t + j], recv_sem=recv_sems.at[3 * t + j],
            device_id=(*chip, c), device_id_type=MESH) for t in range(n) for j, chip in enumerate(others)]

    def start(*refs):
        for cp in copies(*refs):
            cp.start()

    def finish(*refs):
        remote = copies(*refs)
        for cp in remote:
            cp.wait_recv()
        for cp in remote:
            cp.wait_send()

    return _Comm("chip_exchange", sends, [jax.ShapeDtypeStruct((3, s[1], s[2]), BF16) for s in shapes],
                 {}, 3 * n, start, finish)


def _sibling_share(fulls):
    n = len(fulls)
    shapes = [f.shape for f in fulls]

    def copies(src, out, send_sems, recv_sems, mine):
        x, y, c, _ = _place()
        hc = c if mine else 1 - c
        res = []
        for t in range(n):
            r2 = shapes[t][0] // 2
            res.append(pltpu.make_async_remote_copy(
                src_ref=src[t].at[pl.ds(hc * r2, r2), :], dst_ref=out[t].at[pl.ds(hc * r2, r2), :],
                send_sem=send_sems.at[t], recv_sem=recv_sems.at[t], device_id=(x, y, 1 - c), device_id_type=MESH))
        return res

    def start(*refs):
        for cp in copies(*refs, mine=True):
            cp.start()

    def finish(*refs):
        for cp in copies(*refs, mine=False):
            cp.wait_recv()
        for cp in copies(*refs, mine=True):
            cp.wait_send()

    return _Comm("sibling_share", fulls, [jax.ShapeDtypeStruct(s, F32) for s in shapes],
                 {t: t for t in range(n)}, n, start, finish)


def _gather_all(v):
    R, C = v.shape

    def body(v_ref, out_ref, send_sems, recv_sems, local_sem):
        x, y, c, others = _place()
        me, sibling = (x, y, c), (x, y, 1 - c)

        def rows(px, py, pc):
            return out_ref.at[4 * px + 2 * py + pc]

        def copy(k, block, to, src=None):
            return pltpu.make_async_remote_copy(
                src_ref=rows(*block) if src is None else src, dst_ref=rows(*block),
                send_sem=send_sems.at[k], recv_sem=recv_sems.at[k], device_id=to, device_id_type=MESH)

        mine = pltpu.make_async_copy(v_ref, rows(*me), local_sem)
        mine.start()
        first = [copy(0, me, sibling, src=v_ref)]
        first += [copy(1 + j, me, (*chip, c), src=v_ref) for j, chip in enumerate(others)]
        for cp in first:
            cp.start()
        passed = [copy(4 + j, (*chip, c), sibling) for j, chip in enumerate(others)]
        for j, chip in enumerate(others):
            copy(1 + j, (*chip, c), me).wait_recv()
            passed[j].start()
        copy(0, sibling, me).wait_recv()
        for j, chip in enumerate(others):
            copy(4 + j, (*chip, 1 - c), me).wait_recv()
        for cp in first + passed:
            cp.wait_send()
        mine.wait()

    return pl.pallas_call(
        body, name="gather_all", in_specs=[ANY], out_specs=ANY,
        out_shape=jax.ShapeDtypeStruct((8, R, C), F32),
        scratch_shapes=[pltpu.SemaphoreType.DMA((7,)), pltpu.SemaphoreType.DMA((7,)), pltpu.SemaphoreType.DMA],
    )(v)


def _my_chip():
    return 2 * lax.axis_index("x") + lax.axis_index("y")


def _my_core():
    return lax.axis_index("c")


def _pair_sum(grad, land):
    K, R2, C = land.shape
    tm = _row_tile(R2, C)
    nrb = R2 // tm

    def body(a_ref, b_ref, sb_ref):
        sb_ref[...] = (a_ref[...] + b_ref[...]).astype(BF16)

    blk = pl.BlockSpec((None, tm, C), lambda k, r: (k, r, 0))
    return pl.pallas_call(
        body, name="pair_sum", grid=(K, nrb),
        in_specs=[pl.BlockSpec((None, tm, C), lambda k, r: (k, _my_core() * nrb + r, 0)), blk],
        out_specs=blk, out_shape=jax.ShapeDtypeStruct((K, R2, C), BF16),
        compiler_params=_params(("parallel", "parallel")),
    )(grad, land)


def _chip_sum(grad, land, got):
    _, R2, C = land.shape
    tm = _row_tile(R2, C)
    nrb = R2 // tm

    def body(a_ref, b_ref, g_ref, s_ref):
        own = a_ref[...] + b_ref[...]
        s_ref[...] = ((own + g_ref[0].astype(F32)) + g_ref[1].astype(F32)) + g_ref[2].astype(F32)

    return pl.pallas_call(
        body, name="chip_sum", grid=(nrb,),
        in_specs=[pl.BlockSpec((None, tm, C), lambda r: (_my_chip(), _my_core() * nrb + r, 0)),
                  pl.BlockSpec((None, tm, C), lambda r: (_my_chip(), r, 0)),
                  pl.BlockSpec((3, tm, C), lambda r: (0, r, 0))],
        out_specs=pl.BlockSpec((tm, C), lambda r: (_my_core() * nrb + r, 0)),
        out_shape=jax.ShapeDtypeStruct((2 * R2, C), F32),
        compiler_params=_params(("parallel",)),
    )(grad, land, got)


def _adamw_math(w, g, m, v):
    m = ADAM_B1 * m + (1.0 - ADAM_B1) * g
    v = ADAM_B2 * v + (1.0 - ADAM_B2) * (g * g)
    m_hat = m / (1.0 - ADAM_B1 ** ADAM_STEP)
    v_hat = v / (1.0 - ADAM_B2 ** ADAM_STEP)
    delta = -ADAM_LR * (m_hat / (jnp.sqrt(v_hat) + ADAM_EPS) + ADAM_WD * w)
    return delta, m, v


def _adamw_stacked(grads, w, m, v, carry=None):
    L, R, C = w.shape
    tm = _row_tile(R, C)
    nrb = R // tm
    nci = len(carry.ins) if carry else 0
    nco = len(carry.outs) if carry else 0

    def body(*refs):
        g_refs = refs[:L]
        w_ref, m_ref, v_ref = refs[L:L + 3]
        ci_refs = refs[L + 3:L + 3 + nci]
        go_ref, d_ref, mo_ref, vo_ref = refs[L + 3 + nci:L + 7 + nci]
        co_refs = refs[L + 7 + nci:L + 7 + nci + nco]
        sems = refs[L + 7 + nci + nco:]
        l, r = pl.program_id(0), pl.program_id(1)
        if carry:
            @pl.when((l == 0) & (r == 0))
            def _():
                carry.start(ci_refs, co_refs, *sems)

        for ll in range(L):
            @pl.when(l == ll)
            def _(ll=ll):
                g = g_refs[ll][...]
                delta, mn, vn = _adamw_math(w_ref[...], g, m_ref[...], v_ref[...])
                go_ref[...] = g
                d_ref[...] = delta
                mo_ref[...] = mn
                vo_ref[...] = vn

        if carry:
            @pl.when((l == L - 1) & (r == nrb - 1))
            def _():
                carry.finish(ci_refs, co_refs, *sems)

    stacked = pl.BlockSpec((None, tm, C), lambda l, r: (l, r, 0))
    g_specs = [pl.BlockSpec((tm, C), lambda l, r, ll=ll: (jnp.where(l == ll, r, 0), 0)) for ll in range(L)]
    shp = jax.ShapeDtypeStruct((L, R, C), F32)
    outs = pl.pallas_call(
        body, name="adamw", grid=(L, nrb),
        in_specs=[*g_specs, stacked, stacked, stacked, *([ANY] * nci)],
        out_specs=[*([stacked] * 4), *([ANY] * nco)], out_shape=[*([shp] * 4), *(carry.outs if carry else [])],
        scratch_shapes=carry.sem_scratch() if carry else [],
        input_output_aliases={L + 3 + i: 4 + o for i, o in carry.aliases.items()} if carry else {},
        compiler_params=_params(("arbitrary", "arbitrary")),
    )(*grads, w, m, v, *(carry.ins if carry else []))
    return (outs[:4], list(outs[4:])) if carry else outs


def _adamw_small(parts, w, m, v):
    _, R, C = parts.shape
    tm = _row_tile(R, 8 * C)

    def body(p_ref, w_ref, m_ref, v_ref, go_ref, d_ref, mo_ref, vo_ref):
        g = p_ref[0]
        for k in range(1, 8):
            g = g + p_ref[k]
        delta, mn, vn = _adamw_math(w_ref[...], g, m_ref[...], v_ref[...])
        go_ref[...] = g
        d_ref[...] = delta
        mo_ref[...] = mn
        vo_ref[...] = vn

    blk = pl.BlockSpec((tm, C), lambda i: (i, 0))
    shp = jax.ShapeDtypeStruct((R, C), F32)
    return pl.pallas_call(
        body, name="adamw_small", grid=(R // tm,),
        in_specs=[pl.BlockSpec((8, tm, C), lambda i: (0, i, 0)), blk, blk, blk],
        out_specs=[blk] * 4, out_shape=[shp] * 4,
        compiler_params=_params(("parallel",)),
    )(parts, w, m, v)


def _cast_place(w, layer):
    _, R, C = w.shape
    tm = _row_tile(R, C)

    def body(w_ref, o_ref):
        o_ref[...] = w_ref[...].astype(BF16)

    return pl.pallas_call(
        body, name="cast_place", grid=(R // tm,),
        in_specs=[pl.BlockSpec((None, tm, C), lambda r: (layer, r, 0))],
        out_specs=pl.BlockSpec((None, tm, C), lambda r: (_my_chip(), r, 0)),
        out_shape=jax.ShapeDtypeStruct((N_CHIPS, R, C), BF16),
        compiler_params=_params(("parallel",)),
    )(w)


def _trig_tables(positions):
    half = ROPE_DIM // 2
    inv_freq = ROPE_THETA ** (-jnp.arange(0, ROPE_DIM, 2, dtype=F32) / ROPE_DIM)
    ang = positions.astype(F32)[:, None] * inv_freq
    cos, sin = jnp.cos(ang), jnp.sin(ang)
    S = positions.shape[0]
    zeros = lambda w: jnp.zeros((S, w), F32)
    cos_h = jnp.concatenate([cos, cos, jnp.ones((S, HEAD_DIM - ROPE_DIM), F32)], axis=1)
    sa_h = jnp.concatenate([-sin, zeros(HEAD_DIM - half)], axis=1)
    sb_h = jnp.concatenate([zeros(half), sin, zeros(HEAD_DIM - ROPE_DIM)], axis=1)
    rep = LANES // HEAD_DIM
    return [jnp.tile(t, (1, rep)) for t in (cos_h, sa_h, sb_h)]


def _row(vec):
    return vec.reshape(1, -1)


def _lane_row(vec):
    return jnp.zeros((8, LANES), F32).at[0, :vec.shape[0]].set(vec)


def _pack(pieces, rows):
    flat = jnp.concatenate([p.reshape(-1).astype(F32) for p in pieces])
    return jnp.pad(flat, (0, rows * LANES - flat.shape[0])).reshape(rows, LANES)


def kernel(x, positions, norm1_g, w_in, b_in, sinks, sgu_ln_g, sgu_ln_b, sgu_w, sgu_b, w_attn_branch, w_sgu_branch, w_out, norm2_g, w_gate_up, w_down, final_g, loss_target, m_norm1_g, m_w_in, m_b_in, m_sinks, m_sgu_ln_g, m_sgu_ln_b, m_sgu_w, m_sgu_b, m_w_attn_branch, m_w_sgu_branch, m_w_out, m_norm2_g, m_w_gate_up, m_w_down, m_final_g, v_norm1_g, v_w_in, v_b_in, v_sinks, v_sgu_ln_g, v_sgu_ln_b, v_sgu_w, v_sgu_b, v_w_attn_branch, v_w_sgu_branch, v_w_out, v_norm2_g, v_w_gate_up, v_w_down, v_final_g):
    L = norm1_g.shape[0]
    S, D = x.shape[1], x.shape[2]
    NQ = sinks.shape[1]
    A = NQ * HEAD_DIM
    KV = N_KV_HEADS * HEAD_DIM
    G = sgu_ln_g.shape[1]
    NG = sgu_w.shape[1]
    IN = b_in.shape[1]
    Fd = w_down.shape[1] * N_CHIPS
    dm = dict(D=D, A=A, KV=KV, NQ=NQ, G=G, NG=NG, IN=IN, F=Fd,
              OFF_K=A, OFF_V=A + KV, OFF_Z=A + 2 * KV, OFF_G=A + 2 * KV + 2 * G)
    assert sgu_w.shape[2] == WINDOW and G == NG * LANES and IN == dm["OFF_G"] + 2 * D

    h = x[0]
    target = loss_target[0]
    trig = _trig_tables(positions[0])
    tril = jnp.tril(jnp.ones((WINDOW, WINDOW), bool))

    big = [w_in, w_attn_branch, w_sgu_branch, w_out, w_gate_up, w_down]
    big_m = [m_w_in, m_w_attn_branch, m_w_sgu_branch, m_w_out, m_w_gate_up, m_w_down]
    big_v = [v_w_in, v_w_attn_branch, v_w_sgu_branch, v_w_out, v_w_gate_up, v_w_down]

    placed = [[_cast_place(w, l) for w in big] for l in range(L)]
    IN_, AB, SB, OUT, GU, DOWN = range(len(big))
    gathered = [[None] * len(big) for _ in range(L)]
    gathered[0][IN_] = _gather_weights([placed[0][IN_]]).run()[0]

    def fetch(layer, idx):
        return _gather_weights([placed[layer][t] for t in idx]) if layer < L else None

    def fetched(layer, idx, res):
        if layer >= L:
            return res
        main, got = res
        for t, g in zip(idx, got):
            gathered[layer][t] = g
        return main

    def weights(l):
        flat = lambda w, rows: None if w is None else w.reshape(rows, D)
        w_in_g, w_ab_g, w_sb_g, w_out_g, w_gu_g, w_down_g = gathered[l]
        return (w_in_g, w_ab_g, w_sb_g, flat(w_out_g, D), w_gu_g, flat(w_down_g, Fd))

    def small(l):
        return dict(
            g1=_row(norm1_g[l]), b_in=_row(b_in[l]), sink=_lane_row(sinks[l]),
            ln_g=_row(sgu_ln_g[l]), ln_b=_row(sgu_ln_b[l]),
            w_tril=jnp.where(tril[None], sgu_w[l], 0.0).astype(BF16),
            b_t=jnp.zeros((WINDOW, LANES), F32).at[:, :NG].set(sgu_b[l].T),
            g2=_row(norm2_g[l]))

    saved = []
    for l in range(L):
        sp = small(l)
        xn = _rms_fwd(h, sp["g1"])
        now = [AB, SB, OUT, GU] if l == 0 else [DOWN]
        proj = fetched(l, now, _in_proj(xn, gathered[l][IN_], sp["b_in"], trig, dm, carry=fetch(l, now)))
        w_in_g, w_ab_g, w_sb_g, w_out_g = weights(l)[:4]
        y_attn, lse = _attn_fwd(proj, sp["sink"], dm)
        y_sgu = _sgu_fwd(proj, sp["w_tril"], sp["b_t"], sp["ln_g"], sp["ln_b"], dm)
        a_attn = _branch_attn(y_attn, w_ab_g, dm)
        a_sgu, merged = _branch_sgu_merge(y_sgu, w_sb_g, a_attn, proj, dm)
        if l == 0:
            h_mid = fetched(l, [DOWN], _residual_matmul("out_proj", merged, w_out_g, h, carry=fetch(l, [DOWN])))
        else:
            h_mid = _residual_matmul("out_proj", merged, w_out_g, h)
        w_gu_g, w_down_g = weights(l)[4:]
        hn = _rms_fwd(h_mid, sp["g2"])
        ahead = [IN_, AB, SB, OUT]
        gu, act = fetched(l + 1, ahead, _gate_up(hn, w_gu_g, dm, carry=fetch(l + 1, ahead)))
        h_out = fetched(l + 1, [GU], _residual_matmul("down_proj", act, w_down_g, h_mid, carry=fetch(l + 1, [GU])))
        saved.append(dict(h=h, xn=xn, proj=proj, y_attn=y_attn, lse=lse, y_sgu=y_sgu, a_attn=a_attn, a_sgu=a_sgu,
                          merged=merged, h_mid=h_mid, hn=hn, gu=gu, act=act))
        h = h_out

    dh, dh_b, d_final, loss_part = _loss_head(h, _row(final_g), target)

    small_grads = [None] * L
    reduced = [[None] * len(big) for _ in range(L)]
    early, mid, late = [GU, DOWN], [AB, SB, OUT], [IN_]

    def riding(has_carry, res):
        return res if has_carry else (res, None)

    def sends_of(grads, land):
        return [_pair_sum(g, d) for g, d in zip(grads, land)]

    def finished(grads, land, got):
        return [_chip_sum(g, d, p) for g, d, p in zip(grads, land, got)]

    def file_reduced(layer, idx, fulls):
        for t, f in zip(idx, fulls):
            reduced[layer][t] = f

    late_grads = None
    mid_fulls = None
    n_late, n_mid, n_early = len(late), len(mid), len(early)
    for l in reversed(range(L)):
        w_in_g, w_ab_g, w_sb_g, w_out_g, w_gu_g, w_down_g = weights(l)
        sp, sv = small(l), saved[l]
        have = late_grads is not None
        dgu, rode = riding(have, _down_bwd(
            dh_b, w_down_g, sv["gu"], dm,
            carry=_sibling_exchange(late_grads).beside(_sibling_share(mid_fulls)) if have else None))
        if have:
            land = rode[:n_late]
            file_reduced(l + 1, mid, rode[n_late:])
        g_down = _wgrad_rows("wgrad_down", sv["act"], dh_b)
        dhn, got = riding(have, _gate_up_bwd(dgu, w_gu_g, dm,
                                             carry=_chip_exchange(sends_of(late_grads, land)) if have else None))
        g_gu, shared = riding(have, _wgrad_gate_up(sv["hn"], dgu, dm,
                                                   carry=_sibling_share(finished(late_grads, land, got)) if have else None))
        if have:
            file_reduced(l + 1, late, shared)
        dh_mid, dh_mid_b, d_g2 = _rms_bwd(dhn, sv["h_mid"], sp["g2"], dh)
        early_grads = [g_gu, g_down]
        (da_attn, da_sgu, dgate), land_e = _out_bwd(dh_mid_b, w_out_g, sv["proj"], sv["a_attn"], sv["a_sgu"], dm,
                                                     carry=_sibling_exchange(early_grads))
        sends_e = sends_of(early_grads, land_e)
        g_out = _wgrad_rows("wgrad_out", sv["merged"], dh_mid_b)
        dy_attn = _colsharded_bwd("branch_attn_bwd", da_attn, w_ab_g, BF16)
        dy_sgu = _colsharded_bwd("branch_sgu_bwd", da_sgu, w_sb_g, BF16)
        g_ab = _wgrad_cols("wgrad_attn_branch", sv["y_attn"], da_attn)
        g_sb = _wgrad_cols("wgrad_sgu_branch", sv["y_sgu"], da_sgu)
        mid_grads = [g_ab, g_sb, g_out]
        dq, dk, dv, d_sink = _attn_bwd(sv["proj"], trig, sp["sink"], sv["y_attn"], sv["lse"], dy_attn, dm)
        dz, d_sgu_w, d_bt, d_lng, d_lnb = _sgu_bwd(sv["proj"], sp["w_tril"], sp["b_t"], sp["ln_g"], sp["ln_b"], dy_sgu, dm)
        dproj = jnp.concatenate([dq, dk, dv, dz, dgate[0], dgate[1]], axis=1)
        dxn, rode = _colsharded_bwd("in_proj_bwd", dproj, w_in_g, F32,
                                    carry=_chip_exchange(sends_e).beside(_sibling_exchange(mid_grads)))
        got_e, land_m = rode[:n_early], rode[n_early:]
        (g_in, d_bin), rode = _wgrad_cols(
            "wgrad_in", sv["xn"], dproj, colsum=True,
            carry=_sibling_share(finished(early_grads, land_e, got_e)).beside(_chip_exchange(sends_of(mid_grads, land_m))))
        file_reduced(l, early, rode[:n_early])
        mid_fulls = finished(mid_grads, land_m, rode[n_early:])
        dh, dh_b, d_g1 = _rms_bwd(dxn, sv["h"], sp["g1"], dh_mid)
        late_grads = [g_in]
        small_grads[l] = dict(norm1_g=d_g1[0], b_in=d_bin[0, :, 0, :].reshape(-1), sinks=d_sink[0, :NQ],
                              sgu_ln_g=d_lng[0], sgu_ln_b=d_lnb[0], sgu_w=d_sgu_w, sgu_b=d_bt[:, :NG].T, norm2_g=d_g2[0])
    grad_x = dh[None]

    land = _sibling_exchange(late_grads).run()
    got = _chip_exchange(sends_of(late_grads, land)).run()
    shared = _sibling_share(finished(late_grads, land, got) + mid_fulls).run()
    file_reduced(0, late, shared[:n_late])
    file_reduced(0, mid, shared[n_late:])
    big_out = [_adamw_stacked([reduced[l][t] for l in range(L)], big[t], big_m[t], big_v[t]) for t in range(len(big))]

    names = ["norm1_g", "b_in", "sinks", "sgu_ln_g", "sgu_ln_b", "sgu_w", "sgu_b", "norm2_g"]
    small_w = [norm1_g, b_in, sinks, sgu_ln_g, sgu_ln_b, sgu_w, sgu_b, norm2_g, final_g]
    small_m = [m_norm1_g, m_b_in, m_sinks, m_sgu_ln_g, m_sgu_ln_b, m_sgu_w, m_sgu_b, m_norm2_g, m_final_g]
    small_v = [v_norm1_g, v_b_in, v_sinks, v_sgu_ln_g, v_sgu_ln_b, v_sgu_w, v_sgu_b, v_norm2_g, v_final_g]
    small_g = [jnp.stack([small_grads[l][nm] for l in range(L)]) for nm in names] + [d_final[0]]
    sizes = [w.size for w in small_w]
    total = sum(sizes) + 1
    rows = -(-total // (512 * LANES)) * 512
    loss_piece = jnp.sum(loss_part[0]).reshape(1)
    packed_g = _pack(small_g + [loss_piece], rows)
    one = jnp.ones((1,), F32)
    parts = _gather_all(packed_g)
    outs = _adamw_small(parts, _pack(small_w + [one], rows), _pack(small_m + [one], rows), _pack(small_v + [one], rows))

    def unpack(p):
        flat = p.reshape(-1)
        res, off = [], 0
        for w, n in zip(small_w, sizes):
            res.append(flat[off:off + n].reshape(w.shape))
            off += n
        return res, flat[off]

    (sg, loss), (sd, _), (smm, _), (svv, _) = [unpack(o) for o in outs]

    order = ["norm1_g", "w_in", "b_in", "sinks", "sgu_ln_g", "sgu_ln_b", "sgu_w", "sgu_b", "w_attn_branch",
             "w_sgu_branch", "w_out", "norm2_g", "w_gate_up", "w_down", "final_g"]
    big_names = ["w_in", "w_attn_branch", "w_sgu_branch", "w_out", "w_gate_up", "w_down"]
    small_names = names + ["final_g"]

    def collect(kind):
        res = []
        for nm in order:
            if nm in big_names:
                res.append(big_out[big_names.index(nm)][kind])
            else:
                res.append((sg, sd, smm, svv)[kind][small_names.index(nm)])
        return res

    return (loss, grad_x, *collect(0), *collect(1), *collect(2), *collect(3))
```

```python
import math

import jax
import jax.numpy as jnp
from jax import lax
from jax.experimental import pallas as pl
from jax.experimental.pallas import tpu as pltpu

F32 = jnp.float32
BF16 = jnp.bfloat16
MESH = pl.DeviceIdType.MESH
ANY = pl.BlockSpec(memory_space=pl.ANY)

HEAD_DIM = 64
N_KV_HEADS = 4
WINDOW = 128
ROPE_DIM = HEAD_DIM // 4
ROPE_THETA = 500000.0
EPS = 1e-5
NEG = -1e30
N_CHIPS = 4
LANES = 128
V7X_VMEM_LIMIT = 56 * 1024 * 1024

ADAM_LR = 0.001
ADAM_B1 = 0.9
ADAM_B2 = 0.999
ADAM_EPS = 1e-08
ADAM_WD = 0.01
ADAM_STEP = 10

NN = (((1,), (0,)), ((), ()))
NT = (((1,), (1,)), ((), ()))
TN = (((0,), (0,)), ((), ()))


ROW_TILES = (1024, 512, 256, 128, 64, 32, 16, 8)
BLOCK_BYTES = 2 * 1024 * 1024


def _pick(n, prefs):
    for p in prefs:
        if n % p == 0:
            return p
    raise ValueError(f"no tile for {n} among {prefs}")


def _row_tile(rows, cols, itemsize=4):
    return _pick(rows, [t for t in ROW_TILES if t * cols * itemsize <= BLOCK_BYTES or t == ROW_TILES[-1]])


def _dot(a, b, dims):
    return lax.dot_general(a, b, dims, preferred_element_type=F32)


def _sigmoid(x):
    return 1.0 / (1.0 + jnp.exp(-x))


def _gelu(x):
    return 0.5 * x * (1.0 + lax.erf(x * (1.0 / math.sqrt(2.0))))


def _gelu_grad(x):
    return 0.5 * (1.0 + lax.erf(x * (1.0 / math.sqrt(2.0)))) + x * jnp.exp(-0.5 * x * x) * (1.0 / math.sqrt(2.0 * math.pi))


def _params(sem):
    return pltpu.CompilerParams(dimension_semantics=sem, vmem_limit_bytes=V7X_VMEM_LIMIT)


def _matmul(name, lhs, rhs_list, *, dims, grid, lhs_spec, rhs_specs, acc_shape, out_shape, out_specs,
            epilogue, extra=(), extra_specs=(), carry=None, rhs_colsum=False, cols_outer=False):
    if cols_outer:
        swap = lambda s: pl.BlockSpec(s.block_shape, lambda j, i, k, f=s.index_map: f(i, j, k))
        grid = (grid[1], grid[0], grid[2])
        lhs_spec, rhs_specs = swap(lhs_spec), [swap(s) for s in rhs_specs]
        extra_specs, out_specs = [swap(s) for s in extra_specs], [swap(s) for s in out_specs]
    gk = grid[2]
    nr, ne, no = len(rhs_list), len(extra), len(out_shape)
    nci = len(carry.ins) if carry else 0
    nco = len(carry.outs) if carry else 0
    acc_shapes = [acc_shape] * nr + ([(8, acc_shape[1])] if rhs_colsum else [])
    nacc = len(acc_shapes) if gk > 1 else 0

    def body(*refs):
        a_ref = refs[0]
        b_refs = refs[1:1 + nr]
        e_refs = refs[1 + nr:1 + nr + ne]
        base = 1 + nr + ne
        ci_refs = refs[base:base + nci]
        o_refs = refs[base + nci:base + nci + no]
        co_refs = refs[base + nci + no:base + nci + no + nco]
        acc_refs = refs[base + nci + no + nco:base + nci + no + nco + nacc]
        sems = refs[base + nci + no + nco + nacc:]
        ids = [pl.program_id(d) for d in range(3)]
        if carry:
            @pl.when((ids[0] == 0) & (ids[1] == 0) & (ids[2] == 0))
            def _():
                carry.start(ci_refs, co_refs, *sems)

        a = a_ref[...]
        if gk == 1:
            n_axis = 1 - dims[0][1][0]
            for cols in _col_chunks(acc_shape[1]):
                pick = (slice(None), cols) if n_axis == 1 else (cols, slice(None))
                parts = [_dot(a, b[pick], dims) for b in b_refs]
                if rhs_colsum:
                    b0 = b_refs[0][pick]
                    parts.append(_dot(jnp.ones((8, b0.shape[0]), b0.dtype), b0, NN))
                epilogue(parts, e_refs, o_refs, cols)
        else:
            k = ids[2]

            @pl.when(k == 0)
            def _():
                for acc in acc_refs:
                    acc[...] = jnp.zeros_like(acc)

            for acc, b in zip(acc_refs, b_refs):
                acc[...] += _dot(a, b[...], dims)
            if rhs_colsum:
                b0 = b_refs[0][...]
                acc_refs[-1][...] += _dot(jnp.ones((8, b0.shape[0]), b0.dtype), b0, NN)

            @pl.when(k == gk - 1)
            def _():
                epilogue([acc[...] for acc in acc_refs], e_refs, o_refs, slice(None))

        if carry:
            @pl.when((ids[0] == grid[0] - 1) & (ids[1] == grid[1] - 1) & (ids[2] == grid[2] - 1))
            def _():
                carry.finish(ci_refs, co_refs, *sems)

    scratch = [pltpu.VMEM(s, F32) for s in acc_shapes[:nacc]]
    kwargs = {}
    if carry:
        scratch += carry.sem_scratch()
        kwargs["input_output_aliases"] = {1 + nr + ne + i: no + o for i, o in carry.aliases.items()}
    outs = pl.pallas_call(
        body, name=name, grid=grid,
        in_specs=[lhs_spec, *rhs_specs, *extra_specs, *([ANY] * nci)],
        out_specs=[*out_specs, *([ANY] * nco)],
        out_shape=[*out_shape, *(carry.outs if carry else [])], scratch_shapes=scratch,
        compiler_params=_params(("arbitrary",) * 3 if carry else ("parallel", "parallel", "arbitrary")),
        **kwargs,
    )(lhs, *rhs_list, *extra, *(carry.ins if carry else []))
    return outs


class _Comm:
    def __init__(self, name, ins, outs, aliases, n_sems, start, finish):
        self.name, self.ins, self.outs, self.aliases, self.n_sems = name, list(ins), list(outs), dict(aliases), n_sems
        self.start, self.finish = start, finish

    def sem_scratch(self):
        return [pltpu.SemaphoreType.DMA((self.n_sems,)), pltpu.SemaphoreType.DMA((self.n_sems,))]

    def beside(self, other):
        ni, no, ns = len(self.ins), len(self.outs), self.n_sems

        def split(ins, outs, send_sems, recv_sems):
            mine = (ins[:ni], outs[:no], send_sems.at[pl.ds(0, ns)], recv_sems.at[pl.ds(0, ns)])
            theirs = (ins[ni:], outs[no:], send_sems.at[pl.ds(ns, other.n_sems)], recv_sems.at[pl.ds(ns, other.n_sems)])
            return mine, theirs

        def start(*refs):
            mine, theirs = split(*refs)
            self.start(*mine)
            other.start(*theirs)

        def finish(*refs):
            mine, theirs = split(*refs)
            self.finish(*mine)
            other.finish(*theirs)

        aliases = {**self.aliases, **{ni + i: no + o for i, o in other.aliases.items()}}
        return _Comm(self.name + "+" + other.name, self.ins + other.ins, self.outs + other.outs, aliases,
                     ns + other.n_sems, start, finish)

    def run(self):
        ni = len(self.ins)

        def body(*refs):
            in_refs, out_refs, sems = refs[:ni], refs[ni:ni + len(self.outs)], refs[ni + len(self.outs):]
            self.start(in_refs, out_refs, *sems)
            self.finish(in_refs, out_refs, *sems)

        return pl.pallas_call(
            body, name=self.name, in_specs=[ANY] * ni, out_specs=[ANY] * len(self.outs), out_shape=self.outs,
            input_output_aliases=self.aliases, scratch_shapes=self.sem_scratch(),
        )(*self.ins)


MXU_CHUNK = 512


def _col_chunks(n):
    if n % LANES:
        return [slice(0, n)]
    return [slice(s, min(s + MXU_CHUNK, n)) for s in range(0, n, MXU_CHUNK)]


def _store_epilogue(dtype):
    def ep(parts, e_refs, o_refs, cols):
        o_refs[0][:, cols] = parts[0].astype(dtype)
    return ep


def _rms_fwd(h, g_row):
    S, D = h.shape
    tm = _row_tile(S, D)

    def body(h_ref, g_ref, o_ref):
        x = h_ref[...]
        r = lax.rsqrt(jnp.mean(x * x, axis=-1, keepdims=True) + EPS)
        o_ref[...] = (x * r * g_ref[...]).astype(BF16)

    return pl.pallas_call(
        body, name="rms_fwd", grid=(S // tm,),
        in_specs=[pl.BlockSpec((tm, D), lambda i: (i, 0)), pl.BlockSpec((1, D), lambda i: (0, 0))],
        out_specs=pl.BlockSpec((tm, D), lambda i: (i, 0)),
        out_shape=jax.ShapeDtypeStruct((S, D), BF16),
        compiler_params=_params(("parallel",)),
    )(h, g_row)


def _rms_bwd(dy, h, g_row, dres):
    S, D = h.shape
    tm = _row_tile(S, D)

    def body(dy_ref, h_ref, g_ref, dres_ref, dh_ref, dhb_ref, dg_ref):
        i = pl.program_id(0)
        x = h_ref[...]
        d = dy_ref[...]
        r = lax.rsqrt(jnp.mean(x * x, axis=-1, keepdims=True) + EPS)
        dg = d * g_ref[...]
        dot = jnp.mean(dg * x, axis=-1, keepdims=True)
        dh = dres_ref[...] + r * dg - x * (r * r * r) * dot
        dh_ref[...] = dh
        dhb_ref[...] = dh.astype(BF16)
        part = jnp.sum(d * x * r, axis=0, keepdims=True)

        @pl.when(i == 0)
        def _():
            dg_ref[...] = jnp.zeros_like(dg_ref)

        dg_ref[0:1, :] += part

    return pl.pallas_call(
        body, name="rms_bwd", grid=(S // tm,),
        in_specs=[pl.BlockSpec((tm, D), lambda i: (i, 0)), pl.BlockSpec((tm, D), lambda i: (i, 0)),
                  pl.BlockSpec((1, D), lambda i: (0, 0)), pl.BlockSpec((tm, D), lambda i: (i, 0))],
        out_specs=[pl.BlockSpec((tm, D), lambda i: (i, 0)), pl.BlockSpec((tm, D), lambda i: (i, 0)),
                   pl.BlockSpec((8, D), lambda i: (0, 0))],
        out_shape=[jax.ShapeDtypeStruct((S, D), F32), jax.ShapeDtypeStruct((S, D), BF16),
                   jax.ShapeDtypeStruct((8, D), F32)],
        compiler_params=_params(("arbitrary",)),
    )(dy, h, g_row, dres)


def _loss_head(h, g_row, target):
    S, D = h.shape
    tm = _row_tile(S, D)

    def body(h_ref, g_ref, t_ref, dh_ref, dhb_ref, dg_ref, loss_ref):
        i = pl.program_id(0)
        x = h_ref[...]
        g = g_ref[...]
        r = lax.rsqrt(jnp.mean(x * x, axis=-1, keepdims=True) + EPS)
        y = x * r * g
        e = y - t_ref[...]
        d = e * (1.0 / D)
        dg = d * g
        dot = jnp.mean(dg * x, axis=-1, keepdims=True)
        dh = r * dg - x * (r * r * r) * dot
        dh_ref[...] = dh
        dhb_ref[...] = dh.astype(BF16)

        @pl.when(i == 0)
        def _():
            dg_ref[...] = jnp.zeros_like(dg_ref)
            loss_ref[...] = jnp.zeros_like(loss_ref)

        dg_ref[0:1, :] += jnp.sum(d * x * r, axis=0, keepdims=True)
        loss_ref[0:1, :] += jnp.sum((0.5 / D) * e * e, axis=0, keepdims=True)

    return pl.pallas_call(
        body, name="loss_head", grid=(S // tm,),
        in_specs=[pl.BlockSpec((tm, D), lambda i: (i, 0)), pl.BlockSpec((1, D), lambda i: (0, 0)),
                  pl.BlockSpec((tm, D), lambda i: (i, 0))],
        out_specs=[pl.BlockSpec((tm, D), lambda i: (i, 0)), pl.BlockSpec((tm, D), lambda i: (i, 0)),
                   pl.BlockSpec((8, D), lambda i: (0, 0)), pl.BlockSpec((8, D), lambda i: (0, 0))],
        out_shape=[jax.ShapeDtypeStruct((S, D), F32), jax.ShapeDtypeStruct((S, D), BF16),
                   jax.ShapeDtypeStruct((8, D), F32), jax.ShapeDtypeStruct((8, D), F32)],
        compiler_params=_params(("arbitrary",)),
    )(h, g_row, target)


def _rotary_partner(t):
    half = ROPE_DIM // 2
    r = lax.broadcasted_iota(jnp.int32, (LANES, LANES), 0)
    c = lax.broadcasted_iota(jnp.int32, (LANES, LANES), 1)
    cm = c & (HEAD_DIM - 1)
    perm = (((cm < half) & (r == c + half)) | ((cm >= half) & (cm < ROPE_DIM) & (r == c - half))).astype(BF16)
    hi = t.astype(BF16)
    lo = (t - hi.astype(F32)).astype(BF16)
    cols = [slice(s, s + LANES) for s in range(0, t.shape[-1], LANES)]
    return jnp.concatenate([_dot(hi[:, c_], perm, NN) + _dot(lo[:, c_], perm, NN) for c_ in cols], axis=1)


def _rope(t, cos, sin):
    return t * cos + _rotary_partner(t) * sin


def _rope_t(g, cos, sin):
    return g * cos + _rotary_partner(g * sin)


def _band_mask(n, qpk):
    qi = lax.broadcasted_iota(jnp.int32, (qpk * WINDOW, 2 * WINDOW), 0) & (WINDOW - 1)
    kj = lax.broadcasted_iota(jnp.int32, (qpk * WINDOW, 2 * WINDOW), 1)
    rel = qi + WINDOW - kj
    ok = (rel >= 0) & (rel < WINDOW)
    return ok & ((kj >= WINDOW) | (n > 0))


def _stack_heads(x, g, qpk):
    return jnp.concatenate([x[:, (g * qpk + hh) * HEAD_DIM:(g * qpk + hh + 1) * HEAD_DIM] for hh in range(qpk)], axis=0)


def _stack_cols(row, g, qpk):
    return jnp.concatenate([row[:, g * qpk + hh:g * qpk + hh + 1] for hh in range(qpk)], axis=0)


def _attn_specs(dm, nb):
    A, KV = dm["A"], dm["KV"]
    kb, vb = dm["OFF_K"] // KV, dm["OFF_V"] // KV
    cur = lambda n: jnp.minimum(n, nb - 1)
    prev = lambda n: jnp.maximum(jnp.minimum(n, nb - 1) - 1, 0)
    proj_specs = [
        pl.BlockSpec((WINDOW, A), lambda n: (cur(n), 0)),
        pl.BlockSpec((WINDOW, KV), lambda n: (prev(n), kb)),
        pl.BlockSpec((WINDOW, KV), lambda n: (cur(n), kb)),
        pl.BlockSpec((WINDOW, KV), lambda n: (prev(n), vb)),
        pl.BlockSpec((WINDOW, KV), lambda n: (cur(n), vb)),
    ]
    trig_cur = [pl.BlockSpec((WINDOW, LANES), lambda n: (cur(n), 0)) for _ in range(2)]
    trig_prev = [pl.BlockSpec((WINDOW, LANES), lambda n: (prev(n), 0)) for _ in range(2)]
    return proj_specs, trig_cur, trig_prev, cur, prev


def _attn_fwd(proj, sink_row, dm):
    S = proj.shape[0]
    A, KV, NQ = dm["A"], dm["KV"], dm["NQ"]
    qpk = NQ // N_KV_HEADS
    nb = S // WINDOW
    scale = HEAD_DIM ** -0.5
    proj_specs = _attn_specs(dm, nb)[0]

    def body(q_ref, kp_ref, kc_ref, vp_ref, vc_ref, sink_ref, y_ref, lse_ref):
        n = pl.program_id(0)
        qr = q_ref[...]
        kr = jnp.concatenate([kp_ref[...], kc_ref[...]], axis=0)
        vband = jnp.concatenate([vp_ref[...], vc_ref[...]], axis=0)
        mask = _band_mask(n, qpk)
        lane = lax.broadcasted_iota(jnp.int32, (WINDOW, LANES), 1)
        lse_all = jnp.zeros((WINDOW, LANES), F32)
        sink_rows = jnp.broadcast_to(sink_ref[0:1, :], (WINDOW, LANES))
        groups = range(N_KV_HEADS)
        head = lambda x, g: x[:, g * HEAD_DIM:(g + 1) * HEAD_DIM]
        ones = jnp.ones((2 * WINDOW, HEAD_DIM), BF16)
        sink = [_stack_cols(sink_rows, g, qpk) for g in groups]
        s = [jnp.where(mask, _dot(_stack_heads(qr, g, qpk), head(kr, g), NT) * scale, NEG) for g in groups]
        m = [jnp.maximum(jnp.max(s[g], axis=-1, keepdims=True), sink[g]) for g in groups]
        p = [jnp.exp(s[g] - m[g]).astype(BF16) for g in groups]
        ov = [_dot(p[g], jnp.concatenate([head(vband, g), ones], axis=1), NN) for g in groups]
        den = [ov[g][:, HEAD_DIM:HEAD_DIM + 1] + jnp.exp(sink[g] - m[g]) for g in groups]
        o = [ov[g][:, :HEAD_DIM] * (1.0 / den[g]) for g in groups]
        lse = [m[g] + jnp.log(den[g]) for g in groups]
        for g in groups:
            for hh in range(qpk):
                h = g * qpk + hh
                rows = slice(hh * WINDOW, (hh + 1) * WINDOW)
                y_ref[:, h * HEAD_DIM:(h + 1) * HEAD_DIM] = o[g][rows].astype(BF16)
                lse_all = jnp.where(lane == h, lse[g][rows], lse_all)
        lse_ref[...] = lse_all

    return pl.pallas_call(
        body, name="attn_fwd", grid=(nb,),
        in_specs=[*proj_specs, pl.BlockSpec((8, LANES), lambda n: (0, 0))],
        out_specs=[pl.BlockSpec((WINDOW, A), lambda n: (n, 0)), pl.BlockSpec((WINDOW, LANES), lambda n: (n, 0))],
        out_shape=[jax.ShapeDtypeStruct((S, A), BF16), jax.ShapeDtypeStruct((S, LANES), F32)],
        compiler_params=_params(("parallel",)),
    )(proj, proj, proj, proj, proj, sink_row)


def _attn_bwd(proj, trig, sink_row, y, lse, dy, dm):
    S = proj.shape[0]
    A, KV, NQ = dm["A"], dm["KV"], dm["NQ"]
    qpk = NQ // N_KV_HEADS
    nb = S // WINDOW
    scale = HEAD_DIM ** -0.5
    proj_specs, trig_cur, trig_prev, cur, prev = _attn_specs(dm, nb)

    def body(q_ref, kp_ref, kc_ref, vp_ref, vc_ref, cc_ref, sc_ref, cp_ref, sp_ref,
             sink_ref, y_ref, lse_ref, dy_ref, dq_ref, dk_ref, dv_ref, dsink_ref,
             ck_ref, cv_ref, bk_ref, bv_ref, dqr_ref):
        n = pl.program_id(0)

        @pl.when(n == 0)
        def _():
            dsink_ref[...] = jnp.zeros_like(dsink_ref)
            ck_ref[...] = jnp.zeros_like(ck_ref)
            cv_ref[...] = jnp.zeros_like(cv_ref)

        @pl.when(n < nb)
        def _():
            tq = lambda r: jnp.tile(r[...], (1, A // LANES))
            tk = lambda rp, rc: jnp.tile(jnp.concatenate([rp[...], rc[...]], axis=0), (1, KV // LANES))
            cq, sq = tq(cc_ref), tq(sc_ref)
            ck, sk = tk(cp_ref, cc_ref), tk(sp_ref, sc_ref)
            qr = q_ref[...]
            kr = jnp.concatenate([kp_ref[...], kc_ref[...]], axis=0)
            vband = jnp.concatenate([vp_ref[...], vc_ref[...]], axis=0)
            mask = _band_mask(n, qpk)
            lane = lax.broadcasted_iota(jnp.int32, (1, LANES), 1)
            lse_all = lse_ref[...]
            sink_rows = jnp.broadcast_to(sink_ref[0:1, :], (WINDOW, LANES))
            dy_all = dy_ref[...]
            y_all = y_ref[...]
            dsink = jnp.zeros((1, LANES), F32)
            groups = range(N_KV_HEADS)
            head = lambda x, g: x[:, g * HEAD_DIM:(g + 1) * HEAD_DIM]
            q = [_stack_heads(qr, g, qpk) for g in groups]
            dy = [_stack_heads(dy_all, g, qpk) for g in groups]
            lse = [_stack_cols(lse_all, g, qpk) for g in groups]
            s = [jnp.where(mask, _dot(q[g], head(kr, g), NT) * scale, NEG) for g in groups]
            dp = [_dot(dy[g], head(vband, g), NT) for g in groups]
            delta = [jnp.sum(dy[g].astype(F32) * _stack_heads(y_all, g, qpk).astype(F32), axis=-1, keepdims=True)
                     for g in groups]
            p = [jnp.exp(s[g] - lse[g]) for g in groups]
            ds = [(p[g] * (dp[g] - delta[g]) * scale).astype(BF16) for g in groups]
            dq = [_dot(ds[g], head(kr, g), NN) for g in groups]
            for g in groups:
                bk_ref[:, g * HEAD_DIM:(g + 1) * HEAD_DIM] = _dot(ds[g], q[g], TN)
                bv_ref[:, g * HEAD_DIM:(g + 1) * HEAD_DIM] = _dot(p[g].astype(BF16), dy[g], TN)
            for g in groups:
                sink_d = jnp.exp(_stack_cols(sink_rows, g, qpk) - lse[g]) * delta[g]
                for hh in range(qpk):
                    h = g * qpk + hh
                    rows = slice(hh * WINDOW, (hh + 1) * WINDOW)
                    dqr_ref[:, h * HEAD_DIM:(h + 1) * HEAD_DIM] = dq[g][rows]
                    dsink = dsink + jnp.where(lane == h, -jnp.sum(sink_d[rows], axis=0, keepdims=True), 0.0)
            dsink_ref[0:1, :] += dsink
            dq_ref[...] = _rope_t(dqr_ref[...], cq, sq).astype(BF16)
            dkb = _rope_t(bk_ref[...], ck, sk)
            dvb = bv_ref[...]
            dk_ref[...] = (ck_ref[...] + dkb[:WINDOW]).astype(BF16)
            dv_ref[...] = (cv_ref[...] + dvb[:WINDOW]).astype(BF16)
            ck_ref[...] = dkb[WINDOW:]
            cv_ref[...] = dvb[WINDOW:]

        @pl.when(n == nb)
        def _():
            dk_ref[...] = ck_ref[...].astype(BF16)
            dv_ref[...] = cv_ref[...].astype(BF16)

    row = lambda w: pl.BlockSpec((WINDOW, w), lambda n: (cur(n), 0))
    done = lambda w: pl.BlockSpec((WINDOW, w), lambda n: (jnp.maximum(n - 1, 0), 0))
    return pl.pallas_call(
        body, name="attn_bwd", grid=(nb + 1,),
        in_specs=[*proj_specs, *trig_cur, *trig_prev, pl.BlockSpec((8, LANES), lambda n: (0, 0)),
                  row(A), row(LANES), row(A)],
        out_specs=[row(A), done(KV), done(KV), pl.BlockSpec((8, LANES), lambda n: (0, 0))],
        out_shape=[jax.ShapeDtypeStruct((S, A), BF16), jax.ShapeDtypeStruct((S, KV), BF16),
                   jax.ShapeDtypeStruct((S, KV), BF16), jax.ShapeDtypeStruct((8, LANES), F32)],
        scratch_shapes=[pltpu.VMEM((WINDOW, KV), F32), pltpu.VMEM((WINDOW, KV), F32),
                        pltpu.VMEM((2 * WINDOW, KV), F32), pltpu.VMEM((2 * WINDOW, KV), F32),
                        pltpu.VMEM((WINDOW, A), F32)],
        compiler_params=_params(("arbitrary",)),
    )(proj, proj, proj, proj, proj, *trig, *trig, sink_row, y, lse, dy)


def _sgu_layout(dm, S):
    G = dm["G"]
    pw = math.gcd(dm["OFF_Z"], G)
    npc = G // pw
    tm = _pick(S, (256, 128))
    u_specs = [pl.BlockSpec((tm, pw), lambda i, p=p: (i, dm["OFF_Z"] // pw + p)) for p in range(npc)]
    v_specs = [pl.BlockSpec((tm, pw), lambda i, p=p: (i, (dm["OFF_Z"] + G) // pw + p)) for p in range(npc)]
    return pw, npc, tm, u_specs, v_specs


def _sgu_norm(v_refs, lg_ref, lb_ref):
    v = jnp.concatenate([_gelu(r[...].astype(F32)) for r in v_refs], axis=1)
    mu = jnp.mean(v, axis=-1, keepdims=True)
    vc = v - mu
    rstd = lax.rsqrt(jnp.mean(vc * vc, axis=-1, keepdims=True) + EPS)
    xhat = vc * rstd
    return xhat, rstd, (xhat * lg_ref[...] + lb_ref[...]).astype(BF16)


def _sgu_fwd(proj, w_tril, b_t, ln_g_row, ln_b_row, dm):
    S = proj.shape[0]
    G, NG = dm["G"], dm["NG"]
    pw, npc, tm, u_specs, v_specs = _sgu_layout(dm, S)
    nch = tm // WINDOW

    def body(*refs):
        u_refs, v_refs = refs[:npc], refs[npc:2 * npc]
        w_ref, bt_ref, lg_ref, lb_ref, y_ref = refs[2 * npc:]
        _, _, vn = _sgu_norm(v_refs, lg_ref, lb_ref)
        u = jnp.concatenate([_gelu(r[...].astype(F32)) for r in u_refs], axis=1)
        for c in range(nch):
            rows = slice(c * WINDOW, (c + 1) * WINDOW)
            for g in range(NG):
                cols = slice(g * LANES, (g + 1) * LANES)
                sv = _dot(w_ref[g], vn[rows, cols], NN) + bt_ref[:, g:g + 1]
                y_ref[rows, cols] = (u[rows, cols] * sv).astype(BF16)

    return pl.pallas_call(
        body, name="sgu_fwd", grid=(S // tm,),
        in_specs=[*u_specs, *v_specs,
                  pl.BlockSpec((NG, WINDOW, WINDOW), lambda i: (0, 0, 0)),
                  pl.BlockSpec((WINDOW, LANES), lambda i: (0, 0)),
                  pl.BlockSpec((1, G), lambda i: (0, 0)), pl.BlockSpec((1, G), lambda i: (0, 0))],
        out_specs=pl.BlockSpec((tm, G), lambda i: (i, 0)),
        out_shape=jax.ShapeDtypeStruct((S, G), BF16),
        compiler_params=_params(("parallel",)),
    )(*([proj] * (2 * npc)), w_tril, b_t, ln_g_row, ln_b_row)


def _sgu_bwd(proj, w_tril, b_t, ln_g_row, ln_b_row, dy, dm):
    S = proj.shape[0]
    G, NG = dm["G"], dm["NG"]
    pw, npc, tm, u_specs, v_specs = _sgu_layout(dm, S)
    nch = tm // WINDOW

    def body(*refs):
        u_refs, v_refs = refs[:npc], refs[npc:2 * npc]
        w_ref, bt_ref, lg_ref, lb_ref, dy_ref, dz_ref, dw_ref, dbt_ref, dlg_ref, dlb_ref, dvn_ref = refs[2 * npc:]
        i = pl.program_id(0)

        @pl.when(i == 0)
        def _():
            dw_ref[...] = jnp.zeros_like(dw_ref)
            dbt_ref[...] = jnp.zeros_like(dbt_ref)
            dlg_ref[...] = jnp.zeros_like(dlg_ref)
            dlb_ref[...] = jnp.zeros_like(dlb_ref)

        xhat, rstd, vn = _sgu_norm(v_refs, lg_ref, lb_ref)
        u_pre = jnp.concatenate([r[...].astype(F32) for r in u_refs], axis=1)
        u = _gelu(u_pre)
        dy = dy_ref[...].astype(F32)
        lane = lax.broadcasted_iota(jnp.int32, (WINDOW, LANES), 1)
        tri = lax.broadcasted_iota(jnp.int32, (WINDOW, WINDOW), 0) >= lax.broadcasted_iota(jnp.int32, (WINDOW, WINDOW), 1)
        dbt = jnp.zeros((WINDOW, LANES), F32)
        for c in range(nch):
            rows = slice(c * WINDOW, (c + 1) * WINDOW)
            for g in range(NG):
                cols = slice(g * LANES, (g + 1) * LANES)
                vn_cg = vn[rows, cols]
                sv = _dot(w_ref[g], vn_cg, NN) + bt_ref[:, g:g + 1]
                dy_cg = dy[rows, cols]
                dsv = dy_cg * u[rows, cols]
                dsv_b = dsv.astype(BF16)
                dz_ref[rows, cols] = (dy_cg * sv * _gelu_grad(u_pre[rows, cols])).astype(BF16)
                dvn_ref[rows, cols] = _dot(w_ref[g], dsv_b, TN)
                dw_ref[g] += jnp.where(tri, _dot(dsv_b, vn_cg, NT), 0.0)
                dbt = dbt + jnp.where(lane == g, jnp.sum(dsv, axis=-1, keepdims=True), 0.0)
        dbt_ref[...] += dbt
        dvn = dvn_ref[...]
        dlg_ref[0:1, :] += jnp.sum(dvn * xhat, axis=0, keepdims=True)
        dlb_ref[0:1, :] += jnp.sum(dvn, axis=0, keepdims=True)
        dxh = dvn * lg_ref[...]
        dv = rstd * (dxh - jnp.mean(dxh, axis=-1, keepdims=True) - xhat * jnp.mean(dxh * xhat, axis=-1, keepdims=True))
        v_pre = jnp.concatenate([r[...].astype(F32) for r in v_refs], axis=1)
        dz_ref[:, G:] = (dv * _gelu_grad(v_pre)).astype(BF16)

    return pl.pallas_call(
        body, name="sgu_bwd", grid=(S // tm,),
        in_specs=[*u_specs, *v_specs,
                  pl.BlockSpec((NG, WINDOW, WINDOW), lambda i: (0, 0, 0)),
                  pl.BlockSpec((WINDOW, LANES), lambda i: (0, 0)),
                  pl.BlockSpec((1, G), lambda i: (0, 0)), pl.BlockSpec((1, G), lambda i: (0, 0)),
                  pl.BlockSpec((tm, G), lambda i: (i, 0))],
        out_specs=[pl.BlockSpec((tm, 2 * G), lambda i: (i, 0)),
                   pl.BlockSpec((NG, WINDOW, WINDOW), lambda i: (0, 0, 0)),
                   pl.BlockSpec((WINDOW, LANES), lambda i: (0, 0)),
                   pl.BlockSpec((8, G), lambda i: (0, 0)), pl.BlockSpec((8, G), lambda i: (0, 0))],
        out_shape=[jax.ShapeDtypeStruct((S, 2 * G), BF16), jax.ShapeDtypeStruct((NG, WINDOW, WINDOW), F32),
                   jax.ShapeDtypeStruct((WINDOW, LANES), F32), jax.ShapeDtypeStruct((8, G), F32),
                   jax.ShapeDtypeStruct((8, G), F32)],
        scratch_shapes=[pltpu.VMEM((tm, G), F32)],
        compiler_params=_params(("arbitrary",)),
    )(*([proj] * (2 * npc)), w_tril, b_t, ln_g_row, ln_b_row, dy)


def _result(outs, n_main, carry):
    main = outs[0] if n_main == 1 else tuple(outs[:n_main])
    return (main, list(outs[n_main:])) if carry else main


def _in_proj(xn, w_in_g, b_row, trig, dm, carry=None):
    S, D = xn.shape
    IN = dm["IN"]
    cw = IN // N_CHIPS
    tm = _pick(S, (512, 256, 128))
    tn = _pick(cw, (1920, 640, 512, 256, 128))
    nbc = cw // tn
    rope_cols = dm["OFF_V"]
    rope_blocks = -(-rope_cols // tn)

    def ep(parts, e_refs, o_refs, cols):
        j = pl.program_id(0)
        val = parts[0] + e_refs[0][:, cols]
        width = cols.stop - cols.start

        @pl.when(j >= rope_blocks)
        def _():
            o_refs[0][:, cols] = val.astype(BF16)

        for jj in range(rope_blocks):
            r = min(max(rope_cols - (jj * tn + cols.start), 0), width)

            @pl.when(j == jj)
            def _(r=r):
                out = val
                if r:
                    cos, sin = [jnp.tile(e[...], (1, r // LANES)) for e in e_refs[1:]]
                    roped = _rope(val[:, :r], cos, sin)
                    out = roped if r == width else jnp.concatenate([roped, val[:, r:]], axis=1)
                o_refs[0][:, cols] = out.astype(BF16)

    rows = pl.BlockSpec((tm, LANES), lambda i, j, k: (i, 0))
    return _result(_matmul(
        "in_proj", xn, [w_in_g], dims=NN, grid=(S // tm, IN // tn, 1),
        lhs_spec=pl.BlockSpec((tm, D), lambda i, j, k: (i, 0)),
        rhs_specs=[pl.BlockSpec((None, D, tn), lambda i, j, k: (j // nbc, 0, j % nbc))],
        acc_shape=(tm, tn), extra=[b_row, *trig],
        extra_specs=[pl.BlockSpec((1, tn), lambda i, j, k: (0, j)), rows, rows],
        out_shape=[jax.ShapeDtypeStruct((S, IN), BF16)],
        out_specs=[pl.BlockSpec((tm, tn), lambda i, j, k: (i, j))], epilogue=ep, carry=carry, cols_outer=True), 1, carry)


def _branch_attn(y_attn, w_ab_g, dm):
    S, A = y_attn.shape
    D = dm["D"]
    cw = D // N_CHIPS
    tm = _pick(S, (1024, 512, 256, 128))
    return _matmul(
        "branch_attn", y_attn, [w_ab_g], dims=NN, grid=(S // tm, N_CHIPS, 1),
        lhs_spec=pl.BlockSpec((tm, A), lambda i, j, k: (i, 0)),
        rhs_specs=[pl.BlockSpec((None, A, cw), lambda i, j, k: (j, 0, 0))],
        acc_shape=(tm, cw), out_shape=[jax.ShapeDtypeStruct((S, D), BF16)],
        out_specs=[pl.BlockSpec((tm, cw), lambda i, j, k: (i, j))], epilogue=_store_epilogue(BF16))[0]


def _branch_sgu_merge(y_sgu, w_sb_g, a_attn, proj, dm):
    S, G = y_sgu.shape
    D, OFF_G = dm["D"], dm["OFF_G"]
    cw = D // N_CHIPS
    tm = _pick(S, (1024, 512, 256, 128))

    def ep(parts, e_refs, o_refs, cols):
        a_sgu = parts[0].astype(BF16)
        ga = _sigmoid(e_refs[1][:, cols].astype(F32))
        gs = _sigmoid(e_refs[2][:, cols].astype(F32))
        o_refs[0][:, cols] = a_sgu
        o_refs[1][:, cols] = (ga * e_refs[0][:, cols].astype(F32) + gs * a_sgu.astype(F32)).astype(BF16)

    blk = pl.BlockSpec((tm, cw), lambda i, j, k: (i, j))
    return _matmul(
        "branch_sgu_merge", y_sgu, [w_sb_g], dims=NN, grid=(S // tm, N_CHIPS, 1),
        lhs_spec=pl.BlockSpec((tm, G), lambda i, j, k: (i, 0)),
        rhs_specs=[pl.BlockSpec((None, G, cw), lambda i, j, k: (j, 0, 0))],
        acc_shape=(tm, cw), extra=[a_attn, proj, proj],
        extra_specs=[blk, pl.BlockSpec((tm, cw), lambda i, j, k: (i, OFF_G // cw + j)),
                     pl.BlockSpec((tm, cw), lambda i, j, k: (i, (OFF_G + D) // cw + j))],
        out_shape=[jax.ShapeDtypeStruct((S, D), BF16), jax.ShapeDtypeStruct((S, D), BF16)],
        out_specs=[blk, blk], epilogue=ep)


def _residual_matmul(name, a, w_g, h, carry=None):
    S, K = a.shape
    D = w_g.shape[1]
    tm = _pick(S, (1024, 512, 256, 128))
    tn = _pick(D, (512, 256, 128))

    def ep(parts, e_refs, o_refs, cols):
        o_refs[0][:, cols] = e_refs[0][:, cols] + parts[0]

    blk = pl.BlockSpec((tm, tn), lambda i, j, k: (i, j))
    return _result(_matmul(
        name, a, [w_g], dims=NN, grid=(S // tm, D // tn, 1),
        lhs_spec=pl.BlockSpec((tm, K), lambda i, j, k: (i, 0)),
        rhs_specs=[pl.BlockSpec((K, tn), lambda i, j, k: (0, j))],
        acc_shape=(tm, tn), extra=[h], extra_specs=[blk],
        out_shape=[jax.ShapeDtypeStruct((S, D), F32)], out_specs=[blk], epilogue=ep, carry=carry), 1, carry)


def _gate_up(hn, w_gu_g, dm, carry=None):
    S, D = hn.shape
    Fd = dm["F"]
    cw = 2 * Fd // N_CHIPS
    tm = _pick(S, (512, 256, 128))
    tn = _pick(cw, (1408, 512, 384, 256, 128))
    nbc = cw // tn
    half = N_CHIPS // 2

    def ep(parts, e_refs, o_refs, cols):
        gate, up = parts[0].astype(BF16), parts[1].astype(BF16)
        o_refs[0][0, :, cols] = gate
        o_refs[0][1, :, cols] = up
        g32 = gate.astype(F32)
        o_refs[1][:, cols] = (g32 * _sigmoid(g32) * up.astype(F32)).astype(BF16)

    return _result(_matmul(
        "gate_up", hn, [w_gu_g, w_gu_g], dims=NN, grid=(S // tm, Fd // tn, 1),
        lhs_spec=pl.BlockSpec((tm, D), lambda i, j, k: (i, 0)),
        rhs_specs=[pl.BlockSpec((None, D, tn), lambda i, j, k: (j // nbc, 0, j % nbc)),
                   pl.BlockSpec((None, D, tn), lambda i, j, k: (half + j // nbc, 0, j % nbc))],
        acc_shape=(tm, tn),
        out_shape=[jax.ShapeDtypeStruct((2, S, Fd), BF16), jax.ShapeDtypeStruct((S, Fd), BF16)],
        out_specs=[pl.BlockSpec((2, tm, tn), lambda i, j, k: (0, i, j)), pl.BlockSpec((tm, tn), lambda i, j, k: (i, j))],
        epilogue=ep, carry=carry, cols_outer=True), 2, carry)


def _down_bwd(dh_b, w_down_g, gu, dm, carry=None):
    S, D = dh_b.shape
    Fd = dm["F"]
    tm = _pick(S, (1024, 512, 256, 128))
    tn = _pick(Fd, (512, 256, 128))

    def ep(parts, e_refs, o_refs, cols):
        gate = e_refs[0][0, :, cols].astype(F32)
        up = e_refs[0][1, :, cols].astype(F32)
        s = _sigmoid(gate)
        dact = parts[0]
        o_refs[0][0, :, cols] = (dact * up * s * (1.0 + gate * (1.0 - s))).astype(BF16)
        o_refs[0][1, :, cols] = (dact * gate * s).astype(BF16)

    blk = pl.BlockSpec((2, tm, tn), lambda i, j, k: (0, i, j))
    return _result(_matmul(
        "down_bwd", dh_b, [w_down_g], dims=NT, grid=(S // tm, Fd // tn, 1),
        lhs_spec=pl.BlockSpec((tm, D), lambda i, j, k: (i, 0)),
        rhs_specs=[pl.BlockSpec((tn, D), lambda i, j, k: (j, 0))],
        acc_shape=(tm, tn), extra=[gu], extra_specs=[blk],
        out_shape=[jax.ShapeDtypeStruct((2, S, Fd), BF16)], out_specs=[blk], epilogue=ep, carry=carry), 1, carry)


def _gate_up_bwd(dgu, w_gu_g, dm, carry=None):
    S = dgu.shape[1]
    D, Fd = dm["D"], dm["F"]
    cw = 2 * Fd // N_CHIPS
    half = N_CHIPS // 2
    tm = _pick(S, (1024, 512, 256, 128))
    tn = _pick(D, (1024, 512, 256, 128))
    return _result(_matmul(
        "gate_up_bwd", dgu, [w_gu_g], dims=NT, grid=(S // tm, D // tn, N_CHIPS),
        lhs_spec=pl.BlockSpec((None, tm, cw), lambda i, j, k: (k // half, i, k % half)),
        rhs_specs=[pl.BlockSpec((None, tn, cw), lambda i, j, k: (k, j, 0))],
        acc_shape=(tm, tn), out_shape=[jax.ShapeDtypeStruct((S, D), F32)],
        out_specs=[pl.BlockSpec((tm, tn), lambda i, j, k: (i, j))], epilogue=_store_epilogue(F32), carry=carry), 1, carry)


def _out_bwd(dh_b, w_out_g, proj, a_attn, a_sgu, dm, carry=None):
    S, D = dh_b.shape
    OFF_G = dm["OFF_G"]
    tm = _pick(S, (1024, 512, 256, 128))
    tn = D // N_CHIPS

    def ep(parts, e_refs, o_refs, cols):
        dm_ = parts[0]
        ga = _sigmoid(e_refs[0][:, cols].astype(F32))
        gs = _sigmoid(e_refs[1][:, cols].astype(F32))
        o_refs[0][:, cols] = (dm_ * ga).astype(BF16)
        o_refs[1][:, cols] = (dm_ * gs).astype(BF16)
        o_refs[2][0, :, cols] = (dm_ * e_refs[2][:, cols].astype(F32) * ga * (1.0 - ga)).astype(BF16)
        o_refs[2][1, :, cols] = (dm_ * e_refs[3][:, cols].astype(F32) * gs * (1.0 - gs)).astype(BF16)

    blk = pl.BlockSpec((tm, tn), lambda i, j, k: (i, j))
    return _result(_matmul(
        "out_bwd", dh_b, [w_out_g], dims=NT, grid=(S // tm, D // tn, 1),
        lhs_spec=pl.BlockSpec((tm, D), lambda i, j, k: (i, 0)),
        rhs_specs=[pl.BlockSpec((tn, D), lambda i, j, k: (j, 0))],
        acc_shape=(tm, tn), extra=[proj, proj, a_attn, a_sgu],
        extra_specs=[pl.BlockSpec((tm, tn), lambda i, j, k: (i, OFF_G // tn + j)),
                     pl.BlockSpec((tm, tn), lambda i, j, k: (i, (OFF_G + D) // tn + j)), blk, blk],
        out_shape=[jax.ShapeDtypeStruct((S, D), BF16), jax.ShapeDtypeStruct((S, D), BF16),
                   jax.ShapeDtypeStruct((2, S, D), BF16)],
        out_specs=[blk, blk, pl.BlockSpec((2, tm, tn), lambda i, j, k: (0, i, j))], epilogue=ep, carry=carry), 3, carry)


def _colsharded_bwd(name, dy, w_g, out_dtype, carry=None):
    S = dy.shape[0]
    _, K, cw = w_g.shape
    tm = _pick(S, (1024, 512, 256, 128))
    tn = _pick(K, (1024, 512, 256, 128))
    return _result(_matmul(
        name, dy, [w_g], dims=NT, grid=(S // tm, K // tn, N_CHIPS),
        lhs_spec=pl.BlockSpec((tm, cw), lambda i, j, k: (i, k)),
        rhs_specs=[pl.BlockSpec((None, tn, cw), lambda i, j, k: (k, j, 0))],
        acc_shape=(tm, tn), out_shape=[jax.ShapeDtypeStruct((S, K), out_dtype)],
        out_specs=[pl.BlockSpec((tm, tn), lambda i, j, k: (i, j))], epilogue=_store_epilogue(out_dtype),
        carry=carry), 1, carry)


def _wgrad_cols(name, x, dy, carry=None, colsum=False):
    S, R = x.shape
    C = dy.shape[1]
    cw = C // N_CHIPS
    tm = _pick(R, (1024, 512, 256, 128))
    tk = _pick(S, (1024, 512, 256, 128) if cw >= 1024 else (2048, 1024, 512, 256, 128))

    def ep(parts, e_refs, o_refs, cols):
        for o, p in zip(o_refs, parts):
            o[:, cols] = p

    out_shape = [jax.ShapeDtypeStruct((N_CHIPS, R, cw), F32)]
    out_specs = [pl.BlockSpec((None, tm, cw), lambda i, j, k: (j, i, 0))]
    if colsum:
        out_shape.append(jax.ShapeDtypeStruct((R // tm, N_CHIPS, 8, cw), F32))
        out_specs.append(pl.BlockSpec((None, None, 8, cw), lambda i, j, k: (i, j, 0, 0)))
    return _result(_matmul(
        name, x, [dy], dims=TN, grid=(R // tm, N_CHIPS, S // tk),
        lhs_spec=pl.BlockSpec((tk, tm), lambda i, j, k: (k, i)),
        rhs_specs=[pl.BlockSpec((tk, cw), lambda i, j, k: (k, j))],
        acc_shape=(tm, cw), out_shape=out_shape, out_specs=out_specs, epilogue=ep,
        carry=carry, rhs_colsum=colsum), len(out_shape), carry)


def _wgrad_gate_up(hn, dgu, dm, carry=None):
    S, D = hn.shape
    Fd = dm["F"]
    cw = 2 * Fd // N_CHIPS
    half = N_CHIPS // 2
    tm = _pick(D, (1024, 512, 256, 128))
    tk = _pick(S, (1024, 512, 256, 128))
    tn = _pick(cw, (1408, 512, 384, 256, 128))
    nbc = cw // tn
    return _result(_matmul(
        "wgrad_gate_up", hn, [dgu], dims=TN, grid=(D // tm, 2 * Fd // tn, S // tk),
        lhs_spec=pl.BlockSpec((tk, tm), lambda i, j, k: (k, i)),
        rhs_specs=[pl.BlockSpec((None, tk, tn), lambda i, j, k: (j // (half * nbc), k, j % (half * nbc)))],
        acc_shape=(tm, tn), out_shape=[jax.ShapeDtypeStruct((N_CHIPS, D, cw), F32)],
        out_specs=[pl.BlockSpec((None, tm, tn), lambda i, j, k: (j // nbc, i, j % nbc))], epilogue=_store_epilogue(F32),
        carry=carry), 1, carry)


def _wgrad_rows(name, x, dy):
    S, R = x.shape
    C = dy.shape[1]
    rw = R // N_CHIPS
    tn = _pick(C, (1024, 512, 256, 128))
    tk = _pick(S, (1024, 512, 256, 128))
    return _matmul(
        name, x, [dy], dims=TN, grid=(N_CHIPS, C // tn, S // tk),
        lhs_spec=pl.BlockSpec((tk, rw), lambda i, j, k: (k, i)),
        rhs_specs=[pl.BlockSpec((tk, tn), lambda i, j, k: (k, j))],
        acc_shape=(rw, tn), out_shape=[jax.ShapeDtypeStruct((N_CHIPS, rw, C), F32)],
        out_specs=[pl.BlockSpec((None, rw, tn), lambda i, j, k: (i, 0, j))], epilogue=_store_epilogue(F32))[0]


def _place():
    x, y, c = lax.axis_index("x"), lax.axis_index("y"), lax.axis_index("c")
    others = [(1 - x, y), (x, 1 - y), (1 - x, 1 - y)]
    return x, y, c, others


def _chip_index(chip):
    return 2 * chip[0] + chip[1]


def _gather_weights(bufs):
    n = len(bufs)

    def copies(src, out, send_sems, recv_sems):
        x, y, c, others = _place()

        def half(ref, chip_idx, hc):
            r2 = ref.shape[1] // 2
            return ref.at[chip_idx, pl.ds(hc * r2, r2), :]

        def copy(t, k, chip, hc, to):
            return pltpu.make_async_remote_copy(
                src_ref=half(src[t], _chip_index(chip), hc), dst_ref=half(out[t], _chip_index(chip), hc),
                send_sem=send_sems.at[6 * t + k], recv_sem=recv_sems.at[6 * t + k],
                device_id=to, device_id_type=MESH)

        me, sibling = (x, y, c), (x, y, 1 - c)
        pairs = [(t, j, chip) for t in range(n) for j, chip in enumerate(others)]
        sent = [copy(t, j, (x, y), c, (*chip, c)) for t, j, chip in pairs]
        landed = [copy(t, j, chip, c, me) for t, j, chip in pairs]
        passed = [copy(t, 3 + j, chip, c, sibling) for t, j, chip in pairs]
        handed = [copy(t, 3 + j, chip, 1 - c, me) for t, j, chip in pairs]
        return sent, landed, passed, handed

    def start(src, out, send_sems, recv_sems):
        for cp in copies(src, out, send_sems, recv_sems)[0]:
            cp.start()

    def finish(src, out, send_sems, recv_sems):
        sent, landed, passed, handed = copies(src, out, send_sems, recv_sems)
        for arrival, forward in zip(landed, passed):
            arrival.wait_recv()
            forward.start()
        for cp in handed:
            cp.wait_recv()
        for cp in sent + passed:
            cp.wait_send()

    return _Comm("gather_weights", bufs, [jax.ShapeDtypeStruct(b.shape, BF16) for b in bufs],
                 {t: t for t in range(n)}, 6 * n, start, finish)


def _sibling_exchange(grads):
    n = len(grads)
    shapes = [g.shape for g in grads]

    def copies(src, land, send_sems, recv_sems):
        x, y, c, _ = _place()
        res = []
        for t in range(n):
            r2 = shapes[t][1] // 2
            res.append(pltpu.make_async_remote_copy(
                src_ref=src[t].at[:, pl.ds((1 - c) * r2, r2), :], dst_ref=land[t],
                send_sem=send_sems.at[t], recv_sem=recv_sems.at[t], device_id=(x, y, 1 - c), device_id_type=MESH))
        return res

    def start(*refs):
        for cp in copies(*refs):
            cp.start()

    def finish(*refs):
        remote = copies(*refs)
        for cp in remote:
            cp.wait_recv()
        for cp in remote:
            cp.wait_send()

    return _Comm("sibling_exchange", grads, [jax.ShapeDtypeStruct((s[0], s[1] // 2, s[2]), F32) for s in shapes],
                 {}, n, start, finish)


def _chip_exchange(sends):
    n = len(sends)
    shapes = [s.shape for s in sends]

    def copies(snd, got, send_sems, recv_sems):
        x, y, c, others = _place()
        return [pltpu.make_async_remote_copy(
            src_ref=snd[t].at[_chip_index(chip)], dst_ref=got[t].at[j],
            send_sem=send_sems.at[3 * t + j], recv_sem=recv_sems.at[3 * t + j],
            device_id=(*chip, c), device_id_type=MESH) for t in range(n) for j, chip in enumerate(others)]

    def start(*refs):
        for cp in copies(*refs):
            cp.start()

    def finish(*refs):
        remote = copies(*refs)
        for cp in remote:
            cp.wait_recv()
        for cp in remote:
            cp.wait_send()

    return _Comm("chip_exchange", sends, [jax.ShapeDtypeStruct((3, s[1], s[2]), BF16) for s in shapes],
                 {}, 3 * n, start, finish)


def _sibling_share(fulls):
    n = len(fulls)
    shapes = [f.shape for f in fulls]

    def copies(src, out, send_sems, recv_sems, mine):
        x, y, c, _ = _place()
        hc = c if mine else 1 - c
        res = []
        for t in range(n):
            r2 = shapes[t][0] // 2
            res.append(pltpu.make_async_remote_copy(
                src_ref=src[t].at[pl.ds(hc * r2, r2), :], dst_ref=out[t].at[pl.ds(hc * r2, r2), :],
                send_sem=send_sems.at[t], recv_sem=recv_sems.at[t], device_id=(x, y, 1 - c), device_id_type=MESH))
        return res

    def start(*refs):
        for cp in copies(*refs, mine=True):
            cp.start()

    def finish(*refs):
        for cp in copies(*refs, mine=False):
            cp.wait_recv()
        for cp in copies(*refs, mine=True):
            cp.wait_send()

    return _Comm("sibling_share", fulls, [jax.ShapeDtypeStruct(s, F32) for s in shapes],
                 {t: t for t in range(n)}, n, start, finish)


def _gather_all(v):
    R, C = v.shape

    def body(v_ref, out_ref, send_sems, recv_sems, local_sem):
        x, y, c, others = _place()
        me, sibling = (x, y, c), (x, y, 1 - c)

        def rows(px, py, pc):
            return out_ref.at[4 * px + 2 * py + pc]

        def copy(k, block, to, src=None):
            return pltpu.make_async_remote_copy(
                src_ref=rows(*block) if src is None else src, dst_ref=rows(*block),
                send_sem=send_sems.at[k], recv_sem=recv_sems.at[k], device_id=to, device_id_type=MESH)

        mine = pltpu.make_async_copy(v_ref, rows(*me), local_sem)
        mine.start()
        first = [copy(0, me, sibling, src=v_ref)]
        first += [copy(1 + j, me, (*chip, c), src=v_ref) for j, chip in enumerate(others)]
        for cp in first:
            cp.start()
        passed = [copy(4 + j, (*chip, c), sibling) for j, chip in enumerate(others)]
        for j, chip in enumerate(others):
            copy(1 + j, (*chip, c), me).wait_recv()
            passed[j].start()
        copy(0, sibling, me).wait_recv()
        for j, chip in enumerate(others):
            copy(4 + j, (*chip, 1 - c), me).wait_recv()
        for cp in first + passed:
            cp.wait_send()
        mine.wait()

    return pl.pallas_call(
        body, name="gather_all", in_specs=[ANY], out_specs=ANY,
        out_shape=jax.ShapeDtypeStruct((8, R, C), F32),
        scratch_shapes=[pltpu.SemaphoreType.DMA((7,)), pltpu.SemaphoreType.DMA((7,)), pltpu.SemaphoreType.DMA],
    )(v)


def _my_chip():
    return 2 * lax.axis_index("x") + lax.axis_index("y")


def _my_core():
    return lax.axis_index("c")


def _pair_sum(grad, land):
    K, R2, C = land.shape
    tm = _row_tile(R2, C)
    nrb = R2 // tm

    def body(a_ref, b_ref, sb_ref):
        sb_ref[...] = (a_ref[...] + b_ref[...]).astype(BF16)

    blk = pl.BlockSpec((None, tm, C), lambda k, r: (k, r, 0))
    return pl.pallas_call(
        body, name="pair_sum", grid=(K, nrb),
        in_specs=[pl.BlockSpec((None, tm, C), lambda k, r: (k, _my_core() * nrb + r, 0)), blk],
        out_specs=blk, out_shape=jax.ShapeDtypeStruct((K, R2, C), BF16),
        compiler_params=_params(("parallel", "parallel")),
    )(grad, land)


def _chip_sum(grad, land, got):
    _, R2, C = land.shape
    tm = _row_tile(R2, C)
    nrb = R2 // tm

    def body(a_ref, b_ref, g_ref, s_ref):
        own = a_ref[...] + b_ref[...]
        s_ref[...] = ((own + g_ref[0].astype(F32)) + g_ref[1].astype(F32)) + g_ref[2].astype(F32)

    return pl.pallas_call(
        body, name="chip_sum", grid=(nrb,),
        in_specs=[pl.BlockSpec((None, tm, C), lambda r: (_my_chip(), _my_core() * nrb + r, 0)),
                  pl.BlockSpec((None, tm, C), lambda r: (_my_chip(), r, 0)),
                  pl.BlockSpec((3, tm, C), lambda r: (0, r, 0))],
        out_specs=pl.BlockSpec((tm, C), lambda r: (_my_core() * nrb + r, 0)),
        out_shape=jax.ShapeDtypeStruct((2 * R2, C), F32),
        compiler_params=_params(("parallel",)),
    )(grad, land, got)


def _adamw_math(w, g, m, v):
    m = ADAM_B1 * m + (1.0 - ADAM_B1) * g
    v = ADAM_B2 * v + (1.0 - ADAM_B2) * (g * g)
    m_hat = m / (1.0 - ADAM_B1 ** ADAM_STEP)
    v_hat = v / (1.0 - ADAM_B2 ** ADAM_STEP)
    delta = -ADAM_LR * (m_hat / (jnp.sqrt(v_hat) + ADAM_EPS) + ADAM_WD * w)
    return delta, m, v


def _adamw_stacked(grads, w, m, v, carry=None):
    L, R, C = w.shape
    tm = _row_tile(R, C)
    nrb = R // tm
    nci = len(carry.ins) if carry else 0
    nco = len(carry.outs) if carry else 0

    def body(*refs):
        g_refs = refs[:L]
        w_ref, m_ref, v_ref = refs[L:L + 3]
        ci_refs = refs[L + 3:L + 3 + nci]
        go_ref, d_ref, mo_ref, vo_ref = refs[L + 3 + nci:L + 7 + nci]
        co_refs = refs[L + 7 + nci:L + 7 + nci + nco]
        sems = refs[L + 7 + nci + nco:]
        l, r = pl.program_id(0), pl.program_id(1)
        if carry:
            @pl.when((l == 0) & (r == 0))
            def _():
                carry.start(ci_refs, co_refs, *sems)

        for ll in range(L):
            @pl.when(l == ll)
            def _(ll=ll):
                g = g_refs[ll][...]
                delta, mn, vn = _adamw_math(w_ref[...], g, m_ref[...], v_ref[...])
                go_ref[...] = g
                d_ref[...] = delta
                mo_ref[...] = mn
                vo_ref[...] = vn

        if carry:
            @pl.when((l == L - 1) & (r == nrb - 1))
            def _():
                carry.finish(ci_refs, co_refs, *sems)

    stacked = pl.BlockSpec((None, tm, C), lambda l, r: (l, r, 0))
    g_specs = [pl.BlockSpec((tm, C), lambda l, r, ll=ll: (jnp.where(l == ll, r, 0), 0)) for ll in range(L)]
    shp = jax.ShapeDtypeStruct((L, R, C), F32)
    outs = pl.pallas_call(
        body, name="adamw", grid=(L, nrb),
        in_specs=[*g_specs, stacked, stacked, stacked, *([ANY] * nci)],
        out_specs=[*([stacked] * 4), *([ANY] * nco)], out_shape=[*([shp] * 4), *(carry.outs if carry else [])],
        scratch_shapes=carry.sem_scratch() if carry else [],
        input_output_aliases={L + 3 + i: 4 + o for i, o in carry.aliases.items()} if carry else {},
        compiler_params=_params(("arbitrary", "arbitrary")),
    )(*grads, w, m, v, *(carry.ins if carry else []))
    return (outs[:4], list(outs[4:])) if carry else outs


def _adamw_small(parts, w, m, v):
    _, R, C = parts.shape
    tm = _row_tile(R, 8 * C)

    def body(p_ref, w_ref, m_ref, v_ref, go_ref, d_ref, mo_ref, vo_ref):
        g = p_ref[0]
        for k in range(1, 8):
            g = g + p_ref[k]
        delta, mn, vn = _adamw_math(w_ref[...], g, m_ref[...], v_ref[...])
        go_ref[...] = g
        d_ref[...] = delta
        mo_ref[...] = mn
        vo_ref[...] = vn

    blk = pl.BlockSpec((tm, C), lambda i: (i, 0))
    shp = jax.ShapeDtypeStruct((R, C), F32)
    return pl.pallas_call(
        body, name="adamw_small", grid=(R // tm,),
        in_specs=[pl.BlockSpec((8, tm, C), lambda i: (0, i, 0)), blk, blk, blk],
        out_specs=[blk] * 4, out_shape=[shp] * 4,
        compiler_params=_params(("parallel",)),
    )(parts, w, m, v)


def _cast_place(w, layer):
    _, R, C = w.shape
    tm = _row_tile(R, C)

    def body(w_ref, o_ref):
        o_ref[...] = w_ref[...].astype(BF16)

    return pl.pallas_call(
        body, name="cast_place", grid=(R // tm,),
        in_specs=[pl.BlockSpec((None, tm, C), lambda r: (layer, r, 0))],
        out_specs=pl.BlockSpec((None, tm, C), lambda r: (_my_chip(), r, 0)),
        out_shape=jax.ShapeDtypeStruct((N_CHIPS, R, C), BF16),
        compiler_params=_params(("parallel",)),
    )(w)


def _trig_tables(positions):
    inv_freq = ROPE_THETA ** (-jnp.arange(0, ROPE_DIM, 2, dtype=F32) / ROPE_DIM)
    ang = positions.astype(F32)[:, None] * inv_freq
    cos, sin = jnp.cos(ang), jnp.sin(ang)
    S = positions.shape[0]
    cos_h = jnp.concatenate([cos, cos, jnp.ones((S, HEAD_DIM - ROPE_DIM), F32)], axis=1)
    sin_h = jnp.concatenate([-sin, sin, jnp.zeros((S, HEAD_DIM - ROPE_DIM), F32)], axis=1)
    rep = LANES // HEAD_DIM
    return [jnp.tile(t, (1, rep)) for t in (cos_h, sin_h)]


def _row(vec):
    return vec.reshape(1, -1)


def _lane_row(vec):
    return jnp.zeros((8, LANES), F32).at[0, :vec.shape[0]].set(vec)


def _pack(pieces, rows):
    flat = jnp.concatenate([p.reshape(-1).astype(F32) for p in pieces])
    return jnp.pad(flat, (0, rows * LANES - flat.shape[0])).reshape(rows, LANES)


def kernel(x, positions, norm1_g, w_in, b_in, sinks, sgu_ln_g, sgu_ln_b, sgu_w, sgu_b, w_attn_branch, w_sgu_branch, w_out, norm2_g, w_gate_up, w_down, final_g, loss_target, m_norm1_g, m_w_in, m_b_in, m_sinks, m_sgu_ln_g, m_sgu_ln_b, m_sgu_w, m_sgu_b, m_w_attn_branch, m_w_sgu_branch, m_w_out, m_norm2_g, m_w_gate_up, m_w_down, m_final_g, v_norm1_g, v_w_in, v_b_in, v_sinks, v_sgu_ln_g, v_sgu_ln_b, v_sgu_w, v_sgu_b, v_w_attn_branch, v_w_sgu_branch, v_w_out, v_norm2_g, v_w_gate_up, v_w_down, v_final_g):
    L = norm1_g.shape[0]
    S, D = x.shape[1], x.shape[2]
    NQ = sinks.shape[1]
    A = NQ * HEAD_DIM
    KV = N_KV_HEADS * HEAD_DIM
    G = sgu_ln_g.shape[1]
    NG = sgu_w.shape[1]
    IN = b_in.shape[1]
    Fd = w_down.shape[1] * N_CHIPS
    dm = dict(D=D, A=A, KV=KV, NQ=NQ, G=G, NG=NG, IN=IN, F=Fd,
              OFF_K=A, OFF_V=A + KV, OFF_Z=A + 2 * KV, OFF_G=A + 2 * KV + 2 * G)
    assert sgu_w.shape[2] == WINDOW and G == NG * LANES and IN == dm["OFF_G"] + 2 * D

    h = x[0]
    target = loss_target[0]
    trig = _trig_tables(positions[0])
    tril = jnp.tril(jnp.ones((WINDOW, WINDOW), bool))

    big = [w_in, w_attn_branch, w_sgu_branch, w_out, w_gate_up, w_down]
    big_m = [m_w_in, m_w_attn_branch, m_w_sgu_branch, m_w_out, m_w_gate_up, m_w_down]
    big_v = [v_w_in, v_w_attn_branch, v_w_sgu_branch, v_w_out, v_w_gate_up, v_w_down]

    placed = [[_cast_place(w, l) for w in big] for l in range(L)]
    IN_, AB, SB, OUT, GU, DOWN = range(len(big))
    gathered = [[None] * len(big) for _ in range(L)]
    gathered[0][IN_] = _gather_weights([placed[0][IN_]]).run()[0]

    def fetch(layer, idx):
        return _gather_weights([placed[layer][t] for t in idx]) if layer < L else None

    def fetched(layer, idx, res):
        if layer >= L:
            return res
        main, got = res
        for t, g in zip(idx, got):
            gathered[layer][t] = g
        return main

    def weights(l):
        flat = lambda w, rows: None if w is None else w.reshape(rows, D)
        w_in_g, w_ab_g, w_sb_g, w_out_g, w_gu_g, w_down_g = gathered[l]
        return (w_in_g, w_ab_g, w_sb_g, flat(w_out_g, D), w_gu_g, flat(w_down_g, Fd))

    def small(l):
        return dict(
            g1=_row(norm1_g[l]), b_in=_row(b_in[l]), sink=_lane_row(sinks[l]),
            ln_g=_row(sgu_ln_g[l]), ln_b=_row(sgu_ln_b[l]),
            w_tril=jnp.where(tril[None], sgu_w[l], 0.0).astype(BF16),
            b_t=jnp.zeros((WINDOW, LANES), F32).at[:, :NG].set(sgu_b[l].T),
            g2=_row(norm2_g[l]))

    saved = []
    for l in range(L):
        sp = small(l)
        xn = _rms_fwd(h, sp["g1"])
        now = [AB, SB, OUT, GU] if l == 0 else [DOWN]
        proj = fetched(l, now, _in_proj(xn, gathered[l][IN_], sp["b_in"], trig, dm, carry=fetch(l, now)))
        w_in_g, w_ab_g, w_sb_g, w_out_g = weights(l)[:4]
        y_attn, lse = _attn_fwd(proj, sp["sink"], dm)
        y_sgu = _sgu_fwd(proj, sp["w_tril"], sp["b_t"], sp["ln_g"], sp["ln_b"], dm)
        a_attn = _branch_attn(y_attn, w_ab_g, dm)
        a_sgu, merged = _branch_sgu_merge(y_sgu, w_sb_g, a_attn, proj, dm)
        if l == 0:
            h_mid = fetched(l, [DOWN], _residual_matmul("out_proj", merged, w_out_g, h, carry=fetch(l, [DOWN])))
        else:
            h_mid = _residual_matmul("out_proj", merged, w_out_g, h)
        w_gu_g, w_down_g = weights(l)[4:]
        hn = _rms_fwd(h_mid, sp["g2"])
        ahead = [IN_, AB, SB, OUT]
        gu, act = fetched(l + 1, ahead, _gate_up(hn, w_gu_g, dm, carry=fetch(l + 1, ahead)))
        h_out = fetched(l + 1, [GU], _residual_matmul("down_proj", act, w_down_g, h_mid, carry=fetch(l + 1, [GU])))
        saved.append(dict(h=h, xn=xn, proj=proj, y_attn=y_attn, lse=lse, y_sgu=y_sgu, a_attn=a_attn, a_sgu=a_sgu,
                          merged=merged, h_mid=h_mid, hn=hn, gu=gu, act=act))
        h = h_out

    dh, dh_b, d_final, loss_part = _loss_head(h, _row(final_g), target)

    small_grads = [None] * L
    reduced = [[None] * len(big) for _ in range(L)]
    early, mid, late = [GU, DOWN], [AB, SB, OUT], [IN_]

    def riding(has_carry, res):
        return res if has_carry else (res, None)

    def sends_of(grads, land):
        return [_pair_sum(g, d) for g, d in zip(grads, land)]

    def finished(grads, land, got):
        return [_chip_sum(g, d, p) for g, d, p in zip(grads, land, got)]

    def file_reduced(layer, idx, fulls):
        for t, f in zip(idx, fulls):
            reduced[layer][t] = f

    late_grads = None
    mid_fulls = None
    n_late, n_mid, n_early = len(late), len(mid), len(early)
    for l in reversed(range(L)):
        w_in_g, w_ab_g, w_sb_g, w_out_g, w_gu_g, w_down_g = weights(l)
        sp, sv = small(l), saved[l]
        have = late_grads is not None
        dgu, rode = riding(have, _down_bwd(
            dh_b, w_down_g, sv["gu"], dm,
            carry=_sibling_exchange(late_grads).beside(_sibling_share(mid_fulls)) if have else None))
        if have:
            land = rode[:n_late]
            file_reduced(l + 1, mid, rode[n_late:])
        g_down = _wgrad_rows("wgrad_down", sv["act"], dh_b)
        dhn, got = riding(have, _gate_up_bwd(dgu, w_gu_g, dm,
                                             carry=_chip_exchange(sends_of(late_grads, land)) if have else None))
        g_gu, shared = riding(have, _wgrad_gate_up(sv["hn"], dgu, dm,
                                                   carry=_sibling_share(finished(late_grads, land, got)) if have else None))
        if have:
            file_reduced(l + 1, late, shared)
        dh_mid, dh_mid_b, d_g2 = _rms_bwd(dhn, sv["h_mid"], sp["g2"], dh)
        early_grads = [g_gu, g_down]
        (da_attn, da_sgu, dgate), land_e = _out_bwd(dh_mid_b, w_out_g, sv["proj"], sv["a_attn"], sv["a_sgu"], dm,
                                                     carry=_sibling_exchange(early_grads))
        sends_e = sends_of(early_grads, land_e)
        g_out = _wgrad_rows("wgrad_out", sv["merged"], dh_mid_b)
        dy_attn = _colsharded_bwd("branch_attn_bwd", da_attn, w_ab_g, BF16)
        dy_sgu = _colsharded_bwd("branch_sgu_bwd", da_sgu, w_sb_g, BF16)
        g_ab = _wgrad_cols("wgrad_attn_branch", sv["y_attn"], da_attn)
        g_sb = _wgrad_cols("wgrad_sgu_branch", sv["y_sgu"], da_sgu)
        mid_grads = [g_ab, g_sb, g_out]
        dq, dk, dv, d_sink = _attn_bwd(sv["proj"], trig, sp["sink"], sv["y_attn"], sv["lse"], dy_attn, dm)
        dz, d_sgu_w, d_bt, d_lng, d_lnb = _sgu_bwd(sv["proj"], sp["w_tril"], sp["b_t"], sp["ln_g"], sp["ln_b"], dy_sgu, dm)
        dproj = jnp.concatenate([dq, dk, dv, dz, dgate[0], dgate[1]], axis=1)
        dxn, rode = _colsharded_bwd("in_proj_bwd", dproj, w_in_g, F32,
                                    carry=_chip_exchange(sends_e).beside(_sibling_exchange(mid_grads)))
        got_e, land_m = rode[:n_early], rode[n_early:]
        (g_in, d_bin), rode = _wgrad_cols(
            "wgrad_in", sv["xn"], dproj, colsum=True,
            carry=_sibling_share(finished(early_grads, land_e, got_e)).beside(_chip_exchange(sends_of(mid_grads, land_m))))
        file_reduced(l, early, rode[:n_early])
        mid_fulls = finished(mid_grads, land_m, rode[n_early:])
        dh, dh_b, d_g1 = _rms_bwd(dxn, sv["h"], sp["g1"], dh_mid)
        late_grads = [g_in]
        small_grads[l] = dict(norm1_g=d_g1[0], b_in=d_bin[0, :, 0, :].reshape(-1), sinks=d_sink[0, :NQ],
                              sgu_ln_g=d_lng[0], sgu_ln_b=d_lnb[0], sgu_w=d_sgu_w, sgu_b=d_bt[:, :NG].T, norm2_g=d_g2[0])
    grad_x = dh[None]

    land = _sibling_exchange(late_grads).run()
    got = _chip_exchange(sends_of(late_grads, land)).run()
    shared = _sibling_share(finished(late_grads, land, got) + mid_fulls).run()
    file_reduced(0, late, shared[:n_late])
    file_reduced(0, mid, shared[n_late:])
    big_out = [_adamw_stacked([reduced[l][t] for l in range(L)], big[t], big_m[t], big_v[t]) for t in range(len(big))]

    names = ["norm1_g", "b_in", "sinks", "sgu_ln_g", "sgu_ln_b", "sgu_w", "sgu_b", "norm2_g"]
    small_w = [norm1_g, b_in, sinks, sgu_ln_g, sgu_ln_b, sgu_w, sgu_b, norm2_g, final_g]
    small_m = [m_norm1_g, m_b_in, m_sinks, m_sgu_ln_g, m_sgu_ln_b, m_sgu_w, m_sgu_b, m_norm2_g, m_final_g]
    small_v = [v_norm1_g, v_b_in, v_sinks, v_sgu_ln_g, v_sgu_ln_b, v_sgu_w, v_sgu_b, v_norm2_g, v_final_g]
    small_g = [jnp.stack([small_grads[l][nm] for l in range(L)]) for nm in names] + [d_final[0]]
    sizes = [w.size for w in small_w]
    total = sum(sizes) + 1
    rows = -(-total // (512 * LANES)) * 512
    loss_piece = jnp.sum(loss_part[0]).reshape(1)
    packed_g = _pack(small_g + [loss_piece], rows)
    one = jnp.ones((1,), F32)
    parts = _gather_all(packed_g)
    outs = _adamw_small(parts, _pack(small_w + [one], rows), _pack(small_m + [one], rows), _pack(small_v + [one], rows))

    def unpack(p):
        flat = p.reshape(-1)
        res, off = [], 0
        for w, n in zip(small_w, sizes):
            res.append(flat[off:off + n].reshape(w.shape))
            off += n
        return res, flat[off]

    (sg, loss), (sd, _), (smm, _), (svv, _) = [unpack(o) for o in outs]

    order = ["norm1_g", "w_in", "b_in", "sinks", "sgu_ln_g", "sgu_ln_b", "sgu_w", "sgu_b", "w_attn_branch",
             "w_sgu_branch", "w_out", "norm2_g", "w_gate_up", "w_down", "final_g"]
    big_names = ["w_in", "w_attn_branch", "w_sgu_branch", "w_out", "w_gate_up", "w_down"]
    small_names = names + ["final_g"]

    def collect(kind):
        res = []
        for nm in order:
            if nm in big_names:
                res.append(big_out[big_names.index(nm)][kind])
            else:
                res.append((sg, sd, smm, svv)[kind][small_names.index(nm)])
        return res

    return (loss, grad_x, *collect(0), *collect(1), *collect(2), *collect(3))
```

```python
import math

import jax
import jax.numpy as jnp
from jax import lax
from jax.experimental import pallas as pl
from jax.experimental.pallas import tpu as pltpu

F32 = jnp.float32
BF16 = jnp.bfloat16
MESH = pl.DeviceIdType.MESH
ANY = pl.BlockSpec(memory_space=pl.ANY)

HEAD_DIM = 64
N_KV_HEADS = 4
WINDOW = 128
ROPE_DIM = HEAD_DIM // 4
ROPE_THETA = 500000.0
EPS = 1e-5
NEG = -1e30
N_CHIPS = 4
LANES = 128
V7X_VMEM_LIMIT = 56 * 1024 * 1024

ADAM_LR = 0.001
ADAM_B1 = 0.9
ADAM_B2 = 0.999
ADAM_EPS = 1e-08
ADAM_WD = 0.01
ADAM_STEP = 10

NN = (((1,), (0,)), ((), ()))
NT = (((1,), (1,)), ((), ()))
TN = (((0,), (0,)), ((), ()))


ROW_TILES = (1024, 512, 256, 128, 64, 32, 16, 8)
BLOCK_BYTES = 2 * 1024 * 1024


def _pick(n, prefs):
    for p in prefs:
        if n % p == 0:
            return p
    raise ValueError(f"no tile for {n} among {prefs}")


def _row_tile(rows, cols, itemsize=4):
    return _pick(rows, [t for t in ROW_TILES if t * cols * itemsize <= BLOCK_BYTES or t == ROW_TILES[-1]])


def _dot(a, b, dims):
    return lax.dot_general(a, b, dims, preferred_element_type=F32)


def _sigmoid(x):
    return 1.0 / (1.0 + jnp.exp(-x))


def _gelu(x):
    return 0.5 * x * (1.0 + lax.erf(x * (1.0 / math.sqrt(2.0))))


def _gelu_grad(x):
    return 0.5 * (1.0 + lax.erf(x * (1.0 / math.sqrt(2.0)))) + x * jnp.exp(-0.5 * x * x) * (1.0 / math.sqrt(2.0 * math.pi))


def _params(sem):
    return pltpu.CompilerParams(dimension_semantics=sem, vmem_limit_bytes=V7X_VMEM_LIMIT)


def _matmul(name, lhs, rhs_list, *, dims, grid, lhs_spec, rhs_specs, acc_shape, out_shape, out_specs,
            epilogue, extra=(), extra_specs=(), carry=None, rhs_colsum=False, cols_outer=False, extra_aliases=None):
    if cols_outer:
        swap = lambda s: s if s.index_map is None else pl.BlockSpec(s.block_shape, lambda j, i, k, f=s.index_map: f(i, j, k))
        grid = (grid[1], grid[0], grid[2])
        lhs_spec, rhs_specs = swap(lhs_spec), [swap(s) for s in rhs_specs]
        extra_specs, out_specs = [swap(s) for s in extra_specs], [swap(s) for s in out_specs]
    gk = grid[2]
    nr, ne, no = len(rhs_list), len(extra), len(out_shape)
    nci = len(carry.ins) if carry else 0
    nco = len(carry.outs) if carry else 0
    acc_shapes = [acc_shape] * nr + ([(8, acc_shape[1])] if rhs_colsum else [])
    nacc = len(acc_shapes) if gk > 1 else 0

    def body(*refs):
        a_ref = refs[0]
        b_refs = refs[1:1 + nr]
        e_refs = refs[1 + nr:1 + nr + ne]
        base = 1 + nr + ne
        ci_refs = refs[base:base + nci]
        o_refs = refs[base + nci:base + nci + no]
        co_refs = refs[base + nci + no:base + nci + no + nco]
        acc_refs = refs[base + nci + no + nco:base + nci + no + nco + nacc]
        sems = refs[base + nci + no + nco + nacc:]
        ids = [pl.program_id(d) for d in range(3)]
        if carry:
            @pl.when((ids[0] == 0) & (ids[1] == 0) & (ids[2] == 0))
            def _():
                carry.start(ci_refs, co_refs, *sems)

        a = a_ref[...]
        if gk == 1:
            n_axis = 1 - dims[0][1][0]
            for cols in _col_chunks(acc_shape[1]):
                pick = (slice(None), cols) if n_axis == 1 else (cols, slice(None))
                parts = [_dot(a, b[pick], dims) for b in b_refs]
                if rhs_colsum:
                    b0 = b_refs[0][pick]
                    parts.append(_dot(jnp.ones((8, b0.shape[0]), b0.dtype), b0, NN))
                epilogue(parts, e_refs, o_refs, cols)
        else:
            k = ids[2]

            @pl.when(k == 0)
            def _():
                for acc in acc_refs:
                    acc[...] = jnp.zeros_like(acc)

            for acc, b in zip(acc_refs, b_refs):
                acc[...] += _dot(a, b[...], dims)
            if rhs_colsum:
                b0 = b_refs[0][...]
                acc_refs[-1][...] += _dot(jnp.ones((8, b0.shape[0]), b0.dtype), b0, NN)

            @pl.when(k == gk - 1)
            def _():
                epilogue([acc[...] for acc in acc_refs], e_refs, o_refs, slice(None))

        if carry:
            @pl.when((ids[0] == grid[0] - 1) & (ids[1] == grid[1] - 1) & (ids[2] == grid[2] - 1))
            def _():
                carry.finish(ci_refs, co_refs, *sems)

    scratch = [pltpu.VMEM(s, F32) for s in acc_shapes[:nacc]]
    kwargs = {}
    aliases = {1 + nr + e: o for e, o in (extra_aliases or {}).items()}
    if carry:
        scratch += carry.sem_scratch()
        aliases.update({1 + nr + ne + i: no + o for i, o in carry.aliases.items()})
    if aliases:
        kwargs["input_output_aliases"] = aliases
    outs = pl.pallas_call(
        body, name=name, grid=grid,
        in_specs=[lhs_spec, *rhs_specs, *extra_specs, *([ANY] * nci)],
        out_specs=[*out_specs, *([ANY] * nco)],
        out_shape=[*out_shape, *(carry.outs if carry else [])], scratch_shapes=scratch,
        compiler_params=_params(("arbitrary",) * 3 if carry else ("parallel", "parallel", "arbitrary")),
        **kwargs,
    )(lhs, *rhs_list, *extra, *(carry.ins if carry else []))
    return outs


class _Comm:
    def __init__(self, name, ins, outs, aliases, n_sems, start, finish):
        self.name, self.ins, self.outs, self.aliases, self.n_sems = name, list(ins), list(outs), dict(aliases), n_sems
        self.start, self.finish = start, finish

    def sem_scratch(self):
        return [pltpu.SemaphoreType.DMA((self.n_sems,)), pltpu.SemaphoreType.DMA((self.n_sems,))]

    def beside(self, other):
        ni, no, ns = len(self.ins), len(self.outs), self.n_sems

        def split(ins, outs, send_sems, recv_sems):
            mine = (ins[:ni], outs[:no], send_sems.at[pl.ds(0, ns)], recv_sems.at[pl.ds(0, ns)])
            theirs = (ins[ni:], outs[no:], send_sems.at[pl.ds(ns, other.n_sems)], recv_sems.at[pl.ds(ns, other.n_sems)])
            return mine, theirs

        def start(*refs):
            mine, theirs = split(*refs)
            self.start(*mine)
            other.start(*theirs)

        def finish(*refs):
            mine, theirs = split(*refs)
            self.finish(*mine)
            other.finish(*theirs)

        aliases = {**self.aliases, **{ni + i: no + o for i, o in other.aliases.items()}}
        return _Comm(self.name + "+" + other.name, self.ins + other.ins, self.outs + other.outs, aliases,
                     ns + other.n_sems, start, finish)

    def run(self):
        ni = len(self.ins)

        def body(*refs):
            in_refs, out_refs, sems = refs[:ni], refs[ni:ni + len(self.outs)], refs[ni + len(self.outs):]
            self.start(in_refs, out_refs, *sems)
            self.finish(in_refs, out_refs, *sems)

        return pl.pallas_call(
            body, name=self.name, in_specs=[ANY] * ni, out_specs=[ANY] * len(self.outs), out_shape=self.outs,
            input_output_aliases=self.aliases, scratch_shapes=self.sem_scratch(),
        )(*self.ins)


MXU_CHUNK = 256


def _col_chunks(n):
    if n % LANES:
        return [slice(0, n)]
    return [slice(s, min(s + MXU_CHUNK, n)) for s in range(0, n, MXU_CHUNK)]


def _store_epilogue(dtype):
    def ep(parts, e_refs, o_refs, cols):
        o_refs[0][:, cols] = parts[0].astype(dtype)
    return ep


def _rms_fwd(h, g_row):
    S, D = h.shape
    tm = _row_tile(S, D)

    def body(h_ref, g_ref, o_ref):
        x = h_ref[...]
        r = lax.rsqrt(jnp.mean(x * x, axis=-1, keepdims=True) + EPS)
        o_ref[...] = (x * r * g_ref[...]).astype(BF16)

    return pl.pallas_call(
        body, name="rms_fwd", grid=(S // tm,),
        in_specs=[pl.BlockSpec((tm, D), lambda i: (i, 0)), pl.BlockSpec((1, D), lambda i: (0, 0))],
        out_specs=pl.BlockSpec((tm, D), lambda i: (i, 0)),
        out_shape=jax.ShapeDtypeStruct((S, D), BF16),
        compiler_params=_params(("parallel",)),
    )(h, g_row)


def _rms_bwd(dy, h, g_row, dres):
    S, D = h.shape
    tm = _row_tile(S, D)

    def body(dy_ref, h_ref, g_ref, dres_ref, dh_ref, dhb_ref, dg_ref):
        i = pl.program_id(0)
        x = h_ref[...]
        d = dy_ref[...]
        r = lax.rsqrt(jnp.mean(x * x, axis=-1, keepdims=True) + EPS)
        dg = d * g_ref[...]
        dot = jnp.mean(dg * x, axis=-1, keepdims=True)
        dh = dres_ref[...] + r * dg - x * (r * r * r) * dot
        dh_ref[...] = dh
        dhb_ref[...] = dh.astype(BF16)
        part = jnp.sum(d * x * r, axis=0, keepdims=True)

        @pl.when(i == 0)
        def _():
            dg_ref[...] = jnp.zeros_like(dg_ref)

        dg_ref[0:1, :] += part

    return pl.pallas_call(
        body, name="rms_bwd", grid=(S // tm,),
        in_specs=[pl.BlockSpec((tm, D), lambda i: (i, 0)), pl.BlockSpec((tm, D), lambda i: (i, 0)),
                  pl.BlockSpec((1, D), lambda i: (0, 0)), pl.BlockSpec((tm, D), lambda i: (i, 0))],
        out_specs=[pl.BlockSpec((tm, D), lambda i: (i, 0)), pl.BlockSpec((tm, D), lambda i: (i, 0)),
                   pl.BlockSpec((8, D), lambda i: (0, 0))],
        out_shape=[jax.ShapeDtypeStruct((S, D), F32), jax.ShapeDtypeStruct((S, D), BF16),
                   jax.ShapeDtypeStruct((8, D), F32)],
        compiler_params=_params(("arbitrary",)),
    )(dy, h, g_row, dres)


def _loss_head(h, g_row, target):
    S, D = h.shape
    tm = _row_tile(S, D)

    def body(h_ref, g_ref, t_ref, dh_ref, dhb_ref, dg_ref, loss_ref):
        i = pl.program_id(0)
        x = h_ref[...]
        g = g_ref[...]
        r = lax.rsqrt(jnp.mean(x * x, axis=-1, keepdims=True) + EPS)
        y = x * r * g
        e = y - t_ref[...]
        d = e * (1.0 / D)
        dg = d * g
        dot = jnp.mean(dg * x, axis=-1, keepdims=True)
        dh = r * dg - x * (r * r * r) * dot
        dh_ref[...] = dh
        dhb_ref[...] = dh.astype(BF16)

        @pl.when(i == 0)
        def _():
            dg_ref[...] = jnp.zeros_like(dg_ref)
            loss_ref[...] = jnp.zeros_like(loss_ref)

        dg_ref[0:1, :] += jnp.sum(d * x * r, axis=0, keepdims=True)
        loss_ref[0:1, :] += jnp.sum((0.5 / D) * e * e, axis=0, keepdims=True)

    return pl.pallas_call(
        body, name="loss_head", grid=(S // tm,),
        in_specs=[pl.BlockSpec((tm, D), lambda i: (i, 0)), pl.BlockSpec((1, D), lambda i: (0, 0)),
                  pl.BlockSpec((tm, D), lambda i: (i, 0))],
        out_specs=[pl.BlockSpec((tm, D), lambda i: (i, 0)), pl.BlockSpec((tm, D), lambda i: (i, 0)),
                   pl.BlockSpec((8, D), lambda i: (0, 0)), pl.BlockSpec((8, D), lambda i: (0, 0))],
        out_shape=[jax.ShapeDtypeStruct((S, D), F32), jax.ShapeDtypeStruct((S, D), BF16),
                   jax.ShapeDtypeStruct((8, D), F32), jax.ShapeDtypeStruct((8, D), F32)],
        compiler_params=_params(("arbitrary",)),
    )(h, g_row, target)


def _rotary_partner(t):
    half = ROPE_DIM // 2
    r = lax.broadcasted_iota(jnp.int32, (LANES, LANES), 0)
    c = lax.broadcasted_iota(jnp.int32, (LANES, LANES), 1)
    cm = c & (HEAD_DIM - 1)
    perm = (((cm < half) & (r == c + half)) | ((cm >= half) & (cm < ROPE_DIM) & (r == c - half))).astype(BF16)
    hi = t.astype(BF16)
    lo = (t - hi.astype(F32)).astype(BF16)
    cols = [slice(s, s + LANES) for s in range(0, t.shape[-1], LANES)]
    return jnp.concatenate([_dot(hi[:, c_], perm, NN) + _dot(lo[:, c_], perm, NN) for c_ in cols], axis=1)


def _rope(t, cos, sin):
    return t * cos + _rotary_partner(t) * sin


def _rope_t(g, cos, sin):
    return g * cos + _rotary_partner(g * sin)


def _band_mask(n, qpk):
    qi = lax.broadcasted_iota(jnp.int32, (qpk * WINDOW, 2 * WINDOW), 0) & (WINDOW - 1)
    kj = lax.broadcasted_iota(jnp.int32, (qpk * WINDOW, 2 * WINDOW), 1)
    rel = qi + WINDOW - kj
    ok = (rel >= 0) & (rel < WINDOW)
    return ok & ((kj >= WINDOW) | (n > 0))


def _stack_heads(x, g, qpk):
    return jnp.concatenate([x[:, (g * qpk + hh) * HEAD_DIM:(g * qpk + hh + 1) * HEAD_DIM] for hh in range(qpk)], axis=0)


def _stack_cols(row, g, qpk):
    return jnp.concatenate([row[:, g * qpk + hh:g * qpk + hh + 1] for hh in range(qpk)], axis=0)


def _attn_specs(dm, nb):
    A, KV = dm["A"], dm["KV"]
    kb, vb = dm["OFF_K"] // KV, dm["OFF_V"] // KV
    cur = lambda n: jnp.minimum(n, nb - 1)
    prev = lambda n: jnp.maximum(jnp.minimum(n, nb - 1) - 1, 0)
    proj_specs = [
        pl.BlockSpec((WINDOW, A), lambda n: (cur(n), 0)),
        pl.BlockSpec((WINDOW, KV), lambda n: (prev(n), kb)),
        pl.BlockSpec((WINDOW, KV), lambda n: (cur(n), kb)),
        pl.BlockSpec((WINDOW, KV), lambda n: (prev(n), vb)),
        pl.BlockSpec((WINDOW, KV), lambda n: (cur(n), vb)),
    ]
    trig_cur = [pl.BlockSpec((WINDOW, LANES), lambda n: (cur(n), 0)) for _ in range(2)]
    trig_prev = [pl.BlockSpec((WINDOW, LANES), lambda n: (prev(n), 0)) for _ in range(2)]
    return proj_specs, trig_cur, trig_prev, cur, prev


def _attn_fwd(proj, sink_row, dm):
    S = proj.shape[0]
    A, KV, NQ = dm["A"], dm["KV"], dm["NQ"]
    qpk = NQ // N_KV_HEADS
    nb = S // WINDOW
    scale = HEAD_DIM ** -0.5
    proj_specs = _attn_specs(dm, nb)[0]

    def body(q_ref, kp_ref, kc_ref, vp_ref, vc_ref, sink_ref, y_ref, lse_ref):
        n = pl.program_id(0)
        qr = q_ref[...]
        kr = jnp.concatenate([kp_ref[...], kc_ref[...]], axis=0)
        vband = jnp.concatenate([vp_ref[...], vc_ref[...]], axis=0)
        mask = _band_mask(n, qpk)
        lane = lax.broadcasted_iota(jnp.int32, (WINDOW, LANES), 1)
        lse_all = jnp.zeros((WINDOW, LANES), F32)
        sink_rows = jnp.broadcast_to(sink_ref[0:1, :], (WINDOW, LANES))
        groups = range(N_KV_HEADS)
        head = lambda x, g: x[:, g * HEAD_DIM:(g + 1) * HEAD_DIM]
        ones = jnp.ones((2 * WINDOW, HEAD_DIM), BF16)
        sink = [_stack_cols(sink_rows, g, qpk) for g in groups]
        s = [jnp.where(mask, _dot(_stack_heads(qr, g, qpk), head(kr, g), NT) * scale, NEG) for g in groups]
        m = [jnp.maximum(jnp.max(s[g], axis=-1, keepdims=True), sink[g]) for g in groups]
        p = [jnp.exp(s[g] - m[g]).astype(BF16) for g in groups]
        ov = [_dot(p[g], jnp.concatenate([head(vband, g), ones], axis=1), NN) for g in groups]
        den = [ov[g][:, HEAD_DIM:HEAD_DIM + 1] + jnp.exp(sink[g] - m[g]) for g in groups]
        o = [ov[g][:, :HEAD_DIM] * (1.0 / den[g]) for g in groups]
        lse = [m[g] + jnp.log(den[g]) for g in groups]
        for g in groups:
            for hh in range(qpk):
                h = g * qpk + hh
                rows = slice(hh * WINDOW, (hh + 1) * WINDOW)
                y_ref[:, h * HEAD_DIM:(h + 1) * HEAD_DIM] = o[g][rows].astype(BF16)
                lse_all = jnp.where(lane == h, lse[g][rows], lse_all)
        lse_ref[...] = lse_all

    return pl.pallas_call(
        body, name="attn_fwd", grid=(nb,),
        in_specs=[*proj_specs, pl.BlockSpec((8, LANES), lambda n: (0, 0))],
        out_specs=[pl.BlockSpec((WINDOW, A), lambda n: (n, 0)), pl.BlockSpec((WINDOW, LANES), lambda n: (n, 0))],
        out_shape=[jax.ShapeDtypeStruct((S, A), BF16), jax.ShapeDtypeStruct((S, LANES), F32)],
        compiler_params=_params(("parallel",)),
    )(proj, proj, proj, proj, proj, sink_row)


def _attn_bwd(proj, trig, sink_row, y, lse, dy, dm):
    S = proj.shape[0]
    A, KV, NQ = dm["A"], dm["KV"], dm["NQ"]
    qpk = NQ // N_KV_HEADS
    nb = S // WINDOW
    scale = HEAD_DIM ** -0.5
    proj_specs, trig_cur, trig_prev, cur, prev = _attn_specs(dm, nb)

    def body(q_ref, kp_ref, kc_ref, vp_ref, vc_ref, cc_ref, sc_ref, cp_ref, sp_ref,
             sink_ref, y_ref, lse_ref, dy_ref, dq_ref, dk_ref, dv_ref, dsink_ref,
             ck_ref, cv_ref, bk_ref, bv_ref, dqr_ref):
        n = pl.program_id(0)

        @pl.when(n == 0)
        def _():
            dsink_ref[...] = jnp.zeros_like(dsink_ref)
            ck_ref[...] = jnp.zeros_like(ck_ref)
            cv_ref[...] = jnp.zeros_like(cv_ref)

        @pl.when(n < nb)
        def _():
            tq = lambda r: jnp.tile(r[...], (1, A // LANES))
            tk = lambda rp, rc: jnp.tile(jnp.concatenate([rp[...], rc[...]], axis=0), (1, KV // LANES))
            cq, sq = tq(cc_ref), tq(sc_ref)
            ck, sk = tk(cp_ref, cc_ref), tk(sp_ref, sc_ref)
            qr = q_ref[...]
            kr = jnp.concatenate([kp_ref[...], kc_ref[...]], axis=0)
            vband = jnp.concatenate([vp_ref[...], vc_ref[...]], axis=0)
            mask = _band_mask(n, qpk)
            lane = lax.broadcasted_iota(jnp.int32, (1, LANES), 1)
            lse_all = lse_ref[...]
            sink_rows = jnp.broadcast_to(sink_ref[0:1, :], (WINDOW, LANES))
            dy_all = dy_ref[...]
            y_all = y_ref[...]
            dsink = jnp.zeros((1, LANES), F32)
            groups = range(N_KV_HEADS)
            head = lambda x, g: x[:, g * HEAD_DIM:(g + 1) * HEAD_DIM]
            q = [_stack_heads(qr, g, qpk) for g in groups]
            dy = [_stack_heads(dy_all, g, qpk) for g in groups]
            lse = [_stack_cols(lse_all, g, qpk) for g in groups]
            s = [jnp.where(mask, _dot(q[g], head(kr, g), NT) * scale, NEG) for g in groups]
            dp = [_dot(dy[g], head(vband, g), NT) for g in groups]
            delta = [jnp.sum(dy[g].astype(F32) * _stack_heads(y_all, g, qpk).astype(F32), axis=-1, keepdims=True)
                     for g in groups]
            p = [jnp.exp(s[g] - lse[g]) for g in groups]
            ds = [(p[g] * (dp[g] - delta[g]) * scale).astype(BF16) for g in groups]
            dq = [_dot(ds[g], head(kr, g), NN) for g in groups]
            for g in groups:
                bk_ref[:, g * HEAD_DIM:(g + 1) * HEAD_DIM] = _dot(ds[g], q[g], TN)
                bv_ref[:, g * HEAD_DIM:(g + 1) * HEAD_DIM] = _dot(p[g].astype(BF16), dy[g], TN)
            for g in groups:
                sink_d = jnp.exp(_stack_cols(sink_rows, g, qpk) - lse[g]) * delta[g]
                for hh in range(qpk):
                    h = g * qpk + hh
                    rows = slice(hh * WINDOW, (hh + 1) * WINDOW)
                    dqr_ref[:, h * HEAD_DIM:(h + 1) * HEAD_DIM] = dq[g][rows]
                    dsink = dsink + jnp.where(lane == h, -jnp.sum(sink_d[rows], axis=0, keepdims=True), 0.0)
            dsink_ref[0:1, :] += dsink
            dq_ref[...] = _rope_t(dqr_ref[...], cq, sq).astype(BF16)
            dkb = _rope_t(bk_ref[...], ck, sk)
            dvb = bv_ref[...]
            dk_ref[...] = (ck_ref[...] + dkb[:WINDOW]).astype(BF16)
            dv_ref[...] = (cv_ref[...] + dvb[:WINDOW]).astype(BF16)
            ck_ref[...] = dkb[WINDOW:]
            cv_ref[...] = dvb[WINDOW:]

        @pl.when(n == nb)
        def _():
            dk_ref[...] = ck_ref[...].astype(BF16)
            dv_ref[...] = cv_ref[...].astype(BF16)

    row = lambda w: pl.BlockSpec((WINDOW, w), lambda n: (cur(n), 0))
    done = lambda w: pl.BlockSpec((WINDOW, w), lambda n: (jnp.maximum(n - 1, 0), 0))
    return pl.pallas_call(
        body, name="attn_bwd", grid=(nb + 1,),
        in_specs=[*proj_specs, *trig_cur, *trig_prev, pl.BlockSpec((8, LANES), lambda n: (0, 0)),
                  row(A), row(LANES), row(A)],
        out_specs=[row(A), done(KV), done(KV), pl.BlockSpec((8, LANES), lambda n: (0, 0))],
        out_shape=[jax.ShapeDtypeStruct((S, A), BF16), jax.ShapeDtypeStruct((S, KV), BF16),
                   jax.ShapeDtypeStruct((S, KV), BF16), jax.ShapeDtypeStruct((8, LANES), F32)],
        scratch_shapes=[pltpu.VMEM((WINDOW, KV), F32), pltpu.VMEM((WINDOW, KV), F32),
                        pltpu.VMEM((2 * WINDOW, KV), F32), pltpu.VMEM((2 * WINDOW, KV), F32),
                        pltpu.VMEM((WINDOW, A), F32)],
        compiler_params=_params(("arbitrary",)),
    )(proj, proj, proj, proj, proj, *trig, *trig, sink_row, y, lse, dy)


def _sgu_layout(dm, S):
    G = dm["G"]
    pw = math.gcd(dm["OFF_Z"], G)
    npc = G // pw
    tm = _pick(S, (256, 128))
    u_specs = [pl.BlockSpec((tm, pw), lambda i, p=p: (i, dm["OFF_Z"] // pw + p)) for p in range(npc)]
    v_specs = [pl.BlockSpec((tm, pw), lambda i, p=p: (i, (dm["OFF_Z"] + G) // pw + p)) for p in range(npc)]
    return pw, npc, tm, u_specs, v_specs


def _sgu_norm(v_refs, lg_ref, lb_ref):
    v = jnp.concatenate([_gelu(r[...].astype(F32)) for r in v_refs], axis=1)
    mu = jnp.mean(v, axis=-1, keepdims=True)
    vc = v - mu
    rstd = lax.rsqrt(jnp.mean(vc * vc, axis=-1, keepdims=True) + EPS)
    xhat = vc * rstd
    return xhat, rstd, (xhat * lg_ref[...] + lb_ref[...]).astype(BF16)


def _sgu_fwd(proj, w_tril, b_t, ln_g_row, ln_b_row, dm):
    S = proj.shape[0]
    G, NG = dm["G"], dm["NG"]
    pw, npc, tm, u_specs, v_specs = _sgu_layout(dm, S)
    nch = tm // WINDOW

    def body(*refs):
        u_refs, v_refs = refs[:npc], refs[npc:2 * npc]
        w_ref, bt_ref, lg_ref, lb_ref, y_ref = refs[2 * npc:]
        _, _, vn = _sgu_norm(v_refs, lg_ref, lb_ref)
        u = jnp.concatenate([_gelu(r[...].astype(F32)) for r in u_refs], axis=1)
        for c in range(nch):
            rows = slice(c * WINDOW, (c + 1) * WINDOW)
            for g in range(NG):
                cols = slice(g * LANES, (g + 1) * LANES)
                sv = _dot(w_ref[g], vn[rows, cols], NN) + bt_ref[:, g:g + 1]
                y_ref[rows, cols] = (u[rows, cols] * sv).astype(BF16)

    return pl.pallas_call(
        body, name="sgu_fwd", grid=(S // tm,),
        in_specs=[*u_specs, *v_specs,
                  pl.BlockSpec((NG, WINDOW, WINDOW), lambda i: (0, 0, 0)),
                  pl.BlockSpec((WINDOW, LANES), lambda i: (0, 0)),
                  pl.BlockSpec((1, G), lambda i: (0, 0)), pl.BlockSpec((1, G), lambda i: (0, 0))],
        out_specs=pl.BlockSpec((tm, G), lambda i: (i, 0)),
        out_shape=jax.ShapeDtypeStruct((S, G), BF16),
        compiler_params=_params(("parallel",)),
    )(*([proj] * (2 * npc)), w_tril, b_t, ln_g_row, ln_b_row)


def _sgu_bwd(proj, w_tril, b_t, ln_g_row, ln_b_row, dy, dm):
    S = proj.shape[0]
    G, NG = dm["G"], dm["NG"]
    pw, npc, tm, u_specs, v_specs = _sgu_layout(dm, S)
    nch = tm // WINDOW

    def body(*refs):
        u_refs, v_refs = refs[:npc], refs[npc:2 * npc]
        w_ref, bt_ref, lg_ref, lb_ref, dy_ref, dz_ref, dw_ref, dbt_ref, dlg_ref, dlb_ref, dvn_ref = refs[2 * npc:]
        i = pl.program_id(0)

        @pl.when(i == 0)
        def _():
            dw_ref[...] = jnp.zeros_like(dw_ref)
            dbt_ref[...] = jnp.zeros_like(dbt_ref)
            dlg_ref[...] = jnp.zeros_like(dlg_ref)
            dlb_ref[...] = jnp.zeros_like(dlb_ref)

        xhat, rstd, vn = _sgu_norm(v_refs, lg_ref, lb_ref)
        u_pre = jnp.concatenate([r[...].astype(F32) for r in u_refs], axis=1)
        u = _gelu(u_pre)
        dy = dy_ref[...].astype(F32)
        lane = lax.broadcasted_iota(jnp.int32, (WINDOW, LANES), 1)
        tri = lax.broadcasted_iota(jnp.int32, (WINDOW, WINDOW), 0) >= lax.broadcasted_iota(jnp.int32, (WINDOW, WINDOW), 1)
        dbt = jnp.zeros((WINDOW, LANES), F32)
        for c in range(nch):
            rows = slice(c * WINDOW, (c + 1) * WINDOW)
            for g in range(NG):
                cols = slice(g * LANES, (g + 1) * LANES)
                vn_cg = vn[rows, cols]
                sv = _dot(w_ref[g], vn_cg, NN) + bt_ref[:, g:g + 1]
                dy_cg = dy[rows, cols]
                dsv = dy_cg * u[rows, cols]
                dsv_b = dsv.astype(BF16)
                dz_ref[rows, cols] = (dy_cg * sv * _gelu_grad(u_pre[rows, cols])).astype(BF16)
                dvn_ref[rows, cols] = _dot(w_ref[g], dsv_b, TN)
                dw_ref[g] += jnp.where(tri, _dot(dsv_b, vn_cg, NT), 0.0)
                dbt = dbt + jnp.where(lane == g, jnp.sum(dsv, axis=-1, keepdims=True), 0.0)
        dbt_ref[...] += dbt
        dvn = dvn_ref[...]
        dlg_ref[0:1, :] += jnp.sum(dvn * xhat, axis=0, keepdims=True)
        dlb_ref[0:1, :] += jnp.sum(dvn, axis=0, keepdims=True)
        dxh = dvn * lg_ref[...]
        dv = rstd * (dxh - jnp.mean(dxh, axis=-1, keepdims=True) - xhat * jnp.mean(dxh * xhat, axis=-1, keepdims=True))
        v_pre = jnp.concatenate([r[...].astype(F32) for r in v_refs], axis=1)
        dz_ref[:, G:] = (dv * _gelu_grad(v_pre)).astype(BF16)

    return pl.pallas_call(
        body, name="sgu_bwd", grid=(S // tm,),
        in_specs=[*u_specs, *v_specs,
                  pl.BlockSpec((NG, WINDOW, WINDOW), lambda i: (0, 0, 0)),
                  pl.BlockSpec((WINDOW, LANES), lambda i: (0, 0)),
                  pl.BlockSpec((1, G), lambda i: (0, 0)), pl.BlockSpec((1, G), lambda i: (0, 0)),
                  pl.BlockSpec((tm, G), lambda i: (i, 0))],
        out_specs=[pl.BlockSpec((tm, 2 * G), lambda i: (i, 0)),
                   pl.BlockSpec((NG, WINDOW, WINDOW), lambda i: (0, 0, 0)),
                   pl.BlockSpec((WINDOW, LANES), lambda i: (0, 0)),
                   pl.BlockSpec((8, G), lambda i: (0, 0)), pl.BlockSpec((8, G), lambda i: (0, 0))],
        out_shape=[jax.ShapeDtypeStruct((S, 2 * G), BF16), jax.ShapeDtypeStruct((NG, WINDOW, WINDOW), F32),
                   jax.ShapeDtypeStruct((WINDOW, LANES), F32), jax.ShapeDtypeStruct((8, G), F32),
                   jax.ShapeDtypeStruct((8, G), F32)],
        scratch_shapes=[pltpu.VMEM((tm, G), F32)],
        compiler_params=_params(("arbitrary",)),
    )(*([proj] * (2 * npc)), w_tril, b_t, ln_g_row, ln_b_row, dy)


def _result(outs, n_main, carry):
    main = outs[0] if n_main == 1 else tuple(outs[:n_main])
    return (main, list(outs[n_main:])) if carry else main


def _in_proj(xn, w_in_g, b_row, trig, dm, carry=None):
    S, D = xn.shape
    IN = dm["IN"]
    cw = IN // N_CHIPS
    tm = _pick(S, (512, 256, 128))
    tn = _pick(cw, (1920, 640, 512, 256, 128))
    nbc = cw // tn
    rope_cols = dm["OFF_V"]
    rope_blocks = -(-rope_cols // tn)

    def roped_store(val, e_refs, o_refs, cols, jj):
        r = min(max(rope_cols - (jj * tn + cols.start), 0), cols.stop - cols.start)
        out = val
        if r:
            cos, sin = [jnp.tile(e[...], (1, r // LANES)) for e in e_refs[1:]]
            roped = _rope(val[:, :r], cos, sin)
            out = roped if r == val.shape[1] else jnp.concatenate([roped, val[:, r:]], axis=1)
        o_refs[0][:, cols] = out.astype(BF16)

    def ep_rope(parts, e_refs, o_refs, cols):
        val = parts[0] + e_refs[0][:, cols]
        if rope_blocks == 1:
            roped_store(val, e_refs, o_refs, cols, 0)
        else:
            for jj in range(rope_blocks):
                @pl.when(pl.program_id(0) == jj)
                def _(jj=jj):
                    roped_store(val, e_refs, o_refs, cols, jj)

    def ep_plain(parts, e_refs, o_refs, cols):
        o_refs[0][:, cols] = (parts[0] + e_refs[0][:, cols]).astype(BF16)

    rows = pl.BlockSpec((tm, LANES), lambda i, j, k: (i, 0))
    first = IN // tn - rope_blocks
    common = dict(dims=NN, lhs_spec=pl.BlockSpec((tm, D), lambda i, j, k: (i, 0)), acc_shape=(tm, tn),
                  out_shape=[jax.ShapeDtypeStruct((S, IN), BF16)], cols_outer=True)
    proj = _matmul(
        "in_proj_qk", xn, [w_in_g], grid=(S // tm, rope_blocks, 1),
        rhs_specs=[pl.BlockSpec((None, D, tn), lambda i, j, k: (j // nbc, 0, j % nbc))],
        extra=[b_row, *trig], extra_specs=[pl.BlockSpec((1, tn), lambda i, j, k: (0, j)), rows, rows],
        out_specs=[pl.BlockSpec((tm, tn), lambda i, j, k: (i, j))], epilogue=ep_rope, **common)[0]
    jb = lambda j: j + rope_blocks
    return _result(_matmul(
        "in_proj", xn, [w_in_g], grid=(S // tm, first, 1),
        rhs_specs=[pl.BlockSpec((None, D, tn), lambda i, j, k: (jb(j) // nbc, 0, jb(j) % nbc))],
        extra=[b_row, proj], extra_specs=[pl.BlockSpec((1, tn), lambda i, j, k: (0, jb(j))), ANY], extra_aliases={1: 0},
        out_specs=[pl.BlockSpec((tm, tn), lambda i, j, k: (i, jb(j)))], epilogue=ep_plain, carry=carry, **common), 1, carry)


def _branch_attn(y_attn, w_ab_g, dm):
    S, A = y_attn.shape
    D = dm["D"]
    cw = D // N_CHIPS
    tm = _pick(S, (1024, 512, 256, 128))
    return _matmul(
        "branch_attn", y_attn, [w_ab_g], dims=NN, grid=(S // tm, N_CHIPS, 1),
        lhs_spec=pl.BlockSpec((tm, A), lambda i, j, k: (i, 0)),
        rhs_specs=[pl.BlockSpec((None, A, cw), lambda i, j, k: (j, 0, 0))],
        acc_shape=(tm, cw), out_shape=[jax.ShapeDtypeStruct((S, D), BF16)],
        out_specs=[pl.BlockSpec((tm, cw), lambda i, j, k: (i, j))], epilogue=_store_epilogue(BF16))[0]


def _branch_sgu_merge(y_sgu, w_sb_g, a_attn, proj, dm):
    S, G = y_sgu.shape
    D, OFF_G = dm["D"], dm["OFF_G"]
    cw = D // N_CHIPS
    tm = _pick(S, (1024, 512, 256, 128))

    def ep(parts, e_refs, o_refs, cols):
        a_sgu = parts[0].astype(BF16)
        ga = _sigmoid(e_refs[1][:, cols].astype(F32))
        gs = _sigmoid(e_refs[2][:, cols].astype(F32))
        o_refs[0][:, cols] = a_sgu
        o_refs[1][:, cols] = (ga * e_refs[0][:, cols].astype(F32) + gs * a_sgu.astype(F32)).astype(BF16)

    blk = pl.BlockSpec((tm, cw), lambda i, j, k: (i, j))
    return _matmul(
        "branch_sgu_merge", y_sgu, [w_sb_g], dims=NN, grid=(S // tm, N_CHIPS, 1),
        lhs_spec=pl.BlockSpec((tm, G), lambda i, j, k: (i, 0)),
        rhs_specs=[pl.BlockSpec((None, G, cw), lambda i, j, k: (j, 0, 0))],
        acc_shape=(tm, cw), extra=[a_attn, proj, proj],
        extra_specs=[blk, pl.BlockSpec((tm, cw), lambda i, j, k: (i, OFF_G // cw + j)),
                     pl.BlockSpec((tm, cw), lambda i, j, k: (i, (OFF_G + D) // cw + j))],
        out_shape=[jax.ShapeDtypeStruct((S, D), BF16), jax.ShapeDtypeStruct((S, D), BF16)],
        out_specs=[blk, blk], epilogue=ep)


def _residual_matmul(name, a, w_g, h, carry=None):
    S, K = a.shape
    D = w_g.shape[1]
    tm = _pick(S, (1024, 512, 256, 128))
    tn = _pick(D, (512, 256, 128))

    def ep(parts, e_refs, o_refs, cols):
        o_refs[0][:, cols] = e_refs[0][:, cols] + parts[0]

    blk = pl.BlockSpec((tm, tn), lambda i, j, k: (i, j))
    return _result(_matmul(
        name, a, [w_g], dims=NN, grid=(S // tm, D // tn, 1),
        lhs_spec=pl.BlockSpec((tm, K), lambda i, j, k: (i, 0)),
        rhs_specs=[pl.BlockSpec((K, tn), lambda i, j, k: (0, j))],
        acc_shape=(tm, tn), extra=[h], extra_specs=[blk],
        out_shape=[jax.ShapeDtypeStruct((S, D), F32)], out_specs=[blk], epilogue=ep, carry=carry), 1, carry)


def _gate_up(hn, w_gu_g, dm, carry=None):
    S, D = hn.shape
    Fd = dm["F"]
    cw = 2 * Fd // N_CHIPS
    tm = _pick(S, (512, 256, 128))
    tn = _pick(cw, (1408, 512, 384, 256, 128))
    nbc = cw // tn
    half = N_CHIPS // 2

    def ep(parts, e_refs, o_refs, cols):
        gate, up = parts[0].astype(BF16), parts[1].astype(BF16)
        o_refs[0][0, :, cols] = gate
        o_refs[0][1, :, cols] = up
        g32 = gate.astype(F32)
        o_refs[1][:, cols] = (g32 * _sigmoid(g32) * up.astype(F32)).astype(BF16)

    return _result(_matmul(
        "gate_up", hn, [w_gu_g, w_gu_g], dims=NN, grid=(S // tm, Fd // tn, 1),
        lhs_spec=pl.BlockSpec((tm, D), lambda i, j, k: (i, 0)),
        rhs_specs=[pl.BlockSpec((None, D, tn), lambda i, j, k: (j // nbc, 0, j % nbc)),
                   pl.BlockSpec((None, D, tn), lambda i, j, k: (half + j // nbc, 0, j % nbc))],
        acc_shape=(tm, tn),
        out_shape=[jax.ShapeDtypeStruct((2, S, Fd), BF16), jax.ShapeDtypeStruct((S, Fd), BF16)],
        out_specs=[pl.BlockSpec((2, tm, tn), lambda i, j, k: (0, i, j)), pl.BlockSpec((tm, tn), lambda i, j, k: (i, j))],
        epilogue=ep, carry=carry, cols_outer=True), 2, carry)


def _down_bwd(dh_b, w_down_g, gu, dm, carry=None):
    S, D = dh_b.shape
    Fd = dm["F"]
    tm = _pick(S, (1024, 512, 256, 128))
    tn = _pick(Fd, (512, 256, 128))

    def ep(parts, e_refs, o_refs, cols):
        gate = e_refs[0][0, :, cols].astype(F32)
        up = e_refs[0][1, :, cols].astype(F32)
        s = _sigmoid(gate)
        dact = parts[0]
        o_refs[0][0, :, cols] = (dact * up * s * (1.0 + gate * (1.0 - s))).astype(BF16)
        o_refs[0][1, :, cols] = (dact * gate * s).astype(BF16)

    blk = pl.BlockSpec((2, tm, tn), lambda i, j, k: (0, i, j))
    return _result(_matmul(
        "down_bwd", dh_b, [w_down_g], dims=NT, grid=(S // tm, Fd // tn, 1),
        lhs_spec=pl.BlockSpec((tm, D), lambda i, j, k: (i, 0)),
        rhs_specs=[pl.BlockSpec((tn, D), lambda i, j, k: (j, 0))],
        acc_shape=(tm, tn), extra=[gu], extra_specs=[blk],
        out_shape=[jax.ShapeDtypeStruct((2, S, Fd), BF16)], out_specs=[blk], epilogue=ep, carry=carry), 1, carry)


def _gate_up_bwd(dgu, w_gu_g, dm, carry=None):
    S = dgu.shape[1]
    D, Fd = dm["D"], dm["F"]
    cw = 2 * Fd // N_CHIPS
    half = N_CHIPS // 2
    tm = _pick(S, (1024, 512, 256, 128))
    tn = _pick(D, (1024, 512, 256, 128))
    return _result(_matmul(
        "gate_up_bwd", dgu, [w_gu_g], dims=NT, grid=(S // tm, D // tn, N_CHIPS),
        lhs_spec=pl.BlockSpec((None, tm, cw), lambda i, j, k: (k // half, i, k % half)),
        rhs_specs=[pl.BlockSpec((None, tn, cw), lambda i, j, k: (k, j, 0))],
        acc_shape=(tm, tn), out_shape=[jax.ShapeDtypeStruct((S, D), F32)],
        out_specs=[pl.BlockSpec((tm, tn), lambda i, j, k: (i, j))], epilogue=_store_epilogue(F32), carry=carry), 1, carry)


def _out_bwd(dh_b, w_out_g, proj, a_attn, a_sgu, dm, carry=None):
    S, D = dh_b.shape
    OFF_G = dm["OFF_G"]
    tm = _pick(S, (1024, 512, 256, 128))
    tn = D // N_CHIPS

    def ep(parts, e_refs, o_refs, cols):
        dm_ = parts[0]
        ga = _sigmoid(e_refs[0][:, cols].astype(F32))
        gs = _sigmoid(e_refs[1][:, cols].astype(F32))
        o_refs[0][:, cols] = (dm_ * ga).astype(BF16)
        o_refs[1][:, cols] = (dm_ * gs).astype(BF16)
        o_refs[2][0, :, cols] = (dm_ * e_refs[2][:, cols].astype(F32) * ga * (1.0 - ga)).astype(BF16)
        o_refs[2][1, :, cols] = (dm_ * e_refs[3][:, cols].astype(F32) * gs * (1.0 - gs)).astype(BF16)

    blk = pl.BlockSpec((tm, tn), lambda i, j, k: (i, j))
    return _result(_matmul(
        "out_bwd", dh_b, [w_out_g], dims=NT, grid=(S // tm, D // tn, 1),
        lhs_spec=pl.BlockSpec((tm, D), lambda i, j, k: (i, 0)),
        rhs_specs=[pl.BlockSpec((tn, D), lambda i, j, k: (j, 0))],
        acc_shape=(tm, tn), extra=[proj, proj, a_attn, a_sgu],
        extra_specs=[pl.BlockSpec((tm, tn), lambda i, j, k: (i, OFF_G // tn + j)),
                     pl.BlockSpec((tm, tn), lambda i, j, k: (i, (OFF_G + D) // tn + j)), blk, blk],
        out_shape=[jax.ShapeDtypeStruct((S, D), BF16), jax.ShapeDtypeStruct((S, D), BF16),
                   jax.ShapeDtypeStruct((2, S, D), BF16)],
        out_specs=[blk, blk, pl.BlockSpec((2, tm, tn), lambda i, j, k: (0, i, j))], epilogue=ep, carry=carry), 3, carry)


def _colsharded_bwd(name, dy, w_g, out_dtype, carry=None):
    S = dy.shape[0]
    _, K, cw = w_g.shape
    tm = _pick(S, (1024, 512, 256, 128))
    tn = _pick(K, (1024, 512, 256, 128))
    return _result(_matmul(
        name, dy, [w_g], dims=NT, grid=(S // tm, K // tn, N_CHIPS),
        lhs_spec=pl.BlockSpec((tm, cw), lambda i, j, k: (i, k)),
        rhs_specs=[pl.BlockSpec((None, tn, cw), lambda i, j, k: (k, j, 0))],
        acc_shape=(tm, tn), out_shape=[jax.ShapeDtypeStruct((S, K), out_dtype)],
        out_specs=[pl.BlockSpec((tm, tn), lambda i, j, k: (i, j))], epilogue=_store_epilogue(out_dtype),
        carry=carry), 1, carry)


def _wgrad_cols(name, x, dy, carry=None, colsum=False):
    S, R = x.shape
    C = dy.shape[1]
    cw = C // N_CHIPS
    tm = _pick(R, (1024, 512, 256, 128))
    tk = _pick(S, (1024, 512, 256, 128) if cw >= 1024 else (2048, 1024, 512, 256, 128))

    def ep(parts, e_refs, o_refs, cols):
        for o, p in zip(o_refs, parts):
            o[:, cols] = p

    out_shape = [jax.ShapeDtypeStruct((N_CHIPS, R, cw), F32)]
    out_specs = [pl.BlockSpec((None, tm, cw), lambda i, j, k: (j, i, 0))]
    if colsum:
        out_shape.append(jax.ShapeDtypeStruct((R // tm, N_CHIPS, 8, cw), F32))
        out_specs.append(pl.BlockSpec((None, None, 8, cw), lambda i, j, k: (i, j, 0, 0)))
    return _result(_matmul(
        name, x, [dy], dims=TN, grid=(R // tm, N_CHIPS, S // tk),
        lhs_spec=pl.BlockSpec((tk, tm), lambda i, j, k: (k, i)),
        rhs_specs=[pl.BlockSpec((tk, cw), lambda i, j, k: (k, j))],
        acc_shape=(tm, cw), out_shape=out_shape, out_specs=out_specs, epilogue=ep,
        carry=carry, rhs_colsum=colsum), len(out_shape), carry)


def _wgrad_gate_up(hn, dgu, dm, carry=None):
    S, D = hn.shape
    Fd = dm["F"]
    cw = 2 * Fd // N_CHIPS
    half = N_CHIPS // 2
    tm = _pick(D, (1024, 512, 256, 128))
    tk = _pick(S, (1024, 512, 256, 128))
    tn = _pick(cw, (1408, 512, 384, 256, 128))
    nbc = cw // tn
    return _result(_matmul(
        "wgrad_gate_up", hn, [dgu], dims=TN, grid=(D // tm, 2 * Fd // tn, S // tk),
        lhs_spec=pl.BlockSpec((tk, tm), lambda i, j, k: (k, i)),
        rhs_specs=[pl.BlockSpec((None, tk, tn), lambda i, j, k: (j // (half * nbc), k, j % (half * nbc)))],
        acc_shape=(tm, tn), out_shape=[jax.ShapeDtypeStruct((N_CHIPS, D, cw), F32)],
        out_specs=[pl.BlockSpec((None, tm, tn), lambda i, j, k: (j // nbc, i, j % nbc))], epilogue=_store_epilogue(F32),
        carry=carry), 1, carry)


def _wgrad_rows(name, x, dy):
    S, R = x.shape
    C = dy.shape[1]
    rw = R // N_CHIPS
    tn = _pick(C, (1024, 512, 256, 128))
    tk = _pick(S, (1024, 512, 256, 128))
    return _matmul(
        name, x, [dy], dims=TN, grid=(N_CHIPS, C // tn, S // tk),
        lhs_spec=pl.BlockSpec((tk, rw), lambda i, j, k: (k, i)),
        rhs_specs=[pl.BlockSpec((tk, tn), lambda i, j, k: (k, j))],
        acc_shape=(rw, tn), out_shape=[jax.ShapeDtypeStruct((N_CHIPS, rw, C), F32)],
        out_specs=[pl.BlockSpec((None, rw, tn), lambda i, j, k: (i, 0, j))], epilogue=_store_epilogue(F32))[0]


def _place():
    x, y, c = lax.axis_index("x"), lax.axis_index("y"), lax.axis_index("c")
    others = [(1 - x, y), (x, 1 - y), (1 - x, 1 - y)]
    return x, y, c, others


def _chip_index(chip):
    return 2 * chip[0] + chip[1]


def _gather_weights(bufs):
    n = len(bufs)

    def copies(src, out, send_sems, recv_sems):
        x, y, c, others = _place()

        def half(ref, chip_idx, hc):
            r2 = ref.shape[1] // 2
            return ref.at[chip_idx, pl.ds(hc * r2, r2), :]

        def copy(t, k, chip, hc, to):
            return pltpu.make_async_remote_copy(
                src_ref=half(src[t], _chip_index(chip), hc), dst_ref=half(out[t], _chip_index(chip), hc),
                send_sem=send_sems.at[6 * t + k], recv_sem=recv_sems.at[6 * t + k],
                device_id=to, device_id_type=MESH)

        me, sibling = (x, y, c), (x, y, 1 - c)
        pairs = [(t, j, chip) for t in range(n) for j, chip in enumerate(others)]
        sent = [copy(t, j, (x, y), c, (*chip, c)) for t, j, chip in pairs]
        landed = [copy(t, j, chip, c, me) for t, j, chip in pairs]
        passed = [copy(t, 3 + j, chip, c, sibling) for t, j, chip in pairs]
        handed = [copy(t, 3 + j, chip, 1 - c, me) for t, j, chip in pairs]
        return sent, landed, passed, handed

    def start(src, out, send_sems, recv_sems):
        for cp in copies(src, out, send_sems, recv_sems)[0]:
            cp.start()

    def finish(src, out, send_sems, recv_sems):
        sent, landed, passed, handed = copies(src, out, send_sems, recv_sems)
        for arrival, forward in zip(landed, passed):
            arrival.wait_recv()
            forward.start()
        for cp in handed:
            cp.wait_recv()
        for cp in sent + passed:
            cp.wait_send()

    return _Comm("gather_weights", bufs, [jax.ShapeDtypeStruct(b.shape, BF16) for b in bufs],
                 {t: t for t in range(n)}, 6 * n, start, finish)


def _sibling_exchange(grads):
    n = len(grads)
    shapes = [g.shape for g in grads]

    def copies(src, land, send_sems, recv_sems):
        x, y, c, _ = _place()
        res = []
        for t in range(n):
            r2 = shapes[t][1] // 2
            res.append(pltpu.make_async_remote_copy(
                src_ref=src[t].at[:, pl.ds((1 - c) * r2, r2), :], dst_ref=land[t],
                send_sem=send_sems.at[t], recv_sem=recv_sems.at[t], device_id=(x, y, 1 - c), device_id_type=MESH))
        return res

    def start(*refs):
        for cp in copies(*refs):
            cp.start()

    def finish(*refs):
        remote = copies(*refs)
        for cp in remote:
            cp.wait_recv()
        for cp in remote:
            cp.wait_send()

    return _Comm("sibling_exchange", grads, [jax.ShapeDtypeStruct((s[0], s[1] // 2, s[2]), F32) for s in shapes],
                 {}, n, start, finish)


def _chip_exchange(sends):
    n = len(sends)
    shapes = [s.shape for s in sends]

    def copies(snd, got, send_sems, recv_sems):
        x, y, c, others = _place()
        return [pltpu.make_async_remote_copy(
            src_ref=snd[t].at[_chip_index(chip)], dst_ref=got[t].at[j],
            send_sem=send_sems.at[3 * t + j], recv_sem=recv_sems.at[3 * t + j],
            device_id=(*chip, c), device_id_type=MESH) for t in range(n) for j, chip in enumerate(others)]

    def start(*refs):
        for cp in copies(*refs):
            cp.start()

    def finish(*refs):
        remote = copies(*refs)
        for cp in remote:
            cp.wait_recv()
        for cp in remote:
            cp.wait_send()

    return _Comm("chip_exchange", sends, [jax.ShapeDtypeStruct((3, s[1], s[2]), BF16) for s in shapes],
                 {}, 3 * n, start, finish)


def _sibling_share(fulls):
    n = len(fulls)
    shapes = [f.shape for f in fulls]

    def copies(src, out, send_sems, recv_sems, mine):
        x, y, c, _ = _place()
        hc = c if mine else 1 - c
        res = []
        for t in range(n):
            r2 = shapes[t][0] // 2
            res.append(pltpu.make_async_remote_copy(
                src_ref=src[t].at[pl.ds(hc * r2, r2), :], dst_ref=out[t].at[pl.ds(hc * r2, r2), :],
                send_sem=send_sems.at[t], recv_sem=recv_sems.at[t], device_id=(x, y, 1 - c), device_id_type=MESH))
        return res

    def start(*refs):
        for cp in copies(*refs, mine=True):
            cp.start()

    def finish(*refs):
        for cp in copies(*refs, mine=False):
            cp.wait_recv()
        for cp in copies(*refs, mine=True):
            cp.wait_send()

    return _Comm("sibling_share", fulls, [jax.ShapeDtypeStruct(s, F32) for s in shapes],
                 {t: t for t in range(n)}, n, start, finish)


def _gather_all(v):
    R, C = v.shape

    def body(v_ref, out_ref, send_sems, recv_sems, local_sem):
        x, y, c, others = _place()
        me, sibling = (x, y, c), (x, y, 1 - c)

        def rows(px, py, pc):
            return out_ref.at[4 * px + 2 * py + pc]

        def copy(k, block, to, src=None):
            return pltpu.make_async_remote_copy(
                src_ref=rows(*block) if src is None else src, dst_ref=rows(*block),
                send_sem=send_sems.at[k], recv_sem=recv_sems.at[k], device_id=to, device_id_type=MESH)

        mine = pltpu.make_async_copy(v_ref, rows(*me), local_sem)
        mine.start()
        first = [copy(0, me, sibling, src=v_ref)]
        first += [copy(1 + j, me, (*chip, c), src=v_ref) for j, chip in enumerate(others)]
        for cp in first:
            cp.start()
        passed = [copy(4 + j, (*chip, c), sibling) for j, chip in enumerate(others)]
        for j, chip in enumerate(others):
            copy(1 + j, (*chip, c), me).wait_recv()
            passed[j].start()
        copy(0, sibling, me).wait_recv()
        for j, chip in enumerate(others):
            copy(4 + j, (*chip, 1 - c), me).wait_recv()
        for cp in first + passed:
            cp.wait_send()
        mine.wait()

    return pl.pallas_call(
        body, name="gather_all", in_specs=[ANY], out_specs=ANY,
        out_shape=jax.ShapeDtypeStruct((8, R, C), F32),
        scratch_shapes=[pltpu.SemaphoreType.DMA((7,)), pltpu.SemaphoreType.DMA((7,)), pltpu.SemaphoreType.DMA],
    )(v)


def _my_chip():
    return 2 * lax.axis_index("x") + lax.axis_index("y")


def _my_core():
    return lax.axis_index("c")


def _pair_sum(grad, land):
    K, R2, C = land.shape
    tm = _row_tile(R2, C)
    nrb = R2 // tm

    def body(a_ref, b_ref, sb_ref):
        sb_ref[...] = (a_ref[...] + b_ref[...]).astype(BF16)

    blk = pl.BlockSpec((None, tm, C), lambda k, r: (k, r, 0))
    return pl.pallas_call(
        body, name="pair_sum", grid=(K, nrb),
        in_specs=[pl.BlockSpec((None, tm, C), lambda k, r: (k, _my_core() * nrb + r, 0)), blk],
        out_specs=blk, out_shape=jax.ShapeDtypeStruct((K, R2, C), BF16),
        compiler_params=_params(("parallel", "parallel")),
    )(grad, land)


def _chip_sum(grad, land, got):
    _, R2, C = land.shape
    tm = _row_tile(R2, C)
    nrb = R2 // tm

    def body(a_ref, b_ref, g_ref, s_ref):
        own = a_ref[...] + b_ref[...]
        s_ref[...] = ((own + g_ref[0].astype(F32)) + g_ref[1].astype(F32)) + g_ref[2].astype(F32)

    return pl.pallas_call(
        body, name="chip_sum", grid=(nrb,),
        in_specs=[pl.BlockSpec((None, tm, C), lambda r: (_my_chip(), _my_core() * nrb + r, 0)),
                  pl.BlockSpec((None, tm, C), lambda r: (_my_chip(), r, 0)),
                  pl.BlockSpec((3, tm, C), lambda r: (0, r, 0))],
        out_specs=pl.BlockSpec((tm, C), lambda r: (_my_core() * nrb + r, 0)),
        out_shape=jax.ShapeDtypeStruct((2 * R2, C), F32),
        compiler_params=_params(("parallel",)),
    )(grad, land, got)


def _adamw_math(w, g, m, v):
    m = ADAM_B1 * m + (1.0 - ADAM_B1) * g
    v = ADAM_B2 * v + (1.0 - ADAM_B2) * (g * g)
    m_hat = m / (1.0 - ADAM_B1 ** ADAM_STEP)
    v_hat = v / (1.0 - ADAM_B2 ** ADAM_STEP)
    delta = -ADAM_LR * (m_hat / (jnp.sqrt(v_hat) + ADAM_EPS) + ADAM_WD * w)
    return delta, m, v


def _adamw_stacked(grads, w, m, v, carry=None):
    L, R, C = w.shape
    tm = _row_tile(R, C)
    nrb = R // tm
    nci = len(carry.ins) if carry else 0
    nco = len(carry.outs) if carry else 0

    def body(*refs):
        g_refs = refs[:L]
        w_ref, m_ref, v_ref = refs[L:L + 3]
        ci_refs = refs[L + 3:L + 3 + nci]
        go_ref, d_ref, mo_ref, vo_ref = refs[L + 3 + nci:L + 7 + nci]
        co_refs = refs[L + 7 + nci:L + 7 + nci + nco]
        sems = refs[L + 7 + nci + nco:]
        l, r = pl.program_id(0), pl.program_id(1)
        if carry:
            @pl.when((l == 0) & (r == 0))
            def _():
                carry.start(ci_refs, co_refs, *sems)

        for ll in range(L):
            @pl.when(l == ll)
            def _(ll=ll):
                g = g_refs[ll][...]
                delta, mn, vn = _adamw_math(w_ref[...], g, m_ref[...], v_ref[...])
                go_ref[...] = g
                d_ref[...] = delta
                mo_ref[...] = mn
                vo_ref[...] = vn

        if carry:
            @pl.when((l == L - 1) & (r == nrb - 1))
            def _():
                carry.finish(ci_refs, co_refs, *sems)

    stacked = pl.BlockSpec((None, tm, C), lambda l, r: (l, r, 0))
    g_specs = [pl.BlockSpec((tm, C), lambda l, r, ll=ll: (jnp.where(l == ll, r, 0), 0)) for ll in range(L)]
    shp = jax.ShapeDtypeStruct((L, R, C), F32)
    outs = pl.pallas_call(
        body, name="adamw", grid=(L, nrb),
        in_specs=[*g_specs, stacked, stacked, stacked, *([ANY] * nci)],
        out_specs=[*([stacked] * 4), *([ANY] * nco)], out_shape=[*([shp] * 4), *(carry.outs if carry else [])],
        scratch_shapes=carry.sem_scratch() if carry else [],
        input_output_aliases={L + 3 + i: 4 + o for i, o in carry.aliases.items()} if carry else {},
        compiler_params=_params(("arbitrary", "arbitrary")),
    )(*grads, w, m, v, *(carry.ins if carry else []))
    return (outs[:4], list(outs[4:])) if carry else outs


def _adamw_small(parts, w, m, v):
    _, R, C = parts.shape
    tm = _row_tile(R, 8 * C)

    def body(p_ref, w_ref, m_ref, v_ref, go_ref, d_ref, mo_ref, vo_ref):
        g = p_ref[0]
        for k in range(1, 8):
            g = g + p_ref[k]
        delta, mn, vn = _adamw_math(w_ref[...], g, m_ref[...], v_ref[...])
        go_ref[...] = g
        d_ref[...] = delta
        mo_ref[...] = mn
        vo_ref[...] = vn

    blk = pl.BlockSpec((tm, C), lambda i: (i, 0))
    shp = jax.ShapeDtypeStruct((R, C), F32)
    return pl.pallas_call(
        body, name="adamw_small", grid=(R // tm,),
        in_specs=[pl.BlockSpec((8, tm, C), lambda i: (0, i, 0)), blk, blk, blk],
        out_specs=[blk] * 4, out_shape=[shp] * 4,
        compiler_params=_params(("parallel",)),
    )(parts, w, m, v)


def _cast_place(w, layer):
    _, R, C = w.shape
    tm = _row_tile(R, C)

    def body(w_ref, o_ref):
        o_ref[...] = w_ref[...].astype(BF16)

    return pl.pallas_call(
        body, name="cast_place", grid=(R // tm,),
        in_specs=[pl.BlockSpec((None, tm, C), lambda r: (layer, r, 0))],
        out_specs=pl.BlockSpec((None, tm, C), lambda r: (_my_chip(), r, 0)),
        out_shape=jax.ShapeDtypeStruct((N_CHIPS, R, C), BF16),
        compiler_params=_params(("parallel",)),
    )(w)


def _trig_tables(positions):
    inv_freq = ROPE_THETA ** (-jnp.arange(0, ROPE_DIM, 2, dtype=F32) / ROPE_DIM)
    ang = positions.astype(F32)[:, None] * inv_freq
    cos, sin = jnp.cos(ang), jnp.sin(ang)
    S = positions.shape[0]
    cos_h = jnp.concatenate([cos, cos, jnp.ones((S, HEAD_DIM - ROPE_DIM), F32)], axis=1)
    sin_h = jnp.concatenate([-sin, sin, jnp.zeros((S, HEAD_DIM - ROPE_DIM), F32)], axis=1)
    rep = LANES // HEAD_DIM
    return [jnp.tile(t, (1, rep)) for t in (cos_h, sin_h)]


def _row(vec):
    return vec.reshape(1, -1)


def _lane_row(vec):
    return jnp.zeros((8, LANES), F32).at[0, :vec.shape[0]].set(vec)


def _pack(pieces, rows):
    flat = jnp.concatenate([p.reshape(-1).astype(F32) for p in pieces])
    return jnp.pad(flat, (0, rows * LANES - flat.shape[0])).reshape(rows, LANES)


def kernel(x, positions, norm1_g, w_in, b_in, sinks, sgu_ln_g, sgu_ln_b, sgu_w, sgu_b, w_attn_branch, w_sgu_branch, w_out, norm2_g, w_gate_up, w_down, final_g, loss_target, m_norm1_g, m_w_in, m_b_in, m_sinks, m_sgu_ln_g, m_sgu_ln_b, m_sgu_w, m_sgu_b, m_w_attn_branch, m_w_sgu_branch, m_w_out, m_norm2_g, m_w_gate_up, m_w_down, m_final_g, v_norm1_g, v_w_in, v_b_in, v_sinks, v_sgu_ln_g, v_sgu_ln_b, v_sgu_w, v_sgu_b, v_w_attn_branch, v_w_sgu_branch, v_w_out, v_norm2_g, v_w_gate_up, v_w_down, v_final_g):
    L = norm1_g.shape[0]
    S, D = x.shape[1], x.shape[2]
    NQ = sinks.shape[1]
    A = NQ * HEAD_DIM
    KV = N_KV_HEADS * HEAD_DIM
    G = sgu_ln_g.shape[1]
    NG = sgu_w.shape[1]
    IN = b_in.shape[1]
    Fd = w_down.shape[1] * N_CHIPS
    dm = dict(D=D, A=A, KV=KV, NQ=NQ, G=G, NG=NG, IN=IN, F=Fd,
              OFF_K=A, OFF_V=A + KV, OFF_Z=A + 2 * KV, OFF_G=A + 2 * KV + 2 * G)
    assert sgu_w.shape[2] == WINDOW and G == NG * LANES and IN == dm["OFF_G"] + 2 * D

    h = x[0]
    target = loss_target[0]
    trig = _trig_tables(positions[0])
    tril = jnp.tril(jnp.ones((WINDOW, WINDOW), bool))

    big = [w_in, w_attn_branch, w_sgu_branch, w_out, w_gate_up, w_down]
    big_m = [m_w_in, m_w_attn_branch, m_w_sgu_branch, m_w_out, m_w_gate_up, m_w_down]
    big_v = [v_w_in, v_w_attn_branch, v_w_sgu_branch, v_w_out, v_w_gate_up, v_w_down]

    placed = [[_cast_place(w, l) for w in big] for l in range(L)]
    IN_, AB, SB, OUT, GU, DOWN = range(len(big))
    gathered = [[None] * len(big) for _ in range(L)]
    gathered[0][IN_] = _gather_weights([placed[0][IN_]]).run()[0]

    def fetch(layer, idx):
        return _gather_weights([placed[layer][t] for t in idx]) if layer < L else None

    def fetched(layer, idx, res):
        if layer >= L:
            return res
        main, got = res
        for t, g in zip(idx, got):
            gathered[layer][t] = g
        return main

    def weights(l):
        flat = lambda w, rows: None if w is None else w.reshape(rows, D)
        w_in_g, w_ab_g, w_sb_g, w_out_g, w_gu_g, w_down_g = gathered[l]
        return (w_in_g, w_ab_g, w_sb_g, flat(w_out_g, D), w_gu_g, flat(w_down_g, Fd))

    def small(l):
        return dict(
            g1=_row(norm1_g[l]), b_in=_row(b_in[l]), sink=_lane_row(sinks[l]),
            ln_g=_row(sgu_ln_g[l]), ln_b=_row(sgu_ln_b[l]),
            w_tril=jnp.where(tril[None], sgu_w[l], 0.0).astype(BF16),
            b_t=jnp.zeros((WINDOW, LANES), F32).at[:, :NG].set(sgu_b[l].T),
            g2=_row(norm2_g[l]))

    saved = []
    for l in range(L):
        sp = small(l)
        xn = _rms_fwd(h, sp["g1"])
        now = [AB, SB, OUT, GU] if l == 0 else [DOWN]
        proj = fetched(l, now, _in_proj(xn, gathered[l][IN_], sp["b_in"], trig, dm, carry=fetch(l, now)))
        w_in_g, w_ab_g, w_sb_g, w_out_g = weights(l)[:4]
        y_attn, lse = _attn_fwd(proj, sp["sink"], dm)
        y_sgu = _sgu_fwd(proj, sp["w_tril"], sp["b_t"], sp["ln_g"], sp["ln_b"], dm)
        a_attn = _branch_attn(y_attn, w_ab_g, dm)
        a_sgu, merged = _branch_sgu_merge(y_sgu, w_sb_g, a_attn, proj, dm)
        if l == 0:
            h_mid = fetched(l, [DOWN], _residual_matmul("out_proj", merged, w_out_g, h, carry=fetch(l, [DOWN])))
        else:
            h_mid = _residual_matmul("out_proj", merged, w_out_g, h)
        w_gu_g, w_down_g = weights(l)[4:]
        hn = _rms_fwd(h_mid, sp["g2"])
        ahead = [IN_, AB, SB, OUT]
        gu, act = fetched(l + 1, ahead, _gate_up(hn, w_gu_g, dm, carry=fetch(l + 1, ahead)))
        h_out = fetched(l + 1, [GU], _residual_matmul("down_proj", act, w_down_g, h_mid, carry=fetch(l + 1, [GU])))
        saved.append(dict(h=h, xn=xn, proj=proj, y_attn=y_attn, lse=lse, y_sgu=y_sgu, a_attn=a_attn, a_sgu=a_sgu,
                          merged=merged, h_mid=h_mid, hn=hn, gu=gu, act=act))
        h = h_out

    dh, dh_b, d_final, loss_part = _loss_head(h, _row(final_g), target)

    small_grads = [None] * L
    reduced = [[None] * len(big) for _ in range(L)]
    early, mid, late = [GU, DOWN], [AB, SB, OUT], [IN_]

    def riding(has_carry, res):
        return res if has_carry else (res, None)

    def sends_of(grads, land):
        return [_pair_sum(g, d) for g, d in zip(grads, land)]

    def finished(grads, land, got):
        return [_chip_sum(g, d, p) for g, d, p in zip(grads, land, got)]

    def file_reduced(layer, idx, fulls):
        for t, f in zip(idx, fulls):
            reduced[layer][t] = f

    late_grads = None
    mid_fulls = None
    n_late, n_mid, n_early = len(late), len(mid), len(early)
    for l in reversed(range(L)):
        w_in_g, w_ab_g, w_sb_g, w_out_g, w_gu_g, w_down_g = weights(l)
        sp, sv = small(l), saved[l]
        have = late_grads is not None
        dgu, rode = riding(have, _down_bwd(
            dh_b, w_down_g, sv["gu"], dm,
            carry=_sibling_exchange(late_grads).beside(_sibling_share(mid_fulls)) if have else None))
        if have:
            land = rode[:n_late]
            file_reduced(l + 1, mid, rode[n_late:])
        g_down = _wgrad_rows("wgrad_down", sv["act"], dh_b)
        dhn, got = riding(have, _gate_up_bwd(dgu, w_gu_g, dm,
                                             carry=_chip_exchange(sends_of(late_grads, land)) if have else None))
        g_gu, shared = riding(have, _wgrad_gate_up(sv["hn"], dgu, dm,
                                                   carry=_sibling_share(finished(late_grads, land, got)) if have else None))
        if have:
            file_reduced(l + 1, late, shared)
        dh_mid, dh_mid_b, d_g2 = _rms_bwd(dhn, sv["h_mid"], sp["g2"], dh)
        early_grads = [g_gu, g_down]
        (da_attn, da_sgu, dgate), land_e = _out_bwd(dh_mid_b, w_out_g, sv["proj"], sv["a_attn"], sv["a_sgu"], dm,
                                                     carry=_sibling_exchange(early_grads))
        sends_e = sends_of(early_grads, land_e)
        g_out = _wgrad_rows("wgrad_out", sv["merged"], dh_mid_b)
        dy_attn = _colsharded_bwd("branch_attn_bwd", da_attn, w_ab_g, BF16)
        dy_sgu = _colsharded_bwd("branch_sgu_bwd", da_sgu, w_sb_g, BF16)
        g_ab = _wgrad_cols("wgrad_attn_branch", sv["y_attn"], da_attn)
        g_sb = _wgrad_cols("wgrad_sgu_branch", sv["y_sgu"], da_sgu)
        mid_grads = [g_ab, g_sb, g_out]
        dq, dk, dv, d_sink = _attn_bwd(sv["proj"], trig, sp["sink"], sv["y_attn"], sv["lse"], dy_attn, dm)
        dz, d_sgu_w, d_bt, d_lng, d_lnb = _sgu_bwd(sv["proj"], sp["w_tril"], sp["b_t"], sp["ln_g"], sp["ln_b"], dy_sgu, dm)
        dproj = jnp.concatenate([dq, dk, dv, dz, dgate[0], dgate[1]], axis=1)
        dxn, rode = _colsharded_bwd("in_proj_bwd", dproj, w_in_g, F32,
                                    carry=_chip_exchange(sends_e).beside(_sibling_exchange(mid_grads)))
        got_e, land_m = rode[:n_early], rode[n_early:]
        (g_in, d_bin), rode = _wgrad_cols(
            "wgrad_in", sv["xn"], dproj, colsum=True,
            carry=_sibling_share(finished(early_grads, land_e, got_e)).beside(_chip_exchange(sends_of(mid_grads, land_m))))
        file_reduced(l, early, rode[:n_early])
        mid_fulls = finished(mid_grads, land_m, rode[n_early:])
        dh, dh_b, d_g1 = _rms_bwd(dxn, sv["h"], sp["g1"], dh_mid)
        late_grads = [g_in]
        small_grads[l] = dict(norm1_g=d_g1[0], b_in=d_bin[0, :, 0, :].reshape(-1), sinks=d_sink[0, :NQ],
                              sgu_ln_g=d_lng[0], sgu_ln_b=d_lnb[0], sgu_w=d_sgu_w, sgu_b=d_bt[:, :NG].T, norm2_g=d_g2[0])
    grad_x = dh[None]

    land = _sibling_exchange(late_grads).run()
    got = _chip_exchange(sends_of(late_grads, land)).run()
    shared = _sibling_share(finished(late_grads, land, got) + mid_fulls).run()
    file_reduced(0, late, shared[:n_late])
    file_reduced(0, mid, shared[n_late:])
    big_out = [_adamw_stacked([reduced[l][t] for l in range(L)], big[t], big_m[t], big_v[t]) for t in range(len(big))]

    names = ["norm1_g", "b_in", "sinks", "sgu_ln_g", "sgu_ln_b", "sgu_w", "sgu_b", "norm2_g"]
    small_w = [norm1_g, b_in, sinks, sgu_ln_g, sgu_ln_b, sgu_w, sgu_b, norm2_g, final_g]
    small_m = [m_norm1_g, m_b_in, m_sinks, m_sgu_ln_g, m_sgu_ln_b, m_sgu_w, m_sgu_b, m_norm2_g, m_final_g]
    small_v = [v_norm1_g, v_b_in, v_sinks, v_sgu_ln_g, v_sgu_ln_b, v_sgu_w, v_sgu_b, v_norm2_g, v_final_g]
    small_g = [jnp.stack([small_grads[l][nm] for l in range(L)]) for nm in names] + [d_final[0]]
    sizes = [w.size for w in small_w]
    total = sum(sizes) + 1
    rows = -(-total // (512 * LANES)) * 512
    loss_piece = jnp.sum(loss_part[0]).reshape(1)
    packed_g = _pack(small_g + [loss_piece], rows)
    one = jnp.ones((1,), F32)
    parts = _gather_all(packed_g)
    outs = _adamw_small(parts, _pack(small_w + [one], rows), _pack(small_m + [one], rows), _pack(small_v + [one], rows))

    def unpack(p):
        flat = p.reshape(-1)
        res, off = [], 0
        for w, n in zip(small_w, sizes):
            res.append(flat[off:off + n].reshape(w.shape))
            off += n
        return res, flat[off]

    (sg, loss), (sd, _), (smm, _), (svv, _) = [unpack(o) for o in outs]

    order = ["norm1_g", "w_in", "b_in", "sinks", "sgu_ln_g", "sgu_ln_b", "sgu_w", "sgu_b", "w_attn_branch",
             "w_sgu_branch", "w_out", "norm2_g", "w_gate_up", "w_down", "final_g"]
    big_names = ["w_in", "w_attn_branch", "w_sgu_branch", "w_out", "w_gate_up", "w_down"]
    small_names = names + ["final_g"]

    def collect(kind):
        res = []
        for nm in order:
            if nm in big_names:
                res.append(big_out[big_names.index(nm)][kind])
            else:
                res.append((sg, sd, smm, svv)[kind][small_names.index(nm)])
        return res

    return (loss, grad_x, *collect(0), *collect(1), *collect(2), *collect(3))
```

```python
import math

import jax
import jax.numpy as jnp
from jax import lax
from jax.experimental import pallas as pl
from jax.experimental.pallas import tpu as pltpu

F32 = jnp.float32
BF16 = jnp.bfloat16
MESH = pl.DeviceIdType.MESH
ANY = pl.BlockSpec(memory_space=pl.ANY)

HEAD_DIM = 64
N_KV_HEADS = 4
WINDOW = 128
ROPE_DIM = HEAD_DIM // 4
ROPE_THETA = 500000.0
EPS = 1e-5
NEG = -1e30
N_CHIPS = 4
LANES = 128
V7X_VMEM_LIMIT = 56 * 1024 * 1024

ADAM_LR = 0.001
ADAM_B1 = 0.9
ADAM_B2 = 0.999
ADAM_EPS = 1e-08
ADAM_WD = 0.01
ADAM_STEP = 10

NN = (((1,), (0,)), ((), ()))
NT = (((1,), (1,)), ((), ()))
TN = (((0,), (0,)), ((), ()))


ROW_TILES = (1024, 512, 256, 128, 64, 32, 16, 8)
BLOCK_BYTES = 2 * 1024 * 1024


def _pick(n, prefs):
    for p in prefs:
        if n % p == 0:
            return p
    raise ValueError(f"no tile for {n} among {prefs}")


def _row_tile(rows, cols, itemsize=4):
    return _pick(rows, [t for t in ROW_TILES if t * cols * itemsize <= BLOCK_BYTES or t == ROW_TILES[-1]])


def _dot(a, b, dims):
    return lax.dot_general(a, b, dims, preferred_element_type=F32)


def _sigmoid(x):
    return 1.0 / (1.0 + jnp.exp(-x))


def _gelu(x):
    return 0.5 * x * (1.0 + lax.erf(x * (1.0 / math.sqrt(2.0))))


def _gelu_grad(x):
    return 0.5 * (1.0 + lax.erf(x * (1.0 / math.sqrt(2.0)))) + x * jnp.exp(-0.5 * x * x) * (1.0 / math.sqrt(2.0 * math.pi))


def _params(sem):
    return pltpu.CompilerParams(dimension_semantics=sem, vmem_limit_bytes=V7X_VMEM_LIMIT)


def _matmul(name, lhs, rhs_list, *, dims, grid, lhs_spec, rhs_specs, acc_shape, out_shape, out_specs,
            epilogue, extra=(), extra_specs=(), carry=None, rhs_colsum=False, cols_outer=False, extra_aliases=None):
    if cols_outer:
        swap = lambda s: s if s.index_map is None else pl.BlockSpec(s.block_shape, lambda j, i, k, f=s.index_map: f(i, j, k))
        grid = (grid[1], grid[0], grid[2])
        lhs_spec, rhs_specs = swap(lhs_spec), [swap(s) for s in rhs_specs]
        extra_specs, out_specs = [swap(s) for s in extra_specs], [swap(s) for s in out_specs]
    gk = grid[2]
    nr, ne, no = len(rhs_list), len(extra), len(out_shape)
    nci = len(carry.ins) if carry else 0
    nco = len(carry.outs) if carry else 0
    acc_shapes = [acc_shape] * nr + ([(8, acc_shape[1])] if rhs_colsum else [])
    nacc = len(acc_shapes) if gk > 1 else 0

    def body(*refs):
        a_ref = refs[0]
        b_refs = refs[1:1 + nr]
        e_refs = refs[1 + nr:1 + nr + ne]
        base = 1 + nr + ne
        ci_refs = refs[base:base + nci]
        o_refs = refs[base + nci:base + nci + no]
        co_refs = refs[base + nci + no:base + nci + no + nco]
        acc_refs = refs[base + nci + no + nco:base + nci + no + nco + nacc]
        sems = refs[base + nci + no + nco + nacc:]
        ids = [pl.program_id(d) for d in range(3)]
        if carry:
            @pl.when((ids[0] == 0) & (ids[1] == 0) & (ids[2] == 0))
            def _():
                carry.start(ci_refs, co_refs, *sems)

        a = a_ref[...]
        if gk == 1:
            n_axis = 1 - dims[0][1][0]
            for cols in _col_chunks(acc_shape[1]):
                pick = (slice(None), cols) if n_axis == 1 else (cols, slice(None))
                parts = [_dot(a, b[pick], dims) for b in b_refs]
                if rhs_colsum:
                    b0 = b_refs[0][pick]
                    parts.append(_dot(jnp.ones((8, b0.shape[0]), b0.dtype), b0, NN))
                epilogue(parts, e_refs, o_refs, cols)
        else:
            k = ids[2]

            @pl.when(k == 0)
            def _():
                for acc in acc_refs:
                    acc[...] = jnp.zeros_like(acc)

            for acc, b in zip(acc_refs, b_refs):
                acc[...] += _dot(a, b[...], dims)
            if rhs_colsum:
                b0 = b_refs[0][...]
                acc_refs[-1][...] += _dot(jnp.ones((8, b0.shape[0]), b0.dtype), b0, NN)

            @pl.when(k == gk - 1)
            def _():
                epilogue([acc[...] for acc in acc_refs], e_refs, o_refs, slice(None))

        if carry:
            @pl.when((ids[0] == grid[0] - 1) & (ids[1] == grid[1] - 1) & (ids[2] == grid[2] - 1))
            def _():
                carry.finish(ci_refs, co_refs, *sems)

    scratch = [pltpu.VMEM(s, F32) for s in acc_shapes[:nacc]]
    kwargs = {}
    aliases = {1 + nr + e: o for e, o in (extra_aliases or {}).items()}
    if carry:
        scratch += carry.sem_scratch()
        aliases.update({1 + nr + ne + i: no + o for i, o in carry.aliases.items()})
    if aliases:
        kwargs["input_output_aliases"] = aliases
    outs = pl.pallas_call(
        body, name=name, grid=grid,
        in_specs=[lhs_spec, *rhs_specs, *extra_specs, *([ANY] * nci)],
        out_specs=[*out_specs, *([ANY] * nco)],
        out_shape=[*out_shape, *(carry.outs if carry else [])], scratch_shapes=scratch,
        compiler_params=_params(("arbitrary",) * 3 if carry else ("parallel", "parallel", "arbitrary")),
        **kwargs,
    )(lhs, *rhs_list, *extra, *(carry.ins if carry else []))
    return outs


class _Comm:
    def __init__(self, name, ins, outs, aliases, n_sems, start, finish):
        self.name, self.ins, self.outs, self.aliases, self.n_sems = name, list(ins), list(outs), dict(aliases), n_sems
        self.start, self.finish = start, finish

    def sem_scratch(self):
        return [pltpu.SemaphoreType.DMA((self.n_sems,)), pltpu.SemaphoreType.DMA((self.n_sems,))]

    def beside(self, other):
        ni, no, ns = len(self.ins), len(self.outs), self.n_sems

        def split(ins, outs, send_sems, recv_sems):
            mine = (ins[:ni], outs[:no], send_sems.at[pl.ds(0, ns)], recv_sems.at[pl.ds(0, ns)])
            theirs = (ins[ni:], outs[no:], send_sems.at[pl.ds(ns, other.n_sems)], recv_sems.at[pl.ds(ns, other.n_sems)])
            return mine, theirs

        def start(*refs):
            mine, theirs = split(*refs)
            self.start(*mine)
            other.start(*theirs)

        def finish(*refs):
            mine, theirs = split(*refs)
            self.finish(*mine)
            other.finish(*theirs)

        aliases = {**self.aliases, **{ni + i: no + o for i, o in other.aliases.items()}}
        return _Comm(self.name + "+" + other.name, self.ins + other.ins, self.outs + other.outs, aliases,
                     ns + other.n_sems, start, finish)

    def run(self):
        ni = len(self.ins)

        def body(*refs):
            in_refs, out_refs, sems = refs[:ni], refs[ni:ni + len(self.outs)], refs[ni + len(self.outs):]
            self.start(in_refs, out_refs, *sems)
            self.finish(in_refs, out_refs, *sems)

        return pl.pallas_call(
            body, name=self.name, in_specs=[ANY] * ni, out_specs=[ANY] * len(self.outs), out_shape=self.outs,
            input_output_aliases=self.aliases, scratch_shapes=self.sem_scratch(),
        )(*self.ins)


MXU_CHUNK = 256


def _col_chunks(n):
    if n % LANES:
        return [slice(0, n)]
    return [slice(s, min(s + MXU_CHUNK, n)) for s in range(0, n, MXU_CHUNK)]


def _store_epilogue(dtype):
    def ep(parts, e_refs, o_refs, cols):
        o_refs[0][:, cols] = parts[0].astype(dtype)
    return ep


def _rms_fwd(h, g_row):
    S, D = h.shape
    tm = _row_tile(S, D)

    def body(h_ref, g_ref, o_ref):
        x = h_ref[...]
        r = lax.rsqrt(jnp.mean(x * x, axis=-1, keepdims=True) + EPS)
        o_ref[...] = (x * r * g_ref[...]).astype(BF16)

    return pl.pallas_call(
        body, name="rms_fwd", grid=(S // tm,),
        in_specs=[pl.BlockSpec((tm, D), lambda i: (i, 0)), pl.BlockSpec((1, D), lambda i: (0, 0))],
        out_specs=pl.BlockSpec((tm, D), lambda i: (i, 0)),
        out_shape=jax.ShapeDtypeStruct((S, D), BF16),
        compiler_params=_params(("parallel",)),
    )(h, g_row)


def _rms_bwd(dy, h, g_row, dres):
    S, D = h.shape
    tm = _row_tile(S, D)

    def body(dy_ref, h_ref, g_ref, dres_ref, dh_ref, dhb_ref, dg_ref):
        i = pl.program_id(0)
        x = h_ref[...]
        d = dy_ref[...]
        r = lax.rsqrt(jnp.mean(x * x, axis=-1, keepdims=True) + EPS)
        dg = d * g_ref[...]
        dot = jnp.mean(dg * x, axis=-1, keepdims=True)
        dh = dres_ref[...] + r * dg - x * (r * r * r) * dot
        dh_ref[...] = dh
        dhb_ref[...] = dh.astype(BF16)
        part = jnp.sum(d * x * r, axis=0, keepdims=True)

        @pl.when(i == 0)
        def _():
            dg_ref[...] = jnp.zeros_like(dg_ref)

        dg_ref[0:1, :] += part

    return pl.pallas_call(
        body, name="rms_bwd", grid=(S // tm,),
        in_specs=[pl.BlockSpec((tm, D), lambda i: (i, 0)), pl.BlockSpec((tm, D), lambda i: (i, 0)),
                  pl.BlockSpec((1, D), lambda i: (0, 0)), pl.BlockSpec((tm, D), lambda i: (i, 0))],
        out_specs=[pl.BlockSpec((tm, D), lambda i: (i, 0)), pl.BlockSpec((tm, D), lambda i: (i, 0)),
                   pl.BlockSpec((8, D), lambda i: (0, 0))],
        out_shape=[jax.ShapeDtypeStruct((S, D), F32), jax.ShapeDtypeStruct((S, D), BF16),
                   jax.ShapeDtypeStruct((8, D), F32)],
        compiler_params=_params(("arbitrary",)),
    )(dy, h, g_row, dres)


def _loss_head(h, g_row, target):
    S, D = h.shape
    tm = _row_tile(S, D)

    def body(h_ref, g_ref, t_ref, dh_ref, dhb_ref, dg_ref, loss_ref):
        i = pl.program_id(0)
        x = h_ref[...]
        g = g_ref[...]
        r = lax.rsqrt(jnp.mean(x * x, axis=-1, keepdims=True) + EPS)
        y = x * r * g
        e = y - t_ref[...]
        d = e * (1.0 / D)
        dg = d * g
        dot = jnp.mean(dg * x, axis=-1, keepdims=True)
        dh = r * dg - x * (r * r * r) * dot
        dh_ref[...] = dh
        dhb_ref[...] = dh.astype(BF16)

        @pl.when(i == 0)
        def _():
            dg_ref[...] = jnp.zeros_like(dg_ref)
            loss_ref[...] = jnp.zeros_like(loss_ref)

        dg_ref[0:1, :] += jnp.sum(d * x * r, axis=0, keepdims=True)
        loss_ref[0:1, :] += jnp.sum((0.5 / D) * e * e, axis=0, keepdims=True)

    return pl.pallas_call(
        body, name="loss_head", grid=(S // tm,),
        in_specs=[pl.BlockSpec((tm, D), lambda i: (i, 0)), pl.BlockSpec((1, D), lambda i: (0, 0)),
                  pl.BlockSpec((tm, D), lambda i: (i, 0))],
        out_specs=[pl.BlockSpec((tm, D), lambda i: (i, 0)), pl.BlockSpec((tm, D), lambda i: (i, 0)),
                   pl.BlockSpec((8, D), lambda i: (0, 0)), pl.BlockSpec((8, D), lambda i: (0, 0))],
        out_shape=[jax.ShapeDtypeStruct((S, D), F32), jax.ShapeDtypeStruct((S, D), BF16),
                   jax.ShapeDtypeStruct((8, D), F32), jax.ShapeDtypeStruct((8, D), F32)],
        compiler_params=_params(("arbitrary",)),
    )(h, g_row, target)


def _rotary_partner(t):
    half = ROPE_DIM // 2
    if t.shape[-1] == LANES:
        lane = lax.broadcasted_iota(jnp.int32, t.shape, 1) & (HEAD_DIM - 1)
        return jnp.where(lane < half, pltpu.roll(t, LANES - half, 1), pltpu.roll(t, half, 1))
    r = lax.broadcasted_iota(jnp.int32, (LANES, LANES), 0)
    c = lax.broadcasted_iota(jnp.int32, (LANES, LANES), 1)
    cm = c & (HEAD_DIM - 1)
    perm = (((cm < half) & (r == c + half)) | ((cm >= half) & (cm < ROPE_DIM) & (r == c - half))).astype(BF16)
    hi = t.astype(BF16)
    lo = (t - hi.astype(F32)).astype(BF16)
    cols = [slice(s, s + LANES) for s in range(0, t.shape[-1], LANES)]
    return jnp.concatenate([_dot(hi[:, c_], perm, NN) + _dot(lo[:, c_], perm, NN) for c_ in cols], axis=1)


def _rope(t, cos, sin):
    return t * cos + _rotary_partner(t) * sin


def _rope_t(g, cos, sin):
    return g * cos + _rotary_partner(g * sin)


def _band_mask(n, qpk):
    qi = lax.broadcasted_iota(jnp.int32, (qpk * WINDOW, 2 * WINDOW), 0) & (WINDOW - 1)
    kj = lax.broadcasted_iota(jnp.int32, (qpk * WINDOW, 2 * WINDOW), 1)
    rel = qi + WINDOW - kj
    ok = (rel >= 0) & (rel < WINDOW)
    return ok & ((kj >= WINDOW) | (n > 0))


def _stack_heads(x, g, qpk):
    return jnp.concatenate([x[:, (g * qpk + hh) * HEAD_DIM:(g * qpk + hh + 1) * HEAD_DIM] for hh in range(qpk)], axis=0)


def _stack_cols(row, g, qpk):
    return jnp.concatenate([row[:, g * qpk + hh:g * qpk + hh + 1] for hh in range(qpk)], axis=0)


def _attn_specs(dm, nb):
    A, KV = dm["A"], dm["KV"]
    kb, vb = dm["OFF_K"] // KV, dm["OFF_V"] // KV
    cur = lambda n: jnp.minimum(n, nb - 1)
    prev = lambda n: jnp.maximum(jnp.minimum(n, nb - 1) - 1, 0)
    proj_specs = [
        pl.BlockSpec((WINDOW, A), lambda n: (cur(n), 0)),
        pl.BlockSpec((WINDOW, KV), lambda n: (prev(n), kb)),
        pl.BlockSpec((WINDOW, KV), lambda n: (cur(n), kb)),
        pl.BlockSpec((WINDOW, KV), lambda n: (prev(n), vb)),
        pl.BlockSpec((WINDOW, KV), lambda n: (cur(n), vb)),
    ]
    trig_cur = [pl.BlockSpec((WINDOW, LANES), lambda n: (cur(n), 0)) for _ in range(2)]
    trig_prev = [pl.BlockSpec((WINDOW, LANES), lambda n: (prev(n), 0)) for _ in range(2)]
    return proj_specs, trig_cur, trig_prev, cur, prev


def _attn_fwd(proj, sink_row, dm):
    S = proj.shape[0]
    A, KV, NQ = dm["A"], dm["KV"], dm["NQ"]
    qpk = NQ // N_KV_HEADS
    nb = S // WINDOW
    scale = HEAD_DIM ** -0.5
    proj_specs = _attn_specs(dm, nb)[0]

    def body(q_ref, kp_ref, kc_ref, vp_ref, vc_ref, sink_ref, y_ref, lse_ref):
        n = pl.program_id(0)
        qr = q_ref[...]
        kr = jnp.concatenate([kp_ref[...], kc_ref[...]], axis=0)
        vband = jnp.concatenate([vp_ref[...], vc_ref[...]], axis=0)
        mask = _band_mask(n, qpk)
        lane = lax.broadcasted_iota(jnp.int32, (WINDOW, LANES), 1)
        lse_all = jnp.zeros((WINDOW, LANES), F32)
        sink_rows = jnp.broadcast_to(sink_ref[0:1, :], (WINDOW, LANES))
        groups = range(N_KV_HEADS)
        head = lambda x, g: x[:, g * HEAD_DIM:(g + 1) * HEAD_DIM]
        ones = jnp.ones((2 * WINDOW, HEAD_DIM), BF16)
        sink = [_stack_cols(sink_rows, g, qpk) for g in groups]
        s = [jnp.where(mask, _dot(_stack_heads(qr, g, qpk), head(kr, g), NT) * scale, NEG) for g in groups]
        m = [jnp.maximum(jnp.max(s[g], axis=-1, keepdims=True), sink[g]) for g in groups]
        p = [jnp.exp(s[g] - m[g]).astype(BF16) for g in groups]
        ov = [_dot(p[g], jnp.concatenate([head(vband, g), ones], axis=1), NN) for g in groups]
        den = [ov[g][:, HEAD_DIM:HEAD_DIM + 1] + jnp.exp(sink[g] - m[g]) for g in groups]
        o = [ov[g][:, :HEAD_DIM] * (1.0 / den[g]) for g in groups]
        lse = [m[g] + jnp.log(den[g]) for g in groups]
        for g in groups:
            for hh in range(qpk):
                h = g * qpk + hh
                rows = slice(hh * WINDOW, (hh + 1) * WINDOW)
                y_ref[:, h * HEAD_DIM:(h + 1) * HEAD_DIM] = o[g][rows].astype(BF16)
                lse_all = jnp.where(lane == h, lse[g][rows], lse_all)
        lse_ref[...] = lse_all

    return pl.pallas_call(
        body, name="attn_fwd", grid=(nb,),
        in_specs=[*proj_specs, pl.BlockSpec((8, LANES), lambda n: (0, 0))],
        out_specs=[pl.BlockSpec((WINDOW, A), lambda n: (n, 0)), pl.BlockSpec((WINDOW, LANES), lambda n: (n, 0))],
        out_shape=[jax.ShapeDtypeStruct((S, A), BF16), jax.ShapeDtypeStruct((S, LANES), F32)],
        compiler_params=_params(("parallel",)),
    )(proj, proj, proj, proj, proj, sink_row)


def _attn_bwd(proj, trig, sink_row, y, lse, dy, dm):
    S = proj.shape[0]
    A, KV, NQ = dm["A"], dm["KV"], dm["NQ"]
    qpk = NQ // N_KV_HEADS
    nb = S // WINDOW
    scale = HEAD_DIM ** -0.5
    proj_specs, trig_cur, trig_prev, cur, prev = _attn_specs(dm, nb)

    def body(q_ref, kp_ref, kc_ref, vp_ref, vc_ref, cc_ref, sc_ref, cp_ref, sp_ref,
             sink_ref, y_ref, lse_ref, dy_ref, dq_ref, dk_ref, dv_ref, dsink_ref,
             ck_ref, cv_ref, bk_ref, bv_ref, dqr_ref):
        n = pl.program_id(0)

        @pl.when(n == 0)
        def _():
            dsink_ref[...] = jnp.zeros_like(dsink_ref)
            ck_ref[...] = jnp.zeros_like(ck_ref)
            cv_ref[...] = jnp.zeros_like(cv_ref)

        @pl.when(n < nb)
        def _():
            tq = lambda r: jnp.tile(r[...], (1, A // LANES))
            tk = lambda rp, rc: jnp.tile(jnp.concatenate([rp[...], rc[...]], axis=0), (1, KV // LANES))
            cq, sq = tq(cc_ref), tq(sc_ref)
            ck, sk = tk(cp_ref, cc_ref), tk(sp_ref, sc_ref)
            qr = q_ref[...]
            kr = jnp.concatenate([kp_ref[...], kc_ref[...]], axis=0)
            vband = jnp.concatenate([vp_ref[...], vc_ref[...]], axis=0)
            mask = _band_mask(n, qpk)
            lane = lax.broadcasted_iota(jnp.int32, (1, LANES), 1)
            lse_all = lse_ref[...]
            sink_rows = jnp.broadcast_to(sink_ref[0:1, :], (WINDOW, LANES))
            dy_all = dy_ref[...]
            y_all = y_ref[...]
            dsink = jnp.zeros((1, LANES), F32)
            groups = range(N_KV_HEADS)
            head = lambda x, g: x[:, g * HEAD_DIM:(g + 1) * HEAD_DIM]
            q = [_stack_heads(qr, g, qpk) for g in groups]
            dy = [_stack_heads(dy_all, g, qpk) for g in groups]
            lse = [_stack_cols(lse_all, g, qpk) for g in groups]
            s = [jnp.where(mask, _dot(q[g], head(kr, g), NT) * scale, NEG) for g in groups]
            dp = [_dot(dy[g], head(vband, g), NT) for g in groups]
            delta = [jnp.sum(dy[g].astype(F32) * _stack_heads(y_all, g, qpk).astype(F32), axis=-1, keepdims=True)
                     for g in groups]
            p = [jnp.exp(s[g] - lse[g]) for g in groups]
            ds = [(p[g] * (dp[g] - delta[g]) * scale).astype(BF16) for g in groups]
            dq = [_dot(ds[g], head(kr, g), NN) for g in groups]
            for g in groups:
                bk_ref[:, g * HEAD_DIM:(g + 1) * HEAD_DIM] = _dot(ds[g], q[g], TN)
                bv_ref[:, g * HEAD_DIM:(g + 1) * HEAD_DIM] = _dot(p[g].astype(BF16), dy[g], TN)
            for g in groups:
                sink_d = jnp.exp(_stack_cols(sink_rows, g, qpk) - lse[g]) * delta[g]
                for hh in range(qpk):
                    h = g * qpk + hh
                    rows = slice(hh * WINDOW, (hh + 1) * WINDOW)
                    dqr_ref[:, h * HEAD_DIM:(h + 1) * HEAD_DIM] = dq[g][rows]
                    dsink = dsink + jnp.where(lane == h, -jnp.sum(sink_d[rows], axis=0, keepdims=True), 0.0)
            dsink_ref[0:1, :] += dsink
            dq_ref[...] = _rope_t(dqr_ref[...], cq, sq).astype(BF16)
            dkb = _rope_t(bk_ref[...], ck, sk)
            dvb = bv_ref[...]
            dk_ref[...] = (ck_ref[...] + dkb[:WINDOW]).astype(BF16)
            dv_ref[...] = (cv_ref[...] + dvb[:WINDOW]).astype(BF16)
            ck_ref[...] = dkb[WINDOW:]
            cv_ref[...] = dvb[WINDOW:]

        @pl.when(n == nb)
        def _():
            dk_ref[...] = ck_ref[...].astype(BF16)
            dv_ref[...] = cv_ref[...].astype(BF16)

    row = lambda w: pl.BlockSpec((WINDOW, w), lambda n: (cur(n), 0))
    done = lambda w: pl.BlockSpec((WINDOW, w), lambda n: (jnp.maximum(n - 1, 0), 0))
    return pl.pallas_call(
        body, name="attn_bwd", grid=(nb + 1,),
        in_specs=[*proj_specs, *trig_cur, *trig_prev, pl.BlockSpec((8, LANES), lambda n: (0, 0)),
                  row(A), row(LANES), row(A)],
        out_specs=[row(A), done(KV), done(KV), pl.BlockSpec((8, LANES), lambda n: (0, 0))],
        out_shape=[jax.ShapeDtypeStruct((S, A), BF16), jax.ShapeDtypeStruct((S, KV), BF16),
                   jax.ShapeDtypeStruct((S, KV), BF16), jax.ShapeDtypeStruct((8, LANES), F32)],
        scratch_shapes=[pltpu.VMEM((WINDOW, KV), F32), pltpu.VMEM((WINDOW, KV), F32),
                        pltpu.VMEM((2 * WINDOW, KV), F32), pltpu.VMEM((2 * WINDOW, KV), F32),
                        pltpu.VMEM((WINDOW, A), F32)],
        compiler_params=_params(("arbitrary",)),
    )(proj, proj, proj, proj, proj, *trig, *trig, sink_row, y, lse, dy)


def _sgu_layout(dm, S):
    G = dm["G"]
    pw = math.gcd(dm["OFF_Z"], G)
    npc = G // pw
    tm = _pick(S, (256, 128))
    u_specs = [pl.BlockSpec((tm, pw), lambda i, p=p: (i, dm["OFF_Z"] // pw + p)) for p in range(npc)]
    v_specs = [pl.BlockSpec((tm, pw), lambda i, p=p: (i, (dm["OFF_Z"] + G) // pw + p)) for p in range(npc)]
    return pw, npc, tm, u_specs, v_specs


def _sgu_norm(v_refs, lg_ref, lb_ref):
    v = jnp.concatenate([_gelu(r[...].astype(F32)) for r in v_refs], axis=1)
    mu = jnp.mean(v, axis=-1, keepdims=True)
    vc = v - mu
    rstd = lax.rsqrt(jnp.mean(vc * vc, axis=-1, keepdims=True) + EPS)
    xhat = vc * rstd
    return xhat, rstd, (xhat * lg_ref[...] + lb_ref[...]).astype(BF16)


def _sgu_fwd(proj, w_tril, b_t, ln_g_row, ln_b_row, dm):
    S = proj.shape[0]
    G, NG = dm["G"], dm["NG"]
    pw, npc, tm, u_specs, v_specs = _sgu_layout(dm, S)
    nch = tm // WINDOW

    def body(*refs):
        u_refs, v_refs = refs[:npc], refs[npc:2 * npc]
        w_ref, bt_ref, lg_ref, lb_ref, y_ref = refs[2 * npc:]
        _, _, vn = _sgu_norm(v_refs, lg_ref, lb_ref)
        u = jnp.concatenate([_gelu(r[...].astype(F32)) for r in u_refs], axis=1)
        for c in range(nch):
            rows = slice(c * WINDOW, (c + 1) * WINDOW)
            for g in range(NG):
                cols = slice(g * LANES, (g + 1) * LANES)
                sv = _dot(w_ref[g], vn[rows, cols], NN) + bt_ref[:, g:g + 1]
                y_ref[rows, cols] = (u[rows, cols] * sv).astype(BF16)

    return pl.pallas_call(
        body, name="sgu_fwd", grid=(S // tm,),
        in_specs=[*u_specs, *v_specs,
                  pl.BlockSpec((NG, WINDOW, WINDOW), lambda i: (0, 0, 0)),
                  pl.BlockSpec((WINDOW, LANES), lambda i: (0, 0)),
                  pl.BlockSpec((1, G), lambda i: (0, 0)), pl.BlockSpec((1, G), lambda i: (0, 0))],
        out_specs=pl.BlockSpec((tm, G), lambda i: (i, 0)),
        out_shape=jax.ShapeDtypeStruct((S, G), BF16),
        compiler_params=_params(("parallel",)),
    )(*([proj] * (2 * npc)), w_tril, b_t, ln_g_row, ln_b_row)


def _sgu_bwd(proj, w_tril, b_t, ln_g_row, ln_b_row, dy, dm):
    S = proj.shape[0]
    G, NG = dm["G"], dm["NG"]
    pw, npc, tm, u_specs, v_specs = _sgu_layout(dm, S)
    nch = tm // WINDOW

    def body(*refs):
        u_refs, v_refs = refs[:npc], refs[npc:2 * npc]
        w_ref, bt_ref, lg_ref, lb_ref, dy_ref, dz_ref, dw_ref, dbt_ref, dlg_ref, dlb_ref, dvn_ref = refs[2 * npc:]
        i = pl.program_id(0)

        @pl.when(i == 0)
        def _():
            dw_ref[...] = jnp.zeros_like(dw_ref)
            dbt_ref[...] = jnp.zeros_like(dbt_ref)
            dlg_ref[...] = jnp.zeros_like(dlg_ref)
            dlb_ref[...] = jnp.zeros_like(dlb_ref)

        xhat, rstd, vn = _sgu_norm(v_refs, lg_ref, lb_ref)
        u_pre = jnp.concatenate([r[...].astype(F32) for r in u_refs], axis=1)
        u = _gelu(u_pre)
        dy = dy_ref[...].astype(F32)
        lane = lax.broadcasted_iota(jnp.int32, (WINDOW, LANES), 1)
        tri = lax.broadcasted_iota(jnp.int32, (WINDOW, WINDOW), 0) >= lax.broadcasted_iota(jnp.int32, (WINDOW, WINDOW), 1)
        dbt = jnp.zeros((WINDOW, LANES), F32)
        for c in range(nch):
            rows = slice(c * WINDOW, (c + 1) * WINDOW)
            for g in range(NG):
                cols = slice(g * LANES, (g + 1) * LANES)
                vn_cg = vn[rows, cols]
                sv = _dot(w_ref[g], vn_cg, NN) + bt_ref[:, g:g + 1]
                dy_cg = dy[rows, cols]
                dsv = dy_cg * u[rows, cols]
                dsv_b = dsv.astype(BF16)
                dz_ref[rows, cols] = (dy_cg * sv * _gelu_grad(u_pre[rows, cols])).astype(BF16)
                dvn_ref[rows, cols] = _dot(w_ref[g], dsv_b, TN)
                dw_ref[g] += jnp.where(tri, _dot(dsv_b, vn_cg, NT), 0.0)
                dbt = dbt + jnp.where(lane == g, jnp.sum(dsv, axis=-1, keepdims=True), 0.0)
        dbt_ref[...] += dbt
        dvn = dvn_ref[...]
        dlg_ref[0:1, :] += jnp.sum(dvn * xhat, axis=0, keepdims=True)
        dlb_ref[0:1, :] += jnp.sum(dvn, axis=0, keepdims=True)
        dxh = dvn * lg_ref[...]
        dv = rstd * (dxh - jnp.mean(dxh, axis=-1, keepdims=True) - xhat * jnp.mean(dxh * xhat, axis=-1, keepdims=True))
        v_pre = jnp.concatenate([r[...].astype(F32) for r in v_refs], axis=1)
        dz_ref[:, G:] = (dv * _gelu_grad(v_pre)).astype(BF16)

    return pl.pallas_call(
        body, name="sgu_bwd", grid=(S // tm,),
        in_specs=[*u_specs, *v_specs,
                  pl.BlockSpec((NG, WINDOW, WINDOW), lambda i: (0, 0, 0)),
                  pl.BlockSpec((WINDOW, LANES), lambda i: (0, 0)),
                  pl.BlockSpec((1, G), lambda i: (0, 0)), pl.BlockSpec((1, G), lambda i: (0, 0)),
                  pl.BlockSpec((tm, G), lambda i: (i, 0))],
        out_specs=[pl.BlockSpec((tm, 2 * G), lambda i: (i, 0)),
                   pl.BlockSpec((NG, WINDOW, WINDOW), lambda i: (0, 0, 0)),
                   pl.BlockSpec((WINDOW, LANES), lambda i: (0, 0)),
                   pl.BlockSpec((8, G), lambda i: (0, 0)), pl.BlockSpec((8, G), lambda i: (0, 0))],
        out_shape=[jax.ShapeDtypeStruct((S, 2 * G), BF16), jax.ShapeDtypeStruct((NG, WINDOW, WINDOW), F32),
                   jax.ShapeDtypeStruct((WINDOW, LANES), F32), jax.ShapeDtypeStruct((8, G), F32),
                   jax.ShapeDtypeStruct((8, G), F32)],
        scratch_shapes=[pltpu.VMEM((tm, G), F32)],
        compiler_params=_params(("arbitrary",)),
    )(*([proj] * (2 * npc)), w_tril, b_t, ln_g_row, ln_b_row, dy)


def _result(outs, n_main, carry):
    main = outs[0] if n_main == 1 else tuple(outs[:n_main])
    return (main, list(outs[n_main:])) if carry else main


def _in_proj(xn, w_in_g, b_row, trig, dm, carry=None):
    S, D = xn.shape
    IN = dm["IN"]
    cw = IN // N_CHIPS
    tm = _pick(S, (512, 256, 128))
    tn = _pick(cw, (1920, 640, 512, 256, 128))
    nbc = cw // tn
    rope_cols = dm["OFF_V"]
    rope_blocks = -(-rope_cols // tn)

    def roped_store(val, e_refs, o_refs, cols, jj):
        r = min(max(rope_cols - (jj * tn + cols.start), 0), cols.stop - cols.start)
        if not r:
            o_refs[0][:, cols] = val.astype(BF16)
            return
        for rows in [slice(s, s + WINDOW) for s in range(0, tm, WINDOW)]:
            cos, sin = e_refs[1][rows, :], e_refs[2][rows, :]
            for c0 in range(0, r, LANES):
                piece = val[rows, c0:c0 + LANES]
                o_refs[0][rows, cols.start + c0:cols.start + c0 + LANES] = _rope(piece, cos, sin).astype(BF16)
        if r < val.shape[1]:
            o_refs[0][:, cols.start + r:cols.stop] = val[:, r:].astype(BF16)

    def ep_rope(parts, e_refs, o_refs, cols):
        val = parts[0] + e_refs[0][:, cols]
        if rope_blocks == 1:
            roped_store(val, e_refs, o_refs, cols, 0)
        else:
            for jj in range(rope_blocks):
                @pl.when(pl.program_id(0) == jj)
                def _(jj=jj):
                    roped_store(val, e_refs, o_refs, cols, jj)

    def ep_plain(parts, e_refs, o_refs, cols):
        o_refs[0][:, cols] = (parts[0] + e_refs[0][:, cols]).astype(BF16)

    rows = pl.BlockSpec((tm, LANES), lambda i, j, k: (i, 0))
    first = IN // tn - rope_blocks
    common = dict(dims=NN, lhs_spec=pl.BlockSpec((tm, D), lambda i, j, k: (i, 0)), acc_shape=(tm, tn),
                  out_shape=[jax.ShapeDtypeStruct((S, IN), BF16)], cols_outer=True)
    proj = _matmul(
        "in_proj_qk", xn, [w_in_g], grid=(S // tm, rope_blocks, 1),
        rhs_specs=[pl.BlockSpec((None, D, tn), lambda i, j, k: (j // nbc, 0, j % nbc))],
        extra=[b_row, *trig], extra_specs=[pl.BlockSpec((1, tn), lambda i, j, k: (0, j)), rows, rows],
        out_specs=[pl.BlockSpec((tm, tn), lambda i, j, k: (i, j))], epilogue=ep_rope, **common)[0]
    jb = lambda j: j + rope_blocks
    return _result(_matmul(
        "in_proj", xn, [w_in_g], grid=(S // tm, first, 1),
        rhs_specs=[pl.BlockSpec((None, D, tn), lambda i, j, k: (jb(j) // nbc, 0, jb(j) % nbc))],
        extra=[b_row, proj], extra_specs=[pl.BlockSpec((1, tn), lambda i, j, k: (0, jb(j))), ANY], extra_aliases={1: 0},
        out_specs=[pl.BlockSpec((tm, tn), lambda i, j, k: (i, jb(j)))], epilogue=ep_plain, carry=carry, **common), 1, carry)


def _branch_attn(y_attn, w_ab_g, dm):
    S, A = y_attn.shape
    D = dm["D"]
    cw = D // N_CHIPS
    tm = _pick(S, (1024, 512, 256, 128))
    return _matmul(
        "branch_attn", y_attn, [w_ab_g], dims=NN, grid=(S // tm, N_CHIPS, 1),
        lhs_spec=pl.BlockSpec((tm, A), lambda i, j, k: (i, 0)),
        rhs_specs=[pl.BlockSpec((None, A, cw), lambda i, j, k: (j, 0, 0))],
        acc_shape=(tm, cw), out_shape=[jax.ShapeDtypeStruct((S, D), BF16)],
        out_specs=[pl.BlockSpec((tm, cw), lambda i, j, k: (i, j))], epilogue=_store_epilogue(BF16))[0]


def _branch_sgu_merge(y_sgu, w_sb_g, a_attn, proj, dm):
    S, G = y_sgu.shape
    D, OFF_G = dm["D"], dm["OFF_G"]
    cw = D // N_CHIPS
    tm = _pick(S, (1024, 512, 256, 128))

    def ep(parts, e_refs, o_refs, cols):
        a_sgu = parts[0].astype(BF16)
        ga = _sigmoid(e_refs[1][:, cols].astype(F32))
        gs = _sigmoid(e_refs[2][:, cols].astype(F32))
        o_refs[0][:, cols] = a_sgu
        o_refs[1][:, cols] = (ga * e_refs[0][:, cols].astype(F32) + gs * a_sgu.astype(F32)).astype(BF16)

    blk = pl.BlockSpec((tm, cw), lambda i, j, k: (i, j))
    return _matmul(
        "branch_sgu_merge", y_sgu, [w_sb_g], dims=NN, grid=(S // tm, N_CHIPS, 1),
        lhs_spec=pl.BlockSpec((tm, G), lambda i, j, k: (i, 0)),
        rhs_specs=[pl.BlockSpec((None, G, cw), lambda i, j, k: (j, 0, 0))],
        acc_shape=(tm, cw), extra=[a_attn, proj, proj],
        extra_specs=[blk, pl.BlockSpec((tm, cw), lambda i, j, k: (i, OFF_G // cw + j)),
                     pl.BlockSpec((tm, cw), lambda i, j, k: (i, (OFF_G + D) // cw + j))],
        out_shape=[jax.ShapeDtypeStruct((S, D), BF16), jax.ShapeDtypeStruct((S, D), BF16)],
        out_specs=[blk, blk], epilogue=ep)


def _residual_matmul(name, a, w_g, h, carry=None):
    S, K = a.shape
    D = w_g.shape[1]
    tm = _pick(S, (1024, 512, 256, 128))
    tn = _pick(D, (512, 256, 128))

    def ep(parts, e_refs, o_refs, cols):
        o_refs[0][:, cols] = e_refs[0][:, cols] + parts[0]

    blk = pl.BlockSpec((tm, tn), lambda i, j, k: (i, j))
    return _result(_matmul(
        name, a, [w_g], dims=NN, grid=(S // tm, D // tn, 1),
        lhs_spec=pl.BlockSpec((tm, K), lambda i, j, k: (i, 0)),
        rhs_specs=[pl.BlockSpec((K, tn), lambda i, j, k: (0, j))],
        acc_shape=(tm, tn), extra=[h], extra_specs=[blk],
        out_shape=[jax.ShapeDtypeStruct((S, D), F32)], out_specs=[blk], epilogue=ep, carry=carry), 1, carry)


def _gate_up(hn, w_gu_g, dm, carry=None):
    S, D = hn.shape
    Fd = dm["F"]
    cw = 2 * Fd // N_CHIPS
    tm = _pick(S, (512, 256, 128))
    tn = _pick(cw, (1408, 512, 384, 256, 128))
    nbc = cw // tn
    half = N_CHIPS // 2

    def ep(parts, e_refs, o_refs, cols):
        gate, up = parts[0].astype(BF16), parts[1].astype(BF16)
        o_refs[0][0, :, cols] = gate
        o_refs[0][1, :, cols] = up
        g32 = gate.astype(F32)
        o_refs[1][:, cols] = (g32 * _sigmoid(g32) * up.astype(F32)).astype(BF16)

    return _result(_matmul(
        "gate_up", hn, [w_gu_g, w_gu_g], dims=NN, grid=(S // tm, Fd // tn, 1),
        lhs_spec=pl.BlockSpec((tm, D), lambda i, j, k: (i, 0)),
        rhs_specs=[pl.BlockSpec((None, D, tn), lambda i, j, k: (j // nbc, 0, j % nbc)),
                   pl.BlockSpec((None, D, tn), lambda i, j, k: (half + j // nbc, 0, j % nbc))],
        acc_shape=(tm, tn),
        out_shape=[jax.ShapeDtypeStruct((2, S, Fd), BF16), jax.ShapeDtypeStruct((S, Fd), BF16)],
        out_specs=[pl.BlockSpec((2, tm, tn), lambda i, j, k: (0, i, j)), pl.BlockSpec((tm, tn), lambda i, j, k: (i, j))],
        epilogue=ep, carry=carry, cols_outer=True), 2, carry)


def _down_bwd(dh_b, w_down_g, gu, dm, carry=None):
    S, D = dh_b.shape
    Fd = dm["F"]
    tm = _pick(S, (1024, 512, 256, 128))
    tn = _pick(Fd, (512, 256, 128))

    def ep(parts, e_refs, o_refs, cols):
        gate = e_refs[0][0, :, cols].astype(F32)
        up = e_refs[0][1, :, cols].astype(F32)
        s = _sigmoid(gate)
        dact = parts[0]
        o_refs[0][0, :, cols] = (dact * up * s * (1.0 + gate * (1.0 - s))).astype(BF16)
        o_refs[0][1, :, cols] = (dact * gate * s).astype(BF16)

    blk = pl.BlockSpec((2, tm, tn), lambda i, j, k: (0, i, j))
    return _result(_matmul(
        "down_bwd", dh_b, [w_down_g], dims=NT, grid=(S // tm, Fd // tn, 1),
        lhs_spec=pl.BlockSpec((tm, D), lambda i, j, k: (i, 0)),
        rhs_specs=[pl.BlockSpec((tn, D), lambda i, j, k: (j, 0))],
        acc_shape=(tm, tn), extra=[gu], extra_specs=[blk],
        out_shape=[jax.ShapeDtypeStruct((2, S, Fd), BF16)], out_specs=[blk], epilogue=ep, carry=carry), 1, carry)


def _gate_up_bwd(dgu, w_gu_g, dm, carry=None):
    S = dgu.shape[1]
    D, Fd = dm["D"], dm["F"]
    cw = 2 * Fd // N_CHIPS
    half = N_CHIPS // 2
    tm = _pick(S, (1024, 512, 256, 128))
    tn = _pick(D, (1024, 512, 256, 128))
    return _result(_matmul(
        "gate_up_bwd", dgu, [w_gu_g], dims=NT, grid=(S // tm, D // tn, N_CHIPS),
        lhs_spec=pl.BlockSpec((None, tm, cw), lambda i, j, k: (k // half, i, k % half)),
        rhs_specs=[pl.BlockSpec((None, tn, cw), lambda i, j, k: (k, j, 0))],
        acc_shape=(tm, tn), out_shape=[jax.ShapeDtypeStruct((S, D), F32)],
        out_specs=[pl.BlockSpec((tm, tn), lambda i, j, k: (i, j))], epilogue=_store_epilogue(F32), carry=carry), 1, carry)


def _out_bwd(dh_b, w_out_g, proj, a_attn, a_sgu, dm, carry=None):
    S, D = dh_b.shape
    OFF_G = dm["OFF_G"]
    tm = _pick(S, (1024, 512, 256, 128))
    tn = D // N_CHIPS

    def ep(parts, e_refs, o_refs, cols):
        dm_ = parts[0]
        ga = _sigmoid(e_refs[0][:, cols].astype(F32))
        gs = _sigmoid(e_refs[1][:, cols].astype(F32))
        o_refs[0][:, cols] = (dm_ * ga).astype(BF16)
        o_refs[1][:, cols] = (dm_ * gs).astype(BF16)
        o_refs[2][0, :, cols] = (dm_ * e_refs[2][:, cols].astype(F32) * ga * (1.0 - ga)).astype(BF16)
        o_refs[2][1, :, cols] = (dm_ * e_refs[3][:, cols].astype(F32) * gs * (1.0 - gs)).astype(BF16)

    blk = pl.BlockSpec((tm, tn), lambda i, j, k: (i, j))
    return _result(_matmul(
        "out_bwd", dh_b, [w_out_g], dims=NT, grid=(S // tm, D // tn, 1),
        lhs_spec=pl.BlockSpec((tm, D), lambda i, j, k: (i, 0)),
        rhs_specs=[pl.BlockSpec((tn, D), lambda i, j, k: (j, 0))],
        acc_shape=(tm, tn), extra=[proj, proj, a_attn, a_sgu],
        extra_specs=[pl.BlockSpec((tm, tn), lambda i, j, k: (i, OFF_G // tn + j)),
                     pl.BlockSpec((tm, tn), lambda i, j, k: (i, (OFF_G + D) // tn + j)), blk, blk],
        out_shape=[jax.ShapeDtypeStruct((S, D), BF16), jax.ShapeDtypeStruct((S, D), BF16),
                   jax.ShapeDtypeStruct((2, S, D), BF16)],
        out_specs=[blk, blk, pl.BlockSpec((2, tm, tn), lambda i, j, k: (0, i, j))], epilogue=ep, carry=carry), 3, carry)


def _colsharded_bwd(name, dy, w_g, out_dtype, carry=None):
    S = dy.shape[0]
    _, K, cw = w_g.shape
    tm = _pick(S, (1024, 512, 256, 128))
    tn = _pick(K, (1024, 512, 256, 128))
    return _result(_matmul(
        name, dy, [w_g], dims=NT, grid=(S // tm, K // tn, N_CHIPS),
        lhs_spec=pl.BlockSpec((tm, cw), lambda i, j, k: (i, k)),
        rhs_specs=[pl.BlockSpec((None, tn, cw), lambda i, j, k: (k, j, 0))],
        acc_shape=(tm, tn), out_shape=[jax.ShapeDtypeStruct((S, K), out_dtype)],
        out_specs=[pl.BlockSpec((tm, tn), lambda i, j, k: (i, j))], epilogue=_store_epilogue(out_dtype),
        carry=carry), 1, carry)


def _wgrad_cols(name, x, dy, carry=None, colsum=False):
    S, R = x.shape
    C = dy.shape[1]
    cw = C // N_CHIPS
    tm = _pick(R, (1024, 512, 256, 128))
    tk = _pick(S, (1024, 512, 256, 128) if cw >= 1024 else (2048, 1024, 512, 256, 128))

    def ep(parts, e_refs, o_refs, cols):
        for o, p in zip(o_refs, parts):
            o[:, cols] = p

    out_shape = [jax.ShapeDtypeStruct((N_CHIPS, R, cw), F32)]
    out_specs = [pl.BlockSpec((None, tm, cw), lambda i, j, k: (j, i, 0))]
    if colsum:
        out_shape.append(jax.ShapeDtypeStruct((R // tm, N_CHIPS, 8, cw), F32))
        out_specs.append(pl.BlockSpec((None, None, 8, cw), lambda i, j, k: (i, j, 0, 0)))
    return _result(_matmul(
        name, x, [dy], dims=TN, grid=(R // tm, N_CHIPS, S // tk),
        lhs_spec=pl.BlockSpec((tk, tm), lambda i, j, k: (k, i)),
        rhs_specs=[pl.BlockSpec((tk, cw), lambda i, j, k: (k, j))],
        acc_shape=(tm, cw), out_shape=out_shape, out_specs=out_specs, epilogue=ep,
        carry=carry, rhs_colsum=colsum), len(out_shape), carry)


def _wgrad_gate_up(hn, dgu, dm, carry=None):
    S, D = hn.shape
    Fd = dm["F"]
    cw = 2 * Fd // N_CHIPS
    half = N_CHIPS // 2
    tm = _pick(D, (1024, 512, 256, 128))
    tk = _pick(S, (1024, 512, 256, 128))
    tn = _pick(cw, (1408, 512, 384, 256, 128))
    nbc = cw // tn
    return _result(_matmul(
        "wgrad_gate_up", hn, [dgu], dims=TN, grid=(D // tm, 2 * Fd // tn, S // tk),
        lhs_spec=pl.BlockSpec((tk, tm), lambda i, j, k: (k, i)),
        rhs_specs=[pl.BlockSpec((None, tk, tn), lambda i, j, k: (j // (half * nbc), k, j % (half * nbc)))],
        acc_shape=(tm, tn), out_shape=[jax.ShapeDtypeStruct((N_CHIPS, D, cw), F32)],
        out_specs=[pl.BlockSpec((None, tm, tn), lambda i, j, k: (j // nbc, i, j % nbc))], epilogue=_store_epilogue(F32),
        carry=carry), 1, carry)


def _wgrad_rows(name, x, dy):
    S, R = x.shape
    C = dy.shape[1]
    rw = R // N_CHIPS
    tn = _pick(C, (1024, 512, 256, 128))
    tk = _pick(S, (1024, 512, 256, 128))
    return _matmul(
        name, x, [dy], dims=TN, grid=(N_CHIPS, C // tn, S // tk),
        lhs_spec=pl.BlockSpec((tk, rw), lambda i, j, k: (k, i)),
        rhs_specs=[pl.BlockSpec((tk, tn), lambda i, j, k: (k, j))],
        acc_shape=(rw, tn), out_shape=[jax.ShapeDtypeStruct((N_CHIPS, rw, C), F32)],
        out_specs=[pl.BlockSpec((None, rw, tn), lambda i, j, k: (i, 0, j))], epilogue=_store_epilogue(F32))[0]


def _place():
    x, y, c = lax.axis_index("x"), lax.axis_index("y"), lax.axis_index("c")
    others = [(1 - x, y), (x, 1 - y), (1 - x, 1 - y)]
    return x, y, c, others


def _chip_index(chip):
    return 2 * chip[0] + chip[1]


def _gather_weights(bufs):
    n = len(bufs)

    def copies(src, out, send_sems, recv_sems):
        x, y, c, others = _place()

        def half(ref, chip_idx, hc):
            r2 = ref.shape[1] // 2
            return ref.at[chip_idx, pl.ds(hc * r2, r2), :]

        def copy(t, k, chip, hc, to):
            return pltpu.make_async_remote_copy(
                src_ref=half(src[t], _chip_index(chip), hc), dst_ref=half(out[t], _chip_index(chip), hc),
                send_sem=send_sems.at[6 * t + k], recv_sem=recv_sems.at[6 * t + k],
                device_id=to, device_id_type=MESH)

        me, sibling = (x, y, c), (x, y, 1 - c)
        pairs = [(t, j, chip) for t in range(n) for j, chip in enumerate(others)]
        sent = [copy(t, j, (x, y), c, (*chip, c)) for t, j, chip in pairs]
        landed = [copy(t, j, chip, c, me) for t, j, chip in pairs]
        passed = [copy(t, 3 + j, chip, c, sibling) for t, j, chip in pairs]
        handed = [copy(t, 3 + j, chip, 1 - c, me) for t, j, chip in pairs]
        return sent, landed, passed, handed

    def start(src, out, send_sems, recv_sems):
        for cp in copies(src, out, send_sems, recv_sems)[0]:
            cp.start()

    def finish(src, out, send_sems, recv_sems):
        sent, landed, passed, handed = copies(src, out, send_sems, recv_sems)
        for arrival, forward in zip(landed, passed):
            arrival.wait_recv()
            forward.start()
        for cp in handed:
            cp.wait_recv()
        for cp in sent + passed:
            cp.wait_send()

    return _Comm("gather_weights", bufs, [jax.ShapeDtypeStruct(b.shape, BF16) for b in bufs],
                 {t: t for t in range(n)}, 6 * n, start, finish)


def _sibling_exchange(grads):
    n = len(grads)
    shapes = [g.shape for g in grads]

    def copies(src, land, send_sems, recv_sems):
        x, y, c, _ = _place()
        res = []
        for t in range(n):
            r2 = shapes[t][1] // 2
            res.append(pltpu.make_async_remote_copy(
                src_ref=src[t].at[:, pl.ds((1 - c) * r2, r2), :], dst_ref=land[t],
                send_sem=send_sems.at[t], recv_sem=recv_sems.at[t], device_id=(x, y, 1 - c), device_id_type=MESH))
        return res

    def start(*refs):
        for cp in copies(*refs):
            cp.start()

    def finish(*refs):
        remote = copies(*refs)
        for cp in remote:
            cp.wait_recv()
        for cp in remote:
            cp.wait_send()

    return _Comm("sibling_exchange", grads, [jax.ShapeDtypeStruct((s[0], s[1] // 2, s[2]), F32) for s in shapes],
                 {}, n, start, finish)


def _chip_exchange(sends):
    n = len(sends)
    shapes = [s.shape for s in sends]

    def copies(snd, got, send_sems, recv_sems):
        x, y, c, others = _place()
        return [pltpu.make_async_remote_copy(
            src_ref=snd[t].at[_chip_index(chip)], dst_ref=got[t].at[j],
            send_sem=send_sems.at[3 * t + j], recv_sem=recv_sems.at[3 * t + j],
            device_id=(*chip, c), device_id_type=MESH) for t in range(n) for j, chip in enumerate(others)]

    def start(*refs):
        for cp in copies(*refs):
            cp.start()

    def finish(*refs):
        remote = copies(*refs)
        for cp in remote:
            cp.wait_recv()
        for cp in remote:
            cp.wait_send()

    return _Comm("chip_exchange", sends, [jax.ShapeDtypeStruct((3, s[1], s[2]), BF16) for s in shapes],
                 {}, 3 * n, start, finish)


def _sibling_share(fulls):
    n = len(fulls)
    shapes = [f.shape for f in fulls]

    def copies(src, out, send_sems, recv_sems, mine):
        x, y, c, _ = _place()
        hc = c if mine else 1 - c
        res = []
        for t in range(n):
            r2 = shapes[t][0] // 2
            res.append(pltpu.make_async_remote_copy(
                src_ref=src[t].at[pl.ds(hc * r2, r2), :], dst_ref=out[t].at[pl.ds(hc * r2, r2), :],
                send_sem=send_sems.at[t], recv_sem=recv_sems.at[t], device_id=(x, y, 1 - c), device_id_type=MESH))
        return res

    def start(*refs):
        for cp in copies(*refs, mine=True):
            cp.start()

    def finish(*refs):
        for cp in copies(*refs, mine=False):
            cp.wait_recv()
        for cp in copies(*refs, mine=True):
            cp.wait_send()

    return _Comm("sibling_share", fulls, [jax.ShapeDtypeStruct(s, F32) for s in shapes],
                 {t: t for t in range(n)}, n, start, finish)


def _gather_all(v):
    R, C = v.shape

    def body(v_ref, out_ref, send_sems, recv_sems, local_sem):
        x, y, c, others = _place()
        me, sibling = (x, y, c), (x, y, 1 - c)

        def rows(px, py, pc):
            return out_ref.at[4 * px + 2 * py + pc]

        def copy(k, block, to, src=None):
            return pltpu.make_async_remote_copy(
                src_ref=rows(*block) if src is None else src, dst_ref=rows(*block),
                send_sem=send_sems.at[k], recv_sem=recv_sems.at[k], device_id=to, device_id_type=MESH)

        mine = pltpu.make_async_copy(v_ref, rows(*me), local_sem)
        mine.start()
        first = [copy(0, me, sibling, src=v_ref)]
        first += [copy(1 + j, me, (*chip, c), src=v_ref) for j, chip in enumerate(others)]
        for cp in first:
            cp.start()
        passed = [copy(4 + j, (*chip, c), sibling) for j, chip in enumerate(others)]
        for j, chip in enumerate(others):
            copy(1 + j, (*chip, c), me).wait_recv()
            passed[j].start()
        copy(0, sibling, me).wait_recv()
        for j, chip in enumerate(others):
            copy(4 + j, (*chip, 1 - c), me).wait_recv()
        for cp in first + passed:
            cp.wait_send()
        mine.wait()

    return pl.pallas_call(
        body, name="gather_all", in_specs=[ANY], out_specs=ANY,
        out_shape=jax.ShapeDtypeStruct((8, R, C), F32),
        scratch_shapes=[pltpu.SemaphoreType.DMA((7,)), pltpu.SemaphoreType.DMA((7,)), pltpu.SemaphoreType.DMA],
    )(v)


def _my_chip():
    return 2 * lax.axis_index("x") + lax.axis_index("y")


def _my_core():
    return lax.axis_index("c")


def _pair_sum(grad, land):
    K, R2, C = land.shape
    tm = _row_tile(R2, C)
    nrb = R2 // tm

    def body(a_ref, b_ref, sb_ref):
        sb_ref[...] = (a_ref[...] + b_ref[...]).astype(BF16)

    blk = pl.BlockSpec((None, tm, C), lambda k, r: (k, r, 0))
    return pl.pallas_call(
        body, name="pair_sum", grid=(K, nrb),
        in_specs=[pl.BlockSpec((None, tm, C), lambda k, r: (k, _my_core() * nrb + r, 0)), blk],
        out_specs=blk, out_shape=jax.ShapeDtypeStruct((K, R2, C), BF16),
        compiler_params=_params(("parallel", "parallel")),
    )(grad, land)


def _chip_sum(grad, land, got):
    _, R2, C = land.shape
    tm = _row_tile(R2, C)
    nrb = R2 // tm

    def body(a_ref, b_ref, g_ref, s_ref):
        own = a_ref[...] + b_ref[...]
        s_ref[...] = ((own + g_ref[0].astype(F32)) + g_ref[1].astype(F32)) + g_ref[2].astype(F32)

    return pl.pallas_call(
        body, name="chip_sum", grid=(nrb,),
        in_specs=[pl.BlockSpec((None, tm, C), lambda r: (_my_chip(), _my_core() * nrb + r, 0)),
                  pl.BlockSpec((None, tm, C), lambda r: (_my_chip(), r, 0)),
                  pl.BlockSpec((3, tm, C), lambda r: (0, r, 0))],
        out_specs=pl.BlockSpec((tm, C), lambda r: (_my_core() * nrb + r, 0)),
        out_shape=jax.ShapeDtypeStruct((2 * R2, C), F32),
        compiler_params=_params(("parallel",)),
    )(grad, land, got)


def _adamw_math(w, g, m, v):
    m = ADAM_B1 * m + (1.0 - ADAM_B1) * g
    v = ADAM_B2 * v + (1.0 - ADAM_B2) * (g * g)
    m_hat = m / (1.0 - ADAM_B1 ** ADAM_STEP)
    v_hat = v / (1.0 - ADAM_B2 ** ADAM_STEP)
    delta = -ADAM_LR * (m_hat / (jnp.sqrt(v_hat) + ADAM_EPS) + ADAM_WD * w)
    return delta, m, v


def _adamw_stacked(grads, w, m, v, carry=None):
    L, R, C = w.shape
    tm = _row_tile(R, C)
    nrb = R // tm
    nci = len(carry.ins) if carry else 0
    nco = len(carry.outs) if carry else 0

    def body(*refs):
        g_refs = refs[:L]
        w_ref, m_ref, v_ref = refs[L:L + 3]
        ci_refs = refs[L + 3:L + 3 + nci]
        go_ref, d_ref, mo_ref, vo_ref = refs[L + 3 + nci:L + 7 + nci]
        co_refs = refs[L + 7 + nci:L + 7 + nci + nco]
        sems = refs[L + 7 + nci + nco:]
        l, r = pl.program_id(0), pl.program_id(1)
        if carry:
            @pl.when((l == 0) & (r == 0))
            def _():
                carry.start(ci_refs, co_refs, *sems)

        for ll in range(L):
            @pl.when(l == ll)
            def _(ll=ll):
                g = g_refs[ll][...]
                delta, mn, vn = _adamw_math(w_ref[...], g, m_ref[...], v_ref[...])
                go_ref[...] = g
                d_ref[...] = delta
                mo_ref[...] = mn
                vo_ref[...] = vn

        if carry:
            @pl.when((l == L - 1) & (r == nrb - 1))
            def _():
                carry.finish(ci_refs, co_refs, *sems)

    stacked = pl.BlockSpec((None, tm, C), lambda l, r: (l, r, 0))
    g_specs = [pl.BlockSpec((tm, C), lambda l, r, ll=ll: (jnp.where(l == ll, r, 0), 0)) for ll in range(L)]
    shp = jax.ShapeDtypeStruct((L, R, C), F32)
    outs = pl.pallas_call(
        body, name="adamw", grid=(L, nrb),
        in_specs=[*g_specs, stacked, stacked, stacked, *([ANY] * nci)],
        out_specs=[*([stacked] * 4), *([ANY] * nco)], out_shape=[*([shp] * 4), *(carry.outs if carry else [])],
        scratch_shapes=carry.sem_scratch() if carry else [],
        input_output_aliases={L + 3 + i: 4 + o for i, o in carry.aliases.items()} if carry else {},
        compiler_params=_params(("arbitrary", "arbitrary")),
    )(*grads, w, m, v, *(carry.ins if carry else []))
    return (outs[:4], list(outs[4:])) if carry else outs


def _adamw_small(parts, w, m, v):
    _, R, C = parts.shape
    tm = _row_tile(R, 8 * C)

    def body(p_ref, w_ref, m_ref, v_ref, go_ref, d_ref, mo_ref, vo_ref):
        g = p_ref[0]
        for k in range(1, 8):
            g = g + p_ref[k]
        delta, mn, vn = _adamw_math(w_ref[...], g, m_ref[...], v_ref[...])
        go_ref[...] = g
        d_ref[...] = delta
        mo_ref[...] = mn
        vo_ref[...] = vn

    blk = pl.BlockSpec((tm, C), lambda i: (i, 0))
    shp = jax.ShapeDtypeStruct((R, C), F32)
    return pl.pallas_call(
        body, name="adamw_small", grid=(R // tm,),
        in_specs=[pl.BlockSpec((8, tm, C), lambda i: (0, i, 0)), blk, blk, blk],
        out_specs=[blk] * 4, out_shape=[shp] * 4,
        compiler_params=_params(("parallel",)),
    )(parts, w, m, v)


def _cast_place(w, layer):
    _, R, C = w.shape
    tm = _row_tile(R, C)

    def body(w_ref, o_ref):
        o_ref[...] = w_ref[...].astype(BF16)

    return pl.pallas_call(
        body, name="cast_place", grid=(R // tm,),
        in_specs=[pl.BlockSpec((None, tm, C), lambda r: (layer, r, 0))],
        out_specs=pl.BlockSpec((None, tm, C), lambda r: (_my_chip(), r, 0)),
        out_shape=jax.ShapeDtypeStruct((N_CHIPS, R, C), BF16),
        compiler_params=_params(("parallel",)),
    )(w)


def _trig_tables(positions):
    inv_freq = ROPE_THETA ** (-jnp.arange(0, ROPE_DIM, 2, dtype=F32) / ROPE_DIM)
    ang = positions.astype(F32)[:, None] * inv_freq
    cos, sin = jnp.cos(ang), jnp.sin(ang)
    S = positions.shape[0]
    cos_h = jnp.concatenate([cos, cos, jnp.ones((S, HEAD_DIM - ROPE_DIM), F32)], axis=1)
    sin_h = jnp.concatenate([-sin, sin, jnp.zeros((S, HEAD_DIM - ROPE_DIM), F32)], axis=1)
    rep = LANES // HEAD_DIM
    return [jnp.tile(t, (1, rep)) for t in (cos_h, sin_h)]


def _row(vec):
    return vec.reshape(1, -1)


def _lane_row(vec):
    return jnp.zeros((8, LANES), F32).at[0, :vec.shape[0]].set(vec)


def _pack(pieces, rows):
    flat = jnp.concatenate([p.reshape(-1).astype(F32) for p in pieces])
    return jnp.pad(flat, (0, rows * LANES - flat.shape[0])).reshape(rows, LANES)


def kernel(x, positions, norm1_g, w_in, b_in, sinks, sgu_ln_g, sgu_ln_b, sgu_w, sgu_b, w_attn_branch, w_sgu_branch, w_out, norm2_g, w_gate_up, w_down, final_g, loss_target, m_norm1_g, m_w_in, m_b_in, m_sinks, m_sgu_ln_g, m_sgu_ln_b, m_sgu_w, m_sgu_b, m_w_attn_branch, m_w_sgu_branch, m_w_out, m_norm2_g, m_w_gate_up, m_w_down, m_final_g, v_norm1_g, v_w_in, v_b_in, v_sinks, v_sgu_ln_g, v_sgu_ln_b, v_sgu_w, v_sgu_b, v_w_attn_branch, v_w_sgu_branch, v_w_out, v_norm2_g, v_w_gate_up, v_w_down, v_final_g):
    L = norm1_g.shape[0]
    S, D = x.shape[1], x.shape[2]
    NQ = sinks.shape[1]
    A = NQ * HEAD_DIM
    KV = N_KV_HEADS * HEAD_DIM
    G = sgu_ln_g.shape[1]
    NG = sgu_w.shape[1]
    IN = b_in.shape[1]
    Fd = w_down.shape[1] * N_CHIPS
    dm = dict(D=D, A=A, KV=KV, NQ=NQ, G=G, NG=NG, IN=IN, F=Fd,
              OFF_K=A, OFF_V=A + KV, OFF_Z=A + 2 * KV, OFF_G=A + 2 * KV + 2 * G)
    assert sgu_w.shape[2] == WINDOW and G == NG * LANES and IN == dm["OFF_G"] + 2 * D

    h = x[0]
    target = loss_target[0]
    trig = _trig_tables(positions[0])
    tril = jnp.tril(jnp.ones((WINDOW, WINDOW), bool))

    big = [w_in, w_attn_branch, w_sgu_branch, w_out, w_gate_up, w_down]
    big_m = [m_w_in, m_w_attn_branch, m_w_sgu_branch, m_w_out, m_w_gate_up, m_w_down]
    big_v = [v_w_in, v_w_attn_branch, v_w_sgu_branch, v_w_out, v_w_gate_up, v_w_down]

    placed = [[_cast_place(w, l) for w in big] for l in range(L)]
    IN_, AB, SB, OUT, GU, DOWN = range(len(big))
    gathered = [[None] * len(big) for _ in range(L)]
    gathered[0][IN_] = _gather_weights([placed[0][IN_]]).run()[0]

    def fetch(layer, idx):
        return _gather_weights([placed[layer][t] for t in idx]) if layer < L else None

    def fetched(layer, idx, res):
        if layer >= L:
            return res
        main, got = res
        for t, g in zip(idx, got):
            gathered[layer][t] = g
        return main

    def weights(l):
        flat = lambda w, rows: None if w is None else w.reshape(rows, D)
        w_in_g, w_ab_g, w_sb_g, w_out_g, w_gu_g, w_down_g = gathered[l]
        return (w_in_g, w_ab_g, w_sb_g, flat(w_out_g, D), w_gu_g, flat(w_down_g, Fd))

    def small(l):
        return dict(
            g1=_row(norm1_g[l]), b_in=_row(b_in[l]), sink=_lane_row(sinks[l]),
            ln_g=_row(sgu_ln_g[l]), ln_b=_row(sgu_ln_b[l]),
            w_tril=jnp.where(tril[None], sgu_w[l], 0.0).astype(BF16),
            b_t=jnp.zeros((WINDOW, LANES), F32).at[:, :NG].set(sgu_b[l].T),
            g2=_row(norm2_g[l]))

    saved = []
    for l in range(L):
        sp = small(l)
        xn = _rms_fwd(h, sp["g1"])
        now = [AB, SB, OUT, GU] if l == 0 else [DOWN]
        proj = fetched(l, now, _in_proj(xn, gathered[l][IN_], sp["b_in"], trig, dm, carry=fetch(l, now)))
        w_in_g, w_ab_g, w_sb_g, w_out_g = weights(l)[:4]
        y_attn, lse = _attn_fwd(proj, sp["sink"], dm)
        y_sgu = _sgu_fwd(proj, sp["w_tril"], sp["b_t"], sp["ln_g"], sp["ln_b"], dm)
        a_attn = _branch_attn(y_attn, w_ab_g, dm)
        a_sgu, merged = _branch_sgu_merge(y_sgu, w_sb_g, a_attn, proj, dm)
        if l == 0:
            h_mid = fetched(l, [DOWN], _residual_matmul("out_proj", merged, w_out_g, h, carry=fetch(l, [DOWN])))
        else:
            h_mid = _residual_matmul("out_proj", merged, w_out_g, h)
        w_gu_g, w_down_g = weights(l)[4:]
        hn = _rms_fwd(h_mid, sp["g2"])
        ahead = [IN_, AB, SB, OUT]
        gu, act = fetched(l + 1, ahead, _gate_up(hn, w_gu_g, dm, carry=fetch(l + 1, ahead)))
        h_out = fetched(l + 1, [GU], _residual_matmul("down_proj", act, w_down_g, h_mid, carry=fetch(l + 1, [GU])))
        saved.append(dict(h=h, xn=xn, proj=proj, y_attn=y_attn, lse=lse, y_sgu=y_sgu, a_attn=a_attn, a_sgu=a_sgu,
                          merged=merged, h_mid=h_mid, hn=hn, gu=gu, act=act))
        h = h_out

    dh, dh_b, d_final, loss_part = _loss_head(h, _row(final_g), target)

    small_grads = [None] * L
    reduced = [[None] * len(big) for _ in range(L)]
    early, mid, late = [GU, DOWN], [AB, SB, OUT], [IN_]

    def riding(has_carry, res):
        return res if has_carry else (res, None)

    def sends_of(grads, land):
        return [_pair_sum(g, d) for g, d in zip(grads, land)]

    def finished(grads, land, got):
        return [_chip_sum(g, d, p) for g, d, p in zip(grads, land, got)]

    def file_reduced(layer, idx, fulls):
        for t, f in zip(idx, fulls):
            reduced[layer][t] = f

    late_grads = None
    mid_fulls = None
    n_late, n_mid, n_early = len(late), len(mid), len(early)
    for l in reversed(range(L)):
        w_in_g, w_ab_g, w_sb_g, w_out_g, w_gu_g, w_down_g = weights(l)
        sp, sv = small(l), saved[l]
        have = late_grads is not None
        dgu, rode = riding(have, _down_bwd(
            dh_b, w_down_g, sv["gu"], dm,
            carry=_sibling_exchange(late_grads).beside(_sibling_share(mid_fulls)) if have else None))
        if have:
            land = rode[:n_late]
            file_reduced(l + 1, mid, rode[n_late:])
        g_down = _wgrad_rows("wgrad_down", sv["act"], dh_b)
        dhn, got = riding(have, _gate_up_bwd(dgu, w_gu_g, dm,
                                             carry=_chip_exchange(sends_of(late_grads, land)) if have else None))
        g_gu, shared = riding(have, _wgrad_gate_up(sv["hn"], dgu, dm,
                                                   carry=_sibling_share(finished(late_grads, land, got)) if have else None))
        if have:
            file_reduced(l + 1, late, shared)
        dh_mid, dh_mid_b, d_g2 = _rms_bwd(dhn, sv["h_mid"], sp["g2"], dh)
        early_grads = [g_gu, g_down]
        (da_attn, da_sgu, dgate), land_e = _out_bwd(dh_mid_b, w_out_g, sv["proj"], sv["a_attn"], sv["a_sgu"], dm,
                                                     carry=_sibling_exchange(early_grads))
        sends_e = sends_of(early_grads, land_e)
        g_out = _wgrad_rows("wgrad_out", sv["merged"], dh_mid_b)
        dy_attn = _colsharded_bwd("branch_attn_bwd", da_attn, w_ab_g, BF16)
        dy_sgu = _colsharded_bwd("branch_sgu_bwd", da_sgu, w_sb_g, BF16)
        g_ab = _wgrad_cols("wgrad_attn_branch", sv["y_attn"], da_attn)
        g_sb = _wgrad_cols("wgrad_sgu_branch", sv["y_sgu"], da_sgu)
        mid_grads = [g_ab, g_sb, g_out]
        dq, dk, dv, d_sink = _attn_bwd(sv["proj"], trig, sp["sink"], sv["y_attn"], sv["lse"], dy_attn, dm)
        dz, d_sgu_w, d_bt, d_lng, d_lnb = _sgu_bwd(sv["proj"], sp["w_tril"], sp["b_t"], sp["ln_g"], sp["ln_b"], dy_sgu, dm)
        dproj = jnp.concatenate([dq, dk, dv, dz, dgate[0], dgate[1]], axis=1)
        dxn, rode = _colsharded_bwd("in_proj_bwd", dproj, w_in_g, F32,
                                    carry=_chip_exchange(sends_e).beside(_sibling_exchange(mid_grads)))
        got_e, land_m = rode[:n_early], rode[n_early:]
        (g_in, d_bin), rode = _wgrad_cols(
            "wgrad_in", sv["xn"], dproj, colsum=True,
            carry=_sibling_share(finished(early_grads, land_e, got_e)).beside(_chip_exchange(sends_of(mid_grads, land_m))))
        file_reduced(l, early, rode[:n_early])
        mid_fulls = finished(mid_grads, land_m, rode[n_early:])
        dh, dh_b, d_g1 = _rms_bwd(dxn, sv["h"], sp["g1"], dh_mid)
        late_grads = [g_in]
        small_grads[l] = dict(norm1_g=d_g1[0], b_in=d_bin[0, :, 0, :].reshape(-1), sinks=d_sink[0, :NQ],
                              sgu_ln_g=d_lng[0], sgu_ln_b=d_lnb[0], sgu_w=d_sgu_w, sgu_b=d_bt[:, :NG].T, norm2_g=d_g2[0])
    grad_x = dh[None]

    land = _sibling_exchange(late_grads).run()
    got = _chip_exchange(sends_of(late_grads, land)).run()
    shared = _sibling_share(finished(late_grads, land, got) + mid_fulls).run()
    file_reduced(0, late, shared[:n_late])
    file_reduced(0, mid, shared[n_late:])
    big_out = [_adamw_stacked([reduced[l][t] for l in range(L)], big[t], big_m[t], big_v[t]) for t in range(len(big))]

    names = ["norm1_g", "b_in", "sinks", "sgu_ln_g", "sgu_ln_b", "sgu_w", "sgu_b", "norm2_g"]
    small_w = [norm1_g, b_in, sinks, sgu_ln_g, sgu_ln_b, sgu_w, sgu_b, norm2_g, final_g]
    small_m = [m_norm1_g, m_b_in, m_sinks, m_sgu_ln_g, m_sgu_ln_b, m_sgu_w, m_sgu_b, m_norm2_g, m_final_g]
    small_v = [v_norm1_g, v_b_in, v_sinks, v_sgu_ln_g, v_sgu_ln_b, v_sgu_w, v_sgu_b, v_norm2_g, v_final_g]
    small_g = [jnp.stack([small_grads[l][nm] for l in range(L)]) for nm in names] + [d_final[0]]
    sizes = [w.size for w in small_w]
    total = sum(sizes) + 1
    rows = -(-total // (512 * LANES)) * 512
    loss_piece = jnp.sum(loss_part[0]).reshape(1)
    packed_g = _pack(small_g + [loss_piece], rows)
    one = jnp.ones((1,), F32)
    parts = _gather_all(packed_g)
    outs = _adamw_small(parts, _pack(small_w + [one], rows), _pack(small_m + [one], rows), _pack(small_v + [one], rows))

    def unpack(p):
        flat = p.reshape(-1)
        res, off = [], 0
        for w, n in zip(small_w, sizes):
            res.append(flat[off:off + n].reshape(w.shape))
            off += n
        return res, flat[off]

    (sg, loss), (sd, _), (smm, _), (svv, _) = [unpack(o) for o in outs]

    order = ["norm1_g", "w_in", "b_in", "sinks", "sgu_ln_g", "sgu_ln_b", "sgu_w", "sgu_b", "w_attn_branch",
             "w_sgu_branch", "w_out", "norm2_g", "w_gate_up", "w_down", "final_g"]
    big_names = ["w_in", "w_attn_branch", "w_sgu_branch", "w_out", "w_gate_up", "w_down"]
    small_names = names + ["final_g"]

    def collect(kind):
        res = []
        for nm in order:
            if nm in big_names:
                res.append(big_out[big_names.index(nm)][kind])
            else:
                res.append((sg, sd, smm, svv)[kind][small_names.index(nm)])
        return res

    return (loss, grad_x, *collect(0), *collect(1), *collect(2), *collect(3))
```

```python
import math

import jax
import jax.numpy as jnp
from jax import lax
from jax.experimental import pallas as pl
from jax.experimental.pallas import tpu as pltpu

F32 = jnp.float32
BF16 = jnp.bfloat16
MESH = pl.DeviceIdType.MESH
ANY = pl.BlockSpec(memory_space=pl.ANY)

HEAD_DIM = 64
N_KV_HEADS = 4
WINDOW = 128
ROPE_DIM = HEAD_DIM // 4
ROPE_THETA = 500000.0
EPS = 1e-5
NEG = -1e30
N_CHIPS = 4
LANES = 128
V7X_VMEM_LIMIT = 56 * 1024 * 1024

ADAM_LR = 0.001
ADAM_B1 = 0.9
ADAM_B2 = 0.999
ADAM_EPS = 1e-08
ADAM_WD = 0.01
ADAM_STEP = 10

NN = (((1,), (0,)), ((), ()))
NT = (((1,), (1,)), ((), ()))
TN = (((0,), (0,)), ((), ()))


ROW_TILES = (1024, 512, 256, 128, 64, 32, 16, 8)
BLOCK_BYTES = 2 * 1024 * 1024


def _pick(n, prefs):
    for p in prefs:
        if n % p == 0:
            return p
    raise ValueError(f"no tile for {n} among {prefs}")


def _row_tile(rows, cols, itemsize=4):
    return _pick(rows, [t for t in ROW_TILES if t * cols * itemsize <= BLOCK_BYTES or t == ROW_TILES[-1]])


def _dot(a, b, dims):
    return lax.dot_general(a, b, dims, preferred_element_type=F32)


def _sigmoid(x):
    return 0.5 * jnp.tanh(0.5 * x) + 0.5


def _gelu(x):
    return 0.5 * x * (1.0 + lax.erf(x * (1.0 / math.sqrt(2.0))))


def _gelu_grad(x):
    return 0.5 * (1.0 + lax.erf(x * (1.0 / math.sqrt(2.0)))) + x * jnp.exp(-0.5 * x * x) * (1.0 / math.sqrt(2.0 * math.pi))


def _params(sem):
    return pltpu.CompilerParams(dimension_semantics=sem, vmem_limit_bytes=V7X_VMEM_LIMIT)


def _matmul(name, lhs, rhs_list, *, dims, grid, lhs_spec, rhs_specs, acc_shape, out_shape, out_specs,
            epilogue, extra=(), extra_specs=(), carry=None, rhs_colsum=False, cols_outer=False, extra_aliases=None):
    if cols_outer:
        swap = lambda s: s if s.index_map is None else pl.BlockSpec(s.block_shape, lambda j, i, k, f=s.index_map: f(i, j, k))
        grid = (grid[1], grid[0], grid[2])
        lhs_spec, rhs_specs = swap(lhs_spec), [swap(s) for s in rhs_specs]
        extra_specs, out_specs = [swap(s) for s in extra_specs], [swap(s) for s in out_specs]
    gk = grid[2]
    nr, ne, no = len(rhs_list), len(extra), len(out_shape)
    nci = len(carry.ins) if carry else 0
    nco = len(carry.outs) if carry else 0
    acc_shapes = [acc_shape] * nr + ([(8, acc_shape[1])] if rhs_colsum else [])
    nacc = len(acc_shapes) if gk > 1 else 0

    def body(*refs):
        a_ref = refs[0]
        b_refs = refs[1:1 + nr]
        e_refs = refs[1 + nr:1 + nr + ne]
        base = 1 + nr + ne
        ci_refs = refs[base:base + nci]
        o_refs = refs[base + nci:base + nci + no]
        co_refs = refs[base + nci + no:base + nci + no + nco]
        acc_refs = refs[base + nci + no + nco:base + nci + no + nco + nacc]
        sems = refs[base + nci + no + nco + nacc:]
        ids = [pl.program_id(d) for d in range(3)]
        if carry:
            @pl.when((ids[0] == 0) & (ids[1] == 0) & (ids[2] == 0))
            def _():
                carry.start(ci_refs, co_refs, *sems)

        a = a_ref[...]
        if gk == 1:
            n_axis = 1 - dims[0][1][0]
            for cols in _col_chunks(acc_shape[1]):
                pick = (slice(None), cols) if n_axis == 1 else (cols, slice(None))
                parts = [_dot(a, b[pick], dims) for b in b_refs]
                if rhs_colsum:
                    b0 = b_refs[0][pick]
                    parts.append(_dot(jnp.ones((8, b0.shape[0]), b0.dtype), b0, NN))
                epilogue(parts, e_refs, o_refs, cols)
        else:
            k = ids[2]

            @pl.when(k == 0)
            def _():
                for acc in acc_refs:
                    acc[...] = jnp.zeros_like(acc)

            for acc, b in zip(acc_refs, b_refs):
                acc[...] += _dot(a, b[...], dims)
            if rhs_colsum:
                b0 = b_refs[0][...]
                acc_refs[-1][...] += _dot(jnp.ones((8, b0.shape[0]), b0.dtype), b0, NN)

            @pl.when(k == gk - 1)
            def _():
                epilogue([acc[...] for acc in acc_refs], e_refs, o_refs, slice(None))

        if carry:
            @pl.when((ids[0] == grid[0] - 1) & (ids[1] == grid[1] - 1) & (ids[2] == grid[2] - 1))
            def _():
                carry.finish(ci_refs, co_refs, *sems)

    scratch = [pltpu.VMEM(s, F32) for s in acc_shapes[:nacc]]
    kwargs = {}
    aliases = {1 + nr + e: o for e, o in (extra_aliases or {}).items()}
    if carry:
        scratch += carry.sem_scratch()
        aliases.update({1 + nr + ne + i: no + o for i, o in carry.aliases.items()})
    if aliases:
        kwargs["input_output_aliases"] = aliases
    outs = pl.pallas_call(
        body, name=name, grid=grid,
        in_specs=[lhs_spec, *rhs_specs, *extra_specs, *([ANY] * nci)],
        out_specs=[*out_specs, *([ANY] * nco)],
        out_shape=[*out_shape, *(carry.outs if carry else [])], scratch_shapes=scratch,
        compiler_params=_params(("arbitrary",) * 3 if carry else ("parallel", "parallel", "arbitrary")),
        **kwargs,
    )(lhs, *rhs_list, *extra, *(carry.ins if carry else []))
    return outs


class _Comm:
    def __init__(self, name, ins, outs, aliases, n_sems, start, finish):
        self.name, self.ins, self.outs, self.aliases, self.n_sems = name, list(ins), list(outs), dict(aliases), n_sems
        self.start, self.finish = start, finish

    def sem_scratch(self):
        return [pltpu.SemaphoreType.DMA((self.n_sems,)), pltpu.SemaphoreType.DMA((self.n_sems,))]

    def beside(self, other):
        ni, no, ns = len(self.ins), len(self.outs), self.n_sems

        def split(ins, outs, send_sems, recv_sems):
            mine = (ins[:ni], outs[:no], send_sems.at[pl.ds(0, ns)], recv_sems.at[pl.ds(0, ns)])
            theirs = (ins[ni:], outs[no:], send_sems.at[pl.ds(ns, other.n_sems)], recv_sems.at[pl.ds(ns, other.n_sems)])
            return mine, theirs

        def start(*refs):
            mine, theirs = split(*refs)
            self.start(*mine)
            other.start(*theirs)

        def finish(*refs):
            mine, theirs = split(*refs)
            self.finish(*mine)
            other.finish(*theirs)

        aliases = {**self.aliases, **{ni + i: no + o for i, o in other.aliases.items()}}
        return _Comm(self.name + "+" + other.name, self.ins + other.ins, self.outs + other.outs, aliases,
                     ns + other.n_sems, start, finish)

    def run(self):
        ni = len(self.ins)

        def body(*refs):
            in_refs, out_refs, sems = refs[:ni], refs[ni:ni + len(self.outs)], refs[ni + len(self.outs):]
            self.start(in_refs, out_refs, *sems)
            self.finish(in_refs, out_refs, *sems)

        return pl.pallas_call(
            body, name=self.name, in_specs=[ANY] * ni, out_specs=[ANY] * len(self.outs), out_shape=self.outs,
            input_output_aliases=self.aliases, scratch_shapes=self.sem_scratch(),
        )(*self.ins)


MXU_CHUNK = 256


def _col_chunks(n):
    if n % LANES:
        return [slice(0, n)]
    return [slice(s, min(s + MXU_CHUNK, n)) for s in range(0, n, MXU_CHUNK)]


def _store_epilogue(dtype):
    def ep(parts, e_refs, o_refs, cols):
        o_refs[0][:, cols] = parts[0].astype(dtype)
    return ep


def _rms_fwd(h, g_row):
    S, D = h.shape
    tm = _row_tile(S, D)

    def body(h_ref, g_ref, o_ref):
        x = h_ref[...]
        r = lax.rsqrt(jnp.mean(x * x, axis=-1, keepdims=True) + EPS)
        o_ref[...] = (x * r * g_ref[...]).astype(BF16)

    return pl.pallas_call(
        body, name="rms_fwd", grid=(S // tm,),
        in_specs=[pl.BlockSpec((tm, D), lambda i: (i, 0)), pl.BlockSpec((1, D), lambda i: (0, 0))],
        out_specs=pl.BlockSpec((tm, D), lambda i: (i, 0)),
        out_shape=jax.ShapeDtypeStruct((S, D), BF16),
        compiler_params=_params(("parallel",)),
    )(h, g_row)


def _rms_bwd(dy, h, g_row, dres):
    S, D = h.shape
    tm = _row_tile(S, D)

    def body(dy_ref, h_ref, g_ref, dres_ref, dh_ref, dhb_ref, dg_ref):
        i = pl.program_id(0)
        x = h_ref[...]
        d = dy_ref[...]
        r = lax.rsqrt(jnp.mean(x * x, axis=-1, keepdims=True) + EPS)
        dg = d * g_ref[...]
        dot = jnp.mean(dg * x, axis=-1, keepdims=True)
        dh = dres_ref[...] + r * dg - x * (r * r * r) * dot
        dh_ref[...] = dh
        dhb_ref[...] = dh.astype(BF16)
        part = jnp.sum(d * x * r, axis=0, keepdims=True)

        @pl.when(i == 0)
        def _():
            dg_ref[...] = jnp.zeros_like(dg_ref)

        dg_ref[0:1, :] += part

    return pl.pallas_call(
        body, name="rms_bwd", grid=(S // tm,),
        in_specs=[pl.BlockSpec((tm, D), lambda i: (i, 0)), pl.BlockSpec((tm, D), lambda i: (i, 0)),
                  pl.BlockSpec((1, D), lambda i: (0, 0)), pl.BlockSpec((tm, D), lambda i: (i, 0))],
        out_specs=[pl.BlockSpec((tm, D), lambda i: (i, 0)), pl.BlockSpec((tm, D), lambda i: (i, 0)),
                   pl.BlockSpec((8, D), lambda i: (0, 0))],
        out_shape=[jax.ShapeDtypeStruct((S, D), F32), jax.ShapeDtypeStruct((S, D), BF16),
                   jax.ShapeDtypeStruct((8, D), F32)],
        compiler_params=_params(("arbitrary",)),
    )(dy, h, g_row, dres)


def _loss_head(h, g_row, target):
    S, D = h.shape
    tm = _row_tile(S, D)

    def body(h_ref, g_ref, t_ref, dh_ref, dhb_ref, dg_ref, loss_ref):
        i = pl.program_id(0)
        x = h_ref[...]
        g = g_ref[...]
        r = lax.rsqrt(jnp.mean(x * x, axis=-1, keepdims=True) + EPS)
        y = x * r * g
        e = y - t_ref[...]
        d = e * (1.0 / D)
        dg = d * g
        dot = jnp.mean(dg * x, axis=-1, keepdims=True)
        dh = r * dg - x * (r * r * r) * dot
        dh_ref[...] = dh
        dhb_ref[...] = dh.astype(BF16)

        @pl.when(i == 0)
        def _():
            dg_ref[...] = jnp.zeros_like(dg_ref)
            loss_ref[...] = jnp.zeros_like(loss_ref)

        dg_ref[0:1, :] += jnp.sum(d * x * r, axis=0, keepdims=True)
        loss_ref[0:1, :] += jnp.sum((0.5 / D) * e * e, axis=0, keepdims=True)

    return pl.pallas_call(
        body, name="loss_head", grid=(S // tm,),
        in_specs=[pl.BlockSpec((tm, D), lambda i: (i, 0)), pl.BlockSpec((1, D), lambda i: (0, 0)),
                  pl.BlockSpec((tm, D), lambda i: (i, 0))],
        out_specs=[pl.BlockSpec((tm, D), lambda i: (i, 0)), pl.BlockSpec((tm, D), lambda i: (i, 0)),
                   pl.BlockSpec((8, D), lambda i: (0, 0)), pl.BlockSpec((8, D), lambda i: (0, 0))],
        out_shape=[jax.ShapeDtypeStruct((S, D), F32), jax.ShapeDtypeStruct((S, D), BF16),
                   jax.ShapeDtypeStruct((8, D), F32), jax.ShapeDtypeStruct((8, D), F32)],
        compiler_params=_params(("arbitrary",)),
    )(h, g_row, target)


def _rotary_partner(t):
    half = ROPE_DIM // 2
    if t.shape[-1] == LANES:
        lane = lax.broadcasted_iota(jnp.int32, t.shape, 1) & (HEAD_DIM - 1)
        return jnp.where(lane < half, pltpu.roll(t, LANES - half, 1), pltpu.roll(t, half, 1))
    r = lax.broadcasted_iota(jnp.int32, (LANES, LANES), 0)
    c = lax.broadcasted_iota(jnp.int32, (LANES, LANES), 1)
    cm = c & (HEAD_DIM - 1)
    perm = (((cm < half) & (r == c + half)) | ((cm >= half) & (cm < ROPE_DIM) & (r == c - half))).astype(BF16)
    hi = t.astype(BF16)
    lo = (t - hi.astype(F32)).astype(BF16)
    cols = [slice(s, s + LANES) for s in range(0, t.shape[-1], LANES)]
    return jnp.concatenate([_dot(hi[:, c_], perm, NN) + _dot(lo[:, c_], perm, NN) for c_ in cols], axis=1)


def _rope(t, cos, sin):
    return t * cos + _rotary_partner(t) * sin


def _rope_t(g, cos, sin):
    return g * cos + _rotary_partner(g * sin)


def _band_mask(n, qpk):
    qi = lax.broadcasted_iota(jnp.int32, (qpk * WINDOW, 2 * WINDOW), 0) & (WINDOW - 1)
    kj = lax.broadcasted_iota(jnp.int32, (qpk * WINDOW, 2 * WINDOW), 1)
    rel = qi + WINDOW - kj
    ok = (rel >= 0) & (rel < WINDOW)
    return ok & ((kj >= WINDOW) | (n > 0))


def _stack_heads(x, g, qpk):
    return jnp.concatenate([x[:, (g * qpk + hh) * HEAD_DIM:(g * qpk + hh + 1) * HEAD_DIM] for hh in range(qpk)], axis=0)


def _stack_cols(row, g, qpk):
    return jnp.concatenate([row[:, g * qpk + hh:g * qpk + hh + 1] for hh in range(qpk)], axis=0)


def _attn_specs(dm, nb):
    A, KV = dm["A"], dm["KV"]
    kb, vb = dm["OFF_K"] // KV, dm["OFF_V"] // KV
    cur = lambda n: jnp.minimum(n, nb - 1)
    prev = lambda n: jnp.maximum(jnp.minimum(n, nb - 1) - 1, 0)
    proj_specs = [
        pl.BlockSpec((WINDOW, A), lambda n: (cur(n), 0)),
        pl.BlockSpec((WINDOW, KV), lambda n: (prev(n), kb)),
        pl.BlockSpec((WINDOW, KV), lambda n: (cur(n), kb)),
        pl.BlockSpec((WINDOW, KV), lambda n: (prev(n), vb)),
        pl.BlockSpec((WINDOW, KV), lambda n: (cur(n), vb)),
    ]
    trig_cur = [pl.BlockSpec((WINDOW, LANES), lambda n: (cur(n), 0)) for _ in range(2)]
    trig_prev = [pl.BlockSpec((WINDOW, LANES), lambda n: (prev(n), 0)) for _ in range(2)]
    return proj_specs, trig_cur, trig_prev, cur, prev


def _attn_fwd(proj, sink_row, dm):
    S = proj.shape[0]
    A, KV, NQ = dm["A"], dm["KV"], dm["NQ"]
    qpk = NQ // N_KV_HEADS
    nb = S // WINDOW
    scale = HEAD_DIM ** -0.5
    proj_specs = _attn_specs(dm, nb)[0]

    def body(q_ref, kp_ref, kc_ref, vp_ref, vc_ref, sink_ref, y_ref, lse_ref):
        n = pl.program_id(0)
        qr = q_ref[...]
        kr = jnp.concatenate([kp_ref[...], kc_ref[...]], axis=0)
        vband = jnp.concatenate([vp_ref[...], vc_ref[...]], axis=0)
        mask = _band_mask(n, qpk)
        lane = lax.broadcasted_iota(jnp.int32, (WINDOW, LANES), 1)
        lse_all = jnp.zeros((WINDOW, LANES), F32)
        sink_rows = jnp.broadcast_to(sink_ref[0:1, :], (WINDOW, LANES))
        groups = range(N_KV_HEADS)
        head = lambda x, g: x[:, g * HEAD_DIM:(g + 1) * HEAD_DIM]
        ones = jnp.ones((2 * WINDOW, HEAD_DIM), BF16)
        sink = [_stack_cols(sink_rows, g, qpk) for g in groups]
        s = [jnp.where(mask, _dot(_stack_heads(qr, g, qpk), head(kr, g), NT) * scale, NEG) for g in groups]
        m = [jnp.maximum(jnp.max(s[g], axis=-1, keepdims=True), sink[g]) for g in groups]
        p = [jnp.exp(s[g] - m[g]).astype(BF16) for g in groups]
        ov = [_dot(p[g], jnp.concatenate([head(vband, g), ones], axis=1), NN) for g in groups]
        den = [ov[g][:, HEAD_DIM:HEAD_DIM + 1] + jnp.exp(sink[g] - m[g]) for g in groups]
        o = [ov[g][:, :HEAD_DIM] * (1.0 / den[g]) for g in groups]
        lse = [m[g] + jnp.log(den[g]) for g in groups]
        for g in groups:
            for hh in range(qpk):
                h = g * qpk + hh
                rows = slice(hh * WINDOW, (hh + 1) * WINDOW)
                y_ref[:, h * HEAD_DIM:(h + 1) * HEAD_DIM] = o[g][rows].astype(BF16)
                lse_all = jnp.where(lane == h, lse[g][rows], lse_all)
        lse_ref[...] = lse_all

    return pl.pallas_call(
        body, name="attn_fwd", grid=(nb,),
        in_specs=[*proj_specs, pl.BlockSpec((8, LANES), lambda n: (0, 0))],
        out_specs=[pl.BlockSpec((WINDOW, A), lambda n: (n, 0)), pl.BlockSpec((WINDOW, LANES), lambda n: (n, 0))],
        out_shape=[jax.ShapeDtypeStruct((S, A), BF16), jax.ShapeDtypeStruct((S, LANES), F32)],
        compiler_params=_params(("parallel",)),
    )(proj, proj, proj, proj, proj, sink_row)


def _attn_bwd(proj, trig, sink_row, y, lse, dy, dm):
    S = proj.shape[0]
    A, KV, NQ = dm["A"], dm["KV"], dm["NQ"]
    qpk = NQ // N_KV_HEADS
    nb = S // WINDOW
    scale = HEAD_DIM ** -0.5
    proj_specs, trig_cur, trig_prev, cur, prev = _attn_specs(dm, nb)

    def body(q_ref, kp_ref, kc_ref, vp_ref, vc_ref, cc_ref, sc_ref, cp_ref, sp_ref,
             sink_ref, y_ref, lse_ref, dy_ref, dq_ref, dk_ref, dv_ref, dsink_ref,
             ck_ref, cv_ref, bk_ref, bv_ref, dqr_ref):
        n = pl.program_id(0)

        @pl.when(n == 0)
        def _():
            dsink_ref[...] = jnp.zeros_like(dsink_ref)
            ck_ref[...] = jnp.zeros_like(ck_ref)
            cv_ref[...] = jnp.zeros_like(cv_ref)

        @pl.when(n < nb)
        def _():
            tq = lambda r: jnp.tile(r[...], (1, A // LANES))
            tk = lambda rp, rc: jnp.tile(jnp.concatenate([rp[...], rc[...]], axis=0), (1, KV // LANES))
            cq, sq = tq(cc_ref), tq(sc_ref)
            ck, sk = tk(cp_ref, cc_ref), tk(sp_ref, sc_ref)
            qr = q_ref[...]
            kr = jnp.concatenate([kp_ref[...], kc_ref[...]], axis=0)
            vband = jnp.concatenate([vp_ref[...], vc_ref[...]], axis=0)
            mask = _band_mask(n, qpk)
            lane = lax.broadcasted_iota(jnp.int32, (1, LANES), 1)
            lse_all = lse_ref[...]
            sink_rows = jnp.broadcast_to(sink_ref[0:1, :], (WINDOW, LANES))
            dy_all = dy_ref[...]
            y_all = y_ref[...]
            dsink = jnp.zeros((1, LANES), F32)
            groups = range(N_KV_HEADS)
            head = lambda x, g: x[:, g * HEAD_DIM:(g + 1) * HEAD_DIM]
            q = [_stack_heads(qr, g, qpk) for g in groups]
            dy = [_stack_heads(dy_all, g, qpk) for g in groups]
            lse = [_stack_cols(lse_all, g, qpk) for g in groups]
            s = [jnp.where(mask, _dot(q[g], head(kr, g), NT) * scale, NEG) for g in groups]
            dp = [_dot(dy[g], head(vband, g), NT) for g in groups]
            delta = [jnp.sum(dy[g].astype(F32) * _stack_heads(y_all, g, qpk).astype(F32), axis=-1, keepdims=True)
                     for g in groups]
            p = [jnp.exp(s[g] - lse[g]) for g in groups]
            ds = [(p[g] * (dp[g] - delta[g]) * scale).astype(BF16) for g in groups]
            dq = [_dot(ds[g], head(kr, g), NN) for g in groups]
            for g in groups:
                bk_ref[:, g * HEAD_DIM:(g + 1) * HEAD_DIM] = _dot(ds[g], q[g], TN)
                bv_ref[:, g * HEAD_DIM:(g + 1) * HEAD_DIM] = _dot(p[g].astype(BF16), dy[g], TN)
            for g in groups:
                sink_d = jnp.exp(_stack_cols(sink_rows, g, qpk) - lse[g]) * delta[g]
                for hh in range(qpk):
                    h = g * qpk + hh
                    rows = slice(hh * WINDOW, (hh + 1) * WINDOW)
                    dqr_ref[:, h * HEAD_DIM:(h + 1) * HEAD_DIM] = dq[g][rows]
                    dsink = dsink + jnp.where(lane == h, -jnp.sum(sink_d[rows], axis=0, keepdims=True), 0.0)
            dsink_ref[0:1, :] += dsink
            dq_ref[...] = _rope_t(dqr_ref[...], cq, sq).astype(BF16)
            dkb = _rope_t(bk_ref[...], ck, sk)
            dvb = bv_ref[...]
            dk_ref[...] = (ck_ref[...] + dkb[:WINDOW]).astype(BF16)
            dv_ref[...] = (cv_ref[...] + dvb[:WINDOW]).astype(BF16)
            ck_ref[...] = dkb[WINDOW:]
            cv_ref[...] = dvb[WINDOW:]

        @pl.when(n == nb)
        def _():
            dk_ref[...] = ck_ref[...].astype(BF16)
            dv_ref[...] = cv_ref[...].astype(BF16)

    row = lambda w: pl.BlockSpec((WINDOW, w), lambda n: (cur(n), 0))
    done = lambda w: pl.BlockSpec((WINDOW, w), lambda n: (jnp.maximum(n - 1, 0), 0))
    return pl.pallas_call(
        body, name="attn_bwd", grid=(nb + 1,),
        in_specs=[*proj_specs, *trig_cur, *trig_prev, pl.BlockSpec((8, LANES), lambda n: (0, 0)),
                  row(A), row(LANES), row(A)],
        out_specs=[row(A), done(KV), done(KV), pl.BlockSpec((8, LANES), lambda n: (0, 0))],
        out_shape=[jax.ShapeDtypeStruct((S, A), BF16), jax.ShapeDtypeStruct((S, KV), BF16),
                   jax.ShapeDtypeStruct((S, KV), BF16), jax.ShapeDtypeStruct((8, LANES), F32)],
        scratch_shapes=[pltpu.VMEM((WINDOW, KV), F32), pltpu.VMEM((WINDOW, KV), F32),
                        pltpu.VMEM((2 * WINDOW, KV), F32), pltpu.VMEM((2 * WINDOW, KV), F32),
                        pltpu.VMEM((WINDOW, A), F32)],
        compiler_params=_params(("arbitrary",)),
    )(proj, proj, proj, proj, proj, *trig, *trig, sink_row, y, lse, dy)


def _sgu_layout(dm, S):
    G = dm["G"]
    pw = math.gcd(dm["OFF_Z"], G)
    npc = G // pw
    tm = _pick(S, (256, 128))
    u_specs = [pl.BlockSpec((tm, pw), lambda i, p=p: (i, dm["OFF_Z"] // pw + p)) for p in range(npc)]
    v_specs = [pl.BlockSpec((tm, pw), lambda i, p=p: (i, (dm["OFF_Z"] + G) // pw + p)) for p in range(npc)]
    return pw, npc, tm, u_specs, v_specs


def _sgu_norm(v_refs, lg_ref, lb_ref):
    v = jnp.concatenate([_gelu(r[...].astype(F32)) for r in v_refs], axis=1)
    mu = jnp.mean(v, axis=-1, keepdims=True)
    vc = v - mu
    rstd = lax.rsqrt(jnp.mean(vc * vc, axis=-1, keepdims=True) + EPS)
    xhat = vc * rstd
    return xhat, rstd, (xhat * lg_ref[...] + lb_ref[...]).astype(BF16)


def _sgu_fwd(proj, w_tril, b_t, ln_g_row, ln_b_row, dm):
    S = proj.shape[0]
    G, NG = dm["G"], dm["NG"]
    pw, npc, tm, u_specs, v_specs = _sgu_layout(dm, S)
    nch = tm // WINDOW

    def body(*refs):
        u_refs, v_refs = refs[:npc], refs[npc:2 * npc]
        w_ref, bt_ref, lg_ref, lb_ref, y_ref = refs[2 * npc:]
        _, _, vn = _sgu_norm(v_refs, lg_ref, lb_ref)
        u = jnp.concatenate([_gelu(r[...].astype(F32)) for r in u_refs], axis=1)
        for c in range(nch):
            rows = slice(c * WINDOW, (c + 1) * WINDOW)
            for g in range(NG):
                cols = slice(g * LANES, (g + 1) * LANES)
                sv = _dot(w_ref[g], vn[rows, cols], NN) + bt_ref[:, g:g + 1]
                y_ref[rows, cols] = (u[rows, cols] * sv).astype(BF16)

    return pl.pallas_call(
        body, name="sgu_fwd", grid=(S // tm,),
        in_specs=[*u_specs, *v_specs,
                  pl.BlockSpec((NG, WINDOW, WINDOW), lambda i: (0, 0, 0)),
                  pl.BlockSpec((WINDOW, LANES), lambda i: (0, 0)),
                  pl.BlockSpec((1, G), lambda i: (0, 0)), pl.BlockSpec((1, G), lambda i: (0, 0))],
        out_specs=pl.BlockSpec((tm, G), lambda i: (i, 0)),
        out_shape=jax.ShapeDtypeStruct((S, G), BF16),
        compiler_params=_params(("parallel",)),
    )(*([proj] * (2 * npc)), w_tril, b_t, ln_g_row, ln_b_row)


def _sgu_bwd(proj, w_tril, b_t, ln_g_row, ln_b_row, dy, dm):
    S = proj.shape[0]
    G, NG = dm["G"], dm["NG"]
    pw, npc, tm, u_specs, v_specs = _sgu_layout(dm, S)
    nch = tm // WINDOW

    def body(*refs):
        u_refs, v_refs = refs[:npc], refs[npc:2 * npc]
        w_ref, bt_ref, lg_ref, lb_ref, dy_ref, dz_ref, dw_ref, dbt_ref, dlg_ref, dlb_ref, dvn_ref = refs[2 * npc:]
        i = pl.program_id(0)

        @pl.when(i == 0)
        def _():
            dw_ref[...] = jnp.zeros_like(dw_ref)
            dbt_ref[...] = jnp.zeros_like(dbt_ref)
            dlg_ref[...] = jnp.zeros_like(dlg_ref)
            dlb_ref[...] = jnp.zeros_like(dlb_ref)

        xhat, rstd, vn = _sgu_norm(v_refs, lg_ref, lb_ref)
        u_pre = jnp.concatenate([r[...].astype(F32) for r in u_refs], axis=1)
        u = _gelu(u_pre)
        dy = dy_ref[...].astype(F32)
        lane = lax.broadcasted_iota(jnp.int32, (WINDOW, LANES), 1)
        tri = lax.broadcasted_iota(jnp.int32, (WINDOW, WINDOW), 0) >= lax.broadcasted_iota(jnp.int32, (WINDOW, WINDOW), 1)
        dbt = jnp.zeros((WINDOW, LANES), F32)
        for c in range(nch):
            rows = slice(c * WINDOW, (c + 1) * WINDOW)
            for g in range(NG):
                cols = slice(g * LANES, (g + 1) * LANES)
                vn_cg = vn[rows, cols]
                sv = _dot(w_ref[g], vn_cg, NN) + bt_ref[:, g:g + 1]
                dy_cg = dy[rows, cols]
                dsv = dy_cg * u[rows, cols]
                dsv_b = dsv.astype(BF16)
                dz_ref[rows, cols] = (dy_cg * sv * _gelu_grad(u_pre[rows, cols])).astype(BF16)
                dvn_ref[rows, cols] = _dot(w_ref[g], dsv_b, TN)
                dw_ref[g] += jnp.where(tri, _dot(dsv_b, vn_cg, NT), 0.0)
                dbt = dbt + jnp.where(lane == g, jnp.sum(dsv, axis=-1, keepdims=True), 0.0)
        dbt_ref[...] += dbt
        dvn = dvn_ref[...]
        dlg_ref[0:1, :] += jnp.sum(dvn * xhat, axis=0, keepdims=True)
        dlb_ref[0:1, :] += jnp.sum(dvn, axis=0, keepdims=True)
        dxh = dvn * lg_ref[...]
        dv = rstd * (dxh - jnp.mean(dxh, axis=-1, keepdims=True) - xhat * jnp.mean(dxh * xhat, axis=-1, keepdims=True))
        v_pre = jnp.concatenate([r[...].astype(F32) for r in v_refs], axis=1)
        dz_ref[:, G:] = (dv * _gelu_grad(v_pre)).astype(BF16)

    return pl.pallas_call(
        body, name="sgu_bwd", grid=(S // tm,),
        in_specs=[*u_specs, *v_specs,
                  pl.BlockSpec((NG, WINDOW, WINDOW), lambda i: (0, 0, 0)),
                  pl.BlockSpec((WINDOW, LANES), lambda i: (0, 0)),
                  pl.BlockSpec((1, G), lambda i: (0, 0)), pl.BlockSpec((1, G), lambda i: (0, 0)),
                  pl.BlockSpec((tm, G), lambda i: (i, 0))],
        out_specs=[pl.BlockSpec((tm, 2 * G), lambda i: (i, 0)),
                   pl.BlockSpec((NG, WINDOW, WINDOW), lambda i: (0, 0, 0)),
                   pl.BlockSpec((WINDOW, LANES), lambda i: (0, 0)),
                   pl.BlockSpec((8, G), lambda i: (0, 0)), pl.BlockSpec((8, G), lambda i: (0, 0))],
        out_shape=[jax.ShapeDtypeStruct((S, 2 * G), BF16), jax.ShapeDtypeStruct((NG, WINDOW, WINDOW), F32),
                   jax.ShapeDtypeStruct((WINDOW, LANES), F32), jax.ShapeDtypeStruct((8, G), F32),
                   jax.ShapeDtypeStruct((8, G), F32)],
        scratch_shapes=[pltpu.VMEM((tm, G), F32)],
        compiler_params=_params(("arbitrary",)),
    )(*([proj] * (2 * npc)), w_tril, b_t, ln_g_row, ln_b_row, dy)


def _result(outs, n_main, carry):
    main = outs[0] if n_main == 1 else tuple(outs[:n_main])
    return (main, list(outs[n_main:])) if carry else main


def _in_proj(xn, w_in_g, b_row, trig, dm, carry=None):
    S, D = xn.shape
    IN = dm["IN"]
    cw = IN // N_CHIPS
    tm = _pick(S, (512, 256, 128))
    tn = _pick(cw, (1920, 640, 512, 256, 128))
    nbc = cw // tn
    rope_cols = dm["OFF_V"]
    rope_blocks = -(-rope_cols // tn)

    def roped_store(val, e_refs, o_refs, cols, jj):
        r = min(max(rope_cols - (jj * tn + cols.start), 0), cols.stop - cols.start)
        if not r:
            o_refs[0][:, cols] = val.astype(BF16)
            return
        for rows in [slice(s, s + WINDOW) for s in range(0, tm, WINDOW)]:
            cos, sin = e_refs[1][rows, :], e_refs[2][rows, :]
            for c0 in range(0, r, LANES):
                piece = val[rows, c0:c0 + LANES]
                o_refs[0][rows, cols.start + c0:cols.start + c0 + LANES] = _rope(piece, cos, sin).astype(BF16)
        if r < val.shape[1]:
            o_refs[0][:, cols.start + r:cols.stop] = val[:, r:].astype(BF16)

    def ep_rope(parts, e_refs, o_refs, cols):
        val = parts[0] + e_refs[0][:, cols]
        if rope_blocks == 1:
            roped_store(val, e_refs, o_refs, cols, 0)
        else:
            for jj in range(rope_blocks):
                @pl.when(pl.program_id(0) == jj)
                def _(jj=jj):
                    roped_store(val, e_refs, o_refs, cols, jj)

    def ep_plain(parts, e_refs, o_refs, cols):
        o_refs[0][:, cols] = (parts[0] + e_refs[0][:, cols]).astype(BF16)

    rows = pl.BlockSpec((tm, LANES), lambda i, j, k: (i, 0))
    first = IN // tn - rope_blocks
    common = dict(dims=NN, lhs_spec=pl.BlockSpec((tm, D), lambda i, j, k: (i, 0)), acc_shape=(tm, tn),
                  out_shape=[jax.ShapeDtypeStruct((S, IN), BF16)], cols_outer=True)
    proj = _matmul(
        "in_proj_qk", xn, [w_in_g], grid=(S // tm, rope_blocks, 1),
        rhs_specs=[pl.BlockSpec((None, D, tn), lambda i, j, k: (j // nbc, 0, j % nbc))],
        extra=[b_row, *trig], extra_specs=[pl.BlockSpec((1, tn), lambda i, j, k: (0, j)), rows, rows],
        out_specs=[pl.BlockSpec((tm, tn), lambda i, j, k: (i, j))], epilogue=ep_rope, **common)[0]
    jb = lambda j: j + rope_blocks
    return _result(_matmul(
        "in_proj", xn, [w_in_g], grid=(S // tm, first, 1),
        rhs_specs=[pl.BlockSpec((None, D, tn), lambda i, j, k: (jb(j) // nbc, 0, jb(j) % nbc))],
        extra=[b_row, proj], extra_specs=[pl.BlockSpec((1, tn), lambda i, j, k: (0, jb(j))), ANY], extra_aliases={1: 0},
        out_specs=[pl.BlockSpec((tm, tn), lambda i, j, k: (i, jb(j)))], epilogue=ep_plain, carry=carry, **common), 1, carry)


def _branch_attn(y_attn, w_ab_g, dm):
    S, A = y_attn.shape
    D = dm["D"]
    cw = D // N_CHIPS
    tm = _pick(S, (1024, 512, 256, 128))
    return _matmul(
        "branch_attn", y_attn, [w_ab_g], dims=NN, grid=(S // tm, N_CHIPS, 1),
        lhs_spec=pl.BlockSpec((tm, A), lambda i, j, k: (i, 0)),
        rhs_specs=[pl.BlockSpec((None, A, cw), lambda i, j, k: (j, 0, 0))],
        acc_shape=(tm, cw), out_shape=[jax.ShapeDtypeStruct((S, D), BF16)],
        out_specs=[pl.BlockSpec((tm, cw), lambda i, j, k: (i, j))], epilogue=_store_epilogue(BF16))[0]


def _branch_sgu_merge(y_sgu, w_sb_g, a_attn, proj, dm):
    S, G = y_sgu.shape
    D, OFF_G = dm["D"], dm["OFF_G"]
    cw = D // N_CHIPS
    tm = _pick(S, (1024, 512, 256, 128))

    def ep(parts, e_refs, o_refs, cols):
        a_sgu = parts[0].astype(BF16)
        ga = _sigmoid(e_refs[1][:, cols].astype(F32))
        gs = _sigmoid(e_refs[2][:, cols].astype(F32))
        o_refs[0][:, cols] = a_sgu
        o_refs[1][:, cols] = (ga * e_refs[0][:, cols].astype(F32) + gs * a_sgu.astype(F32)).astype(BF16)

    blk = pl.BlockSpec((tm, cw), lambda i, j, k: (i, j))
    return _matmul(
        "branch_sgu_merge", y_sgu, [w_sb_g], dims=NN, grid=(S // tm, N_CHIPS, 1),
        lhs_spec=pl.BlockSpec((tm, G), lambda i, j, k: (i, 0)),
        rhs_specs=[pl.BlockSpec((None, G, cw), lambda i, j, k: (j, 0, 0))],
        acc_shape=(tm, cw), extra=[a_attn, proj, proj],
        extra_specs=[blk, pl.BlockSpec((tm, cw), lambda i, j, k: (i, OFF_G // cw + j)),
                     pl.BlockSpec((tm, cw), lambda i, j, k: (i, (OFF_G + D) // cw + j))],
        out_shape=[jax.ShapeDtypeStruct((S, D), BF16), jax.ShapeDtypeStruct((S, D), BF16)],
        out_specs=[blk, blk], epilogue=ep)


def _residual_matmul(name, a, w_g, h, carry=None):
    S, K = a.shape
    D = w_g.shape[1]
    tm = _pick(S, (1024, 512, 256, 128))
    tn = _pick(D, (512, 256, 128))

    def ep(parts, e_refs, o_refs, cols):
        o_refs[0][:, cols] = e_refs[0][:, cols] + parts[0]

    blk = pl.BlockSpec((tm, tn), lambda i, j, k: (i, j))
    return _result(_matmul(
        name, a, [w_g], dims=NN, grid=(S // tm, D // tn, 1),
        lhs_spec=pl.BlockSpec((tm, K), lambda i, j, k: (i, 0)),
        rhs_specs=[pl.BlockSpec((K, tn), lambda i, j, k: (0, j))],
        acc_shape=(tm, tn), extra=[h], extra_specs=[blk],
        out_shape=[jax.ShapeDtypeStruct((S, D), F32)], out_specs=[blk], epilogue=ep, carry=carry), 1, carry)


def _gate_up(hn, w_gu_g, dm, carry=None):
    S, D = hn.shape
    Fd = dm["F"]
    cw = 2 * Fd // N_CHIPS
    tm = _pick(S, (512, 256, 128))
    tn = _pick(cw, (1408, 512, 384, 256, 128))
    nbc = cw // tn
    half = N_CHIPS // 2

    def ep(parts, e_refs, o_refs, cols):
        gate, up = parts[0].astype(BF16), parts[1].astype(BF16)
        o_refs[0][0, :, cols] = gate
        o_refs[0][1, :, cols] = up
        g32 = gate.astype(F32)
        o_refs[1][:, cols] = (g32 * _sigmoid(g32) * up.astype(F32)).astype(BF16)

    return _result(_matmul(
        "gate_up", hn, [w_gu_g, w_gu_g], dims=NN, grid=(S // tm, Fd // tn, 1),
        lhs_spec=pl.BlockSpec((tm, D), lambda i, j, k: (i, 0)),
        rhs_specs=[pl.BlockSpec((None, D, tn), lambda i, j, k: (j // nbc, 0, j % nbc)),
                   pl.BlockSpec((None, D, tn), lambda i, j, k: (half + j // nbc, 0, j % nbc))],
        acc_shape=(tm, tn),
        out_shape=[jax.ShapeDtypeStruct((2, S, Fd), BF16), jax.ShapeDtypeStruct((S, Fd), BF16)],
        out_specs=[pl.BlockSpec((2, tm, tn), lambda i, j, k: (0, i, j)), pl.BlockSpec((tm, tn), lambda i, j, k: (i, j))],
        epilogue=ep, carry=carry, cols_outer=True), 2, carry)


def _down_bwd(dh_b, w_down_g, gu, dm, carry=None):
    S, D = dh_b.shape
    Fd = dm["F"]
    tm = _pick(S, (1024, 512, 256, 128))
    tn = _pick(Fd, (512, 256, 128))

    def ep(parts, e_refs, o_refs, cols):
        gate = e_refs[0][0, :, cols].astype(F32)
        up = e_refs[0][1, :, cols].astype(F32)
        s = _sigmoid(gate)
        dact = parts[0]
        o_refs[0][0, :, cols] = (dact * up * s * (1.0 + gate * (1.0 - s))).astype(BF16)
        o_refs[0][1, :, cols] = (dact * gate * s).astype(BF16)

    blk = pl.BlockSpec((2, tm, tn), lambda i, j, k: (0, i, j))
    return _result(_matmul(
        "down_bwd", dh_b, [w_down_g], dims=NT, grid=(S // tm, Fd // tn, 1),
        lhs_spec=pl.BlockSpec((tm, D), lambda i, j, k: (i, 0)),
        rhs_specs=[pl.BlockSpec((tn, D), lambda i, j, k: (j, 0))],
        acc_shape=(tm, tn), extra=[gu], extra_specs=[blk],
        out_shape=[jax.ShapeDtypeStruct((2, S, Fd), BF16)], out_specs=[blk], epilogue=ep, carry=carry), 1, carry)


def _gate_up_bwd(dgu, w_gu_g, dm, carry=None):
    S = dgu.shape[1]
    D, Fd = dm["D"], dm["F"]
    cw = 2 * Fd // N_CHIPS
    half = N_CHIPS // 2
    tm = _pick(S, (1024, 512, 256, 128))
    tn = _pick(D, (1024, 512, 256, 128))
    return _result(_matmul(
        "gate_up_bwd", dgu, [w_gu_g], dims=NT, grid=(S // tm, D // tn, N_CHIPS),
        lhs_spec=pl.BlockSpec((None, tm, cw), lambda i, j, k: (k // half, i, k % half)),
        rhs_specs=[pl.BlockSpec((None, tn, cw), lambda i, j, k: (k, j, 0))],
        acc_shape=(tm, tn), out_shape=[jax.ShapeDtypeStruct((S, D), F32)],
        out_specs=[pl.BlockSpec((tm, tn), lambda i, j, k: (i, j))], epilogue=_store_epilogue(F32), carry=carry), 1, carry)


def _out_bwd(dh_b, w_out_g, proj, a_attn, a_sgu, dm, carry=None):
    S, D = dh_b.shape
    OFF_G = dm["OFF_G"]
    tm = _pick(S, (1024, 512, 256, 128))
    tn = D // N_CHIPS

    def ep(parts, e_refs, o_refs, cols):
        dm_ = parts[0]
        ga = _sigmoid(e_refs[0][:, cols].astype(F32))
        gs = _sigmoid(e_refs[1][:, cols].astype(F32))
        o_refs[0][:, cols] = (dm_ * ga).astype(BF16)
        o_refs[1][:, cols] = (dm_ * gs).astype(BF16)
        o_refs[2][0, :, cols] = (dm_ * e_refs[2][:, cols].astype(F32) * ga * (1.0 - ga)).astype(BF16)
        o_refs[2][1, :, cols] = (dm_ * e_refs[3][:, cols].astype(F32) * gs * (1.0 - gs)).astype(BF16)

    blk = pl.BlockSpec((tm, tn), lambda i, j, k: (i, j))
    return _result(_matmul(
        "out_bwd", dh_b, [w_out_g], dims=NT, grid=(S // tm, D // tn, 1),
        lhs_spec=pl.BlockSpec((tm, D), lambda i, j, k: (i, 0)),
        rhs_specs=[pl.BlockSpec((tn, D), lambda i, j, k: (j, 0))],
        acc_shape=(tm, tn), extra=[proj, proj, a_attn, a_sgu],
        extra_specs=[pl.BlockSpec((tm, tn), lambda i, j, k: (i, OFF_G // tn + j)),
                     pl.BlockSpec((tm, tn), lambda i, j, k: (i, (OFF_G + D) // tn + j)), blk, blk],
        out_shape=[jax.ShapeDtypeStruct((S, D), BF16), jax.ShapeDtypeStruct((S, D), BF16),
                   jax.ShapeDtypeStruct((2, S, D), BF16)],
        out_specs=[blk, blk, pl.BlockSpec((2, tm, tn), lambda i, j, k: (0, i, j))], epilogue=ep, carry=carry), 3, carry)


def _colsharded_bwd(name, dy, w_g, out_dtype, carry=None):
    S = dy.shape[0]
    _, K, cw = w_g.shape
    tm = _pick(S, (1024, 512, 256, 128))
    tn = _pick(K, (1024, 512, 256, 128))
    return _result(_matmul(
        name, dy, [w_g], dims=NT, grid=(S // tm, K // tn, N_CHIPS),
        lhs_spec=pl.BlockSpec((tm, cw), lambda i, j, k: (i, k)),
        rhs_specs=[pl.BlockSpec((None, tn, cw), lambda i, j, k: (k, j, 0))],
        acc_shape=(tm, tn), out_shape=[jax.ShapeDtypeStruct((S, K), out_dtype)],
        out_specs=[pl.BlockSpec((tm, tn), lambda i, j, k: (i, j))], epilogue=_store_epilogue(out_dtype),
        carry=carry), 1, carry)


def _wgrad_cols(name, x, dy, carry=None, colsum=False):
    S, R = x.shape
    C = dy.shape[1]
    cw = C // N_CHIPS
    tm = _pick(R, (1024, 512, 256, 128))
    tk = _pick(S, (1024, 512, 256, 128) if cw >= 1024 else (2048, 1024, 512, 256, 128))

    def ep(parts, e_refs, o_refs, cols):
        for o, p in zip(o_refs, parts):
            o[:, cols] = p

    out_shape = [jax.ShapeDtypeStruct((N_CHIPS, R, cw), F32)]
    out_specs = [pl.BlockSpec((None, tm, cw), lambda i, j, k: (j, i, 0))]
    if colsum:
        out_shape.append(jax.ShapeDtypeStruct((R // tm, N_CHIPS, 8, cw), F32))
        out_specs.append(pl.BlockSpec((None, None, 8, cw), lambda i, j, k: (i, j, 0, 0)))
    return _result(_matmul(
        name, x, [dy], dims=TN, grid=(R // tm, N_CHIPS, S // tk),
        lhs_spec=pl.BlockSpec((tk, tm), lambda i, j, k: (k, i)),
        rhs_specs=[pl.BlockSpec((tk, cw), lambda i, j, k: (k, j))],
        acc_shape=(tm, cw), out_shape=out_shape, out_specs=out_specs, epilogue=ep,
        carry=carry, rhs_colsum=colsum), len(out_shape), carry)


def _wgrad_gate_up(hn, dgu, dm, carry=None):
    S, D = hn.shape
    Fd = dm["F"]
    cw = 2 * Fd // N_CHIPS
    half = N_CHIPS // 2
    tm = _pick(D, (1024, 512, 256, 128))
    tk = _pick(S, (1024, 512, 256, 128))
    tn = _pick(cw, (1408, 512, 384, 256, 128))
    nbc = cw // tn
    return _result(_matmul(
        "wgrad_gate_up", hn, [dgu], dims=TN, grid=(D // tm, 2 * Fd // tn, S // tk),
        lhs_spec=pl.BlockSpec((tk, tm), lambda i, j, k: (k, i)),
        rhs_specs=[pl.BlockSpec((None, tk, tn), lambda i, j, k: (j // (half * nbc), k, j % (half * nbc)))],
        acc_shape=(tm, tn), out_shape=[jax.ShapeDtypeStruct((N_CHIPS, D, cw), F32)],
        out_specs=[pl.BlockSpec((None, tm, tn), lambda i, j, k: (j // nbc, i, j % nbc))], epilogue=_store_epilogue(F32),
        carry=carry), 1, carry)


def _wgrad_rows(name, x, dy):
    S, R = x.shape
    C = dy.shape[1]
    rw = R // N_CHIPS
    tn = _pick(C, (1024, 512, 256, 128))
    tk = _pick(S, (1024, 512, 256, 128))
    return _matmul(
        name, x, [dy], dims=TN, grid=(N_CHIPS, C // tn, S // tk),
        lhs_spec=pl.BlockSpec((tk, rw), lambda i, j, k: (k, i)),
        rhs_specs=[pl.BlockSpec((tk, tn), lambda i, j, k: (k, j))],
        acc_shape=(rw, tn), out_shape=[jax.ShapeDtypeStruct((N_CHIPS, rw, C), F32)],
        out_specs=[pl.BlockSpec((None, rw, tn), lambda i, j, k: (i, 0, j))], epilogue=_store_epilogue(F32))[0]


def _place():
    x, y, c = lax.axis_index("x"), lax.axis_index("y"), lax.axis_index("c")
    others = [(1 - x, y), (x, 1 - y), (1 - x, 1 - y)]
    return x, y, c, others


def _chip_index(chip):
    return 2 * chip[0] + chip[1]


def _gather_weights(bufs):
    n = len(bufs)

    def copies(src, out, send_sems, recv_sems):
        x, y, c, others = _place()

        def half(ref, chip_idx, hc):
            r2 = ref.shape[1] // 2
            return ref.at[chip_idx, pl.ds(hc * r2, r2), :]

        def copy(t, k, chip, hc, to):
            return pltpu.make_async_remote_copy(
                src_ref=half(src[t], _chip_index(chip), hc), dst_ref=half(out[t], _chip_index(chip), hc),
                send_sem=send_sems.at[6 * t + k], recv_sem=recv_sems.at[6 * t + k],
                device_id=to, device_id_type=MESH)

        me, sibling = (x, y, c), (x, y, 1 - c)
        pairs = [(t, j, chip) for t in range(n) for j, chip in enumerate(others)]
        sent = [copy(t, j, (x, y), c, (*chip, c)) for t, j, chip in pairs]
        landed = [copy(t, j, chip, c, me) for t, j, chip in pairs]
        passed = [copy(t, 3 + j, chip, c, sibling) for t, j, chip in pairs]
        handed = [copy(t, 3 + j, chip, 1 - c, me) for t, j, chip in pairs]
        return sent, landed, passed, handed

    def start(src, out, send_sems, recv_sems):
        for cp in copies(src, out, send_sems, recv_sems)[0]:
            cp.start()

    def finish(src, out, send_sems, recv_sems):
        sent, landed, passed, handed = copies(src, out, send_sems, recv_sems)
        for arrival, forward in zip(landed, passed):
            arrival.wait_recv()
            forward.start()
        for cp in handed:
            cp.wait_recv()
        for cp in sent + passed:
            cp.wait_send()

    return _Comm("gather_weights", bufs, [jax.ShapeDtypeStruct(b.shape, BF16) for b in bufs],
                 {t: t for t in range(n)}, 6 * n, start, finish)


def _sibling_exchange(grads):
    n = len(grads)
    shapes = [g.shape for g in grads]

    def copies(src, land, send_sems, recv_sems):
        x, y, c, _ = _place()
        res = []
        for t in range(n):
            r2 = shapes[t][1] // 2
            res.append(pltpu.make_async_remote_copy(
                src_ref=src[t].at[:, pl.ds((1 - c) * r2, r2), :], dst_ref=land[t],
                send_sem=send_sems.at[t], recv_sem=recv_sems.at[t], device_id=(x, y, 1 - c), device_id_type=MESH))
        return res

    def start(*refs):
        for cp in copies(*refs):
            cp.start()

    def finish(*refs):
        remote = copies(*refs)
        for cp in remote:
            cp.wait_recv()
        for cp in remote:
            cp.wait_send()

    return _Comm("sibling_exchange", grads, [jax.ShapeDtypeStruct((s[0], s[1] // 2, s[2]), F32) for s in shapes],
                 {}, n, start, finish)


def _chip_exchange(sends):
    n = len(sends)
    shapes = [s.shape for s in sends]

    def copies(snd, got, send_sems, recv_sems):
        x, y, c, others = _place()
        return [pltpu.make_async_remote_copy(
            src_ref=snd[t].at[_chip_index(chip)], dst_ref=got[t].at[j],
            send_sem=send_sems.at[3 * t + j], recv_sem=recv_sems.at[3 * t + j],
            device_id=(*chip, c), device_id_type=MESH) for t in range(n) for j, chip in enumerate(others)]

    def start(*refs):
        for cp in copies(*refs):
            cp.start()

    def finish(*refs):
        remote = copies(*refs)
        for cp in remote:
            cp.wait_recv()
        for cp in remote:
            cp.wait_send()

    return _Comm("chip_exchange", sends, [jax.ShapeDtypeStruct((3, s[1], s[2]), BF16) for s in shapes],
                 {}, 3 * n, start, finish)


def _sibling_share(fulls):
    n = len(fulls)
    shapes = [f.shape for f in fulls]

    def copies(src, out, send_sems, recv_sems, mine):
        x, y, c, _ = _place()
        hc = c if mine else 1 - c
        res = []
        for t in range(n):
            r2 = shapes[t][0] // 2
            res.append(pltpu.make_async_remote_copy(
                src_ref=src[t].at[pl.ds(hc * r2, r2), :], dst_ref=out[t].at[pl.ds(hc * r2, r2), :],
                send_sem=send_sems.at[t], recv_sem=recv_sems.at[t], device_id=(x, y, 1 - c), device_id_type=MESH))
        return res

    def start(*refs):
        for cp in copies(*refs, mine=True):
            cp.start()

    def finish(*refs):
        for cp in copies(*refs, mine=False):
            cp.wait_recv()
        for cp in copies(*refs, mine=True):
            cp.wait_send()

    return _Comm("sibling_share", fulls, [jax.ShapeDtypeStruct(s, F32) for s in shapes],
                 {t: t for t in range(n)}, n, start, finish)


def _gather_all(v):
    R, C = v.shape

    def body(v_ref, out_ref, send_sems, recv_sems, local_sem):
        x, y, c, others = _place()
        me, sibling = (x, y, c), (x, y, 1 - c)

        def rows(px, py, pc):
            return out_ref.at[4 * px + 2 * py + pc]

        def copy(k, block, to, src=None):
            return pltpu.make_async_remote_copy(
                src_ref=rows(*block) if src is None else src, dst_ref=rows(*block),
                send_sem=send_sems.at[k], recv_sem=recv_sems.at[k], device_id=to, device_id_type=MESH)

        mine = pltpu.make_async_copy(v_ref, rows(*me), local_sem)
        mine.start()
        first = [copy(0, me, sibling, src=v_ref)]
        first += [copy(1 + j, me, (*chip, c), src=v_ref) for j, chip in enumerate(others)]
        for cp in first:
            cp.start()
        passed = [copy(4 + j, (*chip, c), sibling) for j, chip in enumerate(others)]
        for j, chip in enumerate(others):
            copy(1 + j, (*chip, c), me).wait_recv()
            passed[j].start()
        copy(0, sibling, me).wait_recv()
        for j, chip in enumerate(others):
            copy(4 + j, (*chip, 1 - c), me).wait_recv()
        for cp in first + passed:
            cp.wait_send()
        mine.wait()

    return pl.pallas_call(
        body, name="gather_all", in_specs=[ANY], out_specs=ANY,
        out_shape=jax.ShapeDtypeStruct((8, R, C), F32),
        scratch_shapes=[pltpu.SemaphoreType.DMA((7,)), pltpu.SemaphoreType.DMA((7,)), pltpu.SemaphoreType.DMA],
    )(v)


def _my_chip():
    return 2 * lax.axis_index("x") + lax.axis_index("y")


def _my_core():
    return lax.axis_index("c")


def _pair_sum(grad, land):
    K, R2, C = land.shape
    tm = _row_tile(R2, C)
    nrb = R2 // tm

    def body(a_ref, b_ref, sb_ref):
        sb_ref[...] = (a_ref[...] + b_ref[...]).astype(BF16)

    blk = pl.BlockSpec((None, tm, C), lambda k, r: (k, r, 0))
    return pl.pallas_call(
        body, name="pair_sum", grid=(K, nrb),
        in_specs=[pl.BlockSpec((None, tm, C), lambda k, r: (k, _my_core() * nrb + r, 0)), blk],
        out_specs=blk, out_shape=jax.ShapeDtypeStruct((K, R2, C), BF16),
        compiler_params=_params(("parallel", "parallel")),
    )(grad, land)


def _chip_sum(grad, land, got):
    _, R2, C = land.shape
    tm = _row_tile(R2, C)
    nrb = R2 // tm

    def body(a_ref, b_ref, g_ref, s_ref):
        own = a_ref[...] + b_ref[...]
        s_ref[...] = ((own + g_ref[0].astype(F32)) + g_ref[1].astype(F32)) + g_ref[2].astype(F32)

    return pl.pallas_call(
        body, name="chip_sum", grid=(nrb,),
        in_specs=[pl.BlockSpec((None, tm, C), lambda r: (_my_chip(), _my_core() * nrb + r, 0)),
                  pl.BlockSpec((None, tm, C), lambda r: (_my_chip(), r, 0)),
                  pl.BlockSpec((3, tm, C), lambda r: (0, r, 0))],
        out_specs=pl.BlockSpec((tm, C), lambda r: (_my_core() * nrb + r, 0)),
        out_shape=jax.ShapeDtypeStruct((2 * R2, C), F32),
        compiler_params=_params(("parallel",)),
    )(grad, land, got)


def _adamw_math(w, g, m, v):
    m = ADAM_B1 * m + (1.0 - ADAM_B1) * g
    v = ADAM_B2 * v + (1.0 - ADAM_B2) * (g * g)
    m_hat = m / (1.0 - ADAM_B1 ** ADAM_STEP)
    v_hat = v / (1.0 - ADAM_B2 ** ADAM_STEP)
    delta = -ADAM_LR * (m_hat / (jnp.sqrt(v_hat) + ADAM_EPS) + ADAM_WD * w)
    return delta, m, v


def _adamw_stacked(grads, w, m, v):
    L, R, C = w.shape
    tm = _row_tile(R, C)
    nrb = R // tm

    def body(*refs):
        g_refs = refs[:L]
        w_ref, m_ref, v_ref, go_ref, d_ref, mo_ref, vo_ref = refs[L:]
        l = pl.program_id(0)
        for ll in range(L):
            @pl.when(l == ll)
            def _(ll=ll):
                g = g_refs[ll][...]
                delta, mn, vn = _adamw_math(w_ref[...], g, m_ref[...], v_ref[...])
                go_ref[...] = g
                d_ref[...] = delta
                mo_ref[...] = mn
                vo_ref[...] = vn

    stacked = pl.BlockSpec((None, tm, C), lambda l, r: (l, r, 0))
    g_specs = [pl.BlockSpec((tm, C), lambda l, r, ll=ll: (jnp.where(l == ll, r, 0), 0)) for ll in range(L)]
    shp = jax.ShapeDtypeStruct((L, R, C), F32)
    return pl.pallas_call(
        body, name="adamw", grid=(L, nrb),
        in_specs=[*g_specs, stacked, stacked, stacked], out_specs=[stacked] * 4, out_shape=[shp] * 4,
        compiler_params=_params(("arbitrary", "arbitrary")),
    )(*grads, w, m, v)


def _adamw_small(parts, w, m, v):
    _, R, C = parts.shape
    tm = _row_tile(R, 8 * C)

    def body(p_ref, w_ref, m_ref, v_ref, go_ref, d_ref, mo_ref, vo_ref):
        g = p_ref[0]
        for k in range(1, 8):
            g = g + p_ref[k]
        delta, mn, vn = _adamw_math(w_ref[...], g, m_ref[...], v_ref[...])
        go_ref[...] = g
        d_ref[...] = delta
        mo_ref[...] = mn
        vo_ref[...] = vn

    blk = pl.BlockSpec((tm, C), lambda i: (i, 0))
    shp = jax.ShapeDtypeStruct((R, C), F32)
    return pl.pallas_call(
        body, name="adamw_small", grid=(R // tm,),
        in_specs=[pl.BlockSpec((8, tm, C), lambda i: (0, i, 0)), blk, blk, blk],
        out_specs=[blk] * 4, out_shape=[shp] * 4,
        compiler_params=_params(("parallel",)),
    )(parts, w, m, v)


def _cast_place(w, layer):
    _, R, C = w.shape
    tm = _row_tile(R, C)

    def body(w_ref, o_ref):
        o_ref[...] = w_ref[...].astype(BF16)

    return pl.pallas_call(
        body, name="cast_place", grid=(R // tm,),
        in_specs=[pl.BlockSpec((None, tm, C), lambda r: (layer, r, 0))],
        out_specs=pl.BlockSpec((None, tm, C), lambda r: (_my_chip(), r, 0)),
        out_shape=jax.ShapeDtypeStruct((N_CHIPS, R, C), BF16),
        compiler_params=_params(("parallel",)),
    )(w)


def _trig_tables(positions):
    inv_freq = ROPE_THETA ** (-jnp.arange(0, ROPE_DIM, 2, dtype=F32) / ROPE_DIM)
    ang = positions.astype(F32)[:, None] * inv_freq
    cos, sin = jnp.cos(ang), jnp.sin(ang)
    S = positions.shape[0]
    cos_h = jnp.concatenate([cos, cos, jnp.ones((S, HEAD_DIM - ROPE_DIM), F32)], axis=1)
    sin_h = jnp.concatenate([-sin, sin, jnp.zeros((S, HEAD_DIM - ROPE_DIM), F32)], axis=1)
    rep = LANES // HEAD_DIM
    return [jnp.tile(t, (1, rep)) for t in (cos_h, sin_h)]


def _row(vec):
    return vec.reshape(1, -1)


def _lane_row(vec):
    return jnp.zeros((8, LANES), F32).at[0, :vec.shape[0]].set(vec)


def _pack(pieces, rows):
    flat = jnp.concatenate([p.reshape(-1).astype(F32) for p in pieces])
    return jnp.pad(flat, (0, rows * LANES - flat.shape[0])).reshape(rows, LANES)


def kernel(x, positions, norm1_g, w_in, b_in, sinks, sgu_ln_g, sgu_ln_b, sgu_w, sgu_b, w_attn_branch, w_sgu_branch, w_out, norm2_g, w_gate_up, w_down, final_g, loss_target, m_norm1_g, m_w_in, m_b_in, m_sinks, m_sgu_ln_g, m_sgu_ln_b, m_sgu_w, m_sgu_b, m_w_attn_branch, m_w_sgu_branch, m_w_out, m_norm2_g, m_w_gate_up, m_w_down, m_final_g, v_norm1_g, v_w_in, v_b_in, v_sinks, v_sgu_ln_g, v_sgu_ln_b, v_sgu_w, v_sgu_b, v_w_attn_branch, v_w_sgu_branch, v_w_out, v_norm2_g, v_w_gate_up, v_w_down, v_final_g):
    L = norm1_g.shape[0]
    S, D = x.shape[1], x.shape[2]
    NQ = sinks.shape[1]
    A = NQ * HEAD_DIM
    KV = N_KV_HEADS * HEAD_DIM
    G = sgu_ln_g.shape[1]
    NG = sgu_w.shape[1]
    IN = b_in.shape[1]
    Fd = w_down.shape[1] * N_CHIPS
    dm = dict(D=D, A=A, KV=KV, NQ=NQ, G=G, NG=NG, IN=IN, F=Fd,
              OFF_K=A, OFF_V=A + KV, OFF_Z=A + 2 * KV, OFF_G=A + 2 * KV + 2 * G)
    assert sgu_w.shape[2] == WINDOW and G == NG * LANES and IN == dm["OFF_G"] + 2 * D

    h = x[0]
    target = loss_target[0]
    trig = _trig_tables(positions[0])
    tril = jnp.tril(jnp.ones((WINDOW, WINDOW), bool))

    big = [w_in, w_attn_branch, w_sgu_branch, w_out, w_gate_up, w_down]
    big_m = [m_w_in, m_w_attn_branch, m_w_sgu_branch, m_w_out, m_w_gate_up, m_w_down]
    big_v = [v_w_in, v_w_attn_branch, v_w_sgu_branch, v_w_out, v_w_gate_up, v_w_down]

    placed = [[_cast_place(w, l) for w in big] for l in range(L)]
    IN_, AB, SB, OUT, GU, DOWN = range(len(big))
    gathered = [[None] * len(big) for _ in range(L)]
    gathered[0][IN_] = _gather_weights([placed[0][IN_]]).run()[0]

    def fetch(layer, idx):
        return _gather_weights([placed[layer][t] for t in idx]) if layer < L else None

    def fetched(layer, idx, res):
        if layer >= L:
            return res
        main, got = res
        for t, g in zip(idx, got):
            gathered[layer][t] = g
        return main

    def weights(l):
        flat = lambda w, rows: None if w is None else w.reshape(rows, D)
        w_in_g, w_ab_g, w_sb_g, w_out_g, w_gu_g, w_down_g = gathered[l]
        return (w_in_g, w_ab_g, w_sb_g, flat(w_out_g, D), w_gu_g, flat(w_down_g, Fd))

    def small(l):
        return dict(
            g1=_row(norm1_g[l]), b_in=_row(b_in[l]), sink=_lane_row(sinks[l]),
            ln_g=_row(sgu_ln_g[l]), ln_b=_row(sgu_ln_b[l]),
            w_tril=jnp.where(tril[None], sgu_w[l], 0.0).astype(BF16),
            b_t=jnp.zeros((WINDOW, LANES), F32).at[:, :NG].set(sgu_b[l].T),
            g2=_row(norm2_g[l]))

    saved = []
    for l in range(L):
        sp = small(l)
        xn = _rms_fwd(h, sp["g1"])
        now = [AB, SB, OUT, GU] if l == 0 else [DOWN]
        proj = fetched(l, now, _in_proj(xn, gathered[l][IN_], sp["b_in"], trig, dm, carry=fetch(l, now)))
        w_in_g, w_ab_g, w_sb_g, w_out_g = weights(l)[:4]
        y_attn, lse = _attn_fwd(proj, sp["sink"], dm)
        y_sgu = _sgu_fwd(proj, sp["w_tril"], sp["b_t"], sp["ln_g"], sp["ln_b"], dm)
        a_attn = _branch_attn(y_attn, w_ab_g, dm)
        a_sgu, merged = _branch_sgu_merge(y_sgu, w_sb_g, a_attn, proj, dm)
        if l == 0:
            h_mid = fetched(l, [DOWN], _residual_matmul("out_proj", merged, w_out_g, h, carry=fetch(l, [DOWN])))
        else:
            h_mid = _residual_matmul("out_proj", merged, w_out_g, h)
        w_gu_g, w_down_g = weights(l)[4:]
        hn = _rms_fwd(h_mid, sp["g2"])
        ahead = [IN_, AB, SB, OUT]
        gu, act = fetched(l + 1, ahead, _gate_up(hn, w_gu_g, dm, carry=fetch(l + 1, ahead)))
        h_out = fetched(l + 1, [GU], _residual_matmul("down_proj", act, w_down_g, h_mid, carry=fetch(l + 1, [GU])))
        saved.append(dict(h=h, xn=xn, proj=proj, y_attn=y_attn, lse=lse, y_sgu=y_sgu, a_attn=a_attn, a_sgu=a_sgu,
                          merged=merged, h_mid=h_mid, hn=hn, gu=gu, act=act))
        h = h_out

    dh, dh_b, d_final, loss_part = _loss_head(h, _row(final_g), target)

    small_grads = [None] * L
    reduced = [[None] * len(big) for _ in range(L)]
    early, mid, late = [GU, DOWN], [AB, SB, OUT], [IN_]

    def riding(has_carry, res):
        return res if has_carry else (res, None)

    def sends_of(grads, land):
        return [_pair_sum(g, d) for g, d in zip(grads, land)]

    def finished(grads, land, got):
        return [_chip_sum(g, d, p) for g, d, p in zip(grads, land, got)]

    def file_reduced(layer, idx, fulls):
        for t, f in zip(idx, fulls):
            reduced[layer][t] = f

    late_grads = None
    mid_fulls = None
    n_late, n_mid, n_early = len(late), len(mid), len(early)
    for l in reversed(range(L)):
        w_in_g, w_ab_g, w_sb_g, w_out_g, w_gu_g, w_down_g = weights(l)
        sp, sv = small(l), saved[l]
        have = late_grads is not None
        dgu, rode = riding(have, _down_bwd(
            dh_b, w_down_g, sv["gu"], dm,
            carry=_sibling_exchange(late_grads).beside(_sibling_share(mid_fulls)) if have else None))
        if have:
            land = rode[:n_late]
            file_reduced(l + 1, mid, rode[n_late:])
        g_down = _wgrad_rows("wgrad_down", sv["act"], dh_b)
        dhn, got = riding(have, _gate_up_bwd(dgu, w_gu_g, dm,
                                             carry=_chip_exchange(sends_of(late_grads, land)) if have else None))
        g_gu, shared = riding(have, _wgrad_gate_up(sv["hn"], dgu, dm,
                                                   carry=_sibling_share(finished(late_grads, land, got)) if have else None))
        if have:
            file_reduced(l + 1, late, shared)
        dh_mid, dh_mid_b, d_g2 = _rms_bwd(dhn, sv["h_mid"], sp["g2"], dh)
        early_grads = [g_gu, g_down]
        (da_attn, da_sgu, dgate), land_e = _out_bwd(dh_mid_b, w_out_g, sv["proj"], sv["a_attn"], sv["a_sgu"], dm,
                                                     carry=_sibling_exchange(early_grads))
        sends_e = sends_of(early_grads, land_e)
        g_out = _wgrad_rows("wgrad_out", sv["merged"], dh_mid_b)
        dy_attn = _colsharded_bwd("branch_attn_bwd", da_attn, w_ab_g, BF16)
        dy_sgu = _colsharded_bwd("branch_sgu_bwd", da_sgu, w_sb_g, BF16)
        g_ab = _wgrad_cols("wgrad_attn_branch", sv["y_attn"], da_attn)
        g_sb = _wgrad_cols("wgrad_sgu_branch", sv["y_sgu"], da_sgu)
        mid_grads = [g_ab, g_sb, g_out]
        dq, dk, dv, d_sink = _attn_bwd(sv["proj"], trig, sp["sink"], sv["y_attn"], sv["lse"], dy_attn, dm)
        dz, d_sgu_w, d_bt, d_lng, d_lnb = _sgu_bwd(sv["proj"], sp["w_tril"], sp["b_t"], sp["ln_g"], sp["ln_b"], dy_sgu, dm)
        dproj = jnp.concatenate([dq, dk, dv, dz, dgate[0], dgate[1]], axis=1)
        dxn, rode = _colsharded_bwd("in_proj_bwd", dproj, w_in_g, F32,
                                    carry=_chip_exchange(sends_e).beside(_sibling_exchange(mid_grads)))
        got_e, land_m = rode[:n_early], rode[n_early:]
        (g_in, d_bin), rode = _wgrad_cols(
            "wgrad_in", sv["xn"], dproj, colsum=True,
            carry=_sibling_share(finished(early_grads, land_e, got_e)).beside(_chip_exchange(sends_of(mid_grads, land_m))))
        file_reduced(l, early, rode[:n_early])
        mid_fulls = finished(mid_grads, land_m, rode[n_early:])
        dh, dh_b, d_g1 = _rms_bwd(dxn, sv["h"], sp["g1"], dh_mid)
        late_grads = [g_in]
        small_grads[l] = dict(norm1_g=d_g1[0], b_in=d_bin[0, :, 0, :].reshape(-1), sinks=d_sink[0, :NQ],
                              sgu_ln_g=d_lng[0], sgu_ln_b=d_lnb[0], sgu_w=d_sgu_w, sgu_b=d_bt[:, :NG].T, norm2_g=d_g2[0])
    grad_x = dh[None]

    land = _sibling_exchange(late_grads).run()
    got = _chip_exchange(sends_of(late_grads, land)).run()
    shared = _sibling_share(finished(late_grads, land, got) + mid_fulls).run()
    file_reduced(0, late, shared[:n_late])
    file_reduced(0, mid, shared[n_late:])
    big_out = [_adamw_stacked([reduced[l][t] for l in range(L)], big[t], big_m[t], big_v[t]) for t in range(len(big))]

    names = ["norm1_g", "b_in", "sinks", "sgu_ln_g", "sgu_ln_b", "sgu_w", "sgu_b", "norm2_g"]
    small_w = [norm1_g, b_in, sinks, sgu_ln_g, sgu_ln_b, sgu_w, sgu_b, norm2_g, final_g]
    small_m = [m_norm1_g, m_b_in, m_sinks, m_sgu_ln_g, m_sgu_ln_b, m_sgu_w, m_sgu_b, m_norm2_g, m_final_g]
    small_v = [v_norm1_g, v_b_in, v_sinks, v_sgu_ln_g, v_sgu_ln_b, v_sgu_w, v_sgu_b, v_norm2_g, v_final_g]
    small_g = [jnp.stack([small_grads[l][nm] for l in range(L)]) for nm in names] + [d_final[0]]
    sizes = [w.size for w in small_w]
    total = sum(sizes) + 1
    rows = -(-total // (512 * LANES)) * 512
    loss_piece = jnp.sum(loss_part[0]).reshape(1)
    packed_g = _pack(small_g + [loss_piece], rows)
    one = jnp.ones((1,), F32)
    parts = _gather_all(packed_g)
    outs = _adamw_small(parts, _pack(small_w + [one], rows), _pack(small_m + [one], rows), _pack(small_v + [one], rows))

    def unpack(p):
        flat = p.reshape(-1)
        res, off = [], 0
        for w, n in zip(small_w, sizes):
            res.append(flat[off:off + n].reshape(w.shape))
            off += n
        return res, flat[off]

    (sg, loss), (sd, _), (smm, _), (svv, _) = [unpack(o) for o in outs]

    order = ["norm1_g", "w_in", "b_in", "sinks", "sgu_ln_g", "sgu_ln_b", "sgu_w", "sgu_b", "w_attn_branch",
             "w_sgu_branch", "w_out", "norm2_g", "w_gate_up", "w_down", "final_g"]
    big_names = ["w_in", "w_attn_branch", "w_sgu_branch", "w_out", "w_gate_up", "w_down"]
    small_names = names + ["final_g"]

    def collect(kind):
        res = []
        for nm in order:
            if nm in big_names:
                res.append(big_out[big_names.index(nm)][kind])
            else:
                res.append((sg, sd, smm, svv)[kind][small_names.index(nm)])
        return res

    return (loss, grad_x, *collect(0), *collect(1), *collect(2), *collect(3))
```

```python
import math

import jax
import jax.numpy as jnp
from jax import lax
from jax.experimental import pallas as pl
from jax.experimental.pallas import tpu as pltpu

F32 = jnp.float32
BF16 = jnp.bfloat16
MESH = pl.DeviceIdType.MESH
ANY = pl.BlockSpec(memory_space=pl.ANY)

HEAD_DIM = 64
N_KV_HEADS = 4
WINDOW = 128
ROPE_DIM = HEAD_DIM // 4
ROPE_THETA = 500000.0
EPS = 1e-5
NEG = -1e30
N_CHIPS = 4
LANES = 128
V7X_VMEM_LIMIT = 56 * 1024 * 1024

ADAM_LR = 0.001
ADAM_B1 = 0.9
ADAM_B2 = 0.999
ADAM_EPS = 1e-08
ADAM_WD = 0.01
ADAM_STEP = 10

NN = (((1,), (0,)), ((), ()))
NT = (((1,), (1,)), ((), ()))
TN = (((0,), (0,)), ((), ()))


ROW_TILES = (1024, 512, 256, 128, 64, 32, 16, 8)
BLOCK_BYTES = 2 * 1024 * 1024


def _pick(n, prefs):
    for p in prefs:
        if n % p == 0:
            return p
    raise ValueError(f"no tile for {n} among {prefs}")


def _row_tile(rows, cols, itemsize=4):
    return _pick(rows, [t for t in ROW_TILES if t * cols * itemsize <= BLOCK_BYTES or t == ROW_TILES[-1]])


def _dot(a, b, dims):
    return lax.dot_general(a, b, dims, preferred_element_type=F32)


def _sigmoid(x):
    return 0.5 * jnp.tanh(0.5 * x) + 0.5


def _gelu(x):
    return 0.5 * x * (1.0 + lax.erf(x * (1.0 / math.sqrt(2.0))))


def _gelu_grad(x):
    return 0.5 * (1.0 + lax.erf(x * (1.0 / math.sqrt(2.0)))) + x * jnp.exp(-0.5 * x * x) * (1.0 / math.sqrt(2.0 * math.pi))


def _params(sem):
    return pltpu.CompilerParams(dimension_semantics=sem, vmem_limit_bytes=V7X_VMEM_LIMIT)


def _matmul(name, lhs, rhs_list, *, dims, grid, lhs_spec, rhs_specs, acc_shape, out_shape, out_specs,
            epilogue, extra=(), extra_specs=(), carry=None, rhs_colsum=False, cols_outer=False, extra_aliases=None):
    if cols_outer:
        swap = lambda s: s if s.index_map is None else pl.BlockSpec(s.block_shape, lambda j, i, k, f=s.index_map: f(i, j, k))
        grid = (grid[1], grid[0], grid[2])
        lhs_spec, rhs_specs = swap(lhs_spec), [swap(s) for s in rhs_specs]
        extra_specs, out_specs = [swap(s) for s in extra_specs], [swap(s) for s in out_specs]
    gk = grid[2]
    nr, ne, no = len(rhs_list), len(extra), len(out_shape)
    nci = len(carry.ins) if carry else 0
    nco = len(carry.outs) if carry else 0
    acc_shapes = [acc_shape] * nr + ([(8, acc_shape[1])] if rhs_colsum else [])
    nacc = len(acc_shapes) if gk > 1 else 0

    def body(*refs):
        a_ref = refs[0]
        b_refs = refs[1:1 + nr]
        e_refs = refs[1 + nr:1 + nr + ne]
        base = 1 + nr + ne
        ci_refs = refs[base:base + nci]
        o_refs = refs[base + nci:base + nci + no]
        co_refs = refs[base + nci + no:base + nci + no + nco]
        acc_refs = refs[base + nci + no + nco:base + nci + no + nco + nacc]
        sems = refs[base + nci + no + nco + nacc:]
        ids = [pl.program_id(d) for d in range(3)]
        if carry:
            @pl.when((ids[0] == 0) & (ids[1] == 0) & (ids[2] == 0))
            def _():
                carry.start(ci_refs, co_refs, *sems)

        a = a_ref[...]
        if gk == 1:
            n_axis = 1 - dims[0][1][0]
            for cols in _col_chunks(acc_shape[1]):
                pick = (slice(None), cols) if n_axis == 1 else (cols, slice(None))
                parts = [_dot(a, b[pick], dims) for b in b_refs]
                if rhs_colsum:
                    b0 = b_refs[0][pick]
                    parts.append(_dot(jnp.ones((8, b0.shape[0]), b0.dtype), b0, NN))
                epilogue(parts, e_refs, o_refs, cols)
        else:
            k = ids[2]

            @pl.when(k == 0)
            def _():
                for acc in acc_refs:
                    acc[...] = jnp.zeros_like(acc)

            for acc, b in zip(acc_refs, b_refs):
                acc[...] += _dot(a, b[...], dims)
            if rhs_colsum:
                b0 = b_refs[0][...]
                acc_refs[-1][...] += _dot(jnp.ones((8, b0.shape[0]), b0.dtype), b0, NN)

            @pl.when(k == gk - 1)
            def _():
                epilogue([acc[...] for acc in acc_refs], e_refs, o_refs, slice(None))

        if carry:
            @pl.when((ids[0] == grid[0] - 1) & (ids[1] == grid[1] - 1) & (ids[2] == grid[2] - 1))
            def _():
                carry.finish(ci_refs, co_refs, *sems)

    scratch = [pltpu.VMEM(s, F32) for s in acc_shapes[:nacc]]
    kwargs = {}
    aliases = {1 + nr + e: o for e, o in (extra_aliases or {}).items()}
    if carry:
        scratch += carry.sem_scratch()
        aliases.update({1 + nr + ne + i: no + o for i, o in carry.aliases.items()})
    if aliases:
        kwargs["input_output_aliases"] = aliases
    outs = pl.pallas_call(
        body, name=name, grid=grid,
        in_specs=[lhs_spec, *rhs_specs, *extra_specs, *([ANY] * nci)],
        out_specs=[*out_specs, *([ANY] * nco)],
        out_shape=[*out_shape, *(carry.outs if carry else [])], scratch_shapes=scratch,
        compiler_params=_params(("arbitrary",) * 3 if carry else ("parallel", "parallel", "arbitrary")),
        **kwargs,
    )(lhs, *rhs_list, *extra, *(carry.ins if carry else []))
    return outs


class _Comm:
    def __init__(self, name, ins, outs, aliases, n_sems, start, finish):
        self.name, self.ins, self.outs, self.aliases, self.n_sems = name, list(ins), list(outs), dict(aliases), n_sems
        self.start, self.finish = start, finish

    def sem_scratch(self):
        return [pltpu.SemaphoreType.DMA((self.n_sems,)), pltpu.SemaphoreType.DMA((self.n_sems,))]

    def beside(self, other):
        ni, no, ns = len(self.ins), len(self.outs), self.n_sems

        def split(ins, outs, send_sems, recv_sems):
            mine = (ins[:ni], outs[:no], send_sems.at[pl.ds(0, ns)], recv_sems.at[pl.ds(0, ns)])
            theirs = (ins[ni:], outs[no:], send_sems.at[pl.ds(ns, other.n_sems)], recv_sems.at[pl.ds(ns, other.n_sems)])
            return mine, theirs

        def start(*refs):
            mine, theirs = split(*refs)
            self.start(*mine)
            other.start(*theirs)

        def finish(*refs):
            mine, theirs = split(*refs)
            self.finish(*mine)
            other.finish(*theirs)

        aliases = {**self.aliases, **{ni + i: no + o for i, o in other.aliases.items()}}
        return _Comm(self.name + "+" + other.name, self.ins + other.ins, self.outs + other.outs, aliases,
                     ns + other.n_sems, start, finish)

    def run(self):
        ni = len(self.ins)

        def body(*refs):
            in_refs, out_refs, sems = refs[:ni], refs[ni:ni + len(self.outs)], refs[ni + len(self.outs):]
            self.start(in_refs, out_refs, *sems)
            self.finish(in_refs, out_refs, *sems)

        return pl.pallas_call(
            body, name=self.name, in_specs=[ANY] * ni, out_specs=[ANY] * len(self.outs), out_shape=self.outs,
            input_output_aliases=self.aliases, scratch_shapes=self.sem_scratch(),
        )(*self.ins)


MXU_CHUNK = 256


def _col_chunks(n):
    if n % LANES:
        return [slice(0, n)]
    return [slice(s, min(s + MXU_CHUNK, n)) for s in range(0, n, MXU_CHUNK)]


def _store_epilogue(dtype):
    def ep(parts, e_refs, o_refs, cols):
        o_refs[0][:, cols] = parts[0].astype(dtype)
    return ep


def _rms_fwd(h, g_row):
    S, D = h.shape
    tm = _row_tile(S, D)

    def body(h_ref, g_ref, o_ref):
        x = h_ref[...]
        r = lax.rsqrt(jnp.mean(x * x, axis=-1, keepdims=True) + EPS)
        o_ref[...] = (x * r * g_ref[...]).astype(BF16)

    return pl.pallas_call(
        body, name="rms_fwd", grid=(S // tm,),
        in_specs=[pl.BlockSpec((tm, D), lambda i: (i, 0)), pl.BlockSpec((1, D), lambda i: (0, 0))],
        out_specs=pl.BlockSpec((tm, D), lambda i: (i, 0)),
        out_shape=jax.ShapeDtypeStruct((S, D), BF16),
        compiler_params=_params(("parallel",)),
    )(h, g_row)


def _rms_bwd(dy, h, g_row, dres):
    S, D = h.shape
    tm = _row_tile(S, D)

    def body(dy_ref, h_ref, g_ref, dres_ref, dh_ref, dhb_ref, dg_ref):
        i = pl.program_id(0)
        x = h_ref[...]
        d = dy_ref[...]
        r = lax.rsqrt(jnp.mean(x * x, axis=-1, keepdims=True) + EPS)
        dg = d * g_ref[...]
        dot = jnp.mean(dg * x, axis=-1, keepdims=True)
        dh = dres_ref[...] + r * dg - x * (r * r * r) * dot
        dh_ref[...] = dh
        dhb_ref[...] = dh.astype(BF16)
        part = jnp.sum(d * x * r, axis=0, keepdims=True)

        @pl.when(i == 0)
        def _():
            dg_ref[...] = jnp.zeros_like(dg_ref)

        dg_ref[0:1, :] += part

    return pl.pallas_call(
        body, name="rms_bwd", grid=(S // tm,),
        in_specs=[pl.BlockSpec((tm, D), lambda i: (i, 0)), pl.BlockSpec((tm, D), lambda i: (i, 0)),
                  pl.BlockSpec((1, D), lambda i: (0, 0)), pl.BlockSpec((tm, D), lambda i: (i, 0))],
        out_specs=[pl.BlockSpec((tm, D), lambda i: (i, 0)), pl.BlockSpec((tm, D), lambda i: (i, 0)),
                   pl.BlockSpec((8, D), lambda i: (0, 0))],
        out_shape=[jax.ShapeDtypeStruct((S, D), F32), jax.ShapeDtypeStruct((S, D), BF16),
                   jax.ShapeDtypeStruct((8, D), F32)],
        compiler_params=_params(("arbitrary",)),
    )(dy, h, g_row, dres)


def _loss_head(h, g_row, target):
    S, D = h.shape
    tm = _row_tile(S, D)

    def body(h_ref, g_ref, t_ref, dh_ref, dhb_ref, dg_ref, loss_ref):
        i = pl.program_id(0)
        x = h_ref[...]
        g = g_ref[...]
        r = lax.rsqrt(jnp.mean(x * x, axis=-1, keepdims=True) + EPS)
        y = x * r * g
        e = y - t_ref[...]
        d = e * (1.0 / D)
        dg = d * g
        dot = jnp.mean(dg * x, axis=-1, keepdims=True)
        dh = r * dg - x * (r * r * r) * dot
        dh_ref[...] = dh
        dhb_ref[...] = dh.astype(BF16)

        @pl.when(i == 0)
        def _():
            dg_ref[...] = jnp.zeros_like(dg_ref)
            loss_ref[...] = jnp.zeros_like(loss_ref)

        dg_ref[0:1, :] += jnp.sum(d * x * r, axis=0, keepdims=True)
        loss_ref[0:1, :] += jnp.sum((0.5 / D) * e * e, axis=0, keepdims=True)

    return pl.pallas_call(
        body, name="loss_head", grid=(S // tm,),
        in_specs=[pl.BlockSpec((tm, D), lambda i: (i, 0)), pl.BlockSpec((1, D), lambda i: (0, 0)),
                  pl.BlockSpec((tm, D), lambda i: (i, 0))],
        out_specs=[pl.BlockSpec((tm, D), lambda i: (i, 0)), pl.BlockSpec((tm, D), lambda i: (i, 0)),
                   pl.BlockSpec((8, D), lambda i: (0, 0)), pl.BlockSpec((8, D), lambda i: (0, 0))],
        out_shape=[jax.ShapeDtypeStruct((S, D), F32), jax.ShapeDtypeStruct((S, D), BF16),
                   jax.ShapeDtypeStruct((8, D), F32), jax.ShapeDtypeStruct((8, D), F32)],
        compiler_params=_params(("arbitrary",)),
    )(h, g_row, target)


def _rotary_partner(t):
    half = ROPE_DIM // 2
    if t.shape[-1] == LANES:
        lane = lax.broadcasted_iota(jnp.int32, t.shape, 1) & (HEAD_DIM - 1)
        return jnp.where(lane < half, pltpu.roll(t, LANES - half, 1), pltpu.roll(t, half, 1))
    r = lax.broadcasted_iota(jnp.int32, (LANES, LANES), 0)
    c = lax.broadcasted_iota(jnp.int32, (LANES, LANES), 1)
    cm = c & (HEAD_DIM - 1)
    perm = (((cm < half) & (r == c + half)) | ((cm >= half) & (cm < ROPE_DIM) & (r == c - half))).astype(BF16)
    hi = t.astype(BF16)
    lo = (t - hi.astype(F32)).astype(BF16)
    cols = [slice(s, s + LANES) for s in range(0, t.shape[-1], LANES)]
    return jnp.concatenate([_dot(hi[:, c_], perm, NN) + _dot(lo[:, c_], perm, NN) for c_ in cols], axis=1)


def _rope(t, cos, sin):
    return t * cos + _rotary_partner(t) * sin


def _rope_t(g, cos, sin):
    return g * cos + _rotary_partner(g * sin)


def _band_mask(n, qpk):
    qi = lax.broadcasted_iota(jnp.int32, (qpk * WINDOW, 2 * WINDOW), 0) & (WINDOW - 1)
    kj = lax.broadcasted_iota(jnp.int32, (qpk * WINDOW, 2 * WINDOW), 1)
    rel = qi + WINDOW - kj
    ok = (rel >= 0) & (rel < WINDOW)
    return ok & ((kj >= WINDOW) | (n > 0))


def _stack_heads(x, g, qpk):
    return jnp.concatenate([x[:, (g * qpk + hh) * HEAD_DIM:(g * qpk + hh + 1) * HEAD_DIM] for hh in range(qpk)], axis=0)


def _stack_cols(row, g, qpk):
    return jnp.concatenate([row[:, g * qpk + hh:g * qpk + hh + 1] for hh in range(qpk)], axis=0)


def _attn_specs(dm, nb):
    A, KV = dm["A"], dm["KV"]
    kb, vb = dm["OFF_K"] // KV, dm["OFF_V"] // KV
    cur = lambda n: jnp.minimum(n, nb - 1)
    prev = lambda n: jnp.maximum(jnp.minimum(n, nb - 1) - 1, 0)
    proj_specs = [
        pl.BlockSpec((WINDOW, A), lambda n: (cur(n), 0)),
        pl.BlockSpec((WINDOW, KV), lambda n: (prev(n), kb)),
        pl.BlockSpec((WINDOW, KV), lambda n: (cur(n), kb)),
        pl.BlockSpec((WINDOW, KV), lambda n: (prev(n), vb)),
        pl.BlockSpec((WINDOW, KV), lambda n: (cur(n), vb)),
    ]
    trig_cur = [pl.BlockSpec((WINDOW, LANES), lambda n: (cur(n), 0)) for _ in range(2)]
    trig_prev = [pl.BlockSpec((WINDOW, LANES), lambda n: (prev(n), 0)) for _ in range(2)]
    return proj_specs, trig_cur, trig_prev, cur, prev


def _attn_fwd(proj, sink_row, dm):
    S = proj.shape[0]
    A, KV, NQ = dm["A"], dm["KV"], dm["NQ"]
    qpk = NQ // N_KV_HEADS
    nb = S // WINDOW
    scale = HEAD_DIM ** -0.5
    proj_specs = _attn_specs(dm, nb)[0]

    def body(q_ref, kp_ref, kc_ref, vp_ref, vc_ref, sink_ref, y_ref, lse_ref):
        n = pl.program_id(0)
        qr = q_ref[...]
        kr = jnp.concatenate([kp_ref[...], kc_ref[...]], axis=0)
        vband = jnp.concatenate([vp_ref[...], vc_ref[...]], axis=0)
        mask = _band_mask(n, qpk)
        lane = lax.broadcasted_iota(jnp.int32, (WINDOW, LANES), 1)
        lse_all = jnp.zeros((WINDOW, LANES), F32)
        sink_rows = jnp.broadcast_to(sink_ref[0:1, :], (WINDOW, LANES))
        groups = range(N_KV_HEADS)
        head = lambda x, g: x[:, g * HEAD_DIM:(g + 1) * HEAD_DIM]
        ones = jnp.ones((2 * WINDOW, HEAD_DIM), BF16)
        sink = [_stack_cols(sink_rows, g, qpk) for g in groups]
        s = [jnp.where(mask, _dot(_stack_heads(qr, g, qpk), head(kr, g), NT) * scale, NEG) for g in groups]
        m = [jnp.maximum(jnp.max(s[g], axis=-1, keepdims=True), sink[g]) for g in groups]
        p = [jnp.exp(s[g] - m[g]).astype(BF16) for g in groups]
        ov = [_dot(p[g], jnp.concatenate([head(vband, g), ones], axis=1), NN) for g in groups]
        den = [ov[g][:, HEAD_DIM:HEAD_DIM + 1] + jnp.exp(sink[g] - m[g]) for g in groups]
        o = [ov[g][:, :HEAD_DIM] * (1.0 / den[g]) for g in groups]
        lse = [m[g] + jnp.log(den[g]) for g in groups]
        for g in groups:
            for hh in range(qpk):
                h = g * qpk + hh
                rows = slice(hh * WINDOW, (hh + 1) * WINDOW)
                y_ref[:, h * HEAD_DIM:(h + 1) * HEAD_DIM] = o[g][rows].astype(BF16)
                lse_all = jnp.where(lane == h, lse[g][rows], lse_all)
        lse_ref[...] = lse_all

    return pl.pallas_call(
        body, name="attn_fwd", grid=(nb,),
        in_specs=[*proj_specs, pl.BlockSpec((8, LANES), lambda n: (0, 0))],
        out_specs=[pl.BlockSpec((WINDOW, A), lambda n: (n, 0)), pl.BlockSpec((WINDOW, LANES), lambda n: (n, 0))],
        out_shape=[jax.ShapeDtypeStruct((S, A), BF16), jax.ShapeDtypeStruct((S, LANES), F32)],
        compiler_params=_params(("parallel",)),
    )(proj, proj, proj, proj, proj, sink_row)


def _attn_bwd(proj, trig, sink_row, y, lse, dy, dm):
    S = proj.shape[0]
    A, KV, NQ = dm["A"], dm["KV"], dm["NQ"]
    qpk = NQ // N_KV_HEADS
    nb = S // WINDOW
    scale = HEAD_DIM ** -0.5
    proj_specs, trig_cur, trig_prev, cur, prev = _attn_specs(dm, nb)

    def body(q_ref, kp_ref, kc_ref, vp_ref, vc_ref, cc_ref, sc_ref, cp_ref, sp_ref,
             sink_ref, y_ref, lse_ref, dy_ref, dq_ref, dk_ref, dv_ref, dsink_ref,
             ck_ref, cv_ref, bk_ref, bv_ref, dqr_ref):
        n = pl.program_id(0)

        @pl.when(n == 0)
        def _():
            dsink_ref[...] = jnp.zeros_like(dsink_ref)
            ck_ref[...] = jnp.zeros_like(ck_ref)
            cv_ref[...] = jnp.zeros_like(cv_ref)

        @pl.when(n < nb)
        def _():
            tq = lambda r: jnp.tile(r[...], (1, A // LANES))
            tk = lambda rp, rc: jnp.tile(jnp.concatenate([rp[...], rc[...]], axis=0), (1, KV // LANES))
            cq, sq = tq(cc_ref), tq(sc_ref)
            ck, sk = tk(cp_ref, cc_ref), tk(sp_ref, sc_ref)
            qr = q_ref[...]
            kr = jnp.concatenate([kp_ref[...], kc_ref[...]], axis=0)
            vband = jnp.concatenate([vp_ref[...], vc_ref[...]], axis=0)
            mask = _band_mask(n, qpk)
            lane = lax.broadcasted_iota(jnp.int32, (1, LANES), 1)
            lse_all = lse_ref[...]
            sink_rows = jnp.broadcast_to(sink_ref[0:1, :], (WINDOW, LANES))
            dy_all = dy_ref[...]
            y_all = y_ref[...]
            dsink = jnp.zeros((1, LANES), F32)
            groups = range(N_KV_HEADS)
            head = lambda x, g: x[:, g * HEAD_DIM:(g + 1) * HEAD_DIM]
            q = [_stack_heads(qr, g, qpk) for g in groups]
            dy = [_stack_heads(dy_all, g, qpk) for g in groups]
            lse = [_stack_cols(lse_all, g, qpk) for g in groups]
            s = [jnp.where(mask, _dot(q[g], head(kr, g), NT) * scale, NEG) for g in groups]
            dp = [_dot(dy[g], head(vband, g), NT) for g in groups]
            delta = [jnp.sum(dy[g].astype(F32) * _stack_heads(y_all, g, qpk).astype(F32), axis=-1, keepdims=True)
                     for g in groups]
            p = [jnp.exp(s[g] - lse[g]) for g in groups]
            ds = [(p[g] * (dp[g] - delta[g]) * scale).astype(BF16) for g in groups]
            dq = [_dot(ds[g], head(kr, g), NN) for g in groups]
            for g in groups:
                bk_ref[:, g * HEAD_DIM:(g + 1) * HEAD_DIM] = _dot(ds[g], q[g], TN)
                bv_ref[:, g * HEAD_DIM:(g + 1) * HEAD_DIM] = _dot(p[g].astype(BF16), dy[g], TN)
            for g in groups:
                sink_d = jnp.exp(_stack_cols(sink_rows, g, qpk) - lse[g]) * delta[g]
                for hh in range(qpk):
                    h = g * qpk + hh
                    rows = slice(hh * WINDOW, (hh + 1) * WINDOW)
                    dqr_ref[:, h * HEAD_DIM:(h + 1) * HEAD_DIM] = dq[g][rows]
                    dsink = dsink + jnp.where(lane == h, -jnp.sum(sink_d[rows], axis=0, keepdims=True), 0.0)
            dsink_ref[0:1, :] += dsink
            dq_ref[...] = _rope_t(dqr_ref[...], cq, sq).astype(BF16)
            dkb = _rope_t(bk_ref[...], ck, sk)
            dvb = bv_ref[...]
            dk_ref[...] = (ck_ref[...] + dkb[:WINDOW]).astype(BF16)
            dv_ref[...] = (cv_ref[...] + dvb[:WINDOW]).astype(BF16)
            ck_ref[...] = dkb[WINDOW:]
            cv_ref[...] = dvb[WINDOW:]

        @pl.when(n == nb)
        def _():
            dk_ref[...] = ck_ref[...].astype(BF16)
            dv_ref[...] = cv_ref[...].astype(BF16)

    row = lambda w: pl.BlockSpec((WINDOW, w), lambda n: (cur(n), 0))
    done = lambda w: pl.BlockSpec((WINDOW, w), lambda n: (jnp.maximum(n - 1, 0), 0))
    return pl.pallas_call(
        body, name="attn_bwd", grid=(nb + 1,),
        in_specs=[*proj_specs, *trig_cur, *trig_prev, pl.BlockSpec((8, LANES), lambda n: (0, 0)),
                  row(A), row(LANES), row(A)],
        out_specs=[row(A), done(KV), done(KV), pl.BlockSpec((8, LANES), lambda n: (0, 0))],
        out_shape=[jax.ShapeDtypeStruct((S, A), BF16), jax.ShapeDtypeStruct((S, KV), BF16),
                   jax.ShapeDtypeStruct((S, KV), BF16), jax.ShapeDtypeStruct((8, LANES), F32)],
        scratch_shapes=[pltpu.VMEM((WINDOW, KV), F32), pltpu.VMEM((WINDOW, KV), F32),
                        pltpu.VMEM((2 * WINDOW, KV), F32), pltpu.VMEM((2 * WINDOW, KV), F32),
                        pltpu.VMEM((WINDOW, A), F32)],
        compiler_params=_params(("arbitrary",)),
    )(proj, proj, proj, proj, proj, *trig, *trig, sink_row, y, lse, dy)


def _sgu_layout(dm, S):
    G = dm["G"]
    pw = math.gcd(dm["OFF_Z"], G)
    npc = G // pw
    tm = _pick(S, (256, 128))
    u_specs = [pl.BlockSpec((tm, pw), lambda i, p=p: (i, dm["OFF_Z"] // pw + p)) for p in range(npc)]
    v_specs = [pl.BlockSpec((tm, pw), lambda i, p=p: (i, (dm["OFF_Z"] + G) // pw + p)) for p in range(npc)]
    return pw, npc, tm, u_specs, v_specs


def _sgu_norm(v_refs, lg_ref, lb_ref):
    v = jnp.concatenate([_gelu(r[...].astype(F32)) for r in v_refs], axis=1)
    mu = jnp.mean(v, axis=-1, keepdims=True)
    vc = v - mu
    rstd = lax.rsqrt(jnp.mean(vc * vc, axis=-1, keepdims=True) + EPS)
    xhat = vc * rstd
    return xhat, rstd, (xhat * lg_ref[...] + lb_ref[...]).astype(BF16)


def _sgu_fwd(proj, w_tril, b_t, ln_g_row, ln_b_row, dm):
    S = proj.shape[0]
    G, NG = dm["G"], dm["NG"]
    pw, npc, tm, u_specs, v_specs = _sgu_layout(dm, S)
    nch = tm // WINDOW

    def body(*refs):
        u_refs, v_refs = refs[:npc], refs[npc:2 * npc]
        w_ref, bt_ref, lg_ref, lb_ref, y_ref = refs[2 * npc:]
        _, _, vn = _sgu_norm(v_refs, lg_ref, lb_ref)
        u = jnp.concatenate([_gelu(r[...].astype(F32)) for r in u_refs], axis=1)
        for c in range(nch):
            rows = slice(c * WINDOW, (c + 1) * WINDOW)
            for g in range(NG):
                cols = slice(g * LANES, (g + 1) * LANES)
                sv = _dot(w_ref[g], vn[rows, cols], NN) + bt_ref[:, g:g + 1]
                y_ref[rows, cols] = (u[rows, cols] * sv).astype(BF16)

    return pl.pallas_call(
        body, name="sgu_fwd", grid=(S // tm,),
        in_specs=[*u_specs, *v_specs,
                  pl.BlockSpec((NG, WINDOW, WINDOW), lambda i: (0, 0, 0)),
                  pl.BlockSpec((WINDOW, LANES), lambda i: (0, 0)),
                  pl.BlockSpec((1, G), lambda i: (0, 0)), pl.BlockSpec((1, G), lambda i: (0, 0))],
        out_specs=pl.BlockSpec((tm, G), lambda i: (i, 0)),
        out_shape=jax.ShapeDtypeStruct((S, G), BF16),
        compiler_params=_params(("parallel",)),
    )(*([proj] * (2 * npc)), w_tril, b_t, ln_g_row, ln_b_row)


def _sgu_bwd(proj, w_tril, b_t, ln_g_row, ln_b_row, dy, dm):
    S = proj.shape[0]
    G, NG = dm["G"], dm["NG"]
    pw, npc, tm, u_specs, v_specs = _sgu_layout(dm, S)
    nch = tm // WINDOW

    def body(*refs):
        u_refs, v_refs = refs[:npc], refs[npc:2 * npc]
        w_ref, bt_ref, lg_ref, lb_ref, dy_ref, dz_ref, dw_ref, dbt_ref, dlg_ref, dlb_ref, dvn_ref = refs[2 * npc:]
        i = pl.program_id(0)

        @pl.when(i == 0)
        def _():
            dw_ref[...] = jnp.zeros_like(dw_ref)
            dbt_ref[...] = jnp.zeros_like(dbt_ref)
            dlg_ref[...] = jnp.zeros_like(dlg_ref)
            dlb_ref[...] = jnp.zeros_like(dlb_ref)

        xhat, rstd, vn = _sgu_norm(v_refs, lg_ref, lb_ref)
        u_pre = jnp.concatenate([r[...].astype(F32) for r in u_refs], axis=1)
        u = _gelu(u_pre)
        dy = dy_ref[...].astype(F32)
        lane = lax.broadcasted_iota(jnp.int32, (WINDOW, LANES), 1)
        tri = lax.broadcasted_iota(jnp.int32, (WINDOW, WINDOW), 0) >= lax.broadcasted_iota(jnp.int32, (WINDOW, WINDOW), 1)
        dbt = jnp.zeros((WINDOW, LANES), F32)
        for c in range(nch):
            rows = slice(c * WINDOW, (c + 1) * WINDOW)
            for g in range(NG):
                cols = slice(g * LANES, (g + 1) * LANES)
                vn_cg = vn[rows, cols]
                sv = _dot(w_ref[g], vn_cg, NN) + bt_ref[:, g:g + 1]
                dy_cg = dy[rows, cols]
                dsv = dy_cg * u[rows, cols]
                dsv_b = dsv.astype(BF16)
                dz_ref[rows, cols] = (dy_cg * sv * _gelu_grad(u_pre[rows, cols])).astype(BF16)
                dvn_ref[rows, cols] = _dot(w_ref[g], dsv_b, TN)
                dw_ref[g] += jnp.where(tri, _dot(dsv_b, vn_cg, NT), 0.0)
                dbt = dbt + jnp.where(lane == g, jnp.sum(dsv, axis=-1, keepdims=True), 0.0)
        dbt_ref[...] += dbt
        dvn = dvn_ref[...]
        dlg_ref[0:1, :] += jnp.sum(dvn * xhat, axis=0, keepdims=True)
        dlb_ref[0:1, :] += jnp.sum(dvn, axis=0, keepdims=True)
        dxh = dvn * lg_ref[...]
        dv = rstd * (dxh - jnp.mean(dxh, axis=-1, keepdims=True) - xhat * jnp.mean(dxh * xhat, axis=-1, keepdims=True))
        v_pre = jnp.concatenate([r[...].astype(F32) for r in v_refs], axis=1)
        dz_ref[:, G:] = (dv * _gelu_grad(v_pre)).astype(BF16)

    return pl.pallas_call(
        body, name="sgu_bwd", grid=(S // tm,),
        in_specs=[*u_specs, *v_specs,
                  pl.BlockSpec((NG, WINDOW, WINDOW), lambda i: (0, 0, 0)),
                  pl.BlockSpec((WINDOW, LANES), lambda i: (0, 0)),
                  pl.BlockSpec((1, G), lambda i: (0, 0)), pl.BlockSpec((1, G), lambda i: (0, 0)),
                  pl.BlockSpec((tm, G), lambda i: (i, 0))],
        out_specs=[pl.BlockSpec((tm, 2 * G), lambda i: (i, 0)),
                   pl.BlockSpec((NG, WINDOW, WINDOW), lambda i: (0, 0, 0)),
                   pl.BlockSpec((WINDOW, LANES), lambda i: (0, 0)),
                   pl.BlockSpec((8, G), lambda i: (0, 0)), pl.BlockSpec((8, G), lambda i: (0, 0))],
        out_shape=[jax.ShapeDtypeStruct((S, 2 * G), BF16), jax.ShapeDtypeStruct((NG, WINDOW, WINDOW), F32),
                   jax.ShapeDtypeStruct((WINDOW, LANES), F32), jax.ShapeDtypeStruct((8, G), F32),
                   jax.ShapeDtypeStruct((8, G), F32)],
        scratch_shapes=[pltpu.VMEM((tm, G), F32)],
        compiler_params=_params(("arbitrary",)),
    )(*([proj] * (2 * npc)), w_tril, b_t, ln_g_row, ln_b_row, dy)


def _result(outs, n_main, carry):
    main = outs[0] if n_main == 1 else tuple(outs[:n_main])
    return (main, list(outs[n_main:])) if carry else main


def _in_proj(xn, w_in_g, b_row, trig, dm, carry=None):
    S, D = xn.shape
    IN = dm["IN"]
    cw = IN // N_CHIPS
    tm = _pick(S, (512, 256, 128))
    tn = _pick(cw, (1920, 640, 512, 256, 128))
    nbc = cw // tn
    rope_cols = dm["OFF_V"]
    rope_blocks = -(-rope_cols // tn)

    def roped_store(val, e_refs, o_refs, cols, jj):
        r = min(max(rope_cols - (jj * tn + cols.start), 0), cols.stop - cols.start)
        if not r:
            o_refs[0][:, cols] = val.astype(BF16)
            return
        for rows in [slice(s, s + WINDOW) for s in range(0, tm, WINDOW)]:
            cos, sin = e_refs[1][rows, :], e_refs[2][rows, :]
            for c0 in range(0, r, LANES):
                piece = val[rows, c0:c0 + LANES]
                o_refs[0][rows, cols.start + c0:cols.start + c0 + LANES] = _rope(piece, cos, sin).astype(BF16)
        if r < val.shape[1]:
            o_refs[0][:, cols.start + r:cols.stop] = val[:, r:].astype(BF16)

    def ep_rope(parts, e_refs, o_refs, cols):
        val = parts[0] + e_refs[0][:, cols]
        if rope_blocks == 1:
            roped_store(val, e_refs, o_refs, cols, 0)
        else:
            for jj in range(rope_blocks):
                @pl.when(pl.program_id(0) == jj)
                def _(jj=jj):
                    roped_store(val, e_refs, o_refs, cols, jj)

    def ep_plain(parts, e_refs, o_refs, cols):
        o_refs[0][:, cols] = (parts[0] + e_refs[0][:, cols]).astype(BF16)

    rows = pl.BlockSpec((tm, LANES), lambda i, j, k: (i, 0))
    first = IN // tn - rope_blocks
    common = dict(dims=NN, lhs_spec=pl.BlockSpec((tm, D), lambda i, j, k: (i, 0)), acc_shape=(tm, tn),
                  out_shape=[jax.ShapeDtypeStruct((S, IN), BF16)], cols_outer=True)
    proj = _matmul(
        "in_proj_qk", xn, [w_in_g], grid=(S // tm, rope_blocks, 1),
        rhs_specs=[pl.BlockSpec((None, D, tn), lambda i, j, k: (j // nbc, 0, j % nbc))],
        extra=[b_row, *trig], extra_specs=[pl.BlockSpec((1, tn), lambda i, j, k: (0, j)), rows, rows],
        out_specs=[pl.BlockSpec((tm, tn), lambda i, j, k: (i, j))], epilogue=ep_rope, **common)[0]
    jb = lambda j: j + rope_blocks
    return _result(_matmul(
        "in_proj", xn, [w_in_g], grid=(S // tm, first, 1),
        rhs_specs=[pl.BlockSpec((None, D, tn), lambda i, j, k: (jb(j) // nbc, 0, jb(j) % nbc))],
        extra=[b_row, proj], extra_specs=[pl.BlockSpec((1, tn), lambda i, j, k: (0, jb(j))), ANY], extra_aliases={1: 0},
        out_specs=[pl.BlockSpec((tm, tn), lambda i, j, k: (i, jb(j)))], epilogue=ep_plain, carry=carry, **common), 1, carry)


def _branch_attn(y_attn, w_ab_g, dm):
    S, A = y_attn.shape
    D = dm["D"]
    cw = D // N_CHIPS
    tm = _pick(S, (1024, 512, 256, 128))
    return _matmul(
        "branch_attn", y_attn, [w_ab_g], dims=NN, grid=(S // tm, N_CHIPS, 1),
        lhs_spec=pl.BlockSpec((tm, A), lambda i, j, k: (i, 0)),
        rhs_specs=[pl.BlockSpec((None, A, cw), lambda i, j, k: (j, 0, 0))],
        acc_shape=(tm, cw), out_shape=[jax.ShapeDtypeStruct((S, D), BF16)],
        out_specs=[pl.BlockSpec((tm, cw), lambda i, j, k: (i, j))], epilogue=_store_epilogue(BF16))[0]


def _branch_sgu_merge(y_sgu, w_sb_g, a_attn, proj, dm):
    S, G = y_sgu.shape
    D, OFF_G = dm["D"], dm["OFF_G"]
    cw = D // N_CHIPS
    tm = _pick(S, (1024, 512, 256, 128))

    def ep(parts, e_refs, o_refs, cols):
        a_sgu = parts[0].astype(BF16)
        ga = _sigmoid(e_refs[1][:, cols].astype(F32))
        gs = _sigmoid(e_refs[2][:, cols].astype(F32))
        o_refs[0][:, cols] = a_sgu
        o_refs[1][:, cols] = (ga * e_refs[0][:, cols].astype(F32) + gs * a_sgu.astype(F32)).astype(BF16)

    blk = pl.BlockSpec((tm, cw), lambda i, j, k: (i, j))
    return _matmul(
        "branch_sgu_merge", y_sgu, [w_sb_g], dims=NN, grid=(S // tm, N_CHIPS, 1),
        lhs_spec=pl.BlockSpec((tm, G), lambda i, j, k: (i, 0)),
        rhs_specs=[pl.BlockSpec((None, G, cw), lambda i, j, k: (j, 0, 0))],
        acc_shape=(tm, cw), extra=[a_attn, proj, proj],
        extra_specs=[blk, pl.BlockSpec((tm, cw), lambda i, j, k: (i, OFF_G // cw + j)),
                     pl.BlockSpec((tm, cw), lambda i, j, k: (i, (OFF_G + D) // cw + j))],
        out_shape=[jax.ShapeDtypeStruct((S, D), BF16), jax.ShapeDtypeStruct((S, D), BF16)],
        out_specs=[blk, blk], epilogue=ep)


def _residual_matmul(name, a, w_g, h, carry=None):
    S, K = a.shape
    D = w_g.shape[1]
    tm = _pick(S, (1024, 512, 256, 128))
    tn = _pick(D, (512, 256, 128))

    def ep(parts, e_refs, o_refs, cols):
        o_refs[0][:, cols] = e_refs[0][:, cols] + parts[0]

    blk = pl.BlockSpec((tm, tn), lambda i, j, k: (i, j))
    return _result(_matmul(
        name, a, [w_g], dims=NN, grid=(S // tm, D // tn, 1),
        lhs_spec=pl.BlockSpec((tm, K), lambda i, j, k: (i, 0)),
        rhs_specs=[pl.BlockSpec((K, tn), lambda i, j, k: (0, j))],
        acc_shape=(tm, tn), extra=[h], extra_specs=[blk],
        out_shape=[jax.ShapeDtypeStruct((S, D), F32)], out_specs=[blk], epilogue=ep, carry=carry), 1, carry)


def _gate_up(hn, w_gu_g, dm, carry=None):
    S, D = hn.shape
    Fd = dm["F"]
    cw = 2 * Fd // N_CHIPS
    tm = _pick(S, (512, 256, 128))
    tn = _pick(cw, (1408, 512, 384, 256, 128))
    nbc = cw // tn
    half = N_CHIPS // 2

    def ep(parts, e_refs, o_refs, cols):
        gate, up = parts[0].astype(BF16), parts[1].astype(BF16)
        o_refs[0][0, :, cols] = gate
        o_refs[0][1, :, cols] = up
        g32 = gate.astype(F32)
        o_refs[1][:, cols] = (g32 * _sigmoid(g32) * up.astype(F32)).astype(BF16)

    return _result(_matmul(
        "gate_up", hn, [w_gu_g, w_gu_g], dims=NN, grid=(S // tm, Fd // tn, 1),
        lhs_spec=pl.BlockSpec((tm, D), lambda i, j, k: (i, 0)),
        rhs_specs=[pl.BlockSpec((None, D, tn), lambda i, j, k: (j // nbc, 0, j % nbc)),
                   pl.BlockSpec((None, D, tn), lambda i, j, k: (half + j // nbc, 0, j % nbc))],
        acc_shape=(tm, tn),
        out_shape=[jax.ShapeDtypeStruct((2, S, Fd), BF16), jax.ShapeDtypeStruct((S, Fd), BF16)],
        out_specs=[pl.BlockSpec((2, tm, tn), lambda i, j, k: (0, i, j)), pl.BlockSpec((tm, tn), lambda i, j, k: (i, j))],
        epilogue=ep, carry=carry, cols_outer=True), 2, carry)


def _down_bwd(dh_b, w_down_g, gu, dm, carry=None):
    S, D = dh_b.shape
    Fd = dm["F"]
    tm = _pick(S, (1024, 512, 256, 128))
    tn = _pick(Fd, (512, 256, 128))

    def ep(parts, e_refs, o_refs, cols):
        gate = e_refs[0][0, :, cols].astype(F32)
        up = e_refs[0][1, :, cols].astype(F32)
        s = _sigmoid(gate)
        dact = parts[0]
        o_refs[0][0, :, cols] = (dact * up * s * (1.0 + gate * (1.0 - s))).astype(BF16)
        o_refs[0][1, :, cols] = (dact * gate * s).astype(BF16)

    blk = pl.BlockSpec((2, tm, tn), lambda i, j, k: (0, i, j))
    return _result(_matmul(
        "down_bwd", dh_b, [w_down_g], dims=NT, grid=(S // tm, Fd // tn, 1),
        lhs_spec=pl.BlockSpec((tm, D), lambda i, j, k: (i, 0)),
        rhs_specs=[pl.BlockSpec((tn, D), lambda i, j, k: (j, 0))],
        acc_shape=(tm, tn), extra=[gu], extra_specs=[blk],
        out_shape=[jax.ShapeDtypeStruct((2, S, Fd), BF16)], out_specs=[blk], epilogue=ep, carry=carry), 1, carry)


def _gate_up_bwd(dgu, w_gu_g, dm, carry=None):
    S = dgu.shape[1]
    D, Fd = dm["D"], dm["F"]
    cw = 2 * Fd // N_CHIPS
    half = N_CHIPS // 2
    tm = _pick(S, (1024, 512, 256, 128))
    tn = _pick(D, (1024, 512, 256, 128))
    return _result(_matmul(
        "gate_up_bwd", dgu, [w_gu_g], dims=NT, grid=(S // tm, D // tn, N_CHIPS),
        lhs_spec=pl.BlockSpec((None, tm, cw), lambda i, j, k: (k // half, i, k % half)),
        rhs_specs=[pl.BlockSpec((None, tn, cw), lambda i, j, k: (k, j, 0))],
        acc_shape=(tm, tn), out_shape=[jax.ShapeDtypeStruct((S, D), F32)],
        out_specs=[pl.BlockSpec((tm, tn), lambda i, j, k: (i, j))], epilogue=_store_epilogue(F32), carry=carry), 1, carry)


def _out_bwd(dh_b, w_out_g, proj, a_attn, a_sgu, dm, carry=None):
    S, D = dh_b.shape
    OFF_G = dm["OFF_G"]
    tm = _pick(S, (1024, 512, 256, 128))
    tn = D // N_CHIPS

    def ep(parts, e_refs, o_refs, cols):
        dm_ = parts[0]
        ga = _sigmoid(e_refs[0][:, cols].astype(F32))
        gs = _sigmoid(e_refs[1][:, cols].astype(F32))
        o_refs[0][:, cols] = (dm_ * ga).astype(BF16)
        o_refs[1][:, cols] = (dm_ * gs).astype(BF16)
        o_refs[2][0, :, cols] = (dm_ * e_refs[2][:, cols].astype(F32) * ga * (1.0 - ga)).astype(BF16)
        o_refs[2][1, :, cols] = (dm_ * e_refs[3][:, cols].astype(F32) * gs * (1.0 - gs)).astype(BF16)

    blk = pl.BlockSpec((tm, tn), lambda i, j, k: (i, j))
    return _result(_matmul(
        "out_bwd", dh_b, [w_out_g], dims=NT, grid=(S // tm, D // tn, 1),
        lhs_spec=pl.BlockSpec((tm, D), lambda i, j, k: (i, 0)),
        rhs_specs=[pl.BlockSpec((tn, D), lambda i, j, k: (j, 0))],
        acc_shape=(tm, tn), extra=[proj, proj, a_attn, a_sgu],
        extra_specs=[pl.BlockSpec((tm, tn), lambda i, j, k: (i, OFF_G // tn + j)),
                     pl.BlockSpec((tm, tn), lambda i, j, k: (i, (OFF_G + D) // tn + j)), blk, blk],
        out_shape=[jax.ShapeDtypeStruct((S, D), BF16), jax.ShapeDtypeStruct((S, D), BF16),
                   jax.ShapeDtypeStruct((2, S, D), BF16)],
        out_specs=[blk, blk, pl.BlockSpec((2, tm, tn), lambda i, j, k: (0, i, j))], epilogue=ep, carry=carry), 3, carry)


def _colsharded_bwd(name, dy, w_g, out_dtype, carry=None):
    S = dy.shape[0]
    _, K, cw = w_g.shape
    tm = _pick(S, (1024, 512, 256, 128))
    tn = _pick(K, (1024, 512, 256, 128))
    return _result(_matmul(
        name, dy, [w_g], dims=NT, grid=(S // tm, K // tn, N_CHIPS),
        lhs_spec=pl.BlockSpec((tm, cw), lambda i, j, k: (i, k)),
        rhs_specs=[pl.BlockSpec((None, tn, cw), lambda i, j, k: (k, j, 0))],
        acc_shape=(tm, tn), out_shape=[jax.ShapeDtypeStruct((S, K), out_dtype)],
        out_specs=[pl.BlockSpec((tm, tn), lambda i, j, k: (i, j))], epilogue=_store_epilogue(out_dtype),
        carry=carry), 1, carry)


def _wgrad_cols(name, x, dy, carry=None, colsum=False):
    S, R = x.shape
    C = dy.shape[1]
    cw = C // N_CHIPS
    tm = _pick(R, (1024, 512, 256, 128))
    tk = _pick(S, (2048, 1024, 512, 256, 128))

    def ep(parts, e_refs, o_refs, cols):
        for o, p in zip(o_refs, parts):
            o[:, cols] = p

    out_shape = [jax.ShapeDtypeStruct((N_CHIPS, R, cw), F32)]
    out_specs = [pl.BlockSpec((None, tm, cw), lambda i, j, k: (j, i, 0))]
    if colsum:
        out_shape.append(jax.ShapeDtypeStruct((R // tm, N_CHIPS, 8, cw), F32))
        out_specs.append(pl.BlockSpec((None, None, 8, cw), lambda i, j, k: (i, j, 0, 0)))
    return _result(_matmul(
        name, x, [dy], dims=TN, grid=(R // tm, N_CHIPS, S // tk),
        lhs_spec=pl.BlockSpec((tk, tm), lambda i, j, k: (k, i)),
        rhs_specs=[pl.BlockSpec((tk, cw), lambda i, j, k: (k, j))],
        acc_shape=(tm, cw), out_shape=out_shape, out_specs=out_specs, epilogue=ep,
        carry=carry, rhs_colsum=colsum), len(out_shape), carry)


def _wgrad_gate_up(hn, dgu, dm, carry=None):
    S, D = hn.shape
    Fd = dm["F"]
    cw = 2 * Fd // N_CHIPS
    half = N_CHIPS // 2
    tm = _pick(D, (1024, 512, 256, 128))
    tk = _pick(S, (2048, 1024, 512, 256, 128))
    tn = _pick(cw, (1408, 512, 384, 256, 128))
    nbc = cw // tn
    return _result(_matmul(
        "wgrad_gate_up", hn, [dgu], dims=TN, grid=(D // tm, 2 * Fd // tn, S // tk),
        lhs_spec=pl.BlockSpec((tk, tm), lambda i, j, k: (k, i)),
        rhs_specs=[pl.BlockSpec((None, tk, tn), lambda i, j, k: (j // (half * nbc), k, j % (half * nbc)))],
        acc_shape=(tm, tn), out_shape=[jax.ShapeDtypeStruct((N_CHIPS, D, cw), F32)],
        out_specs=[pl.BlockSpec((None, tm, tn), lambda i, j, k: (j // nbc, i, j % nbc))], epilogue=_store_epilogue(F32),
        carry=carry), 1, carry)


def _wgrad_rows(name, x, dy):
    S, R = x.shape
    C = dy.shape[1]
    rw = R // N_CHIPS
    tn = _pick(C, (1024, 512, 256, 128))
    tk = _pick(S, (2048, 1024, 512, 256, 128))
    return _matmul(
        name, x, [dy], dims=TN, grid=(N_CHIPS, C // tn, S // tk),
        lhs_spec=pl.BlockSpec((tk, rw), lambda i, j, k: (k, i)),
        rhs_specs=[pl.BlockSpec((tk, tn), lambda i, j, k: (k, j))],
        acc_shape=(rw, tn), out_shape=[jax.ShapeDtypeStruct((N_CHIPS, rw, C), F32)],
        out_specs=[pl.BlockSpec((None, rw, tn), lambda i, j, k: (i, 0, j))], epilogue=_store_epilogue(F32))[0]


def _place():
    x, y, c = lax.axis_index("x"), lax.axis_index("y"), lax.axis_index("c")
    others = [(1 - x, y), (x, 1 - y), (1 - x, 1 - y)]
    return x, y, c, others


def _chip_index(chip):
    return 2 * chip[0] + chip[1]


def _gather_weights(bufs):
    n = len(bufs)

    def copies(src, out, send_sems, recv_sems):
        x, y, c, others = _place()

        def half(ref, chip_idx, hc):
            r2 = ref.shape[1] // 2
            return ref.at[chip_idx, pl.ds(hc * r2, r2), :]

        def copy(t, k, chip, hc, to):
            return pltpu.make_async_remote_copy(
                src_ref=half(src[t], _chip_index(chip), hc), dst_ref=half(out[t], _chip_index(chip), hc),
                send_sem=send_sems.at[6 * t + k], recv_sem=recv_sems.at[6 * t + k],
                device_id=to, device_id_type=MESH)

        me, sibling = (x, y, c), (x, y, 1 - c)
        pairs = [(t, j, chip) for t in range(n) for j, chip in enumerate(others)]
        sent = [copy(t, j, (x, y), c, (*chip, c)) for t, j, chip in pairs]
        landed = [copy(t, j, chip, c, me) for t, j, chip in pairs]
        passed = [copy(t, 3 + j, chip, c, sibling) for t, j, chip in pairs]
        handed = [copy(t, 3 + j, chip, 1 - c, me) for t, j, chip in pairs]
        return sent, landed, passed, handed

    def start(src, out, send_sems, recv_sems):
        for cp in copies(src, out, send_sems, recv_sems)[0]:
            cp.start()

    def finish(src, out, send_sems, recv_sems):
        sent, landed, passed, handed = copies(src, out, send_sems, recv_sems)
        for arrival, forward in zip(landed, passed):
            arrival.wait_recv()
            forward.start()
        for cp in handed:
            cp.wait_recv()
        for cp in sent + passed:
            cp.wait_send()

    return _Comm("gather_weights", bufs, [jax.ShapeDtypeStruct(b.shape, BF16) for b in bufs],
                 {t: t for t in range(n)}, 6 * n, start, finish)


def _sibling_exchange(grads):
    n = len(grads)
    shapes = [g.shape for g in grads]

    def copies(src, land, send_sems, recv_sems):
        x, y, c, _ = _place()
        res = []
        for t in range(n):
            r2 = shapes[t][1] // 2
            res.append(pltpu.make_async_remote_copy(
                src_ref=src[t].at[:, pl.ds((1 - c) * r2, r2), :], dst_ref=land[t],
                send_sem=send_sems.at[t], recv_sem=recv_sems.at[t], device_id=(x, y, 1 - c), device_id_type=MESH))
        return res

    def start(*refs):
        for cp in copies(*refs):
            cp.start()

    def finish(*refs):
        remote = copies(*refs)
        for cp in remote:
            cp.wait_recv()
        for cp in remote:
            cp.wait_send()

    return _Comm("sibling_exchange", grads, [jax.ShapeDtypeStruct((s[0], s[1] // 2, s[2]), F32) for s in shapes],
                 {}, n, start, finish)


def _chip_exchange(sends):
    n = len(sends)
    shapes = [s.shape for s in sends]

    def copies(snd, got, send_sems, recv_sems):
        x, y, c, others = _place()
        return [pltpu.make_async_remote_copy(
            src_ref=snd[t].at[_chip_index(chip)], dst_ref=got[t].at[j],
            send_sem=send_sems.at[3 * t + j], recv_sem=recv_sems.at[3 * t + j],
            device_id=(*chip, c), device_id_type=MESH) for t in range(n) for j, chip in enumerate(others)]

    def start(*refs):
        for cp in copies(*refs):
            cp.start()

    def finish(*refs):
        remote = copies(*refs)
        for cp in remote:
            cp.wait_recv()
        for cp in remote:
            cp.wait_send()

    return _Comm("chip_exchange", sends, [jax.ShapeDtypeStruct((3, s[1], s[2]), BF16) for s in shapes],
                 {}, 3 * n, start, finish)


def _sibling_share(fulls):
    n = len(fulls)
    shapes = [f.shape for f in fulls]

    def copies(src, out, send_sems, recv_sems, mine):
        x, y, c, _ = _place()
        hc = c if mine else 1 - c
        res = []
        for t in range(n):
            r2 = shapes[t][0] // 2
            res.append(pltpu.make_async_remote_copy(
                src_ref=src[t].at[pl.ds(hc * r2, r2), :], dst_ref=out[t].at[pl.ds(hc * r2, r2), :],
                send_sem=send_sems.at[t], recv_sem=recv_sems.at[t], device_id=(x, y, 1 - c), device_id_type=MESH))
        return res

    def start(*refs):
        for cp in copies(*refs, mine=True):
            cp.start()

    def finish(*refs):
        for cp in copies(*refs, mine=False):
            cp.wait_recv()
        for cp in copies(*refs, mine=True):
            cp.wait_send()

    return _Comm("sibling_share", fulls, [jax.ShapeDtypeStruct(s, F32) for s in shapes],
                 {t: t for t in range(n)}, n, start, finish)


def _gather_all(v):
    R, C = v.shape

    def body(v_ref, out_ref, send_sems, recv_sems, local_sem):
        x, y, c, others = _place()
        me, sibling = (x, y, c), (x, y, 1 - c)

        def rows(px, py, pc):
            return out_ref.at[4 * px + 2 * py + pc]

        def copy(k, block, to, src=None):
            return pltpu.make_async_remote_copy(
                src_ref=rows(*block) if src is None else src, dst_ref=rows(*block),
                send_sem=send_sems.at[k], recv_sem=recv_sems.at[k], device_id=to, device_id_type=MESH)

        mine = pltpu.make_async_copy(v_ref, rows(*me), local_sem)
        mine.start()
        first = [copy(0, me, sibling, src=v_ref)]
        first += [copy(1 + j, me, (*chip, c), src=v_ref) for j, chip in enumerate(others)]
        for cp in first:
            cp.start()
        passed = [copy(4 + j, (*chip, c), sibling) for j, chip in enumerate(others)]
        for j, chip in enumerate(others):
            copy(1 + j, (*chip, c), me).wait_recv()
            passed[j].start()
        copy(0, sibling, me).wait_recv()
        for j, chip in enumerate(others):
            copy(4 + j, (*chip, 1 - c), me).wait_recv()
        for cp in first + passed:
            cp.wait_send()
        mine.wait()

    return pl.pallas_call(
        body, name="gather_all", in_specs=[ANY], out_specs=ANY,
        out_shape=jax.ShapeDtypeStruct((8, R, C), F32),
        scratch_shapes=[pltpu.SemaphoreType.DMA((7,)), pltpu.SemaphoreType.DMA((7,)), pltpu.SemaphoreType.DMA],
    )(v)


def _my_chip():
    return 2 * lax.axis_index("x") + lax.axis_index("y")


def _my_core():
    return lax.axis_index("c")


def _pair_sum(grad, land):
    K, R2, C = land.shape
    tm = _row_tile(R2, C)
    nrb = R2 // tm

    def body(a_ref, b_ref, sb_ref):
        sb_ref[...] = (a_ref[...] + b_ref[...]).astype(BF16)

    blk = pl.BlockSpec((None, tm, C), lambda k, r: (k, r, 0))
    return pl.pallas_call(
        body, name="pair_sum", grid=(K, nrb),
        in_specs=[pl.BlockSpec((None, tm, C), lambda k, r: (k, _my_core() * nrb + r, 0)), blk],
        out_specs=blk, out_shape=jax.ShapeDtypeStruct((K, R2, C), BF16),
        compiler_params=_params(("parallel", "parallel")),
    )(grad, land)


def _chip_sum(grad, land, got):
    _, R2, C = land.shape
    tm = _row_tile(R2, C)
    nrb = R2 // tm

    def body(a_ref, b_ref, g_ref, s_ref):
        own = a_ref[...] + b_ref[...]
        s_ref[...] = ((own + g_ref[0].astype(F32)) + g_ref[1].astype(F32)) + g_ref[2].astype(F32)

    return pl.pallas_call(
        body, name="chip_sum", grid=(nrb,),
        in_specs=[pl.BlockSpec((None, tm, C), lambda r: (_my_chip(), _my_core() * nrb + r, 0)),
                  pl.BlockSpec((None, tm, C), lambda r: (_my_chip(), r, 0)),
                  pl.BlockSpec((3, tm, C), lambda r: (0, r, 0))],
        out_specs=pl.BlockSpec((tm, C), lambda r: (_my_core() * nrb + r, 0)),
        out_shape=jax.ShapeDtypeStruct((2 * R2, C), F32),
        compiler_params=_params(("parallel",)),
    )(grad, land, got)


def _adamw_math(w, g, m, v):
    m = ADAM_B1 * m + (1.0 - ADAM_B1) * g
    v = ADAM_B2 * v + (1.0 - ADAM_B2) * (g * g)
    m_hat = m / (1.0 - ADAM_B1 ** ADAM_STEP)
    v_hat = v / (1.0 - ADAM_B2 ** ADAM_STEP)
    delta = -ADAM_LR * (m_hat / (jnp.sqrt(v_hat) + ADAM_EPS) + ADAM_WD * w)
    return delta, m, v


def _adamw_stacked(grads, w, m, v):
    L, R, C = w.shape
    tm = _row_tile(R, C)
    nrb = R // tm

    def body(*refs):
        g_refs = refs[:L]
        w_ref, m_ref, v_ref, go_ref, d_ref, mo_ref, vo_ref = refs[L:]
        l = pl.program_id(0)
        for ll in range(L):
            @pl.when(l == ll)
            def _(ll=ll):
                g = g_refs[ll][...]
                delta, mn, vn = _adamw_math(w_ref[...], g, m_ref[...], v_ref[...])
                go_ref[...] = g
                d_ref[...] = delta
                mo_ref[...] = mn
                vo_ref[...] = vn

    stacked = pl.BlockSpec((None, tm, C), lambda l, r: (l, r, 0))
    g_specs = [pl.BlockSpec((tm, C), lambda l, r, ll=ll: (jnp.where(l == ll, r, 0), 0)) for ll in range(L)]
    shp = jax.ShapeDtypeStruct((L, R, C), F32)
    return pl.pallas_call(
        body, name="adamw", grid=(L, nrb),
        in_specs=[*g_specs, stacked, stacked, stacked], out_specs=[stacked] * 4, out_shape=[shp] * 4,
        compiler_params=_params(("arbitrary", "arbitrary")),
    )(*grads, w, m, v)


def _adamw_small(parts, w, m, v):
    _, R, C = parts.shape
    tm = _row_tile(R, 8 * C)

    def body(p_ref, w_ref, m_ref, v_ref, go_ref, d_ref, mo_ref, vo_ref):
        g = p_ref[0]
        for k in range(1, 8):
            g = g + p_ref[k]
        delta, mn, vn = _adamw_math(w_ref[...], g, m_ref[...], v_ref[...])
        go_ref[...] = g
        d_ref[...] = delta
        mo_ref[...] = mn
        vo_ref[...] = vn

    blk = pl.BlockSpec((tm, C), lambda i: (i, 0))
    shp = jax.ShapeDtypeStruct((R, C), F32)
    return pl.pallas_call(
        body, name="adamw_small", grid=(R // tm,),
        in_specs=[pl.BlockSpec((8, tm, C), lambda i: (0, i, 0)), blk, blk, blk],
        out_specs=[blk] * 4, out_shape=[shp] * 4,
        compiler_params=_params(("parallel",)),
    )(parts, w, m, v)


def _cast_place(w, layer):
    _, R, C = w.shape
    tm = _row_tile(R, C)

    def body(w_ref, o_ref):
        o_ref[...] = w_ref[...].astype(BF16)

    return pl.pallas_call(
        body, name="cast_place", grid=(R // tm,),
        in_specs=[pl.BlockSpec((None, tm, C), lambda r: (layer, r, 0))],
        out_specs=pl.BlockSpec((None, tm, C), lambda r: (_my_chip(), r, 0)),
        out_shape=jax.ShapeDtypeStruct((N_CHIPS, R, C), BF16),
        compiler_params=_params(("parallel",)),
    )(w)


def _trig_tables(positions):
    inv_freq = ROPE_THETA ** (-jnp.arange(0, ROPE_DIM, 2, dtype=F32) / ROPE_DIM)
    ang = positions.astype(F32)[:, None] * inv_freq
    cos, sin = jnp.cos(ang), jnp.sin(ang)
    S = positions.shape[0]
    cos_h = jnp.concatenate([cos, cos, jnp.ones((S, HEAD_DIM - ROPE_DIM), F32)], axis=1)
    sin_h = jnp.concatenate([-sin, sin, jnp.zeros((S, HEAD_DIM - ROPE_DIM), F32)], axis=1)
    rep = LANES // HEAD_DIM
    return [jnp.tile(t, (1, rep)) for t in (cos_h, sin_h)]


def _row(vec):
    return vec.reshape(1, -1)


def _lane_row(vec):
    return jnp.zeros((8, LANES), F32).at[0, :vec.shape[0]].set(vec)


def _pack(pieces, rows):
    flat = jnp.concatenate([p.reshape(-1).astype(F32) for p in pieces])
    return jnp.pad(flat, (0, rows * LANES - flat.shape[0])).reshape(rows, LANES)


def kernel(x, positions, norm1_g, w_in, b_in, sinks, sgu_ln_g, sgu_ln_b, sgu_w, sgu_b, w_attn_branch, w_sgu_branch, w_out, norm2_g, w_gate_up, w_down, final_g, loss_target, m_norm1_g, m_w_in, m_b_in, m_sinks, m_sgu_ln_g, m_sgu_ln_b, m_sgu_w, m_sgu_b, m_w_attn_branch, m_w_sgu_branch, m_w_out, m_norm2_g, m_w_gate_up, m_w_down, m_final_g, v_norm1_g, v_w_in, v_b_in, v_sinks, v_sgu_ln_g, v_sgu_ln_b, v_sgu_w, v_sgu_b, v_w_attn_branch, v_w_sgu_branch, v_w_out, v_norm2_g, v_w_gate_up, v_w_down, v_final_g):
    L = norm1_g.shape[0]
    S, D = x.shape[1], x.shape[2]
    NQ = sinks.shape[1]
    A = NQ * HEAD_DIM
    KV = N_KV_HEADS * HEAD_DIM
    G = sgu_ln_g.shape[1]
    NG = sgu_w.shape[1]
    IN = b_in.shape[1]
    Fd = w_down.shape[1] * N_CHIPS
    dm = dict(D=D, A=A, KV=KV, NQ=NQ, G=G, NG=NG, IN=IN, F=Fd,
              OFF_K=A, OFF_V=A + KV, OFF_Z=A + 2 * KV, OFF_G=A + 2 * KV + 2 * G)
    assert sgu_w.shape[2] == WINDOW and G == NG * LANES and IN == dm["OFF_G"] + 2 * D

    h = x[0]
    target = loss_target[0]
    trig = _trig_tables(positions[0])
    tril = jnp.tril(jnp.ones((WINDOW, WINDOW), bool))

    big = [w_in, w_attn_branch, w_sgu_branch, w_out, w_gate_up, w_down]
    big_m = [m_w_in, m_w_attn_branch, m_w_sgu_branch, m_w_out, m_w_gate_up, m_w_down]
    big_v = [v_w_in, v_w_attn_branch, v_w_sgu_branch, v_w_out, v_w_gate_up, v_w_down]

    placed = [[_cast_place(w, l) for w in big] for l in range(L)]
    IN_, AB, SB, OUT, GU, DOWN = range(len(big))
    gathered = [[None] * len(big) for _ in range(L)]
    gathered[0][IN_] = _gather_weights([placed[0][IN_]]).run()[0]

    def fetch(layer, idx):
        return _gather_weights([placed[layer][t] for t in idx]) if layer < L else None

    def fetched(layer, idx, res):
        if layer >= L:
            return res
        main, got = res
        for t, g in zip(idx, got):
            gathered[layer][t] = g
        return main

    def weights(l):
        flat = lambda w, rows: None if w is None else w.reshape(rows, D)
        w_in_g, w_ab_g, w_sb_g, w_out_g, w_gu_g, w_down_g = gathered[l]
        return (w_in_g, w_ab_g, w_sb_g, flat(w_out_g, D), w_gu_g, flat(w_down_g, Fd))

    def small(l):
        return dict(
            g1=_row(norm1_g[l]), b_in=_row(b_in[l]), sink=_lane_row(sinks[l]),
            ln_g=_row(sgu_ln_g[l]), ln_b=_row(sgu_ln_b[l]),
            w_tril=jnp.where(tril[None], sgu_w[l], 0.0).astype(BF16),
            b_t=jnp.zeros((WINDOW, LANES), F32).at[:, :NG].set(sgu_b[l].T),
            g2=_row(norm2_g[l]))

    saved = []
    for l in range(L):
        sp = small(l)
        xn = _rms_fwd(h, sp["g1"])
        now = [AB, SB, OUT, GU] if l == 0 else [DOWN]
        proj = fetched(l, now, _in_proj(xn, gathered[l][IN_], sp["b_in"], trig, dm, carry=fetch(l, now)))
        w_in_g, w_ab_g, w_sb_g, w_out_g = weights(l)[:4]
        y_attn, lse = _attn_fwd(proj, sp["sink"], dm)
        y_sgu = _sgu_fwd(proj, sp["w_tril"], sp["b_t"], sp["ln_g"], sp["ln_b"], dm)
        a_attn = _branch_attn(y_attn, w_ab_g, dm)
        a_sgu, merged = _branch_sgu_merge(y_sgu, w_sb_g, a_attn, proj, dm)
        if l == 0:
            h_mid = fetched(l, [DOWN], _residual_matmul("out_proj", merged, w_out_g, h, carry=fetch(l, [DOWN])))
        else:
            h_mid = _residual_matmul("out_proj", merged, w_out_g, h)
        w_gu_g, w_down_g = weights(l)[4:]
        hn = _rms_fwd(h_mid, sp["g2"])
        ahead = [IN_, AB, SB, OUT]
        gu, act = fetched(l + 1, ahead, _gate_up(hn, w_gu_g, dm, carry=fetch(l + 1, ahead)))
        h_out = fetched(l + 1, [GU], _residual_matmul("down_proj", act, w_down_g, h_mid, carry=fetch(l + 1, [GU])))
        saved.append(dict(h=h, xn=xn, proj=proj, y_attn=y_attn, lse=lse, y_sgu=y_sgu, a_attn=a_attn, a_sgu=a_sgu,
                          merged=merged, h_mid=h_mid, hn=hn, gu=gu, act=act))
        h = h_out

    dh, dh_b, d_final, loss_part = _loss_head(h, _row(final_g), target)

    small_grads = [None] * L
    reduced = [[None] * len(big) for _ in range(L)]
    early, mid, late = [GU, DOWN], [AB, SB, OUT], [IN_]

    def riding(has_carry, res):
        return res if has_carry else (res, None)

    def sends_of(grads, land):
        return [_pair_sum(g, d) for g, d in zip(grads, land)]

    def finished(grads, land, got):
        return [_chip_sum(g, d, p) for g, d, p in zip(grads, land, got)]

    def file_reduced(layer, idx, fulls):
        for t, f in zip(idx, fulls):
            reduced[layer][t] = f

    late_grads = None
    mid_fulls = None
    n_late, n_mid, n_early = len(late), len(mid), len(early)
    for l in reversed(range(L)):
        w_in_g, w_ab_g, w_sb_g, w_out_g, w_gu_g, w_down_g = weights(l)
        sp, sv = small(l), saved[l]
        have = late_grads is not None
        dgu, rode = riding(have, _down_bwd(
            dh_b, w_down_g, sv["gu"], dm,
            carry=_sibling_exchange(late_grads).beside(_sibling_share(mid_fulls)) if have else None))
        if have:
            land = rode[:n_late]
            file_reduced(l + 1, mid, rode[n_late:])
        g_down = _wgrad_rows("wgrad_down", sv["act"], dh_b)
        dhn, got = riding(have, _gate_up_bwd(dgu, w_gu_g, dm,
                                             carry=_chip_exchange(sends_of(late_grads, land)) if have else None))
        g_gu, shared = riding(have, _wgrad_gate_up(sv["hn"], dgu, dm,
                                                   carry=_sibling_share(finished(late_grads, land, got)) if have else None))
        if have:
            file_reduced(l + 1, late, shared)
        dh_mid, dh_mid_b, d_g2 = _rms_bwd(dhn, sv["h_mid"], sp["g2"], dh)
        early_grads = [g_gu, g_down]
        (da_attn, da_sgu, dgate), land_e = _out_bwd(dh_mid_b, w_out_g, sv["proj"], sv["a_attn"], sv["a_sgu"], dm,
                                                     carry=_sibling_exchange(early_grads))
        sends_e = sends_of(early_grads, land_e)
        g_out = _wgrad_rows("wgrad_out", sv["merged"], dh_mid_b)
        dy_attn = _colsharded_bwd("branch_attn_bwd", da_attn, w_ab_g, BF16)
        dy_sgu = _colsharded_bwd("branch_sgu_bwd", da_sgu, w_sb_g, BF16)
        g_ab = _wgrad_cols("wgrad_attn_branch", sv["y_attn"], da_attn)
        g_sb = _wgrad_cols("wgrad_sgu_branch", sv["y_sgu"], da_sgu)
        mid_grads = [g_ab, g_sb, g_out]
        dq, dk, dv, d_sink = _attn_bwd(sv["proj"], trig, sp["sink"], sv["y_attn"], sv["lse"], dy_attn, dm)
        dz, d_sgu_w, d_bt, d_lng, d_lnb = _sgu_bwd(sv["proj"], sp["w_tril"], sp["b_t"], sp["ln_g"], sp["ln_b"], dy_sgu, dm)
        dproj = jnp.concatenate([dq, dk, dv, dz, dgate[0], dgate[1]], axis=1)
        dxn, rode = _colsharded_bwd("in_proj_bwd", dproj, w_in_g, F32,
                                    carry=_chip_exchange(sends_e).beside(_sibling_exchange(mid_grads)))
        got_e, land_m = rode[:n_early], rode[n_early:]
        (g_in, d_bin), rode = _wgrad_cols(
            "wgrad_in", sv["xn"], dproj, colsum=True,
            carry=_sibling_share(finished(early_grads, land_e, got_e)).beside(_chip_exchange(sends_of(mid_grads, land_m))))
        file_reduced(l, early, rode[:n_early])
        mid_fulls = finished(mid_grads, land_m, rode[n_early:])
        dh, dh_b, d_g1 = _rms_bwd(dxn, sv["h"], sp["g1"], dh_mid)
        late_grads = [g_in]
        small_grads[l] = dict(norm1_g=d_g1[0], b_in=d_bin[0, :, 0, :].reshape(-1), sinks=d_sink[0, :NQ],
                              sgu_ln_g=d_lng[0], sgu_ln_b=d_lnb[0], sgu_w=d_sgu_w, sgu_b=d_bt[:, :NG].T, norm2_g=d_g2[0])
    grad_x = dh[None]

    land = _sibling_exchange(late_grads).run()
    got = _chip_exchange(sends_of(late_grads, land)).run()
    shared = _sibling_share(finished(late_grads, land, got) + mid_fulls).run()
    file_reduced(0, late, shared[:n_late])
    file_reduced(0, mid, shared[n_late:])
    big_out = [_adamw_stacked([reduced[l][t] for l in range(L)], big[t], big_m[t], big_v[t]) for t in range(len(big))]

    names = ["norm1_g", "b_in", "sinks", "sgu_ln_g", "sgu_ln_b", "sgu_w", "sgu_b", "norm2_g"]
    small_w = [norm1_g, b_in, sinks, sgu_ln_g, sgu_ln_b, sgu_w, sgu_b, norm2_g, final_g]
    small_m = [m_norm1_g, m_b_in, m_sinks, m_sgu_ln_g, m_sgu_ln_b, m_sgu_w, m_sgu_b, m_norm2_g, m_final_g]
    small_v = [v_norm1_g, v_b_in, v_sinks, v_sgu_ln_g, v_sgu_ln_b, v_sgu_w, v_sgu_b, v_norm2_g, v_final_g]
    small_g = [jnp.stack([small_grads[l][nm] for l in range(L)]) for nm in names] + [d_final[0]]
    sizes = [w.size for w in small_w]
    total = sum(sizes) + 1
    rows = -(-total // (512 * LANES)) * 512
    loss_piece = jnp.sum(loss_part[0]).reshape(1)
    packed_g = _pack(small_g + [loss_piece], rows)
    one = jnp.ones((1,), F32)
    parts = _gather_all(packed_g)
    outs = _adamw_small(parts, _pack(small_w + [one], rows), _pack(small_m + [one], rows), _pack(small_v + [one], rows))

    def unpack(p):
        flat = p.reshape(-1)
        res, off = [], 0
        for w, n in zip(small_w, sizes):
            res.append(flat[off:off + n].reshape(w.shape))
            off += n
        return res, flat[off]

    (sg, loss), (sd, _), (smm, _), (svv, _) = [unpack(o) for o in outs]

    order = ["norm1_g", "w_in", "b_in", "sinks", "sgu_ln_g", "sgu_ln_b", "sgu_w", "sgu_b", "w_attn_branch",
             "w_sgu_branch", "w_out", "norm2_g", "w_gate_up", "w_down", "final_g"]
    big_names = ["w_in", "w_attn_branch", "w_sgu_branch", "w_out", "w_gate_up", "w_down"]
    small_names = names + ["final_g"]

    def collect(kind):
        res = []
        for nm in order:
            if nm in big_names:
                res.append(big_out[big_names.index(nm)][kind])
            else:
                res.append((sg, sd, smm, svv)[kind][small_names.index(nm)])
        return res

    return (loss, grad_x, *collect(0), *collect(1), *collect(2), *collect(3))
```

```python
import math

import jax
import jax.numpy as jnp
from jax import lax
from jax.experimental import pallas as pl
from jax.experimental.pallas import tpu as pltpu

F32 = jnp.float32
BF16 = jnp.bfloat16
MESH = pl.DeviceIdType.MESH
ANY = pl.BlockSpec(memory_space=pl.ANY)

HEAD_DIM = 64
N_KV_HEADS = 4
WINDOW = 128
ROPE_DIM = HEAD_DIM // 4
ROPE_THETA = 500000.0
EPS = 1e-5
NEG = -1e30
N_CHIPS = 4
LANES = 128
V7X_VMEM_LIMIT = 56 * 1024 * 1024

ADAM_LR = 0.001
ADAM_B1 = 0.9
ADAM_B2 = 0.999
ADAM_EPS = 1e-08
ADAM_WD = 0.01
ADAM_STEP = 10

NN = (((1,), (0,)), ((), ()))
NT = (((1,), (1,)), ((), ()))
TN = (((0,), (0,)), ((), ()))


ROW_TILES = (1024, 512, 256, 128, 64, 32, 16, 8)
BLOCK_BYTES = 2 * 1024 * 1024


def _pick(n, prefs):
    for p in prefs:
        if n % p == 0:
            return p
    raise ValueError(f"no tile for {n} among {prefs}")


def _row_tile(rows, cols, itemsize=4):
    return _pick(rows, [t for t in ROW_TILES if t * cols * itemsize <= BLOCK_BYTES or t == ROW_TILES[-1]])


def _dot(a, b, dims):
    return lax.dot_general(a, b, dims, preferred_element_type=F32)


def _sigmoid(x):
    return 0.5 * jnp.tanh(0.5 * x) + 0.5


def _gelu(x):
    return 0.5 * x * (1.0 + lax.erf(x * (1.0 / math.sqrt(2.0))))


def _gelu_grad(x):
    return 0.5 * (1.0 + lax.erf(x * (1.0 / math.sqrt(2.0)))) + x * jnp.exp(-0.5 * x * x) * (1.0 / math.sqrt(2.0 * math.pi))


def _params(sem):
    return pltpu.CompilerParams(dimension_semantics=sem, vmem_limit_bytes=V7X_VMEM_LIMIT)


def _matmul(name, lhs, rhs_list, *, dims, grid, lhs_spec, rhs_specs, acc_shape, out_shape, out_specs,
            epilogue, extra=(), extra_specs=(), carry=None, rhs_colsum=False, cols_outer=False, extra_aliases=None):
    if cols_outer:
        swap = lambda s: s if s.index_map is None else pl.BlockSpec(s.block_shape, lambda j, i, k, f=s.index_map: f(i, j, k))
        grid = (grid[1], grid[0], grid[2])
        lhs_spec, rhs_specs = swap(lhs_spec), [swap(s) for s in rhs_specs]
        extra_specs, out_specs = [swap(s) for s in extra_specs], [swap(s) for s in out_specs]
    gk = grid[2]
    nr, ne, no = len(rhs_list), len(extra), len(out_shape)
    nci = len(carry.ins) if carry else 0
    nco = len(carry.outs) if carry else 0
    acc_shapes = [acc_shape] * nr + ([(8, acc_shape[1])] if rhs_colsum else [])
    nacc = len(acc_shapes) if gk > 1 else 0

    def body(*refs):
        a_ref = refs[0]
        b_refs = refs[1:1 + nr]
        e_refs = refs[1 + nr:1 + nr + ne]
        base = 1 + nr + ne
        ci_refs = refs[base:base + nci]
        o_refs = refs[base + nci:base + nci + no]
        co_refs = refs[base + nci + no:base + nci + no + nco]
        acc_refs = refs[base + nci + no + nco:base + nci + no + nco + nacc]
        sems = refs[base + nci + no + nco + nacc:]
        ids = [pl.program_id(d) for d in range(3)]
        if carry:
            @pl.when((ids[0] == 0) & (ids[1] == 0) & (ids[2] == 0))
            def _():
                carry.start(ci_refs, co_refs, *sems)

        a = a_ref[...]
        if gk == 1:
            n_axis = 1 - dims[0][1][0]
            for cols in _col_chunks(acc_shape[1]):
                pick = (slice(None), cols) if n_axis == 1 else (cols, slice(None))
                parts = [_dot(a, b[pick], dims) for b in b_refs]
                if rhs_colsum:
                    b0 = b_refs[0][pick]
                    parts.append(_dot(jnp.ones((8, b0.shape[0]), b0.dtype), b0, NN))
                epilogue(parts, e_refs, o_refs, cols)
        else:
            k = ids[2]

            @pl.when(k == 0)
            def _():
                for acc in acc_refs:
                    acc[...] = jnp.zeros_like(acc)

            for acc, b in zip(acc_refs, b_refs):
                acc[...] += _dot(a, b[...], dims)
            if rhs_colsum:
                b0 = b_refs[0][...]
                acc_refs[-1][...] += _dot(jnp.ones((8, b0.shape[0]), b0.dtype), b0, NN)

            @pl.when(k == gk - 1)
            def _():
                epilogue([acc[...] for acc in acc_refs], e_refs, o_refs, slice(None))

        if carry:
            @pl.when((ids[0] == grid[0] - 1) & (ids[1] == grid[1] - 1) & (ids[2] == grid[2] - 1))
            def _():
                carry.finish(ci_refs, co_refs, *sems)

    scratch = [pltpu.VMEM(s, F32) for s in acc_shapes[:nacc]]
    kwargs = {}
    aliases = {1 + nr + e: o for e, o in (extra_aliases or {}).items()}
    if carry:
        scratch += carry.sem_scratch()
        aliases.update({1 + nr + ne + i: no + o for i, o in carry.aliases.items()})
    if aliases:
        kwargs["input_output_aliases"] = aliases
    outs = pl.pallas_call(
        body, name=name, grid=grid,
        in_specs=[lhs_spec, *rhs_specs, *extra_specs, *([ANY] * nci)],
        out_specs=[*out_specs, *([ANY] * nco)],
        out_shape=[*out_shape, *(carry.outs if carry else [])], scratch_shapes=scratch,
        compiler_params=_params(("arbitrary",) * 3 if carry else ("parallel", "parallel", "arbitrary")),
        **kwargs,
    )(lhs, *rhs_list, *extra, *(carry.ins if carry else []))
    return outs


class _Comm:
    def __init__(self, name, ins, outs, aliases, n_sems, start, finish):
        self.name, self.ins, self.outs, self.aliases, self.n_sems = name, list(ins), list(outs), dict(aliases), n_sems
        self.start, self.finish = start, finish

    def sem_scratch(self):
        return [pltpu.SemaphoreType.DMA((self.n_sems,)), pltpu.SemaphoreType.DMA((self.n_sems,))]

    def beside(self, other):
        ni, no, ns = len(self.ins), len(self.outs), self.n_sems

        def split(ins, outs, send_sems, recv_sems):
            mine = (ins[:ni], outs[:no], send_sems.at[pl.ds(0, ns)], recv_sems.at[pl.ds(0, ns)])
            theirs = (ins[ni:], outs[no:], send_sems.at[pl.ds(ns, other.n_sems)], recv_sems.at[pl.ds(ns, other.n_sems)])
            return mine, theirs

        def start(*refs):
            mine, theirs = split(*refs)
            self.start(*mine)
            other.start(*theirs)

        def finish(*refs):
            mine, theirs = split(*refs)
            self.finish(*mine)
            other.finish(*theirs)

        aliases = {**self.aliases, **{ni + i: no + o for i, o in other.aliases.items()}}
        return _Comm(self.name + "+" + other.name, self.ins + other.ins, self.outs + other.outs, aliases,
                     ns + other.n_sems, start, finish)

    def run(self):
        ni = len(self.ins)

        def body(*refs):
            in_refs, out_refs, sems = refs[:ni], refs[ni:ni + len(self.outs)], refs[ni + len(self.outs):]
            self.start(in_refs, out_refs, *sems)
            self.finish(in_refs, out_refs, *sems)

        return pl.pallas_call(
            body, name=self.name, in_specs=[ANY] * ni, out_specs=[ANY] * len(self.outs), out_shape=self.outs,
            input_output_aliases=self.aliases, scratch_shapes=self.sem_scratch(),
        )(*self.ins)


MXU_CHUNK = 256


def _col_chunks(n):
    if n % LANES:
        return [slice(0, n)]
    return [slice(s, min(s + MXU_CHUNK, n)) for s in range(0, n, MXU_CHUNK)]


def _store_epilogue(dtype):
    def ep(parts, e_refs, o_refs, cols):
        o_refs[0][:, cols] = parts[0].astype(dtype)
    return ep


def _rms_fwd(h, g_row):
    S, D = h.shape
    tm = _row_tile(S, D)

    def body(h_ref, g_ref, o_ref):
        x = h_ref[...]
        r = lax.rsqrt(jnp.mean(x * x, axis=-1, keepdims=True) + EPS)
        o_ref[...] = (x * r * g_ref[...]).astype(BF16)

    return pl.pallas_call(
        body, name="rms_fwd", grid=(S // tm,),
        in_specs=[pl.BlockSpec((tm, D), lambda i: (i, 0)), pl.BlockSpec((1, D), lambda i: (0, 0))],
        out_specs=pl.BlockSpec((tm, D), lambda i: (i, 0)),
        out_shape=jax.ShapeDtypeStruct((S, D), BF16),
        compiler_params=_params(("parallel",)),
    )(h, g_row)


def _rms_bwd(dy, h, g_row, dres):
    S, D = h.shape
    tm = _row_tile(S, D)

    def body(dy_ref, h_ref, g_ref, dres_ref, dh_ref, dhb_ref, dg_ref):
        i = pl.program_id(0)
        x = h_ref[...]
        d = dy_ref[...]
        r = lax.rsqrt(jnp.mean(x * x, axis=-1, keepdims=True) + EPS)
        dg = d * g_ref[...]
        dot = jnp.mean(dg * x, axis=-1, keepdims=True)
        dh = dres_ref[...] + r * dg - x * (r * r * r) * dot
        dh_ref[...] = dh
        dhb_ref[...] = dh.astype(BF16)
        part = jnp.sum(d * x * r, axis=0, keepdims=True)

        @pl.when(i == 0)
        def _():
            dg_ref[...] = jnp.zeros_like(dg_ref)

        dg_ref[0:1, :] += part

    return pl.pallas_call(
        body, name="rms_bwd", grid=(S // tm,),
        in_specs=[pl.BlockSpec((tm, D), lambda i: (i, 0)), pl.BlockSpec((tm, D), lambda i: (i, 0)),
                  pl.BlockSpec((1, D), lambda i: (0, 0)), pl.BlockSpec((tm, D), lambda i: (i, 0))],
        out_specs=[pl.BlockSpec((tm, D), lambda i: (i, 0)), pl.BlockSpec((tm, D), lambda i: (i, 0)),
                   pl.BlockSpec((8, D), lambda i: (0, 0))],
        out_shape=[jax.ShapeDtypeStruct((S, D), F32), jax.ShapeDtypeStruct((S, D), BF16),
                   jax.ShapeDtypeStruct((8, D), F32)],
        compiler_params=_params(("arbitrary",)),
    )(dy, h, g_row, dres)


def _loss_head(h, g_row, target):
    S, D = h.shape
    tm = _row_tile(S, D)

    def body(h_ref, g_ref, t_ref, dh_ref, dhb_ref, dg_ref, loss_ref):
        i = pl.program_id(0)
        x = h_ref[...]
        g = g_ref[...]
        r = lax.rsqrt(jnp.mean(x * x, axis=-1, keepdims=True) + EPS)
        y = x * r * g
        e = y - t_ref[...]
        d = e * (1.0 / D)
        dg = d * g
        dot = jnp.mean(dg * x, axis=-1, keepdims=True)
        dh = r * dg - x * (r * r * r) * dot
        dh_ref[...] = dh
        dhb_ref[...] = dh.astype(BF16)

        @pl.when(i == 0)
        def _():
            dg_ref[...] = jnp.zeros_like(dg_ref)
            loss_ref[...] = jnp.zeros_like(loss_ref)

        dg_ref[0:1, :] += jnp.sum(d * x * r, axis=0, keepdims=True)
        loss_ref[0:1, :] += jnp.sum((0.5 / D) * e * e, axis=0, keepdims=True)

    return pl.pallas_call(
        body, name="loss_head", grid=(S // tm,),
        in_specs=[pl.BlockSpec((tm, D), lambda i: (i, 0)), pl.BlockSpec((1, D), lambda i: (0, 0)),
                  pl.BlockSpec((tm, D), lambda i: (i, 0))],
        out_specs=[pl.BlockSpec((tm, D), lambda i: (i, 0)), pl.BlockSpec((tm, D), lambda i: (i, 0)),
                   pl.BlockSpec((8, D), lambda i: (0, 0)), pl.BlockSpec((8, D), lambda i: (0, 0))],
        out_shape=[jax.ShapeDtypeStruct((S, D), F32), jax.ShapeDtypeStruct((S, D), BF16),
                   jax.ShapeDtypeStruct((8, D), F32), jax.ShapeDtypeStruct((8, D), F32)],
        compiler_params=_params(("arbitrary",)),
    )(h, g_row, target)


def _rotary_partner(t):
    half = ROPE_DIM // 2
    if t.shape[-1] == LANES:
        lane = lax.broadcasted_iota(jnp.int32, t.shape, 1) & (HEAD_DIM - 1)
        return jnp.where(lane < half, pltpu.roll(t, LANES - half, 1), pltpu.roll(t, half, 1))
    r = lax.broadcasted_iota(jnp.int32, (LANES, LANES), 0)
    c = lax.broadcasted_iota(jnp.int32, (LANES, LANES), 1)
    cm = c & (HEAD_DIM - 1)
    perm = (((cm < half) & (r == c + half)) | ((cm >= half) & (cm < ROPE_DIM) & (r == c - half))).astype(BF16)
    hi = t.astype(BF16)
    lo = (t - hi.astype(F32)).astype(BF16)
    cols = [slice(s, s + LANES) for s in range(0, t.shape[-1], LANES)]
    return jnp.concatenate([_dot(hi[:, c_], perm, NN) + _dot(lo[:, c_], perm, NN) for c_ in cols], axis=1)


def _rope(t, cos, sin):
    return t * cos + _rotary_partner(t) * sin


def _rope_t(g, cos, sin):
    return g * cos + _rotary_partner(g * sin)


def _band_mask(n, qpk):
    qi = lax.broadcasted_iota(jnp.int32, (qpk * WINDOW, 2 * WINDOW), 0) & (WINDOW - 1)
    kj = lax.broadcasted_iota(jnp.int32, (qpk * WINDOW, 2 * WINDOW), 1)
    rel = qi + WINDOW - kj
    ok = (rel >= 0) & (rel < WINDOW)
    return ok & ((kj >= WINDOW) | (n > 0))


def _stack_heads(x, g, qpk):
    return jnp.concatenate([x[:, (g * qpk + hh) * HEAD_DIM:(g * qpk + hh + 1) * HEAD_DIM] for hh in range(qpk)], axis=0)


def _stack_cols(row, g, qpk):
    return jnp.concatenate([row[:, g * qpk + hh:g * qpk + hh + 1] for hh in range(qpk)], axis=0)


def _attn_specs(dm, nb):
    A, KV = dm["A"], dm["KV"]
    kb, vb = dm["OFF_K"] // KV, dm["OFF_V"] // KV
    cur = lambda n: jnp.minimum(n, nb - 1)
    prev = lambda n: jnp.maximum(jnp.minimum(n, nb - 1) - 1, 0)
    proj_specs = [
        pl.BlockSpec((WINDOW, A), lambda n: (cur(n), 0)),
        pl.BlockSpec((WINDOW, KV), lambda n: (prev(n), kb)),
        pl.BlockSpec((WINDOW, KV), lambda n: (cur(n), kb)),
        pl.BlockSpec((WINDOW, KV), lambda n: (prev(n), vb)),
        pl.BlockSpec((WINDOW, KV), lambda n: (cur(n), vb)),
    ]
    trig_cur = [pl.BlockSpec((WINDOW, LANES), lambda n: (cur(n), 0)) for _ in range(2)]
    trig_prev = [pl.BlockSpec((WINDOW, LANES), lambda n: (prev(n), 0)) for _ in range(2)]
    return proj_specs, trig_cur, trig_prev, cur, prev


def _attn_fwd(proj, sink_row, dm):
    S = proj.shape[0]
    A, KV, NQ = dm["A"], dm["KV"], dm["NQ"]
    qpk = NQ // N_KV_HEADS
    nb = S // WINDOW
    scale = HEAD_DIM ** -0.5
    proj_specs = _attn_specs(dm, nb)[0]

    def body(q_ref, kp_ref, kc_ref, vp_ref, vc_ref, sink_ref, y_ref, lse_ref):
        n = pl.program_id(0)
        qr = q_ref[...]
        kr = jnp.concatenate([kp_ref[...], kc_ref[...]], axis=0)
        vband = jnp.concatenate([vp_ref[...], vc_ref[...]], axis=0)
        mask = _band_mask(n, qpk)
        lane = lax.broadcasted_iota(jnp.int32, (WINDOW, LANES), 1)
        lse_all = jnp.zeros((WINDOW, LANES), F32)
        sink_rows = jnp.broadcast_to(sink_ref[0:1, :], (WINDOW, LANES))
        groups = range(N_KV_HEADS)
        head = lambda x, g: x[:, g * HEAD_DIM:(g + 1) * HEAD_DIM]
        ones = jnp.ones((2 * WINDOW, HEAD_DIM), BF16)
        sink = [_stack_cols(sink_rows, g, qpk) for g in groups]
        s = [jnp.where(mask, _dot(_stack_heads(qr, g, qpk), head(kr, g), NT) * scale, NEG) for g in groups]
        m = [jnp.maximum(jnp.max(s[g], axis=-1, keepdims=True), sink[g]) for g in groups]
        p = [jnp.exp(s[g] - m[g]).astype(BF16) for g in groups]
        ov = [_dot(p[g], jnp.concatenate([head(vband, g), ones], axis=1), NN) for g in groups]
        den = [ov[g][:, HEAD_DIM:HEAD_DIM + 1] + jnp.exp(sink[g] - m[g]) for g in groups]
        o = [ov[g][:, :HEAD_DIM] * (1.0 / den[g]) for g in groups]
        lse = [m[g] + jnp.log(den[g]) for g in groups]
        for g in groups:
            for hh in range(qpk):
                h = g * qpk + hh
                rows = slice(hh * WINDOW, (hh + 1) * WINDOW)
                y_ref[:, h * HEAD_DIM:(h + 1) * HEAD_DIM] = o[g][rows].astype(BF16)
                lse_all = jnp.where(lane == h, lse[g][rows], lse_all)
        lse_ref[...] = lse_all

    return pl.pallas_call(
        body, name="attn_fwd", grid=(nb,),
        in_specs=[*proj_specs, pl.BlockSpec((8, LANES), lambda n: (0, 0))],
        out_specs=[pl.BlockSpec((WINDOW, A), lambda n: (n, 0)), pl.BlockSpec((WINDOW, LANES), lambda n: (n, 0))],
        out_shape=[jax.ShapeDtypeStruct((S, A), BF16), jax.ShapeDtypeStruct((S, LANES), F32)],
        compiler_params=_params(("parallel",)),
    )(proj, proj, proj, proj, proj, sink_row)


def _attn_bwd(proj, trig, sink_row, y, lse, dy, dm):
    S = proj.shape[0]
    A, KV, NQ = dm["A"], dm["KV"], dm["NQ"]
    qpk = NQ // N_KV_HEADS
    nb = S // WINDOW
    scale = HEAD_DIM ** -0.5
    proj_specs, trig_cur, trig_prev, cur, prev = _attn_specs(dm, nb)

    def body(q_ref, kp_ref, kc_ref, vp_ref, vc_ref, cc_ref, sc_ref, cp_ref, sp_ref,
             sink_ref, y_ref, lse_ref, dy_ref, dq_ref, dk_ref, dv_ref, dsink_ref,
             ck_ref, cv_ref, bk_ref, bv_ref, dqr_ref):
        n = pl.program_id(0)

        @pl.when(n == 0)
        def _():
            dsink_ref[...] = jnp.zeros_like(dsink_ref)
            ck_ref[...] = jnp.zeros_like(ck_ref)
            cv_ref[...] = jnp.zeros_like(cv_ref)

        @pl.when(n < nb)
        def _():
            tq = lambda r: jnp.tile(r[...], (1, A // LANES))
            tk = lambda rp, rc: jnp.tile(jnp.concatenate([rp[...], rc[...]], axis=0), (1, KV // LANES))
            cq, sq = tq(cc_ref), tq(sc_ref)
            ck, sk = tk(cp_ref, cc_ref), tk(sp_ref, sc_ref)
            qr = q_ref[...]
            kr = jnp.concatenate([kp_ref[...], kc_ref[...]], axis=0)
            vband = jnp.concatenate([vp_ref[...], vc_ref[...]], axis=0)
            mask = _band_mask(n, qpk)
            lane = lax.broadcasted_iota(jnp.int32, (1, LANES), 1)
            lse_all = lse_ref[...]
            sink_rows = jnp.broadcast_to(sink_ref[0:1, :], (WINDOW, LANES))
            dy_all = dy_ref[...]
            y_all = y_ref[...]
            dsink = jnp.zeros((1, LANES), F32)
            groups = range(N_KV_HEADS)
            head = lambda x, g: x[:, g * HEAD_DIM:(g + 1) * HEAD_DIM]
            q = [_stack_heads(qr, g, qpk) for g in groups]
            dy = [_stack_heads(dy_all, g, qpk) for g in groups]
            lse = [_stack_cols(lse_all, g, qpk) for g in groups]
            s = [jnp.where(mask, _dot(q[g], head(kr, g), NT) * scale, NEG) for g in groups]
            dp = [_dot(dy[g], head(vband, g), NT) for g in groups]
            delta = [jnp.sum(dy[g].astype(F32) * _stack_heads(y_all, g, qpk).astype(F32), axis=-1, keepdims=True)
                     for g in groups]
            p = [jnp.exp(s[g] - lse[g]) for g in groups]
            ds = [(p[g] * (dp[g] - delta[g]) * scale).astype(BF16) for g in groups]
            dq = [_dot(ds[g], head(kr, g), NN) for g in groups]
            for g in groups:
                bk_ref[:, g * HEAD_DIM:(g + 1) * HEAD_DIM] = _dot(ds[g], q[g], TN)
                bv_ref[:, g * HEAD_DIM:(g + 1) * HEAD_DIM] = _dot(p[g].astype(BF16), dy[g], TN)
            for g in groups:
                sink_d = jnp.exp(_stack_cols(sink_rows, g, qpk) - lse[g]) * delta[g]
                for hh in range(qpk):
                    h = g * qpk + hh
                    rows = slice(hh * WINDOW, (hh + 1) * WINDOW)
                    dqr_ref[:, h * HEAD_DIM:(h + 1) * HEAD_DIM] = dq[g][rows]
                    dsink = dsink + jnp.where(lane == h, -jnp.sum(sink_d[rows], axis=0, keepdims=True), 0.0)
            dsink_ref[0:1, :] += dsink
            dq_ref[...] = _rope_t(dqr_ref[...], cq, sq).astype(BF16)
            dkb = _rope_t(bk_ref[...], ck, sk)
            dvb = bv_ref[...]
            dk_ref[...] = (ck_ref[...] + dkb[:WINDOW]).astype(BF16)
            dv_ref[...] = (cv_ref[...] + dvb[:WINDOW]).astype(BF16)
            ck_ref[...] = dkb[WINDOW:]
            cv_ref[...] = dvb[WINDOW:]

        @pl.when(n == nb)
        def _():
            dk_ref[...] = ck_ref[...].astype(BF16)
            dv_ref[...] = cv_ref[...].astype(BF16)

    row = lambda w: pl.BlockSpec((WINDOW, w), lambda n: (cur(n), 0))
    done = lambda w: pl.BlockSpec((WINDOW, w), lambda n: (jnp.maximum(n - 1, 0), 0))
    return pl.pallas_call(
        body, name="attn_bwd", grid=(nb + 1,),
        in_specs=[*proj_specs, *trig_cur, *trig_prev, pl.BlockSpec((8, LANES), lambda n: (0, 0)),
                  row(A), row(LANES), row(A)],
        out_specs=[row(A), done(KV), done(KV), pl.BlockSpec((8, LANES), lambda n: (0, 0))],
        out_shape=[jax.ShapeDtypeStruct((S, A), BF16), jax.ShapeDtypeStruct((S, KV), BF16),
                   jax.ShapeDtypeStruct((S, KV), BF16), jax.ShapeDtypeStruct((8, LANES), F32)],
        scratch_shapes=[pltpu.VMEM((WINDOW, KV), F32), pltpu.VMEM((WINDOW, KV), F32),
                        pltpu.VMEM((2 * WINDOW, KV), F32), pltpu.VMEM((2 * WINDOW, KV), F32),
                        pltpu.VMEM((WINDOW, A), F32)],
        compiler_params=_params(("arbitrary",)),
    )(proj, proj, proj, proj, proj, *trig, *trig, sink_row, y, lse, dy)


def _sgu_layout(dm, S):
    G = dm["G"]
    pw = math.gcd(dm["OFF_Z"], G)
    npc = G // pw
    tm = _pick(S, (256, 128))
    u_specs = [pl.BlockSpec((tm, pw), lambda i, p=p: (i, dm["OFF_Z"] // pw + p)) for p in range(npc)]
    v_specs = [pl.BlockSpec((tm, pw), lambda i, p=p: (i, (dm["OFF_Z"] + G) // pw + p)) for p in range(npc)]
    return pw, npc, tm, u_specs, v_specs


def _sgu_norm(v_refs, lg_ref, lb_ref):
    v = jnp.concatenate([_gelu(r[...].astype(F32)) for r in v_refs], axis=1)
    mu = jnp.mean(v, axis=-1, keepdims=True)
    vc = v - mu
    rstd = lax.rsqrt(jnp.mean(vc * vc, axis=-1, keepdims=True) + EPS)
    xhat = vc * rstd
    return xhat, rstd, (xhat * lg_ref[...] + lb_ref[...]).astype(BF16)


def _sgu_fwd(proj, w_tril, b_t, ln_g_row, ln_b_row, dm):
    S = proj.shape[0]
    G, NG = dm["G"], dm["NG"]
    pw, npc, tm, u_specs, v_specs = _sgu_layout(dm, S)
    nch = tm // WINDOW

    def body(*refs):
        u_refs, v_refs = refs[:npc], refs[npc:2 * npc]
        w_ref, bt_ref, lg_ref, lb_ref, y_ref = refs[2 * npc:]
        _, _, vn = _sgu_norm(v_refs, lg_ref, lb_ref)
        u = jnp.concatenate([_gelu(r[...].astype(F32)) for r in u_refs], axis=1)
        for c in range(nch):
            rows = slice(c * WINDOW, (c + 1) * WINDOW)
            for g in range(NG):
                cols = slice(g * LANES, (g + 1) * LANES)
                sv = _dot(w_ref[g], vn[rows, cols], NN) + bt_ref[:, g:g + 1]
                y_ref[rows, cols] = (u[rows, cols] * sv).astype(BF16)

    return pl.pallas_call(
        body, name="sgu_fwd", grid=(S // tm,),
        in_specs=[*u_specs, *v_specs,
                  pl.BlockSpec((NG, WINDOW, WINDOW), lambda i: (0, 0, 0)),
                  pl.BlockSpec((WINDOW, LANES), lambda i: (0, 0)),
                  pl.BlockSpec((1, G), lambda i: (0, 0)), pl.BlockSpec((1, G), lambda i: (0, 0))],
        out_specs=pl.BlockSpec((tm, G), lambda i: (i, 0)),
        out_shape=jax.ShapeDtypeStruct((S, G), BF16),
        compiler_params=_params(("parallel",)),
    )(*([proj] * (2 * npc)), w_tril, b_t, ln_g_row, ln_b_row)


def _sgu_bwd(proj, w_tril, b_t, ln_g_row, ln_b_row, dy, dm):
    S = proj.shape[0]
    G, NG = dm["G"], dm["NG"]
    pw, npc, tm, u_specs, v_specs = _sgu_layout(dm, S)
    nch = tm // WINDOW

    def body(*refs):
        u_refs, v_refs = refs[:npc], refs[npc:2 * npc]
        w_ref, bt_ref, lg_ref, lb_ref, dy_ref, dz_ref, dw_ref, dbt_ref, dlg_ref, dlb_ref, dvn_ref = refs[2 * npc:]
        i = pl.program_id(0)

        @pl.when(i == 0)
        def _():
            dw_ref[...] = jnp.zeros_like(dw_ref)
            dbt_ref[...] = jnp.zeros_like(dbt_ref)
            dlg_ref[...] = jnp.zeros_like(dlg_ref)
            dlb_ref[...] = jnp.zeros_like(dlb_ref)

        xhat, rstd, vn = _sgu_norm(v_refs, lg_ref, lb_ref)
        u_pre = jnp.concatenate([r[...].astype(F32) for r in u_refs], axis=1)
        u = _gelu(u_pre)
        dy = dy_ref[...].astype(F32)
        lane = lax.broadcasted_iota(jnp.int32, (WINDOW, LANES), 1)
        tri = lax.broadcasted_iota(jnp.int32, (WINDOW, WINDOW), 0) >= lax.broadcasted_iota(jnp.int32, (WINDOW, WINDOW), 1)
        dbt = jnp.zeros((WINDOW, LANES), F32)
        for c in range(nch):
            rows = slice(c * WINDOW, (c + 1) * WINDOW)
            for g in range(NG):
                cols = slice(g * LANES, (g + 1) * LANES)
                vn_cg = vn[rows, cols]
                sv = _dot(w_ref[g], vn_cg, NN) + bt_ref[:, g:g + 1]
                dy_cg = dy[rows, cols]
                dsv = dy_cg * u[rows, cols]
                dsv_b = dsv.astype(BF16)
                dz_ref[rows, cols] = (dy_cg * sv * _gelu_grad(u_pre[rows, cols])).astype(BF16)
                dvn_ref[rows, cols] = _dot(w_ref[g], dsv_b, TN)
                dw_ref[g] += jnp.where(tri, _dot(dsv_b, vn_cg, NT), 0.0)
                dbt = dbt + jnp.where(lane == g, jnp.sum(dsv, axis=-1, keepdims=True), 0.0)
        dbt_ref[...] += dbt
        dvn = dvn_ref[...]
        dlg_ref[0:1, :] += jnp.sum(dvn * xhat, axis=0, keepdims=True)
        dlb_ref[0:1, :] += jnp.sum(dvn, axis=0, keepdims=True)
        dxh = dvn * lg_ref[...]
        dv = rstd * (dxh - jnp.mean(dxh, axis=-1, keepdims=True) - xhat * jnp.mean(dxh * xhat, axis=-1, keepdims=True))
        v_pre = jnp.concatenate([r[...].astype(F32) for r in v_refs], axis=1)
        dz_ref[:, G:] = (dv * _gelu_grad(v_pre)).astype(BF16)

    return pl.pallas_call(
        body, name="sgu_bwd", grid=(S // tm,),
        in_specs=[*u_specs, *v_specs,
                  pl.BlockSpec((NG, WINDOW, WINDOW), lambda i: (0, 0, 0)),
                  pl.BlockSpec((WINDOW, LANES), lambda i: (0, 0)),
                  pl.BlockSpec((1, G), lambda i: (0, 0)), pl.BlockSpec((1, G), lambda i: (0, 0)),
                  pl.BlockSpec((tm, G), lambda i: (i, 0))],
        out_specs=[pl.BlockSpec((tm, 2 * G), lambda i: (i, 0)),
                   pl.BlockSpec((NG, WINDOW, WINDOW), lambda i: (0, 0, 0)),
                   pl.BlockSpec((WINDOW, LANES), lambda i: (0, 0)),
                   pl.BlockSpec((8, G), lambda i: (0, 0)), pl.BlockSpec((8, G), lambda i: (0, 0))],
        out_shape=[jax.ShapeDtypeStruct((S, 2 * G), BF16), jax.ShapeDtypeStruct((NG, WINDOW, WINDOW), F32),
                   jax.ShapeDtypeStruct((WINDOW, LANES), F32), jax.ShapeDtypeStruct((8, G), F32),
                   jax.ShapeDtypeStruct((8, G), F32)],
        scratch_shapes=[pltpu.VMEM((tm, G), F32)],
        compiler_params=_params(("arbitrary",)),
    )(*([proj] * (2 * npc)), w_tril, b_t, ln_g_row, ln_b_row, dy)


def _result(outs, n_main, carry):
    main = outs[0] if n_main == 1 else tuple(outs[:n_main])
    return (main, list(outs[n_main:])) if carry else main


def _in_proj(xn, w_in_g, b_row, trig, dm, carry=None):
    S, D = xn.shape
    IN = dm["IN"]
    cw = IN // N_CHIPS
    tm = _pick(S, (1024, 512, 256, 128))
    tn = _pick(cw, (1920, 640, 512, 256, 128))
    nbc = cw // tn
    rope_cols = dm["OFF_V"]
    rope_blocks = -(-rope_cols // tn)

    def roped_store(val, e_refs, o_refs, cols, jj):
        r = min(max(rope_cols - (jj * tn + cols.start), 0), cols.stop - cols.start)
        if not r:
            o_refs[0][:, cols] = val.astype(BF16)
            return
        for rows in [slice(s, s + WINDOW) for s in range(0, tm, WINDOW)]:
            cos, sin = e_refs[1][rows, :], e_refs[2][rows, :]
            for c0 in range(0, r, LANES):
                piece = val[rows, c0:c0 + LANES]
                o_refs[0][rows, cols.start + c0:cols.start + c0 + LANES] = _rope(piece, cos, sin).astype(BF16)
        if r < val.shape[1]:
            o_refs[0][:, cols.start + r:cols.stop] = val[:, r:].astype(BF16)

    def ep_rope(parts, e_refs, o_refs, cols):
        val = parts[0] + e_refs[0][:, cols]
        if rope_blocks == 1:
            roped_store(val, e_refs, o_refs, cols, 0)
        else:
            for jj in range(rope_blocks):
                @pl.when(pl.program_id(0) == jj)
                def _(jj=jj):
                    roped_store(val, e_refs, o_refs, cols, jj)

    def ep_plain(parts, e_refs, o_refs, cols):
        o_refs[0][:, cols] = (parts[0] + e_refs[0][:, cols]).astype(BF16)

    rows = pl.BlockSpec((tm, LANES), lambda i, j, k: (i, 0))
    first = IN // tn - rope_blocks
    common = dict(dims=NN, lhs_spec=pl.BlockSpec((tm, D), lambda i, j, k: (i, 0)), acc_shape=(tm, tn),
                  out_shape=[jax.ShapeDtypeStruct((S, IN), BF16)], cols_outer=True)
    proj = _matmul(
        "in_proj_qk", xn, [w_in_g], grid=(S // tm, rope_blocks, 1),
        rhs_specs=[pl.BlockSpec((None, D, tn), lambda i, j, k: (j // nbc, 0, j % nbc))],
        extra=[b_row, *trig], extra_specs=[pl.BlockSpec((1, tn), lambda i, j, k: (0, j)), rows, rows],
        out_specs=[pl.BlockSpec((tm, tn), lambda i, j, k: (i, j))], epilogue=ep_rope, **common)[0]
    jb = lambda j: j + rope_blocks
    return _result(_matmul(
        "in_proj", xn, [w_in_g], grid=(S // tm, first, 1),
        rhs_specs=[pl.BlockSpec((None, D, tn), lambda i, j, k: (jb(j) // nbc, 0, jb(j) % nbc))],
        extra=[b_row, proj], extra_specs=[pl.BlockSpec((1, tn), lambda i, j, k: (0, jb(j))), ANY], extra_aliases={1: 0},
        out_specs=[pl.BlockSpec((tm, tn), lambda i, j, k: (i, jb(j)))], epilogue=ep_plain, carry=carry, **common), 1, carry)


def _branch_attn(y_attn, w_ab_g, dm):
    S, A = y_attn.shape
    D = dm["D"]
    cw = D // N_CHIPS
    tm = _pick(S, (1024, 512, 256, 128))
    return _matmul(
        "branch_attn", y_attn, [w_ab_g], dims=NN, grid=(S // tm, N_CHIPS, 1),
        lhs_spec=pl.BlockSpec((tm, A), lambda i, j, k: (i, 0)),
        rhs_specs=[pl.BlockSpec((None, A, cw), lambda i, j, k: (j, 0, 0))],
        acc_shape=(tm, cw), out_shape=[jax.ShapeDtypeStruct((S, D), BF16)],
        out_specs=[pl.BlockSpec((tm, cw), lambda i, j, k: (i, j))], epilogue=_store_epilogue(BF16))[0]


def _branch_sgu_merge(y_sgu, w_sb_g, a_attn, proj, dm):
    S, G = y_sgu.shape
    D, OFF_G = dm["D"], dm["OFF_G"]
    cw = D // N_CHIPS
    tm = _pick(S, (1024, 512, 256, 128))

    def ep(parts, e_refs, o_refs, cols):
        a_sgu = parts[0].astype(BF16)
        ga = _sigmoid(e_refs[1][:, cols].astype(F32))
        gs = _sigmoid(e_refs[2][:, cols].astype(F32))
        o_refs[0][:, cols] = a_sgu
        o_refs[1][:, cols] = (ga * e_refs[0][:, cols].astype(F32) + gs * a_sgu.astype(F32)).astype(BF16)

    blk = pl.BlockSpec((tm, cw), lambda i, j, k: (i, j))
    return _matmul(
        "branch_sgu_merge", y_sgu, [w_sb_g], dims=NN, grid=(S // tm, N_CHIPS, 1),
        lhs_spec=pl.BlockSpec((tm, G), lambda i, j, k: (i, 0)),
        rhs_specs=[pl.BlockSpec((None, G, cw), lambda i, j, k: (j, 0, 0))],
        acc_shape=(tm, cw), extra=[a_attn, proj, proj],
        extra_specs=[blk, pl.BlockSpec((tm, cw), lambda i, j, k: (i, OFF_G // cw + j)),
                     pl.BlockSpec((tm, cw), lambda i, j, k: (i, (OFF_G + D) // cw + j))],
        out_shape=[jax.ShapeDtypeStruct((S, D), BF16), jax.ShapeDtypeStruct((S, D), BF16)],
        out_specs=[blk, blk], epilogue=ep)


def _residual_matmul(name, a, w_g, h, carry=None):
    S, K = a.shape
    D = w_g.shape[1]
    tm = _pick(S, (1024, 512, 256, 128))
    tn = _pick(D, (512, 256, 128))

    def ep(parts, e_refs, o_refs, cols):
        o_refs[0][:, cols] = e_refs[0][:, cols] + parts[0]

    blk = pl.BlockSpec((tm, tn), lambda i, j, k: (i, j))
    return _result(_matmul(
        name, a, [w_g], dims=NN, grid=(S // tm, D // tn, 1),
        lhs_spec=pl.BlockSpec((tm, K), lambda i, j, k: (i, 0)),
        rhs_specs=[pl.BlockSpec((K, tn), lambda i, j, k: (0, j))],
        acc_shape=(tm, tn), extra=[h], extra_specs=[blk],
        out_shape=[jax.ShapeDtypeStruct((S, D), F32)], out_specs=[blk], epilogue=ep, carry=carry), 1, carry)


def _gate_up(hn, w_gu_g, dm, carry=None):
    S, D = hn.shape
    Fd = dm["F"]
    cw = 2 * Fd // N_CHIPS
    tm = _pick(S, (512, 256, 128))
    tn = _pick(cw, (1408, 512, 384, 256, 128))
    nbc = cw // tn
    half = N_CHIPS // 2

    def ep(parts, e_refs, o_refs, cols):
        gate, up = parts[0].astype(BF16), parts[1].astype(BF16)
        o_refs[0][0, :, cols] = gate
        o_refs[0][1, :, cols] = up
        g32 = gate.astype(F32)
        o_refs[1][:, cols] = (g32 * _sigmoid(g32) * up.astype(F32)).astype(BF16)

    return _result(_matmul(
        "gate_up", hn, [w_gu_g, w_gu_g], dims=NN, grid=(S // tm, Fd // tn, 1),
        lhs_spec=pl.BlockSpec((tm, D), lambda i, j, k: (i, 0)),
        rhs_specs=[pl.BlockSpec((None, D, tn), lambda i, j, k: (j // nbc, 0, j % nbc)),
                   pl.BlockSpec((None, D, tn), lambda i, j, k: (half + j // nbc, 0, j % nbc))],
        acc_shape=(tm, tn),
        out_shape=[jax.ShapeDtypeStruct((2, S, Fd), BF16), jax.ShapeDtypeStruct((S, Fd), BF16)],
        out_specs=[pl.BlockSpec((2, tm, tn), lambda i, j, k: (0, i, j)), pl.BlockSpec((tm, tn), lambda i, j, k: (i, j))],
        epilogue=ep, carry=carry, cols_outer=True), 2, carry)


def _down_bwd(dh_b, w_down_g, gu, dm, carry=None):
    S, D = dh_b.shape
    Fd = dm["F"]
    tm = _pick(S, (2048, 1024, 512, 256, 128))
    tn = _pick(Fd, (512, 256, 128))

    def ep(parts, e_refs, o_refs, cols):
        gate = e_refs[0][0, :, cols].astype(F32)
        up = e_refs[0][1, :, cols].astype(F32)
        s = _sigmoid(gate)
        dact = parts[0]
        o_refs[0][0, :, cols] = (dact * up * s * (1.0 + gate * (1.0 - s))).astype(BF16)
        o_refs[0][1, :, cols] = (dact * gate * s).astype(BF16)

    blk = pl.BlockSpec((2, tm, tn), lambda i, j, k: (0, i, j))
    return _result(_matmul(
        "down_bwd", dh_b, [w_down_g], dims=NT, grid=(S // tm, Fd // tn, 1),
        lhs_spec=pl.BlockSpec((tm, D), lambda i, j, k: (i, 0)),
        rhs_specs=[pl.BlockSpec((tn, D), lambda i, j, k: (j, 0))],
        acc_shape=(tm, tn), extra=[gu], extra_specs=[blk],
        out_shape=[jax.ShapeDtypeStruct((2, S, Fd), BF16)], out_specs=[blk], epilogue=ep, carry=carry), 1, carry)


def _gate_up_bwd(dgu, w_gu_g, dm, carry=None):
    S = dgu.shape[1]
    D, Fd = dm["D"], dm["F"]
    cw = 2 * Fd // N_CHIPS
    half = N_CHIPS // 2
    tm = _pick(S, (1024, 512, 256, 128))
    tn = _pick(D, (1024, 512, 256, 128))
    return _result(_matmul(
        "gate_up_bwd", dgu, [w_gu_g], dims=NT, grid=(S // tm, D // tn, N_CHIPS),
        lhs_spec=pl.BlockSpec((None, tm, cw), lambda i, j, k: (k // half, i, k % half)),
        rhs_specs=[pl.BlockSpec((None, tn, cw), lambda i, j, k: (k, j, 0))],
        acc_shape=(tm, tn), out_shape=[jax.ShapeDtypeStruct((S, D), F32)],
        out_specs=[pl.BlockSpec((tm, tn), lambda i, j, k: (i, j))], epilogue=_store_epilogue(F32), carry=carry), 1, carry)


def _out_bwd(dh_b, w_out_g, proj, a_attn, a_sgu, dm, carry=None):
    S, D = dh_b.shape
    OFF_G = dm["OFF_G"]
    tm = _pick(S, (1024, 512, 256, 128))
    tn = D // N_CHIPS

    def ep(parts, e_refs, o_refs, cols):
        dm_ = parts[0]
        ga = _sigmoid(e_refs[0][:, cols].astype(F32))
        gs = _sigmoid(e_refs[1][:, cols].astype(F32))
        o_refs[0][:, cols] = (dm_ * ga).astype(BF16)
        o_refs[1][:, cols] = (dm_ * gs).astype(BF16)
        o_refs[2][0, :, cols] = (dm_ * e_refs[2][:, cols].astype(F32) * ga * (1.0 - ga)).astype(BF16)
        o_refs[2][1, :, cols] = (dm_ * e_refs[3][:, cols].astype(F32) * gs * (1.0 - gs)).astype(BF16)

    blk = pl.BlockSpec((tm, tn), lambda i, j, k: (i, j))
    return _result(_matmul(
        "out_bwd", dh_b, [w_out_g], dims=NT, grid=(S // tm, D // tn, 1),
        lhs_spec=pl.BlockSpec((tm, D), lambda i, j, k: (i, 0)),
        rhs_specs=[pl.BlockSpec((tn, D), lambda i, j, k: (j, 0))],
        acc_shape=(tm, tn), extra=[proj, proj, a_attn, a_sgu],
        extra_specs=[pl.BlockSpec((tm, tn), lambda i, j, k: (i, OFF_G // tn + j)),
                     pl.BlockSpec((tm, tn), lambda i, j, k: (i, (OFF_G + D) // tn + j)), blk, blk],
        out_shape=[jax.ShapeDtypeStruct((S, D), BF16), jax.ShapeDtypeStruct((S, D), BF16),
                   jax.ShapeDtypeStruct((2, S, D), BF16)],
        out_specs=[blk, blk, pl.BlockSpec((2, tm, tn), lambda i, j, k: (0, i, j))], epilogue=ep, carry=carry), 3, carry)


def _colsharded_bwd(name, dy, w_g, out_dtype, carry=None):
    S = dy.shape[0]
    _, K, cw = w_g.shape
    tm = _pick(S, (1024, 512, 256, 128))
    tn = _pick(K, (1024, 512, 256, 128))
    return _result(_matmul(
        name, dy, [w_g], dims=NT, grid=(S // tm, K // tn, N_CHIPS),
        lhs_spec=pl.BlockSpec((tm, cw), lambda i, j, k: (i, k)),
        rhs_specs=[pl.BlockSpec((None, tn, cw), lambda i, j, k: (k, j, 0))],
        acc_shape=(tm, tn), out_shape=[jax.ShapeDtypeStruct((S, K), out_dtype)],
        out_specs=[pl.BlockSpec((tm, tn), lambda i, j, k: (i, j))], epilogue=_store_epilogue(out_dtype),
        carry=carry), 1, carry)


def _wgrad_cols(name, x, dy, carry=None, colsum=False):
    S, R = x.shape
    C = dy.shape[1]
    cw = C // N_CHIPS
    tm = _pick(R, (1024, 512, 256, 128))
    tk = _pick(S, (2048, 1024, 512, 256, 128))

    def ep(parts, e_refs, o_refs, cols):
        for o, p in zip(o_refs, parts):
            o[:, cols] = p

    out_shape = [jax.ShapeDtypeStruct((N_CHIPS, R, cw), F32)]
    out_specs = [pl.BlockSpec((None, tm, cw), lambda i, j, k: (j, i, 0))]
    if colsum:
        out_shape.append(jax.ShapeDtypeStruct((R // tm, N_CHIPS, 8, cw), F32))
        out_specs.append(pl.BlockSpec((None, None, 8, cw), lambda i, j, k: (i, j, 0, 0)))
    return _result(_matmul(
        name, x, [dy], dims=TN, grid=(R // tm, N_CHIPS, S // tk),
        lhs_spec=pl.BlockSpec((tk, tm), lambda i, j, k: (k, i)),
        rhs_specs=[pl.BlockSpec((tk, cw), lambda i, j, k: (k, j))],
        acc_shape=(tm, cw), out_shape=out_shape, out_specs=out_specs, epilogue=ep,
        carry=carry, rhs_colsum=colsum), len(out_shape), carry)


def _wgrad_gate_up(hn, dgu, dm, carry=None):
    S, D = hn.shape
    Fd = dm["F"]
    cw = 2 * Fd // N_CHIPS
    half = N_CHIPS // 2
    tm = _pick(D, (1024, 512, 256, 128))
    tk = _pick(S, (2048, 1024, 512, 256, 128))
    tn = _pick(cw, (1408, 512, 384, 256, 128))
    nbc = cw // tn
    return _result(_matmul(
        "wgrad_gate_up", hn, [dgu], dims=TN, grid=(D // tm, 2 * Fd // tn, S // tk),
        lhs_spec=pl.BlockSpec((tk, tm), lambda i, j, k: (k, i)),
        rhs_specs=[pl.BlockSpec((None, tk, tn), lambda i, j, k: (j // (half * nbc), k, j % (half * nbc)))],
        acc_shape=(tm, tn), out_shape=[jax.ShapeDtypeStruct((N_CHIPS, D, cw), F32)],
        out_specs=[pl.BlockSpec((None, tm, tn), lambda i, j, k: (j // nbc, i, j % nbc))], epilogue=_store_epilogue(F32),
        carry=carry), 1, carry)


def _wgrad_rows(name, x, dy):
    S, R = x.shape
    C = dy.shape[1]
    rw = R // N_CHIPS
    tn = _pick(C, (1024, 512, 256, 128))
    tk = _pick(S, (2048, 1024, 512, 256, 128))
    return _matmul(
        name, x, [dy], dims=TN, grid=(N_CHIPS, C // tn, S // tk),
        lhs_spec=pl.BlockSpec((tk, rw), lambda i, j, k: (k, i)),
        rhs_specs=[pl.BlockSpec((tk, tn), lambda i, j, k: (k, j))],
        acc_shape=(rw, tn), out_shape=[jax.ShapeDtypeStruct((N_CHIPS, rw, C), F32)],
        out_specs=[pl.BlockSpec((None, rw, tn), lambda i, j, k: (i, 0, j))], epilogue=_store_epilogue(F32))[0]


def _place():
    x, y, c = lax.axis_index("x"), lax.axis_index("y"), lax.axis_index("c")
    others = [(1 - x, y), (x, 1 - y), (1 - x, 1 - y)]
    return x, y, c, others


def _chip_index(chip):
    return 2 * chip[0] + chip[1]


def _gather_weights(bufs):
    n = len(bufs)

    def copies(src, out, send_sems, recv_sems):
        x, y, c, others = _place()

        def half(ref, chip_idx, hc):
            r2 = ref.shape[1] // 2
            return ref.at[chip_idx, pl.ds(hc * r2, r2), :]

        def copy(t, k, chip, hc, to):
            return pltpu.make_async_remote_copy(
                src_ref=half(src[t], _chip_index(chip), hc), dst_ref=half(out[t], _chip_index(chip), hc),
                send_sem=send_sems.at[6 * t + k], recv_sem=recv_sems.at[6 * t + k],
                device_id=to, device_id_type=MESH)

        me, sibling = (x, y, c), (x, y, 1 - c)
        pairs = [(t, j, chip) for t in range(n) for j, chip in enumerate(others)]
        sent = [copy(t, j, (x, y), c, (*chip, c)) for t, j, chip in pairs]
        landed = [copy(t, j, chip, c, me) for t, j, chip in pairs]
        passed = [copy(t, 3 + j, chip, c, sibling) for t, j, chip in pairs]
        handed = [copy(t, 3 + j, chip, 1 - c, me) for t, j, chip in pairs]
        return sent, landed, passed, handed

    def start(src, out, send_sems, recv_sems):
        for cp in copies(src, out, send_sems, recv_sems)[0]:
            cp.start()

    def finish(src, out, send_sems, recv_sems):
        sent, landed, passed, handed = copies(src, out, send_sems, recv_sems)
        for arrival, forward in zip(landed, passed):
            arrival.wait_recv()
            forward.start()
        for cp in handed:
            cp.wait_recv()
        for cp in sent + passed:
            cp.wait_send()

    return _Comm("gather_weights", bufs, [jax.ShapeDtypeStruct(b.shape, BF16) for b in bufs],
                 {t: t for t in range(n)}, 6 * n, start, finish)


def _sibling_exchange(grads):
    n = len(grads)
    shapes = [g.shape for g in grads]

    def copies(src, land, send_sems, recv_sems):
        x, y, c, _ = _place()
        res = []
        for t in range(n):
            r2 = shapes[t][1] // 2
            res.append(pltpu.make_async_remote_copy(
                src_ref=src[t].at[:, pl.ds((1 - c) * r2, r2), :], dst_ref=land[t],
                send_sem=send_sems.at[t], recv_sem=recv_sems.at[t], device_id=(x, y, 1 - c), device_id_type=MESH))
        return res

    def start(*refs):
        for cp in copies(*refs):
            cp.start()

    def finish(*refs):
        remote = copies(*refs)
        for cp in remote:
            cp.wait_recv()
        for cp in remote:
            cp.wait_send()

    return _Comm("sibling_exchange", grads, [jax.ShapeDtypeStruct((s[0], s[1] // 2, s[2]), F32) for s in shapes],
                 {}, n, start, finish)


def _chip_exchange(sends):
    n = len(sends)
    shapes = [s.shape for s in sends]

    def copies(snd, got, send_sems, recv_sems):
        x, y, c, others = _place()
        return [pltpu.make_async_remote_copy(
            src_ref=snd[t].at[_chip_index(chip)], dst_ref=got[t].at[j],
            send_sem=send_sems.at[3 * t + j], recv_sem=recv_sems.at[3 * t + j],
            device_id=(*chip, c), device_id_type=MESH) for t in range(n) for j, chip in enumerate(others)]

    def start(*refs):
        for cp in copies(*refs):
            cp.start()

    def finish(*refs):
        remote = copies(*refs)
        for cp in remote:
            cp.wait_recv()
        for cp in remote:
            cp.wait_send()

    return _Comm("chip_exchange", sends, [jax.ShapeDtypeStruct((3, s[1], s[2]), BF16) for s in shapes],
                 {}, 3 * n, start, finish)


def _sibling_share(fulls):
    n = len(fulls)
    shapes = [f.shape for f in fulls]

    def copies(src, out, send_sems, recv_sems, mine):
        x, y, c, _ = _place()
        hc = c if mine else 1 - c
        res = []
        for t in range(n):
            r2 = shapes[t][0] // 2
            res.append(pltpu.make_async_remote_copy(
                src_ref=src[t].at[pl.ds(hc * r2, r2), :], dst_ref=out[t].at[pl.ds(hc * r2, r2), :],
                send_sem=send_sems.at[t], recv_sem=recv_sems.at[t], device_id=(x, y, 1 - c), device_id_type=MESH))
        return res

    def start(*refs):
        for cp in copies(*refs, mine=True):
            cp.start()

    def finish(*refs):
        for cp in copies(*refs, mine=False):
            cp.wait_recv()
        for cp in copies(*refs, mine=True):
            cp.wait_send()

    return _Comm("sibling_share", fulls, [jax.ShapeDtypeStruct(s, F32) for s in shapes],
                 {t: t for t in range(n)}, n, start, finish)


def _gather_all(v):
    R, C = v.shape

    def body(v_ref, out_ref, send_sems, recv_sems, local_sem):
        x, y, c, others = _place()
        me, sibling = (x, y, c), (x, y, 1 - c)

        def rows(px, py, pc):
            return out_ref.at[4 * px + 2 * py + pc]

        def copy(k, block, to, src=None):
            return pltpu.make_async_remote_copy(
                src_ref=rows(*block) if src is None else src, dst_ref=rows(*block),
                send_sem=send_sems.at[k], recv_sem=recv_sems.at[k], device_id=to, device_id_type=MESH)

        mine = pltpu.make_async_copy(v_ref, rows(*me), local_sem)
        mine.start()
        first = [copy(0, me, sibling, src=v_ref)]
        first += [copy(1 + j, me, (*chip, c), src=v_ref) for j, chip in enumerate(others)]
        for cp in first:
            cp.start()
        passed = [copy(4 + j, (*chip, c), sibling) for j, chip in enumerate(others)]
        for j, chip in enumerate(others):
            copy(1 + j, (*chip, c), me).wait_recv()
            passed[j].start()
        copy(0, sibling, me).wait_recv()
        for j, chip in enumerate(others):
            copy(4 + j, (*chip, 1 - c), me).wait_recv()
        for cp in first + passed:
            cp.wait_send()
        mine.wait()

    return pl.pallas_call(
        body, name="gather_all", in_specs=[ANY], out_specs=ANY,
        out_shape=jax.ShapeDtypeStruct((8, R, C), F32),
        scratch_shapes=[pltpu.SemaphoreType.DMA((7,)), pltpu.SemaphoreType.DMA((7,)), pltpu.SemaphoreType.DMA],
    )(v)


def _my_chip():
    return 2 * lax.axis_index("x") + lax.axis_index("y")


def _my_core():
    return lax.axis_index("c")


def _pair_sum(grad, land):
    K, R2, C = land.shape
    tm = _row_tile(R2, C)
    nrb = R2 // tm

    def body(a_ref, b_ref, sb_ref):
        sb_ref[...] = (a_ref[...] + b_ref[...]).astype(BF16)

    blk = pl.BlockSpec((None, tm, C), lambda k, r: (k, r, 0))
    return pl.pallas_call(
        body, name="pair_sum", grid=(K, nrb),
        in_specs=[pl.BlockSpec((None, tm, C), lambda k, r: (k, _my_core() * nrb + r, 0)), blk],
        out_specs=blk, out_shape=jax.ShapeDtypeStruct((K, R2, C), BF16),
        compiler_params=_params(("parallel", "parallel")),
    )(grad, land)


def _chip_sum(grad, land, got):
    _, R2, C = land.shape
    tm = _row_tile(R2, C)
    nrb = R2 // tm

    def body(a_ref, b_ref, g_ref, s_ref):
        own = a_ref[...] + b_ref[...]
        s_ref[...] = ((own + g_ref[0].astype(F32)) + g_ref[1].astype(F32)) + g_ref[2].astype(F32)

    return pl.pallas_call(
        body, name="chip_sum", grid=(nrb,),
        in_specs=[pl.BlockSpec((None, tm, C), lambda r: (_my_chip(), _my_core() * nrb + r, 0)),
                  pl.BlockSpec((None, tm, C), lambda r: (_my_chip(), r, 0)),
                  pl.BlockSpec((3, tm, C), lambda r: (0, r, 0))],
        out_specs=pl.BlockSpec((tm, C), lambda r: (_my_core() * nrb + r, 0)),
        out_shape=jax.ShapeDtypeStruct((2 * R2, C), F32),
        compiler_params=_params(("parallel",)),
    )(grad, land, got)


def _adamw_math(w, g, m, v):
    m = ADAM_B1 * m + (1.0 - ADAM_B1) * g
    v = ADAM_B2 * v + (1.0 - ADAM_B2) * (g * g)
    m_hat = m / (1.0 - ADAM_B1 ** ADAM_STEP)
    v_hat = v / (1.0 - ADAM_B2 ** ADAM_STEP)
    delta = -ADAM_LR * (m_hat / (jnp.sqrt(v_hat) + ADAM_EPS) + ADAM_WD * w)
    return delta, m, v


def _adamw_stacked(grads, w, m, v):
    L, R, C = w.shape
    tm = _row_tile(R, C)
    nrb = R // tm

    def body(*refs):
        g_refs = refs[:L]
        w_ref, m_ref, v_ref, go_ref, d_ref, mo_ref, vo_ref = refs[L:]
        l = pl.program_id(0)
        for ll in range(L):
            @pl.when(l == ll)
            def _(ll=ll):
                g = g_refs[ll][...]
                delta, mn, vn = _adamw_math(w_ref[...], g, m_ref[...], v_ref[...])
                go_ref[...] = g
                d_ref[...] = delta
                mo_ref[...] = mn
                vo_ref[...] = vn

    stacked = pl.BlockSpec((None, tm, C), lambda l, r: (l, r, 0))
    g_specs = [pl.BlockSpec((tm, C), lambda l, r, ll=ll: (jnp.where(l == ll, r, 0), 0)) for ll in range(L)]
    shp = jax.ShapeDtypeStruct((L, R, C), F32)
    return pl.pallas_call(
        body, name="adamw", grid=(L, nrb),
        in_specs=[*g_specs, stacked, stacked, stacked], out_specs=[stacked] * 4, out_shape=[shp] * 4,
        compiler_params=_params(("arbitrary", "arbitrary")),
    )(*grads, w, m, v)


def _adamw_small(parts, w, m, v):
    _, R, C = parts.shape
    tm = _row_tile(R, 8 * C)

    def body(p_ref, w_ref, m_ref, v_ref, go_ref, d_ref, mo_ref, vo_ref):
        g = p_ref[0]
        for k in range(1, 8):
            g = g + p_ref[k]
        delta, mn, vn = _adamw_math(w_ref[...], g, m_ref[...], v_ref[...])
        go_ref[...] = g
        d_ref[...] = delta
        mo_ref[...] = mn
        vo_ref[...] = vn

    blk = pl.BlockSpec((tm, C), lambda i: (i, 0))
    shp = jax.ShapeDtypeStruct((R, C), F32)
    return pl.pallas_call(
        body, name="adamw_small", grid=(R // tm,),
        in_specs=[pl.BlockSpec((8, tm, C), lambda i: (0, i, 0)), blk, blk, blk],
        out_specs=[blk] * 4, out_shape=[shp] * 4,
        compiler_params=_params(("parallel",)),
    )(parts, w, m, v)


def _cast_place(w, layer):
    _, R, C = w.shape
    tm = _row_tile(R, C)

    def body(w_ref, o_ref):
        o_ref[...] = w_ref[...].astype(BF16)

    return pl.pallas_call(
        body, name="cast_place", grid=(R // tm,),
        in_specs=[pl.BlockSpec((None, tm, C), lambda r: (layer, r, 0))],
        out_specs=pl.BlockSpec((None, tm, C), lambda r: (_my_chip(), r, 0)),
        out_shape=jax.ShapeDtypeStruct((N_CHIPS, R, C), BF16),
        compiler_params=_params(("parallel",)),
    )(w)


def _trig_tables(positions):
    inv_freq = ROPE_THETA ** (-jnp.arange(0, ROPE_DIM, 2, dtype=F32) / ROPE_DIM)
    ang = positions.astype(F32)[:, None] * inv_freq
    cos, sin = jnp.cos(ang), jnp.sin(ang)
    S = positions.shape[0]
    cos_h = jnp.concatenate([cos, cos, jnp.ones((S, HEAD_DIM - ROPE_DIM), F32)], axis=1)
    sin_h = jnp.concatenate([-sin, sin, jnp.zeros((S, HEAD_DIM - ROPE_DIM), F32)], axis=1)
    rep = LANES // HEAD_DIM
    return [jnp.tile(t, (1, rep)) for t in (cos_h, sin_h)]


def _row(vec):
    return vec.reshape(1, -1)


def _lane_row(vec):
    return jnp.zeros((8, LANES), F32).at[0, :vec.shape[0]].set(vec)


def _pack(pieces, rows):
    flat = jnp.concatenate([p.reshape(-1).astype(F32) for p in pieces])
    return jnp.pad(flat, (0, rows * LANES - flat.shape[0])).reshape(rows, LANES)


def kernel(x, positions, norm1_g, w_in, b_in, sinks, sgu_ln_g, sgu_ln_b, sgu_w, sgu_b, w_attn_branch, w_sgu_branch, w_out, norm2_g, w_gate_up, w_down, final_g, loss_target, m_norm1_g, m_w_in, m_b_in, m_sinks, m_sgu_ln_g, m_sgu_ln_b, m_sgu_w, m_sgu_b, m_w_attn_branch, m_w_sgu_branch, m_w_out, m_norm2_g, m_w_gate_up, m_w_down, m_final_g, v_norm1_g, v_w_in, v_b_in, v_sinks, v_sgu_ln_g, v_sgu_ln_b, v_sgu_w, v_sgu_b, v_w_attn_branch, v_w_sgu_branch, v_w_out, v_norm2_g, v_w_gate_up, v_w_down, v_final_g):
    L = norm1_g.shape[0]
    S, D = x.shape[1], x.shape[2]
    NQ = sinks.shape[1]
    A = NQ * HEAD_DIM
    KV = N_KV_HEADS * HEAD_DIM
    G = sgu_ln_g.shape[1]
    NG = sgu_w.shape[1]
    IN = b_in.shape[1]
    Fd = w_down.shape[1] * N_CHIPS
    dm = dict(D=D, A=A, KV=KV, NQ=NQ, G=G, NG=NG, IN=IN, F=Fd,
              OFF_K=A, OFF_V=A + KV, OFF_Z=A + 2 * KV, OFF_G=A + 2 * KV + 2 * G)
    assert sgu_w.shape[2] == WINDOW and G == NG * LANES and IN == dm["OFF_G"] + 2 * D

    h = x[0]
    target = loss_target[0]
    trig = _trig_tables(positions[0])
    tril = jnp.tril(jnp.ones((WINDOW, WINDOW), bool))

    big = [w_in, w_attn_branch, w_sgu_branch, w_out, w_gate_up, w_down]
    big_m = [m_w_in, m_w_attn_branch, m_w_sgu_branch, m_w_out, m_w_gate_up, m_w_down]
    big_v = [v_w_in, v_w_attn_branch, v_w_sgu_branch, v_w_out, v_w_gate_up, v_w_down]

    placed = [[_cast_place(w, l) for w in big] for l in range(L)]
    IN_, AB, SB, OUT, GU, DOWN = range(len(big))
    gathered = [[None] * len(big) for _ in range(L)]
    gathered[0][IN_] = _gather_weights([placed[0][IN_]]).run()[0]

    def fetch(layer, idx):
        return _gather_weights([placed[layer][t] for t in idx]) if layer < L else None

    def fetched(layer, idx, res):
        if layer >= L:
            return res
        main, got = res
        for t, g in zip(idx, got):
            gathered[layer][t] = g
        return main

    def weights(l):
        flat = lambda w, rows: None if w is None else w.reshape(rows, D)
        w_in_g, w_ab_g, w_sb_g, w_out_g, w_gu_g, w_down_g = gathered[l]
        return (w_in_g, w_ab_g, w_sb_g, flat(w_out_g, D), w_gu_g, flat(w_down_g, Fd))

    def small(l):
        return dict(
            g1=_row(norm1_g[l]), b_in=_row(b_in[l]), sink=_lane_row(sinks[l]),
            ln_g=_row(sgu_ln_g[l]), ln_b=_row(sgu_ln_b[l]),
            w_tril=jnp.where(tril[None], sgu_w[l], 0.0).astype(BF16),
            b_t=jnp.zeros((WINDOW, LANES), F32).at[:, :NG].set(sgu_b[l].T),
            g2=_row(norm2_g[l]))

    saved = []
    for l in range(L):
        sp = small(l)
        xn = _rms_fwd(h, sp["g1"])
        now = [AB, SB, OUT, GU] if l == 0 else [DOWN]
        proj = fetched(l, now, _in_proj(xn, gathered[l][IN_], sp["b_in"], trig, dm, carry=fetch(l, now)))
        w_in_g, w_ab_g, w_sb_g, w_out_g = weights(l)[:4]
        y_attn, lse = _attn_fwd(proj, sp["sink"], dm)
        y_sgu = _sgu_fwd(proj, sp["w_tril"], sp["b_t"], sp["ln_g"], sp["ln_b"], dm)
        a_attn = _branch_attn(y_attn, w_ab_g, dm)
        a_sgu, merged = _branch_sgu_merge(y_sgu, w_sb_g, a_attn, proj, dm)
        if l == 0:
            h_mid = fetched(l, [DOWN], _residual_matmul("out_proj", merged, w_out_g, h, carry=fetch(l, [DOWN])))
        else:
            h_mid = _residual_matmul("out_proj", merged, w_out_g, h)
        w_gu_g, w_down_g = weights(l)[4:]
        hn = _rms_fwd(h_mid, sp["g2"])
        ahead = [IN_, AB, SB, OUT]
        gu, act = fetched(l + 1, ahead, _gate_up(hn, w_gu_g, dm, carry=fetch(l + 1, ahead)))
        h_out = fetched(l + 1, [GU], _residual_matmul("down_proj", act, w_down_g, h_mid, carry=fetch(l + 1, [GU])))
        saved.append(dict(h=h, xn=xn, proj=proj, y_attn=y_attn, lse=lse, y_sgu=y_sgu, a_attn=a_attn, a_sgu=a_sgu,
                          merged=merged, h_mid=h_mid, hn=hn, gu=gu, act=act))
        h = h_out

    dh, dh_b, d_final, loss_part = _loss_head(h, _row(final_g), target)

    small_grads = [None] * L
    reduced = [[None] * len(big) for _ in range(L)]
    early, mid, late = [GU, DOWN], [AB, SB, OUT], [IN_]

    def riding(has_carry, res):
        return res if has_carry else (res, None)

    def sends_of(grads, land):
        return [_pair_sum(g, d) for g, d in zip(grads, land)]

    def finished(grads, land, got):
        return [_chip_sum(g, d, p) for g, d, p in zip(grads, land, got)]

    def file_reduced(layer, idx, fulls):
        for t, f in zip(idx, fulls):
            reduced[layer][t] = f

    late_grads = None
    mid_fulls = None
    n_late, n_mid, n_early = len(late), len(mid), len(early)
    for l in reversed(range(L)):
        w_in_g, w_ab_g, w_sb_g, w_out_g, w_gu_g, w_down_g = weights(l)
        sp, sv = small(l), saved[l]
        have = late_grads is not None
        dgu, rode = riding(have, _down_bwd(
            dh_b, w_down_g, sv["gu"], dm,
            carry=_sibling_exchange(late_grads).beside(_sibling_share(mid_fulls)) if have else None))
        if have:
            land = rode[:n_late]
            file_reduced(l + 1, mid, rode[n_late:])
        g_down = _wgrad_rows("wgrad_down", sv["act"], dh_b)
        dhn, got = riding(have, _gate_up_bwd(dgu, w_gu_g, dm,
                                             carry=_chip_exchange(sends_of(late_grads, land)) if have else None))
        g_gu, shared = riding(have, _wgrad_gate_up(sv["hn"], dgu, dm,
                                                   carry=_sibling_share(finished(late_grads, land, got)) if have else None))
        if have:
            file_reduced(l + 1, late, shared)
        dh_mid, dh_mid_b, d_g2 = _rms_bwd(dhn, sv["h_mid"], sp["g2"], dh)
        early_grads = [g_gu, g_down]
        (da_attn, da_sgu, dgate), land_e = _out_bwd(dh_mid_b, w_out_g, sv["proj"], sv["a_attn"], sv["a_sgu"], dm,
                                                     carry=_sibling_exchange(early_grads))
        sends_e = sends_of(early_grads, land_e)
        g_out = _wgrad_rows("wgrad_out", sv["merged"], dh_mid_b)
        dy_attn = _colsharded_bwd("branch_attn_bwd", da_attn, w_ab_g, BF16)
        dy_sgu = _colsharded_bwd("branch_sgu_bwd", da_sgu, w_sb_g, BF16)
        g_ab = _wgrad_cols("wgrad_attn_branch", sv["y_attn"], da_attn)
        g_sb = _wgrad_cols("wgrad_sgu_branch", sv["y_sgu"], da_sgu)
        mid_grads = [g_ab, g_sb, g_out]
        dq, dk, dv, d_sink = _attn_bwd(sv["proj"], trig, sp["sink"], sv["y_attn"], sv["lse"], dy_attn, dm)
        dz, d_sgu_w, d_bt, d_lng, d_lnb = _sgu_bwd(sv["proj"], sp["w_tril"], sp["b_t"], sp["ln_g"], sp["ln_b"], dy_sgu, dm)
        dproj = jnp.concatenate([dq, dk, dv, dz, dgate[0], dgate[1]], axis=1)
        dxn, rode = _colsharded_bwd("in_proj_bwd", dproj, w_in_g, F32,
                                    carry=_chip_exchange(sends_e).beside(_sibling_exchange(mid_grads)))
        got_e, land_m = rode[:n_early], rode[n_early:]
        (g_in, d_bin), rode = _wgrad_cols(
            "wgrad_in", sv["xn"], dproj, colsum=True,
            carry=_sibling_share(finished(early_grads, land_e, got_e)).beside(_chip_exchange(sends_of(mid_grads, land_m))))
        file_reduced(l, early, rode[:n_early])
        mid_fulls = finished(mid_grads, land_m, rode[n_early:])
        dh, dh_b, d_g1 = _rms_bwd(dxn, sv["h"], sp["g1"], dh_mid)
        late_grads = [g_in]
        small_grads[l] = dict(norm1_g=d_g1[0], b_in=d_bin[0, :, 0, :].reshape(-1), sinks=d_sink[0, :NQ],
                              sgu_ln_g=d_lng[0], sgu_ln_b=d_lnb[0], sgu_w=d_sgu_w, sgu_b=d_bt[:, :NG].T, norm2_g=d_g2[0])
    grad_x = dh[None]

    land = _sibling_exchange(late_grads).run()
    got = _chip_exchange(sends_of(late_grads, land)).run()
    shared = _sibling_share(finished(late_grads, land, got) + mid_fulls).run()
    file_reduced(0, late, shared[:n_late])
    file_reduced(0, mid, shared[n_late:])
    big_out = [_adamw_stacked([reduced[l][t] for l in range(L)], big[t], big_m[t], big_v[t]) for t in range(len(big))]

    names = ["norm1_g", "b_in", "sinks", "sgu_ln_g", "sgu_ln_b", "sgu_w", "sgu_b", "norm2_g"]
    small_w = [norm1_g, b_in, sinks, sgu_ln_g, sgu_ln_b, sgu_w, sgu_b, norm2_g, final_g]
    small_m = [m_norm1_g, m_b_in, m_sinks, m_sgu_ln_g, m_sgu_ln_b, m_sgu_w, m_sgu_b, m_norm2_g, m_final_g]
    small_v = [v_norm1_g, v_b_in, v_sinks, v_sgu_ln_g, v_sgu_ln_b, v_sgu_w, v_sgu_b, v_norm2_g, v_final_g]
    small_g = [jnp.stack([small_grads[l][nm] for l in range(L)]) for nm in names] + [d_final[0]]
    sizes = [w.size for w in small_w]
    total = sum(sizes) + 1
    rows = -(-total // (512 * LANES)) * 512
    loss_piece = jnp.sum(loss_part[0]).reshape(1)
    packed_g = _pack(small_g + [loss_piece], rows)
    one = jnp.ones((1,), F32)
    parts = _gather_all(packed_g)
    outs = _adamw_small(parts, _pack(small_w + [one], rows), _pack(small_m + [one], rows), _pack(small_v + [one], rows))

    def unpack(p):
        flat = p.reshape(-1)
        res, off = [], 0
        for w, n in zip(small_w, sizes):
            res.append(flat[off:off + n].reshape(w.shape))
            off += n
        return res, flat[off]

    (sg, loss), (sd, _), (smm, _), (svv, _) = [unpack(o) for o in outs]

    order = ["norm1_g", "w_in", "b_in", "sinks", "sgu_ln_g", "sgu_ln_b", "sgu_w", "sgu_b", "w_attn_branch",
             "w_sgu_branch", "w_out", "norm2_g", "w_gate_up", "w_down", "final_g"]
    big_names = ["w_in", "w_attn_branch", "w_sgu_branch", "w_out", "w_gate_up", "w_down"]
    small_names = names + ["final_g"]

    def collect(kind):
        res = []
        for nm in order:
            if nm in big_names:
                res.append(big_out[big_names.index(nm)][kind])
            else:
                res.append((sg, sd, smm, svv)[kind][small_names.index(nm)])
        return res

    return (loss, grad_x, *collect(0), *collect(1), *collect(2), *collect(3))
```

```python
import math

import jax
import jax.numpy as jnp
from jax import lax
from jax.experimental import pallas as pl
from jax.experimental.pallas import tpu as pltpu

F32 = jnp.float32
BF16 = jnp.bfloat16
MESH = pl.DeviceIdType.MESH
ANY = pl.BlockSpec(memory_space=pl.ANY)

HEAD_DIM = 64
N_KV_HEADS = 4
WINDOW = 128
ROPE_DIM = HEAD_DIM // 4
ROPE_THETA = 500000.0
EPS = 1e-5
NEG = -1e30
N_CHIPS = 4
LANES = 128
V7X_VMEM_LIMIT = 56 * 1024 * 1024

ADAM_LR = 0.001
ADAM_B1 = 0.9
ADAM_B2 = 0.999
ADAM_EPS = 1e-08
ADAM_WD = 0.01
ADAM_STEP = 10

NN = (((1,), (0,)), ((), ()))
NT = (((1,), (1,)), ((), ()))
TN = (((0,), (0,)), ((), ()))


ROW_TILES = (1024, 512, 256, 128, 64, 32, 16, 8)
BLOCK_BYTES = 2 * 1024 * 1024


def _pick(n, prefs):
    for p in prefs:
        if n % p == 0:
            return p
    raise ValueError(f"no tile for {n} among {prefs}")


def _row_tile(rows, cols, itemsize=4):
    return _pick(rows, [t for t in ROW_TILES if t * cols * itemsize <= BLOCK_BYTES or t == ROW_TILES[-1]])


def _dot(a, b, dims):
    return lax.dot_general(a, b, dims, preferred_element_type=F32)


def _sigmoid(x):
    return 0.5 * jnp.tanh(0.5 * x) + 0.5


def _gelu(x):
    return 0.5 * x * (1.0 + lax.erf(x * (1.0 / math.sqrt(2.0))))


def _gelu_grad(x):
    return 0.5 * (1.0 + lax.erf(x * (1.0 / math.sqrt(2.0)))) + x * jnp.exp(-0.5 * x * x) * (1.0 / math.sqrt(2.0 * math.pi))


def _params(sem):
    return pltpu.CompilerParams(dimension_semantics=sem, vmem_limit_bytes=V7X_VMEM_LIMIT)


def _matmul(name, lhs, rhs_list, *, dims, grid, lhs_spec, rhs_specs, acc_shape, out_shape, out_specs,
            epilogue, extra=(), extra_specs=(), carry=None, rhs_colsum=False, cols_outer=False, extra_aliases=None):
    if cols_outer:
        swap = lambda s: s if s.index_map is None else pl.BlockSpec(s.block_shape, lambda j, i, k, f=s.index_map: f(i, j, k))
        grid = (grid[1], grid[0], grid[2])
        lhs_spec, rhs_specs = swap(lhs_spec), [swap(s) for s in rhs_specs]
        extra_specs, out_specs = [swap(s) for s in extra_specs], [swap(s) for s in out_specs]
    gk = grid[2]
    nr, ne, no = len(rhs_list), len(extra), len(out_shape)
    nci = len(carry.ins) if carry else 0
    nco = len(carry.outs) if carry else 0
    acc_shapes = [acc_shape] * nr + ([(8, acc_shape[1])] if rhs_colsum else [])
    nacc = len(acc_shapes) if gk > 1 else 0

    def body(*refs):
        a_ref = refs[0]
        b_refs = refs[1:1 + nr]
        e_refs = refs[1 + nr:1 + nr + ne]
        base = 1 + nr + ne
        ci_refs = refs[base:base + nci]
        o_refs = refs[base + nci:base + nci + no]
        co_refs = refs[base + nci + no:base + nci + no + nco]
        acc_refs = refs[base + nci + no + nco:base + nci + no + nco + nacc]
        sems = refs[base + nci + no + nco + nacc:]
        ids = [pl.program_id(d) for d in range(3)]
        if carry:
            @pl.when((ids[0] == 0) & (ids[1] == 0) & (ids[2] == 0))
            def _():
                carry.start(ci_refs, co_refs, *sems)

        a = a_ref[...]
        if gk == 1:
            n_axis = 1 - dims[0][1][0]
            for cols in _col_chunks(acc_shape[1]):
                pick = (slice(None), cols) if n_axis == 1 else (cols, slice(None))
                parts = [_dot(a, b[pick], dims) for b in b_refs]
                if rhs_colsum:
                    b0 = b_refs[0][pick]
                    parts.append(_dot(jnp.ones((8, b0.shape[0]), b0.dtype), b0, NN))
                epilogue(parts, e_refs, o_refs, cols)
        else:
            k = ids[2]

            @pl.when(k == 0)
            def _():
                for acc in acc_refs:
                    acc[...] = jnp.zeros_like(acc)

            for acc, b in zip(acc_refs, b_refs):
                acc[...] += _dot(a, b[...], dims)
            if rhs_colsum:
                b0 = b_refs[0][...]
                acc_refs[-1][...] += _dot(jnp.ones((8, b0.shape[0]), b0.dtype), b0, NN)

            @pl.when(k == gk - 1)
            def _():
                epilogue([acc[...] for acc in acc_refs], e_refs, o_refs, slice(None))

        if carry:
            @pl.when((ids[0] == grid[0] - 1) & (ids[1] == grid[1] - 1) & (ids[2] == grid[2] - 1))
            def _():
                carry.finish(ci_refs, co_refs, *sems)

    scratch = [pltpu.VMEM(s, F32) for s in acc_shapes[:nacc]]
    kwargs = {}
    aliases = {1 + nr + e: o for e, o in (extra_aliases or {}).items()}
    if carry:
        scratch += carry.sem_scratch()
        aliases.update({1 + nr + ne + i: no + o for i, o in carry.aliases.items()})
    if aliases:
        kwargs["input_output_aliases"] = aliases
    outs = pl.pallas_call(
        body, name=name, grid=grid,
        in_specs=[lhs_spec, *rhs_specs, *extra_specs, *([ANY] * nci)],
        out_specs=[*out_specs, *([ANY] * nco)],
        out_shape=[*out_shape, *(carry.outs if carry else [])], scratch_shapes=scratch,
        compiler_params=_params(("arbitrary",) * 3 if carry else ("parallel", "parallel", "arbitrary")),
        **kwargs,
    )(lhs, *rhs_list, *extra, *(carry.ins if carry else []))
    return outs


class _Comm:
    def __init__(self, name, ins, outs, aliases, n_sems, start, finish):
        self.name, self.ins, self.outs, self.aliases, self.n_sems = name, list(ins), list(outs), dict(aliases), n_sems
        self.start, self.finish = start, finish

    def sem_scratch(self):
        return [pltpu.SemaphoreType.DMA((self.n_sems,)), pltpu.SemaphoreType.DMA((self.n_sems,))]

    def beside(self, other):
        ni, no, ns = len(self.ins), len(self.outs), self.n_sems

        def split(ins, outs, send_sems, recv_sems):
            mine = (ins[:ni], outs[:no], send_sems.at[pl.ds(0, ns)], recv_sems.at[pl.ds(0, ns)])
            theirs = (ins[ni:], outs[no:], send_sems.at[pl.ds(ns, other.n_sems)], recv_sems.at[pl.ds(ns, other.n_sems)])
            return mine, theirs

        def start(*refs):
            mine, theirs = split(*refs)
            self.start(*mine)
            other.start(*theirs)

        def finish(*refs):
            mine, theirs = split(*refs)
            self.finish(*mine)
            other.finish(*theirs)

        aliases = {**self.aliases, **{ni + i: no + o for i, o in other.aliases.items()}}
        return _Comm(self.name + "+" + other.name, self.ins + other.ins, self.outs + other.outs, aliases,
                     ns + other.n_sems, start, finish)

    def run(self):
        ni = len(self.ins)

        def body(*refs):
            in_refs, out_refs, sems = refs[:ni], refs[ni:ni + len(self.outs)], refs[ni + len(self.outs):]
            self.start(in_refs, out_refs, *sems)
            self.finish(in_refs, out_refs, *sems)

        return pl.pallas_call(
            body, name=self.name, in_specs=[ANY] * ni, out_specs=[ANY] * len(self.outs), out_shape=self.outs,
            input_output_aliases=self.aliases, scratch_shapes=self.sem_scratch(),
        )(*self.ins)


MXU_CHUNK = 256


def _col_chunks(n):
    if n % LANES:
        return [slice(0, n)]
    return [slice(s, min(s + MXU_CHUNK, n)) for s in range(0, n, MXU_CHUNK)]


def _store_epilogue(dtype):
    def ep(parts, e_refs, o_refs, cols):
        o_refs[0][:, cols] = parts[0].astype(dtype)
    return ep


def _rms_fwd(h, g_row):
    S, D = h.shape
    tm = _row_tile(S, D)

    def body(h_ref, g_ref, o_ref):
        x = h_ref[...]
        r = lax.rsqrt(jnp.mean(x * x, axis=-1, keepdims=True) + EPS)
        o_ref[...] = (x * r * g_ref[...]).astype(BF16)

    return pl.pallas_call(
        body, name="rms_fwd", grid=(S // tm,),
        in_specs=[pl.BlockSpec((tm, D), lambda i: (i, 0)), pl.BlockSpec((1, D), lambda i: (0, 0))],
        out_specs=pl.BlockSpec((tm, D), lambda i: (i, 0)),
        out_shape=jax.ShapeDtypeStruct((S, D), BF16),
        compiler_params=_params(("parallel",)),
    )(h, g_row)


def _rms_bwd(dy, h, g_row, dres):
    S, D = h.shape
    tm = _row_tile(S, D)

    def body(dy_ref, h_ref, g_ref, dres_ref, dh_ref, dhb_ref, dg_ref):
        i = pl.program_id(0)
        x = h_ref[...]
        d = dy_ref[...]
        r = lax.rsqrt(jnp.mean(x * x, axis=-1, keepdims=True) + EPS)
        dg = d * g_ref[...]
        dot = jnp.mean(dg * x, axis=-1, keepdims=True)
        dh = dres_ref[...] + r * dg - x * (r * r * r) * dot
        dh_ref[...] = dh
        dhb_ref[...] = dh.astype(BF16)
        part = jnp.sum(d * x * r, axis=0, keepdims=True)

        @pl.when(i == 0)
        def _():
            dg_ref[...] = jnp.zeros_like(dg_ref)

        dg_ref[0:1, :] += part

    return pl.pallas_call(
        body, name="rms_bwd", grid=(S // tm,),
        in_specs=[pl.BlockSpec((tm, D), lambda i: (i, 0)), pl.BlockSpec((tm, D), lambda i: (i, 0)),
                  pl.BlockSpec((1, D), lambda i: (0, 0)), pl.BlockSpec((tm, D), lambda i: (i, 0))],
        out_specs=[pl.BlockSpec((tm, D), lambda i: (i, 0)), pl.BlockSpec((tm, D), lambda i: (i, 0)),
                   pl.BlockSpec((8, D), lambda i: (0, 0))],
        out_shape=[jax.ShapeDtypeStruct((S, D), F32), jax.ShapeDtypeStruct((S, D), BF16),
                   jax.ShapeDtypeStruct((8, D), F32)],
        compiler_params=_params(("arbitrary",)),
    )(dy, h, g_row, dres)


def _loss_head(h, g_row, target):
    S, D = h.shape
    tm = _row_tile(S, D)

    def body(h_ref, g_ref, t_ref, dh_ref, dhb_ref, dg_ref, loss_ref):
        i = pl.program_id(0)
        x = h_ref[...]
        g = g_ref[...]
        r = lax.rsqrt(jnp.mean(x * x, axis=-1, keepdims=True) + EPS)
        y = x * r * g
        e = y - t_ref[...]
        d = e * (1.0 / D)
        dg = d * g
        dot = jnp.mean(dg * x, axis=-1, keepdims=True)
        dh = r * dg - x * (r * r * r) * dot
        dh_ref[...] = dh
        dhb_ref[...] = dh.astype(BF16)

        @pl.when(i == 0)
        def _():
            dg_ref[...] = jnp.zeros_like(dg_ref)
            loss_ref[...] = jnp.zeros_like(loss_ref)

        dg_ref[0:1, :] += jnp.sum(d * x * r, axis=0, keepdims=True)
        loss_ref[0:1, :] += jnp.sum((0.5 / D) * e * e, axis=0, keepdims=True)

    return pl.pallas_call(
        body, name="loss_head", grid=(S // tm,),
        in_specs=[pl.BlockSpec((tm, D), lambda i: (i, 0)), pl.BlockSpec((1, D), lambda i: (0, 0)),
                  pl.BlockSpec((tm, D), lambda i: (i, 0))],
        out_specs=[pl.BlockSpec((tm, D), lambda i: (i, 0)), pl.BlockSpec((tm, D), lambda i: (i, 0)),
                   pl.BlockSpec((8, D), lambda i: (0, 0)), pl.BlockSpec((8, D), lambda i: (0, 0))],
        out_shape=[jax.ShapeDtypeStruct((S, D), F32), jax.ShapeDtypeStruct((S, D), BF16),
                   jax.ShapeDtypeStruct((8, D), F32), jax.ShapeDtypeStruct((8, D), F32)],
        compiler_params=_params(("arbitrary",)),
    )(h, g_row, target)


def _rotary_partner(t):
    half = ROPE_DIM // 2
    if t.shape[-1] == LANES:
        lane = lax.broadcasted_iota(jnp.int32, t.shape, 1) & (HEAD_DIM - 1)
        return jnp.where(lane < half, pltpu.roll(t, LANES - half, 1), pltpu.roll(t, half, 1))
    r = lax.broadcasted_iota(jnp.int32, (LANES, LANES), 0)
    c = lax.broadcasted_iota(jnp.int32, (LANES, LANES), 1)
    cm = c & (HEAD_DIM - 1)
    perm = (((cm < half) & (r == c + half)) | ((cm >= half) & (cm < ROPE_DIM) & (r == c - half))).astype(BF16)
    hi = t.astype(BF16)
    lo = (t - hi.astype(F32)).astype(BF16)
    cols = [slice(s, s + LANES) for s in range(0, t.shape[-1], LANES)]
    return jnp.concatenate([_dot(hi[:, c_], perm, NN) + _dot(lo[:, c_], perm, NN) for c_ in cols], axis=1)


def _rope(t, cos, sin):
    return t * cos + _rotary_partner(t) * sin


def _rope_t(g, cos, sin):
    return g * cos + _rotary_partner(g * sin)


def _band_mask(n, qpk):
    qi = lax.broadcasted_iota(jnp.int32, (qpk * WINDOW, 2 * WINDOW), 0) & (WINDOW - 1)
    kj = lax.broadcasted_iota(jnp.int32, (qpk * WINDOW, 2 * WINDOW), 1)
    rel = qi + WINDOW - kj
    ok = (rel >= 0) & (rel < WINDOW)
    return ok & ((kj >= WINDOW) | (n > 0))


def _stack_heads(x, g, qpk):
    return jnp.concatenate([x[:, (g * qpk + hh) * HEAD_DIM:(g * qpk + hh + 1) * HEAD_DIM] for hh in range(qpk)], axis=0)


def _stack_cols(row, g, qpk):
    return jnp.concatenate([row[:, g * qpk + hh:g * qpk + hh + 1] for hh in range(qpk)], axis=0)


def _attn_specs(dm, nb):
    A, KV = dm["A"], dm["KV"]
    kb, vb = dm["OFF_K"] // KV, dm["OFF_V"] // KV
    cur = lambda n: jnp.minimum(n, nb - 1)
    prev = lambda n: jnp.maximum(jnp.minimum(n, nb - 1) - 1, 0)
    proj_specs = [
        pl.BlockSpec((WINDOW, A), lambda n: (cur(n), 0)),
        pl.BlockSpec((WINDOW, KV), lambda n: (prev(n), kb)),
        pl.BlockSpec((WINDOW, KV), lambda n: (cur(n), kb)),
        pl.BlockSpec((WINDOW, KV), lambda n: (prev(n), vb)),
        pl.BlockSpec((WINDOW, KV), lambda n: (cur(n), vb)),
    ]
    trig_cur = [pl.BlockSpec((WINDOW, LANES), lambda n: (cur(n), 0)) for _ in range(2)]
    trig_prev = [pl.BlockSpec((WINDOW, LANES), lambda n: (prev(n), 0)) for _ in range(2)]
    return proj_specs, trig_cur, trig_prev, cur, prev


def _attn_fwd(proj, sink_row, dm):
    S = proj.shape[0]
    A, KV, NQ = dm["A"], dm["KV"], dm["NQ"]
    qpk = NQ // N_KV_HEADS
    nb = S // WINDOW
    scale = HEAD_DIM ** -0.5
    proj_specs = _attn_specs(dm, nb)[0]

    def body(q_ref, kp_ref, kc_ref, vp_ref, vc_ref, sink_ref, y_ref, lse_ref):
        n = pl.program_id(0)
        qr = q_ref[...]
        kr = jnp.concatenate([kp_ref[...], kc_ref[...]], axis=0)
        vband = jnp.concatenate([vp_ref[...], vc_ref[...]], axis=0)
        mask = _band_mask(n, qpk)
        lane = lax.broadcasted_iota(jnp.int32, (WINDOW, LANES), 1)
        lse_all = jnp.zeros((WINDOW, LANES), F32)
        sink_rows = jnp.broadcast_to(sink_ref[0:1, :], (WINDOW, LANES))
        groups = range(N_KV_HEADS)
        head = lambda x, g: x[:, g * HEAD_DIM:(g + 1) * HEAD_DIM]
        ones = jnp.ones((2 * WINDOW, HEAD_DIM), BF16)
        sink = [_stack_cols(sink_rows, g, qpk) for g in groups]
        s = [jnp.where(mask, _dot(_stack_heads(qr, g, qpk), head(kr, g), NT) * scale, NEG) for g in groups]
        m = [jnp.maximum(jnp.max(s[g], axis=-1, keepdims=True), sink[g]) for g in groups]
        p = [jnp.exp(s[g] - m[g]).astype(BF16) for g in groups]
        ov = [_dot(p[g], jnp.concatenate([head(vband, g), ones], axis=1), NN) for g in groups]
        den = [ov[g][:, HEAD_DIM:HEAD_DIM + 1] + jnp.exp(sink[g] - m[g]) for g in groups]
        o = [ov[g][:, :HEAD_DIM] * (1.0 / den[g]) for g in groups]
        lse = [m[g] + jnp.log(den[g]) for g in groups]
        for g in groups:
            for hh in range(qpk):
                h = g * qpk + hh
                rows = slice(hh * WINDOW, (hh + 1) * WINDOW)
                y_ref[:, h * HEAD_DIM:(h + 1) * HEAD_DIM] = o[g][rows].astype(BF16)
                lse_all = jnp.where(lane == h, lse[g][rows], lse_all)
        lse_ref[...] = lse_all

    return pl.pallas_call(
        body, name="attn_fwd", grid=(nb,),
        in_specs=[*proj_specs, pl.BlockSpec((8, LANES), lambda n: (0, 0))],
        out_specs=[pl.BlockSpec((WINDOW, A), lambda n: (n, 0)), pl.BlockSpec((WINDOW, LANES), lambda n: (n, 0))],
        out_shape=[jax.ShapeDtypeStruct((S, A), BF16), jax.ShapeDtypeStruct((S, LANES), F32)],
        compiler_params=_params(("parallel",)),
    )(proj, proj, proj, proj, proj, sink_row)


def _attn_bwd(proj, trig, sink_row, y, lse, dy, dm):
    S = proj.shape[0]
    A, KV, NQ = dm["A"], dm["KV"], dm["NQ"]
    qpk = NQ // N_KV_HEADS
    nb = S // WINDOW
    scale = HEAD_DIM ** -0.5
    proj_specs, trig_cur, trig_prev, cur, prev = _attn_specs(dm, nb)

    def body(q_ref, kp_ref, kc_ref, vp_ref, vc_ref, cc_ref, sc_ref, cp_ref, sp_ref,
             sink_ref, y_ref, lse_ref, dy_ref, dq_ref, dk_ref, dv_ref, dsink_ref,
             ck_ref, cv_ref, bk_ref, bv_ref, dqr_ref):
        n = pl.program_id(0)

        @pl.when(n == 0)
        def _():
            dsink_ref[...] = jnp.zeros_like(dsink_ref)
            ck_ref[...] = jnp.zeros_like(ck_ref)
            cv_ref[...] = jnp.zeros_like(cv_ref)

        @pl.when(n < nb)
        def _():
            tq = lambda r: jnp.tile(r[...], (1, A // LANES))
            tk = lambda rp, rc: jnp.tile(jnp.concatenate([rp[...], rc[...]], axis=0), (1, KV // LANES))
            cq, sq = tq(cc_ref), tq(sc_ref)
            ck, sk = tk(cp_ref, cc_ref), tk(sp_ref, sc_ref)
            qr = q_ref[...]
            kr = jnp.concatenate([kp_ref[...], kc_ref[...]], axis=0)
            vband = jnp.concatenate([vp_ref[...], vc_ref[...]], axis=0)
            mask = _band_mask(n, qpk)
            lane = lax.broadcasted_iota(jnp.int32, (1, LANES), 1)
            lse_all = lse_ref[...]
            sink_rows = jnp.broadcast_to(sink_ref[0:1, :], (WINDOW, LANES))
            dy_all = dy_ref[...]
            y_all = y_ref[...]
            dsink = jnp.zeros((1, LANES), F32)
            groups = range(N_KV_HEADS)
            head = lambda x, g: x[:, g * HEAD_DIM:(g + 1) * HEAD_DIM]
            q = [_stack_heads(qr, g, qpk) for g in groups]
            dy = [_stack_heads(dy_all, g, qpk) for g in groups]
            lse = [_stack_cols(lse_all, g, qpk) for g in groups]
            s = [jnp.where(mask, _dot(q[g], head(kr, g), NT) * scale, NEG) for g in groups]
            dp = [_dot(dy[g], head(vband, g), NT) for g in groups]
            delta = [jnp.sum(dy[g].astype(F32) * _stack_heads(y_all, g, qpk).astype(F32), axis=-1, keepdims=True)
                     for g in groups]
            p = [jnp.exp(s[g] - lse[g]) for g in groups]
            ds = [(p[g] * (dp[g] - delta[g]) * scale).astype(BF16) for g in groups]
            dq = [_dot(ds[g], head(kr, g), NN) for g in groups]
            for g in groups:
                bk_ref[:, g * HEAD_DIM:(g + 1) * HEAD_DIM] = _dot(ds[g], q[g], TN)
                bv_ref[:, g * HEAD_DIM:(g + 1) * HEAD_DIM] = _dot(p[g].astype(BF16), dy[g], TN)
            for g in groups:
                sink_d = jnp.exp(_stack_cols(sink_rows, g, qpk) - lse[g]) * delta[g]
                for hh in range(qpk):
                    h = g * qpk + hh
                    rows = slice(hh * WINDOW, (hh + 1) * WINDOW)
                    dqr_ref[:, h * HEAD_DIM:(h + 1) * HEAD_DIM] = dq[g][rows]
                    dsink = dsink + jnp.where(lane == h, -jnp.sum(sink_d[rows], axis=0, keepdims=True), 0.0)
            dsink_ref[0:1, :] += dsink
            dq_ref[...] = _rope_t(dqr_ref[...], cq, sq).astype(BF16)
            dkb = _rope_t(bk_ref[...], ck, sk)
            dvb = bv_ref[...]
            dk_ref[...] = (ck_ref[...] + dkb[:WINDOW]).astype(BF16)
            dv_ref[...] = (cv_ref[...] + dvb[:WINDOW]).astype(BF16)
            ck_ref[...] = dkb[WINDOW:]
            cv_ref[...] = dvb[WINDOW:]

        @pl.when(n == nb)
        def _():
            dk_ref[...] = ck_ref[...].astype(BF16)
            dv_ref[...] = cv_ref[...].astype(BF16)

    row = lambda w: pl.BlockSpec((WINDOW, w), lambda n: (cur(n), 0))
    done = lambda w: pl.BlockSpec((WINDOW, w), lambda n: (jnp.maximum(n - 1, 0), 0))
    return pl.pallas_call(
        body, name="attn_bwd", grid=(nb + 1,),
        in_specs=[*proj_specs, *trig_cur, *trig_prev, pl.BlockSpec((8, LANES), lambda n: (0, 0)),
                  row(A), row(LANES), row(A)],
        out_specs=[row(A), done(KV), done(KV), pl.BlockSpec((8, LANES), lambda n: (0, 0))],
        out_shape=[jax.ShapeDtypeStruct((S, A), BF16), jax.ShapeDtypeStruct((S, KV), BF16),
                   jax.ShapeDtypeStruct((S, KV), BF16), jax.ShapeDtypeStruct((8, LANES), F32)],
        scratch_shapes=[pltpu.VMEM((WINDOW, KV), F32), pltpu.VMEM((WINDOW, KV), F32),
                        pltpu.VMEM((2 * WINDOW, KV), F32), pltpu.VMEM((2 * WINDOW, KV), F32),
                        pltpu.VMEM((WINDOW, A), F32)],
        compiler_params=_params(("arbitrary",)),
    )(proj, proj, proj, proj, proj, *trig, *trig, sink_row, y, lse, dy)


def _sgu_layout(dm, S):
    G = dm["G"]
    pw = math.gcd(dm["OFF_Z"], G)
    npc = G // pw
    tm = _pick(S, (256, 128))
    u_specs = [pl.BlockSpec((tm, pw), lambda i, p=p: (i, dm["OFF_Z"] // pw + p)) for p in range(npc)]
    v_specs = [pl.BlockSpec((tm, pw), lambda i, p=p: (i, (dm["OFF_Z"] + G) // pw + p)) for p in range(npc)]
    return pw, npc, tm, u_specs, v_specs


def _sgu_norm(v_refs, lg_ref, lb_ref):
    v = jnp.concatenate([_gelu(r[...].astype(F32)) for r in v_refs], axis=1)
    mu = jnp.mean(v, axis=-1, keepdims=True)
    vc = v - mu
    rstd = lax.rsqrt(jnp.mean(vc * vc, axis=-1, keepdims=True) + EPS)
    xhat = vc * rstd
    return xhat, rstd, (xhat * lg_ref[...] + lb_ref[...]).astype(BF16)


def _sgu_fwd(proj, w_tril, b_t, ln_g_row, ln_b_row, dm):
    S = proj.shape[0]
    G, NG = dm["G"], dm["NG"]
    pw, npc, tm, u_specs, v_specs = _sgu_layout(dm, S)
    nch = tm // WINDOW

    def body(*refs):
        u_refs, v_refs = refs[:npc], refs[npc:2 * npc]
        w_ref, bt_ref, lg_ref, lb_ref, y_ref = refs[2 * npc:]
        _, _, vn = _sgu_norm(v_refs, lg_ref, lb_ref)
        u = jnp.concatenate([_gelu(r[...].astype(F32)) for r in u_refs], axis=1)
        for c in range(nch):
            rows = slice(c * WINDOW, (c + 1) * WINDOW)
            for g in range(NG):
                cols = slice(g * LANES, (g + 1) * LANES)
                sv = _dot(w_ref[g], vn[rows, cols], NN) + bt_ref[:, g:g + 1]
                y_ref[rows, cols] = (u[rows, cols] * sv).astype(BF16)

    return pl.pallas_call(
        body, name="sgu_fwd", grid=(S // tm,),
        in_specs=[*u_specs, *v_specs,
                  pl.BlockSpec((NG, WINDOW, WINDOW), lambda i: (0, 0, 0)),
                  pl.BlockSpec((WINDOW, LANES), lambda i: (0, 0)),
                  pl.BlockSpec((1, G), lambda i: (0, 0)), pl.BlockSpec((1, G), lambda i: (0, 0))],
        out_specs=pl.BlockSpec((tm, G), lambda i: (i, 0)),
        out_shape=jax.ShapeDtypeStruct((S, G), BF16),
        compiler_params=_params(("parallel",)),
    )(*([proj] * (2 * npc)), w_tril, b_t, ln_g_row, ln_b_row)


def _sgu_bwd(proj, w_tril, b_t, ln_g_row, ln_b_row, dy, dm):
    S = proj.shape[0]
    G, NG = dm["G"], dm["NG"]
    pw, npc, tm, u_specs, v_specs = _sgu_layout(dm, S)
    nch = tm // WINDOW

    def body(*refs):
        u_refs, v_refs = refs[:npc], refs[npc:2 * npc]
        w_ref, bt_ref, lg_ref, lb_ref, dy_ref, dz_ref, dw_ref, dbt_ref, dlg_ref, dlb_ref, dvn_ref = refs[2 * npc:]
        i = pl.program_id(0)

        @pl.when(i == 0)
        def _():
            dw_ref[...] = jnp.zeros_like(dw_ref)
            dbt_ref[...] = jnp.zeros_like(dbt_ref)
            dlg_ref[...] = jnp.zeros_like(dlg_ref)
            dlb_ref[...] = jnp.zeros_like(dlb_ref)

        xhat, rstd, vn = _sgu_norm(v_refs, lg_ref, lb_ref)
        u_pre = jnp.concatenate([r[...].astype(F32) for r in u_refs], axis=1)
        u = _gelu(u_pre)
        dy = dy_ref[...].astype(F32)
        lane = lax.broadcasted_iota(jnp.int32, (WINDOW, LANES), 1)
        tri = lax.broadcasted_iota(jnp.int32, (WINDOW, WINDOW), 0) >= lax.broadcasted_iota(jnp.int32, (WINDOW, WINDOW), 1)
        dbt = jnp.zeros((WINDOW, LANES), F32)
        for c in range(nch):
            rows = slice(c * WINDOW, (c + 1) * WINDOW)
            for g in range(NG):
                cols = slice(g * LANES, (g + 1) * LANES)
                vn_cg = vn[rows, cols]
                sv = _dot(w_ref[g], vn_cg, NN) + bt_ref[:, g:g + 1]
                dy_cg = dy[rows, cols]
                dsv = dy_cg * u[rows, cols]
                dsv_b = dsv.astype(BF16)
                dz_ref[rows, cols] = (dy_cg * sv * _gelu_grad(u_pre[rows, cols])).astype(BF16)
                dvn_ref[rows, cols] = _dot(w_ref[g], dsv_b, TN)
                dw_ref[g] += jnp.where(tri, _dot(dsv_b, vn_cg, NT), 0.0)
                dbt = dbt + jnp.where(lane == g, jnp.sum(dsv, axis=-1, keepdims=True), 0.0)
        dbt_ref[...] += dbt
        dvn = dvn_ref[...]
        dlg_ref[0:1, :] += jnp.sum(dvn * xhat, axis=0, keepdims=True)
        dlb_ref[0:1, :] += jnp.sum(dvn, axis=0, keepdims=True)
        dxh = dvn * lg_ref[...]
        dv = rstd * (dxh - jnp.mean(dxh, axis=-1, keepdims=True) - xhat * jnp.mean(dxh * xhat, axis=-1, keepdims=True))
        v_pre = jnp.concatenate([r[...].astype(F32) for r in v_refs], axis=1)
        dz_ref[:, G:] = (dv * _gelu_grad(v_pre)).astype(BF16)

    return pl.pallas_call(
        body, name="sgu_bwd", grid=(S // tm,),
        in_specs=[*u_specs, *v_specs,
                  pl.BlockSpec((NG, WINDOW, WINDOW), lambda i: (0, 0, 0)),
                  pl.BlockSpec((WINDOW, LANES), lambda i: (0, 0)),
                  pl.BlockSpec((1, G), lambda i: (0, 0)), pl.BlockSpec((1, G), lambda i: (0, 0)),
                  pl.BlockSpec((tm, G), lambda i: (i, 0))],
        out_specs=[pl.BlockSpec((tm, 2 * G), lambda i: (i, 0)),
                   pl.BlockSpec((NG, WINDOW, WINDOW), lambda i: (0, 0, 0)),
                   pl.BlockSpec((WINDOW, LANES), lambda i: (0, 0)),
                   pl.BlockSpec((8, G), lambda i: (0, 0)), pl.BlockSpec((8, G), lambda i: (0, 0))],
        out_shape=[jax.ShapeDtypeStruct((S, 2 * G), BF16), jax.ShapeDtypeStruct((NG, WINDOW, WINDOW), F32),
                   jax.ShapeDtypeStruct((WINDOW, LANES), F32), jax.ShapeDtypeStruct((8, G), F32),
                   jax.ShapeDtypeStruct((8, G), F32)],
        scratch_shapes=[pltpu.VMEM((tm, G), F32)],
        compiler_params=_params(("arbitrary",)),
    )(*([proj] * (2 * npc)), w_tril, b_t, ln_g_row, ln_b_row, dy)


def _result(outs, n_main, carry):
    main = outs[0] if n_main == 1 else tuple(outs[:n_main])
    return (main, list(outs[n_main:])) if carry else main


def _in_proj(xn, w_in_g, b_row, trig, dm, carry=None):
    S, D = xn.shape
    IN = dm["IN"]
    cw = IN // N_CHIPS
    tm = _pick(S, (1024, 512, 256, 128))
    tn = _pick(cw, (1920, 640, 512, 256, 128))
    nbc = cw // tn
    rope_cols = dm["OFF_V"]
    rope_blocks = -(-rope_cols // tn)

    def roped_store(val, e_refs, o_refs, cols, jj):
        r = min(max(rope_cols - (jj * tn + cols.start), 0), cols.stop - cols.start)
        if not r:
            o_refs[0][:, cols] = val.astype(BF16)
            return
        for rows in [slice(s, s + WINDOW) for s in range(0, tm, WINDOW)]:
            cos, sin = e_refs[1][rows, :], e_refs[2][rows, :]
            for c0 in range(0, r, LANES):
                piece = val[rows, c0:c0 + LANES]
                o_refs[0][rows, cols.start + c0:cols.start + c0 + LANES] = _rope(piece, cos, sin).astype(BF16)
        if r < val.shape[1]:
            o_refs[0][:, cols.start + r:cols.stop] = val[:, r:].astype(BF16)

    def ep_rope(parts, e_refs, o_refs, cols):
        val = parts[0] + e_refs[0][:, cols]
        if rope_blocks == 1:
            roped_store(val, e_refs, o_refs, cols, 0)
        else:
            for jj in range(rope_blocks):
                @pl.when(pl.program_id(0) == jj)
                def _(jj=jj):
                    roped_store(val, e_refs, o_refs, cols, jj)

    def ep_plain(parts, e_refs, o_refs, cols):
        o_refs[0][:, cols] = (parts[0] + e_refs[0][:, cols]).astype(BF16)

    rows = pl.BlockSpec((tm, LANES), lambda i, j, k: (i, 0))
    first = IN // tn - rope_blocks
    common = dict(dims=NN, lhs_spec=pl.BlockSpec((tm, D), lambda i, j, k: (i, 0)), acc_shape=(tm, tn),
                  out_shape=[jax.ShapeDtypeStruct((S, IN), BF16)], cols_outer=True)
    proj = _matmul(
        "in_proj_qk", xn, [w_in_g], grid=(S // tm, rope_blocks, 1),
        rhs_specs=[pl.BlockSpec((None, D, tn), lambda i, j, k: (j // nbc, 0, j % nbc))],
        extra=[b_row, *trig], extra_specs=[pl.BlockSpec((1, tn), lambda i, j, k: (0, j)), rows, rows],
        out_specs=[pl.BlockSpec((tm, tn), lambda i, j, k: (i, j))], epilogue=ep_rope, **common)[0]
    jb = lambda j: j + rope_blocks
    return _result(_matmul(
        "in_proj", xn, [w_in_g], grid=(S // tm, first, 1),
        rhs_specs=[pl.BlockSpec((None, D, tn), lambda i, j, k: (jb(j) // nbc, 0, jb(j) % nbc))],
        extra=[b_row, proj], extra_specs=[pl.BlockSpec((1, tn), lambda i, j, k: (0, jb(j))), ANY], extra_aliases={1: 0},
        out_specs=[pl.BlockSpec((tm, tn), lambda i, j, k: (i, jb(j)))], epilogue=ep_plain, carry=carry, **common), 1, carry)


def _branch_attn(y_attn, w_ab_g, dm):
    S, A = y_attn.shape
    D = dm["D"]
    cw = D // N_CHIPS
    tm = _pick(S, (1024, 512, 256, 128))
    return _matmul(
        "branch_attn", y_attn, [w_ab_g], dims=NN, grid=(S // tm, N_CHIPS, 1),
        lhs_spec=pl.BlockSpec((tm, A), lambda i, j, k: (i, 0)),
        rhs_specs=[pl.BlockSpec((None, A, cw), lambda i, j, k: (j, 0, 0))],
        acc_shape=(tm, cw), out_shape=[jax.ShapeDtypeStruct((S, D), BF16)],
        out_specs=[pl.BlockSpec((tm, cw), lambda i, j, k: (i, j))], epilogue=_store_epilogue(BF16))[0]


def _branch_sgu_merge(y_sgu, w_sb_g, a_attn, proj, dm):
    S, G = y_sgu.shape
    D, OFF_G = dm["D"], dm["OFF_G"]
    cw = D // N_CHIPS
    tm = _pick(S, (1024, 512, 256, 128))

    def ep(parts, e_refs, o_refs, cols):
        a_sgu = parts[0].astype(BF16)
        ga = _sigmoid(e_refs[1][:, cols].astype(F32))
        gs = _sigmoid(e_refs[2][:, cols].astype(F32))
        o_refs[0][:, cols] = a_sgu
        o_refs[1][:, cols] = (ga * e_refs[0][:, cols].astype(F32) + gs * a_sgu.astype(F32)).astype(BF16)

    blk = pl.BlockSpec((tm, cw), lambda i, j, k: (i, j))
    return _matmul(
        "branch_sgu_merge", y_sgu, [w_sb_g], dims=NN, grid=(S // tm, N_CHIPS, 1),
        lhs_spec=pl.BlockSpec((tm, G), lambda i, j, k: (i, 0)),
        rhs_specs=[pl.BlockSpec((None, G, cw), lambda i, j, k: (j, 0, 0))],
        acc_shape=(tm, cw), extra=[a_attn, proj, proj],
        extra_specs=[blk, pl.BlockSpec((tm, cw), lambda i, j, k: (i, OFF_G // cw + j)),
                     pl.BlockSpec((tm, cw), lambda i, j, k: (i, (OFF_G + D) // cw + j))],
        out_shape=[jax.ShapeDtypeStruct((S, D), BF16), jax.ShapeDtypeStruct((S, D), BF16)],
        out_specs=[blk, blk], epilogue=ep)


def _residual_matmul(name, a, w_g, h, carry=None):
    S, K = a.shape
    D = w_g.shape[1]
    tm = _pick(S, (1024, 512, 256, 128))
    tn = _pick(D, (512, 256, 128))

    def ep(parts, e_refs, o_refs, cols):
        o_refs[0][:, cols] = e_refs[0][:, cols] + parts[0]

    blk = pl.BlockSpec((tm, tn), lambda i, j, k: (i, j))
    return _result(_matmul(
        name, a, [w_g], dims=NN, grid=(S // tm, D // tn, 1),
        lhs_spec=pl.BlockSpec((tm, K), lambda i, j, k: (i, 0)),
        rhs_specs=[pl.BlockSpec((K, tn), lambda i, j, k: (0, j))],
        acc_shape=(tm, tn), extra=[h], extra_specs=[blk],
        out_shape=[jax.ShapeDtypeStruct((S, D), F32)], out_specs=[blk], epilogue=ep, carry=carry), 1, carry)


def _gate_up(hn, w_gu_g, dm, carry=None):
    S, D = hn.shape
    Fd = dm["F"]
    cw = 2 * Fd // N_CHIPS
    tm = _pick(S, (512, 256, 128))
    tn = _pick(cw, (1408, 512, 384, 256, 128))
    nbc = cw // tn
    half = N_CHIPS // 2

    def ep(parts, e_refs, o_refs, cols):
        gate, up = parts[0].astype(BF16), parts[1].astype(BF16)
        o_refs[0][0, :, cols] = gate
        o_refs[0][1, :, cols] = up
        g32 = gate.astype(F32)
        o_refs[1][:, cols] = (g32 * _sigmoid(g32) * up.astype(F32)).astype(BF16)

    return _result(_matmul(
        "gate_up", hn, [w_gu_g, w_gu_g], dims=NN, grid=(S // tm, Fd // tn, 1),
        lhs_spec=pl.BlockSpec((tm, D), lambda i, j, k: (i, 0)),
        rhs_specs=[pl.BlockSpec((None, D, tn), lambda i, j, k: (j // nbc, 0, j % nbc)),
                   pl.BlockSpec((None, D, tn), lambda i, j, k: (half + j // nbc, 0, j % nbc))],
        acc_shape=(tm, tn),
        out_shape=[jax.ShapeDtypeStruct((2, S, Fd), BF16), jax.ShapeDtypeStruct((S, Fd), BF16)],
        out_specs=[pl.BlockSpec((2, tm, tn), lambda i, j, k: (0, i, j)), pl.BlockSpec((tm, tn), lambda i, j, k: (i, j))],
        epilogue=ep, carry=carry, cols_outer=True), 2, carry)


def _down_bwd(dh_b, w_down_g, gu, dm, carry=None):
    S, D = dh_b.shape
    Fd = dm["F"]
    tm = _pick(S, (2048, 1024, 512, 256, 128))
    tn = _pick(Fd, (512, 256, 128))

    def ep(parts, e_refs, o_refs, cols):
        gate = e_refs[0][0, :, cols].astype(F32)
        up = e_refs[0][1, :, cols].astype(F32)
        s = _sigmoid(gate)
        ds = parts[0] * s
        o_refs[0][0, :, cols] = (ds * up * (1.0 + gate * (1.0 - s))).astype(BF16)
        o_refs[0][1, :, cols] = (ds * gate).astype(BF16)

    blk = pl.BlockSpec((2, tm, tn), lambda i, j, k: (0, i, j))
    return _result(_matmul(
        "down_bwd", dh_b, [w_down_g], dims=NT, grid=(S // tm, Fd // tn, 1),
        lhs_spec=pl.BlockSpec((tm, D), lambda i, j, k: (i, 0)),
        rhs_specs=[pl.BlockSpec((tn, D), lambda i, j, k: (j, 0))],
        acc_shape=(tm, tn), extra=[gu], extra_specs=[blk],
        out_shape=[jax.ShapeDtypeStruct((2, S, Fd), BF16)], out_specs=[blk], epilogue=ep, carry=carry), 1, carry)


def _gate_up_bwd(dgu, w_gu_g, dm, carry=None):
    S = dgu.shape[1]
    D, Fd = dm["D"], dm["F"]
    cw = 2 * Fd // N_CHIPS
    half = N_CHIPS // 2
    tm = _pick(S, (1024, 512, 256, 128))
    tn = _pick(D, (1024, 512, 256, 128))
    return _result(_matmul(
        "gate_up_bwd", dgu, [w_gu_g], dims=NT, grid=(S // tm, D // tn, N_CHIPS),
        lhs_spec=pl.BlockSpec((None, tm, cw), lambda i, j, k: (k // half, i, k % half)),
        rhs_specs=[pl.BlockSpec((None, tn, cw), lambda i, j, k: (k, j, 0))],
        acc_shape=(tm, tn), out_shape=[jax.ShapeDtypeStruct((S, D), F32)],
        out_specs=[pl.BlockSpec((tm, tn), lambda i, j, k: (i, j))], epilogue=_store_epilogue(F32), carry=carry), 1, carry)


def _out_bwd(dh_b, w_out_g, proj, a_attn, a_sgu, dm, carry=None):
    S, D = dh_b.shape
    OFF_G = dm["OFF_G"]
    tm = _pick(S, (1024, 512, 256, 128))
    tn = D // N_CHIPS

    def ep(parts, e_refs, o_refs, cols):
        dm_ = parts[0]
        ga = _sigmoid(e_refs[0][:, cols].astype(F32))
        gs = _sigmoid(e_refs[1][:, cols].astype(F32))
        da, ds = dm_ * ga, dm_ * gs
        o_refs[0][:, cols] = da.astype(BF16)
        o_refs[1][:, cols] = ds.astype(BF16)
        o_refs[2][0, :, cols] = (da * e_refs[2][:, cols].astype(F32) * (1.0 - ga)).astype(BF16)
        o_refs[2][1, :, cols] = (ds * e_refs[3][:, cols].astype(F32) * (1.0 - gs)).astype(BF16)

    blk = pl.BlockSpec((tm, tn), lambda i, j, k: (i, j))
    return _result(_matmul(
        "out_bwd", dh_b, [w_out_g], dims=NT, grid=(S // tm, D // tn, 1),
        lhs_spec=pl.BlockSpec((tm, D), lambda i, j, k: (i, 0)),
        rhs_specs=[pl.BlockSpec((tn, D), lambda i, j, k: (j, 0))],
        acc_shape=(tm, tn), extra=[proj, proj, a_attn, a_sgu],
        extra_specs=[pl.BlockSpec((tm, tn), lambda i, j, k: (i, OFF_G // tn + j)),
                     pl.BlockSpec((tm, tn), lambda i, j, k: (i, (OFF_G + D) // tn + j)), blk, blk],
        out_shape=[jax.ShapeDtypeStruct((S, D), BF16), jax.ShapeDtypeStruct((S, D), BF16),
                   jax.ShapeDtypeStruct((2, S, D), BF16)],
        out_specs=[blk, blk, pl.BlockSpec((2, tm, tn), lambda i, j, k: (0, i, j))], epilogue=ep, carry=carry), 3, carry)


def _colsharded_bwd(name, dy, w_g, out_dtype, carry=None):
    S = dy.shape[0]
    _, K, cw = w_g.shape
    tm = _pick(S, (1024, 512, 256, 128))
    tn = _pick(K, (1024, 512, 256, 128))
    return _result(_matmul(
        name, dy, [w_g], dims=NT, grid=(S // tm, K // tn, N_CHIPS),
        lhs_spec=pl.BlockSpec((tm, cw), lambda i, j, k: (i, k)),
        rhs_specs=[pl.BlockSpec((None, tn, cw), lambda i, j, k: (k, j, 0))],
        acc_shape=(tm, tn), out_shape=[jax.ShapeDtypeStruct((S, K), out_dtype)],
        out_specs=[pl.BlockSpec((tm, tn), lambda i, j, k: (i, j))], epilogue=_store_epilogue(out_dtype),
        carry=carry), 1, carry)


def _wgrad_cols(name, x, dy, carry=None, colsum=False):
    S, R = x.shape
    C = dy.shape[1]
    cw = C // N_CHIPS
    tm = _pick(R, (1024, 512, 256, 128))
    tk = _pick(S, (2048, 1024, 512, 256, 128))

    def ep(parts, e_refs, o_refs, cols):
        for o, p in zip(o_refs, parts):
            o[:, cols] = p

    out_shape = [jax.ShapeDtypeStruct((N_CHIPS, R, cw), F32)]
    out_specs = [pl.BlockSpec((None, tm, cw), lambda i, j, k: (j, i, 0))]
    if colsum:
        out_shape.append(jax.ShapeDtypeStruct((R // tm, N_CHIPS, 8, cw), F32))
        out_specs.append(pl.BlockSpec((None, None, 8, cw), lambda i, j, k: (i, j, 0, 0)))
    return _result(_matmul(
        name, x, [dy], dims=TN, grid=(R // tm, N_CHIPS, S // tk),
        lhs_spec=pl.BlockSpec((tk, tm), lambda i, j, k: (k, i)),
        rhs_specs=[pl.BlockSpec((tk, cw), lambda i, j, k: (k, j))],
        acc_shape=(tm, cw), out_shape=out_shape, out_specs=out_specs, epilogue=ep,
        carry=carry, rhs_colsum=colsum), len(out_shape), carry)


def _wgrad_gate_up(hn, dgu, dm, carry=None):
    S, D = hn.shape
    Fd = dm["F"]
    cw = 2 * Fd // N_CHIPS
    half = N_CHIPS // 2
    tm = _pick(D, (1024, 512, 256, 128))
    tk = _pick(S, (2048, 1024, 512, 256, 128))
    tn = _pick(cw, (1408, 512, 384, 256, 128))
    nbc = cw // tn
    return _result(_matmul(
        "wgrad_gate_up", hn, [dgu], dims=TN, grid=(D // tm, 2 * Fd // tn, S // tk),
        lhs_spec=pl.BlockSpec((tk, tm), lambda i, j, k: (k, i)),
        rhs_specs=[pl.BlockSpec((None, tk, tn), lambda i, j, k: (j // (half * nbc), k, j % (half * nbc)))],
        acc_shape=(tm, tn), out_shape=[jax.ShapeDtypeStruct((N_CHIPS, D, cw), F32)],
        out_specs=[pl.BlockSpec((None, tm, tn), lambda i, j, k: (j // nbc, i, j % nbc))], epilogue=_store_epilogue(F32),
        carry=carry), 1, carry)


def _wgrad_rows(name, x, dy):
    S, R = x.shape
    C = dy.shape[1]
    rw = R // N_CHIPS
    tn = _pick(C, (1024, 512, 256, 128))
    tk = _pick(S, (2048, 1024, 512, 256, 128))
    return _matmul(
        name, x, [dy], dims=TN, grid=(N_CHIPS, C // tn, S // tk),
        lhs_spec=pl.BlockSpec((tk, rw), lambda i, j, k: (k, i)),
        rhs_specs=[pl.BlockSpec((tk, tn), lambda i, j, k: (k, j))],
        acc_shape=(rw, tn), out_shape=[jax.ShapeDtypeStruct((N_CHIPS, rw, C), F32)],
        out_specs=[pl.BlockSpec((None, rw, tn), lambda i, j, k: (i, 0, j))], epilogue=_store_epilogue(F32))[0]


def _place():
    x, y, c = lax.axis_index("x"), lax.axis_index("y"), lax.axis_index("c")
    others = [(1 - x, y), (x, 1 - y), (1 - x, 1 - y)]
    return x, y, c, others


def _chip_index(chip):
    return 2 * chip[0] + chip[1]


def _gather_weights(bufs):
    n = len(bufs)

    def copies(src, out, send_sems, recv_sems):
        x, y, c, others = _place()

        def half(ref, chip_idx, hc):
            r2 = ref.shape[1] // 2
            return ref.at[chip_idx, pl.ds(hc * r2, r2), :]

        def copy(t, k, chip, hc, to):
            return pltpu.make_async_remote_copy(
                src_ref=half(src[t], _chip_index(chip), hc), dst_ref=half(out[t], _chip_index(chip), hc),
                send_sem=send_sems.at[6 * t + k], recv_sem=recv_sems.at[6 * t + k],
                device_id=to, device_id_type=MESH)

        me, sibling = (x, y, c), (x, y, 1 - c)
        pairs = [(t, j, chip) for t in range(n) for j, chip in enumerate(others)]
        sent = [copy(t, j, (x, y), c, (*chip, c)) for t, j, chip in pairs]
        landed = [copy(t, j, chip, c, me) for t, j, chip in pairs]
        passed = [copy(t, 3 + j, chip, c, sibling) for t, j, chip in pairs]
        handed = [copy(t, 3 + j, chip, 1 - c, me) for t, j, chip in pairs]
        return sent, landed, passed, handed

    def start(src, out, send_sems, recv_sems):
        for cp in copies(src, out, send_sems, recv_sems)[0]:
            cp.start()

    def finish(src, out, send_sems, recv_sems):
        sent, landed, passed, handed = copies(src, out, send_sems, recv_sems)
        for arrival, forward in zip(landed, passed):
            arrival.wait_recv()
            forward.start()
        for cp in handed:
            cp.wait_recv()
        for cp in sent + passed:
            cp.wait_send()

    return _Comm("gather_weights", bufs, [jax.ShapeDtypeStruct(b.shape, BF16) for b in bufs],
                 {t: t for t in range(n)}, 6 * n, start, finish)


def _sibling_exchange(grads):
    n = len(grads)
    shapes = [g.shape for g in grads]

    def copies(src, land, send_sems, recv_sems):
        x, y, c, _ = _place()
        res = []
        for t in range(n):
            r2 = shapes[t][1] // 2
            res.append(pltpu.make_async_remote_copy(
                src_ref=src[t].at[:, pl.ds((1 - c) * r2, r2), :], dst_ref=land[t],
                send_sem=send_sems.at[t], recv_sem=recv_sems.at[t], device_id=(x, y, 1 - c), device_id_type=MESH))
        return res

    def start(*refs):
        for cp in copies(*refs):
            cp.start()

    def finish(*refs):
        remote = copies(*refs)
        for cp in remote:
            cp.wait_recv()
        for cp in remote:
            cp.wait_send()

    return _Comm("sibling_exchange", grads, [jax.ShapeDtypeStruct((s[0], s[1] // 2, s[2]), F32) for s in shapes],
                 {}, n, start, finish)


def _chip_exchange(sends):
    n = len(sends)
    shapes = [s.shape for s in sends]

    def copies(snd, got, send_sems, recv_sems):
        x, y, c, others = _place()
        return [pltpu.make_async_remote_copy(
            src_ref=snd[t].at[_chip_index(chip)], dst_ref=got[t].at[j],
            send_sem=send_sems.at[3 * t + j], recv_sem=recv_sems.at[3 * t + j],
            device_id=(*chip, c), device_id_type=MESH) for t in range(n) for j, chip in enumerate(others)]

    def start(*refs):
        for cp in copies(*refs):
            cp.start()

    def finish(*refs):
        remote = copies(*refs)
        for cp in remote:
            cp.wait_recv()
        for cp in remote:
            cp.wait_send()

    return _Comm("chip_exchange", sends, [jax.ShapeDtypeStruct((3, s[1], s[2]), BF16) for s in shapes],
                 {}, 3 * n, start, finish)


def _sibling_share(fulls):
    n = len(fulls)
    shapes = [f.shape for f in fulls]

    def copies(src, out, send_sems, recv_sems, mine):
        x, y, c, _ = _place()
        hc = c if mine else 1 - c
        res = []
        for t in range(n):
            r2 = shapes[t][0] // 2
            res.append(pltpu.make_async_remote_copy(
                src_ref=src[t].at[pl.ds(hc * r2, r2), :], dst_ref=out[t].at[pl.ds(hc * r2, r2), :],
                send_sem=send_sems.at[t], recv_sem=recv_sems.at[t], device_id=(x, y, 1 - c), device_id_type=MESH))
        return res

    def start(*refs):
        for cp in copies(*refs, mine=True):
            cp.start()

    def finish(*refs):
        for cp in copies(*refs, mine=False):
            cp.wait_recv()
        for cp in copies(*refs, mine=True):
            cp.wait_send()

    return _Comm("sibling_share", fulls, [jax.ShapeDtypeStruct(s, F32) for s in shapes],
                 {t: t for t in range(n)}, n, start, finish)


def _gather_all(v):
    R, C = v.shape

    def body(v_ref, out_ref, send_sems, recv_sems, local_sem):
        x, y, c, others = _place()
        me, sibling = (x, y, c), (x, y, 1 - c)

        def rows(px, py, pc):
            return out_ref.at[4 * px + 2 * py + pc]

        def copy(k, block, to, src=None):
            return pltpu.make_async_remote_copy(
                src_ref=rows(*block) if src is None else src, dst_ref=rows(*block),
                send_sem=send_sems.at[k], recv_sem=recv_sems.at[k], device_id=to, device_id_type=MESH)

        mine = pltpu.make_async_copy(v_ref, rows(*me), local_sem)
        mine.start()
        first = [copy(0, me, sibling, src=v_ref)]
        first += [copy(1 + j, me, (*chip, c), src=v_ref) for j, chip in enumerate(others)]
        for cp in first:
            cp.start()
        passed = [copy(4 + j, (*chip, c), sibling) for j, chip in enumerate(others)]
        for j, chip in enumerate(others):
            copy(1 + j, (*chip, c), me).wait_recv()
            passed[j].start()
        copy(0, sibling, me).wait_recv()
        for j, chip in enumerate(others):
            copy(4 + j, (*chip, 1 - c), me).wait_recv()
        for cp in first + passed:
            cp.wait_send()
        mine.wait()

    return pl.pallas_call(
        body, name="gather_all", in_specs=[ANY], out_specs=ANY,
        out_shape=jax.ShapeDtypeStruct((8, R, C), F32),
        scratch_shapes=[pltpu.SemaphoreType.DMA((7,)), pltpu.SemaphoreType.DMA((7,)), pltpu.SemaphoreType.DMA],
    )(v)


def _my_chip():
    return 2 * lax.axis_index("x") + lax.axis_index("y")


def _my_core():
    return lax.axis_index("c")


def _pair_sum(grad, land):
    K, R2, C = land.shape
    tm = _row_tile(R2, C)
    nrb = R2 // tm

    def body(a_ref, b_ref, sb_ref):
        sb_ref[...] = (a_ref[...] + b_ref[...]).astype(BF16)

    blk = pl.BlockSpec((None, tm, C), lambda k, r: (k, r, 0))
    return pl.pallas_call(
        body, name="pair_sum", grid=(K, nrb),
        in_specs=[pl.BlockSpec((None, tm, C), lambda k, r: (k, _my_core() * nrb + r, 0)), blk],
        out_specs=blk, out_shape=jax.ShapeDtypeStruct((K, R2, C), BF16),
        compiler_params=_params(("parallel", "parallel")),
    )(grad, land)


def _chip_sum(grad, land, got):
    _, R2, C = land.shape
    tm = _row_tile(R2, C)
    nrb = R2 // tm

    def body(a_ref, b_ref, g_ref, s_ref):
        own = a_ref[...] + b_ref[...]
        s_ref[...] = ((own + g_ref[0].astype(F32)) + g_ref[1].astype(F32)) + g_ref[2].astype(F32)

    return pl.pallas_call(
        body, name="chip_sum", grid=(nrb,),
        in_specs=[pl.BlockSpec((None, tm, C), lambda r: (_my_chip(), _my_core() * nrb + r, 0)),
                  pl.BlockSpec((None, tm, C), lambda r: (_my_chip(), r, 0)),
                  pl.BlockSpec((3, tm, C), lambda r: (0, r, 0))],
        out_specs=pl.BlockSpec((tm, C), lambda r: (_my_core() * nrb + r, 0)),
        out_shape=jax.ShapeDtypeStruct((2 * R2, C), F32),
        compiler_params=_params(("parallel",)),
    )(grad, land, got)


def _adamw_math(w, g, m, v):
    m = ADAM_B1 * m + (1.0 - ADAM_B1) * g
    v = ADAM_B2 * v + (1.0 - ADAM_B2) * (g * g)
    m_hat = m / (1.0 - ADAM_B1 ** ADAM_STEP)
    v_hat = v / (1.0 - ADAM_B2 ** ADAM_STEP)
    delta = -ADAM_LR * (m_hat / (jnp.sqrt(v_hat) + ADAM_EPS) + ADAM_WD * w)
    return delta, m, v


def _adamw_stacked(grads, w, m, v):
    L, R, C = w.shape
    tm = _row_tile(R, C)
    nrb = R // tm

    def body(*refs):
        g_refs = refs[:L]
        w_ref, m_ref, v_ref, go_ref, d_ref, mo_ref, vo_ref = refs[L:]
        l = pl.program_id(0)
        for ll in range(L):
            @pl.when(l == ll)
            def _(ll=ll):
                g = g_refs[ll][...]
                delta, mn, vn = _adamw_math(w_ref[...], g, m_ref[...], v_ref[...])
                go_ref[...] = g
                d_ref[...] = delta
                mo_ref[...] = mn
                vo_ref[...] = vn

    stacked = pl.BlockSpec((None, tm, C), lambda l, r: (l, r, 0))
    g_specs = [pl.BlockSpec((tm, C), lambda l, r, ll=ll: (jnp.where(l == ll, r, 0), 0)) for ll in range(L)]
    shp = jax.ShapeDtypeStruct((L, R, C), F32)
    return pl.pallas_call(
        body, name="adamw", grid=(L, nrb),
        in_specs=[*g_specs, stacked, stacked, stacked], out_specs=[stacked] * 4, out_shape=[shp] * 4,
        compiler_params=_params(("arbitrary", "arbitrary")),
    )(*grads, w, m, v)


def _adamw_small(parts, w, m, v):
    _, R, C = parts.shape
    tm = _row_tile(R, 8 * C)

    def body(p_ref, w_ref, m_ref, v_ref, go_ref, d_ref, mo_ref, vo_ref):
        g = p_ref[0]
        for k in range(1, 8):
            g = g + p_ref[k]
        delta, mn, vn = _adamw_math(w_ref[...], g, m_ref[...], v_ref[...])
        go_ref[...] = g
        d_ref[...] = delta
        mo_ref[...] = mn
        vo_ref[...] = vn

    blk = pl.BlockSpec((tm, C), lambda i: (i, 0))
    shp = jax.ShapeDtypeStruct((R, C), F32)
    return pl.pallas_call(
        body, name="adamw_small", grid=(R // tm,),
        in_specs=[pl.BlockSpec((8, tm, C), lambda i: (0, i, 0)), blk, blk, blk],
        out_specs=[blk] * 4, out_shape=[shp] * 4,
        compiler_params=_params(("parallel",)),
    )(parts, w, m, v)


def _cast_place(w, layer):
    _, R, C = w.shape
    tm = _row_tile(R, C)

    def body(w_ref, o_ref):
        o_ref[...] = w_ref[...].astype(BF16)

    return pl.pallas_call(
        body, name="cast_place", grid=(R // tm,),
        in_specs=[pl.BlockSpec((None, tm, C), lambda r: (layer, r, 0))],
        out_specs=pl.BlockSpec((None, tm, C), lambda r: (_my_chip(), r, 0)),
        out_shape=jax.ShapeDtypeStruct((N_CHIPS, R, C), BF16),
        compiler_params=_params(("parallel",)),
    )(w)


def _trig_tables(positions):
    inv_freq = ROPE_THETA ** (-jnp.arange(0, ROPE_DIM, 2, dtype=F32) / ROPE_DIM)
    ang = positions.astype(F32)[:, None] * inv_freq
    cos, sin = jnp.cos(ang), jnp.sin(ang)
    S = positions.shape[0]
    cos_h = jnp.concatenate([cos, cos, jnp.ones((S, HEAD_DIM - ROPE_DIM), F32)], axis=1)
    sin_h = jnp.concatenate([-sin, sin, jnp.zeros((S, HEAD_DIM - ROPE_DIM), F32)], axis=1)
    rep = LANES // HEAD_DIM
    return [jnp.tile(t, (1, rep)) for t in (cos_h, sin_h)]


def _row(vec):
    return vec.reshape(1, -1)


def _lane_row(vec):
    return jnp.zeros((8, LANES), F32).at[0, :vec.shape[0]].set(vec)


def _pack(pieces, rows):
    flat = jnp.concatenate([p.reshape(-1).astype(F32) for p in pieces])
    return jnp.pad(flat, (0, rows * LANES - flat.shape[0])).reshape(rows, LANES)


def kernel(x, positions, norm1_g, w_in, b_in, sinks, sgu_ln_g, sgu_ln_b, sgu_w, sgu_b, w_attn_branch, w_sgu_branch, w_out, norm2_g, w_gate_up, w_down, final_g, loss_target, m_norm1_g, m_w_in, m_b_in, m_sinks, m_sgu_ln_g, m_sgu_ln_b, m_sgu_w, m_sgu_b, m_w_attn_branch, m_w_sgu_branch, m_w_out, m_norm2_g, m_w_gate_up, m_w_down, m_final_g, v_norm1_g, v_w_in, v_b_in, v_sinks, v_sgu_ln_g, v_sgu_ln_b, v_sgu_w, v_sgu_b, v_w_attn_branch, v_w_sgu_branch, v_w_out, v_norm2_g, v_w_gate_up, v_w_down, v_final_g):
    L = norm1_g.shape[0]
    S, D = x.shape[1], x.shape[2]
    NQ = sinks.shape[1]
    A = NQ * HEAD_DIM
    KV = N_KV_HEADS * HEAD_DIM
    G = sgu_ln_g.shape[1]
    NG = sgu_w.shape[1]
    IN = b_in.shape[1]
    Fd = w_down.shape[1] * N_CHIPS
    dm = dict(D=D, A=A, KV=KV, NQ=NQ, G=G, NG=NG, IN=IN, F=Fd,
              OFF_K=A, OFF_V=A + KV, OFF_Z=A + 2 * KV, OFF_G=A + 2 * KV + 2 * G)
    assert sgu_w.shape[2] == WINDOW and G == NG * LANES and IN == dm["OFF_G"] + 2 * D

    h = x[0]
    target = loss_target[0]
    trig = _trig_tables(positions[0])
    tril = jnp.tril(jnp.ones((WINDOW, WINDOW), bool))

    big = [w_in, w_attn_branch, w_sgu_branch, w_out, w_gate_up, w_down]
    big_m = [m_w_in, m_w_attn_branch, m_w_sgu_branch, m_w_out, m_w_gate_up, m_w_down]
    big_v = [v_w_in, v_w_attn_branch, v_w_sgu_branch, v_w_out, v_w_gate_up, v_w_down]

    placed = [[_cast_place(w, l) for w in big] for l in range(L)]
    IN_, AB, SB, OUT, GU, DOWN = range(len(big))
    gathered = [[None] * len(big) for _ in range(L)]
    gathered[0][IN_] = _gather_weights([placed[0][IN_]]).run()[0]

    def fetch(layer, idx):
        return _gather_weights([placed[layer][t] for t in idx]) if layer < L else None

    def fetched(layer, idx, res):
        if layer >= L:
            return res
        main, got = res
        for t, g in zip(idx, got):
            gathered[layer][t] = g
        return main

    def weights(l):
        flat = lambda w, rows: None if w is None else w.reshape(rows, D)
        w_in_g, w_ab_g, w_sb_g, w_out_g, w_gu_g, w_down_g = gathered[l]
        return (w_in_g, w_ab_g, w_sb_g, flat(w_out_g, D), w_gu_g, flat(w_down_g, Fd))

    def small(l):
        return dict(
            g1=_row(norm1_g[l]), b_in=_row(b_in[l]), sink=_lane_row(sinks[l]),
            ln_g=_row(sgu_ln_g[l]), ln_b=_row(sgu_ln_b[l]),
            w_tril=jnp.where(tril[None], sgu_w[l], 0.0).astype(BF16),
            b_t=jnp.zeros((WINDOW, LANES), F32).at[:, :NG].set(sgu_b[l].T),
            g2=_row(norm2_g[l]))

    saved = []
    for l in range(L):
        sp = small(l)
        xn = _rms_fwd(h, sp["g1"])
        now = [AB, SB, OUT, GU] if l == 0 else [DOWN]
        proj = fetched(l, now, _in_proj(xn, gathered[l][IN_], sp["b_in"], trig, dm, carry=fetch(l, now)))
        w_in_g, w_ab_g, w_sb_g, w_out_g = weights(l)[:4]
        y_attn, lse = _attn_fwd(proj, sp["sink"], dm)
        y_sgu = _sgu_fwd(proj, sp["w_tril"], sp["b_t"], sp["ln_g"], sp["ln_b"], dm)
        a_attn = _branch_attn(y_attn, w_ab_g, dm)
        a_sgu, merged = _branch_sgu_merge(y_sgu, w_sb_g, a_attn, proj, dm)
        if l == 0:
            h_mid = fetched(l, [DOWN], _residual_matmul("out_proj", merged, w_out_g, h, carry=fetch(l, [DOWN])))
        else:
            h_mid = _residual_matmul("out_proj", merged, w_out_g, h)
        w_gu_g, w_down_g = weights(l)[4:]
        hn = _rms_fwd(h_mid, sp["g2"])
        ahead = [IN_, AB, SB, OUT]
        gu, act = fetched(l + 1, ahead, _gate_up(hn, w_gu_g, dm, carry=fetch(l + 1, ahead)))
        h_out = fetched(l + 1, [GU], _residual_matmul("down_proj", act, w_down_g, h_mid, carry=fetch(l + 1, [GU])))
        saved.append(dict(h=h, xn=xn, proj=proj, y_attn=y_attn, lse=lse, y_sgu=y_sgu, a_attn=a_attn, a_sgu=a_sgu,
                          merged=merged, h_mid=h_mid, hn=hn, gu=gu, act=act))
        h = h_out

    dh, dh_b, d_final, loss_part = _loss_head(h, _row(final_g), target)

    small_grads = [None] * L
    reduced = [[None] * len(big) for _ in range(L)]
    early, mid, late = [GU, DOWN], [AB, SB, OUT], [IN_]

    def riding(has_carry, res):
        return res if has_carry else (res, None)

    def sends_of(grads, land):
        return [_pair_sum(g, d) for g, d in zip(grads, land)]

    def finished(grads, land, got):
        return [_chip_sum(g, d, p) for g, d, p in zip(grads, land, got)]

    def file_reduced(layer, idx, fulls):
        for t, f in zip(idx, fulls):
            reduced[layer][t] = f

    late_grads = None
    mid_fulls = None
    n_late, n_mid, n_early = len(late), len(mid), len(early)
    for l in reversed(range(L)):
        w_in_g, w_ab_g, w_sb_g, w_out_g, w_gu_g, w_down_g = weights(l)
        sp, sv = small(l), saved[l]
        have = late_grads is not None
        dgu, rode = riding(have, _down_bwd(
            dh_b, w_down_g, sv["gu"], dm,
            carry=_sibling_exchange(late_grads).beside(_sibling_share(mid_fulls)) if have else None))
        if have:
            land = rode[:n_late]
            file_reduced(l + 1, mid, rode[n_late:])
        g_down = _wgrad_rows("wgrad_down", sv["act"], dh_b)
        dhn, got = riding(have, _gate_up_bwd(dgu, w_gu_g, dm,
                                             carry=_chip_exchange(sends_of(late_grads, land)) if have else None))
        g_gu, shared = riding(have, _wgrad_gate_up(sv["hn"], dgu, dm,
                                                   carry=_sibling_share(finished(late_grads, land, got)) if have else None))
        if have:
            file_reduced(l + 1, late, shared)
        dh_mid, dh_mid_b, d_g2 = _rms_bwd(dhn, sv["h_mid"], sp["g2"], dh)
        early_grads = [g_gu, g_down]
        (da_attn, da_sgu, dgate), land_e = _out_bwd(dh_mid_b, w_out_g, sv["proj"], sv["a_attn"], sv["a_sgu"], dm,
                                                     carry=_sibling_exchange(early_grads))
        sends_e = sends_of(early_grads, land_e)
        g_out = _wgrad_rows("wgrad_out", sv["merged"], dh_mid_b)
        dy_attn = _colsharded_bwd("branch_attn_bwd", da_attn, w_ab_g, BF16)
        dy_sgu = _colsharded_bwd("branch_sgu_bwd", da_sgu, w_sb_g, BF16)
        g_ab = _wgrad_cols("wgrad_attn_branch", sv["y_attn"], da_attn)
        g_sb = _wgrad_cols("wgrad_sgu_branch", sv["y_sgu"], da_sgu)
        mid_grads = [g_ab, g_sb, g_out]
        dq, dk, dv, d_sink = _attn_bwd(sv["proj"], trig, sp["sink"], sv["y_attn"], sv["lse"], dy_attn, dm)
        dz, d_sgu_w, d_bt, d_lng, d_lnb = _sgu_bwd(sv["proj"], sp["w_tril"], sp["b_t"], sp["ln_g"], sp["ln_b"], dy_sgu, dm)
        dproj = jnp.concatenate([dq, dk, dv, dz, dgate[0], dgate[1]], axis=1)
        dxn, rode = _colsharded_bwd("in_proj_bwd", dproj, w_in_g, F32,
                                    carry=_chip_exchange(sends_e).beside(_sibling_exchange(mid_grads)))
        got_e, land_m = rode[:n_early], rode[n_early:]
        (g_in, d_bin), rode = _wgrad_cols(
            "wgrad_in", sv["xn"], dproj, colsum=True,
            carry=_sibling_share(finished(early_grads, land_e, got_e)).beside(_chip_exchange(sends_of(mid_grads, land_m))))
        file_reduced(l, early, rode[:n_early])
        mid_fulls = finished(mid_grads, land_m, rode[n_early:])
        dh, dh_b, d_g1 = _rms_bwd(dxn, sv["h"], sp["g1"], dh_mid)
        late_grads = [g_in]
        small_grads[l] = dict(norm1_g=d_g1[0], b_in=d_bin[0, :, 0, :].reshape(-1), sinks=d_sink[0, :NQ],
                              sgu_ln_g=d_lng[0], sgu_ln_b=d_lnb[0], sgu_w=d_sgu_w, sgu_b=d_bt[:, :NG].T, norm2_g=d_g2[0])
    grad_x = dh[None]

    land = _sibling_exchange(late_grads).run()
    got = _chip_exchange(sends_of(late_grads, land)).run()
    shared = _sibling_share(finished(late_grads, land, got) + mid_fulls).run()
    file_reduced(0, late, shared[:n_late])
    file_reduced(0, mid, shared[n_late:])
    big_out = [_adamw_stacked([reduced[l][t] for l in range(L)], big[t], big_m[t], big_v[t]) for t in range(len(big))]

    names = ["norm1_g", "b_in", "sinks", "sgu_ln_g", "sgu_ln_b", "sgu_w", "sgu_b", "norm2_g"]
    small_w = [norm1_g, b_in, sinks, sgu_ln_g, sgu_ln_b, sgu_w, sgu_b, norm2_g, final_g]
    small_m = [m_norm1_g, m_b_in, m_sinks, m_sgu_ln_g, m_sgu_ln_b, m_sgu_w, m_sgu_b, m_norm2_g, m_final_g]
    small_v = [v_norm1_g, v_b_in, v_sinks, v_sgu_ln_g, v_sgu_ln_b, v_sgu_w, v_sgu_b, v_norm2_g, v_final_g]
    small_g = [jnp.stack([small_grads[l][nm] for l in range(L)]) for nm in names] + [d_final[0]]
    sizes = [w.size for w in small_w]
    total = sum(sizes) + 1
    rows = -(-total // (512 * LANES)) * 512
    loss_piece = jnp.sum(loss_part[0]).reshape(1)
    packed_g = _pack(small_g + [loss_piece], rows)
    one = jnp.ones((1,), F32)
    parts = _gather_all(packed_g)
    outs = _adamw_small(parts, _pack(small_w + [one], rows), _pack(small_m + [one], rows), _pack(small_v + [one], rows))

    def unpack(p):
        flat = p.reshape(-1)
        res, off = [], 0
        for w, n in zip(small_w, sizes):
            res.append(flat[off:off + n].reshape(w.shape))
            off += n
        return res, flat[off]

    (sg, loss), (sd, _), (smm, _), (svv, _) = [unpack(o) for o in outs]

    order = ["norm1_g", "w_in", "b_in", "sinks", "sgu_ln_g", "sgu_ln_b", "sgu_w", "sgu_b", "w_attn_branch",
             "w_sgu_branch", "w_out", "norm2_g", "w_gate_up", "w_down", "final_g"]
    big_names = ["w_in", "w_attn_branch", "w_sgu_branch", "w_out", "w_gate_up", "w_down"]
    small_names = names + ["final_g"]

    def collect(kind):
        res = []
        for nm in order:
            if nm in big_names:
                res.append(big_out[big_names.index(nm)][kind])
            else:
                res.append((sg, sd, smm, svv)[kind][small_names.index(nm)])
        return res

    return (loss, grad_x, *collect(0), *collect(1), *collect(2), *collect(3))
```
